```python
import jax
import jax.numpy as jnp
from jax import lax
import numpy as np

D_MODEL = 1024
BATCH = 8
SEQ = 16384
DEPTH = 1

CTX_LEN = 256
GRID_W = 64
D_CONV = D_MODEL // 2
CONV_WIDTH = 31
CONV_PAD = CONV_WIDTH // 2
N_GLA_HEADS = 4
D_GLA_V = D_MODEL // 2
GLA_HEAD_V = D_GLA_V // N_GLA_HEADS
GLA_HEAD_K = GLA_HEAD_V // 2
D_GLA_K = N_GLA_HEADS * GLA_HEAD_K
GLA_RANK = 16
GLA_TAU = 16.0
CHUNK = 64
D_MIX = D_CONV + D_GLA_V
D_IN = 2 * D_CONV + 2 * D_GLA_K + 2 * D_GLA_V + 2 * GLA_RANK
SPLITS = (2 * D_CONV,
          2 * D_CONV + D_GLA_K,
          2 * D_CONV + 2 * D_GLA_K,
          2 * D_CONV + 2 * D_GLA_K + D_GLA_V,
          2 * D_CONV + 2 * D_GLA_K + 2 * D_GLA_V,
          2 * D_CONV + 2 * D_GLA_K + 2 * D_GLA_V + GLA_RANK)
D_FF = -(-8 * D_MODEL // 768) * 256
EPS = 1e-6

kernel_name = 'hybrid_conformer_gla_dit_block'


def rms_norm(x, g):
    xf = x.astype(jnp.float32)
    y = xf * lax.rsqrt(jnp.mean(xf * xf, axis=-1, keepdims=True) + EPS)
    return (y * g.astype(jnp.float32)).astype(x.dtype)


def layer_norm(x, g, b):
    xf = x.astype(jnp.float32)
    mu = jnp.mean(xf, axis=-1, keepdims=True)
    var = jnp.mean(jnp.square(xf - mu), axis=-1, keepdims=True)
    y = (xf - mu) * lax.rsqrt(var + EPS)
    return (y * g.astype(jnp.float32) + b.astype(jnp.float32)).astype(x.dtype)


def modulate(x, shift, scale):
    return x * (1 + scale) + shift


def conformer_conv(u, conv_w, conv_b, ln_g, ln_b, n_seg, seg_len):
    a, gate = jnp.split(u, 2, axis=-1)
    v = a * jax.nn.sigmoid(gate)
    bsz, L, C = v.shape
    vs = v.reshape(bsz * n_seg, seg_len, C)
    y = lax.conv_general_dilated(vs, conv_w[:, None, :].astype(vs.dtype), (1,),
                                 [(CONV_PAD, CONV_PAD)],
                                 dimension_numbers=('NWC', 'WIO', 'NWC'),
                                 feature_group_count=C)
    y = y.reshape(bsz, L, C) + conv_b
    return jax.nn.silu(layer_norm(y, ln_g, ln_b))


def gla_chunked(q, k, v, log_a, s0, with_output):
    f32 = jnp.float32
    bsz, nh, L, dk = k.shape
    dv = v.shape[-1]
    n = L // CHUNK
    kc = k.astype(f32).reshape(bsz, nh, n, CHUNK, dk)
    vc = v.astype(f32).reshape(bsz, nh, n, CHUNK, dv)
    b = jnp.cumsum(log_a.astype(f32).reshape(bsz, nh, n, CHUNK, dk), axis=3)
    b_last = b[:, :, :, -1:, :]
    chunk_kv = jnp.einsum('bhncd,bhncv->bhndv', kc * jnp.exp(b_last - b), vc)
    decay = jnp.exp(b_last[:, :, :, 0, :])

    def step(s, inp):
        dec, kv = inp
        return dec[..., None] * s + kv, s

    s_final, s_enter = lax.scan(step, s0.astype(f32),
                                (jnp.moveaxis(decay, 2, 0), jnp.moveaxis(chunk_kv, 2, 0)))
    if not with_output:
        return None, s_final
    s_enter = jnp.moveaxis(s_enter, 0, 2)
    qc = q.astype(f32).reshape(bsz, nh, n, CHUNK, dk) * (dk ** -0.5)
    q_t = qc * jnp.exp(b)
    k_t = kc * jnp.exp(-b)
    attn = jnp.einsum('bhncd,bhnsd->bhncs', q_t, k_t)
    mask = jnp.tril(jnp.ones((CHUNK, CHUNK), dtype=bool))
    attn = jnp.where(mask, attn, 0.0)
    o = (jnp.einsum('bhncs,bhnsv->bhncv', attn, vc)
         + jnp.einsum('bhncd,bhndv->bhncv', q_t, s_enter))
    return o.reshape(bsz, nh, L, dv).astype(v.dtype), s_final


def gla_bidir(q, k, v, la_f, la_b, s0_f, s0_b, with_output):
    flip = lambda t: jnp.flip(t, axis=2)
    o_f, s_f = gla_chunked(q, k, v, la_f, s0_f, with_output)
    o_b, s_b = gla_chunked(flip(q), flip(k), flip(v), flip(la_b), s0_b, with_output)
    o = o_f + flip(o_b) if with_output else None
    return o, s_f, s_b


def project(h, w_in, w_a2_f, b_a_f, w_a2_b, b_a_b):
    p = h @ w_in
    u_conv, q, k, v, g, r_f, r_b = jnp.split(p, SPLITS, axis=-1)
    bsz, L, _ = h.shape
    heads = lambda t, d: t.reshape(bsz, L, N_GLA_HEADS, d).transpose(0, 2, 1, 3)
    la_f = jax.nn.log_sigmoid((r_f @ w_a2_f + b_a_f).astype(jnp.float32)) / GLA_TAU
    la_b = jax.nn.log_sigmoid((r_b @ w_a2_b + b_a_b).astype(jnp.float32)) / GLA_TAU
    return (u_conv, heads(q, GLA_HEAD_K), heads(k, GLA_HEAD_K), heads(v, GLA_HEAD_V), g,
            heads(la_f, GLA_HEAD_K), heads(la_b, GLA_HEAD_K))


def merge(u_conv, o, g, n_seg, seg_len, conv_w, conv_b, conv_ln_g, conv_ln_b, gla_norm_g, w_out):
    conv_o = conformer_conv(u_conv, conv_w, conv_b, conv_ln_g, conv_ln_b, n_seg, seg_len)
    bsz, nh, L, dv = o.shape
    o = rms_norm(o, gla_norm_g).transpose(0, 2, 1, 3).reshape(bsz, L, nh * dv).astype(g.dtype)
    o = o * jax.nn.silu(g)
    return jnp.concatenate([conv_o.astype(g.dtype), o], axis=-1) @ w_out


def swiglu(h, w_gate, w_up, w_down):
    return (jax.nn.silu(h @ w_gate) * (h @ w_up)) @ w_down


def _fwd_setup_inputs(seed: int = 0) -> dict:
    key = jax.random.key(seed)
    ks = jax.random.split(key, 23)
    nrm = lambda k, s, sc: jax.random.normal(k, s, jnp.float32) * sc
    L = DEPTH
    return {
        'x': nrm(ks[0], (BATCH, SEQ, D_MODEL), 1.0),
        'c': nrm(ks[1], (BATCH, D_MODEL), 1.0),
        'ctx': nrm(ks[2], (BATCH, CTX_LEN, D_MODEL), 1.0),
        'c_ctx': nrm(ks[3], (D_MODEL,), 1.0),
        'w_mod': nrm(ks[4], (L, D_MODEL, 6 * D_MODEL), 0.5 * D_MODEL ** -0.5),
        'b_mod': nrm(ks[5], (L, 6 * D_MODEL), 0.02),
        'norm1_g': 1.0 + nrm(ks[6], (L, D_MODEL), 0.02),
        'norm2_g': 1.0 + nrm(ks[7], (L, D_MODEL), 0.02),
        'w_in': nrm(ks[8], (L, D_MODEL, D_IN), D_MODEL ** -0.5),
        'conv_w': nrm(ks[9], (L, CONV_WIDTH, D_CONV), CONV_WIDTH ** -0.5),
        'conv_b': nrm(ks[10], (L, D_CONV), 0.02),
        'conv_ln_g': 1.0 + nrm(ks[11], (L, D_CONV), 0.02),
        'conv_ln_b': nrm(ks[12], (L, D_CONV), 0.02),
        'w_a2_f': nrm(ks[13], (L, GLA_RANK, D_GLA_K), GLA_RANK ** -0.5),
        'b_a_f': nrm(ks[14], (L, D_GLA_K), 0.1),
        'w_a2_b': nrm(ks[15], (L, GLA_RANK, D_GLA_K), GLA_RANK ** -0.5),
        'b_a_b': nrm(ks[16], (L, D_GLA_K), 0.1),
        'gla_norm_g': 1.0 + nrm(ks[17], (L, GLA_HEAD_V), 0.02),
        'w_out': nrm(ks[18], (L, D_MIX, D_MODEL), D_MIX ** -0.5),
        'w_gate': nrm(ks[19], (L, D_MODEL, D_FF), D_MODEL ** -0.5),
        'w_up': nrm(ks[20], (L, D_MODEL, D_FF), D_MODEL ** -0.5),
        'w_down': nrm(ks[21], (L, D_FF, D_MODEL), D_FF ** -0.5),
        'final_g': 1.0 + nrm(ks[22], (D_MODEL,), 0.02),
    }


def _fwd_reference(x, c, ctx, c_ctx, w_mod, b_mod, norm1_g, norm2_g, w_in, conv_w, conv_b,
              conv_ln_g, conv_ln_b, w_a2_f, b_a_f, w_a2_b, b_a_b, gla_norm_g, w_out,
              w_gate, w_up, w_down, final_g):
    bsz, n_lat, _ = x.shape
    rows = n_lat // GRID_W
    n_ctx = ctx.shape[1]
    for l in range(DEPTH):
        last = l == DEPTH - 1
        mod = jax.nn.silu(c) @ w_mod[l] + b_mod[l]
        sh1, sc1, g1, sh2, sc2, g2 = jnp.split(mod[:, None, :], 6, axis=-1)
        mod_c = jax.nn.silu(c_ctx) @ w_mod[l] + b_mod[l]
        csh1, csc1, cg1, csh2, csc2, cg2 = jnp.split(mod_c, 6, axis=-1)

        hc = modulate(rms_norm(ctx, norm1_g[l]), csh1, csc1)
        uc, qc, kc, vc, gc, laf_c, lab_c = project(hc, w_in[l], w_a2_f[l], b_a_f[l], w_a2_b[l], b_a_b[l])
        s_zero = jnp.zeros((bsz, N_GLA_HEADS, GLA_HEAD_K, GLA_HEAD_V), jnp.float32)
        oc, sf_c, sb_c = gla_bidir(qc, kc, vc, laf_c, lab_c, s_zero, s_zero, not last)

        h = modulate(rms_norm(x, norm1_g[l]), sh1, sc1)
        u, q, k, v, g, laf, lab = project(h, w_in[l], w_a2_f[l], b_a_f[l], w_a2_b[l], b_a_b[l])
        o, _, _ = gla_bidir(q, k, v, laf, lab, sf_c, sb_c, True)
        x = x + g1 * merge(u, o, g, rows, GRID_W, conv_w[l], conv_b[l], conv_ln_g[l],
                           conv_ln_b[l], gla_norm_g[l], w_out[l])
        h2 = modulate(rms_norm(x, norm2_g[l]), sh2, sc2)
        x = x + g2 * swiglu(h2, w_gate[l], w_up[l], w_down[l])

        if not last:
            ctx = ctx + cg1 * merge(uc, oc, gc, 1, n_ctx, conv_w[l], conv_b[l], conv_ln_g[l],
                                    conv_ln_b[l], gla_norm_g[l], w_out[l])
            hc2 = modulate(rms_norm(ctx, norm2_g[l]), csh2, csc2)
            ctx = ctx + cg2 * swiglu(hc2, w_gate[l], w_up[l], w_down[l])
    return rms_norm(x, final_g)


import jax as _jax
import jax.numpy as _jnp

TWIN_FORMAT = 'train_step'
FWD_PARAMS = ['x', 'c', 'ctx', 'c_ctx', 'w_mod', 'b_mod', 'norm1_g', 'norm2_g', 'w_in', 'conv_w', 'conv_b', 'conv_ln_g', 'conv_ln_b', 'w_a2_f', 'b_a_f', 'w_a2_b', 'b_a_b', 'gla_norm_g', 'w_out', 'w_gate', 'w_up', 'w_down', 'final_g']
TWIN_WEIGHTS = ['c_ctx', 'w_mod', 'b_mod', 'norm1_g', 'norm2_g', 'w_in', 'conv_w', 'conv_b', 'conv_ln_g', 'conv_ln_b', 'w_a2_f', 'b_a_f', 'w_a2_b', 'b_a_b', 'gla_norm_g', 'w_out', 'w_gate', 'w_up', 'w_down', 'final_g']
TWIN_DIFF_INPUT = 'x'
TWIN_INPUTS = ['x', 'c', 'ctx', 'c_ctx', 'w_mod', 'b_mod', 'norm1_g', 'norm2_g', 'w_in', 'conv_w', 'conv_b', 'conv_ln_g', 'conv_ln_b', 'w_a2_f', 'b_a_f', 'w_a2_b', 'b_a_b', 'gla_norm_g', 'w_out', 'w_gate', 'w_up', 'w_down', 'final_g', 'loss_target', 'm_c_ctx', 'm_w_mod', 'm_b_mod', 'm_norm1_g', 'm_norm2_g', 'm_w_in', 'm_conv_w', 'm_conv_b', 'm_conv_ln_g', 'm_conv_ln_b', 'm_w_a2_f', 'm_b_a_f', 'm_w_a2_b', 'm_b_a_b', 'm_gla_norm_g', 'm_w_out', 'm_w_gate', 'm_w_up', 'm_w_down', 'm_final_g', 'v_c_ctx', 'v_w_mod', 'v_b_mod', 'v_norm1_g', 'v_norm2_g', 'v_w_in', 'v_conv_w', 'v_conv_b', 'v_conv_ln_g', 'v_conv_ln_b', 'v_w_a2_f', 'v_b_a_f', 'v_w_a2_b', 'v_b_a_b', 'v_gla_norm_g', 'v_w_out', 'v_w_gate', 'v_w_up', 'v_w_down', 'v_final_g']
TWIN_OUTPUTS = ['loss', 'grad_x', 'grad_c_ctx', 'grad_w_mod', 'grad_b_mod', 'grad_norm1_g', 'grad_norm2_g', 'grad_w_in', 'grad_conv_w', 'grad_conv_b', 'grad_conv_ln_g', 'grad_conv_ln_b', 'grad_w_a2_f', 'grad_b_a_f', 'grad_w_a2_b', 'grad_b_a_b', 'grad_gla_norm_g', 'grad_w_out', 'grad_w_gate', 'grad_w_up', 'grad_w_down', 'grad_final_g', 'delta_c_ctx', 'delta_w_mod', 'delta_b_mod', 'delta_norm1_g', 'delta_norm2_g', 'delta_w_in', 'delta_conv_w', 'delta_conv_b', 'delta_conv_ln_g', 'delta_conv_ln_b', 'delta_w_a2_f', 'delta_b_a_f', 'delta_w_a2_b', 'delta_b_a_b', 'delta_gla_norm_g', 'delta_w_out', 'delta_w_gate', 'delta_w_up', 'delta_w_down', 'delta_final_g', 'new_m_c_ctx', 'new_m_w_mod', 'new_m_b_mod', 'new_m_norm1_g', 'new_m_norm2_g', 'new_m_w_in', 'new_m_conv_w', 'new_m_conv_b', 'new_m_conv_ln_g', 'new_m_conv_ln_b', 'new_m_w_a2_f', 'new_m_b_a_f', 'new_m_w_a2_b', 'new_m_b_a_b', 'new_m_gla_norm_g', 'new_m_w_out', 'new_m_w_gate', 'new_m_w_up', 'new_m_w_down', 'new_m_final_g', 'new_v_c_ctx', 'new_v_w_mod', 'new_v_b_mod', 'new_v_norm1_g', 'new_v_norm2_g', 'new_v_w_in', 'new_v_conv_w', 'new_v_conv_b', 'new_v_conv_ln_g', 'new_v_conv_ln_b', 'new_v_w_a2_f', 'new_v_b_a_f', 'new_v_w_a2_b', 'new_v_b_a_b', 'new_v_gla_norm_g', 'new_v_w_out', 'new_v_w_gate', 'new_v_w_up', 'new_v_w_down', 'new_v_final_g']
TWIN_LEAF_KINDS = {'loss': 'loss', 'grad_x': 'grad_x', 'grad_c_ctx': 'grad_w', 'grad_w_mod': 'grad_w', 'grad_b_mod': 'grad_w', 'grad_norm1_g': 'grad_w', 'grad_norm2_g': 'grad_w', 'grad_w_in': 'grad_w', 'grad_conv_w': 'grad_w', 'grad_conv_b': 'grad_w', 'grad_conv_ln_g': 'grad_w', 'grad_conv_ln_b': 'grad_w', 'grad_w_a2_f': 'grad_w', 'grad_b_a_f': 'grad_w', 'grad_w_a2_b': 'grad_w', 'grad_b_a_b': 'grad_w', 'grad_gla_norm_g': 'grad_w', 'grad_w_out': 'grad_w', 'grad_w_gate': 'grad_w', 'grad_w_up': 'grad_w', 'grad_w_down': 'grad_w', 'grad_final_g': 'grad_w', 'delta_c_ctx': 'delta_w', 'delta_w_mod': 'delta_w', 'delta_b_mod': 'delta_w', 'delta_norm1_g': 'delta_w', 'delta_norm2_g': 'delta_w', 'delta_w_in': 'delta_w', 'delta_conv_w': 'delta_w', 'delta_conv_b': 'delta_w', 'delta_conv_ln_g': 'delta_w', 'delta_conv_ln_b': 'delta_w', 'delta_w_a2_f': 'delta_w', 'delta_b_a_f': 'delta_w', 'delta_w_a2_b': 'delta_w', 'delta_b_a_b': 'delta_w', 'delta_gla_norm_g': 'delta_w', 'delta_w_out': 'delta_w', 'delta_w_gate': 'delta_w', 'delta_w_up': 'delta_w', 'delta_w_down': 'delta_w', 'delta_final_g': 'delta_w', 'new_m_c_ctx': 'new_m', 'new_m_w_mod': 'new_m', 'new_m_b_mod': 'new_m', 'new_m_norm1_g': 'new_m', 'new_m_norm2_g': 'new_m', 'new_m_w_in': 'new_m', 'new_m_conv_w': 'new_m', 'new_m_conv_b': 'new_m', 'new_m_conv_ln_g': 'new_m', 'new_m_conv_ln_b': 'new_m', 'new_m_w_a2_f': 'new_m', 'new_m_b_a_f': 'new_m', 'new_m_w_a2_b': 'new_m', 'new_m_b_a_b': 'new_m', 'new_m_gla_norm_g': 'new_m', 'new_m_w_out': 'new_m', 'new_m_w_gate': 'new_m', 'new_m_w_up': 'new_m', 'new_m_w_down': 'new_m', 'new_m_final_g': 'new_m', 'new_v_c_ctx': 'new_v', 'new_v_w_mod': 'new_v', 'new_v_b_mod': 'new_v', 'new_v_norm1_g': 'new_v', 'new_v_norm2_g': 'new_v', 'new_v_w_in': 'new_v', 'new_v_conv_w': 'new_v', 'new_v_conv_b': 'new_v', 'new_v_conv_ln_g': 'new_v', 'new_v_conv_ln_b': 'new_v', 'new_v_w_a2_f': 'new_v', 'new_v_b_a_f': 'new_v', 'new_v_w_a2_b': 'new_v', 'new_v_b_a_b': 'new_v', 'new_v_gla_norm_g': 'new_v', 'new_v_w_out': 'new_v', 'new_v_w_gate': 'new_v', 'new_v_w_up': 'new_v', 'new_v_w_down': 'new_v', 'new_v_final_g': 'new_v'}


def _forward(args):
    return _fwd_reference(*[args[k] for k in FWD_PARAMS])


def _output_shape():
    def fwd():
        inp = _fwd_setup_inputs(0)
        return _fwd_reference(*[inp[k] for k in FWD_PARAMS])
    out = _jax.eval_shape(fwd)
    return out.shape, out.dtype

N_MICROBATCH = 1
ADAM_LR = 0.001
ADAM_B1 = 0.9
ADAM_B2 = 0.999
ADAM_EPS = 1e-08
ADAM_WD = 0.01
ADAM_STEP = 10
PER_EXAMPLE_BATCH_AXIS = {'x': 0, 'c': 0, 'ctx': 0, 'loss_target': 0}
SHARED_INPUTS = []
_WEIGHT_DTYPES = {'c_ctx': _jnp.float32, 'w_mod': _jnp.float32, 'b_mod': _jnp.float32, 'norm1_g': _jnp.float32, 'norm2_g': _jnp.float32, 'w_in': _jnp.float32, 'conv_w': _jnp.float32, 'conv_b': _jnp.float32, 'conv_ln_g': _jnp.float32, 'conv_ln_b': _jnp.float32, 'w_a2_f': _jnp.float32, 'b_a_f': _jnp.float32, 'w_a2_b': _jnp.float32, 'b_a_b': _jnp.float32, 'gla_norm_g': _jnp.float32, 'w_out': _jnp.float32, 'w_gate': _jnp.float32, 'w_up': _jnp.float32, 'w_down': _jnp.float32, 'final_g': _jnp.float32}
MOMENT_SCALE = {'c_ctx': 7.515258e-03, 'w_mod': 1.038814e-01, 'b_mod': 1.695294e-01, 'norm1_g': 1.186084e-01, 'norm2_g': 1.060601e-01, 'w_in': 7.315865e-02, 'conv_w': 6.670318e-02, 'conv_b': 1.251778e-01, 'conv_ln_g': 7.994039e-02, 'conv_ln_b': 6.773355e-02, 'w_a2_f': 1.237338e-02, 'b_a_f': 3.314934e-02, 'w_a2_b': 1.221766e-02, 'b_a_b': 3.265161e-02, 'gla_norm_g': 1.467098e-01, 'w_out': 6.757024e-02, 'w_gate': 4.609354e-02, 'w_up': 4.455452e-02, 'w_down': 7.392387e-02, 'final_g': 1.280415e+02}


def _to_microbatches(a, axis):
    t = _jnp.moveaxis(a, axis, 0)
    t = t.reshape((N_MICROBATCH, t.shape[0] // N_MICROBATCH) + t.shape[1:])
    return _jnp.moveaxis(t, 1, axis + 1)


def setup_inputs(seed: int = 0) -> dict:
    inp = _fwd_setup_inputs(seed)
    key = _jax.random.fold_in(_jax.random.key(seed), 7919)
    shape, _ = _output_shape()
    out = dict(inp)
    out["loss_target"] = _jax.random.normal(_jax.random.fold_in(key, 0), shape, _jnp.float32)
    for i, name in enumerate(TWIN_WEIGHTS):
        w = inp[name].astype(_jnp.float32)
        if MOMENT_SCALE is None:
            s = _jnp.sqrt(_jnp.mean(_jnp.square(w)) + 1e-30)
        else:
            s = MOMENT_SCALE[name]
        km, kv = _jax.random.split(_jax.random.fold_in(key, i + 1))
        out[name] = w
        out["m_" + name] = s * _jax.random.normal(km, w.shape, _jnp.float32)
        out["v_" + name] = (s * s) * _jax.random.uniform(kv, w.shape, _jnp.float32, 0.5, 1.5)
    if N_MICROBATCH > 1:
        for name, axis in PER_EXAMPLE_BATCH_AXIS.items():
            out[name] = _to_microbatches(out[name], axis)
    return {'x': out['x'], 'c': out['c'], 'ctx': out['ctx'], 'c_ctx': out['c_ctx'], 'w_mod': out['w_mod'], 'b_mod': out['b_mod'], 'norm1_g': out['norm1_g'], 'norm2_g': out['norm2_g'], 'w_in': out['w_in'], 'conv_w': out['conv_w'], 'conv_b': out['conv_b'], 'conv_ln_g': out['conv_ln_g'], 'conv_ln_b': out['conv_ln_b'], 'w_a2_f': out['w_a2_f'], 'b_a_f': out['b_a_f'], 'w_a2_b': out['w_a2_b'], 'b_a_b': out['b_a_b'], 'gla_norm_g': out['gla_norm_g'], 'w_out': out['w_out'], 'w_gate': out['w_gate'], 'w_up': out['w_up'], 'w_down': out['w_down'], 'final_g': out['final_g'], 'loss_target': out['loss_target'], 'm_c_ctx': out['m_c_ctx'], 'm_w_mod': out['m_w_mod'], 'm_b_mod': out['m_b_mod'], 'm_norm1_g': out['m_norm1_g'], 'm_norm2_g': out['m_norm2_g'], 'm_w_in': out['m_w_in'], 'm_conv_w': out['m_conv_w'], 'm_conv_b': out['m_conv_b'], 'm_conv_ln_g': out['m_conv_ln_g'], 'm_conv_ln_b': out['m_conv_ln_b'], 'm_w_a2_f': out['m_w_a2_f'], 'm_b_a_f': out['m_b_a_f'], 'm_w_a2_b': out['m_w_a2_b'], 'm_b_a_b': out['m_b_a_b'], 'm_gla_norm_g': out['m_gla_norm_g'], 'm_w_out': out['m_w_out'], 'm_w_gate': out['m_w_gate'], 'm_w_up': out['m_w_up'], 'm_w_down': out['m_w_down'], 'm_final_g': out['m_final_g'], 'v_c_ctx': out['v_c_ctx'], 'v_w_mod': out['v_w_mod'], 'v_b_mod': out['v_b_mod'], 'v_norm1_g': out['v_norm1_g'], 'v_norm2_g': out['v_norm2_g'], 'v_w_in': out['v_w_in'], 'v_conv_w': out['v_conv_w'], 'v_conv_b': out['v_conv_b'], 'v_conv_ln_g': out['v_conv_ln_g'], 'v_conv_ln_b': out['v_conv_ln_b'], 'v_w_a2_f': out['v_w_a2_f'], 'v_b_a_f': out['v_b_a_f'], 'v_w_a2_b': out['v_w_a2_b'], 'v_b_a_b': out['v_b_a_b'], 'v_gla_norm_g': out['v_gla_norm_g'], 'v_w_out': out['v_w_out'], 'v_w_gate': out['v_w_gate'], 'v_w_up': out['v_w_up'], 'v_w_down': out['v_w_down'], 'v_final_g': out['v_final_g']}


def _loss(weights, diff, rest, loss_target):
    with _jax.named_scope("forward"):
        args = {**rest, TWIN_DIFF_INPUT: diff, **{k: w.astype(_WEIGHT_DTYPES[k]) for k, w in weights.items()}}
        y = _forward(args)
    with _jax.named_scope("loss_head"):
        err = _jnp.square(y.astype(_jnp.float32) - loss_target)
        return 0.5 * _jnp.sum(_jnp.mean(err, axis=-1)) if err.ndim else 0.5 * err


def _adamw(w, g, m, v):
    m = ADAM_B1 * m + (1.0 - ADAM_B1) * g
    v = ADAM_B2 * v + (1.0 - ADAM_B2) * _jnp.square(g)
    m_hat = m / (1.0 - ADAM_B1 ** ADAM_STEP)
    v_hat = v / (1.0 - ADAM_B2 ** ADAM_STEP)
    delta = -ADAM_LR * (m_hat / (_jnp.sqrt(v_hat) + ADAM_EPS) + ADAM_WD * w)
    return delta, m, v


def reference(x, c, ctx, c_ctx, w_mod, b_mod, norm1_g, norm2_g, w_in, conv_w, conv_b, conv_ln_g, conv_ln_b, w_a2_f, b_a_f, w_a2_b, b_a_b, gla_norm_g, w_out, w_gate, w_up, w_down, final_g, loss_target, m_c_ctx, m_w_mod, m_b_mod, m_norm1_g, m_norm2_g, m_w_in, m_conv_w, m_conv_b, m_conv_ln_g, m_conv_ln_b, m_w_a2_f, m_b_a_f, m_w_a2_b, m_b_a_b, m_gla_norm_g, m_w_out, m_w_gate, m_w_up, m_w_down, m_final_g, v_c_ctx, v_w_mod, v_b_mod, v_norm1_g, v_norm2_g, v_w_in, v_conv_w, v_conv_b, v_conv_ln_g, v_conv_ln_b, v_w_a2_f, v_b_a_f, v_w_a2_b, v_b_a_b, v_gla_norm_g, v_w_out, v_w_gate, v_w_up, v_w_down, v_final_g):
    given = dict(x=x, c=c, ctx=ctx, c_ctx=c_ctx, w_mod=w_mod, b_mod=b_mod, norm1_g=norm1_g, norm2_g=norm2_g, w_in=w_in, conv_w=conv_w, conv_b=conv_b, conv_ln_g=conv_ln_g, conv_ln_b=conv_ln_b, w_a2_f=w_a2_f, b_a_f=b_a_f, w_a2_b=w_a2_b, b_a_b=b_a_b, gla_norm_g=gla_norm_g, w_out=w_out, w_gate=w_gate, w_up=w_up, w_down=w_down, final_g=final_g, loss_target=loss_target, m_c_ctx=m_c_ctx, m_w_mod=m_w_mod, m_b_mod=m_b_mod, m_norm1_g=m_norm1_g, m_norm2_g=m_norm2_g, m_w_in=m_w_in, m_conv_w=m_conv_w, m_conv_b=m_conv_b, m_conv_ln_g=m_conv_ln_g, m_conv_ln_b=m_conv_ln_b, m_w_a2_f=m_w_a2_f, m_b_a_f=m_b_a_f, m_w_a2_b=m_w_a2_b, m_b_a_b=m_b_a_b, m_gla_norm_g=m_gla_norm_g, m_w_out=m_w_out, m_w_gate=m_w_gate, m_w_up=m_w_up, m_w_down=m_w_down, m_final_g=m_final_g, v_c_ctx=v_c_ctx, v_w_mod=v_w_mod, v_b_mod=v_b_mod, v_norm1_g=v_norm1_g, v_norm2_g=v_norm2_g, v_w_in=v_w_in, v_conv_w=v_conv_w, v_conv_b=v_conv_b, v_conv_ln_g=v_conv_ln_g, v_conv_ln_b=v_conv_ln_b, v_w_a2_f=v_w_a2_f, v_b_a_f=v_b_a_f, v_w_a2_b=v_w_a2_b, v_b_a_b=v_b_a_b, v_gla_norm_g=v_gla_norm_g, v_w_out=v_w_out, v_w_gate=v_w_gate, v_w_up=v_w_up, v_w_down=v_w_down, v_final_g=v_final_g)
    weights = {n: given[n] for n in TWIN_WEIGHTS}
    shared = {n: given[n] for n in SHARED_INPUTS}
    per_example = {n: given[n] for n in ['x', 'c', 'ctx']}
    grad_fn = _jax.value_and_grad(_loss, argnums=(0, 1))

    def one_microbatch(ex, loss_target):
        ex = dict(ex)
        diff = ex.pop(TWIN_DIFF_INPUT)
        return grad_fn(weights, diff, {**shared, **ex}, loss_target)

    if N_MICROBATCH == 1:
        loss, (grad_w, grad_x) = one_microbatch(per_example, given["loss_target"])
    else:
        def body(carry, xs):
            loss_sum, grad_sum = carry
            l_k, (gw_k, gx_k) = one_microbatch(xs[0], xs[1])
            with _jax.named_scope("update"):
                return (loss_sum + l_k, _jax.tree.map(_jnp.add, grad_sum, gw_k)), gx_k

        init = (_jnp.zeros((), _jnp.float32), _jax.tree.map(_jnp.zeros_like, weights))
        (loss, grad_w), grad_x = _jax.lax.scan(body, init, (per_example, given["loss_target"]))
    with _jax.named_scope("update"):
        delta_w, new_m, new_v = {}, {}, {}
        for n in TWIN_WEIGHTS:
            delta_w[n], new_m[n], new_v[n] = _adamw(weights[n], grad_w[n], given["m_" + n], given["v_" + n])
    return (loss, grad_x, *[grad_w[n] for n in TWIN_WEIGHTS], *[delta_w[n] for n in TWIN_WEIGHTS],
            *[new_m[n] for n in TWIN_WEIGHTS], *[new_v[n] for n in TWIN_WEIGHTS])
```

```python
import functools

import jax
import jax.numpy as jnp
from jax import lax
from jax.experimental import pallas as pl
from jax.experimental.pallas import tpu as pltpu

F32 = jnp.float32
MXU_DTYPE = jnp.bfloat16
HI = lax.Precision.HIGHEST
MESH = pl.DeviceIdType.MESH

N_DEV = 8
D = 1024
DC = 512
NH = 4
HK = 64
HV = 128
DK = NH * HK
DV = NH * HV
RANK = 16
CHUNK = 64
SEG = 64
CW = 31
CPAD = 15
DFF = 2816
DIN = 2592
DINP = 2688
TAU = 16.0
EPS = 1e-6
VMEM_LIMIT = 56 * 1024 * 1024

ADAM_LR = 0.001
ADAM_B1 = 0.9
ADAM_B2 = 0.999
ADAM_EPS = 1e-08
ADAM_WD = 0.01
ADAM_STEP = 10


def _mm(a, b):
    return jnp.dot(a.astype(MXU_DTYPE), b.astype(MXU_DTYPE), preferred_element_type=F32)


def _mm_nt(a, b):
    return lax.dot_general(a.astype(MXU_DTYPE), b.astype(MXU_DTYPE), (((1,), (1,)), ((), ())),
                           preferred_element_type=F32)


def _mm_tn(a, b):
    return lax.dot_general(a.astype(MXU_DTYPE), b.astype(MXU_DTYPE), (((0,), (0,)), ((), ())),
                           preferred_element_type=F32)


def _hi(a, b):
    return jnp.dot(a, b, precision=HI, preferred_element_type=F32)


def _hi_nt(a, b):
    return lax.dot_general(a, b, (((1,), (1,)), ((), ())), precision=HI, preferred_element_type=F32)


def _hi_tn(a, b):
    return lax.dot_general(a, b, (((0,), (0,)), ((), ())), precision=HI, preferred_element_type=F32)


def _sigmoid(x):
    return 1.0 / (1.0 + jnp.exp(-x))


def _cparams(n_axes):
    return pltpu.CompilerParams(dimension_semantics=("arbitrary",) * n_axes, vmem_limit_bytes=VMEM_LIMIT)


def _full(shape):
    n = len(shape)
    return pl.BlockSpec(shape, lambda *_: (0,) * n)


def _rows(tt, width):
    return pl.BlockSpec((tt, width), lambda i: (i, 0))


def _sds(shape, dtype=F32):
    return jax.ShapeDtypeStruct(shape, dtype)


def _norm_mod(x, g, sh, sc):
    r = lax.rsqrt(jnp.mean(x * x, axis=-1, keepdims=True) + EPS)
    xn = x * r
    yy = xn * g
    return r, xn, yy, yy * (1.0 + sc) + sh


def _norm_mod_bwd(dh, r, xn, yy, g, sc):
    dsh = jnp.sum(dh, axis=0, keepdims=True)
    dsc = jnp.sum(dh * yy, axis=0, keepdims=True)
    dy = dh * (1.0 + sc)
    dg = jnp.sum(dy * xn, axis=0, keepdims=True)
    dxn = dy * g
    dx = r * (dxn - xn * jnp.mean(dxn * xn, axis=-1, keepdims=True))
    return dsh, dsc, dg, dx


def _acc_rows(ref, first, rows):
    upd = jnp.concatenate(rows + [jnp.zeros((8 - len(rows), rows[0].shape[1]), F32)], axis=0)

    @pl.when(first)
    def _():
        ref[...] = upd

    @pl.when(jnp.logical_not(first))
    def _():
        ref[...] += upd


def _acc(ref, first, val):
    @pl.when(first)
    def _():
        ref[...] = val

    @pl.when(jnp.logical_not(first))
    def _():
        ref[...] += val


def proj_fwd(x, vec, w_int, wa, ba, tt):
    t = x.shape[0]

    def body(x_ref, vec_ref, w_ref, wa_ref, ba_ref, u_ref, q_ref, k_ref, v_ref, g_ref, r_ref, la_ref, h_ref):
        _, _, _, h = _norm_mod(x_ref[...], vec_ref[0:1, :], vec_ref[1:2, :], vec_ref[2:3, :])
        hb = h.astype(MXU_DTYPE)
        h_ref[...] = hb
        p = _mm_nt(hb, w_ref[...])
        u_ref[...] = p[:, 0:1024]
        q_ref[...] = p[:, 1024:1280]
        k_ref[...] = p[:, 1280:1536]
        v_ref[...] = p[:, 1536:2048]
        g_ref[...] = p[:, 2048:2560]
        rr = p[:, 2560:2688]
        r_ref[...] = rr
        z = _hi(rr, wa_ref[...]) + ba_ref[...]
        la_ref[...] = (jnp.minimum(z, 0.0) - jnp.log(1.0 + jnp.exp(-jnp.abs(z)))) * (1.0 / TAU)

    return pl.pallas_call(
        body, name="proj_fwd", grid=(t // tt,),
        in_specs=[_rows(tt, D), _full((8, D)), _full((DINP, D)), _full((128, 512)), _full((1, 512))],
        out_specs=[_rows(tt, 1024), _rows(tt, DK), _rows(tt, DK), _rows(tt, DV), _rows(tt, DV), _rows(tt, 128),
                   _rows(tt, 512), _rows(tt, D)],
        out_shape=[_sds((t, 1024)), _sds((t, DK)), _sds((t, DK)), _sds((t, DV)), _sds((t, DV)), _sds((t, 128)),
                   _sds((t, 512)), _sds((t, D), MXU_DTYPE)],
        compiler_params=_cparams(1),
    )(x, vec, w_int, wa, ba)


def proj_bwd(du, dq2, dk2, dv2, dg, dla, la, r, x, dx1, vec, w_int, wa, tt):
    t = x.shape[0]

    def body(du_ref, dq_ref, dk_ref, dv_ref, dg_ref, dla_ref, la_ref, r_ref, x_ref, dx1_ref, vec_ref, w_ref, wa_ref,
             gx_ref, dp_ref, acc_ref, dba_ref, dwa_ref):
        first = pl.program_id(0) == 0
        dz = dla_ref[...] * (1.0 - jnp.exp(TAU * la_ref[...])) * (1.0 / TAU)
        rr = r_ref[...]
        _acc_rows(dba_ref, first, [jnp.sum(dz, axis=0, keepdims=True)])
        _acc(dwa_ref, first, _hi_tn(rr, dz))
        dr = _hi_nt(dz, wa_ref[...])
        dp = jnp.concatenate([du_ref[...], dq_ref[0] + dq_ref[1], dk_ref[0] + dk_ref[1], dv_ref[0] + dv_ref[1],
                              dg_ref[...], dr], axis=1).astype(MXU_DTYPE)
        dp_ref[...] = dp
        dh = _mm(dp, w_ref[...])
        g, sc = vec_ref[0:1, :], vec_ref[2:3, :]
        rn, xn, yy, _ = _norm_mod(x_ref[...], g, vec_ref[1:2, :], sc)
        dsh, dsc, dgn, dx = _norm_mod_bwd(dh, rn, xn, yy, g, sc)
        gx_ref[...] = dx1_ref[...] + dx
        _acc_rows(acc_ref, first, [dsh, dsc, dgn])

    two = lambda w: pl.BlockSpec((2, tt, w), lambda i: (0, i, 0))
    return pl.pallas_call(
        body, name="proj_bwd", grid=(t // tt,),
        in_specs=[_rows(tt, 1024), two(DK), two(DK), two(DV), _rows(tt, DV), _rows(tt, 512), _rows(tt, 512),
                  _rows(tt, 128), _rows(tt, D), _rows(tt, D), _full((8, D)), _full((DINP, D)), _full((128, 512))],
        out_specs=[_rows(tt, D), _rows(tt, DINP), _full((8, D)), _full((8, 512)), _full((128, 512))],
        out_shape=[_sds((t, D)), _sds((t, DINP), MXU_DTYPE), _sds((8, D)), _sds((8, 512)), _sds((128, 512))],
        compiler_params=_cparams(1),
    )(du, dq2, dk2, dv2, dg, dla, la, r, x, dx1, vec, w_int, wa)


def _gla_chunk(d, qc, kc, la_c):
    row = lax.broadcasted_iota(jnp.int32, (CHUNK, CHUNK), 0)
    col = lax.broadcasted_iota(jnp.int32, (CHUNK, CHUNK), 1)
    cum = jnp.where(d == 0, (col <= row).astype(F32), (col >= row).astype(F32))
    cum4 = jnp.concatenate([cum] * NH, axis=0)
    head_of_lane = lax.broadcasted_iota(jnp.int32, (1, DK), 1) // HK
    b = _hi(cum, la_c)
    blc = _hi_tn(la_c, jnp.ones((CHUNK, HV), F32))
    bl = jnp.sum(la_c, axis=0, keepdims=True)
    eb = jnp.exp(b)
    enb = jnp.exp(-b)
    ekd = jnp.exp(bl - b)
    qt = qc * (HK ** -0.5) * eb
    kt = kc * enb
    kd = kc * ekd
    qst = jnp.concatenate([jnp.where(head_of_lane == h, qt, 0.0) for h in range(NH)], axis=0)
    a = _mm_nt(qst, kt) * cum4
    return cum, cum4, head_of_lane, eb, enb, ekd, qt, kt, kd, qst, a, jnp.exp(blc)


def gla_fwd(q, k, v, la, s0, cb):
    t = q.shape[0]
    nc = t // CHUNK
    nb = nc // cb
    blk = lambda d, i: i + d * (nb - 1 - 2 * i)

    def body(q_ref, k_ref, v_ref, la_ref, s0_ref, o_ref, sall_ref, sfin_ref, s_scr):
        d = pl.program_id(0)
        i = pl.program_id(1)

        @pl.when(i == 0)
        def _():
            s_scr[...] = s0_ref[0]

        head_of_row = lax.broadcasted_iota(jnp.int32, (DK, 1), 0) // HK
        for j in range(cb):
            jj = j + d * (cb - 1 - 2 * j)
            rows = pl.ds(pl.multiple_of(jj * CHUNK, CHUNK), CHUNK)
            vc = v_ref[rows, :]
            _, _, _, _, _, _, _, _, kd, qst, a, dec = _gla_chunk(d, q_ref[rows, :], k_ref[rows, :], la_ref[rows, :])
            s = s_scr[...]
            sall_ref[0, jj] = s
            inter = _mm(qst, s)
            outs = []
            for h in range(NH):
                hs = slice(h * CHUNK, (h + 1) * CHUNK)
                outs.append(_mm(a[hs], vc[:, h * HV:(h + 1) * HV]) + inter[hs])
            o_ref[0, rows, :] = jnp.concatenate(outs, axis=1)
            kv = _mm_tn(kd, vc)
            s_new = dec * s
            for h in range(NH):
                s_new = s_new + jnp.where(head_of_row == h, kv[:, h * HV:(h + 1) * HV], 0.0)
            s_scr[...] = s_new

        @pl.when(i == nb - 1)
        def _():
            sfin_ref[0] = s_scr[...]

    tb = cb * CHUNK
    return pl.pallas_call(
        body, name="gla_fwd", grid=(2, nb),
        in_specs=[pl.BlockSpec((tb, DK), lambda d, i: (blk(d, i), 0)),
                  pl.BlockSpec((tb, DK), lambda d, i: (blk(d, i), 0)),
                  pl.BlockSpec((tb, DV), lambda d, i: (blk(d, i), 0)),
                  pl.BlockSpec((tb, DK), lambda d, i: (blk(d, i), d)),
                  pl.BlockSpec((1, DK, HV), lambda d, i: (d, 0, 0))],
        out_specs=[pl.BlockSpec((1, tb, DV), lambda d, i: (d, blk(d, i), 0)),
                   pl.BlockSpec((1, cb, DK, HV), lambda d, i: (d, blk(d, i), 0, 0)),
                   pl.BlockSpec((1, DK, HV), lambda d, i: (d, 0, 0))],
        out_shape=[_sds((2, t, DV)), _sds((2, nc, DK, HV)), _sds((2, DK, HV))],
        scratch_shapes=[pltpu.VMEM((DK, HV), F32)],
        compiler_params=_cparams(2),
    )(q, k, v, la, s0)


def gla_bwd(q, k, v, la, do, sall, dsfin, cb):
    t = q.shape[0]
    nc = t // CHUNK
    nb = nc // cb
    blk = lambda d, i: (nb - 1 - i) + d * (2 * i - (nb - 1))

    def body(q_ref, k_ref, v_ref, la_ref, do_ref, sall_ref, dsfin_ref, dq_ref, dk_ref, dv_ref, dla_ref, ds0_ref,
             ds_scr):
        d = pl.program_id(0)
        i = pl.program_id(1)

        @pl.when(i == 0)
        def _():
            ds_scr[...] = dsfin_ref[0]

        for j in range(cb):
            jj = (cb - 1 - j) + d * (2 * j - (cb - 1))
            rows = pl.ds(pl.multiple_of(jj * CHUNK, CHUNK), CHUNK)
            vc = v_ref[rows, :]
            doc = do_ref[rows, :]
            cum, cum4, head_of_lane, eb, enb, ekd, qt, kt, kd, qst, a, dec = _gla_chunk(
                d, q_ref[rows, :], k_ref[rows, :], la_ref[rows, :])
            s = sall_ref[0, jj]
            ds = ds_scr[...]
            hv = lambda x, h: x[:, h * HV:(h + 1) * HV]
            hr = lambda x, h: x[h * CHUNK:(h + 1) * CHUNK]
            fold = lambda x: functools.reduce(
                lambda p, c: p + c, [jnp.where(head_of_lane == h, hr(x, h), 0.0) for h in range(NH)])
            dost = jnp.concatenate([hv(doc, h) for h in range(NH)], axis=0)
            vst = jnp.concatenate([hv(vc, h) for h in range(NH)], axis=0)
            da = jnp.concatenate([_mm_nt(hv(doc, h), hv(vc, h)) for h in range(NH)], axis=0) * cum4
            dqt = fold(_mm(da, kt) + _mm_nt(dost, s))
            dkt = _mm_tn(da, qst)
            kdst = jnp.concatenate([jnp.where(head_of_lane == h, kd, 0.0) for h in range(NH)], axis=0)
            dv_inter = _mm(kdst, ds)
            dv_ref[0, rows, :] = jnp.concatenate(
                [_mm_tn(hr(a, h), hv(doc, h)) + hr(dv_inter, h) for h in range(NH)], axis=1)
            dkd = fold(_mm_nt(vst, ds))
            ds_scr[...] = dec * ds + _mm_tn(qst, dost)
            dblc = ds * s * dec
            tkd = dkd * kd
            db = dqt * qt - dkt * kt - tkd
            dbl = _hi_nt(jnp.ones((CHUNK, HV), F32), dblc) + jnp.sum(tkd, axis=0, keepdims=True)
            dla_ref[rows, :] = _hi_tn(cum, db) + dbl
            dq_ref[0, rows, :] = dqt * eb * (HK ** -0.5)
            dk_ref[0, rows, :] = dkt * enb + dkd * ekd

        @pl.when(i == nb - 1)
        def _():
            ds0_ref[0] = ds_scr[...]

    tb = cb * CHUNK
    return pl.pallas_call(
        body, name="gla_bwd", grid=(2, nb),
        in_specs=[pl.BlockSpec((tb, DK), lambda d, i: (blk(d, i), 0)),
                  pl.BlockSpec((tb, DK), lambda d, i: (blk(d, i), 0)),
                  pl.BlockSpec((tb, DV), lambda d, i: (blk(d, i), 0)),
                  pl.BlockSpec((tb, DK), lambda d, i: (blk(d, i), d)),
                  pl.BlockSpec((tb, DV), lambda d, i: (blk(d, i), 0)),
                  pl.BlockSpec((1, cb, DK, HV), lambda d, i: (d, blk(d, i), 0, 0)),
                  pl.BlockSpec((1, DK, HV), lambda d, i: (d, 0, 0))],
        out_specs=[pl.BlockSpec((1, tb, DK), lambda d, i: (d, blk(d, i), 0)),
                   pl.BlockSpec((1, tb, DK), lambda d, i: (d, blk(d, i), 0)),
                   pl.BlockSpec((1, tb, DV), lambda d, i: (d, blk(d, i), 0)),
                   pl.BlockSpec((tb, DK), lambda d, i: (blk(d, i), d)),
                   pl.BlockSpec((1, DK, HV), lambda d, i: (d, 0, 0))],
        out_shape=[_sds((2, t, DK)), _sds((2, t, DK)), _sds((2, t, DV)), _sds((t, 2 * DK)), _sds((2, DK, HV))],
        scratch_shapes=[pltpu.VMEM((DK, HV), F32)],
        compiler_params=_cparams(2),
    )(q, k, v, la, do, sall, dsfin)


def _seg_pos(tt):
    return lax.broadcasted_iota(jnp.int32, (tt, 1), 0) % SEG


def _shifted(x, s, pos, tt):
    y = x if s == 0 else pltpu.roll(x, (-s) % tt, 0)
    return jnp.where((pos + s >= 0) & (pos + s < SEG), y, 0.0)


def _head_norm(o, gn):
    rs, xs = [], []
    for h in range(NH):
        oh = o[:, h * HV:(h + 1) * HV]
        r = lax.rsqrt(jnp.mean(oh * oh, axis=-1, keepdims=True) + EPS)
        rs.append(r)
        xs.append(oh * r)
    return rs, xs


def merge_fwd(u, g, o2, x, vec, vc, convw, w_out, tt):
    t = x.shape[0]

    def body(u_ref, g_ref, o_ref, x_ref, vec_ref, vc_ref, cw_ref, w_ref, x1_ref, cat_ref, mix_ref, yc_ref):
        a = u_ref[:, 0:DC]
        gate = u_ref[:, DC:2 * DC]
        vv = a * _sigmoid(gate)
        pos = _seg_pos(tt)
        cw = cw_ref[...]
        yc = jnp.zeros((tt, DC), F32) + vc_ref[1:2, :]
        for j in range(CW):
            yc = yc + _shifted(vv, j - CPAD, pos, tt) * cw[j:j + 1, :]
        yc_ref[...] = yc
        mu = jnp.mean(yc, axis=-1, keepdims=True)
        yd = yc - mu
        rs = lax.rsqrt(jnp.mean(yd * yd, axis=-1, keepdims=True) + EPS)
        ln = yd * rs * vc_ref[2:3, :] + vc_ref[3:4, :]
        conv_o = ln * _sigmoid(ln)
        o = o_ref[0] + o_ref[1]
        _, xs = _head_norm(o, None)
        gg = g_ref[...]
        o2g = jnp.concatenate(xs, axis=1) * vc_ref[0:1, :] * (gg * _sigmoid(gg))
        cat = jnp.concatenate([conv_o, o2g], axis=1).astype(MXU_DTYPE)
        cat_ref[...] = cat
        mix = _mm(cat, w_ref[...])
        mix_ref[...] = mix
        x1_ref[...] = x_ref[...] + vec_ref[0:1, :] * mix

    return pl.pallas_call(
        body, name="merge_fwd", grid=(t // tt,),
        in_specs=[_rows(tt, 1024), _rows(tt, DV), pl.BlockSpec((2, tt, DV), lambda i: (0, i, 0)), _rows(tt, D),
                  _full((8, D)), _full((8, DC)), _full((32, DC)), _full((D, D))],
        out_specs=[_rows(tt, D), _rows(tt, D), _rows(tt, D), _rows(tt, DC)],
        out_shape=[_sds((t, D)), _sds((t, D), MXU_DTYPE), _sds((t, D)), _sds((t, DC))],
        compiler_params=_cparams(1),
    )(u, g, o2, x, vec, vc, convw, w_out)


def merge_bwd(dx1, mix, u, g, o2, yc, vec, vc, convw, w_out, tt):
    t = dx1.shape[0]

    def body(dx1_ref, mix_ref, u_ref, g_ref, o_ref, yc_ref, vec_ref, vc_ref, cw_ref, w_ref,
             du_ref, dg_ref, do_ref, dmix_ref, acc1_ref, acc2_ref, dcw_ref):
        first = pl.program_id(0) == 0
        dx1v = dx1_ref[...]
        dg1 = jnp.sum(dx1v * mix_ref[...], axis=0, keepdims=True)
        dmix = (vec_ref[0:1, :] * dx1v).astype(MXU_DTYPE)
        dmix_ref[...] = dmix
        dcat = _mm_nt(dmix, w_ref[...])
        dconv_o = dcat[:, 0:DC]
        do2 = dcat[:, DC:2 * DC]
        gn = vc_ref[0:1, :]
        o = o_ref[0] + o_ref[1]
        rs, xs = _head_norm(o, None)
        xn = jnp.concatenate(xs, axis=1)
        gg = g_ref[...]
        sg = _sigmoid(gg)
        don = do2 * (gg * sg)
        dg_ref[...] = do2 * (xn * gn) * (sg * (1.0 + gg * (1.0 - sg)))
        dgn = jnp.sum(don * xn, axis=0, keepdims=True)
        dxn = don * gn
        dos = []
        for h in range(NH):
            dh = dxn[:, h * HV:(h + 1) * HV]
            dos.append(rs[h] * (dh - xs[h] * jnp.mean(dh * xs[h], axis=-1, keepdims=True)))
        do_ref[...] = jnp.concatenate(dos, axis=1)
        yc = yc_ref[...]
        mu = jnp.mean(yc, axis=-1, keepdims=True)
        yd = yc - mu
        rstd = lax.rsqrt(jnp.mean(yd * yd, axis=-1, keepdims=True) + EPS)
        yhat = yd * rstd
        lg = vc_ref[2:3, :]
        ln = yhat * lg + vc_ref[3:4, :]
        sl = _sigmoid(ln)
        dln = dconv_o * (sl * (1.0 + ln * (1.0 - sl)))
        dlb = jnp.sum(dln, axis=0, keepdims=True)
        dlg = jnp.sum(dln * yhat, axis=0, keepdims=True)
        dyh = dln * lg
        dyc = rstd * (dyh - jnp.mean(dyh, axis=-1, keepdims=True)
                      - yhat * jnp.mean(dyh * yhat, axis=-1, keepdims=True))
        dcb = jnp.sum(dyc, axis=0, keepdims=True)
        a = u_ref[:, 0:DC]
        gate = u_ref[:, DC:2 * DC]
        sgt = _sigmoid(gate)
        vv = a * sgt
        pos = _seg_pos(tt)
        cw = cw_ref[...]
        dvv = jnp.zeros((tt, DC), F32)
        dws = []
        for j in range(CW):
            s = j - CPAD
            dvv = dvv + _shifted(dyc, -s, pos, tt) * cw[j:j + 1, :]
            dws.append(jnp.sum(dyc * _shifted(vv, s, pos, tt), axis=0, keepdims=True))
        dws.append(jnp.zeros((1, DC), F32))
        du_ref[:, 0:DC] = dvv * sgt
        du_ref[:, DC:2 * DC] = dvv * a * sgt * (1.0 - sgt)
        _acc_rows(acc1_ref, first, [dg1])
        _acc_rows(acc2_ref, first, [dgn, dcb, dlg, dlb])
        _acc(dcw_ref, first, jnp.concatenate(dws, axis=0))

    return pl.pallas_call(
        body, name="merge_bwd", grid=(t // tt,),
        in_specs=[_rows(tt, D), _rows(tt, D), _rows(tt, 1024), _rows(tt, DV),
                  pl.BlockSpec((2, tt, DV), lambda i: (0, i, 0)), _rows(tt, DC),
                  _full((8, D)), _full((8, DC)), _full((32, DC)), _full((D, D))],
        out_specs=[_rows(tt, 1024), _rows(tt, DV), _rows(tt, DV), _rows(tt, D), _full((8, D)), _full((8, DC)),
                   _full((32, DC))],
        out_shape=[_sds((t, 1024)), _sds((t, DV)), _sds((t, DV)), _sds((t, D), MXU_DTYPE), _sds((8, D)),
                   _sds((8, DC)), _sds((32, DC))],
        compiler_params=_cparams(1),
    )(dx1, mix, u, g, o2, yc, vec, vc, convw, w_out)


FN = DFF // 2


def ffn_gate_up(x1, vec, wg_t, wu_t, tt):
    t = x1.shape[0]

    def body(x1_ref, vec_ref, wg_ref, wu_ref, h2_ref, gt_ref, up_ref, hid_ref):
        @pl.when(pl.program_id(1) == 0)
        def _():
            _, _, _, h2 = _norm_mod(x1_ref[...], vec_ref[0:1, :], vec_ref[1:2, :], vec_ref[2:3, :])
            h2_ref[...] = h2.astype(MXU_DTYPE)

        h2 = h2_ref[...]
        gt = _mm_nt(h2, wg_ref[...])
        up = _mm_nt(h2, wu_ref[...])
        gt_ref[...] = gt
        up_ref[...] = up
        hid_ref[...] = (gt * _sigmoid(gt) * up).astype(MXU_DTYPE)

    blk = pl.BlockSpec((tt, FN), lambda i, j: (i, j))
    wblk = pl.BlockSpec((FN, D), lambda i, j: (j, 0))
    return pl.pallas_call(
        body, name="ffn_gate_up", grid=(t // tt, 2),
        in_specs=[pl.BlockSpec((tt, D), lambda i, j: (i, 0)), _full((8, D)), wblk, wblk],
        out_specs=[pl.BlockSpec((tt, D), lambda i, j: (i, 0)), blk, blk, blk],
        out_shape=[_sds((t, D), MXU_DTYPE), _sds((t, DFF)), _sds((t, DFF)), _sds((t, DFF), MXU_DTYPE)],
        compiler_params=_cparams(2),
    )(x1, vec, wg_t, wu_t)


def ffn_down_loss(hid, x1, tgt, vec, w_down, tt):
    t = x1.shape[0]

    def body(hid_ref, x1_ref, tgt_ref, vec_ref, w_ref, dx2_ref, dff_ref, acc_ref):
        first = pl.program_id(0) == 0
        g2 = vec_ref[0:1, :]
        fg = vec_ref[1:2, :]
        ff = _mm(hid_ref[...], w_ref[...])
        x2 = x1_ref[...] + g2 * ff
        rf = lax.rsqrt(jnp.mean(x2 * x2, axis=-1, keepdims=True) + EPS)
        xn = x2 * rf
        err = xn * fg - tgt_ref[...]
        dy = err * (1.0 / D)
        dfg = jnp.sum(dy * xn, axis=0, keepdims=True)
        dxn = dy * fg
        dx2 = rf * (dxn - xn * jnp.mean(dxn * xn, axis=-1, keepdims=True))
        dx2_ref[...] = dx2
        dff_ref[...] = (g2 * dx2).astype(MXU_DTYPE)
        dg2 = jnp.sum(dx2 * ff, axis=0, keepdims=True)
        loss = jnp.sum(err * err, axis=0, keepdims=True) * (0.5 / D)
        _acc_rows(acc_ref, first, [dg2, dfg, loss])

    return pl.pallas_call(
        body, name="ffn_down_loss", grid=(t // tt,),
        in_specs=[_rows(tt, DFF), _rows(tt, D), _rows(tt, D), _full((8, D)), _full((DFF, D))],
        out_specs=[_rows(tt, D), _rows(tt, D), _full((8, D))],
        out_shape=[_sds((t, D)), _sds((t, D), MXU_DTYPE), _sds((8, D))],
        compiler_params=_cparams(1),
    )(hid, x1, tgt, vec, w_down)


def ffn_dhid(dff, gt, up, w_down, tt):
    t = dff.shape[0]

    def body(dff_ref, gt_ref, up_ref, w_ref, dgt_ref, dup_ref):
        dhid = _mm_nt(dff_ref[...], w_ref[...])
        gt = gt_ref[...]
        sg = _sigmoid(gt)
        dgt_ref[...] = (dhid * up_ref[...] * (sg * (1.0 + gt * (1.0 - sg)))).astype(MXU_DTYPE)
        dup_ref[...] = (dhid * (gt * sg)).astype(MXU_DTYPE)

    blk = pl.BlockSpec((tt, FN), lambda i, j: (i, j))
    return pl.pallas_call(
        body, name="ffn_dhid", grid=(t // tt, 2),
        in_specs=[pl.BlockSpec((tt, D), lambda i, j: (i, 0)), blk, blk, pl.BlockSpec((FN, D), lambda i, j: (j, 0))],
        out_specs=[blk, blk],
        out_shape=[_sds((t, DFF), MXU_DTYPE), _sds((t, DFF), MXU_DTYPE)],
        compiler_params=_cparams(2),
    )(dff, gt, up, w_down)


def ffn_dh2(dgt, dup, x1, dx2, vec, wg_t, wu_t, tt):
    t = x1.shape[0]

    def body(dgt_ref, dup_ref, x1_ref, dx2_ref, vec_ref, wg_ref, wu_ref, dx1_ref, acc_ref):
        first = pl.program_id(0) == 0
        dh2 = _mm(dgt_ref[...], wg_ref[...]) + _mm(dup_ref[...], wu_ref[...])
        g, sc = vec_ref[0:1, :], vec_ref[2:3, :]
        r, xn, yy, _ = _norm_mod(x1_ref[...], g, vec_ref[1:2, :], sc)
        dsh, dsc, dgn, dx = _norm_mod_bwd(dh2, r, xn, yy, g, sc)
        dx1_ref[...] = dx2_ref[...] + dx
        _acc_rows(acc_ref, first, [dsh, dsc, dgn])

    return pl.pallas_call(
        body, name="ffn_dh2", grid=(t // tt,),
        in_specs=[_rows(tt, DFF), _rows(tt, DFF), _rows(tt, D), _rows(tt, D), _full((8, D)), _full((DFF, D)),
                  _full((DFF, D))],
        out_specs=[_rows(tt, D), _full((8, D))],
        out_shape=[_sds((t, D)), _sds((8, D))],
        compiler_params=_cparams(1),
    )(dgt, dup, x1, dx2, vec, wg_t, wu_t)


def tn_matmul(a, b, bm, bt, init=None):
    t, m = a.shape
    n = b.shape[1]

    def body(*refs):
        if init is None:
            a_ref, b_ref, o_ref = refs
        else:
            a_ref, b_ref, i_ref, o_ref = refs
        prod = _mm_tn(a_ref[...], b_ref[...])

        @pl.when(pl.program_id(1) == 0)
        def _():
            o_ref[...] = prod if init is None else prod + i_ref[...]

        @pl.when(pl.program_id(1) != 0)
        def _():
            o_ref[...] += prod

    in_specs = [pl.BlockSpec((bt, bm), lambda i, k: (k, i)), pl.BlockSpec((bt, n), lambda i, k: (k, 0))]
    args = [a, b]
    if init is not None:
        in_specs.append(pl.BlockSpec((bm, n), lambda i, k: (i, 0)))
        args.append(init)
    return pl.pallas_call(
        body, name="tn_matmul", grid=(m // bm, t // bt),
        in_specs=in_specs, out_specs=pl.BlockSpec((bm, n), lambda i, k: (i, 0)),
        out_shape=_sds((m, n)), compiler_params=_cparams(2),
    )(*args)


def _adamw(w, g, m, v):
    m = ADAM_B1 * m + (1.0 - ADAM_B1) * g
    v = ADAM_B2 * v + (1.0 - ADAM_B2) * (g * g)
    m_hat = m / (1.0 - ADAM_B1 ** ADAM_STEP)
    v_hat = v / (1.0 - ADAM_B2 ** ADAM_STEP)
    delta = -ADAM_LR * (m_hat / (jnp.sqrt(v_hat) + ADAM_EPS) + ADAM_WD * w)
    return delta, m, v


def adamw_sharded(own, recv, w, m, v):
    shape = w.shape

    def body(own_ref, recv_ref, w_ref, m_ref, v_ref, g_ref, d_ref, mo_ref, vo_ref):
        g = own_ref[...]
        for k in range(N_DEV - 1):
            g = g + recv_ref[k]
        g_ref[...] = g
        d_ref[...], mo_ref[...], vo_ref[...] = _adamw(w_ref[...], g, m_ref[...], v_ref[...])

    return pl.pallas_call(
        body, name="adamw_sharded",
        in_specs=[_full(shape), _full((N_DEV - 1,) + shape), _full(shape), _full(shape), _full(shape)],
        out_specs=[_full(shape)] * 4, out_shape=[_sds(shape)] * 4, grid=(1,),
        compiler_params=_cparams(1),
    )(own, recv, w, m, v)


def adamw_small(items):
    n = len(items)
    flat = [a for it in items for a in it]

    def body(*refs):
        ins, outs = refs[:4 * n], refs[4 * n:]
        for i in range(n):
            g, w, m, v = (r[...] for r in ins[4 * i:4 * i + 4])
            outs[3 * i][...], outs[3 * i + 1][...], outs[3 * i + 2][...] = _adamw(w, g, m, v)

    out = pl.pallas_call(
        body, name="adamw_small", grid=(1,),
        in_specs=[_full(a.shape) for a in flat],
        out_specs=[_full(it[1].shape) for it in items for _ in range(3)],
        out_shape=[_sds(it[1].shape) for it in items for _ in range(3)],
        compiler_params=_cparams(1),
    )(*flat)
    return [tuple(out[3 * i:3 * i + 3]) for i in range(n)]


def _mesh_pos():
    x, y, c = lax.axis_index("x"), lax.axis_index("y"), lax.axis_index("c")
    me = 4 * x + 2 * y + c
    peers = []
    for k in range(1, N_DEV):
        peers.append(((1 - x) if (k >> 2) & 1 else x, (1 - y) if (k >> 1) & 1 else y, (1 - c) if k & 1 else c))
    return me, peers


def _all_gather(buf, send_sems, recv_sems, me, peers):
    sends = []
    for k, peer in enumerate(peers):
        cp = pltpu.make_async_remote_copy(src_ref=buf.at[me], dst_ref=buf.at[me], send_sem=send_sems.at[k],
                                          recv_sem=recv_sems.at[k], device_id=peer, device_id_type=MESH)
        cp.start()
        sends.append(cp)
    for k, peer in enumerate(peers):
        src = jnp.bitwise_xor(me, k + 1)
        pltpu.make_async_remote_copy(src_ref=buf.at[src], dst_ref=buf.at[src], send_sem=send_sems.at[k],
                                     recv_sem=recv_sems.at[k], device_id=peer, device_id_type=MESH).wait_recv()
    for cp in sends:
        cp.wait_send()


_VMEM = pl.BlockSpec(memory_space=pltpu.VMEM)
_ANY = pl.BlockSpec(memory_space=pl.ANY)
_SEMS = pltpu.SemaphoreType.DMA((N_DEV - 1,))


def mod_forward(c, c_ctx, w_mod_sh, b_mod):
    ncol = w_mod_sh.shape[1]

    def body(c_ref, cc_ref, w_ref, b_ref, mod_ref, s_ref, cbuf, pbuf, s1, r1, s2, r2):
        me, peers = _mesh_pos()
        cbuf[me] = jnp.broadcast_to(c_ref[...], (8, D))
        _all_gather(cbuf, s1, r1, me, peers)
        rows = [cbuf[j, 0:1, :] for j in range(N_DEV)] + [cc_ref[...], jnp.zeros((7, D), F32)]
        sx = jnp.concatenate(rows, axis=0)
        s = sx * _sigmoid(sx)
        s_ref[...] = s
        pbuf[me] = _hi(s, w_ref[...])
        _all_gather(pbuf, s2, r2, me, peers)
        for j in range(N_DEV):
            mod_ref[:, j * ncol:(j + 1) * ncol] = pbuf[j] + b_ref[:, j * ncol:(j + 1) * ncol]

    return pl.pallas_call(
        body, name="mod_forward",
        in_specs=[_VMEM] * 4, out_specs=[_VMEM] * 2,
        out_shape=[_sds((16, N_DEV * ncol)), _sds((16, D))],
        scratch_shapes=[pltpu.VMEM((N_DEV, 8, D), F32), pltpu.VMEM((N_DEV, 16, ncol), F32), _SEMS, _SEMS, _SEMS,
                        _SEMS],
        compiler_params=pltpu.CompilerParams(vmem_limit_bytes=VMEM_LIMIT),
    )(c, c_ctx, w_mod_sh, b_mod)


def gather_weights(wpack):
    rows = wpack.shape[0]

    def body(w_ref, out_ref, send_sems, recv_sems, local_sem):
        me, peers = _mesh_pos()
        mine = pltpu.make_async_copy(w_ref, out_ref.at[me], local_sem)
        mine.start()
        sends = []
        for k, peer in enumerate(peers):
            cp = pltpu.make_async_remote_copy(src_ref=w_ref, dst_ref=out_ref.at[me], send_sem=send_sems.at[k],
                                              recv_sem=recv_sems.at[k], device_id=peer, device_id_type=MESH)
            cp.start()
            sends.append(cp)
        for k, peer in enumerate(peers):
            src = jnp.bitwise_xor(me, k + 1)
            pltpu.make_async_remote_copy(src_ref=w_ref, dst_ref=out_ref.at[src], send_sem=send_sems.at[k],
                                         recv_sem=recv_sems.at[k], device_id=peer, device_id_type=MESH).wait_recv()
        for cp in sends:
            cp.wait_send()
        mine.wait()

    return pl.pallas_call(
        body, name="gather_weights", in_specs=[_ANY], out_specs=_ANY,
        out_shape=_sds((N_DEV, rows, D), wpack.dtype),
        scratch_shapes=[_SEMS, _SEMS, pltpu.SemaphoreType.DMA],
    )(wpack)


def gather_sum_small(pack):
    shape = pack.shape

    def body(p_ref, tot_ref, gat_ref, send_sems, recv_sems):
        me, peers = _mesh_pos()
        gat_ref[me] = p_ref[...]
        _all_gather(gat_ref, send_sems, recv_sems, me, peers)
        tot = gat_ref[0]
        for j in range(1, N_DEV):
            tot = tot + gat_ref[j]
        tot_ref[...] = tot

    return pl.pallas_call(
        body, name="gather_sum_small", in_specs=[_VMEM], out_specs=[_VMEM, _VMEM],
        out_shape=[_sds(shape), _sds((N_DEV,) + shape)], scratch_shapes=[_SEMS, _SEMS],
        compiler_params=pltpu.CompilerParams(vmem_limit_bytes=VMEM_LIMIT),
    )(pack)


def mod_backward(s, dm_sh, w, m, v, cc, m_cc, v_cc):
    shape = w.shape

    def body(s_ref, dm_ref, w_ref, m_ref, v_ref, cc_ref, mcc_ref, vcc_ref,
             gw_ref, dw_ref, mw_ref, vw_ref, gc_ref, dc_ref, mc_ref, vc_ref, pbuf, send_sems, recv_sems):
        me, peers = _mesh_pos()
        wv = w_ref[...]
        pbuf[me] = _hi_nt(dm_ref[8:16, :], wv)
        _all_gather(pbuf, send_sems, recv_sems, me, peers)
        g = _hi_tn(s_ref[...], dm_ref[...])
        gw_ref[...] = g
        dw_ref[...], mw_ref[...], vw_ref[...] = _adamw(wv, g, m_ref[...], v_ref[...])
        tot = pbuf[0]
        for j in range(1, N_DEV):
            tot = tot + pbuf[j]
        ccv = cc_ref[...]
        sg = _sigmoid(ccv)
        gc = tot[0:1, :] * (sg * (1.0 + ccv * (1.0 - sg)))
        gc_ref[...] = gc
        dc_ref[...], mc_ref[...], vc_ref[...] = _adamw(ccv, gc, mcc_ref[...], vcc_ref[...])

    return pl.pallas_call(
        body, name="mod_backward", in_specs=[_VMEM] * 8, out_specs=[_VMEM] * 8,
        out_shape=[_sds(shape)] * 4 + [_sds((1, D))] * 4,
        scratch_shapes=[pltpu.VMEM((N_DEV, 8, D), F32), _SEMS, _SEMS],
        compiler_params=pltpu.CompilerParams(vmem_limit_bytes=VMEM_LIMIT),
    )(s, dm_sh, w, m, v, cc, m_cc, v_cc)


def reduce_scatter(grads):
    n = len(grads)

    def body(*refs):
        g_refs, r_refs = refs[:n], refs[n:2 * n]
        send_sems, recv_sems = refs[2 * n], refs[2 * n + 1]
        me, peers = _mesh_pos()
        sends = []
        for w in range(n):
            for k, peer in enumerate(peers):
                dst = jnp.bitwise_xor(me, k + 1)
                cp = pltpu.make_async_remote_copy(
                    src_ref=g_refs[w].at[dst], dst_ref=r_refs[w].at[k], send_sem=send_sems.at[w * 7 + k],
                    recv_sem=recv_sems.at[w * 7 + k], device_id=peer, device_id_type=MESH)
                cp.start()
                sends.append(cp)
        for cp in sends:
            cp.wait_recv()
        for cp in sends:
            cp.wait_send()

    return pl.pallas_call(
        body, name="reduce_scatter", in_specs=[_ANY] * n, out_specs=[_ANY] * n,
        out_shape=[_sds((N_DEV - 1,) + g.shape[1:]) for g in grads],
        scratch_shapes=[pltpu.SemaphoreType.DMA((7 * n,)), pltpu.SemaphoreType.DMA((7 * n,))],
    )(*grads)


def _vec8(rows, width):
    return jnp.concatenate([r.reshape(1, width) for r in rows] + [jnp.zeros((8 - len(rows), width), F32)], axis=0)


def local_step(x, ctx, tgt, mod, mod_c, small, w_int, w_out, wg_t, wu_t, w_down, tt, tt_ctx, cb, cb_ctx):
    sh1, sc1, g1, sh2, sc2, g2 = [mod[i * D:(i + 1) * D] for i in range(6)]
    csh1, csc1 = mod_c[0:D], mod_c[D:2 * D]
    vec1 = _vec8([small["norm1_g"], sh1, sc1], D)
    vec1c = _vec8([small["norm1_g"], csh1, csc1], D)
    vec2 = _vec8([small["norm2_g"], sh2, sc2], D)
    vec3 = _vec8([g2, small["final_g"]], D)
    vecm = _vec8([g1], D)
    vcm = _vec8([jnp.tile(small["gla_norm_g"].reshape(HV), NH), small["conv_b"], small["conv_ln_g"],
                 small["conv_ln_b"]], DC)
    convw = jnp.concatenate([small["conv_w"], jnp.zeros((1, DC), F32)], axis=0)
    wa = jnp.zeros((128, 512), F32)
    wa = wa.at[0:RANK, 0:DK].set(small["w_a2_f"]).at[RANK:2 * RANK, DK:2 * DK].set(small["w_a2_b"])
    ba = jnp.concatenate([small["b_a_f"].reshape(1, DK), small["b_a_b"].reshape(1, DK)], axis=1)

    _, _, kc, vc_, _, rc, lac, hc = proj_fwd(ctx, vec1c, w_int, wa, ba, tt_ctx)
    qc0 = jnp.zeros_like(kc)
    _, sall_c, sfin_c = gla_fwd(qc0, kc, vc_, lac, jnp.zeros((2, DK, HV), F32), cb_ctx)
    u, q, k, v, g, r, la, h = proj_fwd(x, vec1, w_int, wa, ba, tt)
    o2, sall, _ = gla_fwd(q, k, v, la, sfin_c, cb)
    x1, cat, mix, yc = merge_fwd(u, g, o2, x, vecm, vcm, convw, w_out, tt)
    h2, gt, up, hid = ffn_gate_up(x1, vec2, wg_t, wu_t, tt)
    dx2, dff, acc3 = ffn_down_loss(hid, x1, tgt, vec3, w_down, tt)
    dgt, dup = ffn_dhid(dff, gt, up, w_down, tt)
    dx1, acc2 = ffn_dh2(dgt, dup, x1, dx2, vec2, wg_t, wu_t, tt)
    du, dg, do, dmix, accm1, accm2, dconvw = merge_bwd(dx1, mix, u, g, o2, yc, vecm, vcm, convw, w_out, tt)
    dq2, dk2, dv2, dla, ds0 = gla_bwd(q, k, v, la, do, sall, jnp.zeros((2, DK, HV), F32), cb)
    gx, dp, acc1, dba, dwa = proj_bwd(du, dq2, dk2, dv2, dg, dla, la, r, x, dx1, vec1, w_int, wa, tt)
    tcx = ctx.shape[0]
    zc = lambda w: jnp.zeros((tcx, w), F32)
    _, dk2c, dv2c, dlac, _ = gla_bwd(qc0, kc, vc_, lac, zc(DV), sall_c, ds0, cb_ctx)
    z2 = jnp.zeros((2, tcx, DK), F32)
    _, dpc, acc1c, dbac, dwac = proj_bwd(zc(1024), z2, dk2c, dv2c, zc(DV), dlac, lac, rc, ctx, zc(D), vec1c, w_int,
                                         wa, tt_ctx)
    bt = min(512, x.shape[0])
    btc = min(512, tcx)
    gw = {
        "w_down": tn_matmul(hid, dff, FN, bt),
        "wg_t": tn_matmul(dgt, h2, FN, bt),
        "wu_t": tn_matmul(dup, h2, FN, bt),
        "w_out": tn_matmul(cat, dmix, 512, bt),
        "w_int": tn_matmul(dp, h, 896, bt, init=tn_matmul(dpc, hc, 896, btc)),
    }
    dwa_t = dwa + dwac
    dba_t = dba + dbac
    gs = {
        "norm1_g": acc1[2] + acc1c[2], "norm2_g": acc2[2], "final_g": acc3[1], "loss": acc3[2],
        "gla_norm_g": accm2[0], "conv_b": accm2[1], "conv_ln_g": accm2[2], "conv_ln_b": accm2[3],
        "conv_w": dconvw, "b_a": dba_t[0], "w_a2": dwa_t,
    }
    dmod = jnp.concatenate([acc1[0], acc1[1], accm1[0], acc2[0], acc2[1], acc3[0]])
    dmod_c = jnp.concatenate([acc1c[0], acc1c[1], jnp.zeros((4 * D,), F32)])
    return gx, gw, gs, dmod, dmod_c


PACK_ROWS = 96
ROW_N1, ROW_N2, ROW_FG, ROW_LOSS, ROW_GN, ROW_CB, ROW_LG, ROW_LB, ROW_BA = 0, 1, 2, 3, 4, 5, 6, 7, 8
ROW_DMOD, ROW_DMODC, ROW_CW, ROW_WA = 9, 15, 24, 56


def _pack_small(gs, dmod, dmod_c):
    pad = lambda a: jnp.pad(a, ((0, 0), (0, D - a.shape[1])))
    rows = [gs["norm1_g"].reshape(1, D), gs["norm2_g"].reshape(1, D), gs["final_g"].reshape(1, D),
            gs["loss"].reshape(1, D), pad(gs["gla_norm_g"].reshape(1, DC)), pad(gs["conv_b"].reshape(1, DC)),
            pad(gs["conv_ln_g"].reshape(1, DC)), pad(gs["conv_ln_b"].reshape(1, DC)), pad(gs["b_a"].reshape(1, 512)),
            dmod.reshape(6, D), dmod_c.reshape(6, D), jnp.zeros((3, D), F32), pad(gs["conv_w"]),
            pad(gs["w_a2"][0:32]), jnp.zeros((8, D), F32)]
    return jnp.concatenate(rows, axis=0)


def kernel(x, c, ctx, c_ctx, w_mod, b_mod, norm1_g, norm2_g, w_in, conv_w, conv_b, conv_ln_g, conv_ln_b, w_a2_f, b_a_f, w_a2_b, b_a_b, gla_norm_g, w_out, w_gate, w_up, w_down, final_g, loss_target, m_c_ctx, m_w_mod, m_b_mod, m_norm1_g, m_norm2_g, m_w_in, m_conv_w, m_conv_b, m_conv_ln_g, m_conv_ln_b, m_w_a2_f, m_b_a_f, m_w_a2_b, m_b_a_b, m_gla_norm_g, m_w_out, m_w_gate, m_w_up, m_w_down, m_final_g, v_c_ctx, v_w_mod, v_b_mod, v_norm1_g, v_norm2_g, v_w_in, v_conv_w, v_conv_b, v_conv_ln_g, v_conv_ln_b, v_w_a2_f, v_b_a_f, v_w_a2_b, v_b_a_b, v_gla_norm_g, v_w_out, v_w_gate, v_w_up, v_w_down, v_final_g):
    me = 4 * lax.axis_index("x") + 2 * lax.axis_index("y") + lax.axis_index("c")
    t = x.shape[1]
    tcx = ctx.shape[1]
    r_in, r_out, r_ff = w_in.shape[2], w_out.shape[1], w_gate.shape[2]
    r_in_b = -(-r_in // 16) * 16
    r_in_f = -(-r_in // 8) * 8

    mod_all, s_all = mod_forward(c, c_ctx.reshape(1, D), w_mod[0], b_mod)
    mod = lax.dynamic_slice(mod_all, (me, 0), (1, 6 * D)).reshape(6 * D)
    mod_c = mod_all[8]

    tb = lambda w: w.T.astype(MXU_DTYPE)
    wpack = jnp.concatenate([jnp.pad(tb(w_in[0]), ((0, r_in_b - r_in), (0, 0))), w_out[0].astype(MXU_DTYPE),
                             tb(w_gate[0]), tb(w_up[0]), w_down[0].astype(MXU_DTYPE)], axis=0)
    wall = gather_weights(wpack)
    o1 = r_in_b
    o2_, o3, o4 = o1 + r_out, o1 + r_out + r_ff, o1 + r_out + 2 * r_ff
    w_int = jnp.pad(wall[:, 0:r_in, :].reshape(N_DEV * r_in, D), ((0, DINP - DIN), (0, 0)))
    w_out_f = wall[:, o1:o2_, :].reshape(N_DEV * r_out, D)
    wg_t = wall[:, o2_:o3, :].reshape(DFF, D)
    wu_t = wall[:, o3:o4, :].reshape(DFF, D)
    w_down_f = wall[:, o4:o4 + r_ff, :].reshape(DFF, D)

    small = dict(norm1_g=norm1_g[0], norm2_g=norm2_g[0], final_g=final_g, gla_norm_g=gla_norm_g[0],
                 conv_b=conv_b[0], conv_ln_g=conv_ln_g[0], conv_ln_b=conv_ln_b[0], b_a_f=b_a_f[0], b_a_b=b_a_b[0])
    sm_pack = jnp.zeros((PACK_ROWS, D), F32)
    sm_pack = lax.dynamic_update_slice(sm_pack, conv_w[0], (ROW_CW, me * (DC // N_DEV)))
    sm_pack = lax.dynamic_update_slice(sm_pack, w_a2_f[0], (ROW_WA, me * (DK // N_DEV)))
    sm_pack = lax.dynamic_update_slice(sm_pack, w_a2_b[0], (ROW_WA + RANK, DK + me * (DK // N_DEV)))
    sm_tot, _ = gather_sum_small(sm_pack)
    small["conv_w"] = sm_tot[ROW_CW:ROW_CW + CW, 0:DC]
    small["w_a2_f"] = sm_tot[ROW_WA:ROW_WA + RANK, 0:DK]
    small["w_a2_b"] = sm_tot[ROW_WA + RANK:ROW_WA + 2 * RANK, DK:2 * DK]

    gx, gw, gs, dmod, dmod_c = local_step(x[0], ctx[0], loss_target[0], mod, mod_c, small, w_int, w_out_f, wg_t,
                                          wu_t, w_down_f, 512, 256, 8, 4)

    tot, gat = gather_sum_small(_pack_small(gs, dmod, dmod_c))
    loss = jnp.sum(tot[ROW_LOSS])
    dm = jnp.concatenate([gat[:, ROW_DMOD:ROW_DMOD + 6, :].reshape(N_DEV, 6 * D),
                          tot[ROW_DMODC:ROW_DMODC + 6, :].reshape(1, 6 * D), jnp.zeros((7, 6 * D), F32)], axis=0)
    ncol = w_mod.shape[2]
    dm_sh = lax.dynamic_slice(dm, (0, me * ncol), (16, ncol))
    g_wmod, d_wmod, nm_wmod, nv_wmod, g_cc, d_cc, nm_cc, nv_cc = mod_backward(
        s_all, dm_sh, w_mod[0], m_w_mod[0], v_w_mod[0], c_ctx.reshape(1, D), m_c_ctx.reshape(1, D),
        v_c_ctx.reshape(1, D))

    g_in = jnp.pad(gw["w_int"][0:DIN].reshape(N_DEV, r_in, D), ((0, 0), (0, r_in_f - r_in), (0, 0)))
    g_list = [g_in, gw["w_out"].reshape(N_DEV, r_out, D), gw["wg_t"].reshape(N_DEV, r_ff, D),
              gw["wu_t"].reshape(N_DEV, r_ff, D), gw["w_down"].reshape(N_DEV, r_ff, D)]
    recv = reduce_scatter(g_list)
    own = [lax.dynamic_index_in_dim(g, me, 0, keepdims=False) for g in g_list]
    padt = lambda w: jnp.pad(w.T, ((0, r_in_f - r_in), (0, 0)))
    big = {}
    big["w_in"] = [a[0:r_in].T for a in adamw_sharded(own[0], recv[0], padt(w_in[0]), padt(m_w_in[0]),
                                                       padt(v_w_in[0]))]
    big["w_out"] = adamw_sharded(own[1], recv[1], w_out[0], m_w_out[0], v_w_out[0])
    big["w_gate"] = [a.T for a in adamw_sharded(own[2], recv[2], w_gate[0].T, m_w_gate[0].T, v_w_gate[0].T)]
    big["w_up"] = [a.T for a in adamw_sharded(own[3], recv[3], w_up[0].T, m_w_up[0].T, v_w_up[0].T)]
    big["w_down"] = adamw_sharded(own[4], recv[4], w_down[0], m_w_down[0], v_w_down[0])
    big["w_mod"] = [g_wmod, d_wmod, nm_wmod, nv_wmod]

    row = lambda r, w: tot[r:r + 1, 0:w]
    gn_row = tot[ROW_GN:ROW_GN + 1, 0:DC]
    g_small = {
        "b_mod": jnp.sum(dm, axis=0, keepdims=True),
        "norm1_g": row(ROW_N1, D), "norm2_g": row(ROW_N2, D),
        "conv_w": lax.dynamic_slice(tot, (ROW_CW, me * (DC // N_DEV)), (CW, DC // N_DEV)),
        "conv_b": row(ROW_CB, DC), "conv_ln_g": row(ROW_LG, DC), "conv_ln_b": row(ROW_LB, DC),
        "w_a2_f": lax.dynamic_slice(tot, (ROW_WA, me * (DK // N_DEV)), (RANK, DK // N_DEV)),
        "b_a_f": tot[ROW_BA:ROW_BA + 1, 0:DK],
        "w_a2_b": lax.dynamic_slice(tot, (ROW_WA + RANK, DK + me * (DK // N_DEV)), (RANK, DK // N_DEV)),
        "b_a_b": tot[ROW_BA:ROW_BA + 1, DK:2 * DK],
        "gla_norm_g": gn_row[:, 0:HV] + gn_row[:, HV:2 * HV] + gn_row[:, 2 * HV:3 * HV] + gn_row[:, 3 * HV:4 * HV],
        "final_g": row(ROW_FG, D),
    }
    wmv = {
        "b_mod": (b_mod, m_b_mod, v_b_mod), "norm1_g": (norm1_g, m_norm1_g, v_norm1_g),
        "norm2_g": (norm2_g, m_norm2_g, v_norm2_g), "conv_w": (conv_w[0], m_conv_w[0], v_conv_w[0]),
        "conv_b": (conv_b, m_conv_b, v_conv_b), "conv_ln_g": (conv_ln_g, m_conv_ln_g, v_conv_ln_g),
        "conv_ln_b": (conv_ln_b, m_conv_ln_b, v_conv_ln_b), "w_a2_f": (w_a2_f[0], m_w_a2_f[0], v_w_a2_f[0]),
        "b_a_f": (b_a_f, m_b_a_f, v_b_a_f), "w_a2_b": (w_a2_b[0], m_w_a2_b[0], v_w_a2_b[0]),
        "b_a_b": (b_a_b, m_b_a_b, v_b_a_b), "gla_norm_g": (gla_norm_g, m_gla_norm_g, v_gla_norm_g),
        "final_g": (final_g.reshape(1, D), m_final_g.reshape(1, D), v_final_g.reshape(1, D)),
    }
    names_small = list(g_small)
    upd = adamw_small([(g_small[n],) + wmv[n] for n in names_small])
    res = {n: (g_small[n],) + upd[i] for i, n in enumerate(names_small)}
    res["c_ctx"] = (g_cc, d_cc, nm_cc, nv_cc)
    for n in ("w_mod", "w_in", "w_out", "w_gate", "w_up", "w_down"):
        res[n] = tuple(big[n])

    order = ["c_ctx", "w_mod", "b_mod", "norm1_g", "norm2_g", "w_in", "conv_w", "conv_b", "conv_ln_g", "conv_ln_b",
             "w_a2_f", "b_a_f", "w_a2_b", "b_a_b", "gla_norm_g", "w_out", "w_gate", "w_up", "w_down", "final_g"]
    shapes = {"c_ctx": c_ctx.shape, "w_mod": w_mod.shape, "b_mod": b_mod.shape, "norm1_g": norm1_g.shape,
              "norm2_g": norm2_g.shape, "w_in": w_in.shape, "conv_w": conv_w.shape, "conv_b": conv_b.shape,
              "conv_ln_g": conv_ln_g.shape, "conv_ln_b": conv_ln_b.shape, "w_a2_f": w_a2_f.shape,
              "b_a_f": b_a_f.shape, "w_a2_b": w_a2_b.shape, "b_a_b": b_a_b.shape, "gla_norm_g": gla_norm_g.shape,
              "w_out": w_out.shape, "w_gate": w_gate.shape, "w_up": w_up.shape, "w_down": w_down.shape,
              "final_g": final_g.shape}
    outs = [loss, gx.reshape(x.shape)]
    for i in range(4):
        outs += [res[n][i].reshape(shapes[n]) for n in order]
    return tuple(outs)
```

```python
import functools

import jax
import jax.numpy as jnp
from jax import lax
from jax.experimental import pallas as pl
from jax.experimental.pallas import tpu as pltpu

F32 = jnp.float32
MXU_DTYPE = jnp.bfloat16
WIRE_DTYPE = jnp.bfloat16
HI = lax.Precision.HIGHEST
MESH = pl.DeviceIdType.MESH

N_DEV = 8
D = 1024
DC = 512
NH = 4
HK = 64
HV = 128
DK = NH * HK
DV = NH * HV
RANK = 16
CHUNK = 64
SEG = 64
CW = 31
CPAD = 15
DFF = 2816
DIN = 2592
DINP = 2688
TAU = 16.0
EPS = 1e-6
VMEM_LIMIT = 56 * 1024 * 1024

ADAM_LR = 0.001
ADAM_B1 = 0.9
ADAM_B2 = 0.999
ADAM_EPS = 1e-08
ADAM_WD = 0.01
ADAM_STEP = 10


def _mm(a, b):
    return jnp.dot(a.astype(MXU_DTYPE), b.astype(MXU_DTYPE), preferred_element_type=F32)


def _mm_nt(a, b):
    return lax.dot_general(a.astype(MXU_DTYPE), b.astype(MXU_DTYPE), (((1,), (1,)), ((), ())),
                           preferred_element_type=F32)


def _mm_tn(a, b):
    return lax.dot_general(a.astype(MXU_DTYPE), b.astype(MXU_DTYPE), (((0,), (0,)), ((), ())),
                           preferred_element_type=F32)


def _hi(a, b):
    return jnp.dot(a, b, precision=HI, preferred_element_type=F32)


def _hi_nt(a, b):
    return lax.dot_general(a, b, (((1,), (1,)), ((), ())), precision=HI, preferred_element_type=F32)


def _hi_tn(a, b):
    return lax.dot_general(a, b, (((0,), (0,)), ((), ())), precision=HI, preferred_element_type=F32)


def _sigmoid(x):
    return 1.0 / (1.0 + jnp.exp(-x))


def _cparams(n_axes):
    return pltpu.CompilerParams(dimension_semantics=("arbitrary",) * n_axes, vmem_limit_bytes=VMEM_LIMIT)


def _full(shape):
    n = len(shape)
    return pl.BlockSpec(shape, lambda *_: (0,) * n)


def _rows(tt, width):
    return pl.BlockSpec((tt, width), lambda i: (i, 0))


def _sds(shape, dtype=F32):
    return jax.ShapeDtypeStruct(shape, dtype)


def _norm_mod(x, g, sh, sc):
    r = lax.rsqrt(jnp.mean(x * x, axis=-1, keepdims=True) + EPS)
    xn = x * r
    yy = xn * g
    return r, xn, yy, yy * (1.0 + sc) + sh


def _norm_mod_bwd(dh, r, xn, yy, g, sc):
    dsh = jnp.sum(dh, axis=0, keepdims=True)
    dsc = jnp.sum(dh * yy, axis=0, keepdims=True)
    dy = dh * (1.0 + sc)
    dg = jnp.sum(dy * xn, axis=0, keepdims=True)
    dxn = dy * g
    dx = r * (dxn - xn * jnp.mean(dxn * xn, axis=-1, keepdims=True))
    return dsh, dsc, dg, dx


def _acc_rows(ref, first, rows):
    upd = jnp.concatenate(rows + [jnp.zeros((8 - len(rows), rows[0].shape[1]), F32)], axis=0)

    @pl.when(first)
    def _():
        ref[...] = upd

    @pl.when(jnp.logical_not(first))
    def _():
        ref[...] += upd


def _acc(ref, first, val):
    @pl.when(first)
    def _():
        ref[...] = val

    @pl.when(jnp.logical_not(first))
    def _():
        ref[...] += val


def proj_fwd(x, vec, w_int, wa, ba, tt):
    t = x.shape[0]

    def body(x_ref, vec_ref, w_ref, wa_ref, ba_ref, u_ref, q_ref, k_ref, v_ref, g_ref, r_ref, la_ref, h_ref):
        _, _, _, h = _norm_mod(x_ref[...], vec_ref[0:1, :], vec_ref[1:2, :], vec_ref[2:3, :])
        hb = h.astype(MXU_DTYPE)
        h_ref[...] = hb
        p = _mm_nt(hb, w_ref[...])
        u_ref[...] = p[:, 0:1024]
        q_ref[...] = p[:, 1024:1280]
        k_ref[...] = p[:, 1280:1536]
        v_ref[...] = p[:, 1536:2048]
        g_ref[...] = p[:, 2048:2560]
        rr = p[:, 2560:2688]
        r_ref[...] = rr
        z = _hi(rr, wa_ref[...]) + ba_ref[...]
        la_ref[...] = (jnp.minimum(z, 0.0) - jnp.log(1.0 + jnp.exp(-jnp.abs(z)))) * (1.0 / TAU)

    return pl.pallas_call(
        body, name="proj_fwd", grid=(t // tt,),
        in_specs=[_rows(tt, D), _full((8, D)), _full((DINP, D)), _full((128, 512)), _full((1, 512))],
        out_specs=[_rows(tt, 1024), _rows(tt, DK), _rows(tt, DK), _rows(tt, DV), _rows(tt, DV), _rows(tt, 128),
                   _rows(tt, 512), _rows(tt, D)],
        out_shape=[_sds((t, 1024)), _sds((t, DK)), _sds((t, DK)), _sds((t, DV)), _sds((t, DV)), _sds((t, 128)),
                   _sds((t, 512)), _sds((t, D), MXU_DTYPE)],
        compiler_params=_cparams(1),
    )(x, vec, w_int, wa, ba)


def proj_bwd(du, dq2, dk2, dv2, dg, dla, la, r, x, dx1, vec, w_int, wa, tt):
    t = x.shape[0]

    def body(du_ref, dq_ref, dk_ref, dv_ref, dg_ref, dla_ref, la_ref, r_ref, x_ref, dx1_ref, vec_ref, w_ref, wa_ref,
             gx_ref, dp_ref, acc_ref, dba_ref, dwa_ref):
        first = pl.program_id(0) == 0
        dz = dla_ref[...] * (1.0 - jnp.exp(TAU * la_ref[...])) * (1.0 / TAU)
        rr = r_ref[...]
        _acc_rows(dba_ref, first, [jnp.sum(dz, axis=0, keepdims=True)])
        _acc(dwa_ref, first, _hi_tn(rr, dz))
        dr = _hi_nt(dz, wa_ref[...])
        dp = jnp.concatenate([du_ref[...], dq_ref[0] + dq_ref[1], dk_ref[0] + dk_ref[1], dv_ref[0] + dv_ref[1],
                              dg_ref[...], dr], axis=1).astype(MXU_DTYPE)
        dp_ref[...] = dp
        dh = _mm(dp, w_ref[...])
        g, sc = vec_ref[0:1, :], vec_ref[2:3, :]
        rn, xn, yy, _ = _norm_mod(x_ref[...], g, vec_ref[1:2, :], sc)
        dsh, dsc, dgn, dx = _norm_mod_bwd(dh, rn, xn, yy, g, sc)
        gx_ref[...] = dx1_ref[...] + dx
        _acc_rows(acc_ref, first, [dsh, dsc, dgn])

    two = lambda w: pl.BlockSpec((2, tt, w), lambda i: (0, i, 0))
    return pl.pallas_call(
        body, name="proj_bwd", grid=(t // tt,),
        in_specs=[_rows(tt, 1024), two(DK), two(DK), two(DV), _rows(tt, DV), _rows(tt, 512), _rows(tt, 512),
                  _rows(tt, 128), _rows(tt, D), _rows(tt, D), _full((8, D)), _full((DINP, D)), _full((128, 512))],
        out_specs=[_rows(tt, D), _rows(tt, DINP), _full((8, D)), _full((8, 512)), _full((128, 512))],
        out_shape=[_sds((t, D)), _sds((t, DINP), MXU_DTYPE), _sds((8, D)), _sds((8, 512)), _sds((128, 512))],
        compiler_params=_cparams(1),
    )(du, dq2, dk2, dv2, dg, dla, la, r, x, dx1, vec, w_int, wa)


def _dot_exact01(m01, x):
    bf = jnp.bfloat16
    w = x.shape[1]
    hi = x.astype(bf)
    r1 = x - hi.astype(F32)
    mid = r1.astype(bf)
    lo = (r1 - mid.astype(F32)).astype(bf)
    y = jnp.dot(m01.astype(bf), jnp.concatenate([hi, mid, lo], axis=1), preferred_element_type=F32)
    return y[:, 0:w] + y[:, w:2 * w] + y[:, 2 * w:3 * w]


def _gla_chunk(d, qc, kc, la_c):
    row = lax.broadcasted_iota(jnp.int32, (CHUNK, CHUNK), 0)
    col = lax.broadcasted_iota(jnp.int32, (CHUNK, CHUNK), 1)
    cum = jnp.where(d == 0, (col <= row).astype(F32), (col >= row).astype(F32))
    cum_t = jnp.where(d == 0, (col >= row).astype(F32), (col <= row).astype(F32))
    cum4 = jnp.concatenate([cum] * NH, axis=0)
    head_of_lane = lax.broadcasted_iota(jnp.int32, (1, DK), 1) // HK
    b = _dot_exact01(cum, la_c)
    bl = jnp.sum(la_c, axis=0, keepdims=True)
    eb = jnp.exp(b)
    enb = jnp.exp(-b)
    ekd = jnp.exp(bl - b)
    qt = qc * (HK ** -0.5) * eb
    kt = kc * enb
    kd = kc * ekd
    qst = jnp.concatenate([jnp.where(head_of_lane == h, qt, 0.0) for h in range(NH)], axis=0)
    a = _mm_nt(qst, kt) * cum4
    return cum_t, cum4, head_of_lane, eb, enb, ekd, qt, kt, kd, qst, a, jnp.exp(bl)


def gla_fwd(q, k, v, la, s0, cb):
    t = q.shape[0]
    nc = t // CHUNK
    nb = nc // cb
    blk = lambda d, i: i + d * (nb - 1 - 2 * i)

    def body(q_ref, k_ref, v_ref, la_ref, s0_ref, o_ref, sall_ref, sfin_ref, s_scr):
        d = pl.program_id(0)
        i = pl.program_id(1)

        @pl.when(i == 0)
        def _():
            s_scr[...] = s0_ref[0]

        for j in range(cb):
            jj = j + d * (cb - 1 - 2 * j)
            rows = pl.ds(pl.multiple_of(jj * CHUNK, CHUNK), CHUNK)
            vc = v_ref[rows, :]
            _, _, head_of_lane, _, _, _, _, _, kd, qst, a, dec = _gla_chunk(
                d, q_ref[rows, :], k_ref[rows, :], la_ref[rows, :])
            s = s_scr[...]
            sall_ref[0, jj] = s
            inter = _mm_nt(qst, s)
            outs = []
            for h in range(NH):
                hs = slice(h * CHUNK, (h + 1) * CHUNK)
                outs.append(_mm(a[hs], vc[:, h * HV:(h + 1) * HV]) + inter[hs])
            o_ref[0, rows, :] = jnp.concatenate(outs, axis=1)
            kv = _mm_tn(vc, kd)
            s_new = dec * s
            for h in range(NH):
                s_new = s_new + jnp.where(head_of_lane == h, kv[h * HV:(h + 1) * HV], 0.0)
            s_scr[...] = s_new

        @pl.when(i == nb - 1)
        def _():
            sfin_ref[0] = s_scr[...]

    tb = cb * CHUNK
    return pl.pallas_call(
        body, name="gla_fwd", grid=(2, nb),
        in_specs=[pl.BlockSpec((tb, DK), lambda d, i: (blk(d, i), 0)),
                  pl.BlockSpec((tb, DK), lambda d, i: (blk(d, i), 0)),
                  pl.BlockSpec((tb, DV), lambda d, i: (blk(d, i), 0)),
                  pl.BlockSpec((tb, DK), lambda d, i: (blk(d, i), d)),
                  pl.BlockSpec((1, HV, DK), lambda d, i: (d, 0, 0))],
        out_specs=[pl.BlockSpec((1, tb, DV), lambda d, i: (d, blk(d, i), 0)),
                   pl.BlockSpec((1, cb, HV, DK), lambda d, i: (d, blk(d, i), 0, 0)),
                   pl.BlockSpec((1, HV, DK), lambda d, i: (d, 0, 0))],
        out_shape=[_sds((2, t, DV)), _sds((2, nc, HV, DK)), _sds((2, HV, DK))],
        scratch_shapes=[pltpu.VMEM((HV, DK), F32)],
        compiler_params=_cparams(2),
    )(q, k, v, la, s0)


def gla_bwd(q, k, v, la, do, sall, dsfin, cb):
    t = q.shape[0]
    nc = t // CHUNK
    nb = nc // cb
    blk = lambda d, i: (nb - 1 - i) + d * (2 * i - (nb - 1))

    def body(q_ref, k_ref, v_ref, la_ref, do_ref, sall_ref, dsfin_ref, dq_ref, dk_ref, dv_ref, dla_ref, ds0_ref,
             ds_scr):
        d = pl.program_id(0)
        i = pl.program_id(1)

        @pl.when(i == 0)
        def _():
            ds_scr[...] = dsfin_ref[0]

        for j in range(cb):
            jj = (cb - 1 - j) + d * (2 * j - (cb - 1))
            rows = pl.ds(pl.multiple_of(jj * CHUNK, CHUNK), CHUNK)
            vc = v_ref[rows, :]
            doc = do_ref[rows, :]
            cum_t, cum4, head_of_lane, eb, enb, ekd, qt, kt, kd, qst, a, dec = _gla_chunk(
                d, q_ref[rows, :], k_ref[rows, :], la_ref[rows, :])
            s = sall_ref[0, jj]
            ds = ds_scr[...]
            hv = lambda x, h: x[:, h * HV:(h + 1) * HV]
            hr = lambda x, h: x[h * CHUNK:(h + 1) * CHUNK]
            fold = lambda x: functools.reduce(
                lambda p, c: p + c, [jnp.where(head_of_lane == h, hr(x, h), 0.0) for h in range(NH)])
            dost = jnp.concatenate([hv(doc, h) for h in range(NH)], axis=0)
            vst = jnp.concatenate([hv(vc, h) for h in range(NH)], axis=0)
            da = jnp.concatenate([_mm_nt(hv(doc, h), hv(vc, h)) for h in range(NH)], axis=0) * cum4
            dqt = fold(_mm(da, kt) + _mm(dost, s))
            dkt = _mm_tn(da, qst)
            kdst = jnp.concatenate([jnp.where(head_of_lane == h, kd, 0.0) for h in range(NH)], axis=0)
            dv_inter = _mm_nt(kdst, ds)
            dv_ref[0, rows, :] = jnp.concatenate(
                [_mm_tn(hr(a, h), hv(doc, h)) + hr(dv_inter, h) for h in range(NH)], axis=1)
            dkd = fold(_mm(vst, ds))
            ds_scr[...] = dec * ds + _mm_tn(dost, qst)
            tkd = dkd * kd
            db = dqt * qt - dkt * kt - tkd
            dbl = jnp.sum(ds * s, axis=0, keepdims=True) * dec + jnp.sum(tkd, axis=0, keepdims=True)
            dla_ref[rows, :] = _dot_exact01(cum_t, db) + dbl
            dq_ref[0, rows, :] = dqt * eb * (HK ** -0.5)
            dk_ref[0, rows, :] = dkt * enb + dkd * ekd

        @pl.when(i == nb - 1)
        def _():
            ds0_ref[0] = ds_scr[...]

    tb = cb * CHUNK
    return pl.pallas_call(
        body, name="gla_bwd", grid=(2, nb),
        in_specs=[pl.BlockSpec((tb, DK), lambda d, i: (blk(d, i), 0)),
                  pl.BlockSpec((tb, DK), lambda d, i: (blk(d, i), 0)),
                  pl.BlockSpec((tb, DV), lambda d, i: (blk(d, i), 0)),
                  pl.BlockSpec((tb, DK), lambda d, i: (blk(d, i), d)),
                  pl.BlockSpec((tb, DV), lambda d, i: (blk(d, i), 0)),
                  pl.BlockSpec((1, cb, HV, DK), lambda d, i: (d, blk(d, i), 0, 0)),
                  pl.BlockSpec((1, HV, DK), lambda d, i: (d, 0, 0))],
        out_specs=[pl.BlockSpec((1, tb, DK), lambda d, i: (d, blk(d, i), 0)),
                   pl.BlockSpec((1, tb, DK), lambda d, i: (d, blk(d, i), 0)),
                   pl.BlockSpec((1, tb, DV), lambda d, i: (d, blk(d, i), 0)),
                   pl.BlockSpec((tb, DK), lambda d, i: (blk(d, i), d)),
                   pl.BlockSpec((1, HV, DK), lambda d, i: (d, 0, 0))],
        out_shape=[_sds((2, t, DK)), _sds((2, t, DK)), _sds((2, t, DV)), _sds((t, 2 * DK)), _sds((2, HV, DK))],
        scratch_shapes=[pltpu.VMEM((HV, DK), F32)],
        compiler_params=_cparams(2),
    )(q, k, v, la, do, sall, dsfin)


def _seg_pos(tt):
    return lax.broadcasted_iota(jnp.int32, (tt, 1), 0) % SEG


def _shifted(x, s, pos, tt):
    y = x if s == 0 else pltpu.roll(x, (-s) % tt, 0)
    return jnp.where((pos + s >= 0) & (pos + s < SEG), y, 0.0)


def _head_norm(o, gn):
    rs, xs = [], []
    for h in range(NH):
        oh = o[:, h * HV:(h + 1) * HV]
        r = lax.rsqrt(jnp.mean(oh * oh, axis=-1, keepdims=True) + EPS)
        rs.append(r)
        xs.append(oh * r)
    return rs, xs


def merge_fwd(u, g, o2, x, vec, vc, convw, w_out, tt):
    t = x.shape[0]

    def body(u_ref, g_ref, o_ref, x_ref, vec_ref, vc_ref, cw_ref, w_ref, x1_ref, cat_ref, mix_ref, yc_ref, h2_ref):
        a = u_ref[:, 0:DC]
        gate = u_ref[:, DC:2 * DC]
        vv = a * _sigmoid(gate)
        pos = _seg_pos(tt)
        cw = cw_ref[...]
        yc = jnp.zeros((tt, DC), F32) + vc_ref[1:2, :]
        for j in range(CW):
            yc = yc + _shifted(vv, j - CPAD, pos, tt) * cw[j:j + 1, :]
        yc_ref[...] = yc
        mu = jnp.mean(yc, axis=-1, keepdims=True)
        yd = yc - mu
        rs = lax.rsqrt(jnp.mean(yd * yd, axis=-1, keepdims=True) + EPS)
        ln = yd * rs * vc_ref[2:3, :] + vc_ref[3:4, :]
        conv_o = ln * _sigmoid(ln)
        o = o_ref[0] + o_ref[1]
        _, xs = _head_norm(o, None)
        gg = g_ref[...]
        o2g = jnp.concatenate(xs, axis=1) * vc_ref[0:1, :] * (gg * _sigmoid(gg))
        cat = jnp.concatenate([conv_o, o2g], axis=1).astype(MXU_DTYPE)
        cat_ref[...] = cat
        mix = _mm(cat, w_ref[...])
        mix_ref[...] = mix
        x1 = x_ref[...] + vec_ref[0:1, :] * mix
        x1_ref[...] = x1
        _, _, _, h2 = _norm_mod(x1, vec_ref[1:2, :], vec_ref[2:3, :], vec_ref[3:4, :])
        h2_ref[...] = h2.astype(MXU_DTYPE)

    return pl.pallas_call(
        body, name="merge_fwd", grid=(t // tt,),
        in_specs=[_rows(tt, 1024), _rows(tt, DV), pl.BlockSpec((2, tt, DV), lambda i: (0, i, 0)), _rows(tt, D),
                  _full((8, D)), _full((8, DC)), _full((32, DC)), _full((D, D))],
        out_specs=[_rows(tt, D), _rows(tt, D), _rows(tt, D), _rows(tt, DC), _rows(tt, D)],
        out_shape=[_sds((t, D)), _sds((t, D), MXU_DTYPE), _sds((t, D)), _sds((t, DC)), _sds((t, D), MXU_DTYPE)],
        compiler_params=_cparams(1),
    )(u, g, o2, x, vec, vc, convw, w_out)


def merge_bwd(dx1, mix, u, g, o2, yc, vec, vc, convw, w_out, tt):
    t = dx1.shape[0]

    def body(dx1_ref, mix_ref, u_ref, g_ref, o_ref, yc_ref, vec_ref, vc_ref, cw_ref, w_ref,
             du_ref, dg_ref, do_ref, dmix_ref, acc1_ref, acc2_ref, dcw_ref):
        first = pl.program_id(0) == 0
        dx1v = dx1_ref[...]
        dg1 = jnp.sum(dx1v * mix_ref[...], axis=0, keepdims=True)
        dmix = (vec_ref[0:1, :] * dx1v).astype(MXU_DTYPE)
        dmix_ref[...] = dmix
        dcat = _mm_nt(dmix, w_ref[...])
        dconv_o = dcat[:, 0:DC]
        do2 = dcat[:, DC:2 * DC]
        gn = vc_ref[0:1, :]
        o = o_ref[0] + o_ref[1]
        rs, xs = _head_norm(o, None)
        xn = jnp.concatenate(xs, axis=1)
        gg = g_ref[...]
        sg = _sigmoid(gg)
        don = do2 * (gg * sg)
        dg_ref[...] = do2 * (xn * gn) * (sg * (1.0 + gg * (1.0 - sg)))
        dgn = jnp.sum(don * xn, axis=0, keepdims=True)
        dxn = don * gn
        dos = []
        for h in range(NH):
            dh = dxn[:, h * HV:(h + 1) * HV]
            dos.append(rs[h] * (dh - xs[h] * jnp.mean(dh * xs[h], axis=-1, keepdims=True)))
        do_ref[...] = jnp.concatenate(dos, axis=1)
        yc = yc_ref[...]
        mu = jnp.mean(yc, axis=-1, keepdims=True)
        yd = yc - mu
        rstd = lax.rsqrt(jnp.mean(yd * yd, axis=-1, keepdims=True) + EPS)
        yhat = yd * rstd
        lg = vc_ref[2:3, :]
        ln = yhat * lg + vc_ref[3:4, :]
        sl = _sigmoid(ln)
        dln = dconv_o * (sl * (1.0 + ln * (1.0 - sl)))
        dlb = jnp.sum(dln, axis=0, keepdims=True)
        dlg = jnp.sum(dln * yhat, axis=0, keepdims=True)
        dyh = dln * lg
        dyc = rstd * (dyh - jnp.mean(dyh, axis=-1, keepdims=True)
                      - yhat * jnp.mean(dyh * yhat, axis=-1, keepdims=True))
        dcb = jnp.sum(dyc, axis=0, keepdims=True)
        a = u_ref[:, 0:DC]
        gate = u_ref[:, DC:2 * DC]
        sgt = _sigmoid(gate)
        vv = a * sgt
        pos = _seg_pos(tt)
        cw = cw_ref[...]
        dvv = jnp.zeros((tt, DC), F32)
        dws = []
        for j in range(CW):
            s = j - CPAD
            dvv = dvv + _shifted(dyc, -s, pos, tt) * cw[j:j + 1, :]
            dws.append(jnp.sum(dyc * _shifted(vv, s, pos, tt), axis=0, keepdims=True))
        dws.append(jnp.zeros((1, DC), F32))
        du_ref[:, 0:DC] = dvv * sgt
        du_ref[:, DC:2 * DC] = dvv * a * sgt * (1.0 - sgt)
        _acc_rows(acc1_ref, first, [dg1])
        _acc_rows(acc2_ref, first, [dgn, dcb, dlg, dlb])
        _acc(dcw_ref, first, jnp.concatenate(dws, axis=0))

    return pl.pallas_call(
        body, name="merge_bwd", grid=(t // tt,),
        in_specs=[_rows(tt, D), _rows(tt, D), _rows(tt, 1024), _rows(tt, DV),
                  pl.BlockSpec((2, tt, DV), lambda i: (0, i, 0)), _rows(tt, DC),
                  _full((8, D)), _full((8, DC)), _full((32, DC)), _full((D, D))],
        out_specs=[_rows(tt, 1024), _rows(tt, DV), _rows(tt, DV), _rows(tt, D), _full((8, D)), _full((8, DC)),
                   _full((32, DC))],
        out_shape=[_sds((t, 1024)), _sds((t, DV)), _sds((t, DV)), _sds((t, D), MXU_DTYPE), _sds((8, D)),
                   _sds((8, DC)), _sds((32, DC))],
        compiler_params=_cparams(1),
    )(dx1, mix, u, g, o2, yc, vec, vc, convw, w_out)


FN = DFF // 2


def ffn_gate_up(h2, wg_t, wu_t, tt):
    t = h2.shape[0]

    def body(h2_ref, wg_ref, wu_ref, gt_ref, up_ref, hid_ref):
        h2v = h2_ref[...]
        gt = _mm_nt(h2v, wg_ref[...])
        up = _mm_nt(h2v, wu_ref[...])
        gt_ref[...] = gt.astype(MXU_DTYPE)
        up_ref[...] = up.astype(MXU_DTYPE)
        hid_ref[...] = (gt * _sigmoid(gt) * up).astype(MXU_DTYPE)

    blk = pl.BlockSpec((tt, FN), lambda j, i: (i, j))
    wblk = pl.BlockSpec((FN, D), lambda j, i: (j, 0))
    return pl.pallas_call(
        body, name="ffn_gate_up", grid=(2, t // tt),
        in_specs=[pl.BlockSpec((tt, D), lambda j, i: (i, 0)), wblk, wblk],
        out_specs=[blk, blk, blk],
        out_shape=[_sds((t, DFF), MXU_DTYPE)] * 3,
        compiler_params=_cparams(2),
    )(h2, wg_t, wu_t)


def ffn_down_loss(hid, x1, tgt, vec, w_down, tt):
    t = x1.shape[0]

    def body(hid_ref, x1_ref, tgt_ref, vec_ref, w_ref, dx2_ref, dff_ref, acc_ref):
        first = pl.program_id(0) == 0
        g2 = vec_ref[0:1, :]
        fg = vec_ref[1:2, :]
        ff = _mm(hid_ref[...], w_ref[...])
        x2 = x1_ref[...] + g2 * ff
        rf = lax.rsqrt(jnp.mean(x2 * x2, axis=-1, keepdims=True) + EPS)
        xn = x2 * rf
        err = xn * fg - tgt_ref[...]
        dy = err * (1.0 / D)
        dfg = jnp.sum(dy * xn, axis=0, keepdims=True)
        dxn = dy * fg
        dx2 = rf * (dxn - xn * jnp.mean(dxn * xn, axis=-1, keepdims=True))
        dx2_ref[...] = dx2
        dff_ref[...] = (g2 * dx2).astype(MXU_DTYPE)
        dg2 = jnp.sum(dx2 * ff, axis=0, keepdims=True)
        loss = jnp.sum(err * err, axis=0, keepdims=True) * (0.5 / D)
        _acc_rows(acc_ref, first, [dg2, dfg, loss])

    return pl.pallas_call(
        body, name="ffn_down_loss", grid=(t // tt,),
        in_specs=[_rows(tt, DFF), _rows(tt, D), _rows(tt, D), _full((8, D)), _full((DFF, D))],
        out_specs=[_rows(tt, D), _rows(tt, D), _full((8, D))],
        out_shape=[_sds((t, D)), _sds((t, D), MXU_DTYPE), _sds((8, D))],
        compiler_params=_cparams(1),
    )(hid, x1, tgt, vec, w_down)


def ffn_dhid(dff, gt, up, w_down, tt):
    t = dff.shape[0]

    def body(dff_ref, gt_ref, up_ref, w_ref, dgt_ref, dup_ref):
        dhid = _mm_nt(dff_ref[...], w_ref[...])
        gt = gt_ref[...].astype(F32)
        sg = _sigmoid(gt)
        dgt_ref[...] = (dhid * up_ref[...].astype(F32) * (sg * (1.0 + gt * (1.0 - sg)))).astype(MXU_DTYPE)
        dup_ref[...] = (dhid * (gt * sg)).astype(MXU_DTYPE)

    blk = pl.BlockSpec((tt, FN), lambda j, i: (i, j))
    return pl.pallas_call(
        body, name="ffn_dhid", grid=(2, t // tt),
        in_specs=[pl.BlockSpec((tt, D), lambda j, i: (i, 0)), blk, blk, pl.BlockSpec((FN, D), lambda j, i: (j, 0))],
        out_specs=[blk, blk],
        out_shape=[_sds((t, DFF), MXU_DTYPE), _sds((t, DFF), MXU_DTYPE)],
        compiler_params=_cparams(2),
    )(dff, gt, up, w_down)


def ffn_dh2(dgt, dup, x1, dx2, vec, wg_t, wu_t, tt):
    t = x1.shape[0]

    def body(dgt_ref, dup_ref, x1_ref, dx2_ref, vec_ref, wg_ref, wu_ref, dx1_ref, acc_ref):
        first = pl.program_id(0) == 0
        dh2 = _mm(dgt_ref[...], wg_ref[...]) + _mm(dup_ref[...], wu_ref[...])
        g, sc = vec_ref[0:1, :], vec_ref[2:3, :]
        r, xn, yy, _ = _norm_mod(x1_ref[...], g, vec_ref[1:2, :], sc)
        dsh, dsc, dgn, dx = _norm_mod_bwd(dh2, r, xn, yy, g, sc)
        dx1_ref[...] = dx2_ref[...] + dx
        _acc_rows(acc_ref, first, [dsh, dsc, dgn])

    return pl.pallas_call(
        body, name="ffn_dh2", grid=(t // tt,),
        in_specs=[_rows(tt, DFF), _rows(tt, DFF), _rows(tt, D), _rows(tt, D), _full((8, D)), _full((DFF, D)),
                  _full((DFF, D))],
        out_specs=[_rows(tt, D), _full((8, D))],
        out_shape=[_sds((t, D)), _sds((8, D))],
        compiler_params=_cparams(1),
    )(dgt, dup, x1, dx2, vec, wg_t, wu_t)


def tn_matmul(a, b, bm, bt, init=None):
    t, m = a.shape
    n = b.shape[1]
    nk = t // bt

    def body(*refs):
        if init is None:
            a_ref, b_ref, o_ref, wire_ref = refs
        else:
            a_ref, b_ref, i_ref, o_ref, wire_ref = refs
        prod = _mm_tn(a_ref[...], b_ref[...])

        @pl.when(pl.program_id(1) == 0)
        def _():
            o_ref[...] = prod if init is None else prod + i_ref[...]

        @pl.when(pl.program_id(1) != 0)
        def _():
            o_ref[...] += prod

        @pl.when(pl.program_id(1) == nk - 1)
        def _():
            wire_ref[...] = o_ref[...].astype(WIRE_DTYPE)

    in_specs = [pl.BlockSpec((bt, bm), lambda i, k: (k, i)), pl.BlockSpec((bt, n), lambda i, k: (k, 0))]
    args = [a, b]
    if init is not None:
        in_specs.append(pl.BlockSpec((bm, n), lambda i, k: (i, 0)))
        args.append(init)
    oblk = pl.BlockSpec((bm, n), lambda i, k: (i, 0))
    return pl.pallas_call(
        body, name="tn_matmul", grid=(m // bm, nk),
        in_specs=in_specs, out_specs=[oblk, oblk],
        out_shape=[_sds((m, n)), _sds((m, n), WIRE_DTYPE)], compiler_params=_cparams(2),
    )(*args)


def _adamw(w, g, m, v):
    m = ADAM_B1 * m + (1.0 - ADAM_B1) * g
    v = ADAM_B2 * v + (1.0 - ADAM_B2) * (g * g)
    m_hat = m / (1.0 - ADAM_B1 ** ADAM_STEP)
    v_hat = v / (1.0 - ADAM_B2 ** ADAM_STEP)
    delta = -ADAM_LR * (m_hat / (jnp.sqrt(v_hat) + ADAM_EPS) + ADAM_WD * w)
    return delta, m, v


def adamw_sharded(own, recv, w, m, v):
    shape = w.shape

    def body(own_ref, recv_ref, w_ref, m_ref, v_ref, g_ref, d_ref, mo_ref, vo_ref):
        g = own_ref[...]
        for k in range(N_DEV - 1):
            g = g + recv_ref[k].astype(F32)
        g_ref[...] = g
        d_ref[...], mo_ref[...], vo_ref[...] = _adamw(w_ref[...], g, m_ref[...], v_ref[...])

    return pl.pallas_call(
        body, name="adamw_sharded",
        in_specs=[_full(shape), _full((N_DEV - 1,) + shape), _full(shape), _full(shape), _full(shape)],
        out_specs=[_full(shape)] * 4, out_shape=[_sds(shape)] * 4, grid=(1,),
        compiler_params=_cparams(1),
    )(own, recv, w, m, v)


def adamw_small(items):
    n = len(items)
    flat = [a for it in items for a in it]

    def body(*refs):
        ins, outs = refs[:4 * n], refs[4 * n:]
        for i in range(n):
            g, w, m, v = (r[...] for r in ins[4 * i:4 * i + 4])
            outs[3 * i][...], outs[3 * i + 1][...], outs[3 * i + 2][...] = _adamw(w, g, m, v)

    out = pl.pallas_call(
        body, name="adamw_small", grid=(1,),
        in_specs=[_full(a.shape) for a in flat],
        out_specs=[_full(it[1].shape) for it in items for _ in range(3)],
        out_shape=[_sds(it[1].shape) for it in items for _ in range(3)],
        compiler_params=_cparams(1),
    )(*flat)
    return [tuple(out[3 * i:3 * i + 3]) for i in range(n)]


def _mesh_pos():
    x, y, c = lax.axis_index("x"), lax.axis_index("y"), lax.axis_index("c")
    me = 4 * x + 2 * y + c
    peers = []
    for k in range(1, N_DEV):
        peers.append(((1 - x) if (k >> 2) & 1 else x, (1 - y) if (k >> 1) & 1 else y, (1 - c) if k & 1 else c))
    return me, peers


def _all_gather(buf, send_sems, recv_sems, me, peers):
    sends = []
    for k, peer in enumerate(peers):
        cp = pltpu.make_async_remote_copy(src_ref=buf.at[me], dst_ref=buf.at[me], send_sem=send_sems.at[k],
                                          recv_sem=recv_sems.at[k], device_id=peer, device_id_type=MESH)
        cp.start()
        sends.append(cp)
    for k, peer in enumerate(peers):
        src = jnp.bitwise_xor(me, k + 1)
        pltpu.make_async_remote_copy(src_ref=buf.at[src], dst_ref=buf.at[src], send_sem=send_sems.at[k],
                                     recv_sem=recv_sems.at[k], device_id=peer, device_id_type=MESH).wait_recv()
    for cp in sends:
        cp.wait_send()


_VMEM = pl.BlockSpec(memory_space=pltpu.VMEM)
_ANY = pl.BlockSpec(memory_space=pl.ANY)
_SEMS = pltpu.SemaphoreType.DMA((N_DEV - 1,))


def mod_forward(c, c_ctx, w_mod_sh, b_mod):
    ncol = w_mod_sh.shape[1]

    def body(c_ref, cc_ref, w_ref, b_ref, mod_ref, s_ref, cbuf, pbuf, s1, r1, s2, r2):
        me, peers = _mesh_pos()
        cbuf[me] = jnp.broadcast_to(c_ref[...], (8, D))
        _all_gather(cbuf, s1, r1, me, peers)
        rows = [cbuf[j, 0:1, :] for j in range(N_DEV)] + [cc_ref[...], jnp.zeros((7, D), F32)]
        sx = jnp.concatenate(rows, axis=0)
        s = sx * _sigmoid(sx)
        s_ref[...] = s
        pbuf[me] = _hi(s, w_ref[...])
        _all_gather(pbuf, s2, r2, me, peers)
        for j in range(N_DEV):
            mod_ref[:, j * ncol:(j + 1) * ncol] = pbuf[j] + b_ref[:, j * ncol:(j + 1) * ncol]

    return pl.pallas_call(
        body, name="mod_forward",
        in_specs=[_VMEM] * 4, out_specs=[_VMEM] * 2,
        out_shape=[_sds((16, N_DEV * ncol)), _sds((16, D))],
        scratch_shapes=[pltpu.VMEM((N_DEV, 8, D), F32), pltpu.VMEM((N_DEV, 16, ncol), F32), _SEMS, _SEMS, _SEMS,
                        _SEMS],
        compiler_params=pltpu.CompilerParams(vmem_limit_bytes=VMEM_LIMIT),
    )(c, c_ctx, w_mod_sh, b_mod)


def gather_weights(wpack):
    rows = wpack.shape[0]

    def body(w_ref, out_ref, send_sems, recv_sems, local_sem):
        me, peers = _mesh_pos()
        mine = pltpu.make_async_copy(w_ref, out_ref.at[me], local_sem)
        mine.start()
        sends = []
        for k, peer in enumerate(peers):
            cp = pltpu.make_async_remote_copy(src_ref=w_ref, dst_ref=out_ref.at[me], send_sem=send_sems.at[k],
                                              recv_sem=recv_sems.at[k], device_id=peer, device_id_type=MESH)
            cp.start()
            sends.append(cp)
        for k, peer in enumerate(peers):
            src = jnp.bitwise_xor(me, k + 1)
            pltpu.make_async_remote_copy(src_ref=w_ref, dst_ref=out_ref.at[src], send_sem=send_sems.at[k],
                                         recv_sem=recv_sems.at[k], device_id=peer, device_id_type=MESH).wait_recv()
        for cp in sends:
            cp.wait_send()
        mine.wait()

    return pl.pallas_call(
        body, name="gather_weights", in_specs=[_ANY], out_specs=_ANY,
        out_shape=_sds((N_DEV, rows, D), wpack.dtype),
        scratch_shapes=[_SEMS, _SEMS, pltpu.SemaphoreType.DMA],
    )(wpack)


def gather_sum_small(pack):
    shape = pack.shape

    def body(p_ref, tot_ref, gat_ref, send_sems, recv_sems):
        me, peers = _mesh_pos()
        gat_ref[me] = p_ref[...]
        _all_gather(gat_ref, send_sems, recv_sems, me, peers)
        tot = gat_ref[0]
        for j in range(1, N_DEV):
            tot = tot + gat_ref[j]
        tot_ref[...] = tot

    return pl.pallas_call(
        body, name="gather_sum_small", in_specs=[_VMEM], out_specs=[_VMEM, _VMEM],
        out_shape=[_sds(shape), _sds((N_DEV,) + shape)], scratch_shapes=[_SEMS, _SEMS],
        compiler_params=pltpu.CompilerParams(vmem_limit_bytes=VMEM_LIMIT),
    )(pack)


def mod_backward(s, dm_sh, w, m, v, cc, m_cc, v_cc):
    shape = w.shape

    def body(s_ref, dm_ref, w_ref, m_ref, v_ref, cc_ref, mcc_ref, vcc_ref,
             gw_ref, dw_ref, mw_ref, vw_ref, gc_ref, dc_ref, mc_ref, vc_ref, pbuf, send_sems, recv_sems):
        me, peers = _mesh_pos()
        wv = w_ref[...]
        pbuf[me] = _hi_nt(dm_ref[8:16, :], wv)
        _all_gather(pbuf, send_sems, recv_sems, me, peers)
        g = _hi_tn(s_ref[...], dm_ref[...])
        gw_ref[...] = g
        dw_ref[...], mw_ref[...], vw_ref[...] = _adamw(wv, g, m_ref[...], v_ref[...])
        tot = pbuf[0]
        for j in range(1, N_DEV):
            tot = tot + pbuf[j]
        ccv = cc_ref[...]
        sg = _sigmoid(ccv)
        gc = tot[0:1, :] * (sg * (1.0 + ccv * (1.0 - sg)))
        gc_ref[...] = gc
        dc_ref[...], mc_ref[...], vc_ref[...] = _adamw(ccv, gc, mcc_ref[...], vcc_ref[...])

    return pl.pallas_call(
        body, name="mod_backward", in_specs=[_VMEM] * 8, out_specs=[_VMEM] * 8,
        out_shape=[_sds(shape)] * 4 + [_sds((1, D))] * 4,
        scratch_shapes=[pltpu.VMEM((N_DEV, 8, D), F32), _SEMS, _SEMS],
        compiler_params=pltpu.CompilerParams(vmem_limit_bytes=VMEM_LIMIT),
    )(s, dm_sh, w, m, v, cc, m_cc, v_cc)


def reduce_scatter(grads):
    n = len(grads)

    def body(*refs):
        g_refs, r_refs = refs[:n], refs[n:2 * n]
        send_sems, recv_sems = refs[2 * n], refs[2 * n + 1]
        me, peers = _mesh_pos()
        sends = []
        for w in range(n):
            for k, peer in enumerate(peers):
                dst = jnp.bitwise_xor(me, k + 1)
                cp = pltpu.make_async_remote_copy(
                    src_ref=g_refs[w].at[dst], dst_ref=r_refs[w].at[k], send_sem=send_sems.at[w * 7 + k],
                    recv_sem=recv_sems.at[w * 7 + k], device_id=peer, device_id_type=MESH)
                cp.start()
                sends.append(cp)
        for cp in sends:
            cp.wait_recv()
        for cp in sends:
            cp.wait_send()

    return pl.pallas_call(
        body, name="reduce_scatter", in_specs=[_ANY] * n, out_specs=[_ANY] * n,
        out_shape=[_sds((N_DEV - 1,) + g.shape[1:], g.dtype) for g in grads],
        scratch_shapes=[pltpu.SemaphoreType.DMA((7 * n,)), pltpu.SemaphoreType.DMA((7 * n,))],
    )(*grads)


def _vec8(rows, width):
    return jnp.concatenate([r.reshape(1, width) for r in rows] + [jnp.zeros((8 - len(rows), width), F32)], axis=0)


def local_step(x, ctx, tgt, mod, mod_c, small, w_int, w_out, wg_t, wu_t, w_down, tt, tt_ctx, cb, cb_ctx):
    sh1, sc1, g1, sh2, sc2, g2 = [mod[i * D:(i + 1) * D] for i in range(6)]
    csh1, csc1 = mod_c[0:D], mod_c[D:2 * D]
    vec1 = _vec8([small["norm1_g"], sh1, sc1], D)
    vec1c = _vec8([small["norm1_g"], csh1, csc1], D)
    vec2 = _vec8([small["norm2_g"], sh2, sc2], D)
    vec3 = _vec8([g2, small["final_g"]], D)
    vecm = _vec8([g1, small["norm2_g"], sh2, sc2], D)
    vcm = _vec8([jnp.tile(small["gla_norm_g"].reshape(HV), NH), small["conv_b"], small["conv_ln_g"],
                 small["conv_ln_b"]], DC)
    convw = jnp.concatenate([small["conv_w"], jnp.zeros((1, DC), F32)], axis=0)
    wa = jnp.zeros((128, 512), F32)
    wa = wa.at[0:RANK, 0:DK].set(small["w_a2_f"]).at[RANK:2 * RANK, DK:2 * DK].set(small["w_a2_b"])
    ba = jnp.concatenate([small["b_a_f"].reshape(1, DK), small["b_a_b"].reshape(1, DK)], axis=1)

    _, _, kc, vc_, _, rc, lac, hc = proj_fwd(ctx, vec1c, w_int, wa, ba, tt_ctx)
    qc0 = jnp.zeros_like(kc)
    _, sall_c, sfin_c = gla_fwd(qc0, kc, vc_, lac, jnp.zeros((2, HV, DK), F32), cb_ctx)
    u, q, k, v, g, r, la, h = proj_fwd(x, vec1, w_int, wa, ba, tt)
    o2, sall, _ = gla_fwd(q, k, v, la, sfin_c, cb)
    x1, cat, mix, yc, h2 = merge_fwd(u, g, o2, x, vecm, vcm, convw, w_out, tt)
    gt, up, hid = ffn_gate_up(h2, wg_t, wu_t, tt)
    dx2, dff, acc3 = ffn_down_loss(hid, x1, tgt, vec3, w_down, tt)
    dgt, dup = ffn_dhid(dff, gt, up, w_down, tt)
    dx1, acc2 = ffn_dh2(dgt, dup, x1, dx2, vec2, wg_t, wu_t, tt)
    du, dg, do, dmix, accm1, accm2, dconvw = merge_bwd(dx1, mix, u, g, o2, yc, vecm, vcm, convw, w_out, tt)
    dq2, dk2, dv2, dla, ds0 = gla_bwd(q, k, v, la, do, sall, jnp.zeros((2, HV, DK), F32), cb)
    gx, dp, acc1, dba, dwa = proj_bwd(du, dq2, dk2, dv2, dg, dla, la, r, x, dx1, vec1, w_int, wa, tt)
    tcx = ctx.shape[0]
    zc = lambda w: jnp.zeros((tcx, w), F32)
    _, dk2c, dv2c, dlac, _ = gla_bwd(qc0, kc, vc_, lac, zc(DV), sall_c, ds0, cb_ctx)
    z2 = jnp.zeros((2, tcx, DK), F32)
    _, dpc, acc1c, dbac, dwac = proj_bwd(zc(1024), z2, dk2c, dv2c, zc(DV), dlac, lac, rc, ctx, zc(D), vec1c, w_int,
                                         wa, tt_ctx)
    bt = min(1024, x.shape[0])
    btc = min(1024, tcx)
    gw = {
        "w_down": tn_matmul(hid, dff, FN, bt),
        "wg_t": tn_matmul(dgt, h2, FN, bt),
        "wu_t": tn_matmul(dup, h2, FN, bt),
        "w_out": tn_matmul(cat, dmix, 512, bt),
        "w_int": tn_matmul(dp, h, 896, bt, init=tn_matmul(dpc, hc, 896, btc)[0]),
    }
    dwa_t = dwa + dwac
    dba_t = dba + dbac
    gs = {
        "norm1_g": acc1[2] + acc1c[2], "norm2_g": acc2[2], "final_g": acc3[1], "loss": acc3[2],
        "gla_norm_g": accm2[0], "conv_b": accm2[1], "conv_ln_g": accm2[2], "conv_ln_b": accm2[3],
        "conv_w": dconvw, "b_a": dba_t[0], "w_a2": dwa_t,
    }
    dmod = jnp.concatenate([acc1[0], acc1[1], accm1[0], acc2[0], acc2[1], acc3[0]])
    dmod_c = jnp.concatenate([acc1c[0], acc1c[1], jnp.zeros((4 * D,), F32)])
    return gx, gw, gs, dmod, dmod_c


PACK_ROWS = 96
ROW_N1, ROW_N2, ROW_FG, ROW_LOSS, ROW_GN, ROW_CB, ROW_LG, ROW_LB, ROW_BA = 0, 1, 2, 3, 4, 5, 6, 7, 8
ROW_DMOD, ROW_DMODC, ROW_CW, ROW_WA = 9, 15, 24, 56


def _pack_small(gs, dmod, dmod_c):
    pad = lambda a: jnp.pad(a, ((0, 0), (0, D - a.shape[1])))
    rows = [gs["norm1_g"].reshape(1, D), gs["norm2_g"].reshape(1, D), gs["final_g"].reshape(1, D),
            gs["loss"].reshape(1, D), pad(gs["gla_norm_g"].reshape(1, DC)), pad(gs["conv_b"].reshape(1, DC)),
            pad(gs["conv_ln_g"].reshape(1, DC)), pad(gs["conv_ln_b"].reshape(1, DC)), pad(gs["b_a"].reshape(1, 512)),
            dmod.reshape(6, D), dmod_c.reshape(6, D), jnp.zeros((3, D), F32), pad(gs["conv_w"]),
            pad(gs["w_a2"][0:32]), jnp.zeros((8, D), F32)]
    return jnp.concatenate(rows, axis=0)


def kernel(x, c, ctx, c_ctx, w_mod, b_mod, norm1_g, norm2_g, w_in, conv_w, conv_b, conv_ln_g, conv_ln_b, w_a2_f, b_a_f, w_a2_b, b_a_b, gla_norm_g, w_out, w_gate, w_up, w_down, final_g, loss_target, m_c_ctx, m_w_mod, m_b_mod, m_norm1_g, m_norm2_g, m_w_in, m_conv_w, m_conv_b, m_conv_ln_g, m_conv_ln_b, m_w_a2_f, m_b_a_f, m_w_a2_b, m_b_a_b, m_gla_norm_g, m_w_out, m_w_gate, m_w_up, m_w_down, m_final_g, v_c_ctx, v_w_mod, v_b_mod, v_norm1_g, v_norm2_g, v_w_in, v_conv_w, v_conv_b, v_conv_ln_g, v_conv_ln_b, v_w_a2_f, v_b_a_f, v_w_a2_b, v_b_a_b, v_gla_norm_g, v_w_out, v_w_gate, v_w_up, v_w_down, v_final_g):
    me = 4 * lax.axis_index("x") + 2 * lax.axis_index("y") + lax.axis_index("c")
    t = x.shape[1]
    tcx = ctx.shape[1]
    r_in, r_out, r_ff = w_in.shape[2], w_out.shape[1], w_gate.shape[2]
    r_in_b = -(-r_in // 16) * 16

    mod_all, s_all = mod_forward(c, c_ctx.reshape(1, D), w_mod[0], b_mod)
    mod = lax.dynamic_slice(mod_all, (me, 0), (1, 6 * D)).reshape(6 * D)
    mod_c = mod_all[8]

    tb = lambda w: w.T.astype(MXU_DTYPE)
    wpack = jnp.concatenate([jnp.pad(tb(w_in[0]), ((0, r_in_b - r_in), (0, 0))), w_out[0].astype(MXU_DTYPE),
                             tb(w_gate[0]), tb(w_up[0]), w_down[0].astype(MXU_DTYPE)], axis=0)
    wall = gather_weights(wpack)
    o1 = r_in_b
    o2_, o3, o4 = o1 + r_out, o1 + r_out + r_ff, o1 + r_out + 2 * r_ff
    w_int = jnp.pad(wall[:, 0:r_in, :].reshape(N_DEV * r_in, D), ((0, DINP - DIN), (0, 0)))
    w_out_f = wall[:, o1:o2_, :].reshape(N_DEV * r_out, D)
    wg_t = wall[:, o2_:o3, :].reshape(DFF, D)
    wu_t = wall[:, o3:o4, :].reshape(DFF, D)
    w_down_f = wall[:, o4:o4 + r_ff, :].reshape(DFF, D)

    small = dict(norm1_g=norm1_g[0], norm2_g=norm2_g[0], final_g=final_g, gla_norm_g=gla_norm_g[0],
                 conv_b=conv_b[0], conv_ln_g=conv_ln_g[0], conv_ln_b=conv_ln_b[0], b_a_f=b_a_f[0], b_a_b=b_a_b[0])
    sm_pack = jnp.zeros((PACK_ROWS, D), F32)
    sm_pack = lax.dynamic_update_slice(sm_pack, conv_w[0], (ROW_CW, me * (DC // N_DEV)))
    sm_pack = lax.dynamic_update_slice(sm_pack, w_a2_f[0], (ROW_WA, me * (DK // N_DEV)))
    sm_pack = lax.dynamic_update_slice(sm_pack, w_a2_b[0], (ROW_WA + RANK, DK + me * (DK // N_DEV)))
    sm_tot, _ = gather_sum_small(sm_pack)
    small["conv_w"] = sm_tot[ROW_CW:ROW_CW + CW, 0:DC]
    small["w_a2_f"] = sm_tot[ROW_WA:ROW_WA + RANK, 0:DK]
    small["w_a2_b"] = sm_tot[ROW_WA + RANK:ROW_WA + 2 * RANK, DK:2 * DK]

    gx, gw, gs, dmod, dmod_c = local_step(x[0], ctx[0], loss_target[0], mod, mod_c, small, w_int, w_out_f, wg_t,
                                          wu_t, w_down_f, 512, 256, 8, 4)

    tot, gat = gather_sum_small(_pack_small(gs, dmod, dmod_c))
    loss = jnp.sum(tot[ROW_LOSS])
    dm = jnp.concatenate([gat[:, ROW_DMOD:ROW_DMOD + 6, :].reshape(N_DEV, 6 * D),
                          tot[ROW_DMODC:ROW_DMODC + 6, :].reshape(1, 6 * D), jnp.zeros((7, 6 * D), F32)], axis=0)
    ncol = w_mod.shape[2]
    dm_sh = lax.dynamic_slice(dm, (0, me * ncol), (16, ncol))
    g_wmod, d_wmod, nm_wmod, nv_wmod, g_cc, d_cc, nm_cc, nv_cc = mod_backward(
        s_all, dm_sh, w_mod[0], m_w_mod[0], v_w_mod[0], c_ctx.reshape(1, D), m_c_ctx.reshape(1, D),
        v_c_ctx.reshape(1, D))

    pad_in = lambda g: jnp.pad(g[0:DIN].reshape(N_DEV, r_in, D), ((0, 0), (0, r_in_b - r_in), (0, 0)))
    blocks = lambda i: [pad_in(gw["w_int"][i]), gw["w_out"][i].reshape(N_DEV, r_out, D),
                        gw["wg_t"][i].reshape(N_DEV, r_ff, D), gw["wu_t"][i].reshape(N_DEV, r_ff, D),
                        gw["w_down"][i].reshape(N_DEV, r_ff, D)]
    recv = reduce_scatter(blocks(1))
    own = [lax.dynamic_index_in_dim(g, me, 0, keepdims=False) for g in blocks(0)]
    padt = lambda w: jnp.pad(w.T, ((0, r_in_b - r_in), (0, 0)))
    big = {}
    big["w_in"] = [a[0:r_in].T for a in adamw_sharded(own[0], recv[0], padt(w_in[0]), padt(m_w_in[0]),
                                                       padt(v_w_in[0]))]
    big["w_out"] = adamw_sharded(own[1], recv[1], w_out[0], m_w_out[0], v_w_out[0])
    big["w_gate"] = [a.T for a in adamw_sharded(own[2], recv[2], w_gate[0].T, m_w_gate[0].T, v_w_gate[0].T)]
    big["w_up"] = [a.T for a in adamw_sharded(own[3], recv[3], w_up[0].T, m_w_up[0].T, v_w_up[0].T)]
    big["w_down"] = adamw_sharded(own[4], recv[4], w_down[0], m_w_down[0], v_w_down[0])
    big["w_mod"] = [g_wmod, d_wmod, nm_wmod, nv_wmod]

    row = lambda r, w: tot[r:r + 1, 0:w]
    gn_row = tot[ROW_GN:ROW_GN + 1, 0:DC]
    g_small = {
        "b_mod": jnp.sum(dm, axis=0, keepdims=True),
        "norm1_g": row(ROW_N1, D), "norm2_g": row(ROW_N2, D),
        "conv_w": lax.dynamic_slice(tot, (ROW_CW, me * (DC // N_DEV)), (CW, DC // N_DEV)),
        "conv_b": row(ROW_CB, DC), "conv_ln_g": row(ROW_LG, DC), "conv_ln_b": row(ROW_LB, DC),
        "w_a2_f": lax.dynamic_slice(tot, (ROW_WA, me * (DK // N_DEV)), (RANK, DK // N_DEV)),
        "b_a_f": tot[ROW_BA:ROW_BA + 1, 0:DK],
        "w_a2_b": lax.dynamic_slice(tot, (ROW_WA + RANK, DK + me * (DK // N_DEV)), (RANK, DK // N_DEV)),
        "b_a_b": tot[ROW_BA:ROW_BA + 1, DK:2 * DK],
        "gla_norm_g": gn_row[:, 0:HV] + gn_row[:, HV:2 * HV] + gn_row[:, 2 * HV:3 * HV] + gn_row[:, 3 * HV:4 * HV],
        "final_g": row(ROW_FG, D),
    }
    wmv = {
        "b_mod": (b_mod, m_b_mod, v_b_mod), "norm1_g": (norm1_g, m_norm1_g, v_norm1_g),
        "norm2_g": (norm2_g, m_norm2_g, v_norm2_g), "conv_w": (conv_w[0], m_conv_w[0], v_conv_w[0]),
        "conv_b": (conv_b, m_conv_b, v_conv_b), "conv_ln_g": (conv_ln_g, m_conv_ln_g, v_conv_ln_g),
        "conv_ln_b": (conv_ln_b, m_conv_ln_b, v_conv_ln_b), "w_a2_f": (w_a2_f[0], m_w_a2_f[0], v_w_a2_f[0]),
        "b_a_f": (b_a_f, m_b_a_f, v_b_a_f), "w_a2_b": (w_a2_b[0], m_w_a2_b[0], v_w_a2_b[0]),
        "b_a_b": (b_a_b, m_b_a_b, v_b_a_b), "gla_norm_g": (gla_norm_g, m_gla_norm_g, v_gla_norm_g),
        "final_g": (final_g.reshape(1, D), m_final_g.reshape(1, D), v_final_g.reshape(1, D)),
    }
    names_small = list(g_small)
    upd = adamw_small([(g_small[n],) + wmv[n] for n in names_small])
    res = {n: (g_small[n],) + upd[i] for i, n in enumerate(names_small)}
    res["c_ctx"] = (g_cc, d_cc, nm_cc, nv_cc)
    for n in ("w_mod", "w_in", "w_out", "w_gate", "w_up", "w_down"):
        res[n] = tuple(big[n])

    order = ["c_ctx", "w_mod", "b_mod", "norm1_g", "norm2_g", "w_in", "conv_w", "conv_b", "conv_ln_g", "conv_ln_b",
             "w_a2_f", "b_a_f", "w_a2_b", "b_a_b", "gla_norm_g", "w_out", "w_gate", "w_up", "w_down", "final_g"]
    shapes = {"c_ctx": c_ctx.shape, "w_mod": w_mod.shape, "b_mod": b_mod.shape, "norm1_g": norm1_g.shape,
              "norm2_g": norm2_g.shape, "w_in": w_in.shape, "conv_w": conv_w.shape, "conv_b": conv_b.shape,
              "conv_ln_g": conv_ln_g.shape, "conv_ln_b": conv_ln_b.shape, "w_a2_f": w_a2_f.shape,
              "b_a_f": b_a_f.shape, "w_a2_b": w_a2_b.shape, "b_a_b": b_a_b.shape, "gla_norm_g": gla_norm_g.shape,
              "w_out": w_out.shape, "w_gate": w_gate.shape, "w_up": w_up.shape, "w_down": w_down.shape,
              "final_g": final_g.shape}
    outs = [loss, gx.reshape(x.shape)]
    for i in range(4):
        outs += [res[n][i].reshape(shapes[n]) for n in order]
    return tuple(outs)
```

```python
import functools

import jax
import jax.numpy as jnp
from jax import lax
from jax.experimental import pallas as pl
from jax.experimental.pallas import tpu as pltpu

F32 = jnp.float32
MXU_DTYPE = jnp.bfloat16
WIRE_DTYPE = jnp.bfloat16
HI = lax.Precision.HIGHEST
MESH = pl.DeviceIdType.MESH

N_DEV = 8
D = 1024
DC = 512
NH = 4
HK = 64
HV = 128
DK = NH * HK
DV = NH * HV
RANK = 16
CHUNK = 64
SEG = 64
CW = 31
CPAD = 15
DFF = 2816
DIN = 2592
DINP = 2688
TAU = 16.0
EPS = 1e-6
VMEM_LIMIT = 56 * 1024 * 1024

ADAM_LR = 0.001
ADAM_B1 = 0.9
ADAM_B2 = 0.999
ADAM_EPS = 1e-08
ADAM_WD = 0.01
ADAM_STEP = 10


def _mm(a, b):
    return jnp.dot(a.astype(MXU_DTYPE), b.astype(MXU_DTYPE), preferred_element_type=F32)


def _mm_nt(a, b):
    return lax.dot_general(a.astype(MXU_DTYPE), b.astype(MXU_DTYPE), (((1,), (1,)), ((), ())),
                           preferred_element_type=F32)


def _mm_tn(a, b):
    return lax.dot_general(a.astype(MXU_DTYPE), b.astype(MXU_DTYPE), (((0,), (0,)), ((), ())),
                           preferred_element_type=F32)


def _hi(a, b):
    return jnp.dot(a, b, precision=HI, preferred_element_type=F32)


def _hi_nt(a, b):
    return lax.dot_general(a, b, (((1,), (1,)), ((), ())), precision=HI, preferred_element_type=F32)


def _hi_tn(a, b):
    return lax.dot_general(a, b, (((0,), (0,)), ((), ())), precision=HI, preferred_element_type=F32)


def _sigmoid(x):
    return 1.0 / (1.0 + jnp.exp(-x))


def _cparams(n_axes):
    return pltpu.CompilerParams(dimension_semantics=("arbitrary",) * n_axes, vmem_limit_bytes=VMEM_LIMIT)


def _full(shape):
    n = len(shape)
    return pl.BlockSpec(shape, lambda *_: (0,) * n)


def _rows(tt, width):
    return pl.BlockSpec((tt, width), lambda i: (i, 0))


def _sds(shape, dtype=F32):
    return jax.ShapeDtypeStruct(shape, dtype)


def _norm_mod(x, g, sh, sc):
    r = lax.rsqrt(jnp.mean(x * x, axis=-1, keepdims=True) + EPS)
    xn = x * r
    yy = xn * g
    return r, xn, yy, yy * (1.0 + sc) + sh


def _norm_mod_bwd(dh, r, xn, yy, g, sc):
    dsh = jnp.sum(dh, axis=0, keepdims=True)
    dsc = jnp.sum(dh * yy, axis=0, keepdims=True)
    dy = dh * (1.0 + sc)
    dg = jnp.sum(dy * xn, axis=0, keepdims=True)
    dxn = dy * g
    dx = r * (dxn - xn * jnp.mean(dxn * xn, axis=-1, keepdims=True))
    return dsh, dsc, dg, dx


def _acc_rows(ref, first, rows):
    upd = jnp.concatenate(rows + [jnp.zeros((8 - len(rows), rows[0].shape[1]), F32)], axis=0)

    @pl.when(first)
    def _():
        ref[...] = upd

    @pl.when(jnp.logical_not(first))
    def _():
        ref[...] += upd


def _acc(ref, first, val):
    @pl.when(first)
    def _():
        ref[...] = val

    @pl.when(jnp.logical_not(first))
    def _():
        ref[...] += val


def proj_fwd(x, vec, w_int, wa, ba, tt):
    t = x.shape[0]

    def body(x_ref, vec_ref, w_ref, wa_ref, ba_ref, u_ref, q_ref, k_ref, v_ref, g_ref, r_ref, la_ref, h_ref):
        _, _, _, h = _norm_mod(x_ref[...], vec_ref[0:1, :], vec_ref[1:2, :], vec_ref[2:3, :])
        hb = h.astype(MXU_DTYPE)
        h_ref[...] = hb
        p = _mm_nt(hb, w_ref[...])
        u_ref[...] = p[:, 0:1024]
        q_ref[...] = p[:, 1024:1280]
        k_ref[...] = p[:, 1280:1536]
        v_ref[...] = p[:, 1536:2048]
        g_ref[...] = p[:, 2048:2560]
        rr = p[:, 2560:2688]
        r_ref[...] = rr
        z = _hi(rr, wa_ref[...]) + ba_ref[...]
        la_ref[...] = (jnp.minimum(z, 0.0) - jnp.log(1.0 + jnp.exp(-jnp.abs(z)))) * (1.0 / TAU)

    return pl.pallas_call(
        body, name="proj_fwd", grid=(t // tt,),
        in_specs=[_rows(tt, D), _full((8, D)), _full((DINP, D)), _full((128, 512)), _full((1, 512))],
        out_specs=[_rows(tt, 1024), _rows(tt, DK), _rows(tt, DK), _rows(tt, DV), _rows(tt, DV), _rows(tt, 128),
                   _rows(tt, 512), _rows(tt, D)],
        out_shape=[_sds((t, 1024)), _sds((t, DK)), _sds((t, DK)), _sds((t, DV)), _sds((t, DV)), _sds((t, 128)),
                   _sds((t, 512)), _sds((t, D), MXU_DTYPE)],
        compiler_params=_cparams(1),
    )(x, vec, w_int, wa, ba)


def proj_bwd(du, dq2, dk2, dv2, dg, dla, la, r, x, dx1, vec, w_int, wa, tt):
    t = x.shape[0]

    def body(du_ref, dq_ref, dk_ref, dv_ref, dg_ref, dla_ref, la_ref, r_ref, x_ref, dx1_ref, vec_ref, w_ref, wa_ref,
             gx_ref, dp_ref, acc_ref, dba_ref, dwa_ref):
        first = pl.program_id(0) == 0
        dz = dla_ref[...] * (1.0 - jnp.exp(TAU * la_ref[...])) * (1.0 / TAU)
        rr = r_ref[...]
        _acc_rows(dba_ref, first, [jnp.sum(dz, axis=0, keepdims=True)])
        _acc(dwa_ref, first, _hi_tn(rr, dz))
        dr = _hi_nt(dz, wa_ref[...])
        dp = jnp.concatenate([du_ref[...], dq_ref[0] + dq_ref[1], dk_ref[0] + dk_ref[1], dv_ref[0] + dv_ref[1],
                              dg_ref[...], dr], axis=1).astype(MXU_DTYPE)
        dp_ref[...] = dp
        dh = _mm(dp, w_ref[...])
        g, sc = vec_ref[0:1, :], vec_ref[2:3, :]
        rn, xn, yy, _ = _norm_mod(x_ref[...], g, vec_ref[1:2, :], sc)
        dsh, dsc, dgn, dx = _norm_mod_bwd(dh, rn, xn, yy, g, sc)
        gx_ref[...] = dx1_ref[...] + dx
        _acc_rows(acc_ref, first, [dsh, dsc, dgn])

    two = lambda w: pl.BlockSpec((2, tt, w), lambda i: (0, i, 0))
    return pl.pallas_call(
        body, name="proj_bwd", grid=(t // tt,),
        in_specs=[_rows(tt, 1024), two(DK), two(DK), two(DV), _rows(tt, DV), _rows(tt, 512), _rows(tt, 512),
                  _rows(tt, 128), _rows(tt, D), _rows(tt, D), _full((8, D)), _full((DINP, D)), _full((128, 512))],
        out_specs=[_rows(tt, D), _rows(tt, DINP), _full((8, D)), _full((8, 512)), _full((128, 512))],
        out_shape=[_sds((t, D)), _sds((t, DINP), MXU_DTYPE), _sds((8, D)), _sds((8, 512)), _sds((128, 512))],
        compiler_params=_cparams(1),
    )(du, dq2, dk2, dv2, dg, dla, la, r, x, dx1, vec, w_int, wa)


def _dot_exact01(m01, x):
    bf = jnp.bfloat16
    w = x.shape[1]
    hi = x.astype(bf)
    r1 = x - hi.astype(F32)
    mid = r1.astype(bf)
    lo = (r1 - mid.astype(F32)).astype(bf)
    y = jnp.dot(m01.astype(bf), jnp.concatenate([hi, mid, lo], axis=1), preferred_element_type=F32)
    return y[:, 0:w] + y[:, w:2 * w] + y[:, 2 * w:3 * w]


def _gla_chunk(d, qc, kc, la_c):
    row = lax.broadcasted_iota(jnp.int32, (CHUNK, CHUNK), 0)
    col = lax.broadcasted_iota(jnp.int32, (CHUNK, CHUNK), 1)
    cum = jnp.where(d == 0, (col <= row).astype(F32), (col >= row).astype(F32))
    cum_t = jnp.where(d == 0, (col >= row).astype(F32), (col <= row).astype(F32))
    cum4 = jnp.concatenate([cum] * NH, axis=0)
    head_of_lane = lax.broadcasted_iota(jnp.int32, (1, DK), 1) // HK
    b = _dot_exact01(cum, la_c)
    bl = jnp.sum(la_c, axis=0, keepdims=True)
    eb = jnp.exp(b)
    enb = jnp.exp(-b)
    ekd = jnp.exp(bl - b)
    qt = qc * (HK ** -0.5) * eb
    kt = kc * enb
    kd = kc * ekd
    qst = jnp.concatenate([jnp.where(head_of_lane == h, qt, 0.0) for h in range(NH)], axis=0)
    a = _mm_nt(qst, kt) * cum4
    return cum_t, cum4, head_of_lane, eb, enb, ekd, qt, kt, kd, qst, a, jnp.exp(bl)


def gla_fwd(q, k, v, la, s0, cb):
    t = q.shape[0]
    nc = t // CHUNK
    nb = nc // cb
    blk = lambda d, i: i + d * (nb - 1 - 2 * i)

    def body(q_ref, k_ref, v_ref, la_ref, s0_ref, o_ref, sall_ref, sfin_ref, s_scr):
        d = pl.program_id(0)
        i = pl.program_id(1)

        @pl.when(i == 0)
        def _():
            s_scr[...] = s0_ref[0]

        for j in range(cb):
            jj = j + d * (cb - 1 - 2 * j)
            rows = pl.ds(pl.multiple_of(jj * CHUNK, CHUNK), CHUNK)
            vc = v_ref[rows, :]
            _, _, head_of_lane, _, _, _, _, _, kd, qst, a, dec = _gla_chunk(
                d, q_ref[rows, :], k_ref[rows, :], la_ref[rows, :])
            s = s_scr[...]
            sall_ref[0, jj] = s
            inter = _mm_nt(qst, s)
            outs = []
            for h in range(NH):
                hs = slice(h * CHUNK, (h + 1) * CHUNK)
                outs.append(_mm(a[hs], vc[:, h * HV:(h + 1) * HV]) + inter[hs])
            o_ref[0, rows, :] = jnp.concatenate(outs, axis=1)
            kv = _mm_tn(vc, kd)
            s_new = dec * s
            for h in range(NH):
                s_new = s_new + jnp.where(head_of_lane == h, kv[h * HV:(h + 1) * HV], 0.0)
            s_scr[...] = s_new

        @pl.when(i == nb - 1)
        def _():
            sfin_ref[0] = s_scr[...]

    tb = cb * CHUNK
    return pl.pallas_call(
        body, name="gla_fwd", grid=(2, nb),
        in_specs=[pl.BlockSpec((tb, DK), lambda d, i: (blk(d, i), 0)),
                  pl.BlockSpec((tb, DK), lambda d, i: (blk(d, i), 0)),
                  pl.BlockSpec((tb, DV), lambda d, i: (blk(d, i), 0)),
                  pl.BlockSpec((tb, DK), lambda d, i: (blk(d, i), d)),
                  pl.BlockSpec((1, HV, DK), lambda d, i: (d, 0, 0))],
        out_specs=[pl.BlockSpec((1, tb, DV), lambda d, i: (d, blk(d, i), 0)),
                   pl.BlockSpec((1, cb, HV, DK), lambda d, i: (d, blk(d, i), 0, 0)),
                   pl.BlockSpec((1, HV, DK), lambda d, i: (d, 0, 0))],
        out_shape=[_sds((2, t, DV)), _sds((2, nc, HV, DK)), _sds((2, HV, DK))],
        scratch_shapes=[pltpu.VMEM((HV, DK), F32)],
        compiler_params=_cparams(2),
    )(q, k, v, la, s0)


def gla_bwd(q, k, v, la, do, sall, dsfin, cb):
    t = q.shape[0]
    nc = t // CHUNK
    nb = nc // cb
    blk = lambda d, i: (nb - 1 - i) + d * (2 * i - (nb - 1))

    def body(q_ref, k_ref, v_ref, la_ref, do_ref, sall_ref, dsfin_ref, dq_ref, dk_ref, dv_ref, dla_ref, ds0_ref,
             ds_scr):
        d = pl.program_id(0)
        i = pl.program_id(1)

        @pl.when(i == 0)
        def _():
            ds_scr[...] = dsfin_ref[0]

        for j in range(cb):
            jj = (cb - 1 - j) + d * (2 * j - (cb - 1))
            rows = pl.ds(pl.multiple_of(jj * CHUNK, CHUNK), CHUNK)
            vc = v_ref[rows, :]
            doc = do_ref[rows, :]
            cum_t, cum4, head_of_lane, eb, enb, ekd, qt, kt, kd, qst, a, dec = _gla_chunk(
                d, q_ref[rows, :], k_ref[rows, :], la_ref[rows, :])
            s = sall_ref[0, jj]
            ds = ds_scr[...]
            hv = lambda x, h: x[:, h * HV:(h + 1) * HV]
            hr = lambda x, h: x[h * CHUNK:(h + 1) * CHUNK]
            fold = lambda x: functools.reduce(
                lambda p, c: p + c, [jnp.where(head_of_lane == h, hr(x, h), 0.0) for h in range(NH)])
            dost = jnp.concatenate([hv(doc, h) for h in range(NH)], axis=0)
            vst = jnp.concatenate([hv(vc, h) for h in range(NH)], axis=0)
            da = jnp.concatenate([_mm_nt(hv(doc, h), hv(vc, h)) for h in range(NH)], axis=0) * cum4
            dqt = fold(_mm(da, kt) + _mm(dost, s))
            dkt = _mm_tn(da, qst)
            kdst = jnp.concatenate([jnp.where(head_of_lane == h, kd, 0.0) for h in range(NH)], axis=0)
            dv_inter = _mm_nt(kdst, ds)
            dv_ref[0, rows, :] = jnp.concatenate(
                [_mm_tn(hr(a, h), hv(doc, h)) + hr(dv_inter, h) for h in range(NH)], axis=1)
            dkd = fold(_mm(vst, ds))
            ds_scr[...] = dec * ds + _mm_tn(dost, qst)
            tkd = dkd * kd
            db = dqt * qt - dkt * kt - tkd
            dbl = jnp.sum(ds * s, axis=0, keepdims=True) * dec + jnp.sum(tkd, axis=0, keepdims=True)
            dla_ref[rows, :] = _dot_exact01(cum_t, db) + dbl
            dq_ref[0, rows, :] = dqt * eb * (HK ** -0.5)
            dk_ref[0, rows, :] = dkt * enb + dkd * ekd

        @pl.when(i == nb - 1)
        def _():
            ds0_ref[0] = ds_scr[...]

    tb = cb * CHUNK
    return pl.pallas_call(
        body, name="gla_bwd", grid=(2, nb),
        in_specs=[pl.BlockSpec((tb, DK), lambda d, i: (blk(d, i), 0)),
                  pl.BlockSpec((tb, DK), lambda d, i: (blk(d, i), 0)),
                  pl.BlockSpec((tb, DV), lambda d, i: (blk(d, i), 0)),
                  pl.BlockSpec((tb, DK), lambda d, i: (blk(d, i), d)),
                  pl.BlockSpec((tb, DV), lambda d, i: (blk(d, i), 0)),
                  pl.BlockSpec((1, cb, HV, DK), lambda d, i: (d, blk(d, i), 0, 0)),
                  pl.BlockSpec((1, HV, DK), lambda d, i: (d, 0, 0))],
        out_specs=[pl.BlockSpec((1, tb, DK), lambda d, i: (d, blk(d, i), 0)),
                   pl.BlockSpec((1, tb, DK), lambda d, i: (d, blk(d, i), 0)),
                   pl.BlockSpec((1, tb, DV), lambda d, i: (d, blk(d, i), 0)),
                   pl.BlockSpec((tb, DK), lambda d, i: (blk(d, i), d)),
                   pl.BlockSpec((1, HV, DK), lambda d, i: (d, 0, 0))],
        out_shape=[_sds((2, t, DK)), _sds((2, t, DK)), _sds((2, t, DV)), _sds((t, 2 * DK)), _sds((2, HV, DK))],
        scratch_shapes=[pltpu.VMEM((HV, DK), F32)],
        compiler_params=_cparams(2),
    )(q, k, v, la, do, sall, dsfin)


def _seg_pos(tt):
    return lax.broadcasted_iota(jnp.int32, (tt, 1), 0) % SEG


def _shifted(x, s, pos, tt):
    y = x if s == 0 else pltpu.roll(x, (-s) % tt, 0)
    return jnp.where((pos + s >= 0) & (pos + s < SEG), y, 0.0)


def _head_norm(o, gn):
    rs, xs = [], []
    for h in range(NH):
        oh = o[:, h * HV:(h + 1) * HV]
        r = lax.rsqrt(jnp.mean(oh * oh, axis=-1, keepdims=True) + EPS)
        rs.append(r)
        xs.append(oh * r)
    return rs, xs


def merge_fwd(u, g, o2, x, vec, vc, convw, w_out, tt):
    t = x.shape[0]

    def body(u_ref, g_ref, o_ref, x_ref, vec_ref, vc_ref, cw_ref, w_ref, x1_ref, cat_ref, mix_ref, yc_ref, h2_ref):
        a = u_ref[:, 0:DC]
        gate = u_ref[:, DC:2 * DC]
        vv = a * _sigmoid(gate)
        pos = _seg_pos(tt)
        cw = cw_ref[...]
        yc = jnp.zeros((tt, DC), F32) + vc_ref[1:2, :]
        for j in range(CW):
            yc = yc + _shifted(vv, j - CPAD, pos, tt) * cw[j:j + 1, :]
        yc_ref[...] = yc
        mu = jnp.mean(yc, axis=-1, keepdims=True)
        yd = yc - mu
        rs = lax.rsqrt(jnp.mean(yd * yd, axis=-1, keepdims=True) + EPS)
        ln = yd * rs * vc_ref[2:3, :] + vc_ref[3:4, :]
        conv_o = ln * _sigmoid(ln)
        o = o_ref[0] + o_ref[1]
        _, xs = _head_norm(o, None)
        gg = g_ref[...]
        o2g = jnp.concatenate(xs, axis=1) * vc_ref[0:1, :] * (gg * _sigmoid(gg))
        cat = jnp.concatenate([conv_o, o2g], axis=1).astype(MXU_DTYPE)
        cat_ref[...] = cat
        mix = _mm(cat, w_ref[...])
        mix_ref[...] = mix
        x1 = x_ref[...] + vec_ref[0:1, :] * mix
        x1_ref[...] = x1
        _, _, _, h2 = _norm_mod(x1, vec_ref[1:2, :], vec_ref[2:3, :], vec_ref[3:4, :])
        h2_ref[...] = h2.astype(MXU_DTYPE)

    return pl.pallas_call(
        body, name="merge_fwd", grid=(t // tt,),
        in_specs=[_rows(tt, 1024), _rows(tt, DV), pl.BlockSpec((2, tt, DV), lambda i: (0, i, 0)), _rows(tt, D),
                  _full((8, D)), _full((8, DC)), _full((32, DC)), _full((D, D))],
        out_specs=[_rows(tt, D), _rows(tt, D), _rows(tt, D), _rows(tt, DC), _rows(tt, D)],
        out_shape=[_sds((t, D)), _sds((t, D), MXU_DTYPE), _sds((t, D)), _sds((t, DC)), _sds((t, D), MXU_DTYPE)],
        compiler_params=_cparams(1),
    )(u, g, o2, x, vec, vc, convw, w_out)


def merge_bwd(dx1, mix, u, g, o2, yc, vec, vc, convw, w_out, tt):
    t = dx1.shape[0]

    def body(dx1_ref, mix_ref, u_ref, g_ref, o_ref, yc_ref, vec_ref, vc_ref, cw_ref, w_ref,
             du_ref, dg_ref, do_ref, dmix_ref, acc1_ref, acc2_ref, dcw_ref):
        first = pl.program_id(0) == 0
        dx1v = dx1_ref[...]
        dg1 = jnp.sum(dx1v * mix_ref[...], axis=0, keepdims=True)
        dmix = (vec_ref[0:1, :] * dx1v).astype(MXU_DTYPE)
        dmix_ref[...] = dmix
        dcat = _mm_nt(dmix, w_ref[...])
        dconv_o = dcat[:, 0:DC]
        do2 = dcat[:, DC:2 * DC]
        gn = vc_ref[0:1, :]
        o = o_ref[0] + o_ref[1]
        rs, xs = _head_norm(o, None)
        xn = jnp.concatenate(xs, axis=1)
        gg = g_ref[...]
        sg = _sigmoid(gg)
        don = do2 * (gg * sg)
        dg_ref[...] = do2 * (xn * gn) * (sg * (1.0 + gg * (1.0 - sg)))
        dgn = jnp.sum(don * xn, axis=0, keepdims=True)
        dxn = don * gn
        dos = []
        for h in range(NH):
            dh = dxn[:, h * HV:(h + 1) * HV]
            dos.append(rs[h] * (dh - xs[h] * jnp.mean(dh * xs[h], axis=-1, keepdims=True)))
        do_ref[...] = jnp.concatenate(dos, axis=1)
        yc = yc_ref[...]
        mu = jnp.mean(yc, axis=-1, keepdims=True)
        yd = yc - mu
        rstd = lax.rsqrt(jnp.mean(yd * yd, axis=-1, keepdims=True) + EPS)
        yhat = yd * rstd
        lg = vc_ref[2:3, :]
        ln = yhat * lg + vc_ref[3:4, :]
        sl = _sigmoid(ln)
        dln = dconv_o * (sl * (1.0 + ln * (1.0 - sl)))
        dlb = jnp.sum(dln, axis=0, keepdims=True)
        dlg = jnp.sum(dln * yhat, axis=0, keepdims=True)
        dyh = dln * lg
        dyc = rstd * (dyh - jnp.mean(dyh, axis=-1, keepdims=True)
                      - yhat * jnp.mean(dyh * yhat, axis=-1, keepdims=True))
        dcb = jnp.sum(dyc, axis=0, keepdims=True)
        a = u_ref[:, 0:DC]
        gate = u_ref[:, DC:2 * DC]
        sgt = _sigmoid(gate)
        vv = a * sgt
        pos = _seg_pos(tt)
        cw = cw_ref[...]
        dvv = jnp.zeros((tt, DC), F32)
        dws = []
        for j in range(CW):
            s = j - CPAD
            dvv = dvv + _shifted(dyc, -s, pos, tt) * cw[j:j + 1, :]
            dws.append(jnp.sum(dyc * _shifted(vv, s, pos, tt), axis=0, keepdims=True))
        dws.append(jnp.zeros((1, DC), F32))
        du_ref[:, 0:DC] = dvv * sgt
        du_ref[:, DC:2 * DC] = dvv * a * sgt * (1.0 - sgt)
        _acc_rows(acc1_ref, first, [dg1])
        _acc_rows(acc2_ref, first, [dgn, dcb, dlg, dlb])
        _acc(dcw_ref, first, jnp.concatenate(dws, axis=0))

    return pl.pallas_call(
        body, name="merge_bwd", grid=(t // tt,),
        in_specs=[_rows(tt, D), _rows(tt, D), _rows(tt, 1024), _rows(tt, DV),
                  pl.BlockSpec((2, tt, DV), lambda i: (0, i, 0)), _rows(tt, DC),
                  _full((8, D)), _full((8, DC)), _full((32, DC)), _full((D, D))],
        out_specs=[_rows(tt, 1024), _rows(tt, DV), _rows(tt, DV), _rows(tt, D), _full((8, D)), _full((8, DC)),
                   _full((32, DC))],
        out_shape=[_sds((t, 1024)), _sds((t, DV)), _sds((t, DV)), _sds((t, D), MXU_DTYPE), _sds((8, D)),
                   _sds((8, DC)), _sds((32, DC))],
        compiler_params=_cparams(1),
    )(dx1, mix, u, g, o2, yc, vec, vc, convw, w_out)


FN = DFF // 2


def ffn_gate_up(h2, wg_t, wu_t, tt):
    t = h2.shape[0]

    def body(h2_ref, wg_ref, wu_ref, gt_ref, up_ref, hid_ref):
        h2v = h2_ref[...]
        gt = _mm_nt(h2v, wg_ref[...])
        up = _mm_nt(h2v, wu_ref[...])
        gt_ref[...] = gt.astype(MXU_DTYPE)
        up_ref[...] = up.astype(MXU_DTYPE)
        hid_ref[...] = (gt * _sigmoid(gt) * up).astype(MXU_DTYPE)

    blk = pl.BlockSpec((tt, FN), lambda j, i: (i, j))
    wblk = pl.BlockSpec((FN, D), lambda j, i: (j, 0))
    return pl.pallas_call(
        body, name="ffn_gate_up", grid=(2, t // tt),
        in_specs=[pl.BlockSpec((tt, D), lambda j, i: (i, 0)), wblk, wblk],
        out_specs=[blk, blk, blk],
        out_shape=[_sds((t, DFF), MXU_DTYPE)] * 3,
        compiler_params=_cparams(2),
    )(h2, wg_t, wu_t)


def ffn_down_loss(hid, x1, tgt, vec, w_down, tt):
    t = x1.shape[0]

    def body(hid_ref, x1_ref, tgt_ref, vec_ref, w_ref, dx2_ref, dff_ref, acc_ref):
        first = pl.program_id(0) == 0
        g2 = vec_ref[0:1, :]
        fg = vec_ref[1:2, :]
        ff = _mm(hid_ref[...], w_ref[...])
        x2 = x1_ref[...] + g2 * ff
        rf = lax.rsqrt(jnp.mean(x2 * x2, axis=-1, keepdims=True) + EPS)
        xn = x2 * rf
        err = xn * fg - tgt_ref[...]
        dy = err * (1.0 / D)
        dfg = jnp.sum(dy * xn, axis=0, keepdims=True)
        dxn = dy * fg
        dx2 = rf * (dxn - xn * jnp.mean(dxn * xn, axis=-1, keepdims=True))
        dx2_ref[...] = dx2
        dff_ref[...] = (g2 * dx2).astype(MXU_DTYPE)
        dg2 = jnp.sum(dx2 * ff, axis=0, keepdims=True)
        loss = jnp.sum(err * err, axis=0, keepdims=True) * (0.5 / D)
        _acc_rows(acc_ref, first, [dg2, dfg, loss])

    return pl.pallas_call(
        body, name="ffn_down_loss", grid=(t // tt,),
        in_specs=[_rows(tt, DFF), _rows(tt, D), _rows(tt, D), _full((8, D)), _full((DFF, D))],
        out_specs=[_rows(tt, D), _rows(tt, D), _full((8, D))],
        out_shape=[_sds((t, D)), _sds((t, D), MXU_DTYPE), _sds((8, D))],
        compiler_params=_cparams(1),
    )(hid, x1, tgt, vec, w_down)


def ffn_dhid(dff, gt, up, w_down, tt):
    t = dff.shape[0]

    def body(dff_ref, gt_ref, up_ref, w_ref, dgt_ref, dup_ref):
        dhid = _mm_nt(dff_ref[...], w_ref[...])
        gt = gt_ref[...].astype(F32)
        sg = _sigmoid(gt)
        dgt_ref[...] = (dhid * up_ref[...].astype(F32) * (sg * (1.0 + gt * (1.0 - sg)))).astype(MXU_DTYPE)
        dup_ref[...] = (dhid * (gt * sg)).astype(MXU_DTYPE)

    blk = pl.BlockSpec((tt, FN), lambda j, i: (i, j))
    return pl.pallas_call(
        body, name="ffn_dhid", grid=(2, t // tt),
        in_specs=[pl.BlockSpec((tt, D), lambda j, i: (i, 0)), blk, blk, pl.BlockSpec((FN, D), lambda j, i: (j, 0))],
        out_specs=[blk, blk],
        out_shape=[_sds((t, DFF), MXU_DTYPE), _sds((t, DFF), MXU_DTYPE)],
        compiler_params=_cparams(2),
    )(dff, gt, up, w_down)


def ffn_dh2(dgt, dup, x1, dx2, vec, wg_t, wu_t, tt):
    t = x1.shape[0]

    def body(dgt_ref, dup_ref, x1_ref, dx2_ref, vec_ref, wg_ref, wu_ref, dx1_ref, acc_ref):
        first = pl.program_id(0) == 0
        dh2 = _mm(dgt_ref[...], wg_ref[...]) + _mm(dup_ref[...], wu_ref[...])
        g, sc = vec_ref[0:1, :], vec_ref[2:3, :]
        r, xn, yy, _ = _norm_mod(x1_ref[...], g, vec_ref[1:2, :], sc)
        dsh, dsc, dgn, dx = _norm_mod_bwd(dh2, r, xn, yy, g, sc)
        dx1_ref[...] = dx2_ref[...] + dx
        _acc_rows(acc_ref, first, [dsh, dsc, dgn])

    return pl.pallas_call(
        body, name="ffn_dh2", grid=(t // tt,),
        in_specs=[_rows(tt, DFF), _rows(tt, DFF), _rows(tt, D), _rows(tt, D), _full((8, D)), _full((DFF, D)),
                  _full((DFF, D))],
        out_specs=[_rows(tt, D), _full((8, D))],
        out_shape=[_sds((t, D)), _sds((8, D))],
        compiler_params=_cparams(1),
    )(dgt, dup, x1, dx2, vec, wg_t, wu_t)


def tn_matmul(a, b, bm, bt, init=None):
    t, m = a.shape
    n = b.shape[1]
    nk = t // bt

    def body(*refs):
        if init is None:
            a_ref, b_ref, o_ref, wire_ref = refs
        else:
            a_ref, b_ref, i_ref, o_ref, wire_ref = refs
        prod = _mm_tn(a_ref[...], b_ref[...])

        @pl.when(pl.program_id(1) == 0)
        def _():
            o_ref[...] = prod if init is None else prod + i_ref[...]

        @pl.when(pl.program_id(1) != 0)
        def _():
            o_ref[...] += prod

        @pl.when(pl.program_id(1) == nk - 1)
        def _():
            wire_ref[...] = o_ref[...].astype(WIRE_DTYPE)

    in_specs = [pl.BlockSpec((bt, bm), lambda i, k: (k, i)), pl.BlockSpec((bt, n), lambda i, k: (k, 0))]
    args = [a, b]
    if init is not None:
        in_specs.append(pl.BlockSpec((bm, n), lambda i, k: (i, 0)))
        args.append(init)
    oblk = pl.BlockSpec((bm, n), lambda i, k: (i, 0))
    return pl.pallas_call(
        body, name="tn_matmul", grid=(m // bm, nk),
        in_specs=in_specs, out_specs=[oblk, oblk],
        out_shape=[_sds((m, n)), _sds((m, n), WIRE_DTYPE)], compiler_params=_cparams(2),
    )(*args)


def _adamw(w, g, m, v):
    m = ADAM_B1 * m + (1.0 - ADAM_B1) * g
    v = ADAM_B2 * v + (1.0 - ADAM_B2) * (g * g)
    m_hat = m / (1.0 - ADAM_B1 ** ADAM_STEP)
    v_hat = v / (1.0 - ADAM_B2 ** ADAM_STEP)
    delta = -ADAM_LR * (m_hat / (jnp.sqrt(v_hat) + ADAM_EPS) + ADAM_WD * w)
    return delta, m, v


def adamw_sharded(own, recv, w, m, v):
    shape = w.shape

    def body(own_ref, recv_ref, w_ref, m_ref, v_ref, g_ref, d_ref, mo_ref, vo_ref):
        g = own_ref[...]
        for k in range(N_DEV - 1):
            g = g + recv_ref[k].astype(F32)
        g_ref[...] = g
        d_ref[...], mo_ref[...], vo_ref[...] = _adamw(w_ref[...], g, m_ref[...], v_ref[...])

    return pl.pallas_call(
        body, name="adamw_sharded",
        in_specs=[_full(shape), _full((N_DEV - 1,) + shape), _full(shape), _full(shape), _full(shape)],
        out_specs=[_full(shape)] * 4, out_shape=[_sds(shape)] * 4, grid=(1,),
        compiler_params=_cparams(1),
    )(own, recv, w, m, v)


def adamw_small(items):
    n = len(items)
    flat = [a for it in items for a in it]

    def body(*refs):
        ins, outs = refs[:4 * n], refs[4 * n:]
        for i in range(n):
            g, w, m, v = (r[...] for r in ins[4 * i:4 * i + 4])
            outs[3 * i][...], outs[3 * i + 1][...], outs[3 * i + 2][...] = _adamw(w, g, m, v)

    out = pl.pallas_call(
        body, name="adamw_small", grid=(1,),
        in_specs=[_full(a.shape) for a in flat],
        out_specs=[_full(it[1].shape) for it in items for _ in range(3)],
        out_shape=[_sds(it[1].shape) for it in items for _ in range(3)],
        compiler_params=_cparams(1),
    )(*flat)
    return [tuple(out[3 * i:3 * i + 3]) for i in range(n)]


def _mesh_pos():
    x, y, c = lax.axis_index("x"), lax.axis_index("y"), lax.axis_index("c")
    me = 4 * x + 2 * y + c
    peers = []
    for k in range(1, N_DEV):
        peers.append(((1 - x) if (k >> 2) & 1 else x, (1 - y) if (k >> 1) & 1 else y, (1 - c) if k & 1 else c))
    return me, peers


def _all_gather(buf, send_sems, recv_sems, me, peers):
    sends = []
    for k, peer in enumerate(peers):
        cp = pltpu.make_async_remote_copy(src_ref=buf.at[me], dst_ref=buf.at[me], send_sem=send_sems.at[k],
                                          recv_sem=recv_sems.at[k], device_id=peer, device_id_type=MESH)
        cp.start()
        sends.append(cp)
    for k, peer in enumerate(peers):
        src = jnp.bitwise_xor(me, k + 1)
        pltpu.make_async_remote_copy(src_ref=buf.at[src], dst_ref=buf.at[src], send_sem=send_sems.at[k],
                                     recv_sem=recv_sems.at[k], device_id=peer, device_id_type=MESH).wait_recv()
    for cp in sends:
        cp.wait_send()


_VMEM = pl.BlockSpec(memory_space=pltpu.VMEM)
_ANY = pl.BlockSpec(memory_space=pl.ANY)
_SEMS = pltpu.SemaphoreType.DMA((N_DEV - 1,))


def mod_forward(c, c_ctx, w_mod_sh, b_mod):
    ncol = w_mod_sh.shape[1]

    def body(c_ref, cc_ref, w_ref, b_ref, mod_ref, s_ref, cbuf, pbuf, s1, r1, s2, r2):
        me, peers = _mesh_pos()
        cbuf[me] = jnp.broadcast_to(c_ref[...], (8, D))
        _all_gather(cbuf, s1, r1, me, peers)
        rows = [cbuf[j, 0:1, :] for j in range(N_DEV)] + [cc_ref[...], jnp.zeros((7, D), F32)]
        sx = jnp.concatenate(rows, axis=0)
        s = sx * _sigmoid(sx)
        s_ref[...] = s
        pbuf[me] = _hi(s, w_ref[...])
        _all_gather(pbuf, s2, r2, me, peers)
        for j in range(N_DEV):
            mod_ref[:, j * ncol:(j + 1) * ncol] = pbuf[j] + b_ref[:, j * ncol:(j + 1) * ncol]

    return pl.pallas_call(
        body, name="mod_forward",
        in_specs=[_VMEM] * 4, out_specs=[_VMEM] * 2,
        out_shape=[_sds((16, N_DEV * ncol)), _sds((16, D))],
        scratch_shapes=[pltpu.VMEM((N_DEV, 8, D), F32), pltpu.VMEM((N_DEV, 16, ncol), F32), _SEMS, _SEMS, _SEMS,
                        _SEMS],
        compiler_params=pltpu.CompilerParams(vmem_limit_bytes=VMEM_LIMIT),
    )(c, c_ctx, w_mod_sh, b_mod)


def gather_weights(wpack):
    rows = wpack.shape[0]

    def body(w_ref, out_ref, send_sems, recv_sems, local_sem):
        me, peers = _mesh_pos()
        mine = pltpu.make_async_copy(w_ref, out_ref.at[me], local_sem)
        mine.start()
        sends = []
        for k, peer in enumerate(peers):
            cp = pltpu.make_async_remote_copy(src_ref=w_ref, dst_ref=out_ref.at[me], send_sem=send_sems.at[k],
                                              recv_sem=recv_sems.at[k], device_id=peer, device_id_type=MESH)
            cp.start()
            sends.append(cp)
        for k, peer in enumerate(peers):
            src = jnp.bitwise_xor(me, k + 1)
            pltpu.make_async_remote_copy(src_ref=w_ref, dst_ref=out_ref.at[src], send_sem=send_sems.at[k],
                                         recv_sem=recv_sems.at[k], device_id=peer, device_id_type=MESH).wait_recv()
        for cp in sends:
            cp.wait_send()
        mine.wait()

    return pl.pallas_call(
        body, name="gather_weights", in_specs=[_ANY], out_specs=_ANY,
        out_shape=_sds((N_DEV, rows, D), wpack.dtype),
        scratch_shapes=[_SEMS, _SEMS, pltpu.SemaphoreType.DMA],
    )(wpack)


def gather_sum_small(pack):
    shape = pack.shape

    def body(p_ref, tot_ref, gat_ref, send_sems, recv_sems):
        me, peers = _mesh_pos()
        gat_ref[me] = p_ref[...]
        _all_gather(gat_ref, send_sems, recv_sems, me, peers)
        tot = gat_ref[0]
        for j in range(1, N_DEV):
            tot = tot + gat_ref[j]
        tot_ref[...] = tot

    return pl.pallas_call(
        body, name="gather_sum_small", in_specs=[_VMEM], out_specs=[_VMEM, _VMEM],
        out_shape=[_sds(shape), _sds((N_DEV,) + shape)], scratch_shapes=[_SEMS, _SEMS],
        compiler_params=pltpu.CompilerParams(vmem_limit_bytes=VMEM_LIMIT),
    )(pack)


def mod_backward(s, dm_sh, w, m, v, cc, m_cc, v_cc):
    shape = w.shape

    def body(s_ref, dm_ref, w_ref, m_ref, v_ref, cc_ref, mcc_ref, vcc_ref,
             gw_ref, dw_ref, mw_ref, vw_ref, gc_ref, dc_ref, mc_ref, vc_ref, pbuf, send_sems, recv_sems):
        me, peers = _mesh_pos()
        wv = w_ref[...]
        pbuf[me] = _hi_nt(dm_ref[8:16, :], wv)
        _all_gather(pbuf, send_sems, recv_sems, me, peers)
        g = _hi_tn(s_ref[...], dm_ref[...])
        gw_ref[...] = g
        dw_ref[...], mw_ref[...], vw_ref[...] = _adamw(wv, g, m_ref[...], v_ref[...])
        tot = pbuf[0]
        for j in range(1, N_DEV):
            tot = tot + pbuf[j]
        ccv = cc_ref[...]
        sg = _sigmoid(ccv)
        gc = tot[0:1, :] * (sg * (1.0 + ccv * (1.0 - sg)))
        gc_ref[...] = gc
        dc_ref[...], mc_ref[...], vc_ref[...] = _adamw(ccv, gc, mcc_ref[...], vcc_ref[...])

    return pl.pallas_call(
        body, name="mod_backward", in_specs=[_VMEM] * 8, out_specs=[_VMEM] * 8,
        out_shape=[_sds(shape)] * 4 + [_sds((1, D))] * 4,
        scratch_shapes=[pltpu.VMEM((N_DEV, 8, D), F32), _SEMS, _SEMS],
        compiler_params=pltpu.CompilerParams(vmem_limit_bytes=VMEM_LIMIT),
    )(s, dm_sh, w, m, v, cc, m_cc, v_cc)


def reduce_scatter(grads):
    n = len(grads)

    def body(*refs):
        g_refs, r_refs = refs[:n], refs[n:2 * n]
        send_sems, recv_sems = refs[2 * n], refs[2 * n + 1]
        me, peers = _mesh_pos()
        sends = []
        for w in range(n):
            for k, peer in enumerate(peers):
                dst = jnp.bitwise_xor(me, k + 1)
                cp = pltpu.make_async_remote_copy(
                    src_ref=g_refs[w].at[dst], dst_ref=r_refs[w].at[k], send_sem=send_sems.at[w * 7 + k],
                    recv_sem=recv_sems.at[w * 7 + k], device_id=peer, device_id_type=MESH)
                cp.start()
                sends.append(cp)
        for cp in sends:
            cp.wait_recv()
        for cp in sends:
            cp.wait_send()

    return pl.pallas_call(
        body, name="reduce_scatter", in_specs=[_ANY] * n, out_specs=[_ANY] * n,
        out_shape=[_sds((N_DEV - 1,) + g.shape[1:], g.dtype) for g in grads],
        scratch_shapes=[pltpu.SemaphoreType.DMA((7 * n,)), pltpu.SemaphoreType.DMA((7 * n,))],
    )(*grads)


_HBM = pl.BlockSpec(memory_space=pltpu.HBM)
_SEM = pl.BlockSpec(memory_space=pltpu.SEMAPHORE)
_EFFECT = pltpu.SideEffectType.DATAFLOW_SIDE_EFFECTING
_hbm = lambda a: pltpu.with_memory_space_constraint(a, pltpu.HBM)


def gather_start(pack):
    land_shape = (N_DEV,) + pack.shape

    def body(p_ref, land_ref, send_sems, recv_sems, p_thru, land_thru, token):
        me, peers = _mesh_pos()
        for k, peer in enumerate(peers):
            pltpu.make_async_remote_copy(src_ref=p_ref, dst_ref=land_ref.at[me], send_sem=send_sems.at[k],
                                         recv_sem=recv_sems.at[k], device_id=peer, device_id_type=MESH).start()
        token[...] = jnp.zeros_like(token)

    return pl.pallas_call(
        body, name="gather_start",
        out_shape=(_SEMS, _SEMS, pltpu.HBM(pack.shape, pack.dtype), pltpu.HBM(land_shape, pack.dtype),
                   _sds((8, 128))),
        in_specs=(_HBM, _HBM), out_specs=(_SEM, _SEM, _HBM, _HBM, _VMEM), input_output_aliases={0: 2, 1: 3},
        compiler_params=pltpu.CompilerParams(has_side_effects=_EFFECT),
    )(_hbm(pack), _hbm(lax.empty(land_shape, pack.dtype)))


def gather_wait(send_sems, recv_sems, pack, land, after):
    def body(p_ref, land_ref, send_sems, recv_sems, after_ref, p_dead, got_ref):
        me, peers = _mesh_pos()
        for k, peer in enumerate(peers):
            src = jnp.bitwise_xor(me, k + 1)
            cp = pltpu.make_async_remote_copy(src_ref=p_ref, dst_ref=land_ref.at[src], send_sem=send_sems.at[k],
                                              recv_sem=recv_sems.at[k], device_id=peer, device_id_type=MESH)
            cp.wait_send()
            cp.wait_recv()

    return pl.pallas_call(
        body, name="gather_wait",
        out_shape=(pltpu.HBM(pack.shape, pack.dtype), pltpu.HBM(land.shape, land.dtype)),
        in_specs=(_HBM, _HBM, _SEM, _SEM, _ANY), out_specs=(_HBM, _HBM), input_output_aliases={0: 0, 1: 1},
        compiler_params=pltpu.CompilerParams(has_side_effects=_EFFECT),
    )(pack, land, send_sems, recv_sems, after)[1]


def scatter_start(grads):
    n = len(grads)
    sems = pltpu.SemaphoreType.DMA((7 * n,))
    lands = [lax.empty((N_DEV - 1,) + g.shape[1:], g.dtype) for g in grads]

    def body(*refs):
        g_refs, l_refs = refs[:n], refs[n:2 * n]
        send_sems, recv_sems = refs[2 * n], refs[2 * n + 1]
        token = refs[-1]
        me, peers = _mesh_pos()
        for w in range(n):
            for k, peer in enumerate(peers):
                dst = jnp.bitwise_xor(me, k + 1)
                pltpu.make_async_remote_copy(
                    src_ref=g_refs[w].at[dst], dst_ref=l_refs[w].at[k], send_sem=send_sems.at[w * 7 + k],
                    recv_sem=recv_sems.at[w * 7 + k], device_id=peer, device_id_type=MESH).start()
        token[...] = jnp.zeros_like(token)

    out = pl.pallas_call(
        body, name="scatter_start",
        out_shape=(sems, sems) + tuple(pltpu.HBM(a.shape, a.dtype) for a in list(grads) + lands) + (_sds((8, 128)),),
        in_specs=(_HBM,) * (2 * n), out_specs=(_SEM, _SEM) + (_HBM,) * (2 * n) + (_VMEM,),
        input_output_aliases={i: i + 2 for i in range(2 * n)},
        compiler_params=pltpu.CompilerParams(has_side_effects=_EFFECT),
    )(*[_hbm(a) for a in list(grads) + lands])
    return out[0], out[1], list(out[2:2 + n]), list(out[2 + n:2 + 2 * n]), out[-1]


def scatter_wait(send_sems, recv_sems, grads, lands, after):
    n = len(grads)

    def body(*refs):
        g_refs, l_refs = refs[:n], refs[n:2 * n]
        send_sems, recv_sems = refs[2 * n], refs[2 * n + 1]
        me, peers = _mesh_pos()
        for w in range(n):
            for k, peer in enumerate(peers):
                dst = jnp.bitwise_xor(me, k + 1)
                cp = pltpu.make_async_remote_copy(
                    src_ref=g_refs[w].at[dst], dst_ref=l_refs[w].at[k], send_sem=send_sems.at[w * 7 + k],
                    recv_sem=recv_sems.at[w * 7 + k], device_id=peer, device_id_type=MESH)
                cp.wait_send()
                cp.wait_recv()

    out = pl.pallas_call(
        body, name="scatter_wait",
        out_shape=tuple(pltpu.HBM(a.shape, a.dtype) for a in list(grads) + list(lands)),
        in_specs=(_HBM,) * (2 * n) + (_SEM, _SEM, _ANY), out_specs=(_HBM,) * (2 * n),
        input_output_aliases={i: i for i in range(2 * n)},
        compiler_params=pltpu.CompilerParams(has_side_effects=_EFFECT),
    )(*grads, *lands, send_sems, recv_sems, after)
    return list(out[n:2 * n])


def _vec8(rows, width):
    rid = lax.broadcasted_iota(jnp.int32, (8, width), 0)
    out = jnp.zeros((8, width), F32)
    for i, r in enumerate(rows):
        r = r.reshape(-1)
        r = jnp.pad(r, (0, width - r.shape[0]))
        out = jnp.where(rid == i, r[None, :], out)
    return out


def local_step(x, ctx, tgt, mod, mod_c, small, w_int, late_weights, ffn_grads_ready, tt, tt_ctx, cb, cb_ctx):
    sh1, sc1, g1, sh2, sc2, g2 = [mod[i * D:(i + 1) * D] for i in range(6)]
    csh1, csc1 = mod_c[0:D], mod_c[D:2 * D]
    vec1 = _vec8([small["norm1_g"], sh1, sc1], D)
    vec1c = _vec8([small["norm1_g"], csh1, csc1], D)
    vec2 = _vec8([small["norm2_g"], sh2, sc2], D)
    vec3 = _vec8([g2, small["final_g"]], D)
    vecm = _vec8([g1, small["norm2_g"], sh2, sc2], D)
    vcm = _vec8([jnp.tile(small["gla_norm_g"].reshape(HV), NH), small["conv_b"], small["conv_ln_g"],
                 small["conv_ln_b"]], DC)
    convw = jnp.pad(small["conv_w"], ((0, 1), (0, 0)))
    wa = jnp.zeros((128, 512), F32)
    wa = wa.at[0:RANK, 0:DK].set(small["w_a2_f"]).at[RANK:2 * RANK, DK:2 * DK].set(small["w_a2_b"])
    ba = jnp.concatenate([small["b_a_f"].reshape(1, DK), small["b_a_b"].reshape(1, DK)], axis=1)

    _, _, kc, vc_, _, rc, lac, hc = proj_fwd(ctx, vec1c, w_int, wa, ba, tt_ctx)
    qc0 = jnp.zeros_like(kc)
    _, sall_c, sfin_c = gla_fwd(qc0, kc, vc_, lac, jnp.zeros((2, HV, DK), F32), cb_ctx)
    u, q, k, v, g, r, la, h = proj_fwd(x, vec1, w_int, wa, ba, tt)
    o2, sall, _ = gla_fwd(q, k, v, la, sfin_c, cb)
    w_out, wg_t, wu_t, w_down = late_weights(o2)
    x1, cat, mix, yc, h2 = merge_fwd(u, g, o2, x, vecm, vcm, convw, w_out, tt)
    gt, up, hid = ffn_gate_up(h2, wg_t, wu_t, tt)
    dx2, dff, acc3 = ffn_down_loss(hid, x1, tgt, vec3, w_down, tt)
    dgt, dup = ffn_dhid(dff, gt, up, w_down, tt)
    dx1, acc2 = ffn_dh2(dgt, dup, x1, dx2, vec2, wg_t, wu_t, tt)
    bt = min(1024, x.shape[0])
    gw = {"w_down": tn_matmul(hid, dff, FN, bt), "wg_t": tn_matmul(dgt, h2, FN, bt),
          "wu_t": tn_matmul(dup, h2, FN, bt)}
    vecm = vecm + ffn_grads_ready(gw)
    du, dg, do, dmix, accm1, accm2, dconvw = merge_bwd(dx1, mix, u, g, o2, yc, vecm, vcm, convw, w_out, tt)
    dq2, dk2, dv2, dla, ds0 = gla_bwd(q, k, v, la, do, sall, jnp.zeros((2, HV, DK), F32), cb)
    gx, dp, acc1, dba, dwa = proj_bwd(du, dq2, dk2, dv2, dg, dla, la, r, x, dx1, vec1, w_int, wa, tt)
    tcx = ctx.shape[0]
    zc = lambda w: jnp.zeros((tcx, w), F32)
    _, dk2c, dv2c, dlac, _ = gla_bwd(qc0, kc, vc_, lac, zc(DV), sall_c, ds0, cb_ctx)
    z2 = jnp.zeros((2, tcx, DK), F32)
    _, dpc, acc1c, dbac, dwac = proj_bwd(zc(1024), z2, dk2c, dv2c, zc(DV), dlac, lac, rc, ctx, zc(D), vec1c, w_int,
                                         wa, tt_ctx)
    btc = min(1024, tcx)
    gw["w_out"] = tn_matmul(cat, dmix, 512, bt)
    gw["w_int"] = tn_matmul(dp, h, 896, bt, init=tn_matmul(dpc, hc, 896, btc)[0])
    dwa_t = dwa + dwac
    dba_t = dba + dbac
    gs = {
        "norm1_g": acc1[2] + acc1c[2], "norm2_g": acc2[2], "final_g": acc3[1], "loss": acc3[2],
        "gla_norm_g": accm2[0], "conv_b": accm2[1], "conv_ln_g": accm2[2], "conv_ln_b": accm2[3],
        "conv_w": dconvw, "b_a": dba_t[0], "w_a2": dwa_t,
    }
    dmod = _vec8([acc1[0], acc1[1], accm1[0], acc2[0], acc2[1], acc3[0]], D)
    dmod_c = _vec8([acc1c[0], acc1c[1]], D)
    return gx, gw, gs, dmod, dmod_c


PACK_ROWS = 96
ROW_N1, ROW_N2, ROW_FG, ROW_LOSS, ROW_GN, ROW_CB, ROW_LG, ROW_LB, ROW_BA = 0, 1, 2, 3, 4, 5, 6, 7, 8
ROW_DMOD, ROW_DMODC, ROW_CW, ROW_WA = 16, 24, 32, 64


def _pack_small(gs, dmod, dmod_c):
    pad = lambda a: jnp.pad(a, ((0, 0), (0, D - a.shape[1])))
    singles = _vec8([gs["norm1_g"], gs["norm2_g"], gs["final_g"], gs["loss"], gs["gla_norm_g"], gs["conv_b"],
                     gs["conv_ln_g"], gs["conv_ln_b"]], D)
    return jnp.concatenate([singles, _vec8([gs["b_a"]], D), dmod, dmod_c, pad(gs["conv_w"]), pad(gs["w_a2"][0:32])],
                           axis=0)


def kernel(x, c, ctx, c_ctx, w_mod, b_mod, norm1_g, norm2_g, w_in, conv_w, conv_b, conv_ln_g, conv_ln_b, w_a2_f, b_a_f, w_a2_b, b_a_b, gla_norm_g, w_out, w_gate, w_up, w_down, final_g, loss_target, m_c_ctx, m_w_mod, m_b_mod, m_norm1_g, m_norm2_g, m_w_in, m_conv_w, m_conv_b, m_conv_ln_g, m_conv_ln_b, m_w_a2_f, m_b_a_f, m_w_a2_b, m_b_a_b, m_gla_norm_g, m_w_out, m_w_gate, m_w_up, m_w_down, m_final_g, v_c_ctx, v_w_mod, v_b_mod, v_norm1_g, v_norm2_g, v_w_in, v_conv_w, v_conv_b, v_conv_ln_g, v_conv_ln_b, v_w_a2_f, v_b_a_f, v_w_a2_b, v_b_a_b, v_gla_norm_g, v_w_out, v_w_gate, v_w_up, v_w_down, v_final_g):
    me = 4 * lax.axis_index("x") + 2 * lax.axis_index("y") + lax.axis_index("c")
    t = x.shape[1]
    tcx = ctx.shape[1]
    r_in, r_out, r_ff = w_in.shape[2], w_out.shape[1], w_gate.shape[2]
    r_in_b = -(-r_in // 16) * 16

    tb = lambda w: w.T.astype(MXU_DTYPE)
    late_pack = jnp.concatenate([w_out[0].astype(MXU_DTYPE), tb(w_gate[0]), tb(w_up[0]), w_down[0].astype(MXU_DTYPE)],
                                axis=0)
    g_send, g_recv, late_thru, land, g_token = gather_start(late_pack)

    mod_all, s_all = mod_forward(c + g_token[0:1, 0:1], c_ctx.reshape(1, D), w_mod[0], b_mod)
    mod = lax.dynamic_slice(mod_all, (me, 0), (1, 6 * D)).reshape(6 * D)
    mod_c = mod_all[8]

    wall = gather_weights(jnp.pad(tb(w_in[0]), ((0, r_in_b - r_in), (0, 0))))
    w_int = jnp.pad(wall[:, 0:r_in, :].reshape(N_DEV * r_in, D), ((0, DINP - DIN), (0, 0)))

    def late_weights(after):
        got = gather_wait(g_send, g_recv, late_thru, land, after)
        mine = (lax.broadcasted_iota(jnp.int32, (N_DEV, 1, 1), 0) == me)
        full = jnp.where(mine, late_pack[None], got)
        o1, o2_, o3 = r_out, r_out + r_ff, r_out + 2 * r_ff
        return (full[:, 0:o1, :].reshape(N_DEV * r_out, D), full[:, o1:o2_, :].reshape(DFF, D),
                full[:, o2_:o3, :].reshape(DFF, D), full[:, o3:o3 + r_ff, :].reshape(DFF, D))

    scat = {}

    def ffn_grads_ready(gw_ffn):
        wires = [gw_ffn[n][1].reshape(N_DEV, r_ff, D) for n in ("wg_t", "wu_t", "w_down")]
        scat["send"], scat["recv"], scat["grads"], scat["lands"], token = scatter_start(wires)
        return token[0:1, 0:1]

    small = dict(norm1_g=norm1_g[0], norm2_g=norm2_g[0], final_g=final_g, gla_norm_g=gla_norm_g[0],
                 conv_b=conv_b[0], conv_ln_g=conv_ln_g[0], conv_ln_b=conv_ln_b[0], b_a_f=b_a_f[0], b_a_b=b_a_b[0])
    sm_pack = jnp.zeros((PACK_ROWS, D), F32)
    sm_pack = lax.dynamic_update_slice(sm_pack, conv_w[0], (ROW_CW, me * (DC // N_DEV)))
    sm_pack = lax.dynamic_update_slice(sm_pack, w_a2_f[0], (ROW_WA, me * (DK // N_DEV)))
    sm_pack = lax.dynamic_update_slice(sm_pack, w_a2_b[0], (ROW_WA + RANK, DK + me * (DK // N_DEV)))
    sm_tot, _ = gather_sum_small(sm_pack)
    small["conv_w"] = sm_tot[ROW_CW:ROW_CW + CW, 0:DC]
    small["w_a2_f"] = sm_tot[ROW_WA:ROW_WA + RANK, 0:DK]
    small["w_a2_b"] = sm_tot[ROW_WA + RANK:ROW_WA + 2 * RANK, DK:2 * DK]

    gx, gw, gs, dmod, dmod_c = local_step(x[0], ctx[0], loss_target[0], mod, mod_c, small, w_int, late_weights,
                                          ffn_grads_ready, 512, 256, 8, 4)

    tot, gat = gather_sum_small(_pack_small(gs, dmod, dmod_c))
    loss = jnp.sum(tot[ROW_LOSS])
    dm = jnp.concatenate([gat[:, ROW_DMOD:ROW_DMOD + 6, :].reshape(N_DEV, 6 * D),
                          jnp.pad(tot[ROW_DMODC:ROW_DMODC + 6, :].reshape(1, 6 * D), ((0, 7), (0, 0)))], axis=0)
    ncol = w_mod.shape[2]
    dm_sh = lax.dynamic_slice(dm, (0, me * ncol), (16, ncol))
    g_wmod, d_wmod, nm_wmod, nv_wmod, g_cc, d_cc, nm_cc, nv_cc = mod_backward(
        s_all, dm_sh, w_mod[0], m_w_mod[0], v_w_mod[0], c_ctx.reshape(1, D), m_c_ctx.reshape(1, D),
        v_c_ctx.reshape(1, D))

    pad_in = lambda g: jnp.pad(g[0:DIN].reshape(N_DEV, r_in, D), ((0, 0), (0, r_in_b - r_in), (0, 0)))
    recv = list(reduce_scatter([pad_in(gw["w_int"][1]), gw["w_out"][1].reshape(N_DEV, r_out, D)]))
    recv += scatter_wait(scat["send"], scat["recv"], scat["grads"], scat["lands"], recv[0])
    own = [lax.dynamic_index_in_dim(g, me, 0, keepdims=False) for g in
           [pad_in(gw["w_int"][0]), gw["w_out"][0].reshape(N_DEV, r_out, D), gw["wg_t"][0].reshape(N_DEV, r_ff, D),
            gw["wu_t"][0].reshape(N_DEV, r_ff, D), gw["w_down"][0].reshape(N_DEV, r_ff, D)]]
    padt = lambda w: jnp.pad(w.T, ((0, r_in_b - r_in), (0, 0)))
    big = {}
    big["w_in"] = [a[0:r_in].T for a in adamw_sharded(own[0], recv[0], padt(w_in[0]), padt(m_w_in[0]),
                                                       padt(v_w_in[0]))]
    big["w_out"] = adamw_sharded(own[1], recv[1], w_out[0], m_w_out[0], v_w_out[0])
    big["w_gate"] = [a.T for a in adamw_sharded(own[2], recv[2], w_gate[0].T, m_w_gate[0].T, v_w_gate[0].T)]
    big["w_up"] = [a.T for a in adamw_sharded(own[3], recv[3], w_up[0].T, m_w_up[0].T, v_w_up[0].T)]
    big["w_down"] = adamw_sharded(own[4], recv[4], w_down[0], m_w_down[0], v_w_down[0])
    big["w_mod"] = [g_wmod, d_wmod, nm_wmod, nv_wmod]

    row = lambda r, w: tot[r:r + 1, 0:w]
    gn_row = tot[ROW_GN:ROW_GN + 1, 0:DC]
    g_small = {
        "b_mod": jnp.sum(dm, axis=0, keepdims=True),
        "norm1_g": row(ROW_N1, D), "norm2_g": row(ROW_N2, D),
        "conv_w": lax.dynamic_slice(tot, (ROW_CW, me * (DC // N_DEV)), (CW, DC // N_DEV)),
        "conv_b": row(ROW_CB, DC), "conv_ln_g": row(ROW_LG, DC), "conv_ln_b": row(ROW_LB, DC),
        "w_a2_f": lax.dynamic_slice(tot, (ROW_WA, me * (DK // N_DEV)), (RANK, DK // N_DEV)),
        "b_a_f": tot[ROW_BA:ROW_BA + 1, 0:DK],
        "w_a2_b": lax.dynamic_slice(tot, (ROW_WA + RANK, DK + me * (DK // N_DEV)), (RANK, DK // N_DEV)),
        "b_a_b": tot[ROW_BA:ROW_BA + 1, DK:2 * DK],
        "gla_norm_g": gn_row[:, 0:HV] + gn_row[:, HV:2 * HV] + gn_row[:, 2 * HV:3 * HV] + gn_row[:, 3 * HV:4 * HV],
        "final_g": row(ROW_FG, D),
    }
    wmv = {
        "b_mod": (b_mod, m_b_mod, v_b_mod), "norm1_g": (norm1_g, m_norm1_g, v_norm1_g),
        "norm2_g": (norm2_g, m_norm2_g, v_norm2_g), "conv_w": (conv_w[0], m_conv_w[0], v_conv_w[0]),
        "conv_b": (conv_b, m_conv_b, v_conv_b), "conv_ln_g": (conv_ln_g, m_conv_ln_g, v_conv_ln_g),
        "conv_ln_b": (conv_ln_b, m_conv_ln_b, v_conv_ln_b), "w_a2_f": (w_a2_f[0], m_w_a2_f[0], v_w_a2_f[0]),
        "b_a_f": (b_a_f, m_b_a_f, v_b_a_f), "w_a2_b": (w_a2_b[0], m_w_a2_b[0], v_w_a2_b[0]),
        "b_a_b": (b_a_b, m_b_a_b, v_b_a_b), "gla_norm_g": (gla_norm_g, m_gla_norm_g, v_gla_norm_g),
        "final_g": (final_g.reshape(1, D), m_final_g.reshape(1, D), v_final_g.reshape(1, D)),
    }
    names_small = list(g_small)
    upd = adamw_small([(g_small[n],) + wmv[n] for n in names_small])
    res = {n: (g_small[n],) + upd[i] for i, n in enumerate(names_small)}
    res["c_ctx"] = (g_cc, d_cc, nm_cc, nv_cc)
    for n in ("w_mod", "w_in", "w_out", "w_gate", "w_up", "w_down"):
        res[n] = tuple(big[n])

    order = ["c_ctx", "w_mod", "b_mod", "norm1_g", "norm2_g", "w_in", "conv_w", "conv_b", "conv_ln_g", "conv_ln_b",
             "w_a2_f", "b_a_f", "w_a2_b", "b_a_b", "gla_norm_g", "w_out", "w_gate", "w_up", "w_down", "final_g"]
    shapes = {"c_ctx": c_ctx.shape, "w_mod": w_mod.shape, "b_mod": b_mod.shape, "norm1_g": norm1_g.shape,
              "norm2_g": norm2_g.shape, "w_in": w_in.shape, "conv_w": conv_w.shape, "conv_b": conv_b.shape,
              "conv_ln_g": conv_ln_g.shape, "conv_ln_b": conv_ln_b.shape, "w_a2_f": w_a2_f.shape,
              "b_a_f": b_a_f.shape, "w_a2_b": w_a2_b.shape, "b_a_b": b_a_b.shape, "gla_norm_g": gla_norm_g.shape,
              "w_out": w_out.shape, "w_gate": w_gate.shape, "w_up": w_up.shape, "w_down": w_down.shape,
              "final_g": final_g.shape}
    outs = [loss, gx.reshape(x.shape)]
    for i in range(4):
        outs += [res[n][i].reshape(shapes[n]) for n in order]
    return tuple(outs)
```

```python
import functools

import jax
import jax.numpy as jnp
from jax import lax
from jax.experimental import pallas as pl
from jax.experimental.pallas import tpu as pltpu

F32 = jnp.float32
MXU_DTYPE = jnp.bfloat16
WIRE_DTYPE = jnp.bfloat16
HI = lax.Precision.HIGHEST
MESH = pl.DeviceIdType.MESH

N_DEV = 8
D = 1024
DC = 512
NH = 4
HK = 64
HV = 128
DK = NH * HK
DV = NH * HV
RANK = 16
CHUNK = 64
SEG = 64
CW = 31
CPAD = 15
DFF = 2816
DIN = 2592
DINP = 2688
TAU = 16.0
EPS = 1e-6
VMEM_LIMIT = 56 * 1024 * 1024

ADAM_LR = 0.001
ADAM_B1 = 0.9
ADAM_B2 = 0.999
ADAM_EPS = 1e-08
ADAM_WD = 0.01
ADAM_STEP = 10


def _mm(a, b):
    return jnp.dot(a.astype(MXU_DTYPE), b.astype(MXU_DTYPE), preferred_element_type=F32)


def _mm_nt(a, b):
    return lax.dot_general(a.astype(MXU_DTYPE), b.astype(MXU_DTYPE), (((1,), (1,)), ((), ())),
                           preferred_element_type=F32)


def _mm_tn(a, b):
    return lax.dot_general(a.astype(MXU_DTYPE), b.astype(MXU_DTYPE), (((0,), (0,)), ((), ())),
                           preferred_element_type=F32)


def _hi(a, b):
    return jnp.dot(a, b, precision=HI, preferred_element_type=F32)


def _hi_nt(a, b):
    return lax.dot_general(a, b, (((1,), (1,)), ((), ())), precision=HI, preferred_element_type=F32)


def _hi_tn(a, b):
    return lax.dot_general(a, b, (((0,), (0,)), ((), ())), precision=HI, preferred_element_type=F32)


def _sigmoid(x):
    return 1.0 / (1.0 + jnp.exp(-x))


def _cparams(n_axes):
    return pltpu.CompilerParams(dimension_semantics=("arbitrary",) * n_axes, vmem_limit_bytes=VMEM_LIMIT)


def _full(shape):
    n = len(shape)
    return pl.BlockSpec(shape, lambda *_: (0,) * n)


def _rows(tt, width):
    return pl.BlockSpec((tt, width), lambda i: (i, 0))


def _sds(shape, dtype=F32):
    return jax.ShapeDtypeStruct(shape, dtype)


def _norm_mod(x, g, sh, sc):
    r = lax.rsqrt(jnp.mean(x * x, axis=-1, keepdims=True) + EPS)
    xn = x * r
    yy = xn * g
    return r, xn, yy, yy * (1.0 + sc) + sh


def _norm_mod_bwd(dh, r, xn, yy, g, sc):
    dsh = jnp.sum(dh, axis=0, keepdims=True)
    dsc = jnp.sum(dh * yy, axis=0, keepdims=True)
    dy = dh * (1.0 + sc)
    dg = jnp.sum(dy * xn, axis=0, keepdims=True)
    dxn = dy * g
    dx = r * (dxn - xn * jnp.mean(dxn * xn, axis=-1, keepdims=True))
    return dsh, dsc, dg, dx


def _acc_rows(ref, first, rows):
    upd = jnp.concatenate(rows + [jnp.zeros((8 - len(rows), rows[0].shape[1]), F32)], axis=0)

    @pl.when(first)
    def _():
        ref[...] = upd

    @pl.when(jnp.logical_not(first))
    def _():
        ref[...] += upd


def _acc(ref, first, val):
    @pl.when(first)
    def _():
        ref[...] = val

    @pl.when(jnp.logical_not(first))
    def _():
        ref[...] += val


def proj_fwd(x, vec, w_int, wa, ba, tt):
    t = x.shape[0]

    def body(x_ref, vec_ref, w_ref, wa_ref, ba_ref, u_ref, q_ref, k_ref, v_ref, g_ref, r_ref, la_ref, h_ref):
        _, _, _, h = _norm_mod(x_ref[...], vec_ref[0:1, :], vec_ref[1:2, :], vec_ref[2:3, :])
        hb = h.astype(MXU_DTYPE)
        h_ref[...] = hb
        p = _mm_nt(hb, w_ref[...])
        u_ref[...] = p[:, 0:1024]
        q_ref[...] = p[:, 1024:1280]
        k_ref[...] = p[:, 1280:1536]
        v_ref[...] = p[:, 1536:2048]
        g_ref[...] = p[:, 2048:2560]
        rr = p[:, 2560:2688]
        r_ref[...] = rr
        z = _hi(rr, wa_ref[...]) + ba_ref[...]
        la_ref[...] = (jnp.minimum(z, 0.0) - jnp.log(1.0 + jnp.exp(-jnp.abs(z)))) * (1.0 / TAU)

    return pl.pallas_call(
        body, name="proj_fwd", grid=(t // tt,),
        in_specs=[_rows(tt, D), _full((8, D)), _full((DINP, D)), _full((128, 512)), _full((1, 512))],
        out_specs=[_rows(tt, 1024), _rows(tt, DK), _rows(tt, DK), _rows(tt, DV), _rows(tt, DV), _rows(tt, 128),
                   _rows(tt, 512), _rows(tt, D)],
        out_shape=[_sds((t, 1024)), _sds((t, DK)), _sds((t, DK)), _sds((t, DV)), _sds((t, DV)), _sds((t, 128)),
                   _sds((t, 512)), _sds((t, D), MXU_DTYPE)],
        compiler_params=_cparams(1),
    )(x, vec, w_int, wa, ba)


def proj_bwd(du, dq2, dk2, dv2, dg, dla, la, r, x, dx1, vec, w_int, wa, tt):
    t = x.shape[0]

    def body(du_ref, dq_ref, dk_ref, dv_ref, dg_ref, dla_ref, la_ref, r_ref, x_ref, dx1_ref, vec_ref, w_ref, wa_ref,
             gx_ref, dp_ref, acc_ref, dba_ref, dwa_ref):
        first = pl.program_id(0) == 0
        dz = dla_ref[...] * (1.0 - jnp.exp(TAU * la_ref[...])) * (1.0 / TAU)
        rr = r_ref[...]
        _acc_rows(dba_ref, first, [jnp.sum(dz, axis=0, keepdims=True)])
        _acc(dwa_ref, first, _hi_tn(rr, dz))
        dr = _hi_nt(dz, wa_ref[...])
        md = lambda a: a.astype(MXU_DTYPE)
        dp = jnp.concatenate([du_ref[...], md(dq_ref[0] + dq_ref[1]), md(dk_ref[0] + dk_ref[1]),
                              md(dv_ref[0] + dv_ref[1]), dg_ref[...], md(dr)], axis=1)
        dp_ref[...] = dp
        dh = _mm(dp, w_ref[...])
        g, sc = vec_ref[0:1, :], vec_ref[2:3, :]
        rn, xn, yy, _ = _norm_mod(x_ref[...], g, vec_ref[1:2, :], sc)
        dsh, dsc, dgn, dx = _norm_mod_bwd(dh, rn, xn, yy, g, sc)
        gx_ref[...] = dx1_ref[...] + dx
        _acc_rows(acc_ref, first, [dsh, dsc, dgn])

    two = lambda w: pl.BlockSpec((2, tt, w), lambda i: (0, i, 0))
    return pl.pallas_call(
        body, name="proj_bwd", grid=(t // tt,),
        in_specs=[_rows(tt, 1024), two(DK), two(DK), two(DV), _rows(tt, DV), _rows(tt, 512), _rows(tt, 512),
                  _rows(tt, 128), _rows(tt, D), _rows(tt, D), _full((8, D)), _full((DINP, D)), _full((128, 512))],
        out_specs=[_rows(tt, D), _rows(tt, DINP), _full((8, D)), _full((8, 512)), _full((128, 512))],
        out_shape=[_sds((t, D)), _sds((t, DINP), MXU_DTYPE), _sds((8, D)), _sds((8, 512)), _sds((128, 512))],
        compiler_params=_cparams(1),
    )(du, dq2, dk2, dv2, dg, dla, la, r, x, dx1, vec, w_int, wa)


def _dot_exact01(m01, x):
    bf = jnp.bfloat16
    w = x.shape[1]
    hi = x.astype(bf)
    r1 = x - hi.astype(F32)
    mid = r1.astype(bf)
    lo = (r1 - mid.astype(F32)).astype(bf)
    y = jnp.dot(m01.astype(bf), jnp.concatenate([hi, mid, lo], axis=1), preferred_element_type=F32)
    return y[:, 0:w] + y[:, w:2 * w] + y[:, 2 * w:3 * w]


def _gla_chunk(d, qc, kc, la_c):
    row = lax.broadcasted_iota(jnp.int32, (CHUNK, CHUNK), 0)
    col = lax.broadcasted_iota(jnp.int32, (CHUNK, CHUNK), 1)
    cum = jnp.where(d == 0, (col <= row).astype(F32), (col >= row).astype(F32))
    cum_t = jnp.where(d == 0, (col >= row).astype(F32), (col <= row).astype(F32))
    cum4 = jnp.concatenate([cum] * NH, axis=0)
    head_of_lane = lax.broadcasted_iota(jnp.int32, (1, DK), 1) // HK
    b = _dot_exact01(cum, la_c)
    bl = jnp.sum(la_c, axis=0, keepdims=True)
    eb = jnp.exp(b)
    enb = jnp.exp(-b)
    ekd = jnp.exp(bl - b)
    qt = qc * (HK ** -0.5) * eb
    kt = kc * enb
    kd = kc * ekd
    qst = jnp.concatenate([jnp.where(head_of_lane == h, qt, 0.0) for h in range(NH)], axis=0)
    a = _mm_nt(qst, kt) * cum4
    return cum_t, cum4, head_of_lane, eb, enb, ekd, qt, kt, kd, qst, a, jnp.exp(bl)


def gla_fwd(q, k, v, la, s0, cb):
    t = q.shape[0]
    nc = t // CHUNK
    nb = nc // cb
    blk = lambda d, i: i + d * (nb - 1 - 2 * i)

    def body(q_ref, k_ref, v_ref, la_ref, s0_ref, o_ref, sall_ref, sfin_ref, s_scr):
        d = pl.program_id(0)
        i = pl.program_id(1)

        @pl.when(i == 0)
        def _():
            s_scr[...] = s0_ref[0]

        for j in range(cb):
            jj = j + d * (cb - 1 - 2 * j)
            rows = pl.ds(pl.multiple_of(jj * CHUNK, CHUNK), CHUNK)
            vc = v_ref[rows, :]
            _, _, head_of_lane, _, _, _, _, _, kd, qst, a, dec = _gla_chunk(
                d, q_ref[rows, :], k_ref[rows, :], la_ref[rows, :])
            s = s_scr[...]
            sall_ref[0, jj] = s
            inter = _mm_nt(qst, s)
            outs = []
            for h in range(NH):
                hs = slice(h * CHUNK, (h + 1) * CHUNK)
                outs.append(_mm(a[hs], vc[:, h * HV:(h + 1) * HV]) + inter[hs])
            o_ref[0, rows, :] = jnp.concatenate(outs, axis=1)
            kv = _mm_tn(vc, kd)
            s_new = dec * s
            for h in range(NH):
                s_new = s_new + jnp.where(head_of_lane == h, kv[h * HV:(h + 1) * HV], 0.0)
            s_scr[...] = s_new

        @pl.when(i == nb - 1)
        def _():
            sfin_ref[0] = s_scr[...]

    tb = cb * CHUNK
    return pl.pallas_call(
        body, name="gla_fwd", grid=(2, nb),
        in_specs=[pl.BlockSpec((tb, DK), lambda d, i: (blk(d, i), 0)),
                  pl.BlockSpec((tb, DK), lambda d, i: (blk(d, i), 0)),
                  pl.BlockSpec((tb, DV), lambda d, i: (blk(d, i), 0)),
                  pl.BlockSpec((tb, DK), lambda d, i: (blk(d, i), d)),
                  pl.BlockSpec((1, HV, DK), lambda d, i: (d, 0, 0))],
        out_specs=[pl.BlockSpec((1, tb, DV), lambda d, i: (d, blk(d, i), 0)),
                   pl.BlockSpec((1, cb, HV, DK), lambda d, i: (d, blk(d, i), 0, 0)),
                   pl.BlockSpec((1, HV, DK), lambda d, i: (d, 0, 0))],
        out_shape=[_sds((2, t, DV)), _sds((2, nc, HV, DK)), _sds((2, HV, DK))],
        scratch_shapes=[pltpu.VMEM((HV, DK), F32)],
        compiler_params=_cparams(2),
    )(q, k, v, la, s0)


def gla_bwd(q, k, v, la, do, sall, dsfin, cb):
    t = q.shape[0]
    nc = t // CHUNK
    nb = nc // cb
    blk = lambda d, i: (nb - 1 - i) + d * (2 * i - (nb - 1))

    def body(q_ref, k_ref, v_ref, la_ref, do_ref, sall_ref, dsfin_ref, dq_ref, dk_ref, dv_ref, dla_ref, ds0_ref,
             ds_scr):
        d = pl.program_id(0)
        i = pl.program_id(1)

        @pl.when(i == 0)
        def _():
            ds_scr[...] = dsfin_ref[0]

        for j in range(cb):
            jj = (cb - 1 - j) + d * (2 * j - (cb - 1))
            rows = pl.ds(pl.multiple_of(jj * CHUNK, CHUNK), CHUNK)
            vc = v_ref[rows, :]
            doc = do_ref[rows, :]
            cum_t, cum4, head_of_lane, eb, enb, ekd, qt, kt, kd, qst, a, dec = _gla_chunk(
                d, q_ref[rows, :], k_ref[rows, :], la_ref[rows, :])
            s = sall_ref[0, jj]
            ds = ds_scr[...]
            hv = lambda x, h: x[:, h * HV:(h + 1) * HV]
            hr = lambda x, h: x[h * CHUNK:(h + 1) * CHUNK]
            fold = lambda x: functools.reduce(
                lambda p, c: p + c, [jnp.where(head_of_lane == h, hr(x, h), 0.0) for h in range(NH)])
            dost = jnp.concatenate([hv(doc, h) for h in range(NH)], axis=0)
            vst = jnp.concatenate([hv(vc, h) for h in range(NH)], axis=0)
            da = jnp.concatenate([_mm_nt(hv(doc, h), hv(vc, h)) for h in range(NH)], axis=0) * cum4
            dqt = fold(_mm(da, kt) + _mm(dost, s))
            dkt = _mm_tn(da, qst)
            kdst = jnp.concatenate([jnp.where(head_of_lane == h, kd, 0.0) for h in range(NH)], axis=0)
            dv_inter = _mm_nt(kdst, ds)
            dv_ref[0, rows, :] = jnp.concatenate(
                [_mm_tn(hr(a, h), hv(doc, h)) + hr(dv_inter, h) for h in range(NH)], axis=1)
            dkd = fold(_mm(vst, ds))
            ds_scr[...] = dec * ds + _mm_tn(dost, qst)
            tkd = dkd * kd
            db = dqt * qt - dkt * kt - tkd
            dbl = jnp.sum(ds * s, axis=0, keepdims=True) * dec + jnp.sum(tkd, axis=0, keepdims=True)
            dla_ref[rows, :] = _dot_exact01(cum_t, db) + dbl
            dq_ref[0, rows, :] = dqt * eb * (HK ** -0.5)
            dk_ref[0, rows, :] = dkt * enb + dkd * ekd

        @pl.when(i == nb - 1)
        def _():
            ds0_ref[0] = ds_scr[...]

    tb = cb * CHUNK
    return pl.pallas_call(
        body, name="gla_bwd", grid=(2, nb),
        in_specs=[pl.BlockSpec((tb, DK), lambda d, i: (blk(d, i), 0)),
                  pl.BlockSpec((tb, DK), lambda d, i: (blk(d, i), 0)),
                  pl.BlockSpec((tb, DV), lambda d, i: (blk(d, i), 0)),
                  pl.BlockSpec((tb, DK), lambda d, i: (blk(d, i), d)),
                  pl.BlockSpec((tb, DV), lambda d, i: (blk(d, i), 0)),
                  pl.BlockSpec((1, cb, HV, DK), lambda d, i: (d, blk(d, i), 0, 0)),
                  pl.BlockSpec((1, HV, DK), lambda d, i: (d, 0, 0))],
        out_specs=[pl.BlockSpec((1, tb, DK), lambda d, i: (d, blk(d, i), 0)),
                   pl.BlockSpec((1, tb, DK), lambda d, i: (d, blk(d, i), 0)),
                   pl.BlockSpec((1, tb, DV), lambda d, i: (d, blk(d, i), 0)),
                   pl.BlockSpec((tb, DK), lambda d, i: (blk(d, i), d)),
                   pl.BlockSpec((1, HV, DK), lambda d, i: (d, 0, 0))],
        out_shape=[_sds((2, t, DK)), _sds((2, t, DK)), _sds((2, t, DV)), _sds((t, 2 * DK)), _sds((2, HV, DK))],
        scratch_shapes=[pltpu.VMEM((HV, DK), F32)],
        compiler_params=_cparams(2),
    )(q, k, v, la, do, sall, dsfin)


def _seg_pos(tt):
    return lax.broadcasted_iota(jnp.int32, (tt, 1), 0) % SEG


def _shifted(x, s, pos, tt):
    y = x if s == 0 else pltpu.roll(x, (-s) % tt, 0)
    return jnp.where((pos + s >= 0) & (pos + s < SEG), y, 0.0)


def _head_norm(o, gn):
    rs, xs = [], []
    for h in range(NH):
        oh = o[:, h * HV:(h + 1) * HV]
        r = lax.rsqrt(jnp.mean(oh * oh, axis=-1, keepdims=True) + EPS)
        rs.append(r)
        xs.append(oh * r)
    return rs, xs


def merge_fwd(u, g, o2, x, vec, vc, convw, w_out, tt):
    t = x.shape[0]

    def body(u_ref, g_ref, o_ref, x_ref, vec_ref, vc_ref, cw_ref, w_ref, x1_ref, cat_ref, mix_ref, yc_ref, h2_ref):
        a = u_ref[:, 0:DC]
        gate = u_ref[:, DC:2 * DC]
        vv = a * _sigmoid(gate)
        pos = _seg_pos(tt)
        cw = cw_ref[...]
        yc = jnp.zeros((tt, DC), F32) + vc_ref[1:2, :]
        for j in range(CW):
            yc = yc + _shifted(vv, j - CPAD, pos, tt) * cw[j:j + 1, :]
        yc_ref[...] = yc
        mu = jnp.mean(yc, axis=-1, keepdims=True)
        yd = yc - mu
        rs = lax.rsqrt(jnp.mean(yd * yd, axis=-1, keepdims=True) + EPS)
        ln = yd * rs * vc_ref[2:3, :] + vc_ref[3:4, :]
        conv_o = ln * _sigmoid(ln)
        o = o_ref[0] + o_ref[1]
        _, xs = _head_norm(o, None)
        gg = g_ref[...]
        o2g = jnp.concatenate(xs, axis=1) * vc_ref[0:1, :] * (gg * _sigmoid(gg))
        cat = jnp.concatenate([conv_o, o2g], axis=1).astype(MXU_DTYPE)
        cat_ref[...] = cat
        mix = _mm(cat, w_ref[...])
        mix_ref[...] = mix
        x1 = x_ref[...] + vec_ref[0:1, :] * mix
        x1_ref[...] = x1
        _, _, _, h2 = _norm_mod(x1, vec_ref[1:2, :], vec_ref[2:3, :], vec_ref[3:4, :])
        h2_ref[...] = h2.astype(MXU_DTYPE)

    return pl.pallas_call(
        body, name="merge_fwd", grid=(t // tt,),
        in_specs=[_rows(tt, 1024), _rows(tt, DV), pl.BlockSpec((2, tt, DV), lambda i: (0, i, 0)), _rows(tt, D),
                  _full((8, D)), _full((8, DC)), _full((32, DC)), _full((D, D))],
        out_specs=[_rows(tt, D), _rows(tt, D), _rows(tt, D), _rows(tt, DC), _rows(tt, D)],
        out_shape=[_sds((t, D)), _sds((t, D), MXU_DTYPE), _sds((t, D)), _sds((t, DC)), _sds((t, D), MXU_DTYPE)],
        compiler_params=_cparams(1),
    )(u, g, o2, x, vec, vc, convw, w_out)


def merge_bwd(dx1, mix, u, g, o2, yc, vec, vc, convw, w_out, tt):
    t = dx1.shape[0]

    def body(dx1_ref, mix_ref, u_ref, g_ref, o_ref, yc_ref, vec_ref, vc_ref, cw_ref, w_ref,
             du_ref, dg_ref, do_ref, dmix_ref, acc1_ref, acc2_ref, dcw_ref):
        first = pl.program_id(0) == 0
        dx1v = dx1_ref[...]
        dg1 = jnp.sum(dx1v * mix_ref[...], axis=0, keepdims=True)
        dmix = (vec_ref[0:1, :] * dx1v).astype(MXU_DTYPE)
        dmix_ref[...] = dmix
        dcat = _mm_nt(dmix, w_ref[...])
        dconv_o = dcat[:, 0:DC]
        do2 = dcat[:, DC:2 * DC]
        gn = vc_ref[0:1, :]
        o = o_ref[0] + o_ref[1]
        rs, xs = _head_norm(o, None)
        xn = jnp.concatenate(xs, axis=1)
        gg = g_ref[...]
        sg = _sigmoid(gg)
        don = do2 * (gg * sg)
        dg_ref[...] = (do2 * (xn * gn) * (sg * (1.0 + gg * (1.0 - sg)))).astype(MXU_DTYPE)
        dgn = jnp.sum(don * xn, axis=0, keepdims=True)
        dxn = don * gn
        dos = []
        for h in range(NH):
            dh = dxn[:, h * HV:(h + 1) * HV]
            dos.append(rs[h] * (dh - xs[h] * jnp.mean(dh * xs[h], axis=-1, keepdims=True)))
        do_ref[...] = jnp.concatenate(dos, axis=1).astype(MXU_DTYPE)
        yc = yc_ref[...]
        mu = jnp.mean(yc, axis=-1, keepdims=True)
        yd = yc - mu
        rstd = lax.rsqrt(jnp.mean(yd * yd, axis=-1, keepdims=True) + EPS)
        yhat = yd * rstd
        lg = vc_ref[2:3, :]
        ln = yhat * lg + vc_ref[3:4, :]
        sl = _sigmoid(ln)
        dln = dconv_o * (sl * (1.0 + ln * (1.0 - sl)))
        dlb = jnp.sum(dln, axis=0, keepdims=True)
        dlg = jnp.sum(dln * yhat, axis=0, keepdims=True)
        dyh = dln * lg
        dyc = rstd * (dyh - jnp.mean(dyh, axis=-1, keepdims=True)
                      - yhat * jnp.mean(dyh * yhat, axis=-1, keepdims=True))
        dcb = jnp.sum(dyc, axis=0, keepdims=True)
        a = u_ref[:, 0:DC]
        gate = u_ref[:, DC:2 * DC]
        sgt = _sigmoid(gate)
        vv = a * sgt
        pos = _seg_pos(tt)
        cw = cw_ref[...]
        dvv = jnp.zeros((tt, DC), F32)
        dws = []
        for j in range(CW):
            s = j - CPAD
            dvv = dvv + _shifted(dyc, -s, pos, tt) * cw[j:j + 1, :]
            dws.append(jnp.sum(dyc * _shifted(vv, s, pos, tt), axis=0, keepdims=True))
        dws.append(jnp.zeros((1, DC), F32))
        du_ref[:, 0:DC] = (dvv * sgt).astype(MXU_DTYPE)
        du_ref[:, DC:2 * DC] = (dvv * a * sgt * (1.0 - sgt)).astype(MXU_DTYPE)
        _acc_rows(acc1_ref, first, [dg1])
        _acc_rows(acc2_ref, first, [dgn, dcb, dlg, dlb])
        _acc(dcw_ref, first, jnp.concatenate(dws, axis=0))

    return pl.pallas_call(
        body, name="merge_bwd", grid=(t // tt,),
        in_specs=[_rows(tt, D), _rows(tt, D), _rows(tt, 1024), _rows(tt, DV),
                  pl.BlockSpec((2, tt, DV), lambda i: (0, i, 0)), _rows(tt, DC),
                  _full((8, D)), _full((8, DC)), _full((32, DC)), _full((D, D))],
        out_specs=[_rows(tt, 1024), _rows(tt, DV), _rows(tt, DV), _rows(tt, D), _full((8, D)), _full((8, DC)),
                   _full((32, DC))],
        out_shape=[_sds((t, 1024), MXU_DTYPE), _sds((t, DV), MXU_DTYPE), _sds((t, DV), MXU_DTYPE),
                   _sds((t, D), MXU_DTYPE), _sds((8, D)),
                   _sds((8, DC)), _sds((32, DC))],
        compiler_params=_cparams(1),
    )(dx1, mix, u, g, o2, yc, vec, vc, convw, w_out)


FN = DFF // 2


def ffn_gate_up(h2, wg_t, wu_t, tt):
    t = h2.shape[0]

    def body(h2_ref, wg_ref, wu_ref, s_ref, d_ref, hid_ref):
        h2v = h2_ref[...]
        gt = _mm_nt(h2v, wg_ref[...])
        up = _mm_nt(h2v, wu_ref[...])
        sg = _sigmoid(gt)
        act = gt * sg
        s_ref[...] = act.astype(MXU_DTYPE)
        d_ref[...] = (up * (sg * (1.0 + gt * (1.0 - sg)))).astype(MXU_DTYPE)
        hid_ref[...] = (act * up).astype(MXU_DTYPE)

    blk = pl.BlockSpec((tt, FN), lambda j, i: (i, j))
    wblk = pl.BlockSpec((FN, D), lambda j, i: (j, 0))
    return pl.pallas_call(
        body, name="ffn_gate_up", grid=(2, t // tt),
        in_specs=[pl.BlockSpec((tt, D), lambda j, i: (i, 0)), wblk, wblk],
        out_specs=[blk, blk, blk],
        out_shape=[_sds((t, DFF), MXU_DTYPE)] * 3,
        compiler_params=_cparams(2),
    )(h2, wg_t, wu_t)


def ffn_down_loss(hid, x1, tgt, vec, w_down, tt):
    t = x1.shape[0]

    def body(hid_ref, x1_ref, tgt_ref, vec_ref, w_ref, dx2_ref, dff_ref, acc_ref):
        first = pl.program_id(0) == 0
        g2 = vec_ref[0:1, :]
        fg = vec_ref[1:2, :]
        ff = _mm(hid_ref[...], w_ref[...])
        x2 = x1_ref[...] + g2 * ff
        rf = lax.rsqrt(jnp.mean(x2 * x2, axis=-1, keepdims=True) + EPS)
        xn = x2 * rf
        err = xn * fg - tgt_ref[...]
        dy = err * (1.0 / D)
        dfg = jnp.sum(dy * xn, axis=0, keepdims=True)
        dxn = dy * fg
        dx2 = rf * (dxn - xn * jnp.mean(dxn * xn, axis=-1, keepdims=True))
        dx2_ref[...] = dx2
        dff_ref[...] = (g2 * dx2).astype(MXU_DTYPE)
        dg2 = jnp.sum(dx2 * ff, axis=0, keepdims=True)
        loss = jnp.sum(err * err, axis=0, keepdims=True) * (0.5 / D)
        _acc_rows(acc_ref, first, [dg2, dfg, loss])

    return pl.pallas_call(
        body, name="ffn_down_loss", grid=(t // tt,),
        in_specs=[_rows(tt, DFF), _rows(tt, D), _rows(tt, D), _full((8, D)), _full((DFF, D))],
        out_specs=[_rows(tt, D), _rows(tt, D), _full((8, D))],
        out_shape=[_sds((t, D)), _sds((t, D), MXU_DTYPE), _sds((8, D))],
        compiler_params=_cparams(1),
    )(hid, x1, tgt, vec, w_down)


def ffn_dhid(dff, s, d, w_down, tt):
    t = dff.shape[0]

    def body(dff_ref, s_ref, d_ref, w_ref, dgt_ref, dup_ref):
        dhid = _mm_nt(dff_ref[...], w_ref[...])
        dgt_ref[...] = (dhid * d_ref[...].astype(F32)).astype(MXU_DTYPE)
        dup_ref[...] = (dhid * s_ref[...].astype(F32)).astype(MXU_DTYPE)

    blk = pl.BlockSpec((tt, FN), lambda j, i: (i, j))
    return pl.pallas_call(
        body, name="ffn_dhid", grid=(2, t // tt),
        in_specs=[pl.BlockSpec((tt, D), lambda j, i: (i, 0)), blk, blk, pl.BlockSpec((FN, D), lambda j, i: (j, 0))],
        out_specs=[blk, blk],
        out_shape=[_sds((t, DFF), MXU_DTYPE), _sds((t, DFF), MXU_DTYPE)],
        compiler_params=_cparams(2),
    )(dff, s, d, w_down)


def ffn_dh2(dgt, dup, x1, dx2, vec, wg_t, wu_t, tt):
    t = x1.shape[0]

    def body(dgt_ref, dup_ref, x1_ref, dx2_ref, vec_ref, wg_ref, wu_ref, dx1_ref, acc_ref):
        first = pl.program_id(0) == 0
        dh2 = _mm(dgt_ref[...], wg_ref[...]) + _mm(dup_ref[...], wu_ref[...])
        g, sc = vec_ref[0:1, :], vec_ref[2:3, :]
        r, xn, yy, _ = _norm_mod(x1_ref[...], g, vec_ref[1:2, :], sc)
        dsh, dsc, dgn, dx = _norm_mod_bwd(dh2, r, xn, yy, g, sc)
        dx1_ref[...] = dx2_ref[...] + dx
        _acc_rows(acc_ref, first, [dsh, dsc, dgn])

    return pl.pallas_call(
        body, name="ffn_dh2", grid=(t // tt,),
        in_specs=[_rows(tt, DFF), _rows(tt, DFF), _rows(tt, D), _rows(tt, D), _full((8, D)), _full((DFF, D)),
                  _full((DFF, D))],
        out_specs=[_rows(tt, D), _full((8, D))],
        out_shape=[_sds((t, D)), _sds((8, D))],
        compiler_params=_cparams(1),
    )(dgt, dup, x1, dx2, vec, wg_t, wu_t)


def tn_matmul(a, b, bm, bt, init=None):
    t, m = a.shape
    n = b.shape[1]
    nk = t // bt

    def body(*refs):
        if init is None:
            a_ref, b_ref, o_ref, wire_ref = refs
        else:
            a_ref, b_ref, i_ref, o_ref, wire_ref = refs
        prod = _mm_tn(a_ref[...], b_ref[...])

        @pl.when(pl.program_id(1) == 0)
        def _():
            o_ref[...] = prod if init is None else prod + i_ref[...]

        @pl.when(pl.program_id(1) != 0)
        def _():
            o_ref[...] += prod

        @pl.when(pl.program_id(1) == nk - 1)
        def _():
            wire_ref[...] = o_ref[...].astype(WIRE_DTYPE)

    in_specs = [pl.BlockSpec((bt, bm), lambda i, k: (k, i)), pl.BlockSpec((bt, n), lambda i, k: (k, 0))]
    args = [a, b]
    if init is not None:
        in_specs.append(pl.BlockSpec((bm, n), lambda i, k: (i, 0)))
        args.append(init)
    oblk = pl.BlockSpec((bm, n), lambda i, k: (i, 0))
    return pl.pallas_call(
        body, name="tn_matmul", grid=(m // bm, nk),
        in_specs=in_specs, out_specs=[oblk, oblk],
        out_shape=[_sds((m, n)), _sds((m, n), WIRE_DTYPE)], compiler_params=_cparams(2),
    )(*args)


def _adamw(w, g, m, v):
    m = ADAM_B1 * m + (1.0 - ADAM_B1) * g
    v = ADAM_B2 * v + (1.0 - ADAM_B2) * (g * g)
    m_hat = m / (1.0 - ADAM_B1 ** ADAM_STEP)
    v_hat = v / (1.0 - ADAM_B2 ** ADAM_STEP)
    delta = -ADAM_LR * (m_hat / (jnp.sqrt(v_hat) + ADAM_EPS) + ADAM_WD * w)
    return delta, m, v


def adamw_sharded(own, recv, w, m, v):
    shape = w.shape

    def body(own_ref, recv_ref, w_ref, m_ref, v_ref, g_ref, d_ref, mo_ref, vo_ref):
        g = own_ref[...]
        for k in range(N_DEV - 1):
            g = g + recv_ref[k].astype(F32)
        g_ref[...] = g
        d_ref[...], mo_ref[...], vo_ref[...] = _adamw(w_ref[...], g, m_ref[...], v_ref[...])

    return pl.pallas_call(
        body, name="adamw_sharded",
        in_specs=[_full(shape), _full((N_DEV - 1,) + shape), _full(shape), _full(shape), _full(shape)],
        out_specs=[_full(shape)] * 4, out_shape=[_sds(shape)] * 4, grid=(1,),
        compiler_params=_cparams(1),
    )(own, recv, w, m, v)


def adamw_small(items):
    n = len(items)
    flat = [a for it in items for a in it]

    def body(*refs):
        ins, outs = refs[:4 * n], refs[4 * n:]
        for i in range(n):
            g, w, m, v = (r[...] for r in ins[4 * i:4 * i + 4])
            outs[3 * i][...], outs[3 * i + 1][...], outs[3 * i + 2][...] = _adamw(w, g, m, v)

    out = pl.pallas_call(
        body, name="adamw_small", grid=(1,),
        in_specs=[_full(a.shape) for a in flat],
        out_specs=[_full(it[1].shape) for it in items for _ in range(3)],
        out_shape=[_sds(it[1].shape) for it in items for _ in range(3)],
        compiler_params=_cparams(1),
    )(*flat)
    return [tuple(out[3 * i:3 * i + 3]) for i in range(n)]


def _mesh_pos():
    x, y, c = lax.axis_index("x"), lax.axis_index("y"), lax.axis_index("c")
    me = 4 * x + 2 * y + c
    peers = []
    for k in range(1, N_DEV):
        peers.append(((1 - x) if (k >> 2) & 1 else x, (1 - y) if (k >> 1) & 1 else y, (1 - c) if k & 1 else c))
    return me, peers


def _all_gather(buf, send_sems, recv_sems, me, peers):
    sends = []
    for k, peer in enumerate(peers):
        cp = pltpu.make_async_remote_copy(src_ref=buf.at[me], dst_ref=buf.at[me], send_sem=send_sems.at[k],
                                          recv_sem=recv_sems.at[k], device_id=peer, device_id_type=MESH)
        cp.start()
        sends.append(cp)
    for k, peer in enumerate(peers):
        src = jnp.bitwise_xor(me, k + 1)
        pltpu.make_async_remote_copy(src_ref=buf.at[src], dst_ref=buf.at[src], send_sem=send_sems.at[k],
                                     recv_sem=recv_sems.at[k], device_id=peer, device_id_type=MESH).wait_recv()
    for cp in sends:
        cp.wait_send()


_VMEM = pl.BlockSpec(memory_space=pltpu.VMEM)
_ANY = pl.BlockSpec(memory_space=pl.ANY)
_SEMS = pltpu.SemaphoreType.DMA((N_DEV - 1,))


def mod_forward(c, c_ctx, w_mod_sh, b_mod):
    ncol = w_mod_sh.shape[1]

    def body(c_ref, cc_ref, w_ref, b_ref, mod_ref, s_ref, cbuf, pbuf, s1, r1, s2, r2):
        me, peers = _mesh_pos()
        cbuf[me] = jnp.broadcast_to(c_ref[...], (8, D))
        _all_gather(cbuf, s1, r1, me, peers)
        rows = [cbuf[j, 0:1, :] for j in range(N_DEV)] + [cc_ref[...], jnp.zeros((7, D), F32)]
        sx = jnp.concatenate(rows, axis=0)
        s = sx * _sigmoid(sx)
        s_ref[...] = s
        pbuf[me] = _hi(s, w_ref[...])
        _all_gather(pbuf, s2, r2, me, peers)
        for j in range(N_DEV):
            mod_ref[:, j * ncol:(j + 1) * ncol] = pbuf[j] + b_ref[:, j * ncol:(j + 1) * ncol]

    return pl.pallas_call(
        body, name="mod_forward",
        in_specs=[_VMEM] * 4, out_specs=[_VMEM] * 2,
        out_shape=[_sds((16, N_DEV * ncol)), _sds((16, D))],
        scratch_shapes=[pltpu.VMEM((N_DEV, 8, D), F32), pltpu.VMEM((N_DEV, 16, ncol), F32), _SEMS, _SEMS, _SEMS,
                        _SEMS],
        compiler_params=pltpu.CompilerParams(vmem_limit_bytes=VMEM_LIMIT),
    )(c, c_ctx, w_mod_sh, b_mod)


def gather_weights(wpack):
    rows = wpack.shape[0]

    def body(w_ref, out_ref, send_sems, recv_sems, local_sem):
        me, peers = _mesh_pos()
        mine = pltpu.make_async_copy(w_ref, out_ref.at[me], local_sem)
        mine.start()
        sends = []
        for k, peer in enumerate(peers):
            cp = pltpu.make_async_remote_copy(src_ref=w_ref, dst_ref=out_ref.at[me], send_sem=send_sems.at[k],
                                              recv_sem=recv_sems.at[k], device_id=peer, device_id_type=MESH)
            cp.start()
            sends.append(cp)
        for k, peer in enumerate(peers):
            src = jnp.bitwise_xor(me, k + 1)
            pltpu.make_async_remote_copy(src_ref=w_ref, dst_ref=out_ref.at[src], send_sem=send_sems.at[k],
                                         recv_sem=recv_sems.at[k], device_id=peer, device_id_type=MESH).wait_recv()
        for cp in sends:
            cp.wait_send()
        mine.wait()

    return pl.pallas_call(
        body, name="gather_weights", in_specs=[_ANY], out_specs=_ANY,
        out_shape=_sds((N_DEV, rows, D), wpack.dtype),
        scratch_shapes=[_SEMS, _SEMS, pltpu.SemaphoreType.DMA],
    )(wpack)


def gather_sum_small(pack):
    shape = pack.shape

    def body(p_ref, tot_ref, gat_ref, send_sems, recv_sems):
        me, peers = _mesh_pos()
        gat_ref[me] = p_ref[...]
        _all_gather(gat_ref, send_sems, recv_sems, me, peers)
        tot = gat_ref[0]
        for j in range(1, N_DEV):
            tot = tot + gat_ref[j]
        tot_ref[...] = tot

    return pl.pallas_call(
        body, name="gather_sum_small", in_specs=[_VMEM], out_specs=[_VMEM, _VMEM],
        out_shape=[_sds(shape), _sds((N_DEV,) + shape)], scratch_shapes=[_SEMS, _SEMS],
        compiler_params=pltpu.CompilerParams(vmem_limit_bytes=VMEM_LIMIT),
    )(pack)


def mod_backward(s, dm_sh, w, m, v, cc, m_cc, v_cc):
    shape = w.shape

    def body(s_ref, dm_ref, w_ref, m_ref, v_ref, cc_ref, mcc_ref, vcc_ref,
             gw_ref, dw_ref, mw_ref, vw_ref, gc_ref, dc_ref, mc_ref, vc_ref, pbuf, send_sems, recv_sems):
        me, peers = _mesh_pos()
        wv = w_ref[...]
        pbuf[me] = _hi_nt(dm_ref[8:16, :], wv)
        _all_gather(pbuf, send_sems, recv_sems, me, peers)
        g = _hi_tn(s_ref[...], dm_ref[...])
        gw_ref[...] = g
        dw_ref[...], mw_ref[...], vw_ref[...] = _adamw(wv, g, m_ref[...], v_ref[...])
        tot = pbuf[0]
        for j in range(1, N_DEV):
            tot = tot + pbuf[j]
        ccv = cc_ref[...]
        sg = _sigmoid(ccv)
        gc = tot[0:1, :] * (sg * (1.0 + ccv * (1.0 - sg)))
        gc_ref[...] = gc
        dc_ref[...], mc_ref[...], vc_ref[...] = _adamw(ccv, gc, mcc_ref[...], vcc_ref[...])

    return pl.pallas_call(
        body, name="mod_backward", in_specs=[_VMEM] * 8, out_specs=[_VMEM] * 8,
        out_shape=[_sds(shape)] * 4 + [_sds((1, D))] * 4,
        scratch_shapes=[pltpu.VMEM((N_DEV, 8, D), F32), _SEMS, _SEMS],
        compiler_params=pltpu.CompilerParams(vmem_limit_bytes=VMEM_LIMIT),
    )(s, dm_sh, w, m, v, cc, m_cc, v_cc)


_HBM = pl.BlockSpec(memory_space=pltpu.HBM)
_SEM = pl.BlockSpec(memory_space=pltpu.SEMAPHORE)
_EFFECT = pltpu.SideEffectType.DATAFLOW_SIDE_EFFECTING
_hbm = lambda a: pltpu.with_memory_space_constraint(a, pltpu.HBM)


def gather_start(pack):
    land_shape = (N_DEV,) + pack.shape

    def body(p_ref, land_ref, send_sems, recv_sems, p_thru, land_thru, token):
        me, peers = _mesh_pos()
        for k, peer in enumerate(peers):
            pltpu.make_async_remote_copy(src_ref=p_ref, dst_ref=land_ref.at[me], send_sem=send_sems.at[k],
                                         recv_sem=recv_sems.at[k], device_id=peer, device_id_type=MESH).start()
        token[...] = jnp.zeros_like(token)

    return pl.pallas_call(
        body, name="gather_start",
        out_shape=(_SEMS, _SEMS, pltpu.HBM(pack.shape, pack.dtype), pltpu.HBM(land_shape, pack.dtype),
                   _sds((8, 128))),
        in_specs=(_HBM, _HBM), out_specs=(_SEM, _SEM, _HBM, _HBM, _VMEM), input_output_aliases={0: 2, 1: 3},
        compiler_params=pltpu.CompilerParams(has_side_effects=_EFFECT),
    )(_hbm(pack), _hbm(lax.empty(land_shape, pack.dtype)))


def gather_wait(send_sems, recv_sems, pack, land, after):
    def body(p_ref, land_ref, send_sems, recv_sems, after_ref, p_dead, got_ref):
        me, peers = _mesh_pos()
        for k, peer in enumerate(peers):
            src = jnp.bitwise_xor(me, k + 1)
            cp = pltpu.make_async_remote_copy(src_ref=p_ref, dst_ref=land_ref.at[src], send_sem=send_sems.at[k],
                                              recv_sem=recv_sems.at[k], device_id=peer, device_id_type=MESH)
            cp.wait_send()
            cp.wait_recv()

    return pl.pallas_call(
        body, name="gather_wait",
        out_shape=(pltpu.HBM(pack.shape, pack.dtype), pltpu.HBM(land.shape, land.dtype)),
        in_specs=(_HBM, _HBM, _SEM, _SEM, _ANY), out_specs=(_HBM, _HBM), input_output_aliases={0: 0, 1: 1},
        compiler_params=pltpu.CompilerParams(has_side_effects=_EFFECT),
    )(pack, land, send_sems, recv_sems, after)[1]


def scatter_start(grads, tag):
    n = len(grads)
    sems = pltpu.SemaphoreType.DMA((7 * n,))
    lands = [lax.empty((N_DEV - 1,) + g.shape[1:], g.dtype) for g in grads]

    def body(*refs):
        g_refs, l_refs = refs[:n], refs[n:2 * n]
        send_sems, recv_sems = refs[2 * n], refs[2 * n + 1]
        token = refs[-1]
        me, peers = _mesh_pos()
        for w in range(n):
            for k, peer in enumerate(peers):
                dst = jnp.bitwise_xor(me, k + 1)
                pltpu.make_async_remote_copy(
                    src_ref=g_refs[w].at[dst], dst_ref=l_refs[w].at[k], send_sem=send_sems.at[w * 7 + k],
                    recv_sem=recv_sems.at[w * 7 + k], device_id=peer, device_id_type=MESH).start()
        token[...] = jnp.zeros_like(token)

    out = pl.pallas_call(
        body, name="scatter_start_" + tag,
        out_shape=(sems, sems) + tuple(pltpu.HBM(a.shape, a.dtype) for a in list(grads) + lands) + (_sds((8, 128)),),
        in_specs=(_HBM,) * (2 * n), out_specs=(_SEM, _SEM) + (_HBM,) * (2 * n) + (_VMEM,),
        input_output_aliases={i: i + 2 for i in range(2 * n)},
        compiler_params=pltpu.CompilerParams(has_side_effects=_EFFECT),
    )(*[_hbm(a) for a in list(grads) + lands])
    return out[0], out[1], list(out[2:2 + n]), list(out[2 + n:2 + 2 * n]), out[-1]


def scatter_wait(send_sems, recv_sems, grads, lands, after, tag):
    n = len(grads)

    def body(*refs):
        g_refs, l_refs = refs[:n], refs[n:2 * n]
        send_sems, recv_sems = refs[2 * n], refs[2 * n + 1]
        me, peers = _mesh_pos()
        for w in range(n):
            for k, peer in enumerate(peers):
                dst = jnp.bitwise_xor(me, k + 1)
                cp = pltpu.make_async_remote_copy(
                    src_ref=g_refs[w].at[dst], dst_ref=l_refs[w].at[k], send_sem=send_sems.at[w * 7 + k],
                    recv_sem=recv_sems.at[w * 7 + k], device_id=peer, device_id_type=MESH)
                cp.wait_send()
                cp.wait_recv()

    out = pl.pallas_call(
        body, name="scatter_wait_" + tag,
        out_shape=tuple(pltpu.HBM(a.shape, a.dtype) for a in list(grads) + list(lands)),
        in_specs=(_HBM,) * (2 * n) + (_SEM, _SEM, _ANY), out_specs=(_HBM,) * (2 * n),
        input_output_aliases={i: i for i in range(2 * n)},
        compiler_params=pltpu.CompilerParams(has_side_effects=_EFFECT),
    )(*grads, *lands, send_sems, recv_sems, after)
    return list(out[n:2 * n])


def _vec8(rows, width):
    rid = lax.broadcasted_iota(jnp.int32, (8, width), 0)
    out = jnp.zeros((8, width), F32)
    for i, r in enumerate(rows):
        r = r.reshape(-1)
        r = jnp.pad(r, (0, width - r.shape[0]))
        out = jnp.where(rid == i, r[None, :], out)
    return out


def local_step(x, ctx, tgt, mod, mod_c, small, w_int, start, late_weights, grads_ready, tt, tt_ctx, cb, cb_ctx):
    sh1, sc1, g1, sh2, sc2, g2 = [mod[i * D:(i + 1) * D] for i in range(6)]
    csh1, csc1 = mod_c[0:D], mod_c[D:2 * D]
    vec1 = _vec8([small["norm1_g"], sh1, sc1], D)
    vec1c = _vec8([small["norm1_g"], csh1, csc1], D)
    vec2 = _vec8([small["norm2_g"], sh2, sc2], D)
    vec3 = _vec8([g2, small["final_g"]], D)
    vecm = _vec8([g1, small["norm2_g"], sh2, sc2], D)
    vcm = _vec8([jnp.tile(small["gla_norm_g"].reshape(HV), NH), small["conv_b"], small["conv_ln_g"],
                 small["conv_ln_b"]], DC)
    convw = jnp.pad(small["conv_w"], ((0, 1), (0, 0)))
    wa = jnp.zeros((128, 512), F32)
    wa = wa.at[0:RANK, 0:DK].set(small["w_a2_f"]).at[RANK:2 * RANK, DK:2 * DK].set(small["w_a2_b"])
    ba = jnp.concatenate([small["b_a_f"].reshape(1, DK), small["b_a_b"].reshape(1, DK)], axis=1)

    _, _, kc, vc_, _, rc, lac, hc = proj_fwd(ctx, vec1c + start, w_int, wa, ba, tt_ctx)
    qc0 = jnp.zeros_like(kc)
    _, sall_c, sfin_c = gla_fwd(qc0, kc, vc_, lac, jnp.zeros((2, HV, DK), F32), cb_ctx)
    u, q, k, v, g, r, la, h = proj_fwd(x, vec1, w_int, wa, ba, tt)
    o2, sall, _ = gla_fwd(q, k, v, la, sfin_c, cb)
    w_out, wg_t, wu_t, w_down = late_weights(o2)
    x1, cat, mix, yc, h2 = merge_fwd(u, g, o2, x, vecm, vcm, convw, w_out, tt)
    act, dact, hid = ffn_gate_up(h2, wg_t, wu_t, tt)
    dx2, dff, acc3 = ffn_down_loss(hid, x1, tgt, vec3, w_down, tt)
    dgt, dup = ffn_dhid(dff, act, dact, w_down, tt)
    dx1, acc2 = ffn_dh2(dgt, dup, x1, dx2, vec2, wg_t, wu_t, tt)
    bt = min(1024, x.shape[0])
    gw = {"w_down": tn_matmul(hid, dff, FN, bt), "wg_t": tn_matmul(dgt, h2, FN, bt),
          "wu_t": tn_matmul(dup, h2, FN, bt)}
    vecm = vecm + grads_ready(("wg_t", "wu_t", "w_down"), gw)
    du, dg, do, dmix, accm1, accm2, dconvw = merge_bwd(dx1, mix, u, g, o2, yc, vecm, vcm, convw, w_out, tt)
    gw["w_out"] = tn_matmul(cat, dmix, 512, bt)
    dsfin = jnp.zeros((2, HV, DK), F32) + grads_ready(("w_out",), gw)
    dq2, dk2, dv2, dla, ds0 = gla_bwd(q, k, v, la, do, sall, dsfin, cb)
    gx, dp, acc1, dba, dwa = proj_bwd(du, dq2, dk2, dv2, dg, dla, la, r, x, dx1, vec1, w_int, wa, tt)
    tcx = ctx.shape[0]
    zc = lambda w, dt=F32: jnp.zeros((tcx, w), dt)
    _, dk2c, dv2c, dlac, _ = gla_bwd(qc0, kc, vc_, lac, zc(DV, MXU_DTYPE), sall_c, ds0, cb_ctx)
    z2 = jnp.zeros((2, tcx, DK), F32)
    _, dpc, acc1c, dbac, dwac = proj_bwd(zc(1024, MXU_DTYPE), z2, dk2c, dv2c, zc(DV, MXU_DTYPE), dlac, lac, rc, ctx,
                                         zc(D), vec1c, w_int, wa, tt_ctx)
    btc = min(1024, tcx)
    gw["w_int"] = tn_matmul(dp, h, 896, bt, init=tn_matmul(dpc, hc, 896, btc)[0])
    grads_ready(("w_int",), gw)
    dwa_t = dwa + dwac
    dba_t = dba + dbac
    gs = {
        "norm1_g": acc1[2] + acc1c[2], "norm2_g": acc2[2], "final_g": acc3[1], "loss": acc3[2],
        "gla_norm_g": accm2[0], "conv_b": accm2[1], "conv_ln_g": accm2[2], "conv_ln_b": accm2[3],
        "conv_w": dconvw, "b_a": dba_t[0], "w_a2": dwa_t,
    }
    dmod = _vec8([acc1[0], acc1[1], accm1[0], acc2[0], acc2[1], acc3[0]], D)
    dmod_c = _vec8([acc1c[0], acc1c[1]], D)
    return gx, gw, gs, dmod, dmod_c


PACK_ROWS = 96
ROW_N1, ROW_N2, ROW_FG, ROW_LOSS, ROW_GN, ROW_CB, ROW_LG, ROW_LB, ROW_BA = 0, 1, 2, 3, 4, 5, 6, 7, 8
ROW_DMOD, ROW_DMODC, ROW_CW, ROW_WA = 16, 24, 32, 64


def _pack_small(gs, dmod, dmod_c):
    pad = lambda a: jnp.pad(a, ((0, 0), (0, D - a.shape[1])))
    singles = _vec8([gs["norm1_g"], gs["norm2_g"], gs["final_g"], gs["loss"], gs["gla_norm_g"], gs["conv_b"],
                     gs["conv_ln_g"], gs["conv_ln_b"]], D)
    return jnp.concatenate([singles, _vec8([gs["b_a"]], D), dmod, dmod_c, pad(gs["conv_w"]), pad(gs["w_a2"][0:32])],
                           axis=0)


def kernel(x, c, ctx, c_ctx, w_mod, b_mod, norm1_g, norm2_g, w_in, conv_w, conv_b, conv_ln_g, conv_ln_b, w_a2_f, b_a_f, w_a2_b, b_a_b, gla_norm_g, w_out, w_gate, w_up, w_down, final_g, loss_target, m_c_ctx, m_w_mod, m_b_mod, m_norm1_g, m_norm2_g, m_w_in, m_conv_w, m_conv_b, m_conv_ln_g, m_conv_ln_b, m_w_a2_f, m_b_a_f, m_w_a2_b, m_b_a_b, m_gla_norm_g, m_w_out, m_w_gate, m_w_up, m_w_down, m_final_g, v_c_ctx, v_w_mod, v_b_mod, v_norm1_g, v_norm2_g, v_w_in, v_conv_w, v_conv_b, v_conv_ln_g, v_conv_ln_b, v_w_a2_f, v_b_a_f, v_w_a2_b, v_b_a_b, v_gla_norm_g, v_w_out, v_w_gate, v_w_up, v_w_down, v_final_g):
    me = 4 * lax.axis_index("x") + 2 * lax.axis_index("y") + lax.axis_index("c")
    t = x.shape[1]
    tcx = ctx.shape[1]
    r_in, r_out, r_ff = w_in.shape[2], w_out.shape[1], w_gate.shape[2]
    r_in_b = -(-r_in // 16) * 16

    mod_all, s_all = mod_forward(c, c_ctx.reshape(1, D), w_mod[0], b_mod)
    mod = lax.dynamic_slice(mod_all, (me, 0), (1, 6 * D)).reshape(6 * D)
    mod_c = mod_all[8]

    small = dict(norm1_g=norm1_g[0], norm2_g=norm2_g[0], final_g=final_g, gla_norm_g=gla_norm_g[0],
                 conv_b=conv_b[0], conv_ln_g=conv_ln_g[0], conv_ln_b=conv_ln_b[0], b_a_f=b_a_f[0], b_a_b=b_a_b[0])
    sm_pack = jnp.zeros((48, DC), F32)
    sm_pack = lax.dynamic_update_slice(sm_pack, conv_w[0], (0, me * (DC // N_DEV)))
    sm_pack = lax.dynamic_update_slice(sm_pack, w_a2_f[0], (32, me * (DK // N_DEV)))
    sm_pack = lax.dynamic_update_slice(sm_pack, w_a2_b[0], (32, DK + me * (DK // N_DEV)))
    sm_tot, _ = gather_sum_small(sm_pack)
    small["conv_w"] = sm_tot[0:CW, :]
    small["w_a2_f"] = sm_tot[32:32 + RANK, 0:DK]
    small["w_a2_b"] = sm_tot[32:32 + RANK, DK:2 * DK]

    tb = lambda w: w.T.astype(MXU_DTYPE)
    wall = gather_weights(jnp.pad(tb(w_in[0]), ((0, r_in_b - r_in), (0, 0))))
    w_int = jnp.pad(wall[:, 0:r_in, :].reshape(N_DEV * r_in, D), ((0, DINP - DIN), (0, 0)))
    after_w_in = (wall[0:1, 0:1, 0:1] * 0).astype(MXU_DTYPE).reshape(1, 1)
    late_pack = jnp.concatenate([w_out[0].astype(MXU_DTYPE), tb(w_gate[0]), tb(w_up[0]), w_down[0].astype(MXU_DTYPE)],
                                axis=0) + after_w_in
    g_send, g_recv, late_thru, land, g_token = gather_start(late_pack)

    def late_weights(after):
        got = gather_wait(g_send, g_recv, late_thru, land, after)
        mine = (lax.broadcasted_iota(jnp.int32, (N_DEV, 1, 1), 0) == me)
        full = jnp.where(mine, late_pack[None], got)
        o1, o2_, o3 = r_out, r_out + r_ff, r_out + 2 * r_ff
        return (full[:, 0:o1, :].reshape(N_DEV * r_out, D), full[:, o1:o2_, :].reshape(DFF, D),
                full[:, o2_:o3, :].reshape(DFF, D), full[:, o3:o3 + r_ff, :].reshape(DFF, D))

    pad_in = lambda g: jnp.pad(g[0:DIN].reshape(N_DEV, r_in, D), ((0, 0), (0, r_in_b - r_in), (0, 0)))
    blocked = {"w_int": pad_in, "w_out": lambda g: g.reshape(N_DEV, r_out, D)}
    as_blocks = lambda n, g: blocked.get(n, lambda a: a.reshape(N_DEV, r_ff, D))(g)
    pending = []

    def grads_ready(names, gw_now):
        send, recv_s, thru, lands, token = scatter_start([as_blocks(n, gw_now[n][1]) for n in names], names[0])
        pending.append((names, send, recv_s, thru, lands))
        return token[0:1, 0:1]

    gx, gw, gs, dmod, dmod_c = local_step(x[0], ctx[0], loss_target[0], mod, mod_c, small, w_int,
                                          g_token[0:1, 0:1], late_weights, grads_ready, 512, 256, 8, 4)

    tot, gat = gather_sum_small(_pack_small(gs, dmod, dmod_c))
    loss = jnp.sum(tot[ROW_LOSS])
    dm = jnp.concatenate([gat[:, ROW_DMOD:ROW_DMOD + 6, :].reshape(N_DEV, 6 * D),
                          jnp.pad(tot[ROW_DMODC:ROW_DMODC + 6, :].reshape(1, 6 * D), ((0, 7), (0, 0)))], axis=0)
    ncol = w_mod.shape[2]
    dm_sh = lax.dynamic_slice(dm, (0, me * ncol), (16, ncol))
    g_wmod, d_wmod, nm_wmod, nv_wmod, g_cc, d_cc, nm_cc, nv_cc = mod_backward(
        s_all, dm_sh, w_mod[0], m_w_mod[0], v_w_mod[0], c_ctx.reshape(1, D), m_c_ctx.reshape(1, D),
        v_c_ctx.reshape(1, D))

    recv = {}
    for names, send, recv_s, thru, lands in pending:
        after = g_wmod if names[0] == "w_int" else tot
        got = scatter_wait(send, recv_s, thru, lands, after, names[0])
        recv.update(dict(zip(names, got)))
    own = {n: lax.dynamic_index_in_dim(as_blocks(n, gw[n][0]), me, 0, keepdims=False) for n in gw}
    padt = lambda w: jnp.pad(w.T, ((0, r_in_b - r_in), (0, 0)))
    big = {}
    big["w_gate"] = [a.T for a in adamw_sharded(own["wg_t"], recv["wg_t"], w_gate[0].T, m_w_gate[0].T,
                                                 v_w_gate[0].T)]
    big["w_up"] = [a.T for a in adamw_sharded(own["wu_t"], recv["wu_t"], w_up[0].T, m_w_up[0].T, v_w_up[0].T)]
    big["w_down"] = adamw_sharded(own["w_down"], recv["w_down"], w_down[0], m_w_down[0], v_w_down[0])
    big["w_out"] = adamw_sharded(own["w_out"], recv["w_out"], w_out[0], m_w_out[0], v_w_out[0])
    big["w_in"] = [a[0:r_in].T for a in adamw_sharded(own["w_int"], recv["w_int"], padt(w_in[0]), padt(m_w_in[0]),
                                                       padt(v_w_in[0]))]
    big["w_mod"] = [g_wmod, d_wmod, nm_wmod, nv_wmod]

    row = lambda r, w: tot[r:r + 1, 0:w]
    gn_row = tot[ROW_GN:ROW_GN + 1, 0:DC]
    g_small = {
        "b_mod": jnp.sum(dm, axis=0, keepdims=True),
        "norm1_g": row(ROW_N1, D), "norm2_g": row(ROW_N2, D),
        "conv_w": lax.dynamic_slice(tot, (ROW_CW, me * (DC // N_DEV)), (CW, DC // N_DEV)),
        "conv_b": row(ROW_CB, DC), "conv_ln_g": row(ROW_LG, DC), "conv_ln_b": row(ROW_LB, DC),
        "w_a2_f": lax.dynamic_slice(tot, (ROW_WA, me * (DK // N_DEV)), (RANK, DK // N_DEV)),
        "b_a_f": tot[ROW_BA:ROW_BA + 1, 0:DK],
        "w_a2_b": lax.dynamic_slice(tot, (ROW_WA + RANK, DK + me * (DK // N_DEV)), (RANK, DK // N_DEV)),
        "b_a_b": tot[ROW_BA:ROW_BA + 1, DK:2 * DK],
        "gla_norm_g": gn_row[:, 0:HV] + gn_row[:, HV:2 * HV] + gn_row[:, 2 * HV:3 * HV] + gn_row[:, 3 * HV:4 * HV],
        "final_g": row(ROW_FG, D),
    }
    wmv = {
        "b_mod": (b_mod, m_b_mod, v_b_mod), "norm1_g": (norm1_g, m_norm1_g, v_norm1_g),
        "norm2_g": (norm2_g, m_norm2_g, v_norm2_g), "conv_w": (conv_w[0], m_conv_w[0], v_conv_w[0]),
        "conv_b": (conv_b, m_conv_b, v_conv_b), "conv_ln_g": (conv_ln_g, m_conv_ln_g, v_conv_ln_g),
        "conv_ln_b": (conv_ln_b, m_conv_ln_b, v_conv_ln_b), "w_a2_f": (w_a2_f[0], m_w_a2_f[0], v_w_a2_f[0]),
        "b_a_f": (b_a_f, m_b_a_f, v_b_a_f), "w_a2_b": (w_a2_b[0], m_w_a2_b[0], v_w_a2_b[0]),
        "b_a_b": (b_a_b, m_b_a_b, v_b_a_b), "gla_norm_g": (gla_norm_g, m_gla_norm_g, v_gla_norm_g),
        "final_g": (final_g.reshape(1, D), m_final_g.reshape(1, D), v_final_g.reshape(1, D)),
    }
    names_small = list(g_small)
    upd = adamw_small([(g_small[n],) + wmv[n] for n in names_small])
    res = {n: (g_small[n],) + upd[i] for i, n in enumerate(names_small)}
    res["c_ctx"] = (g_cc, d_cc, nm_cc, nv_cc)
    for n in ("w_mod", "w_in", "w_out", "w_gate", "w_up", "w_down"):
        res[n] = tuple(big[n])

    order = ["c_ctx", "w_mod", "b_mod", "norm1_g", "norm2_g", "w_in", "conv_w", "conv_b", "conv_ln_g", "conv_ln_b",
             "w_a2_f", "b_a_f", "w_a2_b", "b_a_b", "gla_norm_g", "w_out", "w_gate", "w_up", "w_down", "final_g"]
    shapes = {"c_ctx": c_ctx.shape, "w_mod": w_mod.shape, "b_mod": b_mod.shape, "norm1_g": norm1_g.shape,
              "norm2_g": norm2_g.shape, "w_in": w_in.shape, "conv_w": conv_w.shape, "conv_b": conv_b.shape,
              "conv_ln_g": conv_ln_g.shape, "conv_ln_b": conv_ln_b.shape, "w_a2_f": w_a2_f.shape,
              "b_a_f": b_a_f.shape, "w_a2_b": w_a2_b.shape, "b_a_b": b_a_b.shape, "gla_norm_g": gla_norm_g.shape,
              "w_out": w_out.shape, "w_gate": w_gate.shape, "w_up": w_up.shape, "w_down": w_down.shape,
              "final_g": final_g.shape}
    outs = [loss, gx.reshape(x.shape)]
    for i in range(4):
        outs += [res[n][i].reshape(shapes[n]) for n in order]
    return tuple(outs)
```

```python
import functools

import jax
import jax.numpy as jnp
from jax import lax
from jax.experimental import pallas as pl
from jax.experimental.pallas import tpu as pltpu

F32 = jnp.float32
MXU_DTYPE = jnp.bfloat16
WIRE_DTYPE = jnp.bfloat16
HI = lax.Precision.HIGHEST
MESH = pl.DeviceIdType.MESH

N_DEV = 8
D = 1024
DC = 512
NH = 4
HK = 64
HV = 128
DK = NH * HK
DV = NH * HV
RANK = 16
CHUNK = 64
SEG = 64
CW = 31
CPAD = 15
DFF = 2816
DIN = 2592
DINP = 2688
TAU = 16.0
EPS = 1e-6
VMEM_LIMIT = 56 * 1024 * 1024

ADAM_LR = 0.001
ADAM_B1 = 0.9
ADAM_B2 = 0.999
ADAM_EPS = 1e-08
ADAM_WD = 0.01
ADAM_STEP = 10


def _mm(a, b):
    return jnp.dot(a.astype(MXU_DTYPE), b.astype(MXU_DTYPE), preferred_element_type=F32)


def _mm_nt(a, b):
    return lax.dot_general(a.astype(MXU_DTYPE), b.astype(MXU_DTYPE), (((1,), (1,)), ((), ())),
                           preferred_element_type=F32)


def _mm_tn(a, b):
    return lax.dot_general(a.astype(MXU_DTYPE), b.astype(MXU_DTYPE), (((0,), (0,)), ((), ())),
                           preferred_element_type=F32)


def _hi(a, b):
    return jnp.dot(a, b, precision=HI, preferred_element_type=F32)


def _hi_nt(a, b):
    return lax.dot_general(a, b, (((1,), (1,)), ((), ())), precision=HI, preferred_element_type=F32)


def _hi_tn(a, b):
    return lax.dot_general(a, b, (((0,), (0,)), ((), ())), precision=HI, preferred_element_type=F32)


def _sigmoid(x):
    return 1.0 / (1.0 + jnp.exp(-x))


def _cparams(n_axes):
    return pltpu.CompilerParams(dimension_semantics=("arbitrary",) * n_axes, vmem_limit_bytes=VMEM_LIMIT)


def _full(shape):
    n = len(shape)
    return pl.BlockSpec(shape, lambda *_: (0,) * n)


def _rows(tt, width):
    return pl.BlockSpec((tt, width), lambda i: (i, 0))


def _sds(shape, dtype=F32):
    return jax.ShapeDtypeStruct(shape, dtype)


def _norm_mod(x, g, sh, sc):
    r = lax.rsqrt(jnp.mean(x * x, axis=-1, keepdims=True) + EPS)
    xn = x * r
    yy = xn * g
    return r, xn, yy, yy * (1.0 + sc) + sh


def _norm_mod_bwd(dh, r, xn, yy, g, sc):
    dsh = jnp.sum(dh, axis=0, keepdims=True)
    dsc = jnp.sum(dh * yy, axis=0, keepdims=True)
    dy = dh * (1.0 + sc)
    dg = jnp.sum(dy * xn, axis=0, keepdims=True)
    dxn = dy * g
    dx = r * (dxn - xn * jnp.mean(dxn * xn, axis=-1, keepdims=True))
    return dsh, dsc, dg, dx


def _acc_rows(ref, first, rows):
    upd = jnp.concatenate(rows + [jnp.zeros((8 - len(rows), rows[0].shape[1]), F32)], axis=0)

    @pl.when(first)
    def _():
        ref[...] = upd

    @pl.when(jnp.logical_not(first))
    def _():
        ref[...] += upd


def _acc(ref, first, val):
    @pl.when(first)
    def _():
        ref[...] = val

    @pl.when(jnp.logical_not(first))
    def _():
        ref[...] += val


def proj_fwd(x, vec, w_int, wa, ba, tt):
    t = x.shape[0]

    def body(x_ref, vec_ref, w_ref, wa_ref, ba_ref, u_ref, q_ref, k_ref, v_ref, g_ref, r_ref, la_ref, h_ref):
        _, _, _, h = _norm_mod(x_ref[...], vec_ref[0:1, :], vec_ref[1:2, :], vec_ref[2:3, :])
        hb = h.astype(MXU_DTYPE)
        h_ref[...] = hb
        p = _mm_nt(hb, w_ref[...])
        u_ref[...] = p[:, 0:1024]
        q_ref[...] = p[:, 1024:1280]
        k_ref[...] = p[:, 1280:1536]
        v_ref[...] = p[:, 1536:2048]
        g_ref[...] = p[:, 2048:2560]
        rr = p[:, 2560:2688]
        r_ref[...] = rr
        z = _hi(rr, wa_ref[...]) + ba_ref[...]
        la_ref[...] = (jnp.minimum(z, 0.0) - jnp.log(1.0 + jnp.exp(-jnp.abs(z)))) * (1.0 / TAU)

    return pl.pallas_call(
        body, name="proj_fwd", grid=(t // tt,),
        in_specs=[_rows(tt, D), _full((8, D)), _full((DINP, D)), _full((128, 512)), _full((1, 512))],
        out_specs=[_rows(tt, 1024), _rows(tt, DK), _rows(tt, DK), _rows(tt, DV), _rows(tt, DV), _rows(tt, 128),
                   _rows(tt, 512), _rows(tt, D)],
        out_shape=[_sds((t, 1024)), _sds((t, DK)), _sds((t, DK)), _sds((t, DV)), _sds((t, DV)), _sds((t, 128)),
                   _sds((t, 512)), _sds((t, D), MXU_DTYPE)],
        compiler_params=_cparams(1),
    )(x, vec, w_int, wa, ba)


def proj_bwd(du, dqkv, dg, dla_f, dla_b, la, r, x, dx1, vec, w_int, wa, tt):
    t = x.shape[0]

    def body(du_ref, dqf_ref, dqb_ref, dkf_ref, dkb_ref, dvf_ref, dvb_ref, dg_ref, dlaf_ref, dlab_ref, la_ref, r_ref,
             x_ref, dx1_ref, vec_ref, w_ref, wa_ref, gx_ref, dp_ref, acc_ref, dba_ref, dwa_ref):
        first = pl.program_id(0) == 0
        dla = jnp.concatenate([dlaf_ref[...], dlab_ref[...]], axis=1)
        dz = dla * (1.0 - jnp.exp(TAU * la_ref[...])) * (1.0 / TAU)
        rr = r_ref[...]
        _acc_rows(dba_ref, first, [jnp.sum(dz, axis=0, keepdims=True)])
        _acc(dwa_ref, first, _hi_tn(rr, dz))
        dr = _hi_nt(dz, wa_ref[...])
        md = lambda a: a.astype(MXU_DTYPE)
        both = lambda a_ref, b_ref: md(a_ref[...].astype(F32) + b_ref[...].astype(F32))
        dp = jnp.concatenate([du_ref[...], both(dqf_ref, dqb_ref), both(dkf_ref, dkb_ref), both(dvf_ref, dvb_ref),
                              dg_ref[...], md(dr)], axis=1)
        dp_ref[...] = dp
        dh = _mm(dp, w_ref[...])
        g, sc = vec_ref[0:1, :], vec_ref[2:3, :]
        rn, xn, yy, _ = _norm_mod(x_ref[...], g, vec_ref[1:2, :], sc)
        dsh, dsc, dgn, dx = _norm_mod_bwd(dh, rn, xn, yy, g, sc)
        gx_ref[...] = dx1_ref[...] + dx
        _acc_rows(acc_ref, first, [dsh, dsc, dgn])

    return pl.pallas_call(
        body, name="proj_bwd", grid=(t // tt,),
        in_specs=[_rows(tt, 1024), _rows(tt, DK), _rows(tt, DK), _rows(tt, DK), _rows(tt, DK), _rows(tt, DV),
                  _rows(tt, DV), _rows(tt, DV), _rows(tt, DK), _rows(tt, DK), _rows(tt, 512),
                  _rows(tt, 128), _rows(tt, D), _rows(tt, D), _full((8, D)), _full((DINP, D)), _full((128, 512))],
        out_specs=[_rows(tt, D), _rows(tt, DINP), _full((8, D)), _full((8, 512)), _full((128, 512))],
        out_shape=[_sds((t, D)), _sds((t, DINP), MXU_DTYPE), _sds((8, D)), _sds((8, 512)), _sds((128, 512))],
        compiler_params=_cparams(1),
    )(du, *dqkv, dg, dla_f, dla_b, la, r, x, dx1, vec, w_int, wa)


def _dot_exact01(m01, x):
    bf = jnp.bfloat16
    w = x.shape[1]
    hi = x.astype(bf)
    r1 = x - hi.astype(F32)
    mid = r1.astype(bf)
    lo = (r1 - mid.astype(F32)).astype(bf)
    y = jnp.dot(m01.astype(bf), jnp.concatenate([hi, mid, lo], axis=1), preferred_element_type=F32)
    return y[:, 0:w] + y[:, w:2 * w] + y[:, 2 * w:3 * w]


def _gla_chunk(d, qc, kc, la_c):
    row = lax.broadcasted_iota(jnp.int32, (CHUNK, CHUNK), 0)
    col = lax.broadcasted_iota(jnp.int32, (CHUNK, CHUNK), 1)
    cum = ((col <= row) if d == 0 else (col >= row)).astype(F32)
    cum_t = ((col >= row) if d == 0 else (col <= row)).astype(F32)
    cum4 = jnp.concatenate([cum] * NH, axis=0)
    head_of_lane = lax.broadcasted_iota(jnp.int32, (1, DK), 1) // HK
    b = _dot_exact01(cum, la_c)
    bl = jnp.sum(la_c, axis=0, keepdims=True)
    eb = jnp.exp(b)
    enb = jnp.exp(-b)
    ekd = jnp.exp(bl - b)
    qt = qc * (HK ** -0.5) * eb
    kt = kc * enb
    kd = kc * ekd
    qst = jnp.concatenate([jnp.where(head_of_lane == h, qt, 0.0) for h in range(NH)], axis=0)
    a = _mm_nt(qst, kt) * cum4
    return cum_t, cum4, head_of_lane, eb, enb, ekd, qt, kt, kd, qst, a, jnp.exp(bl)


def gla_fwd(q, k, v, la, s0, cb):
    t = q.shape[0]
    nc = t // CHUNK
    nb = nc // cb

    def body(qf_ref, kf_ref, vf_ref, laf_ref, qb_ref, kb_ref, vb_ref, lab_ref, s0_ref,
             of_ref, ob_ref, sf_ref, sb_ref, sfin_ref, s_scr):
        i = pl.program_id(0)

        @pl.when(i == 0)
        def _():
            s_scr[...] = s0_ref[...]

        def chunk(d, jj, q_ref, k_ref, v_ref, la_ref, o_ref, sall_ref):
            rows = slice(jj * CHUNK, (jj + 1) * CHUNK)
            vc = v_ref[rows, :]
            _, _, head_of_lane, _, _, _, _, _, kd, qst, a, dec = _gla_chunk(
                d, q_ref[rows, :], k_ref[rows, :], la_ref[rows, :])
            s = s_scr[d]
            sall_ref[jj] = s
            inter = _mm_nt(qst, s)
            outs = []
            for h in range(NH):
                hs = slice(h * CHUNK, (h + 1) * CHUNK)
                outs.append(_mm(a[hs], vc[:, h * HV:(h + 1) * HV]) + inter[hs])
            o_ref[rows, :] = jnp.concatenate(outs, axis=1)
            kv = _mm_tn(vc, kd)
            s_new = dec * s
            for h in range(NH):
                s_new = s_new + jnp.where(head_of_lane == h, kv[h * HV:(h + 1) * HV], 0.0)
            s_scr[d] = s_new

        for j in range(cb):
            chunk(0, j, qf_ref, kf_ref, vf_ref, laf_ref, of_ref, sf_ref)
            chunk(1, cb - 1 - j, qb_ref, kb_ref, vb_ref, lab_ref, ob_ref, sb_ref)

        @pl.when(i == nb - 1)
        def _():
            sfin_ref[...] = s_scr[...]

    tb = cb * CHUNK
    fwd = lambda w, c=0: pl.BlockSpec((tb, w), lambda i: (i, c))
    bwd = lambda w, c=0: pl.BlockSpec((tb, w), lambda i: (nb - 1 - i, c))
    return pl.pallas_call(
        body, name="gla_fwd", grid=(nb,),
        in_specs=[fwd(DK), fwd(DK), fwd(DV), fwd(DK, 0), bwd(DK), bwd(DK), bwd(DV), bwd(DK, 1), _full((2, HV, DK))],
        out_specs=[fwd(DV), bwd(DV), pl.BlockSpec((cb, HV, DK), lambda i: (i, 0, 0)),
                   pl.BlockSpec((cb, HV, DK), lambda i: (nb - 1 - i, 0, 0)), _full((2, HV, DK))],
        out_shape=[_sds((t, DV)), _sds((t, DV)), _sds((nc, HV, DK)), _sds((nc, HV, DK)), _sds((2, HV, DK))],
        scratch_shapes=[pltpu.VMEM((2, HV, DK), F32)],
        compiler_params=_cparams(1),
    )(q, k, v, la, q, k, v, la, s0)


def gla_bwd(q, k, v, la, do, sall_f, sall_b, dsfin, cb):
    t = q.shape[0]
    nc = t // CHUNK
    nb = nc // cb

    def body(qf_ref, kf_ref, vf_ref, laf_ref, dof_ref, sf_ref, qb_ref, kb_ref, vb_ref, lab_ref, dob_ref, sb_ref,
             dsfin_ref, dqf_ref, dkf_ref, dvf_ref, dlaf_ref, dqb_ref, dkb_ref, dvb_ref, dlab_ref, ds0_ref, ds_scr):
        i = pl.program_id(0)

        @pl.when(i == 0)
        def _():
            ds_scr[...] = dsfin_ref[...]

        def chunk(d, jj, q_ref, k_ref, v_ref, la_ref, do_ref, sall_ref, dq_ref, dk_ref, dv_ref, dla_ref):
            rows = slice(jj * CHUNK, (jj + 1) * CHUNK)
            vc = v_ref[rows, :]
            doc = do_ref[rows, :]
            cum_t, cum4, head_of_lane, eb, enb, ekd, qt, kt, kd, qst, a, dec = _gla_chunk(
                d, q_ref[rows, :], k_ref[rows, :], la_ref[rows, :])
            s = sall_ref[jj]
            ds = ds_scr[d]
            hv = lambda x, h: x[:, h * HV:(h + 1) * HV]
            hr = lambda x, h: x[h * CHUNK:(h + 1) * CHUNK]
            fold = lambda x: functools.reduce(
                lambda p, c: p + c, [jnp.where(head_of_lane == h, hr(x, h), 0.0) for h in range(NH)])
            dost = jnp.concatenate([hv(doc, h) for h in range(NH)], axis=0)
            vst = jnp.concatenate([hv(vc, h) for h in range(NH)], axis=0)
            da = jnp.concatenate([_mm_nt(hv(doc, h), hv(vc, h)) for h in range(NH)], axis=0) * cum4
            dqt = fold(_mm(da, kt) + _mm(dost, s))
            dkt = _mm_tn(da, qst)
            kdst = jnp.concatenate([jnp.where(head_of_lane == h, kd, 0.0) for h in range(NH)], axis=0)
            dv_inter = _mm_nt(kdst, ds)
            dv_ref[rows, :] = jnp.concatenate(
                [_mm_tn(hr(a, h), hv(doc, h)) + hr(dv_inter, h) for h in range(NH)], axis=1).astype(MXU_DTYPE)
            dkd = fold(_mm(vst, ds))
            ds_scr[d] = dec * ds + _mm_tn(dost, qst)
            tkd = dkd * kd
            db = dqt * qt - dkt * kt - tkd
            dbl = jnp.sum(ds * s, axis=0, keepdims=True) * dec + jnp.sum(tkd, axis=0, keepdims=True)
            dla_ref[rows, :] = _dot_exact01(cum_t, db) + dbl
            dq_ref[rows, :] = (dqt * eb * (HK ** -0.5)).astype(MXU_DTYPE)
            dk_ref[rows, :] = (dkt * enb + dkd * ekd).astype(MXU_DTYPE)

        for j in range(cb):
            chunk(0, cb - 1 - j, qf_ref, kf_ref, vf_ref, laf_ref, dof_ref, sf_ref, dqf_ref, dkf_ref, dvf_ref, dlaf_ref)
            chunk(1, j, qb_ref, kb_ref, vb_ref, lab_ref, dob_ref, sb_ref, dqb_ref, dkb_ref, dvb_ref, dlab_ref)

        @pl.when(i == nb - 1)
        def _():
            ds0_ref[...] = ds_scr[...]

    tb = cb * CHUNK
    rev = lambda w, c=0: pl.BlockSpec((tb, w), lambda i: (nb - 1 - i, c))
    fro = lambda w, c=0: pl.BlockSpec((tb, w), lambda i: (i, c))
    st_rev = pl.BlockSpec((cb, HV, DK), lambda i: (nb - 1 - i, 0, 0))
    st_fro = pl.BlockSpec((cb, HV, DK), lambda i: (i, 0, 0))
    md = MXU_DTYPE
    return pl.pallas_call(
        body, name="gla_bwd", grid=(nb,),
        in_specs=[rev(DK), rev(DK), rev(DV), rev(DK, 0), rev(DV), st_rev,
                  fro(DK), fro(DK), fro(DV), fro(DK, 1), fro(DV), st_fro, _full((2, HV, DK))],
        out_specs=[rev(DK), rev(DK), rev(DV), rev(DK), fro(DK), fro(DK), fro(DV), fro(DK), _full((2, HV, DK))],
        out_shape=[_sds((t, DK), md), _sds((t, DK), md), _sds((t, DV), md), _sds((t, DK)),
                   _sds((t, DK), md), _sds((t, DK), md), _sds((t, DV), md), _sds((t, DK)), _sds((2, HV, DK))],
        scratch_shapes=[pltpu.VMEM((2, HV, DK), F32)],
        compiler_params=_cparams(1),
    )(q, k, v, la, do, sall_f, q, k, v, la, do, sall_b, dsfin)


def _seg_pos(tt):
    return lax.broadcasted_iota(jnp.int32, (tt, 1), 0) % SEG


def _shifted(x, s, pos, tt):
    y = x if s == 0 else pltpu.roll(x, (-s) % tt, 0)
    return jnp.where((pos + s >= 0) & (pos + s < SEG), y, 0.0)


def _head_norm(o, gn):
    rs, xs = [], []
    for h in range(NH):
        oh = o[:, h * HV:(h + 1) * HV]
        r = lax.rsqrt(jnp.mean(oh * oh, axis=-1, keepdims=True) + EPS)
        rs.append(r)
        xs.append(oh * r)
    return rs, xs


def merge_fwd(u, g, o_f, o_b, x, vec, vc, convw, w_out, tt):
    t = x.shape[0]

    def body(u_ref, g_ref, of_ref, ob_ref, x_ref, vec_ref, vc_ref, cw_ref, w_ref, x1_ref, cat_ref, mix_ref, yc_ref,
             h2_ref):
        a = u_ref[:, 0:DC]
        gate = u_ref[:, DC:2 * DC]
        vv = a * _sigmoid(gate)
        pos = _seg_pos(tt)
        cw = cw_ref[...]
        yc = jnp.zeros((tt, DC), F32) + vc_ref[1:2, :]
        for j in range(CW):
            yc = yc + _shifted(vv, j - CPAD, pos, tt) * cw[j:j + 1, :]
        yc_ref[...] = yc
        mu = jnp.mean(yc, axis=-1, keepdims=True)
        yd = yc - mu
        rs = lax.rsqrt(jnp.mean(yd * yd, axis=-1, keepdims=True) + EPS)
        ln = yd * rs * vc_ref[2:3, :] + vc_ref[3:4, :]
        conv_o = ln * _sigmoid(ln)
        o = of_ref[...] + ob_ref[...]
        _, xs = _head_norm(o, None)
        gg = g_ref[...]
        o2g = jnp.concatenate(xs, axis=1) * vc_ref[0:1, :] * (gg * _sigmoid(gg))
        cat = jnp.concatenate([conv_o, o2g], axis=1).astype(MXU_DTYPE)
        cat_ref[...] = cat
        mix = _mm(cat, w_ref[...])
        mix_ref[...] = mix
        x1 = x_ref[...] + vec_ref[0:1, :] * mix
        x1_ref[...] = x1
        _, _, _, h2 = _norm_mod(x1, vec_ref[1:2, :], vec_ref[2:3, :], vec_ref[3:4, :])
        h2_ref[...] = h2.astype(MXU_DTYPE)

    return pl.pallas_call(
        body, name="merge_fwd", grid=(t // tt,),
        in_specs=[_rows(tt, 1024), _rows(tt, DV), _rows(tt, DV), _rows(tt, DV), _rows(tt, D),
                  _full((8, D)), _full((8, DC)), _full((32, DC)), _full((D, D))],
        out_specs=[_rows(tt, D), _rows(tt, D), _rows(tt, D), _rows(tt, DC), _rows(tt, D)],
        out_shape=[_sds((t, D)), _sds((t, D), MXU_DTYPE), _sds((t, D)), _sds((t, DC)), _sds((t, D), MXU_DTYPE)],
        compiler_params=_cparams(1),
    )(u, g, o_f, o_b, x, vec, vc, convw, w_out)


def merge_bwd(dx1, mix, u, g, o_f, o_b, yc, vec, vc, convw, w_out, tt):
    t = dx1.shape[0]

    def body(dx1_ref, mix_ref, u_ref, g_ref, of_ref, ob_ref, yc_ref, vec_ref, vc_ref, cw_ref, w_ref,
             du_ref, dg_ref, do_ref, dmix_ref, acc1_ref, acc2_ref, dcw_ref):
        first = pl.program_id(0) == 0
        dx1v = dx1_ref[...]
        dg1 = jnp.sum(dx1v * mix_ref[...], axis=0, keepdims=True)
        dmix = (vec_ref[0:1, :] * dx1v).astype(MXU_DTYPE)
        dmix_ref[...] = dmix
        dcat = _mm_nt(dmix, w_ref[...])
        dconv_o = dcat[:, 0:DC]
        do2 = dcat[:, DC:2 * DC]
        gn = vc_ref[0:1, :]
        o = of_ref[...] + ob_ref[...]
        rs, xs = _head_norm(o, None)
        xn = jnp.concatenate(xs, axis=1)
        gg = g_ref[...]
        sg = _sigmoid(gg)
        don = do2 * (gg * sg)
        dg_ref[...] = (do2 * (xn * gn) * (sg * (1.0 + gg * (1.0 - sg)))).astype(MXU_DTYPE)
        dgn = jnp.sum(don * xn, axis=0, keepdims=True)
        dxn = don * gn
        dos = []
        for h in range(NH):
            dh = dxn[:, h * HV:(h + 1) * HV]
            dos.append(rs[h] * (dh - xs[h] * jnp.mean(dh * xs[h], axis=-1, keepdims=True)))
        do_ref[...] = jnp.concatenate(dos, axis=1).astype(MXU_DTYPE)
        yc = yc_ref[...]
        mu = jnp.mean(yc, axis=-1, keepdims=True)
        yd = yc - mu
        rstd = lax.rsqrt(jnp.mean(yd * yd, axis=-1, keepdims=True) + EPS)
        yhat = yd * rstd
        lg = vc_ref[2:3, :]
        ln = yhat * lg + vc_ref[3:4, :]
        sl = _sigmoid(ln)
        dln = dconv_o * (sl * (1.0 + ln * (1.0 - sl)))
        dlb = jnp.sum(dln, axis=0, keepdims=True)
        dlg = jnp.sum(dln * yhat, axis=0, keepdims=True)
        dyh = dln * lg
        dyc = rstd * (dyh - jnp.mean(dyh, axis=-1, keepdims=True)
                      - yhat * jnp.mean(dyh * yhat, axis=-1, keepdims=True))
        dcb = jnp.sum(dyc, axis=0, keepdims=True)
        a = u_ref[:, 0:DC]
        gate = u_ref[:, DC:2 * DC]
        sgt = _sigmoid(gate)
        vv = a * sgt
        pos = _seg_pos(tt)
        cw = cw_ref[...]
        dvv = jnp.zeros((tt, DC), F32)
        dws = []
        for j in range(CW):
            s = j - CPAD
            dvv = dvv + _shifted(dyc, -s, pos, tt) * cw[j:j + 1, :]
            dws.append(jnp.sum(dyc * _shifted(vv, s, pos, tt), axis=0, keepdims=True))
        dws.append(jnp.zeros((1, DC), F32))
        du_ref[:, 0:DC] = (dvv * sgt).astype(MXU_DTYPE)
        du_ref[:, DC:2 * DC] = (dvv * a * sgt * (1.0 - sgt)).astype(MXU_DTYPE)
        _acc_rows(acc1_ref, first, [dg1])
        _acc_rows(acc2_ref, first, [dgn, dcb, dlg, dlb])
        _acc(dcw_ref, first, jnp.concatenate(dws, axis=0))

    return pl.pallas_call(
        body, name="merge_bwd", grid=(t // tt,),
        in_specs=[_rows(tt, D), _rows(tt, D), _rows(tt, 1024), _rows(tt, DV), _rows(tt, DV), _rows(tt, DV),
                  _rows(tt, DC),
                  _full((8, D)), _full((8, DC)), _full((32, DC)), _full((D, D))],
        out_specs=[_rows(tt, 1024), _rows(tt, DV), _rows(tt, DV), _rows(tt, D), _full((8, D)), _full((8, DC)),
                   _full((32, DC))],
        out_shape=[_sds((t, 1024), MXU_DTYPE), _sds((t, DV), MXU_DTYPE), _sds((t, DV), MXU_DTYPE),
                   _sds((t, D), MXU_DTYPE), _sds((8, D)),
                   _sds((8, DC)), _sds((32, DC))],
        compiler_params=_cparams(1),
    )(dx1, mix, u, g, o_f, o_b, yc, vec, vc, convw, w_out)


FN = DFF // 2


def ffn_gate_up(h2, wg_t, wu_t, tt):
    t = h2.shape[0]

    def body(h2_ref, wg_ref, wu_ref, s_ref, d_ref, hid_ref):
        h2v = h2_ref[...]
        gt = _mm_nt(h2v, wg_ref[...])
        up = _mm_nt(h2v, wu_ref[...])
        sg = _sigmoid(gt)
        act = gt * sg
        s_ref[...] = act.astype(MXU_DTYPE)
        d_ref[...] = (up * (sg * (1.0 + gt * (1.0 - sg)))).astype(MXU_DTYPE)
        hid_ref[...] = (act * up).astype(MXU_DTYPE)

    blk = pl.BlockSpec((tt, FN), lambda j, i: (i, j))
    wblk = pl.BlockSpec((FN, D), lambda j, i: (j, 0))
    return pl.pallas_call(
        body, name="ffn_gate_up", grid=(2, t // tt),
        in_specs=[pl.BlockSpec((tt, D), lambda j, i: (i, 0)), wblk, wblk],
        out_specs=[blk, blk, blk],
        out_shape=[_sds((t, DFF), MXU_DTYPE)] * 3,
        compiler_params=_cparams(2),
    )(h2, wg_t, wu_t)


def ffn_down_loss(hid, x1, tgt, vec, w_down, tt):
    t = x1.shape[0]

    def body(hid_ref, x1_ref, tgt_ref, vec_ref, w_ref, dx2_ref, dff_ref, acc_ref):
        first = pl.program_id(0) == 0
        g2 = vec_ref[0:1, :]
        fg = vec_ref[1:2, :]
        ff = _mm(hid_ref[...], w_ref[...])
        x2 = x1_ref[...] + g2 * ff
        rf = lax.rsqrt(jnp.mean(x2 * x2, axis=-1, keepdims=True) + EPS)
        xn = x2 * rf
        err = xn * fg - tgt_ref[...]
        dy = err * (1.0 / D)
        dfg = jnp.sum(dy * xn, axis=0, keepdims=True)
        dxn = dy * fg
        dx2 = rf * (dxn - xn * jnp.mean(dxn * xn, axis=-1, keepdims=True))
        dx2_ref[...] = dx2
        dff_ref[...] = (g2 * dx2).astype(MXU_DTYPE)
        dg2 = jnp.sum(dx2 * ff, axis=0, keepdims=True)
        loss = jnp.sum(err * err, axis=0, keepdims=True) * (0.5 / D)
        _acc_rows(acc_ref, first, [dg2, dfg, loss])

    return pl.pallas_call(
        body, name="ffn_down_loss", grid=(t // tt,),
        in_specs=[_rows(tt, DFF), _rows(tt, D), _rows(tt, D), _full((8, D)), _full((DFF, D))],
        out_specs=[_rows(tt, D), _rows(tt, D), _full((8, D))],
        out_shape=[_sds((t, D)), _sds((t, D), MXU_DTYPE), _sds((8, D))],
        compiler_params=_cparams(1),
    )(hid, x1, tgt, vec, w_down)


def ffn_dhid(dff, s, d, w_down, tt):
    t = dff.shape[0]

    def body(dff_ref, s_ref, d_ref, w_ref, dgt_ref, dup_ref):
        dhid = _mm_nt(dff_ref[...], w_ref[...])
        dgt_ref[...] = (dhid * d_ref[...].astype(F32)).astype(MXU_DTYPE)
        dup_ref[...] = (dhid * s_ref[...].astype(F32)).astype(MXU_DTYPE)

    blk = pl.BlockSpec((tt, FN), lambda j, i: (i, j))
    return pl.pallas_call(
        body, name="ffn_dhid", grid=(2, t // tt),
        in_specs=[pl.BlockSpec((tt, D), lambda j, i: (i, 0)), blk, blk, pl.BlockSpec((FN, D), lambda j, i: (j, 0))],
        out_specs=[blk, blk],
        out_shape=[_sds((t, DFF), MXU_DTYPE), _sds((t, DFF), MXU_DTYPE)],
        compiler_params=_cparams(2),
    )(dff, s, d, w_down)


def ffn_dh2(dgt, dup, x1, dx2, vec, wg_t, wu_t, tt):
    t = x1.shape[0]

    def body(dgt_ref, dup_ref, x1_ref, dx2_ref, vec_ref, wg_ref, wu_ref, dx1_ref, acc_ref):
        first = pl.program_id(0) == 0
        dh2 = _mm(dgt_ref[...], wg_ref[...]) + _mm(dup_ref[...], wu_ref[...])
        g, sc = vec_ref[0:1, :], vec_ref[2:3, :]
        r, xn, yy, _ = _norm_mod(x1_ref[...], g, vec_ref[1:2, :], sc)
        dsh, dsc, dgn, dx = _norm_mod_bwd(dh2, r, xn, yy, g, sc)
        dx1_ref[...] = dx2_ref[...] + dx
        _acc_rows(acc_ref, first, [dsh, dsc, dgn])

    return pl.pallas_call(
        body, name="ffn_dh2", grid=(t // tt,),
        in_specs=[_rows(tt, DFF), _rows(tt, DFF), _rows(tt, D), _rows(tt, D), _full((8, D)), _full((DFF, D)),
                  _full((DFF, D))],
        out_specs=[_rows(tt, D), _full((8, D))],
        out_shape=[_sds((t, D)), _sds((8, D))],
        compiler_params=_cparams(1),
    )(dgt, dup, x1, dx2, vec, wg_t, wu_t)


def tn_matmul(a, b, bm, bt, init=None):
    t, m = a.shape
    n = b.shape[1]
    nk = t // bt

    def body(*refs):
        if init is None:
            a_ref, b_ref, o_ref, wire_ref = refs
        else:
            a_ref, b_ref, i_ref, o_ref, wire_ref = refs
        prod = _mm_tn(a_ref[...], b_ref[...])

        @pl.when(pl.program_id(1) == 0)
        def _():
            o_ref[...] = prod if init is None else prod + i_ref[...]

        @pl.when(pl.program_id(1) != 0)
        def _():
            o_ref[...] += prod

        @pl.when(pl.program_id(1) == nk - 1)
        def _():
            wire_ref[...] = o_ref[...].astype(WIRE_DTYPE)

    in_specs = [pl.BlockSpec((bt, bm), lambda i, k: (k, i)), pl.BlockSpec((bt, n), lambda i, k: (k, 0))]
    args = [a, b]
    if init is not None:
        in_specs.append(pl.BlockSpec((bm, n), lambda i, k: (i, 0)))
        args.append(init)
    oblk = pl.BlockSpec((bm, n), lambda i, k: (i, 0))
    return pl.pallas_call(
        body, name="tn_matmul", grid=(m // bm, nk),
        in_specs=in_specs, out_specs=[oblk, oblk],
        out_shape=[_sds((m, n)), _sds((m, n), WIRE_DTYPE)], compiler_params=_cparams(2),
    )(*args)


def _adamw(w, g, m, v):
    m = ADAM_B1 * m + (1.0 - ADAM_B1) * g
    v = ADAM_B2 * v + (1.0 - ADAM_B2) * (g * g)
    m_hat = m / (1.0 - ADAM_B1 ** ADAM_STEP)
    v_hat = v / (1.0 - ADAM_B2 ** ADAM_STEP)
    delta = -ADAM_LR * (m_hat / (jnp.sqrt(v_hat) + ADAM_EPS) + ADAM_WD * w)
    return delta, m, v


def adamw_sharded(own, recv, w, m, v):
    shape = w.shape

    def body(own_ref, recv_ref, w_ref, m_ref, v_ref, g_ref, d_ref, mo_ref, vo_ref):
        g = own_ref[...]
        for k in range(N_DEV - 1):
            g = g + recv_ref[k].astype(F32)
        g_ref[...] = g
        d_ref[...], mo_ref[...], vo_ref[...] = _adamw(w_ref[...], g, m_ref[...], v_ref[...])

    return pl.pallas_call(
        body, name="adamw_sharded",
        in_specs=[_full(shape), _full((N_DEV - 1,) + shape), _full(shape), _full(shape), _full(shape)],
        out_specs=[_full(shape)] * 4, out_shape=[_sds(shape)] * 4, grid=(1,),
        compiler_params=_cparams(1),
    )(own, recv, w, m, v)


def adamw_small(items):
    n = len(items)
    flat = [a for it in items for a in it]

    def body(*refs):
        ins, outs = refs[:4 * n], refs[4 * n:]
        for i in range(n):
            g, w, m, v = (r[...] for r in ins[4 * i:4 * i + 4])
            outs[3 * i][...], outs[3 * i + 1][...], outs[3 * i + 2][...] = _adamw(w, g, m, v)

    out = pl.pallas_call(
        body, name="adamw_small", grid=(1,),
        in_specs=[_full(a.shape) for a in flat],
        out_specs=[_full(it[1].shape) for it in items for _ in range(3)],
        out_shape=[_sds(it[1].shape) for it in items for _ in range(3)],
        compiler_params=_cparams(1),
    )(*flat)
    return [tuple(out[3 * i:3 * i + 3]) for i in range(n)]


def _mesh_pos():
    x, y, c = lax.axis_index("x"), lax.axis_index("y"), lax.axis_index("c")
    me = 4 * x + 2 * y + c
    peers = []
    for k in range(1, N_DEV):
        peers.append(((1 - x) if (k >> 2) & 1 else x, (1 - y) if (k >> 1) & 1 else y, (1 - c) if k & 1 else c))
    return me, peers


def _all_gather(buf, send_sems, recv_sems, me, peers):
    sends = []
    for k, peer in enumerate(peers):
        cp = pltpu.make_async_remote_copy(src_ref=buf.at[me], dst_ref=buf.at[me], send_sem=send_sems.at[k],
                                          recv_sem=recv_sems.at[k], device_id=peer, device_id_type=MESH)
        cp.start()
        sends.append(cp)
    for k, peer in enumerate(peers):
        src = jnp.bitwise_xor(me, k + 1)
        pltpu.make_async_remote_copy(src_ref=buf.at[src], dst_ref=buf.at[src], send_sem=send_sems.at[k],
                                     recv_sem=recv_sems.at[k], device_id=peer, device_id_type=MESH).wait_recv()
    for cp in sends:
        cp.wait_send()


_VMEM = pl.BlockSpec(memory_space=pltpu.VMEM)
_ANY = pl.BlockSpec(memory_space=pl.ANY)
_SEMS = pltpu.SemaphoreType.DMA((N_DEV - 1,))


def mod_forward(c, c_ctx, w_mod_sh, b_mod):
    ncol = w_mod_sh.shape[1]

    def body(c_ref, cc_ref, w_ref, b_ref, mod_ref, s_ref, cbuf, pbuf, s1, r1, s2, r2):
        me, peers = _mesh_pos()
        cbuf[me] = jnp.broadcast_to(c_ref[...], (8, D))
        _all_gather(cbuf, s1, r1, me, peers)
        rows = [cbuf[j, 0:1, :] for j in range(N_DEV)] + [cc_ref[...], jnp.zeros((7, D), F32)]
        sx = jnp.concatenate(rows, axis=0)
        s = sx * _sigmoid(sx)
        s_ref[...] = s
        pbuf[me] = _hi(s, w_ref[...])
        _all_gather(pbuf, s2, r2, me, peers)
        for j in range(N_DEV):
            mod_ref[:, j * ncol:(j + 1) * ncol] = pbuf[j] + b_ref[:, j * ncol:(j + 1) * ncol]

    return pl.pallas_call(
        body, name="mod_forward",
        in_specs=[_VMEM] * 4, out_specs=[_VMEM] * 2,
        out_shape=[_sds((16, N_DEV * ncol)), _sds((16, D))],
        scratch_shapes=[pltpu.VMEM((N_DEV, 8, D), F32), pltpu.VMEM((N_DEV, 16, ncol), F32), _SEMS, _SEMS, _SEMS,
                        _SEMS],
        compiler_params=pltpu.CompilerParams(vmem_limit_bytes=VMEM_LIMIT),
    )(c, c_ctx, w_mod_sh, b_mod)


def gather_weights(wpack):
    rows = wpack.shape[0]

    def body(w_ref, out_ref, send_sems, recv_sems, local_sem):
        me, peers = _mesh_pos()
        mine = pltpu.make_async_copy(w_ref, out_ref.at[me], local_sem)
        mine.start()
        sends = []
        for k, peer in enumerate(peers):
            cp = pltpu.make_async_remote_copy(src_ref=w_ref, dst_ref=out_ref.at[me], send_sem=send_sems.at[k],
                                              recv_sem=recv_sems.at[k], device_id=peer, device_id_type=MESH)
            cp.start()
            sends.append(cp)
        for k, peer in enumerate(peers):
            src = jnp.bitwise_xor(me, k + 1)
            pltpu.make_async_remote_copy(src_ref=w_ref, dst_ref=out_ref.at[src], send_sem=send_sems.at[k],
                                         recv_sem=recv_sems.at[k], device_id=peer, device_id_type=MESH).wait_recv()
        for cp in sends:
            cp.wait_send()
        mine.wait()

    return pl.pallas_call(
        body, name="gather_weights", in_specs=[_ANY], out_specs=_ANY,
        out_shape=_sds((N_DEV, rows, D), wpack.dtype),
        scratch_shapes=[_SEMS, _SEMS, pltpu.SemaphoreType.DMA],
    )(wpack)


def gather_sum_small(pack):
    shape = pack.shape

    def body(p_ref, tot_ref, gat_ref, send_sems, recv_sems):
        me, peers = _mesh_pos()
        gat_ref[me] = p_ref[...]
        _all_gather(gat_ref, send_sems, recv_sems, me, peers)
        tot = gat_ref[0]
        for j in range(1, N_DEV):
            tot = tot + gat_ref[j]
        tot_ref[...] = tot

    return pl.pallas_call(
        body, name="gather_sum_small", in_specs=[_VMEM], out_specs=[_VMEM, _VMEM],
        out_shape=[_sds(shape), _sds((N_DEV,) + shape)], scratch_shapes=[_SEMS, _SEMS],
        compiler_params=pltpu.CompilerParams(vmem_limit_bytes=VMEM_LIMIT),
    )(pack)


def mod_backward(s, dm_sh, w, m, v, cc, m_cc, v_cc):
    shape = w.shape

    def body(s_ref, dm_ref, w_ref, m_ref, v_ref, cc_ref, mcc_ref, vcc_ref,
             gw_ref, dw_ref, mw_ref, vw_ref, gc_ref, dc_ref, mc_ref, vc_ref, pbuf, send_sems, recv_sems):
        me, peers = _mesh_pos()
        wv = w_ref[...]
        pbuf[me] = _hi_nt(dm_ref[8:16, :], wv)
        _all_gather(pbuf, send_sems, recv_sems, me, peers)
        g = _hi_tn(s_ref[...], dm_ref[...])
        gw_ref[...] = g
        dw_ref[...], mw_ref[...], vw_ref[...] = _adamw(wv, g, m_ref[...], v_ref[...])
        tot = pbuf[0]
        for j in range(1, N_DEV):
            tot = tot + pbuf[j]
        ccv = cc_ref[...]
        sg = _sigmoid(ccv)
        gc = tot[0:1, :] * (sg * (1.0 + ccv * (1.0 - sg)))
        gc_ref[...] = gc
        dc_ref[...], mc_ref[...], vc_ref[...] = _adamw(ccv, gc, mcc_ref[...], vcc_ref[...])

    return pl.pallas_call(
        body, name="mod_backward", in_specs=[_VMEM] * 8, out_specs=[_VMEM] * 8,
        out_shape=[_sds(shape)] * 4 + [_sds((1, D))] * 4,
        scratch_shapes=[pltpu.VMEM((N_DEV, 8, D), F32), _SEMS, _SEMS],
        compiler_params=pltpu.CompilerParams(vmem_limit_bytes=VMEM_LIMIT),
    )(s, dm_sh, w, m, v, cc, m_cc, v_cc)


_HBM = pl.BlockSpec(memory_space=pltpu.HBM)
_SEM = pl.BlockSpec(memory_space=pltpu.SEMAPHORE)
_EFFECT = pltpu.SideEffectType.DATAFLOW_SIDE_EFFECTING
_hbm = lambda a: pltpu.with_memory_space_constraint(a, pltpu.HBM)


def gather_start(pack):
    land_shape = (N_DEV,) + pack.shape

    def body(p_ref, land_ref, send_sems, recv_sems, p_thru, land_thru, token):
        me, peers = _mesh_pos()
        for k, peer in enumerate(peers):
            pltpu.make_async_remote_copy(src_ref=p_ref, dst_ref=land_ref.at[me], send_sem=send_sems.at[k],
                                         recv_sem=recv_sems.at[k], device_id=peer, device_id_type=MESH).start()
        token[...] = jnp.zeros_like(token)

    return pl.pallas_call(
        body, name="gather_start",
        out_shape=(_SEMS, _SEMS, pltpu.HBM(pack.shape, pack.dtype), pltpu.HBM(land_shape, pack.dtype),
                   _sds((8, 128))),
        in_specs=(_HBM, _HBM), out_specs=(_SEM, _SEM, _HBM, _HBM, _VMEM), input_output_aliases={0: 2, 1: 3},
        compiler_params=pltpu.CompilerParams(has_side_effects=_EFFECT),
    )(_hbm(pack), _hbm(lax.empty(land_shape, pack.dtype)))


def gather_wait(send_sems, recv_sems, pack, land, after):
    def body(p_ref, land_ref, send_sems, recv_sems, after_ref, p_dead, got_ref):
        me, peers = _mesh_pos()
        for k, peer in enumerate(peers):
            src = jnp.bitwise_xor(me, k + 1)
            cp = pltpu.make_async_remote_copy(src_ref=p_ref, dst_ref=land_ref.at[src], send_sem=send_sems.at[k],
                                              recv_sem=recv_sems.at[k], device_id=peer, device_id_type=MESH)
            cp.wait_send()
            cp.wait_recv()

    return pl.pallas_call(
        body, name="gather_wait",
        out_shape=(pltpu.HBM(pack.shape, pack.dtype), pltpu.HBM(land.shape, land.dtype)),
        in_specs=(_HBM, _HBM, _SEM, _SEM, _ANY), out_specs=(_HBM, _HBM), input_output_aliases={0: 0, 1: 1},
        compiler_params=pltpu.CompilerParams(has_side_effects=_EFFECT),
    )(pack, land, send_sems, recv_sems, after)[1]


def scatter_start(grads, tag):
    n = len(grads)
    sems = pltpu.SemaphoreType.DMA((7 * n,))
    lands = [lax.empty((N_DEV - 1,) + g.shape[1:], g.dtype) for g in grads]

    def body(*refs):
        g_refs, l_refs = refs[:n], refs[n:2 * n]
        send_sems, recv_sems = refs[2 * n], refs[2 * n + 1]
        token = refs[-1]
        me, peers = _mesh_pos()
        for w in range(n):
            for k, peer in enumerate(peers):
                dst = jnp.bitwise_xor(me, k + 1)
                pltpu.make_async_remote_copy(
                    src_ref=g_refs[w].at[dst], dst_ref=l_refs[w].at[k], send_sem=send_sems.at[w * 7 + k],
                    recv_sem=recv_sems.at[w * 7 + k], device_id=peer, device_id_type=MESH).start()
        token[...] = jnp.zeros_like(token)

    out = pl.pallas_call(
        body, name="scatter_start_" + tag,
        out_shape=(sems, sems) + tuple(pltpu.HBM(a.shape, a.dtype) for a in list(grads) + lands) + (_sds((8, 128)),),
        in_specs=(_HBM,) * (2 * n), out_specs=(_SEM, _SEM) + (_HBM,) * (2 * n) + (_VMEM,),
        input_output_aliases={i: i + 2 for i in range(2 * n)},
        compiler_params=pltpu.CompilerParams(has_side_effects=_EFFECT),
    )(*[_hbm(a) for a in list(grads) + lands])
    return out[0], out[1], list(out[2:2 + n]), list(out[2 + n:2 + 2 * n]), out[-1]


def scatter_wait(send_sems, recv_sems, grads, lands, after, tag):
    n = len(grads)

    def body(*refs):
        g_refs, l_refs = refs[:n], refs[n:2 * n]
        send_sems, recv_sems = refs[2 * n], refs[2 * n + 1]
        me, peers = _mesh_pos()
        for w in range(n):
            for k, peer in enumerate(peers):
                dst = jnp.bitwise_xor(me, k + 1)
                cp = pltpu.make_async_remote_copy(
                    src_ref=g_refs[w].at[dst], dst_ref=l_refs[w].at[k], send_sem=send_sems.at[w * 7 + k],
                    recv_sem=recv_sems.at[w * 7 + k], device_id=peer, device_id_type=MESH)
                cp.wait_send()
                cp.wait_recv()

    out = pl.pallas_call(
        body, name="scatter_wait_" + tag,
        out_shape=tuple(pltpu.HBM(a.shape, a.dtype) for a in list(grads) + list(lands)),
        in_specs=(_HBM,) * (2 * n) + (_SEM, _SEM, _ANY), out_specs=(_HBM,) * (2 * n),
        input_output_aliases={i: i for i in range(2 * n)},
        compiler_params=pltpu.CompilerParams(has_side_effects=_EFFECT),
    )(*grads, *lands, send_sems, recv_sems, after)
    return list(out[n:2 * n])


def _vec8(rows, width):
    rid = lax.broadcasted_iota(jnp.int32, (8, width), 0)
    out = jnp.zeros((8, width), F32)
    for i, r in enumerate(rows):
        r = r.reshape(-1)
        r = jnp.pad(r, (0, width - r.shape[0]))
        out = jnp.where(rid == i, r[None, :], out)
    return out


def local_step(x, ctx, tgt, mod, mod_c, small, w_int, start, late_weights, grads_ready, tt, tt_ctx, cb, cb_ctx):
    sh1, sc1, g1, sh2, sc2, g2 = [mod[i * D:(i + 1) * D] for i in range(6)]
    csh1, csc1 = mod_c[0:D], mod_c[D:2 * D]
    vec1 = _vec8([small["norm1_g"], sh1, sc1], D)
    vec1c = _vec8([small["norm1_g"], csh1, csc1], D)
    vec2 = _vec8([small["norm2_g"], sh2, sc2], D)
    vec3 = _vec8([g2, small["final_g"]], D)
    vecm = _vec8([g1, small["norm2_g"], sh2, sc2], D)
    vcm = _vec8([jnp.tile(small["gla_norm_g"].reshape(HV), NH), small["conv_b"], small["conv_ln_g"],
                 small["conv_ln_b"]], DC)
    convw = jnp.pad(small["conv_w"], ((0, 1), (0, 0)))
    wa = jnp.zeros((128, 512), F32)
    wa = wa.at[0:RANK, 0:DK].set(small["w_a2_f"]).at[RANK:2 * RANK, DK:2 * DK].set(small["w_a2_b"])
    ba = jnp.concatenate([small["b_a_f"].reshape(1, DK), small["b_a_b"].reshape(1, DK)], axis=1)

    _, _, kc, vc_, _, rc, lac, hc = proj_fwd(ctx, vec1c + start, w_int, wa, ba, tt_ctx)
    qc0 = jnp.zeros_like(kc)
    _, _, sallf_c, sallb_c, sfin_c = gla_fwd(qc0, kc, vc_, lac, jnp.zeros((2, HV, DK), F32), cb_ctx)
    u, q, k, v, g, r, la, h = proj_fwd(x, vec1, w_int, wa, ba, tt)
    o_f, o_b, sall_f, sall_b, _ = gla_fwd(q, k, v, la, sfin_c, cb)
    w_out, wg_t, wu_t, w_down = late_weights(o_b)
    x1, cat, mix, yc, h2 = merge_fwd(u, g, o_f, o_b, x, vecm, vcm, convw, w_out, tt)
    act, dact, hid = ffn_gate_up(h2, wg_t, wu_t, tt)
    dx2, dff, acc3 = ffn_down_loss(hid, x1, tgt, vec3, w_down, tt)
    dgt, dup = ffn_dhid(dff, act, dact, w_down, tt)
    dx1, acc2 = ffn_dh2(dgt, dup, x1, dx2, vec2, wg_t, wu_t, tt)
    bt = min(2048, x.shape[0])
    gw = {"w_down": tn_matmul(hid, dff, FN, bt), "wg_t": tn_matmul(dgt, h2, FN, bt),
          "wu_t": tn_matmul(dup, h2, FN, bt)}
    vecm = vecm + grads_ready(("wg_t", "wu_t", "w_down"), gw)
    du, dg, do, dmix, accm1, accm2, dconvw = merge_bwd(dx1, mix, u, g, o_f, o_b, yc, vecm, vcm, convw, w_out, tt)
    gw["w_out"] = tn_matmul(cat, dmix, 512, bt)
    dsfin = jnp.zeros((2, HV, DK), F32) + grads_ready(("w_out",), gw)
    dqf, dkf, dvf, dlaf, dqb, dkb, dvb, dlab, ds0 = gla_bwd(q, k, v, la, do, sall_f, sall_b, dsfin, cb)
    gx, dp, acc1, dba, dwa = proj_bwd(du, (dqf, dqb, dkf, dkb, dvf, dvb), dg, dlaf, dlab, la, r, x, dx1, vec1, w_int,
                                      wa, tt)
    tcx = ctx.shape[0]
    zc = lambda w, dt=MXU_DTYPE: jnp.zeros((tcx, w), dt)
    _, dkf, dvf, dlaf, _, dkb, dvb, dlab, _ = gla_bwd(qc0, kc, vc_, lac, zc(DV), sallf_c, sallb_c, ds0, cb_ctx)
    _, dpc, acc1c, dbac, dwac = proj_bwd(zc(1024), (zc(DK), zc(DK), dkf, dkb, dvf, dvb), zc(DV), dlaf, dlab, lac, rc,
                                         ctx, zc(D, F32), vec1c, w_int, wa, tt_ctx)
    btc = min(1024, tcx)
    gw["w_int"] = tn_matmul(dp, h, 896, bt, init=tn_matmul(dpc, hc, 896, btc)[0])
    grads_ready(("w_int",), gw)
    dwa_t = dwa + dwac
    dba_t = dba + dbac
    gs = {
        "norm1_g": acc1[2] + acc1c[2], "norm2_g": acc2[2], "final_g": acc3[1], "loss": acc3[2],
        "gla_norm_g": accm2[0], "conv_b": accm2[1], "conv_ln_g": accm2[2], "conv_ln_b": accm2[3],
        "conv_w": dconvw, "b_a": dba_t[0], "w_a2": dwa_t,
    }
    dmod = _vec8([acc1[0], acc1[1], accm1[0], acc2[0], acc2[1], acc3[0]], D)
    dmod_c = _vec8([acc1c[0], acc1c[1]], D)
    return gx, gw, gs, dmod, dmod_c


PACK_ROWS = 96
ROW_N1, ROW_N2, ROW_FG, ROW_LOSS, ROW_GN, ROW_CB, ROW_LG, ROW_LB, ROW_BA = 0, 1, 2, 3, 4, 5, 6, 7, 8
ROW_DMOD, ROW_DMODC, ROW_CW, ROW_WA = 16, 24, 32, 64


def _pack_small(gs, dmod, dmod_c):
    pad = lambda a: jnp.pad(a, ((0, 0), (0, D - a.shape[1])))
    singles = _vec8([gs["norm1_g"], gs["norm2_g"], gs["final_g"], gs["loss"], gs["gla_norm_g"], gs["conv_b"],
                     gs["conv_ln_g"], gs["conv_ln_b"]], D)
    return jnp.concatenate([singles, _vec8([gs["b_a"]], D), dmod, dmod_c, pad(gs["conv_w"]), pad(gs["w_a2"][0:32])],
                           axis=0)


def kernel(x, c, ctx, c_ctx, w_mod, b_mod, norm1_g, norm2_g, w_in, conv_w, conv_b, conv_ln_g, conv_ln_b, w_a2_f, b_a_f, w_a2_b, b_a_b, gla_norm_g, w_out, w_gate, w_up, w_down, final_g, loss_target, m_c_ctx, m_w_mod, m_b_mod, m_norm1_g, m_norm2_g, m_w_in, m_conv_w, m_conv_b, m_conv_ln_g, m_conv_ln_b, m_w_a2_f, m_b_a_f, m_w_a2_b, m_b_a_b, m_gla_norm_g, m_w_out, m_w_gate, m_w_up, m_w_down, m_final_g, v_c_ctx, v_w_mod, v_b_mod, v_norm1_g, v_norm2_g, v_w_in, v_conv_w, v_conv_b, v_conv_ln_g, v_conv_ln_b, v_w_a2_f, v_b_a_f, v_w_a2_b, v_b_a_b, v_gla_norm_g, v_w_out, v_w_gate, v_w_up, v_w_down, v_final_g):
    me = 4 * lax.axis_index("x") + 2 * lax.axis_index("y") + lax.axis_index("c")
    t = x.shape[1]
    tcx = ctx.shape[1]
    r_in, r_out, r_ff = w_in.shape[2], w_out.shape[1], w_gate.shape[2]
    r_in_b = -(-r_in // 16) * 16

    mod_all, s_all = mod_forward(c, c_ctx.reshape(1, D), w_mod[0], b_mod)
    mod = lax.dynamic_slice(mod_all, (me, 0), (1, 6 * D)).reshape(6 * D)
    mod_c = mod_all[8]

    small = dict(norm1_g=norm1_g[0], norm2_g=norm2_g[0], final_g=final_g, gla_norm_g=gla_norm_g[0],
                 conv_b=conv_b[0], conv_ln_g=conv_ln_g[0], conv_ln_b=conv_ln_b[0], b_a_f=b_a_f[0], b_a_b=b_a_b[0])
    sm_pack = jnp.zeros((48, DC), F32)
    sm_pack = lax.dynamic_update_slice(sm_pack, conv_w[0], (0, me * (DC // N_DEV)))
    sm_pack = lax.dynamic_update_slice(sm_pack, w_a2_f[0], (32, me * (DK // N_DEV)))
    sm_pack = lax.dynamic_update_slice(sm_pack, w_a2_b[0], (32, DK + me * (DK // N_DEV)))
    sm_tot, _ = gather_sum_small(sm_pack)
    small["conv_w"] = sm_tot[0:CW, :]
    small["w_a2_f"] = sm_tot[32:32 + RANK, 0:DK]
    small["w_a2_b"] = sm_tot[32:32 + RANK, DK:2 * DK]

    tb = lambda w: w.T.astype(MXU_DTYPE)
    wall = gather_weights(jnp.pad(tb(w_in[0]), ((0, r_in_b - r_in), (0, 0))))
    w_int = jnp.pad(wall[:, 0:r_in, :].reshape(N_DEV * r_in, D), ((0, DINP - DIN), (0, 0)))
    after_w_in = ((wall[0:1, 0:1, 0] * 0).astype(F32) + sm_tot[0:1, 0:1] * 0 + mod_all[0:1, 0:1] * 0).astype(MXU_DTYPE)
    late_pack = jnp.concatenate([w_out[0].astype(MXU_DTYPE), tb(w_gate[0]), tb(w_up[0]), w_down[0].astype(MXU_DTYPE)],
                                axis=0) + after_w_in
    g_send, g_recv, late_thru, land, g_token = gather_start(late_pack)

    def late_weights(after):
        got = gather_wait(g_send, g_recv, late_thru, land, after)
        mine = (lax.broadcasted_iota(jnp.int32, (N_DEV, 1, 1), 0) == me)
        full = jnp.where(mine, late_pack[None], got)
        o1, o2_, o3 = r_out, r_out + r_ff, r_out + 2 * r_ff
        return (full[:, 0:o1, :].reshape(N_DEV * r_out, D), full[:, o1:o2_, :].reshape(DFF, D),
                full[:, o2_:o3, :].reshape(DFF, D), full[:, o3:o3 + r_ff, :].reshape(DFF, D))

    pad_in = lambda g: jnp.pad(g[0:DIN].reshape(N_DEV, r_in, D), ((0, 0), (0, r_in_b - r_in), (0, 0)))
    blocked = {"w_int": pad_in, "w_out": lambda g: g.reshape(N_DEV, r_out, D)}
    as_blocks = lambda n, g: blocked.get(n, lambda a: a.reshape(N_DEV, r_ff, D))(g)
    pending = []

    def grads_ready(names, gw_now):
        send, recv_s, thru, lands, token = scatter_start([as_blocks(n, gw_now[n][1]) for n in names], names[0])
        pending.append((names, send, recv_s, thru, lands))
        return token[0:1, 0:1]

    gx, gw, gs, dmod, dmod_c = local_step(x[0], ctx[0], loss_target[0], mod, mod_c, small, w_int,
                                          g_token[0:1, 0:1], late_weights, grads_ready, 512, 256, 8, 4)

    tot, gat = gather_sum_small(_pack_small(gs, dmod, dmod_c))
    loss = jnp.sum(tot[ROW_LOSS])
    dm = jnp.concatenate([gat[:, ROW_DMOD:ROW_DMOD + 6, :].reshape(N_DEV, 6 * D),
                          jnp.pad(tot[ROW_DMODC:ROW_DMODC + 6, :].reshape(1, 6 * D), ((0, 7), (0, 0)))], axis=0)
    ncol = w_mod.shape[2]
    dm_sh = lax.dynamic_slice(dm, (0, me * ncol), (16, ncol))
    g_wmod, d_wmod, nm_wmod, nv_wmod, g_cc, d_cc, nm_cc, nv_cc = mod_backward(
        s_all, dm_sh, w_mod[0], m_w_mod[0], v_w_mod[0], c_ctx.reshape(1, D), m_c_ctx.reshape(1, D),
        v_c_ctx.reshape(1, D))

    recv = {}
    for names, send, recv_s, thru, lands in pending:
        after = g_wmod if names[0] == "w_int" else tot
        got = scatter_wait(send, recv_s, thru, lands, after, names[0])
        recv.update(dict(zip(names, got)))
    own = {n: lax.dynamic_index_in_dim(as_blocks(n, gw[n][0]), me, 0, keepdims=False) for n in gw}
    padt = lambda w: jnp.pad(w.T, ((0, r_in_b - r_in), (0, 0)))
    big = {}
    big["w_gate"] = [a.T for a in adamw_sharded(own["wg_t"], recv["wg_t"], w_gate[0].T, m_w_gate[0].T,
                                                 v_w_gate[0].T)]
    big["w_up"] = [a.T for a in adamw_sharded(own["wu_t"], recv["wu_t"], w_up[0].T, m_w_up[0].T, v_w_up[0].T)]
    big["w_down"] = adamw_sharded(own["w_down"], recv["w_down"], w_down[0], m_w_down[0], v_w_down[0])
    big["w_out"] = adamw_sharded(own["w_out"], recv["w_out"], w_out[0], m_w_out[0], v_w_out[0])
    big["w_in"] = [a[0:r_in].T for a in adamw_sharded(own["w_int"], recv["w_int"], padt(w_in[0]), padt(m_w_in[0]),
                                                       padt(v_w_in[0]))]
    big["w_mod"] = [g_wmod, d_wmod, nm_wmod, nv_wmod]

    row = lambda r, w: tot[r:r + 1, 0:w]
    gn_row = tot[ROW_GN:ROW_GN + 1, 0:DC]
    g_small = {
        "b_mod": jnp.sum(dm, axis=0, keepdims=True),
        "norm1_g": row(ROW_N1, D), "norm2_g": row(ROW_N2, D),
        "conv_w": lax.dynamic_slice(tot, (ROW_CW, me * (DC // N_DEV)), (CW, DC // N_DEV)),
        "conv_b": row(ROW_CB, DC), "conv_ln_g": row(ROW_LG, DC), "conv_ln_b": row(ROW_LB, DC),
        "w_a2_f": lax.dynamic_slice(tot, (ROW_WA, me * (DK // N_DEV)), (RANK, DK // N_DEV)),
        "b_a_f": tot[ROW_BA:ROW_BA + 1, 0:DK],
        "w_a2_b": lax.dynamic_slice(tot, (ROW_WA + RANK, DK + me * (DK // N_DEV)), (RANK, DK // N_DEV)),
        "b_a_b": tot[ROW_BA:ROW_BA + 1, DK:2 * DK],
        "gla_norm_g": gn_row[:, 0:HV] + gn_row[:, HV:2 * HV] + gn_row[:, 2 * HV:3 * HV] + gn_row[:, 3 * HV:4 * HV],
        "final_g": row(ROW_FG, D),
    }
    wmv = {
        "b_mod": (b_mod, m_b_mod, v_b_mod), "norm1_g": (norm1_g, m_norm1_g, v_norm1_g),
        "norm2_g": (norm2_g, m_norm2_g, v_norm2_g), "conv_w": (conv_w[0], m_conv_w[0], v_conv_w[0]),
        "conv_b": (conv_b, m_conv_b, v_conv_b), "conv_ln_g": (conv_ln_g, m_conv_ln_g, v_conv_ln_g),
        "conv_ln_b": (conv_ln_b, m_conv_ln_b, v_conv_ln_b), "w_a2_f": (w_a2_f[0], m_w_a2_f[0], v_w_a2_f[0]),
        "b_a_f": (b_a_f, m_b_a_f, v_b_a_f), "w_a2_b": (w_a2_b[0], m_w_a2_b[0], v_w_a2_b[0]),
        "b_a_b": (b_a_b, m_b_a_b, v_b_a_b), "gla_norm_g": (gla_norm_g, m_gla_norm_g, v_gla_norm_g),
        "final_g": (final_g.reshape(1, D), m_final_g.reshape(1, D), v_final_g.reshape(1, D)),
    }
    names_small = list(g_small)
    upd = adamw_small([(g_small[n],) + wmv[n] for n in names_small])
    res = {n: (g_small[n],) + upd[i] for i, n in enumerate(names_small)}
    res["c_ctx"] = (g_cc, d_cc, nm_cc, nv_cc)
    for n in ("w_mod", "w_in", "w_out", "w_gate", "w_up", "w_down"):
        res[n] = tuple(big[n])

    order = ["c_ctx", "w_mod", "b_mod", "norm1_g", "norm2_g", "w_in", "conv_w", "conv_b", "conv_ln_g", "conv_ln_b",
             "w_a2_f", "b_a_f", "w_a2_b", "b_a_b", "gla_norm_g", "w_out", "w_gate", "w_up", "w_down", "final_g"]
    shapes = {"c_ctx": c_ctx.shape, "w_mod": w_mod.shape, "b_mod": b_mod.shape, "norm1_g": norm1_g.shape,
              "norm2_g": norm2_g.shape, "w_in": w_in.shape, "conv_w": conv_w.shape, "conv_b": conv_b.shape,
              "conv_ln_g": conv_ln_g.shape, "conv_ln_b": conv_ln_b.shape, "w_a2_f": w_a2_f.shape,
              "b_a_f": b_a_f.shape, "w_a2_b": w_a2_b.shape, "b_a_b": b_a_b.shape, "gla_norm_g": gla_norm_g.shape,
              "w_out": w_out.shape, "w_gate": w_gate.shape, "w_up": w_up.shape, "w_down": w_down.shape,
              "final_g": final_g.shape}
    outs = [loss, gx.reshape(x.shape)]
    for i in range(4):
        outs += [res[n][i].reshape(shapes[n]) for n in order]
    return tuple(outs)
```

```python
import functools

import jax
import jax.numpy as jnp
from jax import lax
from jax.experimental import pallas as pl
from jax.experimental.pallas import tpu as pltpu

F32 = jnp.float32
MXU_DTYPE = jnp.bfloat16
WIRE_DTYPE = jnp.bfloat16
HI = lax.Precision.HIGHEST
MESH = pl.DeviceIdType.MESH

N_DEV = 8
D = 1024
DC = 512
NH = 4
HK = 64
HV = 128
DK = NH * HK
DV = NH * HV
RANK = 16
CHUNK = 64
SEG = 64
CW = 31
CPAD = 15
DFF = 2816
DIN = 2592
DINP = 2688
TAU = 16.0
EPS = 1e-6
VMEM_LIMIT = 56 * 1024 * 1024

ADAM_LR = 0.001
ADAM_B1 = 0.9
ADAM_B2 = 0.999
ADAM_EPS = 1e-08
ADAM_WD = 0.01
ADAM_STEP = 10


def _mm(a, b):
    return jnp.dot(a.astype(MXU_DTYPE), b.astype(MXU_DTYPE), preferred_element_type=F32)


def _mm_nt(a, b):
    return lax.dot_general(a.astype(MXU_DTYPE), b.astype(MXU_DTYPE), (((1,), (1,)), ((), ())),
                           preferred_element_type=F32)


def _mm_tn(a, b):
    return lax.dot_general(a.astype(MXU_DTYPE), b.astype(MXU_DTYPE), (((0,), (0,)), ((), ())),
                           preferred_element_type=F32)


def _hi(a, b):
    return jnp.dot(a, b, precision=HI, preferred_element_type=F32)


def _hi_nt(a, b):
    return lax.dot_general(a, b, (((1,), (1,)), ((), ())), precision=HI, preferred_element_type=F32)


def _hi_tn(a, b):
    return lax.dot_general(a, b, (((0,), (0,)), ((), ())), precision=HI, preferred_element_type=F32)


def _sigmoid(x):
    return 1.0 / (1.0 + jnp.exp(-x))


def _cparams(n_axes):
    return pltpu.CompilerParams(dimension_semantics=("arbitrary",) * n_axes, vmem_limit_bytes=VMEM_LIMIT)


def _full(shape):
    n = len(shape)
    return pl.BlockSpec(shape, lambda *_: (0,) * n)


def _rows(tt, width):
    return pl.BlockSpec((tt, width), lambda i: (i, 0))


def _sds(shape, dtype=F32):
    return jax.ShapeDtypeStruct(shape, dtype)


def _norm_mod(x, g, sh, sc):
    r = lax.rsqrt(jnp.mean(x * x, axis=-1, keepdims=True) + EPS)
    xn = x * r
    yy = xn * g
    return r, xn, yy, yy * (1.0 + sc) + sh


def _norm_mod_bwd(dh, r, xn, yy, g, sc):
    dsh = jnp.sum(dh, axis=0, keepdims=True)
    dsc = jnp.sum(dh * yy, axis=0, keepdims=True)
    dy = dh * (1.0 + sc)
    dg = jnp.sum(dy * xn, axis=0, keepdims=True)
    dxn = dy * g
    dx = r * (dxn - xn * jnp.mean(dxn * xn, axis=-1, keepdims=True))
    return dsh, dsc, dg, dx


def _acc_rows(ref, first, rows):
    upd = jnp.concatenate(rows + [jnp.zeros((8 - len(rows), rows[0].shape[1]), F32)], axis=0)

    @pl.when(first)
    def _():
        ref[...] = upd

    @pl.when(jnp.logical_not(first))
    def _():
        ref[...] += upd


def _acc(ref, first, val):
    @pl.when(first)
    def _():
        ref[...] = val

    @pl.when(jnp.logical_not(first))
    def _():
        ref[...] += val


def proj_fwd(x, vec, w_int, wa, ba, tt):
    t = x.shape[0]

    def body(x_ref, vec_ref, w_ref, wa_ref, ba_ref, u_ref, q_ref, k_ref, v_ref, g_ref, r_ref, la_ref, h_ref):
        _, _, _, h = _norm_mod(x_ref[...], vec_ref[0:1, :], vec_ref[1:2, :], vec_ref[2:3, :])
        hb = h.astype(MXU_DTYPE)
        h_ref[...] = hb
        p = _mm_nt(hb, w_ref[...])
        u_ref[...] = p[:, 0:1024]
        q_ref[...] = p[:, 1024:1280]
        k_ref[...] = p[:, 1280:1536]
        v_ref[...] = p[:, 1536:2048]
        g_ref[...] = p[:, 2048:2560]
        rr = p[:, 2560:2688]
        r_ref[...] = rr
        z = _mm(rr, wa_ref[...]) + ba_ref[...]
        la_ref[...] = (jnp.minimum(z, 0.0) - jnp.log(1.0 + jnp.exp(-jnp.abs(z)))) * (1.0 / TAU)

    return pl.pallas_call(
        body, name="proj_fwd", grid=(t // tt,),
        in_specs=[_rows(tt, D), _full((8, D)), _full((DINP, D)), _full((128, 512)), _full((1, 512))],
        out_specs=[_rows(tt, 1024), _rows(tt, DK), _rows(tt, DK), _rows(tt, DV), _rows(tt, DV), _rows(tt, 128),
                   _rows(tt, 512), _rows(tt, D)],
        out_shape=[_sds((t, 1024)), _sds((t, DK)), _sds((t, DK)), _sds((t, DV)), _sds((t, DV)), _sds((t, 128)),
                   _sds((t, 512)), _sds((t, D), MXU_DTYPE)],
        compiler_params=_cparams(1),
    )(x, vec, w_int, wa, ba)


def proj_bwd(du, dqkv, dg, dla_f, dla_b, la, r, x, dx1, vec, w_int, wa, tt):
    t = x.shape[0]

    def body(du_ref, dqf_ref, dqb_ref, dkf_ref, dkb_ref, dvf_ref, dvb_ref, dg_ref, dlaf_ref, dlab_ref, la_ref, r_ref,
             x_ref, dx1_ref, vec_ref, w_ref, wa_ref, gx_ref, dp_ref, acc_ref, dba_ref, dwa_ref):
        first = pl.program_id(0) == 0
        dla = jnp.concatenate([dlaf_ref[...], dlab_ref[...]], axis=1)
        dz = dla * (1.0 - jnp.exp(TAU * la_ref[...])) * (1.0 / TAU)
        rr = r_ref[...]
        _acc_rows(dba_ref, first, [jnp.sum(dz, axis=0, keepdims=True)])
        _acc(dwa_ref, first, _mm_tn(rr, dz))
        dr = _mm_nt(dz, wa_ref[...])
        md = lambda a: a.astype(MXU_DTYPE)
        both = lambda a_ref, b_ref: md(a_ref[...].astype(F32) + b_ref[...].astype(F32))
        dp = jnp.concatenate([du_ref[...], both(dqf_ref, dqb_ref), both(dkf_ref, dkb_ref), both(dvf_ref, dvb_ref),
                              dg_ref[...], md(dr)], axis=1)
        dp_ref[...] = dp
        dh = _mm(dp, w_ref[...])
        g, sc = vec_ref[0:1, :], vec_ref[2:3, :]
        rn, xn, yy, _ = _norm_mod(x_ref[...], g, vec_ref[1:2, :], sc)
        dsh, dsc, dgn, dx = _norm_mod_bwd(dh, rn, xn, yy, g, sc)
        gx_ref[...] = dx1_ref[...] + dx
        _acc_rows(acc_ref, first, [dsh, dsc, dgn])

    return pl.pallas_call(
        body, name="proj_bwd", grid=(t // tt,),
        in_specs=[_rows(tt, 1024), _rows(tt, DK), _rows(tt, DK), _rows(tt, DK), _rows(tt, DK), _rows(tt, DV),
                  _rows(tt, DV), _rows(tt, DV), _rows(tt, DK), _rows(tt, DK), _rows(tt, 512),
                  _rows(tt, 128), _rows(tt, D), _rows(tt, D), _full((8, D)), _full((DINP, D)), _full((128, 512))],
        out_specs=[_rows(tt, D), _rows(tt, DINP), _full((8, D)), _full((8, 512)), _full((128, 512))],
        out_shape=[_sds((t, D)), _sds((t, DINP), MXU_DTYPE), _sds((8, D)), _sds((8, 512)), _sds((128, 512))],
        compiler_params=_cparams(1),
    )(du, *dqkv, dg, dla_f, dla_b, la, r, x, dx1, vec, w_int, wa)


def _dot_exact01(m01, x):
    bf = jnp.bfloat16
    w = x.shape[1]
    hi = x.astype(bf)
    r1 = x - hi.astype(F32)
    mid = r1.astype(bf)
    lo = (r1 - mid.astype(F32)).astype(bf)
    y = jnp.dot(m01.astype(bf), jnp.concatenate([hi, mid, lo], axis=1), preferred_element_type=F32)
    return y[:, 0:w] + y[:, w:2 * w] + y[:, 2 * w:3 * w]


def _gla_chunk(d, qc, kc, la_c):
    row = lax.broadcasted_iota(jnp.int32, (CHUNK, CHUNK), 0)
    col = lax.broadcasted_iota(jnp.int32, (CHUNK, CHUNK), 1)
    cum = ((col <= row) if d == 0 else (col >= row)).astype(F32)
    cum_t = ((col >= row) if d == 0 else (col <= row)).astype(F32)
    cum4 = jnp.concatenate([cum] * NH, axis=0)
    head_of_lane = lax.broadcasted_iota(jnp.int32, (1, DK), 1) // HK
    b = _dot_exact01(cum, la_c)
    bl = jnp.sum(la_c, axis=0, keepdims=True)
    eb = jnp.exp(b)
    enb = jnp.exp(-b)
    ekd = jnp.exp(bl - b)
    qt = qc * (HK ** -0.5) * eb
    kt = kc * enb
    kd = kc * ekd
    qst = jnp.concatenate([jnp.where(head_of_lane == h, qt, 0.0) for h in range(NH)], axis=0)
    a = _mm_nt(qst, kt) * cum4
    return cum_t, cum4, head_of_lane, eb, enb, ekd, qt, kt, kd, qst, a, jnp.exp(bl)


def gla_fwd(q, k, v, la, s0, cb):
    t = q.shape[0]
    nc = t // CHUNK
    nb = nc // cb

    def body(qf_ref, kf_ref, vf_ref, laf_ref, qb_ref, kb_ref, vb_ref, lab_ref, s0_ref,
             of_ref, ob_ref, sf_ref, sb_ref, sfin_ref, s_scr):
        i = pl.program_id(0)

        @pl.when(i == 0)
        def _():
            s_scr[...] = s0_ref[...]

        def chunk(d, jj, q_ref, k_ref, v_ref, la_ref, o_ref, sall_ref):
            rows = slice(jj * CHUNK, (jj + 1) * CHUNK)
            vc = v_ref[rows, :]
            _, _, head_of_lane, _, _, _, _, _, kd, qst, a, dec = _gla_chunk(
                d, q_ref[rows, :], k_ref[rows, :], la_ref[rows, :])
            s = s_scr[d]
            sall_ref[jj] = s
            inter = _mm_nt(qst, s)
            outs = []
            for h in range(NH):
                hs = slice(h * CHUNK, (h + 1) * CHUNK)
                outs.append(_mm(a[hs], vc[:, h * HV:(h + 1) * HV]) + inter[hs])
            o_ref[rows, :] = jnp.concatenate(outs, axis=1)
            kv = _mm_tn(vc, kd)
            s_new = dec * s
            for h in range(NH):
                s_new = s_new + jnp.where(head_of_lane == h, kv[h * HV:(h + 1) * HV], 0.0)
            s_scr[d] = s_new

        for j in range(cb):
            chunk(0, j, qf_ref, kf_ref, vf_ref, laf_ref, of_ref, sf_ref)
            chunk(1, cb - 1 - j, qb_ref, kb_ref, vb_ref, lab_ref, ob_ref, sb_ref)

        @pl.when(i == nb - 1)
        def _():
            sfin_ref[...] = s_scr[...]

    tb = cb * CHUNK
    fwd = lambda w, c=0: pl.BlockSpec((tb, w), lambda i: (i, c))
    bwd = lambda w, c=0: pl.BlockSpec((tb, w), lambda i: (nb - 1 - i, c))
    return pl.pallas_call(
        body, name="gla_fwd", grid=(nb,),
        in_specs=[fwd(DK), fwd(DK), fwd(DV), fwd(DK, 0), bwd(DK), bwd(DK), bwd(DV), bwd(DK, 1), _full((2, HV, DK))],
        out_specs=[fwd(DV), bwd(DV), pl.BlockSpec((cb, HV, DK), lambda i: (i, 0, 0)),
                   pl.BlockSpec((cb, HV, DK), lambda i: (nb - 1 - i, 0, 0)), _full((2, HV, DK))],
        out_shape=[_sds((t, DV)), _sds((t, DV)), _sds((nc, HV, DK)), _sds((nc, HV, DK)), _sds((2, HV, DK))],
        scratch_shapes=[pltpu.VMEM((2, HV, DK), F32)],
        compiler_params=_cparams(1),
    )(q, k, v, la, q, k, v, la, s0)


def gla_bwd(q, k, v, la, do, sall_f, sall_b, dsfin, cb):
    t = q.shape[0]
    nc = t // CHUNK
    nb = nc // cb

    def body(qf_ref, kf_ref, vf_ref, laf_ref, dof_ref, sf_ref, qb_ref, kb_ref, vb_ref, lab_ref, dob_ref, sb_ref,
             dsfin_ref, dqf_ref, dkf_ref, dvf_ref, dlaf_ref, dqb_ref, dkb_ref, dvb_ref, dlab_ref, ds0_ref, ds_scr):
        i = pl.program_id(0)

        @pl.when(i == 0)
        def _():
            ds_scr[...] = dsfin_ref[...]

        def chunk(d, jj, q_ref, k_ref, v_ref, la_ref, do_ref, sall_ref, dq_ref, dk_ref, dv_ref, dla_ref):
            rows = slice(jj * CHUNK, (jj + 1) * CHUNK)
            vc = v_ref[rows, :]
            doc = do_ref[rows, :]
            cum_t, cum4, head_of_lane, eb, enb, ekd, qt, kt, kd, qst, a, dec = _gla_chunk(
                d, q_ref[rows, :], k_ref[rows, :], la_ref[rows, :])
            s = sall_ref[jj]
            ds = ds_scr[d]
            hv = lambda x, h: x[:, h * HV:(h + 1) * HV]
            hr = lambda x, h: x[h * CHUNK:(h + 1) * CHUNK]
            fold = lambda x: functools.reduce(
                lambda p, c: p + c, [jnp.where(head_of_lane == h, hr(x, h), 0.0) for h in range(NH)])
            dost = jnp.concatenate([hv(doc, h) for h in range(NH)], axis=0)
            vst = jnp.concatenate([hv(vc, h) for h in range(NH)], axis=0)
            da = jnp.concatenate([_mm_nt(hv(doc, h), hv(vc, h)) for h in range(NH)], axis=0) * cum4
            dqt = fold(_mm(da, kt) + _mm(dost, s))
            dkt = _mm_tn(da, qst)
            kdst = jnp.concatenate([jnp.where(head_of_lane == h, kd, 0.0) for h in range(NH)], axis=0)
            dv_inter = _mm_nt(kdst, ds)
            dv_ref[rows, :] = jnp.concatenate(
                [_mm_tn(hr(a, h), hv(doc, h)) + hr(dv_inter, h) for h in range(NH)], axis=1).astype(MXU_DTYPE)
            dkd = fold(_mm(vst, ds))
            ds_scr[d] = dec * ds + _mm_tn(dost, qst)
            tkd = dkd * kd
            db = dqt * qt - dkt * kt - tkd
            dbl = jnp.sum(ds * s, axis=0, keepdims=True) * dec + jnp.sum(tkd, axis=0, keepdims=True)
            dla_ref[rows, :] = _dot_exact01(cum_t, db) + dbl
            dq_ref[rows, :] = (dqt * eb * (HK ** -0.5)).astype(MXU_DTYPE)
            dk_ref[rows, :] = (dkt * enb + dkd * ekd).astype(MXU_DTYPE)

        for j in range(cb):
            chunk(0, cb - 1 - j, qf_ref, kf_ref, vf_ref, laf_ref, dof_ref, sf_ref, dqf_ref, dkf_ref, dvf_ref, dlaf_ref)
            chunk(1, j, qb_ref, kb_ref, vb_ref, lab_ref, dob_ref, sb_ref, dqb_ref, dkb_ref, dvb_ref, dlab_ref)

        @pl.when(i == nb - 1)
        def _():
            ds0_ref[...] = ds_scr[...]

    tb = cb * CHUNK
    rev = lambda w, c=0: pl.BlockSpec((tb, w), lambda i: (nb - 1 - i, c))
    fro = lambda w, c=0: pl.BlockSpec((tb, w), lambda i: (i, c))
    st_rev = pl.BlockSpec((cb, HV, DK), lambda i: (nb - 1 - i, 0, 0))
    st_fro = pl.BlockSpec((cb, HV, DK), lambda i: (i, 0, 0))
    md = MXU_DTYPE
    return pl.pallas_call(
        body, name="gla_bwd", grid=(nb,),
        in_specs=[rev(DK), rev(DK), rev(DV), rev(DK, 0), rev(DV), st_rev,
                  fro(DK), fro(DK), fro(DV), fro(DK, 1), fro(DV), st_fro, _full((2, HV, DK))],
        out_specs=[rev(DK), rev(DK), rev(DV), rev(DK), fro(DK), fro(DK), fro(DV), fro(DK), _full((2, HV, DK))],
        out_shape=[_sds((t, DK), md), _sds((t, DK), md), _sds((t, DV), md), _sds((t, DK)),
                   _sds((t, DK), md), _sds((t, DK), md), _sds((t, DV), md), _sds((t, DK)), _sds((2, HV, DK))],
        scratch_shapes=[pltpu.VMEM((2, HV, DK), F32)],
        compiler_params=_cparams(1),
    )(q, k, v, la, do, sall_f, q, k, v, la, do, sall_b, dsfin)


def _seg_pos(tt):
    return lax.broadcasted_iota(jnp.int32, (tt, 1), 0) % SEG


def _shifted(x, s, pos, tt):
    y = x if s == 0 else pltpu.roll(x, (-s) % tt, 0)
    return jnp.where((pos + s >= 0) & (pos + s < SEG), y, 0.0)


def _head_norm(o, gn):
    rs, xs = [], []
    for h in range(NH):
        oh = o[:, h * HV:(h + 1) * HV]
        r = lax.rsqrt(jnp.mean(oh * oh, axis=-1, keepdims=True) + EPS)
        rs.append(r)
        xs.append(oh * r)
    return rs, xs


def merge_fwd(u, g, o_f, o_b, x, vec, vc, convw, w_out, tt):
    t = x.shape[0]

    def body(u_ref, g_ref, of_ref, ob_ref, x_ref, vec_ref, vc_ref, cw_ref, w_ref, x1_ref, cat_ref, mix_ref, yc_ref,
             h2_ref):
        a = u_ref[:, 0:DC]
        gate = u_ref[:, DC:2 * DC]
        vv = a * _sigmoid(gate)
        pos = _seg_pos(tt)
        cw = cw_ref[...]
        yc = jnp.zeros((tt, DC), F32) + vc_ref[1:2, :]
        for j in range(CW):
            yc = yc + _shifted(vv, j - CPAD, pos, tt) * cw[j:j + 1, :]
        yc_ref[...] = yc
        mu = jnp.mean(yc, axis=-1, keepdims=True)
        yd = yc - mu
        rs = lax.rsqrt(jnp.mean(yd * yd, axis=-1, keepdims=True) + EPS)
        ln = yd * rs * vc_ref[2:3, :] + vc_ref[3:4, :]
        conv_o = ln * _sigmoid(ln)
        o = of_ref[...] + ob_ref[...]
        _, xs = _head_norm(o, None)
        gg = g_ref[...]
        o2g = jnp.concatenate(xs, axis=1) * vc_ref[0:1, :] * (gg * _sigmoid(gg))
        cat = jnp.concatenate([conv_o, o2g], axis=1).astype(MXU_DTYPE)
        cat_ref[...] = cat
        mix = _mm(cat, w_ref[...])
        mix_ref[...] = mix
        x1 = x_ref[...] + vec_ref[0:1, :] * mix
        x1_ref[...] = x1
        _, _, _, h2 = _norm_mod(x1, vec_ref[1:2, :], vec_ref[2:3, :], vec_ref[3:4, :])
        h2_ref[...] = h2.astype(MXU_DTYPE)

    return pl.pallas_call(
        body, name="merge_fwd", grid=(t // tt,),
        in_specs=[_rows(tt, 1024), _rows(tt, DV), _rows(tt, DV), _rows(tt, DV), _rows(tt, D),
                  _full((8, D)), _full((8, DC)), _full((32, DC)), _full((D, D))],
        out_specs=[_rows(tt, D), _rows(tt, D), _rows(tt, D), _rows(tt, DC), _rows(tt, D)],
        out_shape=[_sds((t, D)), _sds((t, D), MXU_DTYPE), _sds((t, D)), _sds((t, DC)), _sds((t, D), MXU_DTYPE)],
        compiler_params=_cparams(1),
    )(u, g, o_f, o_b, x, vec, vc, convw, w_out)


def merge_bwd(dx1, mix, u, g, o_f, o_b, yc, vec, vc, convw, w_out, tt):
    t = dx1.shape[0]

    def body(dx1_ref, mix_ref, u_ref, g_ref, of_ref, ob_ref, yc_ref, vec_ref, vc_ref, cw_ref, w_ref,
             du_ref, dg_ref, do_ref, dmix_ref, acc1_ref, acc2_ref, dcw_ref):
        first = pl.program_id(0) == 0
        dx1v = dx1_ref[...]
        dg1 = jnp.sum(dx1v * mix_ref[...], axis=0, keepdims=True)
        dmix = (vec_ref[0:1, :] * dx1v).astype(MXU_DTYPE)
        dmix_ref[...] = dmix
        dcat = _mm_nt(dmix, w_ref[...])
        dconv_o = dcat[:, 0:DC]
        do2 = dcat[:, DC:2 * DC]
        gn = vc_ref[0:1, :]
        o = of_ref[...] + ob_ref[...]
        rs, xs = _head_norm(o, None)
        xn = jnp.concatenate(xs, axis=1)
        gg = g_ref[...]
        sg = _sigmoid(gg)
        don = do2 * (gg * sg)
        dg_ref[...] = (do2 * (xn * gn) * (sg * (1.0 + gg * (1.0 - sg)))).astype(MXU_DTYPE)
        dgn = jnp.sum(don * xn, axis=0, keepdims=True)
        dxn = don * gn
        dos = []
        for h in range(NH):
            dh = dxn[:, h * HV:(h + 1) * HV]
            dos.append(rs[h] * (dh - xs[h] * jnp.mean(dh * xs[h], axis=-1, keepdims=True)))
        do_ref[...] = jnp.concatenate(dos, axis=1).astype(MXU_DTYPE)
        yc = yc_ref[...]
        mu = jnp.mean(yc, axis=-1, keepdims=True)
        yd = yc - mu
        rstd = lax.rsqrt(jnp.mean(yd * yd, axis=-1, keepdims=True) + EPS)
        yhat = yd * rstd
        lg = vc_ref[2:3, :]
        ln = yhat * lg + vc_ref[3:4, :]
        sl = _sigmoid(ln)
        dln = dconv_o * (sl * (1.0 + ln * (1.0 - sl)))
        dlb = jnp.sum(dln, axis=0, keepdims=True)
        dlg = jnp.sum(dln * yhat, axis=0, keepdims=True)
        dyh = dln * lg
        dyc = rstd * (dyh - jnp.mean(dyh, axis=-1, keepdims=True)
                      - yhat * jnp.mean(dyh * yhat, axis=-1, keepdims=True))
        dcb = jnp.sum(dyc, axis=0, keepdims=True)
        a = u_ref[:, 0:DC]
        gate = u_ref[:, DC:2 * DC]
        sgt = _sigmoid(gate)
        vv = a * sgt
        pos = _seg_pos(tt)
        cw = cw_ref[...]
        dvv = jnp.zeros((tt, DC), F32)
        dws = []
        for j in range(CW):
            shifted_dyc = _shifted(dyc, CPAD - j, pos, tt)
            dvv = dvv + shifted_dyc * cw[j:j + 1, :]
            dws.append(jnp.sum(shifted_dyc * vv, axis=0, keepdims=True))
        dws.append(jnp.zeros((1, DC), F32))
        du_ref[:, 0:DC] = (dvv * sgt).astype(MXU_DTYPE)
        du_ref[:, DC:2 * DC] = (dvv * a * sgt * (1.0 - sgt)).astype(MXU_DTYPE)
        _acc_rows(acc1_ref, first, [dg1])
        _acc_rows(acc2_ref, first, [dgn, dcb, dlg, dlb])
        _acc(dcw_ref, first, jnp.concatenate(dws, axis=0))

    return pl.pallas_call(
        body, name="merge_bwd", grid=(t // tt,),
        in_specs=[_rows(tt, D), _rows(tt, D), _rows(tt, 1024), _rows(tt, DV), _rows(tt, DV), _rows(tt, DV),
                  _rows(tt, DC),
                  _full((8, D)), _full((8, DC)), _full((32, DC)), _full((D, D))],
        out_specs=[_rows(tt, 1024), _rows(tt, DV), _rows(tt, DV), _rows(tt, D), _full((8, D)), _full((8, DC)),
                   _full((32, DC))],
        out_shape=[_sds((t, 1024), MXU_DTYPE), _sds((t, DV), MXU_DTYPE), _sds((t, DV), MXU_DTYPE),
                   _sds((t, D), MXU_DTYPE), _sds((8, D)),
                   _sds((8, DC)), _sds((32, DC))],
        compiler_params=_cparams(1),
    )(dx1, mix, u, g, o_f, o_b, yc, vec, vc, convw, w_out)


FN = DFF // 2


def ffn_gate_up(h2, wg_t, wu_t, tt):
    t = h2.shape[0]

    def body(h2_ref, wg_ref, wu_ref, s_ref, d_ref, hid_ref):
        h2v = h2_ref[...]
        gt = _mm_nt(h2v, wg_ref[...])
        up = _mm_nt(h2v, wu_ref[...])
        sg = _sigmoid(gt)
        act = gt * sg
        s_ref[...] = act.astype(MXU_DTYPE)
        d_ref[...] = (up * (sg * (1.0 + gt * (1.0 - sg)))).astype(MXU_DTYPE)
        hid_ref[...] = (act * up).astype(MXU_DTYPE)

    blk = pl.BlockSpec((tt, FN), lambda j, i: (i, j))
    wblk = pl.BlockSpec((FN, D), lambda j, i: (j, 0))
    return pl.pallas_call(
        body, name="ffn_gate_up", grid=(2, t // tt),
        in_specs=[pl.BlockSpec((tt, D), lambda j, i: (i, 0)), wblk, wblk],
        out_specs=[blk, blk, blk],
        out_shape=[_sds((t, DFF), MXU_DTYPE)] * 3,
        compiler_params=_cparams(2),
    )(h2, wg_t, wu_t)


def ffn_down_loss(hid, x1, tgt, vec, w_down, tt):
    t = x1.shape[0]

    def body(hid_ref, x1_ref, tgt_ref, vec_ref, w_ref, dx2_ref, dff_ref, acc_ref):
        first = pl.program_id(0) == 0
        g2 = vec_ref[0:1, :]
        fg = vec_ref[1:2, :]
        ff = _mm(hid_ref[...], w_ref[...])
        x2 = x1_ref[...] + g2 * ff
        rf = lax.rsqrt(jnp.mean(x2 * x2, axis=-1, keepdims=True) + EPS)
        xn = x2 * rf
        err = xn * fg - tgt_ref[...]
        dy = err * (1.0 / D)
        dfg = jnp.sum(dy * xn, axis=0, keepdims=True)
        dxn = dy * fg
        dx2 = rf * (dxn - xn * jnp.mean(dxn * xn, axis=-1, keepdims=True))
        dx2_ref[...] = dx2
        dff_ref[...] = (g2 * dx2).astype(MXU_DTYPE)
        dg2 = jnp.sum(dx2 * ff, axis=0, keepdims=True)
        loss = jnp.sum(err * err, axis=0, keepdims=True) * (0.5 / D)
        _acc_rows(acc_ref, first, [dg2, dfg, loss])

    return pl.pallas_call(
        body, name="ffn_down_loss", grid=(t // tt,),
        in_specs=[_rows(tt, DFF), _rows(tt, D), _rows(tt, D), _full((8, D)), _full((DFF, D))],
        out_specs=[_rows(tt, D), _rows(tt, D), _full((8, D))],
        out_shape=[_sds((t, D)), _sds((t, D), MXU_DTYPE), _sds((8, D))],
        compiler_params=_cparams(1),
    )(hid, x1, tgt, vec, w_down)


def ffn_dhid(dff, s, d, w_down, tt):
    t = dff.shape[0]

    def body(dff_ref, s_ref, d_ref, w_ref, dgt_ref, dup_ref):
        dhid = _mm_nt(dff_ref[...], w_ref[...])
        dgt_ref[...] = (dhid * d_ref[...].astype(F32)).astype(MXU_DTYPE)
        dup_ref[...] = (dhid * s_ref[...].astype(F32)).astype(MXU_DTYPE)

    blk = pl.BlockSpec((tt, FN), lambda j, i: (i, j))
    return pl.pallas_call(
        body, name="ffn_dhid", grid=(2, t // tt),
        in_specs=[pl.BlockSpec((tt, D), lambda j, i: (i, 0)), blk, blk, pl.BlockSpec((FN, D), lambda j, i: (j, 0))],
        out_specs=[blk, blk],
        out_shape=[_sds((t, DFF), MXU_DTYPE), _sds((t, DFF), MXU_DTYPE)],
        compiler_params=_cparams(2),
    )(dff, s, d, w_down)


def ffn_dh2(dgt, dup, x1, dx2, vec, wg_t, wu_t, tt):
    t = x1.shape[0]

    def body(dgt_ref, dup_ref, x1_ref, dx2_ref, vec_ref, wg_ref, wu_ref, dx1_ref, acc_ref):
        first = pl.program_id(0) == 0
        dh2 = _mm(dgt_ref[...], wg_ref[...]) + _mm(dup_ref[...], wu_ref[...])
        g, sc = vec_ref[0:1, :], vec_ref[2:3, :]
        r, xn, yy, _ = _norm_mod(x1_ref[...], g, vec_ref[1:2, :], sc)
        dsh, dsc, dgn, dx = _norm_mod_bwd(dh2, r, xn, yy, g, sc)
        dx1_ref[...] = dx2_ref[...] + dx
        _acc_rows(acc_ref, first, [dsh, dsc, dgn])

    return pl.pallas_call(
        body, name="ffn_dh2", grid=(t // tt,),
        in_specs=[_rows(tt, DFF), _rows(tt, DFF), _rows(tt, D), _rows(tt, D), _full((8, D)), _full((DFF, D)),
                  _full((DFF, D))],
        out_specs=[_rows(tt, D), _full((8, D))],
        out_shape=[_sds((t, D)), _sds((8, D))],
        compiler_params=_cparams(1),
    )(dgt, dup, x1, dx2, vec, wg_t, wu_t)


def tn_matmul(a, b, bm, bt, init=None):
    t, m = a.shape
    n = b.shape[1]
    nk = t // bt

    def body(*refs):
        if init is None:
            a_ref, b_ref, o_ref, wire_ref = refs
        else:
            a_ref, b_ref, i_ref, o_ref, wire_ref = refs
        prod = _mm_tn(a_ref[...], b_ref[...])

        @pl.when(pl.program_id(1) == 0)
        def _():
            o_ref[...] = prod if init is None else prod + i_ref[...]

        @pl.when(pl.program_id(1) != 0)
        def _():
            o_ref[...] += prod

        @pl.when(pl.program_id(1) == nk - 1)
        def _():
            wire_ref[...] = o_ref[...].astype(WIRE_DTYPE)

    in_specs = [pl.BlockSpec((bt, bm), lambda i, k: (k, i)), pl.BlockSpec((bt, n), lambda i, k: (k, 0))]
    args = [a, b]
    if init is not None:
        in_specs.append(pl.BlockSpec((bm, n), lambda i, k: (i, 0)))
        args.append(init)
    oblk = pl.BlockSpec((bm, n), lambda i, k: (i, 0))
    return pl.pallas_call(
        body, name="tn_matmul", grid=(m // bm, nk),
        in_specs=in_specs, out_specs=[oblk, oblk],
        out_shape=[_sds((m, n)), _sds((m, n), WIRE_DTYPE)], compiler_params=_cparams(2),
    )(*args)


def _adamw(w, g, m, v):
    m = ADAM_B1 * m + (1.0 - ADAM_B1) * g
    v = ADAM_B2 * v + (1.0 - ADAM_B2) * (g * g)
    m_hat = m / (1.0 - ADAM_B1 ** ADAM_STEP)
    v_hat = v / (1.0 - ADAM_B2 ** ADAM_STEP)
    delta = -ADAM_LR * (m_hat / (jnp.sqrt(v_hat) + ADAM_EPS) + ADAM_WD * w)
    return delta, m, v


def adamw_sharded(own, recv, w, m, v):
    shape = w.shape

    def body(own_ref, recv_ref, w_ref, m_ref, v_ref, g_ref, d_ref, mo_ref, vo_ref):
        g = own_ref[...]
        for k in range(N_DEV - 1):
            g = g + recv_ref[k].astype(F32)
        g_ref[...] = g
        d_ref[...], mo_ref[...], vo_ref[...] = _adamw(w_ref[...], g, m_ref[...], v_ref[...])

    return pl.pallas_call(
        body, name="adamw_sharded",
        in_specs=[_full(shape), _full((N_DEV - 1,) + shape), _full(shape), _full(shape), _full(shape)],
        out_specs=[_full(shape)] * 4, out_shape=[_sds(shape)] * 4, grid=(1,),
        compiler_params=_cparams(1),
    )(own, recv, w, m, v)


def adamw_small(items):
    n = len(items)
    flat = [a for it in items for a in it]

    def body(*refs):
        ins, outs = refs[:4 * n], refs[4 * n:]
        for i in range(n):
            g, w, m, v = (r[...] for r in ins[4 * i:4 * i + 4])
            outs[3 * i][...], outs[3 * i + 1][...], outs[3 * i + 2][...] = _adamw(w, g, m, v)

    out = pl.pallas_call(
        body, name="adamw_small", grid=(1,),
        in_specs=[_full(a.shape) for a in flat],
        out_specs=[_full(it[1].shape) for it in items for _ in range(3)],
        out_shape=[_sds(it[1].shape) for it in items for _ in range(3)],
        compiler_params=_cparams(1),
    )(*flat)
    return [tuple(out[3 * i:3 * i + 3]) for i in range(n)]


def _mesh_pos():
    x, y, c = lax.axis_index("x"), lax.axis_index("y"), lax.axis_index("c")
    me = 4 * x + 2 * y + c
    peers = []
    for k in range(1, N_DEV):
        peers.append(((1 - x) if (k >> 2) & 1 else x, (1 - y) if (k >> 1) & 1 else y, (1 - c) if k & 1 else c))
    return me, peers


def _all_gather(buf, send_sems, recv_sems, me, peers):
    sends = []
    for k, peer in enumerate(peers):
        cp = pltpu.make_async_remote_copy(src_ref=buf.at[me], dst_ref=buf.at[me], send_sem=send_sems.at[k],
                                          recv_sem=recv_sems.at[k], device_id=peer, device_id_type=MESH)
        cp.start()
        sends.append(cp)
    for k, peer in enumerate(peers):
        src = jnp.bitwise_xor(me, k + 1)
        pltpu.make_async_remote_copy(src_ref=buf.at[src], dst_ref=buf.at[src], send_sem=send_sems.at[k],
                                     recv_sem=recv_sems.at[k], device_id=peer, device_id_type=MESH).wait_recv()
    for cp in sends:
        cp.wait_send()


_VMEM = pl.BlockSpec(memory_space=pltpu.VMEM)
_ANY = pl.BlockSpec(memory_space=pl.ANY)
_SEMS = pltpu.SemaphoreType.DMA((N_DEV - 1,))


def mod_forward(c, c_ctx, w_mod_sh, b_mod):
    ncol = w_mod_sh.shape[1]

    def body(c_ref, cc_ref, w_ref, b_ref, mod_ref, s_ref, cbuf, pbuf, s1, r1, s2, r2):
        me, peers = _mesh_pos()
        cbuf[me] = jnp.broadcast_to(c_ref[...], (8, D))
        _all_gather(cbuf, s1, r1, me, peers)
        rows = [cbuf[j, 0:1, :] for j in range(N_DEV)] + [cc_ref[...], jnp.zeros((7, D), F32)]
        sx = jnp.concatenate(rows, axis=0)
        s = sx * _sigmoid(sx)
        s_ref[...] = s
        pbuf[me] = _hi(s, w_ref[...])
        _all_gather(pbuf, s2, r2, me, peers)
        for j in range(N_DEV):
            mod_ref[:, j * ncol:(j + 1) * ncol] = pbuf[j] + b_ref[:, j * ncol:(j + 1) * ncol]

    return pl.pallas_call(
        body, name="mod_forward",
        in_specs=[_VMEM] * 4, out_specs=[_VMEM] * 2,
        out_shape=[_sds((16, N_DEV * ncol)), _sds((16, D))],
        scratch_shapes=[pltpu.VMEM((N_DEV, 8, D), F32), pltpu.VMEM((N_DEV, 16, ncol), F32), _SEMS, _SEMS, _SEMS,
                        _SEMS],
        compiler_params=pltpu.CompilerParams(vmem_limit_bytes=VMEM_LIMIT),
    )(c, c_ctx, w_mod_sh, b_mod)


def gather_sum_small(pack):
    shape = pack.shape

    def body(p_ref, tot_ref, gat_ref, send_sems, recv_sems):
        me, peers = _mesh_pos()
        gat_ref[me] = p_ref[...]
        _all_gather(gat_ref, send_sems, recv_sems, me, peers)
        tot = gat_ref[0]
        for j in range(1, N_DEV):
            tot = tot + gat_ref[j]
        tot_ref[...] = tot

    return pl.pallas_call(
        body, name="gather_sum_small", in_specs=[_VMEM], out_specs=[_VMEM, _VMEM],
        out_shape=[_sds(shape), _sds((N_DEV,) + shape)], scratch_shapes=[_SEMS, _SEMS],
        compiler_params=pltpu.CompilerParams(vmem_limit_bytes=VMEM_LIMIT),
    )(pack)


def mod_backward(s, dm_sh, w, m, v, cc, m_cc, v_cc):
    shape = w.shape

    def body(s_ref, dm_ref, w_ref, m_ref, v_ref, cc_ref, mcc_ref, vcc_ref,
             gw_ref, dw_ref, mw_ref, vw_ref, gc_ref, dc_ref, mc_ref, vc_ref, pbuf, send_sems, recv_sems):
        me, peers = _mesh_pos()
        wv = w_ref[...]
        pbuf[me] = _hi_nt(dm_ref[8:16, :], wv)
        _all_gather(pbuf, send_sems, recv_sems, me, peers)
        g = _hi_tn(s_ref[...], dm_ref[...])
        gw_ref[...] = g
        dw_ref[...], mw_ref[...], vw_ref[...] = _adamw(wv, g, m_ref[...], v_ref[...])
        tot = pbuf[0]
        for j in range(1, N_DEV):
            tot = tot + pbuf[j]
        ccv = cc_ref[...]
        sg = _sigmoid(ccv)
        gc = tot[0:1, :] * (sg * (1.0 + ccv * (1.0 - sg)))
        gc_ref[...] = gc
        dc_ref[...], mc_ref[...], vc_ref[...] = _adamw(ccv, gc, mcc_ref[...], vcc_ref[...])

    return pl.pallas_call(
        body, name="mod_backward", in_specs=[_VMEM] * 8, out_specs=[_VMEM] * 8,
        out_shape=[_sds(shape)] * 4 + [_sds((1, D))] * 4,
        scratch_shapes=[pltpu.VMEM((N_DEV, 8, D), F32), _SEMS, _SEMS],
        compiler_params=pltpu.CompilerParams(vmem_limit_bytes=VMEM_LIMIT),
    )(s, dm_sh, w, m, v, cc, m_cc, v_cc)


_HBM = pl.BlockSpec(memory_space=pltpu.HBM)
_SEM = pl.BlockSpec(memory_space=pltpu.SEMAPHORE)
_EFFECT = pltpu.SideEffectType.DATAFLOW_SIDE_EFFECTING
_hbm = lambda a: pltpu.with_memory_space_constraint(a, pltpu.HBM)


def gather_start(shards, me, tag):
    n = len(shards)
    sems = pltpu.SemaphoreType.DMA((7 * n,))
    lands = [lax.dynamic_update_slice(lax.empty((N_DEV,) + s.shape, s.dtype), s[None], (me, 0, 0)) for s in shards]

    def body(*refs):
        s_refs, l_refs = refs[:n], refs[n:2 * n]
        send_sems, recv_sems = refs[2 * n], refs[2 * n + 1]
        token = refs[-1]
        my, peers = _mesh_pos()
        for w in range(n):
            for k, peer in enumerate(peers):
                pltpu.make_async_remote_copy(
                    src_ref=s_refs[w], dst_ref=l_refs[w].at[my], send_sem=send_sems.at[w * 7 + k],
                    recv_sem=recv_sems.at[w * 7 + k], device_id=peer, device_id_type=MESH).start()
        token[...] = jnp.zeros_like(token)

    out = pl.pallas_call(
        body, name="gather_start_" + tag,
        out_shape=(sems, sems) + tuple(pltpu.HBM(a.shape, a.dtype) for a in list(shards) + lands) + (_sds((8, 128)),),
        in_specs=(_HBM,) * (2 * n), out_specs=(_SEM, _SEM) + (_HBM,) * (2 * n) + (_VMEM,),
        input_output_aliases={i: i + 2 for i in range(2 * n)},
        compiler_params=pltpu.CompilerParams(has_side_effects=_EFFECT),
    )(*[_hbm(a) for a in list(shards) + lands])
    return out[0], out[1], list(out[2:2 + n]), list(out[2 + n:2 + 2 * n]), out[-1]


def gather_wait(send_sems, recv_sems, shards, lands, after, tag):
    n = len(shards)

    def body(*refs):
        s_refs, l_refs = refs[:n], refs[n:2 * n]
        send_sems, recv_sems = refs[2 * n], refs[2 * n + 1]
        my, peers = _mesh_pos()
        for w in range(n):
            for k, peer in enumerate(peers):
                src = jnp.bitwise_xor(my, k + 1)
                cp = pltpu.make_async_remote_copy(
                    src_ref=s_refs[w], dst_ref=l_refs[w].at[src], send_sem=send_sems.at[w * 7 + k],
                    recv_sem=recv_sems.at[w * 7 + k], device_id=peer, device_id_type=MESH)
                cp.wait_send()
                cp.wait_recv()

    out = pl.pallas_call(
        body, name="gather_wait_" + tag,
        out_shape=tuple(pltpu.HBM(a.shape, a.dtype) for a in list(shards) + list(lands)),
        in_specs=(_HBM,) * (2 * n) + (_SEM, _SEM, _ANY), out_specs=(_HBM,) * (2 * n),
        input_output_aliases={i: i for i in range(2 * n)},
        compiler_params=pltpu.CompilerParams(has_side_effects=_EFFECT),
    )(*shards, *lands, send_sems, recv_sems, after)
    return list(out[n:2 * n])


def scatter_start(grads, tag):
    n = len(grads)
    sems = pltpu.SemaphoreType.DMA((7 * n,))
    lands = [lax.empty((N_DEV - 1,) + g.shape[1:], g.dtype) for g in grads]

    def body(*refs):
        g_refs, l_refs = refs[:n], refs[n:2 * n]
        send_sems, recv_sems = refs[2 * n], refs[2 * n + 1]
        token = refs[-1]
        me, peers = _mesh_pos()
        for w in range(n):
            for k, peer in enumerate(peers):
                dst = jnp.bitwise_xor(me, k + 1)
                pltpu.make_async_remote_copy(
                    src_ref=g_refs[w].at[dst], dst_ref=l_refs[w].at[k], send_sem=send_sems.at[w * 7 + k],
                    recv_sem=recv_sems.at[w * 7 + k], device_id=peer, device_id_type=MESH).start()
        token[...] = jnp.zeros_like(token)

    out = pl.pallas_call(
        body, name="scatter_start_" + tag,
        out_shape=(sems, sems) + tuple(pltpu.HBM(a.shape, a.dtype) for a in list(grads) + lands) + (_sds((8, 128)),),
        in_specs=(_HBM,) * (2 * n), out_specs=(_SEM, _SEM) + (_HBM,) * (2 * n) + (_VMEM,),
        input_output_aliases={i: i + 2 for i in range(2 * n)},
        compiler_params=pltpu.CompilerParams(has_side_effects=_EFFECT),
    )(*[_hbm(a) for a in list(grads) + lands])
    return out[0], out[1], list(out[2:2 + n]), list(out[2 + n:2 + 2 * n]), out[-1]


def scatter_wait(send_sems, recv_sems, grads, lands, after, tag):
    n = len(grads)

    def body(*refs):
        g_refs, l_refs = refs[:n], refs[n:2 * n]
        send_sems, recv_sems = refs[2 * n], refs[2 * n + 1]
        me, peers = _mesh_pos()
        for w in range(n):
            for k, peer in enumerate(peers):
                dst = jnp.bitwise_xor(me, k + 1)
                cp = pltpu.make_async_remote_copy(
                    src_ref=g_refs[w].at[dst], dst_ref=l_refs[w].at[k], send_sem=send_sems.at[w * 7 + k],
                    recv_sem=recv_sems.at[w * 7 + k], device_id=peer, device_id_type=MESH)
                cp.wait_send()
                cp.wait_recv()

    out = pl.pallas_call(
        body, name="scatter_wait_" + tag,
        out_shape=tuple(pltpu.HBM(a.shape, a.dtype) for a in list(grads) + list(lands)),
        in_specs=(_HBM,) * (2 * n) + (_SEM, _SEM, _ANY), out_specs=(_HBM,) * (2 * n),
        input_output_aliases={i: i for i in range(2 * n)},
        compiler_params=pltpu.CompilerParams(has_side_effects=_EFFECT),
    )(*grads, *lands, send_sems, recv_sems, after)
    return list(out[n:2 * n])


def _vec8(rows, width):
    rid = lax.broadcasted_iota(jnp.int32, (8, width), 0)
    out = jnp.zeros((8, width), F32)
    for i, r in enumerate(rows):
        r = r.reshape(-1)
        r = jnp.pad(r, (0, width - r.shape[0]))
        out = jnp.where(rid == i, r[None, :], out)
    return out


def local_step(x, ctx, tgt, mod, mod_c, small, w_int, start, late_weights, grads_ready, small_ready, tt, tt_ctx, cb,
               cb_ctx):
    sh1, sc1, g1, sh2, sc2, g2 = [mod[i * D:(i + 1) * D] for i in range(6)]
    csh1, csc1 = mod_c[0:D], mod_c[D:2 * D]
    vec1 = _vec8([small["norm1_g"], sh1, sc1], D)
    vec1c = _vec8([small["norm1_g"], csh1, csc1], D)
    vec2 = _vec8([small["norm2_g"], sh2, sc2], D)
    vec3 = _vec8([g2, small["final_g"]], D)
    vecm = _vec8([g1, small["norm2_g"], sh2, sc2], D)
    vcm = _vec8([jnp.tile(small["gla_norm_g"].reshape(HV), NH), small["conv_b"], small["conv_ln_g"],
                 small["conv_ln_b"]], DC)
    convw = jnp.pad(small["conv_w"], ((0, 1), (0, 0)))
    wa = jnp.zeros((128, 512), F32)
    wa = wa.at[0:RANK, 0:DK].set(small["w_a2_f"]).at[RANK:2 * RANK, DK:2 * DK].set(small["w_a2_b"])
    ba = jnp.concatenate([small["b_a_f"].reshape(1, DK), small["b_a_b"].reshape(1, DK)], axis=1)

    _, _, kc, vc_, _, rc, lac, hc = proj_fwd(ctx, vec1c + start, w_int, wa, ba, tt_ctx)
    qc0 = jnp.zeros_like(kc)
    _, _, sallf_c, sallb_c, sfin_c = gla_fwd(qc0, kc, vc_, lac, jnp.zeros((2, HV, DK), F32), cb_ctx)
    u, q, k, v, g, r, la, h = proj_fwd(x, vec1, w_int, wa, ba, tt)
    o_f, o_b, sall_f, sall_b, _ = gla_fwd(q, k, v, la, sfin_c, cb)
    w_out, wg_t, wu_t, w_down = late_weights(o_b)
    x1, cat, mix, yc, h2 = merge_fwd(u, g, o_f, o_b, x, vecm, vcm, convw, w_out, tt)
    act, dact, hid = ffn_gate_up(h2, wg_t, wu_t, tt)
    dx2, dff, acc3 = ffn_down_loss(hid, x1, tgt, vec3, w_down, tt)
    dgt, dup = ffn_dhid(dff, act, dact, w_down, tt)
    dx1, acc2 = ffn_dh2(dgt, dup, x1, dx2, vec2, wg_t, wu_t, tt)
    bt = min(2048, x.shape[0])
    gw = {"w_down": tn_matmul(hid, dff, FN, bt), "wg_t": tn_matmul(dgt, h2, FN, bt),
          "wu_t": tn_matmul(dup, h2, FN, bt)}
    vecm = vecm + grads_ready(("wg_t", "wu_t", "w_down"), gw)
    du, dg, do, dmix, accm1, accm2, dconvw = merge_bwd(dx1, mix, u, g, o_f, o_b, yc, vecm, vcm, convw, w_out, tt)
    gw["w_out"] = tn_matmul(cat, dmix, 512, bt)
    dsfin = jnp.zeros((2, HV, DK), F32) + grads_ready(("w_out",), gw)
    dqf, dkf, dvf, dlaf, dqb, dkb, dvb, dlab, ds0 = gla_bwd(q, k, v, la, do, sall_f, sall_b, dsfin, cb)
    gx, dp, acc1, dba, dwa = proj_bwd(du, (dqf, dqb, dkf, dkb, dvf, dvb), dg, dlaf, dlab, la, r, x, dx1, vec1, w_int,
                                      wa, tt)
    tcx = ctx.shape[0]
    zc = lambda w, dt=MXU_DTYPE: jnp.zeros((tcx, w), dt)
    _, dkf, dvf, dlaf, _, dkb, dvb, dlab, _ = gla_bwd(qc0, kc, vc_, lac, zc(DV), sallf_c, sallb_c, ds0, cb_ctx)
    _, dpc, acc1c, dbac, dwac = proj_bwd(zc(1024), (zc(DK), zc(DK), dkf, dkb, dvf, dvb), zc(DV), dlaf, dlab, lac, rc,
                                         ctx, zc(D, F32), vec1c, w_int, wa, tt_ctx)
    dwa_t = dwa + dwac
    dba_t = dba + dbac
    gs = {
        "norm1_g": acc1[2] + acc1c[2], "norm2_g": acc2[2], "final_g": acc3[1], "loss": acc3[2],
        "gla_norm_g": accm2[0], "conv_b": accm2[1], "conv_ln_g": accm2[2], "conv_ln_b": accm2[3],
        "conv_w": dconvw, "b_a": dba_t[0], "w_a2": dwa_t,
    }
    dmod = _vec8([acc1[0], acc1[1], accm1[0], acc2[0], acc2[1], acc3[0]], D)
    dmod_c = _vec8([acc1c[0], acc1c[1]], D)
    dpc = dpc + small_ready(gs, dmod, dmod_c).astype(dpc.dtype)
    btc = min(1024, tcx)
    gw["w_int"] = tn_matmul(dp, h, 896, bt, init=tn_matmul(dpc, hc, 896, btc)[0])
    grads_ready(("w_int",), gw)
    return gx, gw


PACK_ROWS = 96
ROW_N1, ROW_N2, ROW_FG, ROW_LOSS, ROW_GN, ROW_CB, ROW_LG, ROW_LB, ROW_BA = 0, 1, 2, 3, 4, 5, 6, 7, 8
ROW_DMOD, ROW_DMODC, ROW_CW, ROW_WA = 16, 24, 32, 64


def _pack_small(gs, dmod, dmod_c):
    pad = lambda a: jnp.pad(a, ((0, 0), (0, D - a.shape[1])))
    singles = _vec8([gs["norm1_g"], gs["norm2_g"], gs["final_g"], gs["loss"], gs["gla_norm_g"], gs["conv_b"],
                     gs["conv_ln_g"], gs["conv_ln_b"]], D)
    return jnp.concatenate([singles, _vec8([gs["b_a"]], D), dmod, dmod_c, pad(gs["conv_w"]), pad(gs["w_a2"][0:32])],
                           axis=0)


def kernel(x, c, ctx, c_ctx, w_mod, b_mod, norm1_g, norm2_g, w_in, conv_w, conv_b, conv_ln_g, conv_ln_b, w_a2_f, b_a_f, w_a2_b, b_a_b, gla_norm_g, w_out, w_gate, w_up, w_down, final_g, loss_target, m_c_ctx, m_w_mod, m_b_mod, m_norm1_g, m_norm2_g, m_w_in, m_conv_w, m_conv_b, m_conv_ln_g, m_conv_ln_b, m_w_a2_f, m_b_a_f, m_w_a2_b, m_b_a_b, m_gla_norm_g, m_w_out, m_w_gate, m_w_up, m_w_down, m_final_g, v_c_ctx, v_w_mod, v_b_mod, v_norm1_g, v_norm2_g, v_w_in, v_conv_w, v_conv_b, v_conv_ln_g, v_conv_ln_b, v_w_a2_f, v_b_a_f, v_w_a2_b, v_b_a_b, v_gla_norm_g, v_w_out, v_w_gate, v_w_up, v_w_down, v_final_g):
    me = 4 * lax.axis_index("x") + 2 * lax.axis_index("y") + lax.axis_index("c")
    t = x.shape[1]
    tcx = ctx.shape[1]
    r_in, r_out, r_ff = w_in.shape[2], w_out.shape[1], w_gate.shape[2]
    r_in_b = -(-r_in // 16) * 16

    tb = lambda w: w.T.astype(MXU_DTYPE)
    i_send, i_recv, i_thru, i_lands, i_token = gather_start([jnp.pad(tb(w_in[0]), ((0, r_in_b - r_in), (0, 0)))], me,
                                                            "w_in")

    mod_all, s_all = mod_forward(c + i_token[0:1, 0:1], c_ctx.reshape(1, D), w_mod[0], b_mod)
    mod = lax.dynamic_slice(mod_all, (me, 0), (1, 6 * D)).reshape(6 * D)
    mod_c = mod_all[8]

    small = dict(norm1_g=norm1_g[0], norm2_g=norm2_g[0], final_g=final_g, gla_norm_g=gla_norm_g[0],
                 conv_b=conv_b[0], conv_ln_g=conv_ln_g[0], conv_ln_b=conv_ln_b[0], b_a_f=b_a_f[0], b_a_b=b_a_b[0])
    sm_pack = jnp.zeros((48, DC), F32)
    sm_pack = lax.dynamic_update_slice(sm_pack, conv_w[0], (0, me * (DC // N_DEV)))
    sm_pack = lax.dynamic_update_slice(sm_pack, w_a2_f[0], (32, me * (DK // N_DEV)))
    sm_pack = lax.dynamic_update_slice(sm_pack, w_a2_b[0], (32, DK + me * (DK // N_DEV)))
    sm_tot, _ = gather_sum_small(sm_pack)
    small["conv_w"] = sm_tot[0:CW, :]
    small["w_a2_f"] = sm_tot[32:32 + RANK, 0:DK]
    small["w_a2_b"] = sm_tot[32:32 + RANK, DK:2 * DK]

    wall = gather_wait(i_send, i_recv, i_thru, i_lands, sm_tot + mod_all[0:1, 0:1], "w_in")[0]
    w_int = jnp.pad(wall[:, 0:r_in, :].reshape(N_DEV * r_in, D), ((0, DINP - DIN), (0, 0)))
    after_w_in = (wall[0:1, 0:1, 0] * 0).astype(MXU_DTYPE)
    late = [w_out[0].astype(MXU_DTYPE) + after_w_in, tb(w_gate[0]) + after_w_in, tb(w_up[0]) + after_w_in,
            w_down[0].astype(MXU_DTYPE) + after_w_in]
    g_send, g_recv, late_thru, late_lands, g_token = gather_start(late, me, "late")

    def late_weights(after):
        got = gather_wait(g_send, g_recv, late_thru, late_lands, after, "late")
        return tuple(a.reshape(N_DEV * a.shape[1], D) for a in got)

    pad_in = lambda g: jnp.pad(g[0:DIN].reshape(N_DEV, r_in, D), ((0, 0), (0, r_in_b - r_in), (0, 0)))
    blocked = {"w_int": pad_in, "w_out": lambda g: g.reshape(N_DEV, r_out, D)}
    as_blocks = lambda n, g: blocked.get(n, lambda a: a.reshape(N_DEV, r_ff, D))(g)
    pending = []

    def grads_ready(names, gw_now):
        send, recv_s, thru, zones, token = scatter_start([as_blocks(n, gw_now[n][1]) for n in names], names[0])
        pending.append((names, send, recv_s, thru, zones))
        return token[0:1, 0:1]

    sm = {}

    def small_ready(gs, dmod, dmod_c):
        sm["tot"], gat = gather_sum_small(_pack_small(gs, dmod, dmod_c))
        sm["dm"] = jnp.concatenate(
            [gat[:, ROW_DMOD:ROW_DMOD + 6, :].reshape(N_DEV, 6 * D),
             jnp.pad(sm["tot"][ROW_DMODC:ROW_DMODC + 6, :].reshape(1, 6 * D), ((0, 7), (0, 0)))], axis=0)
        ncol = w_mod.shape[2]
        dm_sh = lax.dynamic_slice(sm["dm"], (0, me * ncol), (16, ncol))
        sm["mod"] = mod_backward(s_all, dm_sh, w_mod[0], m_w_mod[0], v_w_mod[0], c_ctx.reshape(1, D),
                                 m_c_ctx.reshape(1, D), v_c_ctx.reshape(1, D))
        return sm["mod"][4][0:1, 0:1] * 0

    gx, gw = local_step(x[0], ctx[0], loss_target[0], mod, mod_c, small, w_int, g_token[0:1, 0:1], late_weights,
                        grads_ready, small_ready, 512, 256, 8, 4)
    tot, dm = sm["tot"], sm["dm"]
    loss = jnp.sum(tot[ROW_LOSS])
    g_wmod, d_wmod, nm_wmod, nv_wmod, g_cc, d_cc, nm_cc, nv_cc = sm["mod"]

    recv = {}

    def wait_for(entry, after):
        names, send, recv_s, thru, zones = entry
        recv.update(dict(zip(names, scatter_wait(send, recv_s, thru, zones, after, names[0]))))

    for entry in pending[:-1]:
        wait_for(entry, tot)
    own = {n: lax.dynamic_index_in_dim(as_blocks(n, gw[n][0]), me, 0, keepdims=False) for n in gw}
    padt = lambda w: jnp.pad(w.T, ((0, r_in_b - r_in), (0, 0)))
    big = {}
    big["w_gate"] = [a.T for a in adamw_sharded(own["wg_t"], recv["wg_t"], w_gate[0].T, m_w_gate[0].T,
                                                 v_w_gate[0].T)]
    big["w_up"] = [a.T for a in adamw_sharded(own["wu_t"], recv["wu_t"], w_up[0].T, m_w_up[0].T, v_w_up[0].T)]
    big["w_down"] = adamw_sharded(own["w_down"], recv["w_down"], w_down[0], m_w_down[0], v_w_down[0])
    big["w_out"] = adamw_sharded(own["w_out"], recv["w_out"], w_out[0], m_w_out[0], v_w_out[0])
    wait_for(pending[-1], big["w_out"][0])
    big["w_in"] = [a[0:r_in].T for a in adamw_sharded(own["w_int"], recv["w_int"], padt(w_in[0]), padt(m_w_in[0]),
                                                       padt(v_w_in[0]))]
    big["w_mod"] = [g_wmod, d_wmod, nm_wmod, nv_wmod]

    row = lambda r, w: tot[r:r + 1, 0:w]
    gn_row = tot[ROW_GN:ROW_GN + 1, 0:DC]
    g_small = {
        "b_mod": jnp.sum(dm, axis=0, keepdims=True),
        "norm1_g": row(ROW_N1, D), "norm2_g": row(ROW_N2, D),
        "conv_w": lax.dynamic_slice(tot, (ROW_CW, me * (DC // N_DEV)), (CW, DC // N_DEV)),
        "conv_b": row(ROW_CB, DC), "conv_ln_g": row(ROW_LG, DC), "conv_ln_b": row(ROW_LB, DC),
        "w_a2_f": lax.dynamic_slice(tot, (ROW_WA, me * (DK // N_DEV)), (RANK, DK // N_DEV)),
        "b_a_f": tot[ROW_BA:ROW_BA + 1, 0:DK],
        "w_a2_b": lax.dynamic_slice(tot, (ROW_WA + RANK, DK + me * (DK // N_DEV)), (RANK, DK // N_DEV)),
        "b_a_b": tot[ROW_BA:ROW_BA + 1, DK:2 * DK],
        "gla_norm_g": gn_row[:, 0:HV] + gn_row[:, HV:2 * HV] + gn_row[:, 2 * HV:3 * HV] + gn_row[:, 3 * HV:4 * HV],
        "final_g": row(ROW_FG, D),
    }
    wmv = {
        "b_mod": (b_mod, m_b_mod, v_b_mod), "norm1_g": (norm1_g, m_norm1_g, v_norm1_g),
        "norm2_g": (norm2_g, m_norm2_g, v_norm2_g), "conv_w": (conv_w[0], m_conv_w[0], v_conv_w[0]),
        "conv_b": (conv_b, m_conv_b, v_conv_b), "conv_ln_g": (conv_ln_g, m_conv_ln_g, v_conv_ln_g),
        "conv_ln_b": (conv_ln_b, m_conv_ln_b, v_conv_ln_b), "w_a2_f": (w_a2_f[0], m_w_a2_f[0], v_w_a2_f[0]),
        "b_a_f": (b_a_f, m_b_a_f, v_b_a_f), "w_a2_b": (w_a2_b[0], m_w_a2_b[0], v_w_a2_b[0]),
        "b_a_b": (b_a_b, m_b_a_b, v_b_a_b), "gla_norm_g": (gla_norm_g, m_gla_norm_g, v_gla_norm_g),
        "final_g": (final_g.reshape(1, D), m_final_g.reshape(1, D), v_final_g.reshape(1, D)),
    }
    names_small = list(g_small)
    upd = adamw_small([(g_small[n],) + wmv[n] for n in names_small])
    res = {n: (g_small[n],) + upd[i] for i, n in enumerate(names_small)}
    res["c_ctx"] = (g_cc, d_cc, nm_cc, nv_cc)
    for n in ("w_mod", "w_in", "w_out", "w_gate", "w_up", "w_down"):
        res[n] = tuple(big[n])

    order = ["c_ctx", "w_mod", "b_mod", "norm1_g", "norm2_g", "w_in", "conv_w", "conv_b", "conv_ln_g", "conv_ln_b",
             "w_a2_f", "b_a_f", "w_a2_b", "b_a_b", "gla_norm_g", "w_out", "w_gate", "w_up", "w_down", "final_g"]
    shapes = {"c_ctx": c_ctx.shape, "w_mod": w_mod.shape, "b_mod": b_mod.shape, "norm1_g": norm1_g.shape,
              "norm2_g": norm2_g.shape, "w_in": w_in.shape, "conv_w": conv_w.shape, "conv_b": conv_b.shape,
              "conv_ln_g": conv_ln_g.shape, "conv_ln_b": conv_ln_b.shape, "w_a2_f": w_a2_f.shape,
              "b_a_f": b_a_f.shape, "w_a2_b": w_a2_b.shape, "b_a_b": b_a_b.shape, "gla_norm_g": gla_norm_g.shape,
              "w_out": w_out.shape, "w_gate": w_gate.shape, "w_up": w_up.shape, "w_down": w_down.shape,
              "final_g": final_g.shape}
    outs = [loss, gx.reshape(x.shape)]
    for i in range(4):
        outs += [res[n][i].reshape(shapes[n]) for n in order]
    return tuple(outs)
```

```python
import functools

import jax
import jax.numpy as jnp
from jax import lax
from jax.experimental import pallas as pl
from jax.experimental.pallas import tpu as pltpu

F32 = jnp.float32
MXU_DTYPE = jnp.bfloat16
WIRE_DTYPE = jnp.bfloat16
HI = lax.Precision.HIGHEST
MESH = pl.DeviceIdType.MESH

N_DEV = 8
D = 1024
DC = 512
NH = 4
HK = 64
HV = 128
DK = NH * HK
DV = NH * HV
RANK = 16
CHUNK = 64
SEG = 64
CW = 31
CPAD = 15
DFF = 2816
DIN = 2592
DINP = 2688
TAU = 16.0
EPS = 1e-6
VMEM_LIMIT = 56 * 1024 * 1024

ADAM_LR = 0.001
ADAM_B1 = 0.9
ADAM_B2 = 0.999
ADAM_EPS = 1e-08
ADAM_WD = 0.01
ADAM_STEP = 10


def _mm(a, b):
    return jnp.dot(a.astype(MXU_DTYPE), b.astype(MXU_DTYPE), preferred_element_type=F32)


def _mm_nt(a, b):
    return lax.dot_general(a.astype(MXU_DTYPE), b.astype(MXU_DTYPE), (((1,), (1,)), ((), ())),
                           preferred_element_type=F32)


def _mm_tn(a, b):
    return lax.dot_general(a.astype(MXU_DTYPE), b.astype(MXU_DTYPE), (((0,), (0,)), ((), ())),
                           preferred_element_type=F32)


def _hi(a, b):
    return jnp.dot(a, b, precision=HI, preferred_element_type=F32)


def _hi_nt(a, b):
    return lax.dot_general(a, b, (((1,), (1,)), ((), ())), precision=HI, preferred_element_type=F32)


def _hi_tn(a, b):
    return lax.dot_general(a, b, (((0,), (0,)), ((), ())), precision=HI, preferred_element_type=F32)


def _sigmoid(x):
    return 1.0 / (1.0 + jnp.exp(-x))


def _cparams(n_axes):
    return pltpu.CompilerParams(dimension_semantics=("arbitrary",) * n_axes, vmem_limit_bytes=VMEM_LIMIT)


def _full(shape):
    n = len(shape)
    return pl.BlockSpec(shape, lambda *_: (0,) * n)


def _rows(tt, width):
    return pl.BlockSpec((tt, width), lambda i: (i, 0))


def _sds(shape, dtype=F32):
    return jax.ShapeDtypeStruct(shape, dtype)


def _norm_mod(x, g, sh, sc):
    r = lax.rsqrt(jnp.mean(x * x, axis=-1, keepdims=True) + EPS)
    xn = x * r
    yy = xn * g
    return r, xn, yy, yy * (1.0 + sc) + sh


def _norm_mod_bwd(dh, r, xn, yy, g, sc):
    dsh = jnp.sum(dh, axis=0, keepdims=True)
    dsc = jnp.sum(dh * yy, axis=0, keepdims=True)
    dy = dh * (1.0 + sc)
    dg = jnp.sum(dy * xn, axis=0, keepdims=True)
    dxn = dy * g
    dx = r * (dxn - xn * jnp.mean(dxn * xn, axis=-1, keepdims=True))
    return dsh, dsc, dg, dx


def _zero_first(*refs):
    @pl.when(pl.program_id(0) == 0)
    def _():
        for r in refs:
            r[...] = jnp.zeros_like(r)


def _acc_rows(ref, rows):
    ref[...] += jnp.concatenate(rows + [jnp.zeros((8 - len(rows), rows[0].shape[1]), F32)], axis=0)


def proj_fwd(x, vec, w_int, wa, ba, tt):
    t = x.shape[0]

    def body(x_ref, vec_ref, w_ref, wa_ref, ba_ref, u_ref, q_ref, k_ref, v_ref, g_ref, r_ref, la_ref, h_ref):
        _, _, _, h = _norm_mod(x_ref[...], vec_ref[0:1, :], vec_ref[1:2, :], vec_ref[2:3, :])
        hb = h.astype(MXU_DTYPE)
        h_ref[...] = hb
        p = _mm_nt(hb, w_ref[...])
        u_ref[...] = p[:, 0:1024]
        q_ref[...] = p[:, 1024:1280]
        k_ref[...] = p[:, 1280:1536]
        v_ref[...] = p[:, 1536:2048]
        g_ref[...] = p[:, 2048:2560]
        rr = p[:, 2560:2688]
        r_ref[...] = rr
        z = _mm(rr, wa_ref[...]) + ba_ref[...]
        la_ref[...] = (jnp.minimum(z, 0.0) - jnp.log(1.0 + jnp.exp(-jnp.abs(z)))) * (1.0 / TAU)

    return pl.pallas_call(
        body, name="proj_fwd", grid=(t // tt,),
        in_specs=[_rows(tt, D), _full((8, D)), _full((DINP, D)), _full((128, 512)), _full((1, 512))],
        out_specs=[_rows(tt, 1024), _rows(tt, DK), _rows(tt, DK), _rows(tt, DV), _rows(tt, DV), _rows(tt, 128),
                   _rows(tt, 512), _rows(tt, D)],
        out_shape=[_sds((t, 1024)), _sds((t, DK)), _sds((t, DK)), _sds((t, DV)), _sds((t, DV)), _sds((t, 128)),
                   _sds((t, 512)), _sds((t, D), MXU_DTYPE)],
        compiler_params=_cparams(1),
    )(x, vec, w_int, wa, ba)


def proj_bwd(du, dqkv, dg, dla_f, dla_b, la, r, x, dx1, vec, w_int, wa, tt):
    t = x.shape[0]

    def body(du_ref, dqf_ref, dqb_ref, dkf_ref, dkb_ref, dvf_ref, dvb_ref, dg_ref, dlaf_ref, dlab_ref, la_ref, r_ref,
             x_ref, dx1_ref, vec_ref, w_ref, wa_ref, gx_ref, dp_ref, acc_ref, dba_ref, dwa_ref):
        _zero_first(acc_ref, dba_ref, dwa_ref)
        dla = jnp.concatenate([dlaf_ref[...], dlab_ref[...]], axis=1)
        dz = dla * (1.0 - jnp.exp(TAU * la_ref[...])) * (1.0 / TAU)
        rr = r_ref[...]
        _acc_rows(dba_ref, [jnp.sum(dz, axis=0, keepdims=True)])
        dwa_ref[...] += _mm_tn(rr, dz)
        dr = _mm_nt(dz, wa_ref[...])
        md = lambda a: a.astype(MXU_DTYPE)
        both = lambda a_ref, b_ref: md(a_ref[...].astype(F32) + b_ref[...].astype(F32))
        dp = jnp.concatenate([du_ref[...], both(dqf_ref, dqb_ref), both(dkf_ref, dkb_ref), both(dvf_ref, dvb_ref),
                              dg_ref[...], md(dr)], axis=1)
        dp_ref[...] = dp
        dh = _mm(dp, w_ref[...])
        g, sc = vec_ref[0:1, :], vec_ref[2:3, :]
        rn, xn, yy, _ = _norm_mod(x_ref[...], g, vec_ref[1:2, :], sc)
        dsh, dsc, dgn, dx = _norm_mod_bwd(dh, rn, xn, yy, g, sc)
        gx_ref[...] = dx1_ref[...] + dx
        _acc_rows(acc_ref, [dsh, dsc, dgn])

    return pl.pallas_call(
        body, name="proj_bwd", grid=(t // tt,),
        in_specs=[_rows(tt, 1024), _rows(tt, DK), _rows(tt, DK), _rows(tt, DK), _rows(tt, DK), _rows(tt, DV),
                  _rows(tt, DV), _rows(tt, DV), _rows(tt, DK), _rows(tt, DK), _rows(tt, 512),
                  _rows(tt, 128), _rows(tt, D), _rows(tt, D), _full((8, D)), _full((DINP, D)), _full((128, 512))],
        out_specs=[_rows(tt, D), _rows(tt, DINP), _full((8, D)), _full((8, 512)), _full((128, 512))],
        out_shape=[_sds((t, D)), _sds((t, DINP), MXU_DTYPE), _sds((8, D)), _sds((8, 512)), _sds((128, 512))],
        compiler_params=_cparams(1),
    )(du, *dqkv, dg, dla_f, dla_b, la, r, x, dx1, vec, w_int, wa)


def _dot_exact01(m01, x):
    bf = jnp.bfloat16
    w = x.shape[1]
    hi = x.astype(bf)
    r1 = x - hi.astype(F32)
    mid = r1.astype(bf)
    lo = (r1 - mid.astype(F32)).astype(bf)
    y = jnp.dot(m01.astype(bf), jnp.concatenate([hi, mid, lo], axis=1), preferred_element_type=F32)
    return y[:, 0:w] + y[:, w:2 * w] + y[:, 2 * w:3 * w]


def _gla_chunk(d, qc, kc, la_c):
    row = lax.broadcasted_iota(jnp.int32, (CHUNK, CHUNK), 0)
    col = lax.broadcasted_iota(jnp.int32, (CHUNK, CHUNK), 1)
    cum = ((col <= row) if d == 0 else (col >= row)).astype(F32)
    cum_t = ((col >= row) if d == 0 else (col <= row)).astype(F32)
    cum4 = jnp.concatenate([cum] * NH, axis=0)
    head_of_lane = lax.broadcasted_iota(jnp.int32, (1, DK), 1) // HK
    b = _dot_exact01(cum, la_c)
    bl = jnp.sum(la_c, axis=0, keepdims=True)
    eb = jnp.exp(b)
    enb = jnp.exp(-b)
    ekd = jnp.exp(bl - b)
    qt = qc * (HK ** -0.5) * eb
    kt = kc * enb
    kd = kc * ekd
    qst = jnp.concatenate([jnp.where(head_of_lane == h, qt, 0.0) for h in range(NH)], axis=0)
    a = _mm_nt(qst, kt) * cum4
    return cum_t, cum4, head_of_lane, eb, enb, ekd, qt, kt, kd, qst, a, jnp.exp(bl)


def gla_fwd(q, k, v, la, s0, cb):
    t = q.shape[0]
    nc = t // CHUNK
    nb = nc // cb

    def body(qf_ref, kf_ref, vf_ref, laf_ref, qb_ref, kb_ref, vb_ref, lab_ref, s0_ref,
             of_ref, ob_ref, sf_ref, sb_ref, sfin_ref, s_scr):
        i = pl.program_id(0)

        @pl.when(i == 0)
        def _():
            s_scr[...] = s0_ref[...]

        def chunk(d, jj, q_ref, k_ref, v_ref, la_ref, o_ref, sall_ref):
            rows = slice(jj * CHUNK, (jj + 1) * CHUNK)
            vc = v_ref[rows, :]
            _, _, head_of_lane, _, _, _, _, _, kd, qst, a, dec = _gla_chunk(
                d, q_ref[rows, :], k_ref[rows, :], la_ref[rows, :])
            s = s_scr[d]
            sall_ref[jj] = s
            inter = _mm_nt(qst, s)
            outs = []
            for h in range(NH):
                hs = slice(h * CHUNK, (h + 1) * CHUNK)
                outs.append(_mm(a[hs], vc[:, h * HV:(h + 1) * HV]) + inter[hs])
            o_ref[rows, :] = jnp.concatenate(outs, axis=1)
            kv = _mm_tn(vc, kd)
            s_new = dec * s
            for h in range(NH):
                s_new = s_new + jnp.where(head_of_lane == h, kv[h * HV:(h + 1) * HV], 0.0)
            s_scr[d] = s_new

        for j in range(cb):
            chunk(0, j, qf_ref, kf_ref, vf_ref, laf_ref, of_ref, sf_ref)
            chunk(1, cb - 1 - j, qb_ref, kb_ref, vb_ref, lab_ref, ob_ref, sb_ref)

        @pl.when(i == nb - 1)
        def _():
            sfin_ref[...] = s_scr[...]

    tb = cb * CHUNK
    fwd = lambda w, c=0: pl.BlockSpec((tb, w), lambda i: (i, c))
    bwd = lambda w, c=0: pl.BlockSpec((tb, w), lambda i: (nb - 1 - i, c))
    return pl.pallas_call(
        body, name="gla_fwd", grid=(nb,),
        in_specs=[fwd(DK), fwd(DK), fwd(DV), fwd(DK, 0), bwd(DK), bwd(DK), bwd(DV), bwd(DK, 1), _full((2, HV, DK))],
        out_specs=[fwd(DV), bwd(DV), pl.BlockSpec((cb, HV, DK), lambda i: (i, 0, 0)),
                   pl.BlockSpec((cb, HV, DK), lambda i: (nb - 1 - i, 0, 0)), _full((2, HV, DK))],
        out_shape=[_sds((t, DV)), _sds((t, DV)), _sds((nc, HV, DK)), _sds((nc, HV, DK)), _sds((2, HV, DK))],
        scratch_shapes=[pltpu.VMEM((2, HV, DK), F32)],
        compiler_params=_cparams(1),
    )(q, k, v, la, q, k, v, la, s0)


def gla_bwd(q, k, v, la, do, sall_f, sall_b, dsfin, cb):
    t = q.shape[0]
    nc = t // CHUNK
    nb = nc // cb

    def body(qf_ref, kf_ref, vf_ref, laf_ref, dof_ref, sf_ref, qb_ref, kb_ref, vb_ref, lab_ref, dob_ref, sb_ref,
             dsfin_ref, dqf_ref, dkf_ref, dvf_ref, dlaf_ref, dqb_ref, dkb_ref, dvb_ref, dlab_ref, ds0_ref, ds_scr):
        i = pl.program_id(0)

        @pl.when(i == 0)
        def _():
            ds_scr[...] = dsfin_ref[...]

        def chunk(d, jj, q_ref, k_ref, v_ref, la_ref, do_ref, sall_ref, dq_ref, dk_ref, dv_ref, dla_ref):
            rows = slice(jj * CHUNK, (jj + 1) * CHUNK)
            vc = v_ref[rows, :]
            doc = do_ref[rows, :]
            cum_t, cum4, head_of_lane, eb, enb, ekd, qt, kt, kd, qst, a, dec = _gla_chunk(
                d, q_ref[rows, :], k_ref[rows, :], la_ref[rows, :])
            s = sall_ref[jj]
            ds = ds_scr[d]
            hv = lambda x, h: x[:, h * HV:(h + 1) * HV]
            hr = lambda x, h: x[h * CHUNK:(h + 1) * CHUNK]
            fold = lambda x: functools.reduce(
                lambda p, c: p + c, [jnp.where(head_of_lane == h, hr(x, h), 0.0) for h in range(NH)])
            dost = jnp.concatenate([hv(doc, h) for h in range(NH)], axis=0)
            vst = jnp.concatenate([hv(vc, h) for h in range(NH)], axis=0)
            da = jnp.concatenate([_mm_nt(hv(doc, h), hv(vc, h)) for h in range(NH)], axis=0) * cum4
            dqt = fold(_mm(da, kt) + _mm(dost, s))
            dkt = _mm_tn(da, qst)
            kdst = jnp.concatenate([jnp.where(head_of_lane == h, kd, 0.0) for h in range(NH)], axis=0)
            dv_inter = _mm_nt(kdst, ds)
            dv_ref[rows, :] = jnp.concatenate(
                [_mm_tn(hr(a, h), hv(doc, h)) + hr(dv_inter, h) for h in range(NH)], axis=1).astype(MXU_DTYPE)
            dkd = fold(_mm(vst, ds))
            ds_scr[d] = dec * ds + _mm_tn(dost, qst)
            tkd = dkd * kd
            db = dqt * qt - dkt * kt - tkd
            dbl = jnp.sum(ds * s, axis=0, keepdims=True) * dec + jnp.sum(tkd, axis=0, keepdims=True)
            dla_ref[rows, :] = _dot_exact01(cum_t, db) + dbl
            dq_ref[rows, :] = (dqt * eb * (HK ** -0.5)).astype(MXU_DTYPE)
            dk_ref[rows, :] = (dkt * enb + dkd * ekd).astype(MXU_DTYPE)

        for j in range(cb):
            chunk(0, cb - 1 - j, qf_ref, kf_ref, vf_ref, laf_ref, dof_ref, sf_ref, dqf_ref, dkf_ref, dvf_ref, dlaf_ref)
            chunk(1, j, qb_ref, kb_ref, vb_ref, lab_ref, dob_ref, sb_ref, dqb_ref, dkb_ref, dvb_ref, dlab_ref)

        @pl.when(i == nb - 1)
        def _():
            ds0_ref[...] = ds_scr[...]

    tb = cb * CHUNK
    rev = lambda w, c=0: pl.BlockSpec((tb, w), lambda i: (nb - 1 - i, c))
    fro = lambda w, c=0: pl.BlockSpec((tb, w), lambda i: (i, c))
    st_rev = pl.BlockSpec((cb, HV, DK), lambda i: (nb - 1 - i, 0, 0))
    st_fro = pl.BlockSpec((cb, HV, DK), lambda i: (i, 0, 0))
    md = MXU_DTYPE
    return pl.pallas_call(
        body, name="gla_bwd", grid=(nb,),
        in_specs=[rev(DK), rev(DK), rev(DV), rev(DK, 0), rev(DV), st_rev,
                  fro(DK), fro(DK), fro(DV), fro(DK, 1), fro(DV), st_fro, _full((2, HV, DK))],
        out_specs=[rev(DK), rev(DK), rev(DV), rev(DK), fro(DK), fro(DK), fro(DV), fro(DK), _full((2, HV, DK))],
        out_shape=[_sds((t, DK), md), _sds((t, DK), md), _sds((t, DV), md), _sds((t, DK)),
                   _sds((t, DK), md), _sds((t, DK), md), _sds((t, DV), md), _sds((t, DK)), _sds((2, HV, DK))],
        scratch_shapes=[pltpu.VMEM((2, HV, DK), F32)],
        compiler_params=_cparams(1),
    )(q, k, v, la, do, sall_f, q, k, v, la, do, sall_b, dsfin)


def _seg_pos(tt):
    return lax.broadcasted_iota(jnp.int32, (tt, 1), 0) % SEG


def _shifted(x, s, pos, tt):
    y = x if s == 0 else pltpu.roll(x, (-s) % tt, 0)
    return jnp.where((pos + s >= 0) & (pos + s < SEG), y, 0.0)


def _head_norm(o, gn):
    rs, xs = [], []
    for h in range(NH):
        oh = o[:, h * HV:(h + 1) * HV]
        r = lax.rsqrt(jnp.mean(oh * oh, axis=-1, keepdims=True) + EPS)
        rs.append(r)
        xs.append(oh * r)
    return rs, xs


def merge_fwd(u, g, o_f, o_b, x, vec, vc, convw, w_out, tt):
    t = x.shape[0]

    def body(u_ref, g_ref, of_ref, ob_ref, x_ref, vec_ref, vc_ref, cw_ref, w_ref, x1_ref, cat_ref, mix_ref, yc_ref,
             h2_ref):
        a = u_ref[:, 0:DC]
        gate = u_ref[:, DC:2 * DC]
        vv = a * _sigmoid(gate)
        pos = _seg_pos(tt)
        cw = cw_ref[...]
        yc = jnp.zeros((tt, DC), F32) + vc_ref[1:2, :]
        for j in range(CW):
            yc = yc + _shifted(vv, j - CPAD, pos, tt) * cw[j:j + 1, :]
        yc_ref[...] = yc
        mu = jnp.mean(yc, axis=-1, keepdims=True)
        yd = yc - mu
        rs = lax.rsqrt(jnp.mean(yd * yd, axis=-1, keepdims=True) + EPS)
        ln = yd * rs * vc_ref[2:3, :] + vc_ref[3:4, :]
        conv_o = ln * _sigmoid(ln)
        o = of_ref[...] + ob_ref[...]
        _, xs = _head_norm(o, None)
        gg = g_ref[...]
        o2g = jnp.concatenate(xs, axis=1) * vc_ref[0:1, :] * (gg * _sigmoid(gg))
        cat = jnp.concatenate([conv_o, o2g], axis=1).astype(MXU_DTYPE)
        cat_ref[...] = cat
        mix = _mm(cat, w_ref[...])
        mix_ref[...] = mix
        x1 = x_ref[...] + vec_ref[0:1, :] * mix
        x1_ref[...] = x1
        _, _, _, h2 = _norm_mod(x1, vec_ref[1:2, :], vec_ref[2:3, :], vec_ref[3:4, :])
        h2_ref[...] = h2.astype(MXU_DTYPE)

    return pl.pallas_call(
        body, name="merge_fwd", grid=(t // tt,),
        in_specs=[_rows(tt, 1024), _rows(tt, DV), _rows(tt, DV), _rows(tt, DV), _rows(tt, D),
                  _full((8, D)), _full((8, DC)), _full((32, DC)), _full((D, D))],
        out_specs=[_rows(tt, D), _rows(tt, D), _rows(tt, D), _rows(tt, DC), _rows(tt, D)],
        out_shape=[_sds((t, D)), _sds((t, D), MXU_DTYPE), _sds((t, D)), _sds((t, DC)), _sds((t, D), MXU_DTYPE)],
        compiler_params=_cparams(1),
    )(u, g, o_f, o_b, x, vec, vc, convw, w_out)


def merge_bwd(dx1, mix, u, g, o_f, o_b, yc, vec, vc, convw, w_out, tt):
    t = dx1.shape[0]

    def body(dx1_ref, mix_ref, u_ref, g_ref, of_ref, ob_ref, yc_ref, vec_ref, vc_ref, cw_ref, w_ref,
             du_ref, dg_ref, do_ref, dmix_ref, acc1_ref, acc2_ref, dcw_ref):
        _zero_first(acc1_ref, acc2_ref, dcw_ref)
        dx1v = dx1_ref[...]
        dg1 = jnp.sum(dx1v * mix_ref[...], axis=0, keepdims=True)
        dmix = (vec_ref[0:1, :] * dx1v).astype(MXU_DTYPE)
        dmix_ref[...] = dmix
        dcat = _mm_nt(dmix, w_ref[...])
        dconv_o = dcat[:, 0:DC]
        do2 = dcat[:, DC:2 * DC]
        gn = vc_ref[0:1, :]
        o = of_ref[...] + ob_ref[...]
        rs, xs = _head_norm(o, None)
        xn = jnp.concatenate(xs, axis=1)
        gg = g_ref[...]
        sg = _sigmoid(gg)
        don = do2 * (gg * sg)
        dg_ref[...] = (do2 * (xn * gn) * (sg * (1.0 + gg * (1.0 - sg)))).astype(MXU_DTYPE)
        dgn = jnp.sum(don * xn, axis=0, keepdims=True)
        dxn = don * gn
        dos = []
        for h in range(NH):
            dh = dxn[:, h * HV:(h + 1) * HV]
            dos.append(rs[h] * (dh - xs[h] * jnp.mean(dh * xs[h], axis=-1, keepdims=True)))
        do_ref[...] = jnp.concatenate(dos, axis=1).astype(MXU_DTYPE)
        yc = yc_ref[...]
        mu = jnp.mean(yc, axis=-1, keepdims=True)
        yd = yc - mu
        rstd = lax.rsqrt(jnp.mean(yd * yd, axis=-1, keepdims=True) + EPS)
        yhat = yd * rstd
        lg = vc_ref[2:3, :]
        ln = yhat * lg + vc_ref[3:4, :]
        sl = _sigmoid(ln)
        dln = dconv_o * (sl * (1.0 + ln * (1.0 - sl)))
        dlb = jnp.sum(dln, axis=0, keepdims=True)
        dlg = jnp.sum(dln * yhat, axis=0, keepdims=True)
        dyh = dln * lg
        dyc = rstd * (dyh - jnp.mean(dyh, axis=-1, keepdims=True)
                      - yhat * jnp.mean(dyh * yhat, axis=-1, keepdims=True))
        dcb = jnp.sum(dyc, axis=0, keepdims=True)
        a = u_ref[:, 0:DC]
        gate = u_ref[:, DC:2 * DC]
        sgt = _sigmoid(gate)
        vv = a * sgt
        pos = _seg_pos(tt)
        cw = cw_ref[...]
        dvv = jnp.zeros((tt, DC), F32)
        dws = []
        for j in range(CW):
            shifted_dyc = _shifted(dyc, CPAD - j, pos, tt)
            dvv = dvv + shifted_dyc * cw[j:j + 1, :]
            dws.append(jnp.sum(shifted_dyc * vv, axis=0, keepdims=True))
        dws.append(jnp.zeros((1, DC), F32))
        du_ref[:, 0:DC] = (dvv * sgt).astype(MXU_DTYPE)
        du_ref[:, DC:2 * DC] = (dvv * a * sgt * (1.0 - sgt)).astype(MXU_DTYPE)
        _acc_rows(acc1_ref, [dg1])
        _acc_rows(acc2_ref, [dgn, dcb, dlg, dlb])
        dcw_ref[...] += jnp.concatenate(dws, axis=0)

    return pl.pallas_call(
        body, name="merge_bwd", grid=(t // tt,),
        in_specs=[_rows(tt, D), _rows(tt, D), _rows(tt, 1024), _rows(tt, DV), _rows(tt, DV), _rows(tt, DV),
                  _rows(tt, DC),
                  _full((8, D)), _full((8, DC)), _full((32, DC)), _full((D, D))],
        out_specs=[_rows(tt, 1024), _rows(tt, DV), _rows(tt, DV), _rows(tt, D), _full((8, D)), _full((8, DC)),
                   _full((32, DC))],
        out_shape=[_sds((t, 1024), MXU_DTYPE), _sds((t, DV), MXU_DTYPE), _sds((t, DV), MXU_DTYPE),
                   _sds((t, D), MXU_DTYPE), _sds((8, D)),
                   _sds((8, DC)), _sds((32, DC))],
        compiler_params=_cparams(1),
    )(dx1, mix, u, g, o_f, o_b, yc, vec, vc, convw, w_out)


FN = DFF // 2


def ffn_gate_up(h2, wg_t, wu_t, tt):
    t = h2.shape[0]

    def body(h2_ref, wg_ref, wu_ref, s_ref, d_ref, hid_ref):
        h2v = h2_ref[...]
        gt = _mm_nt(h2v, wg_ref[...])
        up = _mm_nt(h2v, wu_ref[...])
        sg = _sigmoid(gt)
        act = gt * sg
        s_ref[...] = act.astype(MXU_DTYPE)
        d_ref[...] = (up * (sg * (1.0 + gt * (1.0 - sg)))).astype(MXU_DTYPE)
        hid_ref[...] = (act * up).astype(MXU_DTYPE)

    blk = pl.BlockSpec((tt, FN), lambda j, i: (i, j))
    wblk = pl.BlockSpec((FN, D), lambda j, i: (j, 0))
    return pl.pallas_call(
        body, name="ffn_gate_up", grid=(2, t // tt),
        in_specs=[pl.BlockSpec((tt, D), lambda j, i: (i, 0)), wblk, wblk],
        out_specs=[blk, blk, blk],
        out_shape=[_sds((t, DFF), MXU_DTYPE)] * 3,
        compiler_params=_cparams(2),
    )(h2, wg_t, wu_t)


def ffn_down_loss(hid, x1, tgt, vec, w_down, tt):
    t = x1.shape[0]

    def body(hid_ref, x1_ref, tgt_ref, vec_ref, w_ref, dx2_ref, dff_ref, acc_ref):
        _zero_first(acc_ref)
        g2 = vec_ref[0:1, :]
        fg = vec_ref[1:2, :]
        ff = _mm(hid_ref[...], w_ref[...])
        x2 = x1_ref[...] + g2 * ff
        rf = lax.rsqrt(jnp.mean(x2 * x2, axis=-1, keepdims=True) + EPS)
        xn = x2 * rf
        err = xn * fg - tgt_ref[...]
        dy = err * (1.0 / D)
        dfg = jnp.sum(dy * xn, axis=0, keepdims=True)
        dxn = dy * fg
        dx2 = rf * (dxn - xn * jnp.mean(dxn * xn, axis=-1, keepdims=True))
        dx2_ref[...] = dx2
        dff_ref[...] = (g2 * dx2).astype(MXU_DTYPE)
        dg2 = jnp.sum(dx2 * ff, axis=0, keepdims=True)
        loss = jnp.sum(err * err, axis=0, keepdims=True) * (0.5 / D)
        _acc_rows(acc_ref, [dg2, dfg, loss])

    return pl.pallas_call(
        body, name="ffn_down_loss", grid=(t // tt,),
        in_specs=[_rows(tt, DFF), _rows(tt, D), _rows(tt, D), _full((8, D)), _full((DFF, D))],
        out_specs=[_rows(tt, D), _rows(tt, D), _full((8, D))],
        out_shape=[_sds((t, D)), _sds((t, D), MXU_DTYPE), _sds((8, D))],
        compiler_params=_cparams(1),
    )(hid, x1, tgt, vec, w_down)


def ffn_dhid(dff, s, d, w_down, tt):
    t = dff.shape[0]

    def body(dff_ref, s_ref, d_ref, w_ref, dgt_ref, dup_ref):
        dhid = _mm_nt(dff_ref[...], w_ref[...])
        dgt_ref[...] = (dhid * d_ref[...].astype(F32)).astype(MXU_DTYPE)
        dup_ref[...] = (dhid * s_ref[...].astype(F32)).astype(MXU_DTYPE)

    blk = pl.BlockSpec((tt, FN), lambda j, i: (i, j))
    return pl.pallas_call(
        body, name="ffn_dhid", grid=(2, t // tt),
        in_specs=[pl.BlockSpec((tt, D), lambda j, i: (i, 0)), blk, blk, pl.BlockSpec((FN, D), lambda j, i: (j, 0))],
        out_specs=[blk, blk],
        out_shape=[_sds((t, DFF), MXU_DTYPE), _sds((t, DFF), MXU_DTYPE)],
        compiler_params=_cparams(2),
    )(dff, s, d, w_down)


def ffn_dh2(dgt, dup, x1, dx2, vec, wg_t, wu_t, tt):
    t = x1.shape[0]

    def body(dgt_ref, dup_ref, x1_ref, dx2_ref, vec_ref, wg_ref, wu_ref, dx1_ref, acc_ref):
        _zero_first(acc_ref)
        dh2 = _mm(dgt_ref[...], wg_ref[...]) + _mm(dup_ref[...], wu_ref[...])
        g, sc = vec_ref[0:1, :], vec_ref[2:3, :]
        r, xn, yy, _ = _norm_mod(x1_ref[...], g, vec_ref[1:2, :], sc)
        dsh, dsc, dgn, dx = _norm_mod_bwd(dh2, r, xn, yy, g, sc)
        dx1_ref[...] = dx2_ref[...] + dx
        _acc_rows(acc_ref, [dsh, dsc, dgn])

    return pl.pallas_call(
        body, name="ffn_dh2", grid=(t // tt,),
        in_specs=[_rows(tt, DFF), _rows(tt, DFF), _rows(tt, D), _rows(tt, D), _full((8, D)), _full((DFF, D)),
                  _full((DFF, D))],
        out_specs=[_rows(tt, D), _full((8, D))],
        out_shape=[_sds((t, D)), _sds((8, D))],
        compiler_params=_cparams(1),
    )(dgt, dup, x1, dx2, vec, wg_t, wu_t)


def tn_matmul(a, b, bm, bt, init=None):
    t, m = a.shape
    n = b.shape[1]
    nk = t // bt

    def body(*refs):
        if init is None:
            a_ref, b_ref, o_ref, wire_ref = refs
        else:
            a_ref, b_ref, i_ref, o_ref, wire_ref = refs
        @pl.when(pl.program_id(1) == 0)
        def _():
            o_ref[...] = jnp.zeros_like(o_ref) if init is None else i_ref[...]

        o_ref[...] += _mm_tn(a_ref[...], b_ref[...])

        @pl.when(pl.program_id(1) == nk - 1)
        def _():
            wire_ref[...] = o_ref[...].astype(WIRE_DTYPE)

    in_specs = [pl.BlockSpec((bt, bm), lambda i, k: (k, i)), pl.BlockSpec((bt, n), lambda i, k: (k, 0))]
    args = [a, b]
    if init is not None:
        in_specs.append(pl.BlockSpec((bm, n), lambda i, k: (i, 0)))
        args.append(init)
    oblk = pl.BlockSpec((bm, n), lambda i, k: (i, 0))
    return pl.pallas_call(
        body, name="tn_matmul", grid=(m // bm, nk),
        in_specs=in_specs, out_specs=[oblk, oblk],
        out_shape=[_sds((m, n)), _sds((m, n), WIRE_DTYPE)], compiler_params=_cparams(2),
    )(*args)


def _adamw(w, g, m, v):
    m = ADAM_B1 * m + (1.0 - ADAM_B1) * g
    v = ADAM_B2 * v + (1.0 - ADAM_B2) * (g * g)
    m_hat = m / (1.0 - ADAM_B1 ** ADAM_STEP)
    v_hat = v / (1.0 - ADAM_B2 ** ADAM_STEP)
    delta = -ADAM_LR * (m_hat / (jnp.sqrt(v_hat) + ADAM_EPS) + ADAM_WD * w)
    return delta, m, v


def adamw_sharded(own, recv, w, m, v):
    shape = w.shape

    def body(own_ref, recv_ref, w_ref, m_ref, v_ref, g_ref, d_ref, mo_ref, vo_ref):
        g = own_ref[...]
        for k in range(N_DEV - 1):
            g = g + recv_ref[k].astype(F32)
        g_ref[...] = g
        d_ref[...], mo_ref[...], vo_ref[...] = _adamw(w_ref[...], g, m_ref[...], v_ref[...])

    return pl.pallas_call(
        body, name="adamw_sharded",
        in_specs=[_full(shape), _full((N_DEV - 1,) + shape), _full(shape), _full(shape), _full(shape)],
        out_specs=[_full(shape)] * 4, out_shape=[_sds(shape)] * 4, grid=(1,),
        compiler_params=_cparams(1),
    )(own, recv, w, m, v)


def adamw_small(items):
    n = len(items)
    flat = [a for it in items for a in it]

    def body(*refs):
        ins, outs = refs[:4 * n], refs[4 * n:]
        for i in range(n):
            g, w, m, v = (r[...] for r in ins[4 * i:4 * i + 4])
            outs[3 * i][...], outs[3 * i + 1][...], outs[3 * i + 2][...] = _adamw(w, g, m, v)

    out = pl.pallas_call(
        body, name="adamw_small", grid=(1,),
        in_specs=[_full(a.shape) for a in flat],
        out_specs=[_full(it[1].shape) for it in items for _ in range(3)],
        out_shape=[_sds(it[1].shape) for it in items for _ in range(3)],
        compiler_params=_cparams(1),
    )(*flat)
    return [tuple(out[3 * i:3 * i + 3]) for i in range(n)]


def _mesh_pos():
    x, y, c = lax.axis_index("x"), lax.axis_index("y"), lax.axis_index("c")
    me = 4 * x + 2 * y + c
    peers = []
    for k in range(1, N_DEV):
        peers.append(((1 - x) if (k >> 2) & 1 else x, (1 - y) if (k >> 1) & 1 else y, (1 - c) if k & 1 else c))
    return me, peers


def _all_gather(buf, send_sems, recv_sems, me, peers):
    sends = []
    for k, peer in enumerate(peers):
        cp = pltpu.make_async_remote_copy(src_ref=buf.at[me], dst_ref=buf.at[me], send_sem=send_sems.at[k],
                                          recv_sem=recv_sems.at[k], device_id=peer, device_id_type=MESH)
        cp.start()
        sends.append(cp)
    for k, peer in enumerate(peers):
        src = jnp.bitwise_xor(me, k + 1)
        pltpu.make_async_remote_copy(src_ref=buf.at[src], dst_ref=buf.at[src], send_sem=send_sems.at[k],
                                     recv_sem=recv_sems.at[k], device_id=peer, device_id_type=MESH).wait_recv()
    for cp in sends:
        cp.wait_send()


_VMEM = pl.BlockSpec(memory_space=pltpu.VMEM)
_ANY = pl.BlockSpec(memory_space=pl.ANY)
_SEMS = pltpu.SemaphoreType.DMA((N_DEV - 1,))


def mod_forward(c, c_ctx, w_mod_sh, b_mod):
    ncol = w_mod_sh.shape[1]

    def body(c_ref, cc_ref, w_ref, b_ref, mod_ref, s_ref, cbuf, pbuf, s1, r1, s2, r2):
        me, peers = _mesh_pos()
        cbuf[me] = jnp.broadcast_to(c_ref[...], (8, D))
        _all_gather(cbuf, s1, r1, me, peers)
        rows = [cbuf[j, 0:1, :] for j in range(N_DEV)] + [cc_ref[...], jnp.zeros((7, D), F32)]
        sx = jnp.concatenate(rows, axis=0)
        s = sx * _sigmoid(sx)
        s_ref[...] = s
        pbuf[me] = _hi(s, w_ref[...])
        _all_gather(pbuf, s2, r2, me, peers)
        for j in range(N_DEV):
            mod_ref[:, j * ncol:(j + 1) * ncol] = pbuf[j] + b_ref[:, j * ncol:(j + 1) * ncol]

    return pl.pallas_call(
        body, name="mod_forward",
        in_specs=[_VMEM] * 4, out_specs=[_VMEM] * 2,
        out_shape=[_sds((16, N_DEV * ncol)), _sds((16, D))],
        scratch_shapes=[pltpu.VMEM((N_DEV, 8, D), F32), pltpu.VMEM((N_DEV, 16, ncol), F32), _SEMS, _SEMS, _SEMS,
                        _SEMS],
        compiler_params=pltpu.CompilerParams(vmem_limit_bytes=VMEM_LIMIT),
    )(c, c_ctx, w_mod_sh, b_mod)


def gather_sum_small(pack):
    shape = pack.shape

    def body(p_ref, tot_ref, gat_ref, send_sems, recv_sems):
        me, peers = _mesh_pos()
        gat_ref[me] = p_ref[...]
        _all_gather(gat_ref, send_sems, recv_sems, me, peers)
        tot = gat_ref[0]
        for j in range(1, N_DEV):
            tot = tot + gat_ref[j]
        tot_ref[...] = tot

    return pl.pallas_call(
        body, name="gather_sum_small", in_specs=[_VMEM], out_specs=[_VMEM, _VMEM],
        out_shape=[_sds(shape), _sds((N_DEV,) + shape)], scratch_shapes=[_SEMS, _SEMS],
        compiler_params=pltpu.CompilerParams(vmem_limit_bytes=VMEM_LIMIT),
    )(pack)


def mod_backward(s, dm_sh, w, m, v, cc, m_cc, v_cc):
    shape = w.shape

    def body(s_ref, dm_ref, w_ref, m_ref, v_ref, cc_ref, mcc_ref, vcc_ref,
             gw_ref, dw_ref, mw_ref, vw_ref, gc_ref, dc_ref, mc_ref, vc_ref, pbuf, send_sems, recv_sems):
        me, peers = _mesh_pos()
        wv = w_ref[...]
        pbuf[me] = _hi_nt(dm_ref[8:16, :], wv)
        _all_gather(pbuf, send_sems, recv_sems, me, peers)
        g = _hi_tn(s_ref[...], dm_ref[...])
        gw_ref[...] = g
        dw_ref[...], mw_ref[...], vw_ref[...] = _adamw(wv, g, m_ref[...], v_ref[...])
        tot = pbuf[0]
        for j in range(1, N_DEV):
            tot = tot + pbuf[j]
        ccv = cc_ref[...]
        sg = _sigmoid(ccv)
        gc = tot[0:1, :] * (sg * (1.0 + ccv * (1.0 - sg)))
        gc_ref[...] = gc
        dc_ref[...], mc_ref[...], vc_ref[...] = _adamw(ccv, gc, mcc_ref[...], vcc_ref[...])

    return pl.pallas_call(
        body, name="mod_backward", in_specs=[_VMEM] * 8, out_specs=[_VMEM] * 8,
        out_shape=[_sds(shape)] * 4 + [_sds((1, D))] * 4,
        scratch_shapes=[pltpu.VMEM((N_DEV, 8, D), F32), _SEMS, _SEMS],
        compiler_params=pltpu.CompilerParams(vmem_limit_bytes=VMEM_LIMIT),
    )(s, dm_sh, w, m, v, cc, m_cc, v_cc)


_HBM = pl.BlockSpec(memory_space=pltpu.HBM)
_SEM = pl.BlockSpec(memory_space=pltpu.SEMAPHORE)
_EFFECT = pltpu.SideEffectType.DATAFLOW_SIDE_EFFECTING
_hbm = lambda a: pltpu.with_memory_space_constraint(a, pltpu.HBM)


def gather_start(shards, me, tag):
    n = len(shards)
    sems = pltpu.SemaphoreType.DMA((7 * n,))
    lands = [lax.dynamic_update_slice(lax.empty((N_DEV,) + s.shape, s.dtype), s[None], (me, 0, 0)) for s in shards]

    def body(*refs):
        s_refs, l_refs = refs[:n], refs[n:2 * n]
        send_sems, recv_sems = refs[2 * n], refs[2 * n + 1]
        token = refs[-1]
        my, peers = _mesh_pos()
        for w in range(n):
            for k, peer in enumerate(peers):
                pltpu.make_async_remote_copy(
                    src_ref=s_refs[w], dst_ref=l_refs[w].at[my], send_sem=send_sems.at[w * 7 + k],
                    recv_sem=recv_sems.at[w * 7 + k], device_id=peer, device_id_type=MESH).start()
        token[...] = jnp.zeros_like(token)

    out = pl.pallas_call(
        body, name="gather_start_" + tag,
        out_shape=(sems, sems) + tuple(pltpu.HBM(a.shape, a.dtype) for a in list(shards) + lands) + (_sds((8, 128)),),
        in_specs=(_HBM,) * (2 * n), out_specs=(_SEM, _SEM) + (_HBM,) * (2 * n) + (_VMEM,),
        input_output_aliases={i: i + 2 for i in range(2 * n)},
        compiler_params=pltpu.CompilerParams(has_side_effects=_EFFECT),
    )(*[_hbm(a) for a in list(shards) + lands])
    return out[0], out[1], list(out[2:2 + n]), list(out[2 + n:2 + 2 * n]), out[-1]


def gather_wait(send_sems, recv_sems, shards, lands, after, tag):
    n = len(shards)

    def body(*refs):
        s_refs, l_refs = refs[:n], refs[n:2 * n]
        send_sems, recv_sems = refs[2 * n], refs[2 * n + 1]
        my, peers = _mesh_pos()
        for w in range(n):
            for k, peer in enumerate(peers):
                src = jnp.bitwise_xor(my, k + 1)
                cp = pltpu.make_async_remote_copy(
                    src_ref=s_refs[w], dst_ref=l_refs[w].at[src], send_sem=send_sems.at[w * 7 + k],
                    recv_sem=recv_sems.at[w * 7 + k], device_id=peer, device_id_type=MESH)
                cp.wait_send()
                cp.wait_recv()

    out = pl.pallas_call(
        body, name="gather_wait_" + tag,
        out_shape=tuple(pltpu.HBM(a.shape, a.dtype) for a in list(shards) + list(lands)),
        in_specs=(_HBM,) * (2 * n) + (_SEM, _SEM, _ANY), out_specs=(_HBM,) * (2 * n),
        input_output_aliases={i: i for i in range(2 * n)},
        compiler_params=pltpu.CompilerParams(has_side_effects=_EFFECT),
    )(*shards, *lands, send_sems, recv_sems, after)
    return list(out[n:2 * n])


def scatter_start(grads, tag):
    n = len(grads)
    sems = pltpu.SemaphoreType.DMA((7 * n,))
    lands = [lax.empty((N_DEV - 1,) + g.shape[1:], g.dtype) for g in grads]

    def body(*refs):
        g_refs, l_refs = refs[:n], refs[n:2 * n]
        send_sems, recv_sems = refs[2 * n], refs[2 * n + 1]
        token = refs[-1]
        me, peers = _mesh_pos()
        for w in range(n):
            for k, peer in enumerate(peers):
                dst = jnp.bitwise_xor(me, k + 1)
                pltpu.make_async_remote_copy(
                    src_ref=g_refs[w].at[dst], dst_ref=l_refs[w].at[k], send_sem=send_sems.at[w * 7 + k],
                    recv_sem=recv_sems.at[w * 7 + k], device_id=peer, device_id_type=MESH).start()
        token[...] = jnp.zeros_like(token)

    out = pl.pallas_call(
        body, name="scatter_start_" + tag,
        out_shape=(sems, sems) + tuple(pltpu.HBM(a.shape, a.dtype) for a in list(grads) + lands) + (_sds((8, 128)),),
        in_specs=(_HBM,) * (2 * n), out_specs=(_SEM, _SEM) + (_HBM,) * (2 * n) + (_VMEM,),
        input_output_aliases={i: i + 2 for i in range(2 * n)},
        compiler_params=pltpu.CompilerParams(has_side_effects=_EFFECT),
    )(*[_hbm(a) for a in list(grads) + lands])
    return out[0], out[1], list(out[2:2 + n]), list(out[2 + n:2 + 2 * n]), out[-1]


def scatter_wait(send_sems, recv_sems, grads, lands, after, tag):
    n = len(grads)

    def body(*refs):
        g_refs, l_refs = refs[:n], refs[n:2 * n]
        send_sems, recv_sems = refs[2 * n], refs[2 * n + 1]
        me, peers = _mesh_pos()
        for w in range(n):
            for k, peer in enumerate(peers):
                dst = jnp.bitwise_xor(me, k + 1)
                cp = pltpu.make_async_remote_copy(
                    src_ref=g_refs[w].at[dst], dst_ref=l_refs[w].at[k], send_sem=send_sems.at[w * 7 + k],
                    recv_sem=recv_sems.at[w * 7 + k], device_id=peer, device_id_type=MESH)
                cp.wait_send()
                cp.wait_recv()

    out = pl.pallas_call(
        body, name="scatter_wait_" + tag,
        out_shape=tuple(pltpu.HBM(a.shape, a.dtype) for a in list(grads) + list(lands)),
        in_specs=(_HBM,) * (2 * n) + (_SEM, _SEM, _ANY), out_specs=(_HBM,) * (2 * n),
        input_output_aliases={i: i for i in range(2 * n)},
        compiler_params=pltpu.CompilerParams(has_side_effects=_EFFECT),
    )(*grads, *lands, send_sems, recv_sems, after)
    return list(out[n:2 * n])


def _vec8(rows, width):
    rid = lax.broadcasted_iota(jnp.int32, (8, width), 0)
    out = jnp.zeros((8, width), F32)
    for i, r in enumerate(rows):
        r = r.reshape(-1)
        r = jnp.pad(r, (0, width - r.shape[0]))
        out = jnp.where(rid == i, r[None, :], out)
    return out


def local_step(x, ctx, tgt, mod, mod_c, small, w_int, start, late_weights, grads_ready, small_ready, tt, tt_ctx, cb,
               cb_ctx):
    sh1, sc1, g1, sh2, sc2, g2 = [mod[i * D:(i + 1) * D] for i in range(6)]
    csh1, csc1 = mod_c[0:D], mod_c[D:2 * D]
    vec1 = _vec8([small["norm1_g"], sh1, sc1], D)
    vec1c = _vec8([small["norm1_g"], csh1, csc1], D)
    vec2 = _vec8([small["norm2_g"], sh2, sc2], D)
    vec3 = _vec8([g2, small["final_g"]], D)
    vecm = _vec8([g1, small["norm2_g"], sh2, sc2], D)
    vcm = _vec8([jnp.tile(small["gla_norm_g"].reshape(HV), NH), small["conv_b"], small["conv_ln_g"],
                 small["conv_ln_b"]], DC)
    convw = jnp.pad(small["conv_w"], ((0, 1), (0, 0)))
    wa = jnp.zeros((128, 512), F32)
    wa = wa.at[0:RANK, 0:DK].set(small["w_a2_f"]).at[RANK:2 * RANK, DK:2 * DK].set(small["w_a2_b"])
    ba = jnp.concatenate([small["b_a_f"].reshape(1, DK), small["b_a_b"].reshape(1, DK)], axis=1)

    _, _, kc, vc_, _, rc, lac, hc = proj_fwd(ctx, vec1c + start, w_int, wa, ba, tt_ctx)
    qc0 = jnp.zeros_like(kc)
    _, _, sallf_c, sallb_c, sfin_c = gla_fwd(qc0, kc, vc_, lac, jnp.zeros((2, HV, DK), F32), cb_ctx)
    u, q, k, v, g, r, la, h = proj_fwd(x, vec1, w_int, wa, ba, tt)
    o_f, o_b, sall_f, sall_b, _ = gla_fwd(q, k, v, la, sfin_c, cb)
    w_out, wg_t, wu_t, w_down = late_weights(o_b)
    x1, cat, mix, yc, h2 = merge_fwd(u, g, o_f, o_b, x, vecm, vcm, convw, w_out, tt)
    act, dact, hid = ffn_gate_up(h2, wg_t, wu_t, tt)
    dx2, dff, acc3 = ffn_down_loss(hid, x1, tgt, vec3, w_down, tt)
    dgt, dup = ffn_dhid(dff, act, dact, w_down, tt)
    dx1, acc2 = ffn_dh2(dgt, dup, x1, dx2, vec2, wg_t, wu_t, tt)
    bt = min(2048, x.shape[0])
    gw = {"w_down": tn_matmul(hid, dff, FN, bt), "wg_t": tn_matmul(dgt, h2, FN, bt),
          "wu_t": tn_matmul(dup, h2, FN, bt)}
    vecm = vecm + grads_ready(("wg_t", "wu_t", "w_down"), gw)
    du, dg, do, dmix, accm1, accm2, dconvw = merge_bwd(dx1, mix, u, g, o_f, o_b, yc, vecm, vcm, convw, w_out, tt)
    gw["w_out"] = tn_matmul(cat, dmix, 512, bt)
    dsfin = jnp.zeros((2, HV, DK), F32) + grads_ready(("w_out",), gw)
    dqf, dkf, dvf, dlaf, dqb, dkb, dvb, dlab, ds0 = gla_bwd(q, k, v, la, do, sall_f, sall_b, dsfin, cb)
    gx, dp, acc1, dba, dwa = proj_bwd(du, (dqf, dqb, dkf, dkb, dvf, dvb), dg, dlaf, dlab, la, r, x, dx1, vec1, w_int,
                                      wa, tt)
    tcx = ctx.shape[0]
    zc = lambda w, dt=MXU_DTYPE: jnp.zeros((tcx, w), dt)
    _, dkf, dvf, dlaf, _, dkb, dvb, dlab, _ = gla_bwd(qc0, kc, vc_, lac, zc(DV), sallf_c, sallb_c, ds0, cb_ctx)
    _, dpc, acc1c, dbac, dwac = proj_bwd(zc(1024), (zc(DK), zc(DK), dkf, dkb, dvf, dvb), zc(DV), dlaf, dlab, lac, rc,
                                         ctx, zc(D, F32), vec1c, w_int, wa, tt_ctx)
    dwa_t = dwa + dwac
    dba_t = dba + dbac
    gs = {
        "norm1_g": acc1[2] + acc1c[2], "norm2_g": acc2[2], "final_g": acc3[1], "loss": acc3[2],
        "gla_norm_g": accm2[0], "conv_b": accm2[1], "conv_ln_g": accm2[2], "conv_ln_b": accm2[3],
        "conv_w": dconvw, "b_a": dba_t[0], "w_a2": dwa_t,
    }
    dmod = _vec8([acc1[0], acc1[1], accm1[0], acc2[0], acc2[1], acc3[0]], D)
    dmod_c = _vec8([acc1c[0], acc1c[1]], D)
    dpc = dpc + small_ready(gs, dmod, dmod_c).astype(dpc.dtype)
    btc = min(1024, tcx)
    gw["w_int"] = tn_matmul(dp, h, 896, bt, init=tn_matmul(dpc, hc, 896, btc)[0])
    grads_ready(("w_int",), gw)
    return gx, gw


PACK_ROWS = 96
ROW_N1, ROW_N2, ROW_FG, ROW_LOSS, ROW_GN, ROW_CB, ROW_LG, ROW_LB, ROW_BA = 0, 1, 2, 3, 4, 5, 6, 7, 8
ROW_DMOD, ROW_DMODC, ROW_CW, ROW_WA = 16, 24, 32, 64


def _pack_small(gs, dmod, dmod_c):
    pad = lambda a: jnp.pad(a, ((0, 0), (0, D - a.shape[1])))
    singles = _vec8([gs["norm1_g"], gs["norm2_g"], gs["final_g"], gs["loss"], gs["gla_norm_g"], gs["conv_b"],
                     gs["conv_ln_g"], gs["conv_ln_b"]], D)
    return jnp.concatenate([singles, _vec8([gs["b_a"]], D), dmod, dmod_c, pad(gs["conv_w"]), pad(gs["w_a2"][0:32])],
                           axis=0)


def kernel(x, c, ctx, c_ctx, w_mod, b_mod, norm1_g, norm2_g, w_in, conv_w, conv_b, conv_ln_g, conv_ln_b, w_a2_f, b_a_f, w_a2_b, b_a_b, gla_norm_g, w_out, w_gate, w_up, w_down, final_g, loss_target, m_c_ctx, m_w_mod, m_b_mod, m_norm1_g, m_norm2_g, m_w_in, m_conv_w, m_conv_b, m_conv_ln_g, m_conv_ln_b, m_w_a2_f, m_b_a_f, m_w_a2_b, m_b_a_b, m_gla_norm_g, m_w_out, m_w_gate, m_w_up, m_w_down, m_final_g, v_c_ctx, v_w_mod, v_b_mod, v_norm1_g, v_norm2_g, v_w_in, v_conv_w, v_conv_b, v_conv_ln_g, v_conv_ln_b, v_w_a2_f, v_b_a_f, v_w_a2_b, v_b_a_b, v_gla_norm_g, v_w_out, v_w_gate, v_w_up, v_w_down, v_final_g):
    me = 4 * lax.axis_index("x") + 2 * lax.axis_index("y") + lax.axis_index("c")
    t = x.shape[1]
    tcx = ctx.shape[1]
    r_in, r_out, r_ff = w_in.shape[2], w_out.shape[1], w_gate.shape[2]
    r_in_b = -(-r_in // 16) * 16

    tb = lambda w: w.T.astype(MXU_DTYPE)
    i_send, i_recv, i_thru, i_lands, i_token = gather_start([jnp.pad(tb(w_in[0]), ((0, r_in_b - r_in), (0, 0)))], me,
                                                            "w_in")

    mod_all, s_all = mod_forward(c + i_token[0:1, 0:1], c_ctx.reshape(1, D), w_mod[0], b_mod)
    mod = lax.dynamic_slice(mod_all, (me, 0), (1, 6 * D)).reshape(6 * D)
    mod_c = mod_all[8]

    small = dict(norm1_g=norm1_g[0], norm2_g=norm2_g[0], final_g=final_g, gla_norm_g=gla_norm_g[0],
                 conv_b=conv_b[0], conv_ln_g=conv_ln_g[0], conv_ln_b=conv_ln_b[0], b_a_f=b_a_f[0], b_a_b=b_a_b[0])
    sm_pack = jnp.zeros((48, DC), F32)
    sm_pack = lax.dynamic_update_slice(sm_pack, conv_w[0], (0, me * (DC // N_DEV)))
    sm_pack = lax.dynamic_update_slice(sm_pack, w_a2_f[0], (32, me * (DK // N_DEV)))
    sm_pack = lax.dynamic_update_slice(sm_pack, w_a2_b[0], (32, DK + me * (DK // N_DEV)))
    sm_tot, _ = gather_sum_small(sm_pack)
    small["conv_w"] = sm_tot[0:CW, :]
    small["w_a2_f"] = sm_tot[32:32 + RANK, 0:DK]
    small["w_a2_b"] = sm_tot[32:32 + RANK, DK:2 * DK]

    wall = gather_wait(i_send, i_recv, i_thru, i_lands, sm_tot + mod_all[0:1, 0:1], "w_in")[0]
    w_int = jnp.pad(wall[:, 0:r_in, :].reshape(N_DEV * r_in, D), ((0, DINP - DIN), (0, 0)))
    after_w_in = (wall[0:1, 0:1, 0] * 0).astype(MXU_DTYPE)
    late = [w_out[0].astype(MXU_DTYPE) + after_w_in, tb(w_gate[0]) + after_w_in, tb(w_up[0]) + after_w_in,
            w_down[0].astype(MXU_DTYPE) + after_w_in]
    g_send, g_recv, late_thru, late_lands, g_token = gather_start(late, me, "late")

    def late_weights(after):
        got = gather_wait(g_send, g_recv, late_thru, late_lands, after, "late")
        return tuple(a.reshape(N_DEV * a.shape[1], D) for a in got)

    pad_in = lambda g: jnp.pad(g[0:DIN].reshape(N_DEV, r_in, D), ((0, 0), (0, r_in_b - r_in), (0, 0)))
    blocked = {"w_int": pad_in, "w_out": lambda g: g.reshape(N_DEV, r_out, D)}
    as_blocks = lambda n, g: blocked.get(n, lambda a: a.reshape(N_DEV, r_ff, D))(g)
    pending = []

    def grads_ready(names, gw_now):
        send, recv_s, thru, zones, token = scatter_start([as_blocks(n, gw_now[n][1]) for n in names], names[0])
        pending.append((names, send, recv_s, thru, zones))
        return token[0:1, 0:1]

    sm = {}

    def small_ready(gs, dmod, dmod_c):
        sm["tot"], gat = gather_sum_small(_pack_small(gs, dmod, dmod_c))
        sm["dm"] = jnp.concatenate(
            [gat[:, ROW_DMOD:ROW_DMOD + 6, :].reshape(N_DEV, 6 * D),
             jnp.pad(sm["tot"][ROW_DMODC:ROW_DMODC + 6, :].reshape(1, 6 * D), ((0, 7), (0, 0)))], axis=0)
        ncol = w_mod.shape[2]
        dm_sh = lax.dynamic_slice(sm["dm"], (0, me * ncol), (16, ncol))
        sm["mod"] = mod_backward(s_all, dm_sh, w_mod[0], m_w_mod[0], v_w_mod[0], c_ctx.reshape(1, D),
                                 m_c_ctx.reshape(1, D), v_c_ctx.reshape(1, D))
        return sm["mod"][4][0:1, 0:1] * 0

    gx, gw = local_step(x[0], ctx[0], loss_target[0], mod, mod_c, small, w_int, g_token[0:1, 0:1], late_weights,
                        grads_ready, small_ready, 512, 256, 8, 4)
    tot, dm = sm["tot"], sm["dm"]
    loss = jnp.sum(tot[ROW_LOSS])
    g_wmod, d_wmod, nm_wmod, nv_wmod, g_cc, d_cc, nm_cc, nv_cc = sm["mod"]

    recv = {}

    def wait_for(entry, after):
        names, send, recv_s, thru, zones = entry
        recv.update(dict(zip(names, scatter_wait(send, recv_s, thru, zones, after, names[0]))))

    for entry in pending[:-1]:
        wait_for(entry, tot)
    own = {n: lax.dynamic_index_in_dim(as_blocks(n, gw[n][0]), me, 0, keepdims=False) for n in gw}
    padt = lambda w: jnp.pad(w.T, ((0, r_in_b - r_in), (0, 0)))
    big = {}
    big["w_gate"] = [a.T for a in adamw_sharded(own["wg_t"], recv["wg_t"], w_gate[0].T, m_w_gate[0].T,
                                                 v_w_gate[0].T)]
    big["w_up"] = [a.T for a in adamw_sharded(own["wu_t"], recv["wu_t"], w_up[0].T, m_w_up[0].T, v_w_up[0].T)]
    big["w_down"] = adamw_sharded(own["w_down"], recv["w_down"], w_down[0], m_w_down[0], v_w_down[0])
    big["w_out"] = adamw_sharded(own["w_out"], recv["w_out"], w_out[0], m_w_out[0], v_w_out[0])
    wait_for(pending[-1], big["w_out"][0])
    big["w_in"] = [a[0:r_in].T for a in adamw_sharded(own["w_int"], recv["w_int"], padt(w_in[0]), padt(m_w_in[0]),
                                                       padt(v_w_in[0]))]
    big["w_mod"] = [g_wmod, d_wmod, nm_wmod, nv_wmod]

    row = lambda r, w: tot[r:r + 1, 0:w]
    gn_row = tot[ROW_GN:ROW_GN + 1, 0:DC]
    g_small = {
        "b_mod": jnp.sum(dm, axis=0, keepdims=True),
        "norm1_g": row(ROW_N1, D), "norm2_g": row(ROW_N2, D),
        "conv_w": lax.dynamic_slice(tot, (ROW_CW, me * (DC // N_DEV)), (CW, DC // N_DEV)),
        "conv_b": row(ROW_CB, DC), "conv_ln_g": row(ROW_LG, DC), "conv_ln_b": row(ROW_LB, DC),
        "w_a2_f": lax.dynamic_slice(tot, (ROW_WA, me * (DK // N_DEV)), (RANK, DK // N_DEV)),
        "b_a_f": tot[ROW_BA:ROW_BA + 1, 0:DK],
        "w_a2_b": lax.dynamic_slice(tot, (ROW_WA + RANK, DK + me * (DK // N_DEV)), (RANK, DK // N_DEV)),
        "b_a_b": tot[ROW_BA:ROW_BA + 1, DK:2 * DK],
        "gla_norm_g": gn_row[:, 0:HV] + gn_row[:, HV:2 * HV] + gn_row[:, 2 * HV:3 * HV] + gn_row[:, 3 * HV:4 * HV],
        "final_g": row(ROW_FG, D),
    }
    wmv = {
        "b_mod": (b_mod, m_b_mod, v_b_mod), "norm1_g": (norm1_g, m_norm1_g, v_norm1_g),
        "norm2_g": (norm2_g, m_norm2_g, v_norm2_g), "conv_w": (conv_w[0], m_conv_w[0], v_conv_w[0]),
        "conv_b": (conv_b, m_conv_b, v_conv_b), "conv_ln_g": (conv_ln_g, m_conv_ln_g, v_conv_ln_g),
        "conv_ln_b": (conv_ln_b, m_conv_ln_b, v_conv_ln_b), "w_a2_f": (w_a2_f[0], m_w_a2_f[0], v_w_a2_f[0]),
        "b_a_f": (b_a_f, m_b_a_f, v_b_a_f), "w_a2_b": (w_a2_b[0], m_w_a2_b[0], v_w_a2_b[0]),
        "b_a_b": (b_a_b, m_b_a_b, v_b_a_b), "gla_norm_g": (gla_norm_g, m_gla_norm_g, v_gla_norm_g),
        "final_g": (final_g.reshape(1, D), m_final_g.reshape(1, D), v_final_g.reshape(1, D)),
    }
    names_small = list(g_small)
    upd = adamw_small([(g_small[n],) + wmv[n] for n in names_small])
    res = {n: (g_small[n],) + upd[i] for i, n in enumerate(names_small)}
    res["c_ctx"] = (g_cc, d_cc, nm_cc, nv_cc)
    for n in ("w_mod", "w_in", "w_out", "w_gate", "w_up", "w_down"):
        res[n] = tuple(big[n])

    order = ["c_ctx", "w_mod", "b_mod", "norm1_g", "norm2_g", "w_in", "conv_w", "conv_b", "conv_ln_g", "conv_ln_b",
             "w_a2_f", "b_a_f", "w_a2_b", "b_a_b", "gla_norm_g", "w_out", "w_gate", "w_up", "w_down", "final_g"]
    shapes = {"c_ctx": c_ctx.shape, "w_mod": w_mod.shape, "b_mod": b_mod.shape, "norm1_g": norm1_g.shape,
              "norm2_g": norm2_g.shape, "w_in": w_in.shape, "conv_w": conv_w.shape, "conv_b": conv_b.shape,
              "conv_ln_g": conv_ln_g.shape, "conv_ln_b": conv_ln_b.shape, "w_a2_f": w_a2_f.shape,
              "b_a_f": b_a_f.shape, "w_a2_b": w_a2_b.shape, "b_a_b": b_a_b.shape, "gla_norm_g": gla_norm_g.shape,
              "w_out": w_out.shape, "w_gate": w_gate.shape, "w_up": w_up.shape, "w_down": w_down.shape,
              "final_g": final_g.shape}
    outs = [loss, gx.reshape(x.shape)]
    for i in range(4):
        outs += [res[n][i].reshape(shapes[n]) for n in order]
    return tuple(outs)
```

```python
import functools

import jax
import jax.numpy as jnp
from jax import lax
from jax.experimental import pallas as pl
from jax.experimental.pallas import tpu as pltpu

F32 = jnp.float32
MXU_DTYPE = jnp.bfloat16
WIRE_DTYPE = jnp.bfloat16
HI = lax.Precision.HIGHEST
MESH = pl.DeviceIdType.MESH

N_DEV = 8
D = 1024
DC = 512
NH = 4
HK = 64
HV = 128
DK = NH * HK
DV = NH * HV
RANK = 16
CHUNK = 64
SEG = 64
CW = 31
CPAD = 15
DFF = 2816
DIN = 2592
DINP = 2688
TAU = 16.0
EPS = 1e-6
VMEM_LIMIT = 56 * 1024 * 1024

ADAM_LR = 0.001
ADAM_B1 = 0.9
ADAM_B2 = 0.999
ADAM_EPS = 1e-08
ADAM_WD = 0.01
ADAM_STEP = 10


def _mm(a, b):
    return jnp.dot(a.astype(MXU_DTYPE), b.astype(MXU_DTYPE), preferred_element_type=F32)


def _mm_nt(a, b):
    return lax.dot_general(a.astype(MXU_DTYPE), b.astype(MXU_DTYPE), (((1,), (1,)), ((), ())),
                           preferred_element_type=F32)


def _mm_tn(a, b):
    return lax.dot_general(a.astype(MXU_DTYPE), b.astype(MXU_DTYPE), (((0,), (0,)), ((), ())),
                           preferred_element_type=F32)


def _hi(a, b):
    return jnp.dot(a, b, precision=HI, preferred_element_type=F32)


def _hi_nt(a, b):
    return lax.dot_general(a, b, (((1,), (1,)), ((), ())), precision=HI, preferred_element_type=F32)


def _hi_tn(a, b):
    return lax.dot_general(a, b, (((0,), (0,)), ((), ())), precision=HI, preferred_element_type=F32)


def _sigmoid(x):
    return 1.0 / (1.0 + jnp.exp(-x))


def _cparams(n_axes):
    return pltpu.CompilerParams(dimension_semantics=("arbitrary",) * n_axes, vmem_limit_bytes=VMEM_LIMIT)


def _full(shape):
    n = len(shape)
    return pl.BlockSpec(shape, lambda *_: (0,) * n)


def _rows(tt, width):
    return pl.BlockSpec((tt, width), lambda i: (i, 0))


def _sds(shape, dtype=F32):
    return jax.ShapeDtypeStruct(shape, dtype)


def _norm_mod(x, g, sh, sc):
    r = lax.rsqrt(jnp.mean(x * x, axis=-1, keepdims=True) + EPS)
    xn = x * r
    yy = xn * g
    return r, xn, yy, yy * (1.0 + sc) + sh


def _norm_mod_bwd(dh, r, xn, yy, g, sc):
    dsh = jnp.sum(dh, axis=0, keepdims=True)
    dsc = jnp.sum(dh * yy, axis=0, keepdims=True)
    dy = dh * (1.0 + sc)
    dg = jnp.sum(dy * xn, axis=0, keepdims=True)
    dxn = dy * g
    dx = r * (dxn - xn * jnp.mean(dxn * xn, axis=-1, keepdims=True))
    return dsh, dsc, dg, dx


def _zero_first(*refs):
    @pl.when(pl.program_id(0) == 0)
    def _():
        for r in refs:
            r[...] = jnp.zeros_like(r)


def _acc_rows(ref, rows):
    ref[...] += jnp.concatenate(rows + [jnp.zeros((8 - len(rows), rows[0].shape[1]), F32)], axis=0)


def proj_fwd(x, vec, w_int, wa, ba, tt):
    t = x.shape[0]

    def body(x_ref, vec_ref, w_ref, wa_ref, ba_ref, u_ref, q_ref, k_ref, v_ref, g_ref, r_ref, la_ref, h_ref):
        _, _, _, h = _norm_mod(x_ref[...], vec_ref[0:1, :], vec_ref[1:2, :], vec_ref[2:3, :])
        hb = h.astype(MXU_DTYPE)
        h_ref[...] = hb
        p = _mm_nt(hb, w_ref[...])
        u_ref[...] = p[:, 0:1024]
        q_ref[...] = p[:, 1024:1280]
        k_ref[...] = p[:, 1280:1536]
        v_ref[...] = p[:, 1536:2048]
        g_ref[...] = p[:, 2048:2560]
        rr = p[:, 2560:2688]
        r_ref[...] = rr
        z = _mm(rr, wa_ref[...]) + ba_ref[...]
        la_ref[...] = (jnp.minimum(z, 0.0) - jnp.log(1.0 + jnp.exp(-jnp.abs(z)))) * (1.0 / TAU)

    return pl.pallas_call(
        body, name="proj_fwd", grid=(t // tt,),
        in_specs=[_rows(tt, D), _full((8, D)), _full((DINP, D)), _full((128, 512)), _full((1, 512))],
        out_specs=[_rows(tt, 1024), _rows(tt, DK), _rows(tt, DK), _rows(tt, DV), _rows(tt, DV), _rows(tt, 128),
                   _rows(tt, 512), _rows(tt, D)],
        out_shape=[_sds((t, 1024)), _sds((t, DK)), _sds((t, DK)), _sds((t, DV)), _sds((t, DV)), _sds((t, 128)),
                   _sds((t, 512)), _sds((t, D), MXU_DTYPE)],
        compiler_params=_cparams(1),
    )(x, vec, w_int, wa, ba)


def proj_bwd(du, dqkv, dg, dla_f, dla_b, la, r, x, dx1, vec, w_int, wa, tt):
    t = x.shape[0]

    def body(du_ref, dqf_ref, dqb_ref, dkf_ref, dkb_ref, dvf_ref, dvb_ref, dg_ref, dlaf_ref, dlab_ref, la_ref, r_ref,
             x_ref, dx1_ref, vec_ref, w_ref, wa_ref, gx_ref, dp_ref, acc_ref, dba_ref, dwa_ref):
        _zero_first(acc_ref, dba_ref, dwa_ref)
        dla = jnp.concatenate([dlaf_ref[...], dlab_ref[...]], axis=1)
        dz = dla * (1.0 - jnp.exp(TAU * la_ref[...])) * (1.0 / TAU)
        rr = r_ref[...]
        _acc_rows(dba_ref, [jnp.sum(dz, axis=0, keepdims=True)])
        dwa_ref[...] += _mm_tn(rr, dz)
        dr = _mm_nt(dz, wa_ref[...])
        md = lambda a: a.astype(MXU_DTYPE)
        both = lambda a_ref, b_ref: md(a_ref[...].astype(F32) + b_ref[...].astype(F32))
        dp = jnp.concatenate([du_ref[...], both(dqf_ref, dqb_ref), both(dkf_ref, dkb_ref), both(dvf_ref, dvb_ref),
                              dg_ref[...], md(dr)], axis=1)
        dp_ref[...] = dp
        dh = _mm(dp, w_ref[...])
        g, sc = vec_ref[0:1, :], vec_ref[2:3, :]
        rn, xn, yy, _ = _norm_mod(x_ref[...], g, vec_ref[1:2, :], sc)
        dsh, dsc, dgn, dx = _norm_mod_bwd(dh, rn, xn, yy, g, sc)
        gx_ref[...] = dx1_ref[...] + dx
        _acc_rows(acc_ref, [dsh, dsc, dgn])

    return pl.pallas_call(
        body, name="proj_bwd", grid=(t // tt,),
        in_specs=[_rows(tt, 1024), _rows(tt, DK), _rows(tt, DK), _rows(tt, DK), _rows(tt, DK), _rows(tt, DV),
                  _rows(tt, DV), _rows(tt, DV), _rows(tt, DK), _rows(tt, DK), _rows(tt, 512),
                  _rows(tt, 128), _rows(tt, D), _rows(tt, D), _full((8, D)), _full((DINP, D)), _full((128, 512))],
        out_specs=[_rows(tt, D), _rows(tt, DINP), _full((8, D)), _full((8, 512)), _full((128, 512))],
        out_shape=[_sds((t, D)), _sds((t, DINP), MXU_DTYPE), _sds((8, D)), _sds((8, 512)), _sds((128, 512))],
        compiler_params=_cparams(1),
    )(du, *dqkv, dg, dla_f, dla_b, la, r, x, dx1, vec, w_int, wa)


def _dot_exact01(m01, x):
    bf = jnp.bfloat16
    w = x.shape[1]
    hi = x.astype(bf)
    r1 = x - hi.astype(F32)
    mid = r1.astype(bf)
    lo = (r1 - mid.astype(F32)).astype(bf)
    y = jnp.dot(m01.astype(bf), jnp.concatenate([hi, mid, lo], axis=1), preferred_element_type=F32)
    return y[:, 0:w] + y[:, w:2 * w] + y[:, 2 * w:3 * w]


def _gla_chunk(d, qc, kc, la_c):
    row = lax.broadcasted_iota(jnp.int32, (CHUNK, CHUNK), 0)
    col = lax.broadcasted_iota(jnp.int32, (CHUNK, CHUNK), 1)
    cum = ((col <= row) if d == 0 else (col >= row)).astype(F32)
    cum_t = ((col >= row) if d == 0 else (col <= row)).astype(F32)
    cum4 = jnp.concatenate([cum] * NH, axis=0)
    head_of_lane = lax.broadcasted_iota(jnp.int32, (1, DK), 1) // HK
    b = _dot_exact01(cum, la_c)
    bl = jnp.sum(la_c, axis=0, keepdims=True)
    eb = jnp.exp(b)
    enb = jnp.exp(-b)
    ekd = jnp.exp(bl - b)
    qt = qc * (HK ** -0.5) * eb
    kt = kc * enb
    kd = kc * ekd
    qst = jnp.concatenate([jnp.where(head_of_lane == h, qt, 0.0) for h in range(NH)], axis=0)
    a = _mm_nt(qst, kt) * cum4
    return cum_t, cum4, head_of_lane, eb, enb, ekd, qt, kt, kd, qst, a, jnp.exp(bl)


def gla_fwd(q, k, v, la, s0, cb):
    t = q.shape[0]
    nc = t // CHUNK
    nb = nc // cb

    def body(qf_ref, kf_ref, vf_ref, laf_ref, qb_ref, kb_ref, vb_ref, lab_ref, s0_ref,
             of_ref, ob_ref, sf_ref, sb_ref, sfin_ref, s_scr):
        i = pl.program_id(0)

        @pl.when(i == 0)
        def _():
            s_scr[...] = s0_ref[...]

        def chunk(d, jj, q_ref, k_ref, v_ref, la_ref, o_ref, sall_ref):
            rows = slice(jj * CHUNK, (jj + 1) * CHUNK)
            vc = v_ref[rows, :]
            _, _, head_of_lane, _, _, _, _, _, kd, qst, a, dec = _gla_chunk(
                d, q_ref[rows, :], k_ref[rows, :], la_ref[rows, :])
            s = s_scr[d]
            sall_ref[jj] = s
            inter = _mm_nt(qst, s)
            outs = []
            for h in range(NH):
                hs = slice(h * CHUNK, (h + 1) * CHUNK)
                outs.append(_mm(a[hs], vc[:, h * HV:(h + 1) * HV]) + inter[hs])
            o_ref[rows, :] = jnp.concatenate(outs, axis=1)
            kv = _mm_tn(vc, kd)
            s_new = dec * s
            for h in range(NH):
                s_new = s_new + jnp.where(head_of_lane == h, kv[h * HV:(h + 1) * HV], 0.0)
            s_scr[d] = s_new

        for j in range(cb):
            chunk(0, j, qf_ref, kf_ref, vf_ref, laf_ref, of_ref, sf_ref)
            chunk(1, cb - 1 - j, qb_ref, kb_ref, vb_ref, lab_ref, ob_ref, sb_ref)

        @pl.when(i == nb - 1)
        def _():
            sfin_ref[...] = s_scr[...]

    tb = cb * CHUNK
    fwd = lambda w, c=0: pl.BlockSpec((tb, w), lambda i: (i, c))
    bwd = lambda w, c=0: pl.BlockSpec((tb, w), lambda i: (nb - 1 - i, c))
    return pl.pallas_call(
        body, name="gla_fwd", grid=(nb,),
        in_specs=[fwd(DK), fwd(DK), fwd(DV), fwd(DK, 0), bwd(DK), bwd(DK), bwd(DV), bwd(DK, 1), _full((2, HV, DK))],
        out_specs=[fwd(DV), bwd(DV), pl.BlockSpec((cb, HV, DK), lambda i: (i, 0, 0)),
                   pl.BlockSpec((cb, HV, DK), lambda i: (nb - 1 - i, 0, 0)), _full((2, HV, DK))],
        out_shape=[_sds((t, DV)), _sds((t, DV)), _sds((nc, HV, DK)), _sds((nc, HV, DK)), _sds((2, HV, DK))],
        scratch_shapes=[pltpu.VMEM((2, HV, DK), F32)],
        compiler_params=_cparams(1),
    )(q, k, v, la, q, k, v, la, s0)


def gla_bwd(q, k, v, la, do, sall_f, sall_b, dsfin, cb):
    t = q.shape[0]
    nc = t // CHUNK
    nb = nc // cb

    def body(qf_ref, kf_ref, vf_ref, laf_ref, dof_ref, sf_ref, qb_ref, kb_ref, vb_ref, lab_ref, dob_ref, sb_ref,
             dsfin_ref, dqf_ref, dkf_ref, dvf_ref, dlaf_ref, dqb_ref, dkb_ref, dvb_ref, dlab_ref, ds0_ref, ds_scr):
        i = pl.program_id(0)

        @pl.when(i == 0)
        def _():
            ds_scr[...] = dsfin_ref[...]

        def chunk(d, jj, q_ref, k_ref, v_ref, la_ref, do_ref, sall_ref, dq_ref, dk_ref, dv_ref, dla_ref):
            rows = slice(jj * CHUNK, (jj + 1) * CHUNK)
            vc = v_ref[rows, :]
            doc = do_ref[rows, :]
            cum_t, cum4, head_of_lane, eb, enb, ekd, qt, kt, kd, qst, a, dec = _gla_chunk(
                d, q_ref[rows, :], k_ref[rows, :], la_ref[rows, :])
            s = sall_ref[jj]
            ds = ds_scr[d]
            hv = lambda x, h: x[:, h * HV:(h + 1) * HV]
            hr = lambda x, h: x[h * CHUNK:(h + 1) * CHUNK]
            fold = lambda x: functools.reduce(
                lambda p, c: p + c, [jnp.where(head_of_lane == h, hr(x, h), 0.0) for h in range(NH)])
            dost = jnp.concatenate([hv(doc, h) for h in range(NH)], axis=0)
            vst = jnp.concatenate([hv(vc, h) for h in range(NH)], axis=0)
            da = jnp.concatenate([_mm_nt(hv(doc, h), hv(vc, h)) for h in range(NH)], axis=0) * cum4
            dqt = fold(_mm(da, kt) + _mm(dost, s))
            dkt = _mm_tn(da, qst)
            kdst = jnp.concatenate([jnp.where(head_of_lane == h, kd, 0.0) for h in range(NH)], axis=0)
            dv_inter = _mm_nt(kdst, ds)
            dv_ref[rows, :] = jnp.concatenate(
                [_mm_tn(hr(a, h), hv(doc, h)) + hr(dv_inter, h) for h in range(NH)], axis=1).astype(MXU_DTYPE)
            dkd = fold(_mm(vst, ds))
            ds_scr[d] = dec * ds + _mm_tn(dost, qst)
            tkd = dkd * kd
            db = dqt * qt - dkt * kt - tkd
            dbl = jnp.sum(ds * s, axis=0, keepdims=True) * dec + jnp.sum(tkd, axis=0, keepdims=True)
            dla_ref[rows, :] = _dot_exact01(cum_t, db) + dbl
            dq_ref[rows, :] = (dqt * eb * (HK ** -0.5)).astype(MXU_DTYPE)
            dk_ref[rows, :] = (dkt * enb + dkd * ekd).astype(MXU_DTYPE)

        for j in range(cb):
            chunk(0, cb - 1 - j, qf_ref, kf_ref, vf_ref, laf_ref, dof_ref, sf_ref, dqf_ref, dkf_ref, dvf_ref, dlaf_ref)
            chunk(1, j, qb_ref, kb_ref, vb_ref, lab_ref, dob_ref, sb_ref, dqb_ref, dkb_ref, dvb_ref, dlab_ref)

        @pl.when(i == nb - 1)
        def _():
            ds0_ref[...] = ds_scr[...]

    tb = cb * CHUNK
    rev = lambda w, c=0: pl.BlockSpec((tb, w), lambda i: (nb - 1 - i, c))
    fro = lambda w, c=0: pl.BlockSpec((tb, w), lambda i: (i, c))
    st_rev = pl.BlockSpec((cb, HV, DK), lambda i: (nb - 1 - i, 0, 0))
    st_fro = pl.BlockSpec((cb, HV, DK), lambda i: (i, 0, 0))
    md = MXU_DTYPE
    return pl.pallas_call(
        body, name="gla_bwd", grid=(nb,),
        in_specs=[rev(DK), rev(DK), rev(DV), rev(DK, 0), rev(DV), st_rev,
                  fro(DK), fro(DK), fro(DV), fro(DK, 1), fro(DV), st_fro, _full((2, HV, DK))],
        out_specs=[rev(DK), rev(DK), rev(DV), rev(DK), fro(DK), fro(DK), fro(DV), fro(DK), _full((2, HV, DK))],
        out_shape=[_sds((t, DK), md), _sds((t, DK), md), _sds((t, DV), md), _sds((t, DK)),
                   _sds((t, DK), md), _sds((t, DK), md), _sds((t, DV), md), _sds((t, DK)), _sds((2, HV, DK))],
        scratch_shapes=[pltpu.VMEM((2, HV, DK), F32)],
        compiler_params=_cparams(1),
    )(q, k, v, la, do, sall_f, q, k, v, la, do, sall_b, dsfin)


def _seg_pos(tt):
    return lax.broadcasted_iota(jnp.int32, (tt, 1), 0) % SEG


def _shifted(x, s, pos, tt):
    y = x if s == 0 else pltpu.roll(x, (-s) % tt, 0)
    return jnp.where((pos + s >= 0) & (pos + s < SEG), y, 0.0)


def _head_norm(o, gn):
    rs, xs = [], []
    for h in range(NH):
        oh = o[:, h * HV:(h + 1) * HV]
        r = lax.rsqrt(jnp.mean(oh * oh, axis=-1, keepdims=True) + EPS)
        rs.append(r)
        xs.append(oh * r)
    return rs, xs


def merge_fwd(u, g, o_f, o_b, x, vec, vc, convw, w_out, tt):
    t = x.shape[0]

    def body(u_ref, g_ref, of_ref, ob_ref, x_ref, vec_ref, vc_ref, cw_ref, w_ref, x1_ref, cat_ref, mix_ref, yc_ref,
             h2_ref):
        a = u_ref[:, 0:DC]
        gate = u_ref[:, DC:2 * DC]
        vv = a * _sigmoid(gate)
        pos = _seg_pos(tt)
        cw = cw_ref[...]
        yc = jnp.zeros((tt, DC), F32) + vc_ref[1:2, :]
        for j in range(CW):
            yc = yc + _shifted(vv, j - CPAD, pos, tt) * cw[j:j + 1, :]
        yc_ref[...] = yc
        mu = jnp.mean(yc, axis=-1, keepdims=True)
        yd = yc - mu
        rs = lax.rsqrt(jnp.mean(yd * yd, axis=-1, keepdims=True) + EPS)
        ln = yd * rs * vc_ref[2:3, :] + vc_ref[3:4, :]
        conv_o = ln * _sigmoid(ln)
        o = of_ref[...] + ob_ref[...]
        _, xs = _head_norm(o, None)
        gg = g_ref[...]
        o2g = jnp.concatenate(xs, axis=1) * vc_ref[0:1, :] * (gg * _sigmoid(gg))
        cat = jnp.concatenate([conv_o, o2g], axis=1).astype(MXU_DTYPE)
        cat_ref[...] = cat
        mix = _mm(cat, w_ref[...])
        mix_ref[...] = mix
        x1 = x_ref[...] + vec_ref[0:1, :] * mix
        x1_ref[...] = x1
        _, _, _, h2 = _norm_mod(x1, vec_ref[1:2, :], vec_ref[2:3, :], vec_ref[3:4, :])
        h2_ref[...] = h2.astype(MXU_DTYPE)

    return pl.pallas_call(
        body, name="merge_fwd", grid=(t // tt,),
        in_specs=[_rows(tt, 1024), _rows(tt, DV), _rows(tt, DV), _rows(tt, DV), _rows(tt, D),
                  _full((8, D)), _full((8, DC)), _full((32, DC)), _full((D, D))],
        out_specs=[_rows(tt, D), _rows(tt, D), _rows(tt, D), _rows(tt, DC), _rows(tt, D)],
        out_shape=[_sds((t, D)), _sds((t, D), MXU_DTYPE), _sds((t, D)), _sds((t, DC)), _sds((t, D), MXU_DTYPE)],
        compiler_params=_cparams(1),
    )(u, g, o_f, o_b, x, vec, vc, convw, w_out)


def merge_bwd(dx1, mix, u, g, o_f, o_b, yc, vec, vc, convw, w_out, tt):
    t = dx1.shape[0]

    def body(dx1_ref, mix_ref, u_ref, g_ref, of_ref, ob_ref, yc_ref, vec_ref, vc_ref, cw_ref, w_ref,
             du_ref, dg_ref, do_ref, dmix_ref, acc1_ref, acc2_ref, dcw_ref):
        _zero_first(acc1_ref, acc2_ref, dcw_ref)
        dx1v = dx1_ref[...]
        dg1 = jnp.sum(dx1v * mix_ref[...], axis=0, keepdims=True)
        dmix = (vec_ref[0:1, :] * dx1v).astype(MXU_DTYPE)
        dmix_ref[...] = dmix
        dcat = _mm_nt(dmix, w_ref[...])
        dconv_o = dcat[:, 0:DC]
        do2 = dcat[:, DC:2 * DC]
        gn = vc_ref[0:1, :]
        o = of_ref[...] + ob_ref[...]
        rs, xs = _head_norm(o, None)
        xn = jnp.concatenate(xs, axis=1)
        gg = g_ref[...]
        sg = _sigmoid(gg)
        don = do2 * (gg * sg)
        dg_ref[...] = (do2 * (xn * gn) * (sg * (1.0 + gg * (1.0 - sg)))).astype(MXU_DTYPE)
        dgn = jnp.sum(don * xn, axis=0, keepdims=True)
        dxn = don * gn
        dos = []
        for h in range(NH):
            dh = dxn[:, h * HV:(h + 1) * HV]
            dos.append(rs[h] * (dh - xs[h] * jnp.mean(dh * xs[h], axis=-1, keepdims=True)))
        do_ref[...] = jnp.concatenate(dos, axis=1).astype(MXU_DTYPE)
        yc = yc_ref[...]
        mu = jnp.mean(yc, axis=-1, keepdims=True)
        yd = yc - mu
        rstd = lax.rsqrt(jnp.mean(yd * yd, axis=-1, keepdims=True) + EPS)
        yhat = yd * rstd
        lg = vc_ref[2:3, :]
        ln = yhat * lg + vc_ref[3:4, :]
        sl = _sigmoid(ln)
        dln = dconv_o * (sl * (1.0 + ln * (1.0 - sl)))
        dlb = jnp.sum(dln, axis=0, keepdims=True)
        dlg = jnp.sum(dln * yhat, axis=0, keepdims=True)
        dyh = dln * lg
        dyc = rstd * (dyh - jnp.mean(dyh, axis=-1, keepdims=True)
                      - yhat * jnp.mean(dyh * yhat, axis=-1, keepdims=True))
        dcb = jnp.sum(dyc, axis=0, keepdims=True)
        a = u_ref[:, 0:DC]
        gate = u_ref[:, DC:2 * DC]
        sgt = _sigmoid(gate)
        vv = a * sgt
        pos = _seg_pos(tt)
        cw = cw_ref[...]
        dvv = jnp.zeros((tt, DC), F32)
        dws = []
        for j in range(CW):
            shifted_dyc = _shifted(dyc, CPAD - j, pos, tt)
            dvv = dvv + shifted_dyc * cw[j:j + 1, :]
            dws.append(jnp.sum(shifted_dyc * vv, axis=0, keepdims=True))
        dws.append(jnp.zeros((1, DC), F32))
        du_ref[:, 0:DC] = (dvv * sgt).astype(MXU_DTYPE)
        du_ref[:, DC:2 * DC] = (dvv * a * sgt * (1.0 - sgt)).astype(MXU_DTYPE)
        _acc_rows(acc1_ref, [dg1])
        _acc_rows(acc2_ref, [dgn, dcb, dlg, dlb])
        dcw_ref[...] += jnp.concatenate(dws, axis=0)

    return pl.pallas_call(
        body, name="merge_bwd", grid=(t // tt,),
        in_specs=[_rows(tt, D), _rows(tt, D), _rows(tt, 1024), _rows(tt, DV), _rows(tt, DV), _rows(tt, DV),
                  _rows(tt, DC),
                  _full((8, D)), _full((8, DC)), _full((32, DC)), _full((D, D))],
        out_specs=[_rows(tt, 1024), _rows(tt, DV), _rows(tt, DV), _rows(tt, D), _full((8, D)), _full((8, DC)),
                   _full((32, DC))],
        out_shape=[_sds((t, 1024), MXU_DTYPE), _sds((t, DV), MXU_DTYPE), _sds((t, DV), MXU_DTYPE),
                   _sds((t, D), MXU_DTYPE), _sds((8, D)),
                   _sds((8, DC)), _sds((32, DC))],
        compiler_params=_cparams(1),
    )(dx1, mix, u, g, o_f, o_b, yc, vec, vc, convw, w_out)


FN = DFF // 2


def ffn_gate_up(h2, wg_t, wu_t, tt):
    t = h2.shape[0]

    def body(h2_ref, wg_ref, wu_ref, s_ref, d_ref, hid_ref):
        h2v = h2_ref[...]
        gt = _mm_nt(h2v, wg_ref[...])
        up = _mm_nt(h2v, wu_ref[...])
        sg = _sigmoid(gt)
        act = gt * sg
        s_ref[...] = act.astype(MXU_DTYPE)
        d_ref[...] = (up * (sg * (1.0 + gt * (1.0 - sg)))).astype(MXU_DTYPE)
        hid_ref[...] = (act * up).astype(MXU_DTYPE)

    blk = pl.BlockSpec((tt, FN), lambda j, i: (i, j))
    wblk = pl.BlockSpec((FN, D), lambda j, i: (j, 0))
    return pl.pallas_call(
        body, name="ffn_gate_up", grid=(2, t // tt),
        in_specs=[pl.BlockSpec((tt, D), lambda j, i: (i, 0)), wblk, wblk],
        out_specs=[blk, blk, blk],
        out_shape=[_sds((t, DFF), MXU_DTYPE)] * 3,
        compiler_params=_cparams(2),
    )(h2, wg_t, wu_t)


def ffn_down_loss(hid, x1, tgt, vec, w_down, tt):
    t = x1.shape[0]

    def body(hid_ref, x1_ref, tgt_ref, vec_ref, w_ref, dx2_ref, dff_ref, acc_ref):
        _zero_first(acc_ref)
        g2 = vec_ref[0:1, :]
        fg = vec_ref[1:2, :]
        ff = _mm(hid_ref[...], w_ref[...])
        x2 = x1_ref[...] + g2 * ff
        rf = lax.rsqrt(jnp.mean(x2 * x2, axis=-1, keepdims=True) + EPS)
        xn = x2 * rf
        err = xn * fg - tgt_ref[...]
        dy = err * (1.0 / D)
        dfg = jnp.sum(dy * xn, axis=0, keepdims=True)
        dxn = dy * fg
        dx2 = rf * (dxn - xn * jnp.mean(dxn * xn, axis=-1, keepdims=True))
        dx2_ref[...] = dx2
        dff_ref[...] = (g2 * dx2).astype(MXU_DTYPE)
        dg2 = jnp.sum(dx2 * ff, axis=0, keepdims=True)
        loss = jnp.sum(err * err, axis=0, keepdims=True) * (0.5 / D)
        _acc_rows(acc_ref, [dg2, dfg, loss])

    return pl.pallas_call(
        body, name="ffn_down_loss", grid=(t // tt,),
        in_specs=[_rows(tt, DFF), _rows(tt, D), _rows(tt, D), _full((8, D)), _full((DFF, D))],
        out_specs=[_rows(tt, D), _rows(tt, D), _full((8, D))],
        out_shape=[_sds((t, D)), _sds((t, D), MXU_DTYPE), _sds((8, D))],
        compiler_params=_cparams(1),
    )(hid, x1, tgt, vec, w_down)


def ffn_dhid(dff, s, d, w_down, tt):
    t = dff.shape[0]

    def body(dff_ref, s_ref, d_ref, w_ref, dgt_ref, dup_ref):
        dhid = _mm_nt(dff_ref[...], w_ref[...])
        dgt_ref[...] = (dhid * d_ref[...].astype(F32)).astype(MXU_DTYPE)
        dup_ref[...] = (dhid * s_ref[...].astype(F32)).astype(MXU_DTYPE)

    blk = pl.BlockSpec((tt, FN), lambda j, i: (i, j))
    return pl.pallas_call(
        body, name="ffn_dhid", grid=(2, t // tt),
        in_specs=[pl.BlockSpec((tt, D), lambda j, i: (i, 0)), blk, blk, pl.BlockSpec((FN, D), lambda j, i: (j, 0))],
        out_specs=[blk, blk],
        out_shape=[_sds((t, DFF), MXU_DTYPE), _sds((t, DFF), MXU_DTYPE)],
        compiler_params=_cparams(2),
    )(dff, s, d, w_down)


def ffn_dh2(dgt, dup, x1, dx2, vec, wg_t, wu_t, tt):
    t = x1.shape[0]

    def body(dgt_ref, dup_ref, x1_ref, dx2_ref, vec_ref, wg_ref, wu_ref, dx1_ref, acc_ref):
        _zero_first(acc_ref)
        dh2 = _mm(dgt_ref[...], wg_ref[...]) + _mm(dup_ref[...], wu_ref[...])
        g, sc = vec_ref[0:1, :], vec_ref[2:3, :]
        r, xn, yy, _ = _norm_mod(x1_ref[...], g, vec_ref[1:2, :], sc)
        dsh, dsc, dgn, dx = _norm_mod_bwd(dh2, r, xn, yy, g, sc)
        dx1_ref[...] = dx2_ref[...] + dx
        _acc_rows(acc_ref, [dsh, dsc, dgn])

    return pl.pallas_call(
        body, name="ffn_dh2", grid=(t // tt,),
        in_specs=[_rows(tt, DFF), _rows(tt, DFF), _rows(tt, D), _rows(tt, D), _full((8, D)), _full((DFF, D)),
                  _full((DFF, D))],
        out_specs=[_rows(tt, D), _full((8, D))],
        out_shape=[_sds((t, D)), _sds((8, D))],
        compiler_params=_cparams(1),
    )(dgt, dup, x1, dx2, vec, wg_t, wu_t)


def tn_matmul(a, b, bm, bt, init=None):
    t, m = a.shape
    n = b.shape[1]
    nk = t // bt

    def body(*refs):
        if init is None:
            a_ref, b_ref, o_ref, wire_ref = refs
        else:
            a_ref, b_ref, i_ref, o_ref, wire_ref = refs
        @pl.when(pl.program_id(1) == 0)
        def _():
            o_ref[...] = jnp.zeros_like(o_ref) if init is None else i_ref[...]

        o_ref[...] += _mm_tn(a_ref[...], b_ref[...])

        @pl.when(pl.program_id(1) == nk - 1)
        def _():
            wire_ref[...] = o_ref[...].astype(WIRE_DTYPE)

    in_specs = [pl.BlockSpec((bt, bm), lambda i, k: (k, i)), pl.BlockSpec((bt, n), lambda i, k: (k, 0))]
    args = [a, b]
    if init is not None:
        in_specs.append(pl.BlockSpec((bm, n), lambda i, k: (i, 0)))
        args.append(init)
    oblk = pl.BlockSpec((bm, n), lambda i, k: (i, 0))
    return pl.pallas_call(
        body, name="tn_matmul", grid=(m // bm, nk),
        in_specs=in_specs, out_specs=[oblk, oblk],
        out_shape=[_sds((m, n)), _sds((m, n), WIRE_DTYPE)], compiler_params=_cparams(2),
    )(*args)


def _adamw(w, g, m, v):
    m = ADAM_B1 * m + (1.0 - ADAM_B1) * g
    v = ADAM_B2 * v + (1.0 - ADAM_B2) * (g * g)
    m_hat = m / (1.0 - ADAM_B1 ** ADAM_STEP)
    v_hat = v / (1.0 - ADAM_B2 ** ADAM_STEP)
    delta = -ADAM_LR * (m_hat / (jnp.sqrt(v_hat) + ADAM_EPS) + ADAM_WD * w)
    return delta, m, v


def adamw_sharded(own, recv, w, m, v):
    shape = w.shape

    def body(own_ref, recv_ref, w_ref, m_ref, v_ref, g_ref, d_ref, mo_ref, vo_ref):
        g = own_ref[...]
        for k in range(N_DEV - 1):
            g = g + recv_ref[k].astype(F32)
        g_ref[...] = g
        d_ref[...], mo_ref[...], vo_ref[...] = _adamw(w_ref[...], g, m_ref[...], v_ref[...])

    return pl.pallas_call(
        body, name="adamw_sharded",
        in_specs=[_full(shape), _full((N_DEV - 1,) + shape), _full(shape), _full(shape), _full(shape)],
        out_specs=[_full(shape)] * 4, out_shape=[_sds(shape)] * 4, grid=(1,),
        compiler_params=_cparams(1),
    )(own, recv, w, m, v)


def adamw_small(items):
    n = len(items)
    flat = [a for it in items for a in it]

    def body(*refs):
        ins, outs = refs[:4 * n], refs[4 * n:]
        for i in range(n):
            g, w, m, v = (r[...] for r in ins[4 * i:4 * i + 4])
            outs[3 * i][...], outs[3 * i + 1][...], outs[3 * i + 2][...] = _adamw(w, g, m, v)

    out = pl.pallas_call(
        body, name="adamw_small", grid=(1,),
        in_specs=[_full(a.shape) for a in flat],
        out_specs=[_full(it[1].shape) for it in items for _ in range(3)],
        out_shape=[_sds(it[1].shape) for it in items for _ in range(3)],
        compiler_params=_cparams(1),
    )(*flat)
    return [tuple(out[3 * i:3 * i + 3]) for i in range(n)]


def _mesh_pos():
    x, y, c = lax.axis_index("x"), lax.axis_index("y"), lax.axis_index("c")
    me = 4 * x + 2 * y + c
    peers = []
    for k in range(1, N_DEV):
        peers.append(((1 - x) if (k >> 2) & 1 else x, (1 - y) if (k >> 1) & 1 else y, (1 - c) if k & 1 else c))
    return me, peers


def _all_gather_issue(buf, send_sems, recv_sems, me, peers):
    sends = []
    for k, peer in enumerate(peers):
        cp = pltpu.make_async_remote_copy(src_ref=buf.at[me], dst_ref=buf.at[me], send_sem=send_sems.at[k],
                                          recv_sem=recv_sems.at[k], device_id=peer, device_id_type=MESH)
        cp.start()
        sends.append(cp)
    return sends


def _all_gather_finish(buf, send_sems, recv_sems, me, peers, sends):
    for k, peer in enumerate(peers):
        src = jnp.bitwise_xor(me, k + 1)
        pltpu.make_async_remote_copy(src_ref=buf.at[src], dst_ref=buf.at[src], send_sem=send_sems.at[k],
                                     recv_sem=recv_sems.at[k], device_id=peer, device_id_type=MESH).wait_recv()
    for cp in sends:
        cp.wait_send()


def _all_gather(buf, send_sems, recv_sems, me, peers):
    _all_gather_finish(buf, send_sems, recv_sems, me, peers, _all_gather_issue(buf, send_sems, recv_sems, me, peers))


_VMEM = pl.BlockSpec(memory_space=pltpu.VMEM)
_ANY = pl.BlockSpec(memory_space=pl.ANY)
_SEMS = pltpu.SemaphoreType.DMA((N_DEV - 1,))


def mod_forward(c, c_ctx, w_mod_sh, b_mod, sm_pack):
    ncol = w_mod_sh.shape[1]

    def body(c_ref, cc_ref, w_ref, b_ref, sm_ref, mod_ref, s_ref, smt_ref, cbuf, pbuf, smbuf, s1, r1, s2, r2, s3, r3):
        me, peers = _mesh_pos()
        smbuf[me] = sm_ref[...]
        sm_sends = _all_gather_issue(smbuf, s3, r3, me, peers)
        cbuf[me] = jnp.broadcast_to(c_ref[...], (8, D))
        _all_gather(cbuf, s1, r1, me, peers)
        rows = [cbuf[j, 0:1, :] for j in range(N_DEV)] + [cc_ref[...], jnp.zeros((7, D), F32)]
        sx = jnp.concatenate(rows, axis=0)
        s = sx * _sigmoid(sx)
        s_ref[...] = s
        pbuf[me] = _hi(s, w_ref[...])
        _all_gather(pbuf, s2, r2, me, peers)
        for j in range(N_DEV):
            mod_ref[:, j * ncol:(j + 1) * ncol] = pbuf[j] + b_ref[:, j * ncol:(j + 1) * ncol]
        _all_gather_finish(smbuf, s3, r3, me, peers, sm_sends)
        tot = smbuf[0]
        for j in range(1, N_DEV):
            tot = tot + smbuf[j]
        smt_ref[...] = tot

    return pl.pallas_call(
        body, name="mod_forward",
        in_specs=[_VMEM] * 5, out_specs=[_VMEM] * 3,
        out_shape=[_sds((16, N_DEV * ncol)), _sds((16, D)), _sds(sm_pack.shape)],
        scratch_shapes=[pltpu.VMEM((N_DEV, 8, D), F32), pltpu.VMEM((N_DEV, 16, ncol), F32),
                        pltpu.VMEM((N_DEV,) + sm_pack.shape, F32), _SEMS, _SEMS, _SEMS, _SEMS, _SEMS, _SEMS],
        compiler_params=pltpu.CompilerParams(vmem_limit_bytes=VMEM_LIMIT),
    )(c, c_ctx, w_mod_sh, b_mod, sm_pack)


def sum_blocks(gat):
    def body(g_ref, tot_ref):
        tot = g_ref[0]
        for j in range(1, N_DEV):
            tot = tot + g_ref[j]
        tot_ref[...] = tot

    return pl.pallas_call(
        body, name="sum_blocks", in_specs=[_VMEM], out_specs=_VMEM, out_shape=_sds(gat.shape[1:]),
        compiler_params=pltpu.CompilerParams(vmem_limit_bytes=VMEM_LIMIT),
    )(gat)


def mod_backward(s, dm_sh, w, m, v, cc, m_cc, v_cc):
    shape = w.shape

    def body(s_ref, dm_ref, w_ref, m_ref, v_ref, cc_ref, mcc_ref, vcc_ref,
             gw_ref, dw_ref, mw_ref, vw_ref, gc_ref, dc_ref, mc_ref, vc_ref, pbuf, send_sems, recv_sems):
        me, peers = _mesh_pos()
        wv = w_ref[...]
        pbuf[me] = _hi_nt(dm_ref[8:16, :], wv)
        _all_gather(pbuf, send_sems, recv_sems, me, peers)
        g = _hi_tn(s_ref[...], dm_ref[...])
        gw_ref[...] = g
        dw_ref[...], mw_ref[...], vw_ref[...] = _adamw(wv, g, m_ref[...], v_ref[...])
        tot = pbuf[0]
        for j in range(1, N_DEV):
            tot = tot + pbuf[j]
        ccv = cc_ref[...]
        sg = _sigmoid(ccv)
        gc = tot[0:1, :] * (sg * (1.0 + ccv * (1.0 - sg)))
        gc_ref[...] = gc
        dc_ref[...], mc_ref[...], vc_ref[...] = _adamw(ccv, gc, mcc_ref[...], vcc_ref[...])

    return pl.pallas_call(
        body, name="mod_backward", in_specs=[_VMEM] * 8, out_specs=[_VMEM] * 8,
        out_shape=[_sds(shape)] * 4 + [_sds((1, D))] * 4,
        scratch_shapes=[pltpu.VMEM((N_DEV, 8, D), F32), _SEMS, _SEMS],
        compiler_params=pltpu.CompilerParams(vmem_limit_bytes=VMEM_LIMIT),
    )(s, dm_sh, w, m, v, cc, m_cc, v_cc)


_HBM = pl.BlockSpec(memory_space=pltpu.HBM)
_SEM = pl.BlockSpec(memory_space=pltpu.SEMAPHORE)
_EFFECT = pltpu.SideEffectType.DATAFLOW_SIDE_EFFECTING
_hbm = lambda a: pltpu.with_memory_space_constraint(a, pltpu.HBM)


def gather_start(shards, me, tag):
    n = len(shards)
    sems = pltpu.SemaphoreType.DMA((7 * n,))
    lands = [lax.dynamic_update_slice(lax.empty((N_DEV,) + s.shape, s.dtype), s[None], (me, 0, 0)) for s in shards]

    def body(*refs):
        s_refs, l_refs = refs[:n], refs[n:2 * n]
        send_sems, recv_sems = refs[2 * n], refs[2 * n + 1]
        token = refs[-1]
        my, peers = _mesh_pos()
        for w in range(n):
            for k, peer in enumerate(peers):
                pltpu.make_async_remote_copy(
                    src_ref=s_refs[w], dst_ref=l_refs[w].at[my], send_sem=send_sems.at[w * 7 + k],
                    recv_sem=recv_sems.at[w * 7 + k], device_id=peer, device_id_type=MESH).start()
        token[...] = jnp.zeros_like(token)

    out = pl.pallas_call(
        body, name="gather_start_" + tag,
        out_shape=(sems, sems) + tuple(pltpu.HBM(a.shape, a.dtype) for a in list(shards) + lands) + (_sds((8, 128)),),
        in_specs=(_HBM,) * (2 * n), out_specs=(_SEM, _SEM) + (_HBM,) * (2 * n) + (_VMEM,),
        input_output_aliases={i: i + 2 for i in range(2 * n)},
        compiler_params=pltpu.CompilerParams(has_side_effects=_EFFECT),
    )(*[_hbm(a) for a in list(shards) + lands])
    return out[0], out[1], list(out[2:2 + n]), list(out[2 + n:2 + 2 * n]), out[-1]


def gather_wait(send_sems, recv_sems, shards, lands, after, tag):
    n = len(shards)

    def body(*refs):
        s_refs, l_refs = refs[:n], refs[n:2 * n]
        send_sems, recv_sems = refs[2 * n], refs[2 * n + 1]
        my, peers = _mesh_pos()
        for w in range(n):
            for k, peer in enumerate(peers):
                src = jnp.bitwise_xor(my, k + 1)
                cp = pltpu.make_async_remote_copy(
                    src_ref=s_refs[w], dst_ref=l_refs[w].at[src], send_sem=send_sems.at[w * 7 + k],
                    recv_sem=recv_sems.at[w * 7 + k], device_id=peer, device_id_type=MESH)
                cp.wait_send()
                cp.wait_recv()

    out = pl.pallas_call(
        body, name="gather_wait_" + tag,
        out_shape=tuple(pltpu.HBM(a.shape, a.dtype) for a in list(shards) + list(lands)),
        in_specs=(_HBM,) * (2 * n) + (_SEM, _SEM, _ANY), out_specs=(_HBM,) * (2 * n),
        input_output_aliases={i: i for i in range(2 * n)},
        compiler_params=pltpu.CompilerParams(has_side_effects=_EFFECT),
    )(*shards, *lands, send_sems, recv_sems, after)
    return list(out[n:2 * n])


def scatter_start(grads, tag):
    n = len(grads)
    sems = pltpu.SemaphoreType.DMA((7 * n,))
    lands = [lax.empty((N_DEV - 1,) + g.shape[1:], g.dtype) for g in grads]

    def body(*refs):
        g_refs, l_refs = refs[:n], refs[n:2 * n]
        send_sems, recv_sems = refs[2 * n], refs[2 * n + 1]
        token = refs[-1]
        me, peers = _mesh_pos()
        for w in range(n):
            for k, peer in enumerate(peers):
                dst = jnp.bitwise_xor(me, k + 1)
                pltpu.make_async_remote_copy(
                    src_ref=g_refs[w].at[dst], dst_ref=l_refs[w].at[k], send_sem=send_sems.at[w * 7 + k],
                    recv_sem=recv_sems.at[w * 7 + k], device_id=peer, device_id_type=MESH).start()
        token[...] = jnp.zeros_like(token)

    out = pl.pallas_call(
        body, name="scatter_start_" + tag,
        out_shape=(sems, sems) + tuple(pltpu.HBM(a.shape, a.dtype) for a in list(grads) + lands) + (_sds((8, 128)),),
        in_specs=(_HBM,) * (2 * n), out_specs=(_SEM, _SEM) + (_HBM,) * (2 * n) + (_VMEM,),
        input_output_aliases={i: i + 2 for i in range(2 * n)},
        compiler_params=pltpu.CompilerParams(has_side_effects=_EFFECT),
    )(*[_hbm(a) for a in list(grads) + lands])
    return out[0], out[1], list(out[2:2 + n]), list(out[2 + n:2 + 2 * n]), out[-1]


def scatter_wait(send_sems, recv_sems, grads, lands, after, tag):
    n = len(grads)

    def body(*refs):
        g_refs, l_refs = refs[:n], refs[n:2 * n]
        send_sems, recv_sems = refs[2 * n], refs[2 * n + 1]
        me, peers = _mesh_pos()
        for w in range(n):
            for k, peer in enumerate(peers):
                dst = jnp.bitwise_xor(me, k + 1)
                cp = pltpu.make_async_remote_copy(
                    src_ref=g_refs[w].at[dst], dst_ref=l_refs[w].at[k], send_sem=send_sems.at[w * 7 + k],
                    recv_sem=recv_sems.at[w * 7 + k], device_id=peer, device_id_type=MESH)
                cp.wait_send()
                cp.wait_recv()

    out = pl.pallas_call(
        body, name="scatter_wait_" + tag,
        out_shape=tuple(pltpu.HBM(a.shape, a.dtype) for a in list(grads) + list(lands)),
        in_specs=(_HBM,) * (2 * n) + (_SEM, _SEM, _ANY), out_specs=(_HBM,) * (2 * n),
        input_output_aliases={i: i for i in range(2 * n)},
        compiler_params=pltpu.CompilerParams(has_side_effects=_EFFECT),
    )(*grads, *lands, send_sems, recv_sems, after)
    return list(out[n:2 * n])


def _vec8(rows, width):
    rid = lax.broadcasted_iota(jnp.int32, (8, width), 0)
    out = jnp.zeros((8, width), F32)
    for i, r in enumerate(rows):
        r = r.reshape(-1)
        r = jnp.pad(r, (0, width - r.shape[0]))
        out = jnp.where(rid == i, r[None, :], out)
    return out


def local_step(x, ctx, tgt, mod, mod_c, small, w_int, start, late_weights, grads_ready, small_ready, tt, tt_ctx, cb,
               cb_ctx):
    sh1, sc1, g1, sh2, sc2, g2 = [mod[i * D:(i + 1) * D] for i in range(6)]
    csh1, csc1 = mod_c[0:D], mod_c[D:2 * D]
    vec1 = _vec8([small["norm1_g"], sh1, sc1], D)
    vec1c = _vec8([small["norm1_g"], csh1, csc1], D)
    vec2 = _vec8([small["norm2_g"], sh2, sc2], D)
    vec3 = _vec8([g2, small["final_g"]], D)
    vecm = _vec8([g1, small["norm2_g"], sh2, sc2], D)
    vcm = _vec8([jnp.tile(small["gla_norm_g"].reshape(HV), NH), small["conv_b"], small["conv_ln_g"],
                 small["conv_ln_b"]], DC)
    convw = jnp.pad(small["conv_w"], ((0, 1), (0, 0)))
    wa = jnp.zeros((128, 512), F32)
    wa = wa.at[0:RANK, 0:DK].set(small["w_a2_f"]).at[RANK:2 * RANK, DK:2 * DK].set(small["w_a2_b"])
    ba = jnp.concatenate([small["b_a_f"].reshape(1, DK), small["b_a_b"].reshape(1, DK)], axis=1)

    _, _, kc, vc_, _, rc, lac, hc = proj_fwd(ctx, vec1c + start, w_int, wa, ba, tt_ctx)
    qc0 = jnp.zeros_like(kc)
    _, _, sallf_c, sallb_c, sfin_c = gla_fwd(qc0, kc, vc_, lac, jnp.zeros((2, HV, DK), F32), cb_ctx)
    u, q, k, v, g, r, la, h = proj_fwd(x, vec1, w_int, wa, ba, tt)
    o_f, o_b, sall_f, sall_b, _ = gla_fwd(q, k, v, la, sfin_c, cb)
    w_out, wg_t, wu_t, w_down = late_weights(o_b)
    x1, cat, mix, yc, h2 = merge_fwd(u, g, o_f, o_b, x, vecm, vcm, convw, w_out, tt)
    tt2 = min(2 * tt, x.shape[0])
    act, dact, hid = ffn_gate_up(h2, wg_t, wu_t, tt2)
    dx2, dff, acc3 = ffn_down_loss(hid, x1, tgt, vec3, w_down, tt)
    dgt, dup = ffn_dhid(dff, act, dact, w_down, tt2)
    dx1, acc2 = ffn_dh2(dgt, dup, x1, dx2, vec2, wg_t, wu_t, tt)
    bt = min(2048, x.shape[0])
    gw = {"w_down": tn_matmul(hid, dff, FN, bt), "wg_t": tn_matmul(dgt, h2, FN, bt),
          "wu_t": tn_matmul(dup, h2, FN, bt)}
    vecm = vecm + grads_ready(("wg_t", "wu_t", "w_down"), gw)
    du, dg, do, dmix, accm1, accm2, dconvw = merge_bwd(dx1, mix, u, g, o_f, o_b, yc, vecm, vcm, convw, w_out, tt)
    gw["w_out"] = tn_matmul(cat, dmix, 512, bt)
    dsfin = jnp.zeros((2, HV, DK), F32) + grads_ready(("w_out",), gw)
    dqf, dkf, dvf, dlaf, dqb, dkb, dvb, dlab, ds0 = gla_bwd(q, k, v, la, do, sall_f, sall_b, dsfin, cb)
    gx, dp, acc1, dba, dwa = proj_bwd(du, (dqf, dqb, dkf, dkb, dvf, dvb), dg, dlaf, dlab, la, r, x, dx1, vec1, w_int,
                                      wa, tt)
    tcx = ctx.shape[0]
    zc = lambda w, dt=MXU_DTYPE: jnp.zeros((tcx, w), dt)
    _, dkf, dvf, dlaf, _, dkb, dvb, dlab, _ = gla_bwd(qc0, kc, vc_, lac, zc(DV), sallf_c, sallb_c, ds0, cb_ctx)
    _, dpc, acc1c, dbac, dwac = proj_bwd(zc(1024), (zc(DK), zc(DK), dkf, dkb, dvf, dvb), zc(DV), dlaf, dlab, lac, rc,
                                         ctx, zc(D, F32), vec1c, w_int, wa, tt_ctx)
    dwa_t = dwa + dwac
    dba_t = dba + dbac
    gs = {
        "norm1_g": acc1[2] + acc1c[2], "norm2_g": acc2[2], "final_g": acc3[1], "loss": acc3[2],
        "gla_norm_g": accm2[0], "conv_b": accm2[1], "conv_ln_g": accm2[2], "conv_ln_b": accm2[3],
        "conv_w": dconvw, "b_a": dba_t[0], "w_a2": dwa_t,
    }
    dmod = _vec8([acc1[0], acc1[1], accm1[0], acc2[0], acc2[1], acc3[0]], D)
    dmod_c = _vec8([acc1c[0], acc1c[1]], D)
    dpc = dpc + small_ready(gs, dmod, dmod_c).astype(dpc.dtype)
    btc = min(1024, tcx)
    gw["w_int"] = tn_matmul(dp, h, 896, bt, init=tn_matmul(dpc, hc, 896, btc)[0])
    grads_ready(("w_int",), gw)
    return gx, gw


PACK_ROWS = 96
ROW_N1, ROW_N2, ROW_FG, ROW_LOSS, ROW_GN, ROW_CB, ROW_LG, ROW_LB, ROW_BA = 0, 1, 2, 3, 4, 5, 6, 7, 8
ROW_DMOD, ROW_DMODC, ROW_CW, ROW_WA = 16, 24, 32, 64


def _pack_small(gs, dmod, dmod_c):
    pad = lambda a: jnp.pad(a, ((0, 0), (0, D - a.shape[1])))
    singles = _vec8([gs["norm1_g"], gs["norm2_g"], gs["final_g"], gs["loss"], gs["gla_norm_g"], gs["conv_b"],
                     gs["conv_ln_g"], gs["conv_ln_b"]], D)
    return jnp.concatenate([singles, _vec8([gs["b_a"]], D), dmod, dmod_c, pad(gs["conv_w"]), pad(gs["w_a2"][0:32])],
                           axis=0)


def kernel(x, c, ctx, c_ctx, w_mod, b_mod, norm1_g, norm2_g, w_in, conv_w, conv_b, conv_ln_g, conv_ln_b, w_a2_f, b_a_f, w_a2_b, b_a_b, gla_norm_g, w_out, w_gate, w_up, w_down, final_g, loss_target, m_c_ctx, m_w_mod, m_b_mod, m_norm1_g, m_norm2_g, m_w_in, m_conv_w, m_conv_b, m_conv_ln_g, m_conv_ln_b, m_w_a2_f, m_b_a_f, m_w_a2_b, m_b_a_b, m_gla_norm_g, m_w_out, m_w_gate, m_w_up, m_w_down, m_final_g, v_c_ctx, v_w_mod, v_b_mod, v_norm1_g, v_norm2_g, v_w_in, v_conv_w, v_conv_b, v_conv_ln_g, v_conv_ln_b, v_w_a2_f, v_b_a_f, v_w_a2_b, v_b_a_b, v_gla_norm_g, v_w_out, v_w_gate, v_w_up, v_w_down, v_final_g):
    me = 4 * lax.axis_index("x") + 2 * lax.axis_index("y") + lax.axis_index("c")
    t = x.shape[1]
    tcx = ctx.shape[1]
    r_in, r_out, r_ff = w_in.shape[2], w_out.shape[1], w_gate.shape[2]
    r_in_b = -(-r_in // 16) * 16

    tb = lambda w: w.T.astype(MXU_DTYPE)
    i_send, i_recv, i_thru, i_lands, i_token = gather_start([jnp.pad(tb(w_in[0]), ((0, r_in_b - r_in), (0, 0)))], me,
                                                            "w_in")

    small = dict(norm1_g=norm1_g[0], norm2_g=norm2_g[0], final_g=final_g, gla_norm_g=gla_norm_g[0],
                 conv_b=conv_b[0], conv_ln_g=conv_ln_g[0], conv_ln_b=conv_ln_b[0], b_a_f=b_a_f[0], b_a_b=b_a_b[0])
    sm_pack = jnp.zeros((48, DC), F32)
    sm_pack = lax.dynamic_update_slice(sm_pack, conv_w[0], (0, me * (DC // N_DEV)))
    sm_pack = lax.dynamic_update_slice(sm_pack, w_a2_f[0], (32, me * (DK // N_DEV)))
    sm_pack = lax.dynamic_update_slice(sm_pack, w_a2_b[0], (32, DK + me * (DK // N_DEV)))

    mod_all, s_all, sm_tot = mod_forward(c + i_token[0:1, 0:1], c_ctx.reshape(1, D), w_mod[0], b_mod, sm_pack)
    mod = lax.dynamic_slice(mod_all, (me, 0), (1, 6 * D)).reshape(6 * D)
    mod_c = mod_all[8]
    small["conv_w"] = sm_tot[0:CW, :]
    small["w_a2_f"] = sm_tot[32:32 + RANK, 0:DK]
    small["w_a2_b"] = sm_tot[32:32 + RANK, DK:2 * DK]

    wall = gather_wait(i_send, i_recv, i_thru, i_lands, sm_tot, "w_in")[0]
    w_int = jnp.pad(wall[:, 0:r_in, :].reshape(N_DEV * r_in, D), ((0, DINP - DIN), (0, 0)))
    after_w_in = (wall[0:1, 0:1, 0] * 0).astype(MXU_DTYPE)
    late = [w_out[0].astype(MXU_DTYPE) + after_w_in, tb(w_gate[0]) + after_w_in, tb(w_up[0]) + after_w_in,
            w_down[0].astype(MXU_DTYPE) + after_w_in]
    g_send, g_recv, late_thru, late_lands, g_token = gather_start(late, me, "late")

    def late_weights(after):
        got = gather_wait(g_send, g_recv, late_thru, late_lands, after, "late")
        return tuple(a.reshape(N_DEV * a.shape[1], D) for a in got)

    pad_in = lambda g: jnp.pad(g[0:DIN].reshape(N_DEV, r_in, D), ((0, 0), (0, r_in_b - r_in), (0, 0)))
    blocked = {"w_int": pad_in, "w_out": lambda g: g.reshape(N_DEV, r_out, D)}
    as_blocks = lambda n, g: blocked.get(n, lambda a: a.reshape(N_DEV, r_ff, D))(g)
    pending = []

    def grads_ready(names, gw_now):
        blocks = [as_blocks(n, gw_now[n][1]) for n in names]
        if names[0] == "w_int":
            done = finish_small(gw_now["w_int"][0]).astype(WIRE_DTYPE)
            blocks = [b + done for b in blocks]
        send, recv_s, thru, zones, token = scatter_start(blocks, names[0])
        pending.append((names, send, recv_s, thru, zones))
        return token[0:1, 0:1]

    sm = {}

    def small_ready(gs, dmod, dmod_c):
        sm["copy"] = gather_start([_pack_small(gs, dmod, dmod_c)], me, "small")
        return sm["copy"][4][0:1, 0:1]

    def finish_small(after):
        send, recv_s, thru, zones, _ = sm["copy"]
        gat = gather_wait(send, recv_s, thru, zones, after, "small")[0]
        sm["tot"] = sum_blocks(gat)
        sm["dm"] = jnp.concatenate(
            [gat[:, ROW_DMOD:ROW_DMOD + 6, :].reshape(N_DEV, 6 * D),
             jnp.pad(sm["tot"][ROW_DMODC:ROW_DMODC + 6, :].reshape(1, 6 * D), ((0, 7), (0, 0)))], axis=0)
        ncol = w_mod.shape[2]
        dm_sh = lax.dynamic_slice(sm["dm"], (0, me * ncol), (16, ncol))
        sm["mod"] = mod_backward(s_all, dm_sh, w_mod[0], m_w_mod[0], v_w_mod[0], c_ctx.reshape(1, D),
                                 m_c_ctx.reshape(1, D), v_c_ctx.reshape(1, D))
        return sm["mod"][4][0:1, 0:1] * 0

    gx, gw = local_step(x[0], ctx[0], loss_target[0], mod, mod_c, small, w_int, g_token[0:1, 0:1], late_weights,
                        grads_ready, small_ready, 512, 256, 8, 4)
    tot, dm = sm["tot"], sm["dm"]
    loss = jnp.sum(tot[ROW_LOSS])
    g_wmod, d_wmod, nm_wmod, nv_wmod, g_cc, d_cc, nm_cc, nv_cc = sm["mod"]

    recv = {}

    def wait_for(entry, after):
        names, send, recv_s, thru, zones = entry
        recv.update(dict(zip(names, scatter_wait(send, recv_s, thru, zones, after, names[0]))))

    for entry in pending[:-1]:
        wait_for(entry, tot)
    own = {n: lax.dynamic_index_in_dim(as_blocks(n, gw[n][0]), me, 0, keepdims=False) for n in gw}
    padt = lambda w: jnp.pad(w.T, ((0, r_in_b - r_in), (0, 0)))
    big = {}
    big["w_gate"] = [a.T for a in adamw_sharded(own["wg_t"], recv["wg_t"], w_gate[0].T, m_w_gate[0].T,
                                                 v_w_gate[0].T)]
    big["w_up"] = [a.T for a in adamw_sharded(own["wu_t"], recv["wu_t"], w_up[0].T, m_w_up[0].T, v_w_up[0].T)]
    big["w_down"] = adamw_sharded(own["w_down"], recv["w_down"], w_down[0], m_w_down[0], v_w_down[0])
    big["w_out"] = adamw_sharded(own["w_out"], recv["w_out"], w_out[0], m_w_out[0], v_w_out[0])
    wait_for(pending[-1], big["w_out"][0])
    big["w_in"] = [a[0:r_in].T for a in adamw_sharded(own["w_int"], recv["w_int"], padt(w_in[0]), padt(m_w_in[0]),
                                                       padt(v_w_in[0]))]
    big["w_mod"] = [g_wmod, d_wmod, nm_wmod, nv_wmod]

    row = lambda r, w: tot[r:r + 1, 0:w]
    gn_row = tot[ROW_GN:ROW_GN + 1, 0:DC]
    g_small = {
        "b_mod": jnp.sum(dm, axis=0, keepdims=True),
        "norm1_g": row(ROW_N1, D), "norm2_g": row(ROW_N2, D),
        "conv_w": lax.dynamic_slice(tot, (ROW_CW, me * (DC // N_DEV)), (CW, DC // N_DEV)),
        "conv_b": row(ROW_CB, DC), "conv_ln_g": row(ROW_LG, DC), "conv_ln_b": row(ROW_LB, DC),
        "w_a2_f": lax.dynamic_slice(tot, (ROW_WA, me * (DK // N_DEV)), (RANK, DK // N_DEV)),
        "b_a_f": tot[ROW_BA:ROW_BA + 1, 0:DK],
        "w_a2_b": lax.dynamic_slice(tot, (ROW_WA + RANK, DK + me * (DK // N_DEV)), (RANK, DK // N_DEV)),
        "b_a_b": tot[ROW_BA:ROW_BA + 1, DK:2 * DK],
        "gla_norm_g": gn_row[:, 0:HV] + gn_row[:, HV:2 * HV] + gn_row[:, 2 * HV:3 * HV] + gn_row[:, 3 * HV:4 * HV],
        "final_g": row(ROW_FG, D),
    }
    wmv = {
        "b_mod": (b_mod, m_b_mod, v_b_mod), "norm1_g": (norm1_g, m_norm1_g, v_norm1_g),
        "norm2_g": (norm2_g, m_norm2_g, v_norm2_g), "conv_w": (conv_w[0], m_conv_w[0], v_conv_w[0]),
        "conv_b": (conv_b, m_conv_b, v_conv_b), "conv_ln_g": (conv_ln_g, m_conv_ln_g, v_conv_ln_g),
        "conv_ln_b": (conv_ln_b, m_conv_ln_b, v_conv_ln_b), "w_a2_f": (w_a2_f[0], m_w_a2_f[0], v_w_a2_f[0]),
        "b_a_f": (b_a_f, m_b_a_f, v_b_a_f), "w_a2_b": (w_a2_b[0], m_w_a2_b[0], v_w_a2_b[0]),
        "b_a_b": (b_a_b, m_b_a_b, v_b_a_b), "gla_norm_g": (gla_norm_g, m_gla_norm_g, v_gla_norm_g),
        "final_g": (final_g.reshape(1, D), m_final_g.reshape(1, D), v_final_g.reshape(1, D)),
    }
    names_small = list(g_small)
    upd = adamw_small([(g_small[n],) + wmv[n] for n in names_small])
    res = {n: (g_small[n],) + upd[i] for i, n in enumerate(names_small)}
    res["c_ctx"] = (g_cc, d_cc, nm_cc, nv_cc)
    for n in ("w_mod", "w_in", "w_out", "w_gate", "w_up", "w_down"):
        res[n] = tuple(big[n])

    order = ["c_ctx", "w_mod", "b_mod", "norm1_g", "norm2_g", "w_in", "conv_w", "conv_b", "conv_ln_g", "conv_ln_b",
             "w_a2_f", "b_a_f", "w_a2_b", "b_a_b", "gla_norm_g", "w_out", "w_gate", "w_up", "w_down", "final_g"]
    shapes = {"c_ctx": c_ctx.shape, "w_mod": w_mod.shape, "b_mod": b_mod.shape, "norm1_g": norm1_g.shape,
              "norm2_g": norm2_g.shape, "w_in": w_in.shape, "conv_w": conv_w.shape, "conv_b": conv_b.shape,
              "conv_ln_g": conv_ln_g.shape, "conv_ln_b": conv_ln_b.shape, "w_a2_f": w_a2_f.shape,
              "b_a_f": b_a_f.shape, "w_a2_b": w_a2_b.shape, "b_a_b": b_a_b.shape, "gla_norm_g": gla_norm_g.shape,
              "w_out": w_out.shape, "w_gate": w_gate.shape, "w_up": w_up.shape, "w_down": w_down.shape,
              "final_g": final_g.shape}
    outs = [loss, gx.reshape(x.shape)]
    for i in range(4):
        outs += [res[n][i].reshape(shapes[n]) for n in order]
    return tuple(outs)
```

```python
import functools

import jax
import jax.numpy as jnp
from jax import lax
from jax.experimental import pallas as pl
from jax.experimental.pallas import tpu as pltpu

F32 = jnp.float32
MXU_DTYPE = jnp.bfloat16
WIRE_DTYPE = jnp.bfloat16
HI = lax.Precision.HIGHEST
MESH = pl.DeviceIdType.MESH

N_DEV = 8
D = 1024
DC = 512
NH = 4
HK = 64
HV = 128
DK = NH * HK
DV = NH * HV
RANK = 16
CHUNK = 64
SEG = 64
CW = 31
CPAD = 15
DFF = 2816
DIN = 2592
DINP = 2688
TAU = 16.0
EPS = 1e-6
VMEM_LIMIT = 56 * 1024 * 1024

ADAM_LR = 0.001
ADAM_B1 = 0.9
ADAM_B2 = 0.999
ADAM_EPS = 1e-08
ADAM_WD = 0.01
ADAM_STEP = 10


def _mm(a, b):
    return jnp.dot(a.astype(MXU_DTYPE), b.astype(MXU_DTYPE), preferred_element_type=F32)


def _mm_nt(a, b):
    return lax.dot_general(a.astype(MXU_DTYPE), b.astype(MXU_DTYPE), (((1,), (1,)), ((), ())),
                           preferred_element_type=F32)


def _mm_tn(a, b):
    return lax.dot_general(a.astype(MXU_DTYPE), b.astype(MXU_DTYPE), (((0,), (0,)), ((), ())),
                           preferred_element_type=F32)


def _hi(a, b):
    return jnp.dot(a, b, precision=HI, preferred_element_type=F32)


def _hi_nt(a, b):
    return lax.dot_general(a, b, (((1,), (1,)), ((), ())), precision=HI, preferred_element_type=F32)


def _hi_tn(a, b):
    return lax.dot_general(a, b, (((0,), (0,)), ((), ())), precision=HI, preferred_element_type=F32)


def _sigmoid(x):
    return 1.0 / (1.0 + jnp.exp(-x))


def _cparams(n_axes):
    return pltpu.CompilerParams(dimension_semantics=("arbitrary",) * n_axes, vmem_limit_bytes=VMEM_LIMIT)


def _full(shape):
    n = len(shape)
    return pl.BlockSpec(shape, lambda *_: (0,) * n)


def _rows(tt, width):
    return pl.BlockSpec((tt, width), lambda i: (i, 0))


def _sds(shape, dtype=F32):
    return jax.ShapeDtypeStruct(shape, dtype)


def _norm_mod(x, g, sh, sc):
    r = lax.rsqrt(jnp.mean(x * x, axis=-1, keepdims=True) + EPS)
    xn = x * r
    yy = xn * g
    return r, xn, yy, yy * (1.0 + sc) + sh


def _norm_mod_bwd(dh, r, xn, yy, g, sc):
    dsh = jnp.sum(dh, axis=0, keepdims=True)
    dsc = jnp.sum(dh * yy, axis=0, keepdims=True)
    dy = dh * (1.0 + sc)
    dg = jnp.sum(dy * xn, axis=0, keepdims=True)
    dxn = dy * g
    dx = r * (dxn - xn * jnp.mean(dxn * xn, axis=-1, keepdims=True))
    return dsh, dsc, dg, dx


def _zero_first(*refs):
    @pl.when(pl.program_id(0) == 0)
    def _():
        for r in refs:
            r[...] = jnp.zeros_like(r)


def _acc_rows(ref, rows):
    ref[...] += jnp.concatenate(rows + [jnp.zeros((8 - len(rows), rows[0].shape[1]), F32)], axis=0)


def proj_fwd(x, vec, w_int, wa, ba, tt):
    t = x.shape[0]

    def body(x_ref, vec_ref, w_ref, wa_ref, ba_ref, u_ref, q_ref, k_ref, v_ref, g_ref, r_ref, la_ref, h_ref):
        _, _, _, h = _norm_mod(x_ref[...], vec_ref[0:1, :], vec_ref[1:2, :], vec_ref[2:3, :])
        hb = h.astype(MXU_DTYPE)
        h_ref[...] = hb
        p = _mm_nt(hb, w_ref[...])
        u_ref[...] = p[:, 0:1024]
        q_ref[...] = p[:, 1024:1280]
        k_ref[...] = p[:, 1280:1536]
        v_ref[...] = p[:, 1536:2048]
        g_ref[...] = p[:, 2048:2560]
        rr = p[:, 2560:2688]
        r_ref[...] = rr
        z = _mm(rr, wa_ref[...]) + ba_ref[...]
        la_ref[...] = (jnp.minimum(z, 0.0) - jnp.log(1.0 + jnp.exp(-jnp.abs(z)))) * (1.0 / TAU)

    return pl.pallas_call(
        body, name="proj_fwd", grid=(t // tt,),
        in_specs=[_rows(tt, D), _full((8, D)), _full((DINP, D)), _full((128, 512)), _full((1, 512))],
        out_specs=[_rows(tt, 1024), _rows(tt, DK), _rows(tt, DK), _rows(tt, DV), _rows(tt, DV), _rows(tt, 128),
                   _rows(tt, 512), _rows(tt, D)],
        out_shape=[_sds((t, 1024)), _sds((t, DK)), _sds((t, DK)), _sds((t, DV)), _sds((t, DV)), _sds((t, 128)),
                   _sds((t, 512)), _sds((t, D), MXU_DTYPE)],
        compiler_params=_cparams(1),
    )(x, vec, w_int, wa, ba)


def proj_bwd(du, dqkv, dg, dla_f, dla_b, la, r, x, dx1, vec, w_int, wa, tt):
    t = x.shape[0]

    def body(du_ref, dqf_ref, dqb_ref, dkf_ref, dkb_ref, dvf_ref, dvb_ref, dg_ref, dlaf_ref, dlab_ref, la_ref, r_ref,
             x_ref, dx1_ref, vec_ref, w_ref, wa_ref, gx_ref, dp_ref, acc_ref, dba_ref, dwa_ref):
        _zero_first(acc_ref, dba_ref, dwa_ref)
        dla = jnp.concatenate([dlaf_ref[...], dlab_ref[...]], axis=1)
        dz = dla * (1.0 - jnp.exp(TAU * la_ref[...])) * (1.0 / TAU)
        rr = r_ref[...]
        _acc_rows(dba_ref, [jnp.sum(dz, axis=0, keepdims=True)])
        dwa_ref[...] += _mm_tn(rr, dz)
        dr = _mm_nt(dz, wa_ref[...])
        md = lambda a: a.astype(MXU_DTYPE)
        both = lambda a_ref, b_ref: md(a_ref[...].astype(F32) + b_ref[...].astype(F32))
        dp = jnp.concatenate([du_ref[...], both(dqf_ref, dqb_ref), both(dkf_ref, dkb_ref), both(dvf_ref, dvb_ref),
                              dg_ref[...], md(dr)], axis=1)
        dp_ref[...] = dp
        dh = _mm(dp, w_ref[...])
        g, sc = vec_ref[0:1, :], vec_ref[2:3, :]
        rn, xn, yy, _ = _norm_mod(x_ref[...], g, vec_ref[1:2, :], sc)
        dsh, dsc, dgn, dx = _norm_mod_bwd(dh, rn, xn, yy, g, sc)
        gx_ref[...] = dx1_ref[...] + dx
        _acc_rows(acc_ref, [dsh, dsc, dgn])

    return pl.pallas_call(
        body, name="proj_bwd", grid=(t // tt,),
        in_specs=[_rows(tt, 1024), _rows(tt, DK), _rows(tt, DK), _rows(tt, DK), _rows(tt, DK), _rows(tt, DV),
                  _rows(tt, DV), _rows(tt, DV), _rows(tt, DK), _rows(tt, DK), _rows(tt, 512),
                  _rows(tt, 128), _rows(tt, D), _rows(tt, D), _full((8, D)), _full((DINP, D)), _full((128, 512))],
        out_specs=[_rows(tt, D), _rows(tt, DINP), _full((8, D)), _full((8, 512)), _full((128, 512))],
        out_shape=[_sds((t, D)), _sds((t, DINP), MXU_DTYPE), _sds((8, D)), _sds((8, 512)), _sds((128, 512))],
        compiler_params=_cparams(1),
    )(du, *dqkv, dg, dla_f, dla_b, la, r, x, dx1, vec, w_int, wa)


def _dot_exact01(m01, x):
    bf = jnp.bfloat16
    w = x.shape[1]
    hi = x.astype(bf)
    r1 = x - hi.astype(F32)
    mid = r1.astype(bf)
    lo = (r1 - mid.astype(F32)).astype(bf)
    y = jnp.dot(m01.astype(bf), jnp.concatenate([hi, mid, lo], axis=1), preferred_element_type=F32)
    return y[:, 0:w] + y[:, w:2 * w] + y[:, 2 * w:3 * w]


def _gla_chunk(d, qc, kc, la_c):
    row = lax.broadcasted_iota(jnp.int32, (CHUNK, CHUNK), 0)
    col = lax.broadcasted_iota(jnp.int32, (CHUNK, CHUNK), 1)
    cum = ((col <= row) if d == 0 else (col >= row)).astype(F32)
    cum_t = ((col >= row) if d == 0 else (col <= row)).astype(F32)
    cum4 = jnp.concatenate([cum] * NH, axis=0)
    head_of_lane = lax.broadcasted_iota(jnp.int32, (1, DK), 1) // HK
    b = _dot_exact01(cum, la_c)
    bl = jnp.sum(la_c, axis=0, keepdims=True)
    eb = jnp.exp(b)
    enb = jnp.exp(-b)
    ekd = jnp.exp(bl - b)
    qt = qc * (HK ** -0.5) * eb
    kt = kc * enb
    kd = kc * ekd
    qst = jnp.concatenate([jnp.where(head_of_lane == h, qt, 0.0) for h in range(NH)], axis=0)
    a = _mm_nt(qst, kt) * cum4
    return cum_t, cum4, head_of_lane, eb, enb, ekd, qt, kt, kd, qst, a, jnp.exp(bl)


def gla_fwd(q, k, v, la, s0, cb):
    t = q.shape[0]
    nc = t // CHUNK
    nb = nc // cb

    def body(qf_ref, kf_ref, vf_ref, laf_ref, qb_ref, kb_ref, vb_ref, lab_ref, s0_ref,
             of_ref, ob_ref, sf_ref, sb_ref, sfin_ref, s_scr):
        i = pl.program_id(0)

        @pl.when(i == 0)
        def _():
            s_scr[...] = s0_ref[...]

        def chunk(d, jj, q_ref, k_ref, v_ref, la_ref, o_ref, sall_ref):
            rows = slice(jj * CHUNK, (jj + 1) * CHUNK)
            vc = v_ref[rows, :]
            _, _, head_of_lane, _, _, _, _, _, kd, qst, a, dec = _gla_chunk(
                d, q_ref[rows, :], k_ref[rows, :], la_ref[rows, :])
            s = s_scr[d]
            sall_ref[jj] = s
            inter = _mm_nt(qst, s)
            outs = []
            for h in range(NH):
                hs = slice(h * CHUNK, (h + 1) * CHUNK)
                outs.append(_mm(a[hs], vc[:, h * HV:(h + 1) * HV]) + inter[hs])
            o_ref[rows, :] = jnp.concatenate(outs, axis=1)
            kv = _mm_tn(vc, kd)
            s_new = dec * s
            for h in range(NH):
                s_new = s_new + jnp.where(head_of_lane == h, kv[h * HV:(h + 1) * HV], 0.0)
            s_scr[d] = s_new

        for j in range(cb):
            chunk(0, j, qf_ref, kf_ref, vf_ref, laf_ref, of_ref, sf_ref)
            chunk(1, cb - 1 - j, qb_ref, kb_ref, vb_ref, lab_ref, ob_ref, sb_ref)

        @pl.when(i == nb - 1)
        def _():
            sfin_ref[...] = s_scr[...]

    tb = cb * CHUNK
    fwd = lambda w, c=0: pl.BlockSpec((tb, w), lambda i: (i, c))
    bwd = lambda w, c=0: pl.BlockSpec((tb, w), lambda i: (nb - 1 - i, c))
    return pl.pallas_call(
        body, name="gla_fwd", grid=(nb,),
        in_specs=[fwd(DK), fwd(DK), fwd(DV), fwd(DK, 0), bwd(DK), bwd(DK), bwd(DV), bwd(DK, 1), _full((2, HV, DK))],
        out_specs=[fwd(DV), bwd(DV), pl.BlockSpec((cb, HV, DK), lambda i: (i, 0, 0)),
                   pl.BlockSpec((cb, HV, DK), lambda i: (nb - 1 - i, 0, 0)), _full((2, HV, DK))],
        out_shape=[_sds((t, DV)), _sds((t, DV)), _sds((nc, HV, DK)), _sds((nc, HV, DK)), _sds((2, HV, DK))],
        scratch_shapes=[pltpu.VMEM((2, HV, DK), F32)],
        compiler_params=_cparams(1),
    )(q, k, v, la, q, k, v, la, s0)


def gla_bwd(q, k, v, la, do, sall_f, sall_b, dsfin, cb):
    t = q.shape[0]
    nc = t // CHUNK
    nb = nc // cb

    def body(qf_ref, kf_ref, vf_ref, laf_ref, dof_ref, sf_ref, qb_ref, kb_ref, vb_ref, lab_ref, dob_ref, sb_ref,
             dsfin_ref, dqf_ref, dkf_ref, dvf_ref, dlaf_ref, dqb_ref, dkb_ref, dvb_ref, dlab_ref, ds0_ref, ds_scr):
        i = pl.program_id(0)

        @pl.when(i == 0)
        def _():
            ds_scr[...] = dsfin_ref[...]

        def chunk(d, jj, q_ref, k_ref, v_ref, la_ref, do_ref, sall_ref, dq_ref, dk_ref, dv_ref, dla_ref):
            rows = slice(jj * CHUNK, (jj + 1) * CHUNK)
            vc = v_ref[rows, :]
            doc = do_ref[rows, :]
            cum_t, cum4, head_of_lane, eb, enb, ekd, qt, kt, kd, qst, a, dec = _gla_chunk(
                d, q_ref[rows, :], k_ref[rows, :], la_ref[rows, :])
            s = sall_ref[jj]
            ds = ds_scr[d]
            hv = lambda x, h: x[:, h * HV:(h + 1) * HV]
            hr = lambda x, h: x[h * CHUNK:(h + 1) * CHUNK]
            fold = lambda x: functools.reduce(
                lambda p, c: p + c, [jnp.where(head_of_lane == h, hr(x, h), 0.0) for h in range(NH)])
            dost = jnp.concatenate([hv(doc, h) for h in range(NH)], axis=0)
            vst = jnp.concatenate([hv(vc, h) for h in range(NH)], axis=0)
            da = jnp.concatenate([_mm_nt(hv(doc, h), hv(vc, h)) for h in range(NH)], axis=0) * cum4
            dqt = fold(_mm(da, kt) + _mm(dost, s))
            dkt = _mm_tn(da, qst)
            kdst = jnp.concatenate([jnp.where(head_of_lane == h, kd, 0.0) for h in range(NH)], axis=0)
            dv_inter = _mm_nt(kdst, ds)
            dv_ref[rows, :] = jnp.concatenate(
                [_mm_tn(hr(a, h), hv(doc, h)) + hr(dv_inter, h) for h in range(NH)], axis=1).astype(MXU_DTYPE)
            dkd = fold(_mm(vst, ds))
            ds_scr[d] = dec * ds + _mm_tn(dost, qst)
            tkd = dkd * kd
            db = dqt * qt - dkt * kt - tkd
            dbl = jnp.sum(ds * s, axis=0, keepdims=True) * dec + jnp.sum(tkd, axis=0, keepdims=True)
            dla_ref[rows, :] = _dot_exact01(cum_t, db) + dbl
            dq_ref[rows, :] = (dqt * eb * (HK ** -0.5)).astype(MXU_DTYPE)
            dk_ref[rows, :] = (dkt * enb + dkd * ekd).astype(MXU_DTYPE)

        for j in range(cb):
            chunk(0, cb - 1 - j, qf_ref, kf_ref, vf_ref, laf_ref, dof_ref, sf_ref, dqf_ref, dkf_ref, dvf_ref, dlaf_ref)
            chunk(1, j, qb_ref, kb_ref, vb_ref, lab_ref, dob_ref, sb_ref, dqb_ref, dkb_ref, dvb_ref, dlab_ref)

        @pl.when(i == nb - 1)
        def _():
            ds0_ref[...] = ds_scr[...]

    tb = cb * CHUNK
    rev = lambda w, c=0: pl.BlockSpec((tb, w), lambda i: (nb - 1 - i, c))
    fro = lambda w, c=0: pl.BlockSpec((tb, w), lambda i: (i, c))
    st_rev = pl.BlockSpec((cb, HV, DK), lambda i: (nb - 1 - i, 0, 0))
    st_fro = pl.BlockSpec((cb, HV, DK), lambda i: (i, 0, 0))
    md = MXU_DTYPE
    return pl.pallas_call(
        body, name="gla_bwd", grid=(nb,),
        in_specs=[rev(DK), rev(DK), rev(DV), rev(DK, 0), rev(DV), st_rev,
                  fro(DK), fro(DK), fro(DV), fro(DK, 1), fro(DV), st_fro, _full((2, HV, DK))],
        out_specs=[rev(DK), rev(DK), rev(DV), rev(DK), fro(DK), fro(DK), fro(DV), fro(DK), _full((2, HV, DK))],
        out_shape=[_sds((t, DK), md), _sds((t, DK), md), _sds((t, DV), md), _sds((t, DK)),
                   _sds((t, DK), md), _sds((t, DK), md), _sds((t, DV), md), _sds((t, DK)), _sds((2, HV, DK))],
        scratch_shapes=[pltpu.VMEM((2, HV, DK), F32)],
        compiler_params=_cparams(1),
    )(q, k, v, la, do, sall_f, q, k, v, la, do, sall_b, dsfin)


def _seg_pos(tt):
    return lax.broadcasted_iota(jnp.int32, (tt, 1), 0) % SEG


def _shifted(x, s, pos, tt):
    y = x if s == 0 else pltpu.roll(x, (-s) % tt, 0)
    return jnp.where((pos + s >= 0) & (pos + s < SEG), y, 0.0)


def _head_norm(o, gn):
    rs, xs = [], []
    for h in range(NH):
        oh = o[:, h * HV:(h + 1) * HV]
        r = lax.rsqrt(jnp.mean(oh * oh, axis=-1, keepdims=True) + EPS)
        rs.append(r)
        xs.append(oh * r)
    return rs, xs


def merge_fwd(u, g, o_f, o_b, x, vec, vc, convw, w_out, tt):
    t = x.shape[0]

    def body(u_ref, g_ref, of_ref, ob_ref, x_ref, vec_ref, vc_ref, cw_ref, w_ref, x1_ref, cat_ref, mix_ref, yc_ref,
             h2_ref):
        a = u_ref[:, 0:DC]
        gate = u_ref[:, DC:2 * DC]
        vv = a * _sigmoid(gate)
        pos = _seg_pos(tt)
        cw = cw_ref[...]
        yc = jnp.zeros((tt, DC), F32) + vc_ref[1:2, :]
        for j in range(CW):
            yc = yc + _shifted(vv, j - CPAD, pos, tt) * cw[j:j + 1, :]
        yc_ref[...] = yc
        mu = jnp.mean(yc, axis=-1, keepdims=True)
        yd = yc - mu
        rs = lax.rsqrt(jnp.mean(yd * yd, axis=-1, keepdims=True) + EPS)
        ln = yd * rs * vc_ref[2:3, :] + vc_ref[3:4, :]
        conv_o = ln * _sigmoid(ln)
        o = of_ref[...] + ob_ref[...]
        _, xs = _head_norm(o, None)
        gg = g_ref[...]
        o2g = jnp.concatenate(xs, axis=1) * vc_ref[0:1, :] * (gg * _sigmoid(gg))
        cat = jnp.concatenate([conv_o, o2g], axis=1).astype(MXU_DTYPE)
        cat_ref[...] = cat
        mix = _mm(cat, w_ref[...])
        mix_ref[...] = mix
        x1 = x_ref[...] + vec_ref[0:1, :] * mix
        x1_ref[...] = x1
        _, _, _, h2 = _norm_mod(x1, vec_ref[1:2, :], vec_ref[2:3, :], vec_ref[3:4, :])
        h2_ref[...] = h2.astype(MXU_DTYPE)

    return pl.pallas_call(
        body, name="merge_fwd", grid=(t // tt,),
        in_specs=[_rows(tt, 1024), _rows(tt, DV), _rows(tt, DV), _rows(tt, DV), _rows(tt, D),
                  _full((8, D)), _full((8, DC)), _full((32, DC)), _full((D, D))],
        out_specs=[_rows(tt, D), _rows(tt, D), _rows(tt, D), _rows(tt, DC), _rows(tt, D)],
        out_shape=[_sds((t, D)), _sds((t, D), MXU_DTYPE), _sds((t, D)), _sds((t, DC)), _sds((t, D), MXU_DTYPE)],
        compiler_params=_cparams(1),
    )(u, g, o_f, o_b, x, vec, vc, convw, w_out)


def merge_bwd(dx1, mix, u, g, o_f, o_b, yc, vec, vc, convw, w_out, tt):
    t = dx1.shape[0]

    def body(dx1_ref, mix_ref, u_ref, g_ref, of_ref, ob_ref, yc_ref, vec_ref, vc_ref, cw_ref, w_ref,
             du_ref, dg_ref, do_ref, dmix_ref, acc1_ref, acc2_ref, dcw_ref):
        _zero_first(acc1_ref, acc2_ref, dcw_ref)
        dx1v = dx1_ref[...]
        dg1 = jnp.sum(dx1v * mix_ref[...], axis=0, keepdims=True)
        dmix = (vec_ref[0:1, :] * dx1v).astype(MXU_DTYPE)
        dmix_ref[...] = dmix
        dcat = _mm_nt(dmix, w_ref[...])
        dconv_o = dcat[:, 0:DC]
        do2 = dcat[:, DC:2 * DC]
        gn = vc_ref[0:1, :]
        o = of_ref[...] + ob_ref[...]
        rs, xs = _head_norm(o, None)
        xn = jnp.concatenate(xs, axis=1)
        gg = g_ref[...]
        sg = _sigmoid(gg)
        don = do2 * (gg * sg)
        dg_ref[...] = (do2 * (xn * gn) * (sg * (1.0 + gg * (1.0 - sg)))).astype(MXU_DTYPE)
        dgn = jnp.sum(don * xn, axis=0, keepdims=True)
        dxn = don * gn
        dos = []
        for h in range(NH):
            dh = dxn[:, h * HV:(h + 1) * HV]
            dos.append(rs[h] * (dh - xs[h] * jnp.mean(dh * xs[h], axis=-1, keepdims=True)))
        do_ref[...] = jnp.concatenate(dos, axis=1).astype(MXU_DTYPE)
        yc = yc_ref[...]
        mu = jnp.mean(yc, axis=-1, keepdims=True)
        yd = yc - mu
        rstd = lax.rsqrt(jnp.mean(yd * yd, axis=-1, keepdims=True) + EPS)
        yhat = yd * rstd
        lg = vc_ref[2:3, :]
        ln = yhat * lg + vc_ref[3:4, :]
        sl = _sigmoid(ln)
        dln = dconv_o * (sl * (1.0 + ln * (1.0 - sl)))
        dlb = jnp.sum(dln, axis=0, keepdims=True)
        dlg = jnp.sum(dln * yhat, axis=0, keepdims=True)
        dyh = dln * lg
        dyc = rstd * (dyh - jnp.mean(dyh, axis=-1, keepdims=True)
                      - yhat * jnp.mean(dyh * yhat, axis=-1, keepdims=True))
        dcb = jnp.sum(dyc, axis=0, keepdims=True)
        a = u_ref[:, 0:DC]
        gate = u_ref[:, DC:2 * DC]
        sgt = _sigmoid(gate)
        vv = a * sgt
        pos = _seg_pos(tt)
        cw = cw_ref[...]
        dvv = jnp.zeros((tt, DC), F32)
        dws = []
        for j in range(CW):
            shifted_dyc = _shifted(dyc, CPAD - j, pos, tt)
            dvv = dvv + shifted_dyc * cw[j:j + 1, :]
            dws.append(jnp.sum(shifted_dyc * vv, axis=0, keepdims=True))
        dws.append(jnp.zeros((1, DC), F32))
        du_ref[:, 0:DC] = (dvv * sgt).astype(MXU_DTYPE)
        du_ref[:, DC:2 * DC] = (dvv * a * sgt * (1.0 - sgt)).astype(MXU_DTYPE)
        _acc_rows(acc1_ref, [dg1])
        _acc_rows(acc2_ref, [dgn, dcb, dlg, dlb])
        dcw_ref[...] += jnp.concatenate(dws, axis=0)

    return pl.pallas_call(
        body, name="merge_bwd", grid=(t // tt,),
        in_specs=[_rows(tt, D), _rows(tt, D), _rows(tt, 1024), _rows(tt, DV), _rows(tt, DV), _rows(tt, DV),
                  _rows(tt, DC),
                  _full((8, D)), _full((8, DC)), _full((32, DC)), _full((D, D))],
        out_specs=[_rows(tt, 1024), _rows(tt, DV), _rows(tt, DV), _rows(tt, D), _full((8, D)), _full((8, DC)),
                   _full((32, DC))],
        out_shape=[_sds((t, 1024), MXU_DTYPE), _sds((t, DV), MXU_DTYPE), _sds((t, DV), MXU_DTYPE),
                   _sds((t, D), MXU_DTYPE), _sds((8, D)),
                   _sds((8, DC)), _sds((32, DC))],
        compiler_params=_cparams(1),
    )(dx1, mix, u, g, o_f, o_b, yc, vec, vc, convw, w_out)


FN = DFF // 2


def ffn_gate_up(h2, wg_t, wu_t, tt):
    t = h2.shape[0]

    def body(h2_ref, wg_ref, wu_ref, s_ref, d_ref, hid_ref):
        h2v = h2_ref[...]
        gt = _mm_nt(h2v, wg_ref[...])
        up = _mm_nt(h2v, wu_ref[...])
        sg = _sigmoid(gt)
        act = gt * sg
        s_ref[...] = act.astype(MXU_DTYPE)
        d_ref[...] = (up * (sg * (1.0 + gt * (1.0 - sg)))).astype(MXU_DTYPE)
        hid_ref[...] = (act * up).astype(MXU_DTYPE)

    blk = pl.BlockSpec((tt, FN), lambda j, i: (i, j))
    wblk = pl.BlockSpec((FN, D), lambda j, i: (j, 0))
    return pl.pallas_call(
        body, name="ffn_gate_up", grid=(2, t // tt),
        in_specs=[pl.BlockSpec((tt, D), lambda j, i: (i, 0)), wblk, wblk],
        out_specs=[blk, blk, blk],
        out_shape=[_sds((t, DFF), MXU_DTYPE)] * 3,
        compiler_params=_cparams(2),
    )(h2, wg_t, wu_t)


def ffn_down_loss(hid, x1, tgt, vec, w_down, tt):
    t = x1.shape[0]

    def body(hid_ref, x1_ref, tgt_ref, vec_ref, w_ref, dx2_ref, dff_ref, acc_ref):
        _zero_first(acc_ref)
        g2 = vec_ref[0:1, :]
        fg = vec_ref[1:2, :]
        ff = _mm(hid_ref[...], w_ref[...])
        x2 = x1_ref[...] + g2 * ff
        rf = lax.rsqrt(jnp.mean(x2 * x2, axis=-1, keepdims=True) + EPS)
        xn = x2 * rf
        err = xn * fg - tgt_ref[...]
        dy = err * (1.0 / D)
        dfg = jnp.sum(dy * xn, axis=0, keepdims=True)
        dxn = dy * fg
        dx2 = rf * (dxn - xn * jnp.mean(dxn * xn, axis=-1, keepdims=True))
        dx2_ref[...] = dx2
        dff_ref[...] = (g2 * dx2).astype(MXU_DTYPE)
        dg2 = jnp.sum(dx2 * ff, axis=0, keepdims=True)
        loss = jnp.sum(err * err, axis=0, keepdims=True) * (0.5 / D)
        _acc_rows(acc_ref, [dg2, dfg, loss])

    return pl.pallas_call(
        body, name="ffn_down_loss", grid=(t // tt,),
        in_specs=[_rows(tt, DFF), _rows(tt, D), _rows(tt, D), _full((8, D)), _full((DFF, D))],
        out_specs=[_rows(tt, D), _rows(tt, D), _full((8, D))],
        out_shape=[_sds((t, D)), _sds((t, D), MXU_DTYPE), _sds((8, D))],
        compiler_params=_cparams(1),
    )(hid, x1, tgt, vec, w_down)


def ffn_dhid(dff, s, d, w_down, tt):
    t = dff.shape[0]

    def body(dff_ref, s_ref, d_ref, w_ref, dgt_ref, dup_ref):
        dhid = _mm_nt(dff_ref[...], w_ref[...])
        dgt_ref[...] = (dhid * d_ref[...].astype(F32)).astype(MXU_DTYPE)
        dup_ref[...] = (dhid * s_ref[...].astype(F32)).astype(MXU_DTYPE)

    blk = pl.BlockSpec((tt, FN), lambda j, i: (i, j))
    return pl.pallas_call(
        body, name="ffn_dhid", grid=(2, t // tt),
        in_specs=[pl.BlockSpec((tt, D), lambda j, i: (i, 0)), blk, blk, pl.BlockSpec((FN, D), lambda j, i: (j, 0))],
        out_specs=[blk, blk],
        out_shape=[_sds((t, DFF), MXU_DTYPE), _sds((t, DFF), MXU_DTYPE)],
        compiler_params=_cparams(2),
    )(dff, s, d, w_down)


def ffn_dh2(dgt, dup, x1, dx2, vec, wg_t, wu_t, tt):
    t = x1.shape[0]

    def body(dgt_ref, dup_ref, x1_ref, dx2_ref, vec_ref, wg_ref, wu_ref, dx1_ref, acc_ref):
        _zero_first(acc_ref)
        dh2 = _mm(dgt_ref[...], wg_ref[...]) + _mm(dup_ref[...], wu_ref[...])
        g, sc = vec_ref[0:1, :], vec_ref[2:3, :]
        r, xn, yy, _ = _norm_mod(x1_ref[...], g, vec_ref[1:2, :], sc)
        dsh, dsc, dgn, dx = _norm_mod_bwd(dh2, r, xn, yy, g, sc)
        dx1_ref[...] = dx2_ref[...] + dx
        _acc_rows(acc_ref, [dsh, dsc, dgn])

    return pl.pallas_call(
        body, name="ffn_dh2", grid=(t // tt,),
        in_specs=[_rows(tt, DFF), _rows(tt, DFF), _rows(tt, D), _rows(tt, D), _full((8, D)), _full((DFF, D)),
                  _full((DFF, D))],
        out_specs=[_rows(tt, D), _full((8, D))],
        out_shape=[_sds((t, D)), _sds((8, D))],
        compiler_params=_cparams(1),
    )(dgt, dup, x1, dx2, vec, wg_t, wu_t)


def tn_matmul(a, b, bm, bt, init=None):
    t, m = a.shape
    n = b.shape[1]
    nk = t // bt

    def body(*refs):
        if init is None:
            a_ref, b_ref, o_ref, wire_ref = refs
        else:
            a_ref, b_ref, i_ref, o_ref, wire_ref = refs
        @pl.when(pl.program_id(1) == 0)
        def _():
            o_ref[...] = jnp.zeros_like(o_ref) if init is None else i_ref[...]

        o_ref[...] += _mm_tn(a_ref[...], b_ref[...])

        @pl.when(pl.program_id(1) == nk - 1)
        def _():
            wire_ref[...] = o_ref[...].astype(WIRE_DTYPE)

    in_specs = [pl.BlockSpec((bt, bm), lambda i, k: (k, i)), pl.BlockSpec((bt, n), lambda i, k: (k, 0))]
    args = [a, b]
    if init is not None:
        in_specs.append(pl.BlockSpec((bm, n), lambda i, k: (i, 0)))
        args.append(init)
    oblk = pl.BlockSpec((bm, n), lambda i, k: (i, 0))
    return pl.pallas_call(
        body, name="tn_matmul", grid=(m // bm, nk),
        in_specs=in_specs, out_specs=[oblk, oblk],
        out_shape=[_sds((m, n)), _sds((m, n), WIRE_DTYPE)], compiler_params=_cparams(2),
    )(*args)


def _adamw(w, g, m, v):
    m = ADAM_B1 * m + (1.0 - ADAM_B1) * g
    v = ADAM_B2 * v + (1.0 - ADAM_B2) * (g * g)
    m_hat = m / (1.0 - ADAM_B1 ** ADAM_STEP)
    v_hat = v / (1.0 - ADAM_B2 ** ADAM_STEP)
    delta = -ADAM_LR * (m_hat / (jnp.sqrt(v_hat) + ADAM_EPS) + ADAM_WD * w)
    return delta, m, v


def adamw_sharded(own, recv, w, m, v):
    shape = w.shape

    def body(own_ref, recv_ref, w_ref, m_ref, v_ref, g_ref, d_ref, mo_ref, vo_ref):
        g = own_ref[...]
        for k in range(N_DEV - 1):
            g = g + recv_ref[k].astype(F32)
        g_ref[...] = g
        d_ref[...], mo_ref[...], vo_ref[...] = _adamw(w_ref[...], g, m_ref[...], v_ref[...])

    return pl.pallas_call(
        body, name="adamw_sharded",
        in_specs=[_full(shape), _full((N_DEV - 1,) + shape), _full(shape), _full(shape), _full(shape)],
        out_specs=[_full(shape)] * 4, out_shape=[_sds(shape)] * 4, grid=(1,),
        compiler_params=_cparams(1),
    )(own, recv, w, m, v)


def adamw_small(items):
    n = len(items)
    flat = [a for it in items for a in it]

    def body(*refs):
        ins, outs = refs[:4 * n], refs[4 * n:]
        for i in range(n):
            g, w, m, v = (r[...] for r in ins[4 * i:4 * i + 4])
            outs[3 * i][...], outs[3 * i + 1][...], outs[3 * i + 2][...] = _adamw(w, g, m, v)

    out = pl.pallas_call(
        body, name="adamw_small", grid=(1,),
        in_specs=[_full(a.shape) for a in flat],
        out_specs=[_full(it[1].shape) for it in items for _ in range(3)],
        out_shape=[_sds(it[1].shape) for it in items for _ in range(3)],
        compiler_params=_cparams(1),
    )(*flat)
    return [tuple(out[3 * i:3 * i + 3]) for i in range(n)]


def _mesh_pos():
    x, y, c = lax.axis_index("x"), lax.axis_index("y"), lax.axis_index("c")
    me = 4 * x + 2 * y + c
    peers = []
    for k in range(1, N_DEV):
        peers.append(((1 - x) if (k >> 2) & 1 else x, (1 - y) if (k >> 1) & 1 else y, (1 - c) if k & 1 else c))
    return me, peers


def _all_gather_issue(buf, send_sems, recv_sems, me, peers):
    sends = []
    for k, peer in enumerate(peers):
        cp = pltpu.make_async_remote_copy(src_ref=buf.at[me], dst_ref=buf.at[me], send_sem=send_sems.at[k],
                                          recv_sem=recv_sems.at[k], device_id=peer, device_id_type=MESH)
        cp.start()
        sends.append(cp)
    return sends


def _all_gather_finish(buf, send_sems, recv_sems, me, peers, sends):
    for k, peer in enumerate(peers):
        src = jnp.bitwise_xor(me, k + 1)
        pltpu.make_async_remote_copy(src_ref=buf.at[src], dst_ref=buf.at[src], send_sem=send_sems.at[k],
                                     recv_sem=recv_sems.at[k], device_id=peer, device_id_type=MESH).wait_recv()
    for cp in sends:
        cp.wait_send()


def _all_gather(buf, send_sems, recv_sems, me, peers):
    _all_gather_finish(buf, send_sems, recv_sems, me, peers, _all_gather_issue(buf, send_sems, recv_sems, me, peers))


_VMEM = pl.BlockSpec(memory_space=pltpu.VMEM)
_ANY = pl.BlockSpec(memory_space=pl.ANY)
_SEMS = pltpu.SemaphoreType.DMA((N_DEV - 1,))


def mod_forward(c, c_ctx, w_mod_sh, b_mod, sm_pack, w_first):
    ncol = w_mod_sh.shape[1]

    def body(c_ref, cc_ref, w_ref, b_ref, sm_ref, wf_ref, mod_ref, s_ref, smt_ref, wall_ref, cbuf, pbuf, smbuf,
             s1, r1, s2, r2, s3, r3, ws, wr, wl):
        me, peers = _mesh_pos()
        x, y, cc = lax.axis_index("x"), lax.axis_index("y"), lax.axis_index("c")
        sibling = (x, y, 1 - cc)
        chips = [(1 - x, y), (x, 1 - y), (1 - x, 1 - y)]
        slot = lambda px, py, pc: wall_ref.at[4 * px + 2 * py + pc]

        def wcopy(k, block, to, src=None):
            return pltpu.make_async_remote_copy(
                src_ref=slot(*block) if src is None else src, dst_ref=slot(*block), send_sem=ws.at[k],
                recv_sem=wr.at[k], device_id=to, device_id_type=MESH)

        mine = pltpu.make_async_copy(wf_ref, slot(x, y, cc), wl)
        mine.start()
        first = [wcopy(0, (x, y, cc), sibling, src=wf_ref)]
        first += [wcopy(1 + j, (x, y, cc), (*chip, cc), src=wf_ref) for j, chip in enumerate(chips)]
        for cp in first:
            cp.start()
        smbuf[me] = sm_ref[...]
        sm_sends = _all_gather_issue(smbuf, s3, r3, me, peers)
        cbuf[me] = jnp.broadcast_to(c_ref[...], (8, D))
        _all_gather(cbuf, s1, r1, me, peers)
        rows = [cbuf[j, 0:1, :] for j in range(N_DEV)] + [cc_ref[...], jnp.zeros((7, D), F32)]
        sx = jnp.concatenate(rows, axis=0)
        s = sx * _sigmoid(sx)
        s_ref[...] = s
        pbuf[me] = _hi(s, w_ref[...])
        _all_gather(pbuf, s2, r2, me, peers)
        for j in range(N_DEV):
            mod_ref[:, j * ncol:(j + 1) * ncol] = pbuf[j] + b_ref[:, j * ncol:(j + 1) * ncol]
        _all_gather_finish(smbuf, s3, r3, me, peers, sm_sends)
        tot = smbuf[0]
        for j in range(1, N_DEV):
            tot = tot + smbuf[j]
        smt_ref[...] = tot
        passed = [wcopy(4 + j, (*chip, cc), sibling) for j, chip in enumerate(chips)]
        for j, chip in enumerate(chips):
            wcopy(1 + j, (*chip, cc), (x, y, cc)).wait_recv()
            passed[j].start()
        wcopy(0, (x, y, 1 - cc), (x, y, cc)).wait_recv()
        for j, chip in enumerate(chips):
            wcopy(4 + j, (*chip, 1 - cc), (x, y, cc)).wait_recv()
        for cp in first + passed:
            cp.wait_send()
        mine.wait()

    return pl.pallas_call(
        body, name="mod_forward",
        in_specs=[_VMEM] * 6, out_specs=[_VMEM] * 4,
        out_shape=[_sds((16, N_DEV * ncol)), _sds((16, D)), _sds(sm_pack.shape),
                   _sds((N_DEV,) + w_first.shape, w_first.dtype)],
        scratch_shapes=[pltpu.VMEM((N_DEV, 8, D), F32), pltpu.VMEM((N_DEV, 16, ncol), F32),
                        pltpu.VMEM((N_DEV,) + sm_pack.shape, F32), _SEMS, _SEMS, _SEMS, _SEMS, _SEMS, _SEMS,
                        _SEMS, _SEMS, pltpu.SemaphoreType.DMA],
        compiler_params=pltpu.CompilerParams(vmem_limit_bytes=VMEM_LIMIT),
    )(c, c_ctx, w_mod_sh, b_mod, sm_pack, w_first)


def sum_blocks(gat):
    def body(g_ref, tot_ref):
        tot = g_ref[0]
        for j in range(1, N_DEV):
            tot = tot + g_ref[j]
        tot_ref[...] = tot

    return pl.pallas_call(
        body, name="sum_blocks", in_specs=[_VMEM], out_specs=_VMEM, out_shape=_sds(gat.shape[1:]),
        compiler_params=pltpu.CompilerParams(vmem_limit_bytes=VMEM_LIMIT),
    )(gat)


def mod_backward(s, dm_sh, w, m, v, cc, m_cc, v_cc):
    shape = w.shape

    def body(s_ref, dm_ref, w_ref, m_ref, v_ref, cc_ref, mcc_ref, vcc_ref,
             gw_ref, dw_ref, mw_ref, vw_ref, gc_ref, dc_ref, mc_ref, vc_ref, pbuf, send_sems, recv_sems):
        me, peers = _mesh_pos()
        wv = w_ref[...]
        pbuf[me] = _hi_nt(dm_ref[8:16, :], wv)
        _all_gather(pbuf, send_sems, recv_sems, me, peers)
        g = _hi_tn(s_ref[...], dm_ref[...])
        gw_ref[...] = g
        dw_ref[...], mw_ref[...], vw_ref[...] = _adamw(wv, g, m_ref[...], v_ref[...])
        tot = pbuf[0]
        for j in range(1, N_DEV):
            tot = tot + pbuf[j]
        ccv = cc_ref[...]
        sg = _sigmoid(ccv)
        gc = tot[0:1, :] * (sg * (1.0 + ccv * (1.0 - sg)))
        gc_ref[...] = gc
        dc_ref[...], mc_ref[...], vc_ref[...] = _adamw(ccv, gc, mcc_ref[...], vcc_ref[...])

    return pl.pallas_call(
        body, name="mod_backward", in_specs=[_VMEM] * 8, out_specs=[_VMEM] * 8,
        out_shape=[_sds(shape)] * 4 + [_sds((1, D))] * 4,
        scratch_shapes=[pltpu.VMEM((N_DEV, 8, D), F32), _SEMS, _SEMS],
        compiler_params=pltpu.CompilerParams(vmem_limit_bytes=VMEM_LIMIT),
    )(s, dm_sh, w, m, v, cc, m_cc, v_cc)


_HBM = pl.BlockSpec(memory_space=pltpu.HBM)
_SEM = pl.BlockSpec(memory_space=pltpu.SEMAPHORE)
_EFFECT = pltpu.SideEffectType.DATAFLOW_SIDE_EFFECTING
_hbm = lambda a: pltpu.with_memory_space_constraint(a, pltpu.HBM)


def gather_start(shards, me, tag):
    n = len(shards)
    sems = pltpu.SemaphoreType.DMA((7 * n,))
    lands = [lax.dynamic_update_slice(lax.empty((N_DEV,) + s.shape, s.dtype), s[None], (me, 0, 0)) for s in shards]

    def body(*refs):
        s_refs, l_refs = refs[:n], refs[n:2 * n]
        send_sems, recv_sems = refs[2 * n], refs[2 * n + 1]
        token = refs[-1]
        my, peers = _mesh_pos()
        for w in range(n):
            for k, peer in enumerate(peers):
                pltpu.make_async_remote_copy(
                    src_ref=s_refs[w], dst_ref=l_refs[w].at[my], send_sem=send_sems.at[w * 7 + k],
                    recv_sem=recv_sems.at[w * 7 + k], device_id=peer, device_id_type=MESH).start()
        token[...] = jnp.zeros_like(token)

    out = pl.pallas_call(
        body, name="gather_start_" + tag,
        out_shape=(sems, sems) + tuple(pltpu.HBM(a.shape, a.dtype) for a in list(shards) + lands) + (_sds((8, 128)),),
        in_specs=(_HBM,) * (2 * n), out_specs=(_SEM, _SEM) + (_HBM,) * (2 * n) + (_VMEM,),
        input_output_aliases={i: i + 2 for i in range(2 * n)},
        compiler_params=pltpu.CompilerParams(has_side_effects=_EFFECT),
    )(*[_hbm(a) for a in list(shards) + lands])
    return out[0], out[1], list(out[2:2 + n]), list(out[2 + n:2 + 2 * n]), out[-1]


def gather_wait(send_sems, recv_sems, shards, lands, after, tag):
    n = len(shards)

    def body(*refs):
        s_refs, l_refs = refs[:n], refs[n:2 * n]
        send_sems, recv_sems = refs[2 * n], refs[2 * n + 1]
        my, peers = _mesh_pos()
        for w in range(n):
            for k, peer in enumerate(peers):
                src = jnp.bitwise_xor(my, k + 1)
                cp = pltpu.make_async_remote_copy(
                    src_ref=s_refs[w], dst_ref=l_refs[w].at[src], send_sem=send_sems.at[w * 7 + k],
                    recv_sem=recv_sems.at[w * 7 + k], device_id=peer, device_id_type=MESH)
                cp.wait_send()
                cp.wait_recv()

    out = pl.pallas_call(
        body, name="gather_wait_" + tag,
        out_shape=tuple(pltpu.HBM(a.shape, a.dtype) for a in list(shards) + list(lands)),
        in_specs=(_HBM,) * (2 * n) + (_SEM, _SEM, _ANY), out_specs=(_HBM,) * (2 * n),
        input_output_aliases={i: i for i in range(2 * n)},
        compiler_params=pltpu.CompilerParams(has_side_effects=_EFFECT),
    )(*shards, *lands, send_sems, recv_sems, after)
    return list(out[n:2 * n])


def scatter_start(grads, tag):
    n = len(grads)
    sems = pltpu.SemaphoreType.DMA((7 * n,))
    lands = [lax.empty((N_DEV - 1,) + g.shape[1:], g.dtype) for g in grads]

    def body(*refs):
        g_refs, l_refs = refs[:n], refs[n:2 * n]
        send_sems, recv_sems = refs[2 * n], refs[2 * n + 1]
        token = refs[-1]
        me, peers = _mesh_pos()
        for w in range(n):
            for k, peer in enumerate(peers):
                dst = jnp.bitwise_xor(me, k + 1)
                pltpu.make_async_remote_copy(
                    src_ref=g_refs[w].at[dst], dst_ref=l_refs[w].at[k], send_sem=send_sems.at[w * 7 + k],
                    recv_sem=recv_sems.at[w * 7 + k], device_id=peer, device_id_type=MESH).start()
        token[...] = jnp.zeros_like(token)

    out = pl.pallas_call(
        body, name="scatter_start_" + tag,
        out_shape=(sems, sems) + tuple(pltpu.HBM(a.shape, a.dtype) for a in list(grads) + lands) + (_sds((8, 128)),),
        in_specs=(_HBM,) * (2 * n), out_specs=(_SEM, _SEM) + (_HBM,) * (2 * n) + (_VMEM,),
        input_output_aliases={i: i + 2 for i in range(2 * n)},
        compiler_params=pltpu.CompilerParams(has_side_effects=_EFFECT),
    )(*[_hbm(a) for a in list(grads) + lands])
    return out[0], out[1], list(out[2:2 + n]), list(out[2 + n:2 + 2 * n]), out[-1]


def scatter_wait(send_sems, recv_sems, grads, lands, after, tag):
    n = len(grads)

    def body(*refs):
        g_refs, l_refs = refs[:n], refs[n:2 * n]
        send_sems, recv_sems = refs[2 * n], refs[2 * n + 1]
        me, peers = _mesh_pos()
        for w in range(n):
            for k, peer in enumerate(peers):
                dst = jnp.bitwise_xor(me, k + 1)
                cp = pltpu.make_async_remote_copy(
                    src_ref=g_refs[w].at[dst], dst_ref=l_refs[w].at[k], send_sem=send_sems.at[w * 7 + k],
                    recv_sem=recv_sems.at[w * 7 + k], device_id=peer, device_id_type=MESH)
                cp.wait_send()
                cp.wait_recv()

    out = pl.pallas_call(
        body, name="scatter_wait_" + tag,
        out_shape=tuple(pltpu.HBM(a.shape, a.dtype) for a in list(grads) + list(lands)),
        in_specs=(_HBM,) * (2 * n) + (_SEM, _SEM, _ANY), out_specs=(_HBM,) * (2 * n),
        input_output_aliases={i: i for i in range(2 * n)},
        compiler_params=pltpu.CompilerParams(has_side_effects=_EFFECT),
    )(*grads, *lands, send_sems, recv_sems, after)
    return list(out[n:2 * n])


def _vec8(rows, width):
    rid = lax.broadcasted_iota(jnp.int32, (8, width), 0)
    out = jnp.zeros((8, width), F32)
    for i, r in enumerate(rows):
        r = r.reshape(-1)
        r = jnp.pad(r, (0, width - r.shape[0]))
        out = jnp.where(rid == i, r[None, :], out)
    return out


def local_step(x, ctx, tgt, mod, mod_c, small, w_int, start, late_weights, grads_ready, small_ready, tt, tt_ctx, cb,
               cb_ctx):
    sh1, sc1, g1, sh2, sc2, g2 = [mod[i * D:(i + 1) * D] for i in range(6)]
    csh1, csc1 = mod_c[0:D], mod_c[D:2 * D]
    vec1 = _vec8([small["norm1_g"], sh1, sc1], D)
    vec1c = _vec8([small["norm1_g"], csh1, csc1], D)
    vec2 = _vec8([small["norm2_g"], sh2, sc2], D)
    vec3 = _vec8([g2, small["final_g"]], D)
    vecm = _vec8([g1, small["norm2_g"], sh2, sc2], D)
    vcm = _vec8([jnp.tile(small["gla_norm_g"].reshape(HV), NH), small["conv_b"], small["conv_ln_g"],
                 small["conv_ln_b"]], DC)
    convw = jnp.pad(small["conv_w"], ((0, 1), (0, 0)))
    wa = jnp.zeros((128, 512), F32)
    wa = wa.at[0:RANK, 0:DK].set(small["w_a2_f"]).at[RANK:2 * RANK, DK:2 * DK].set(small["w_a2_b"])
    ba = jnp.concatenate([small["b_a_f"].reshape(1, DK), small["b_a_b"].reshape(1, DK)], axis=1)

    _, _, kc, vc_, _, rc, lac, hc = proj_fwd(ctx, vec1c + start, w_int, wa, ba, tt_ctx)
    qc0 = jnp.zeros_like(kc)
    _, _, sallf_c, sallb_c, sfin_c = gla_fwd(qc0, kc, vc_, lac, jnp.zeros((2, HV, DK), F32), cb_ctx)
    u, q, k, v, g, r, la, h = proj_fwd(x, vec1, w_int, wa, ba, tt)
    o_f, o_b, sall_f, sall_b, _ = gla_fwd(q, k, v, la, sfin_c, cb)
    w_out, wg_t, wu_t, w_down = late_weights(o_b)
    x1, cat, mix, yc, h2 = merge_fwd(u, g, o_f, o_b, x, vecm, vcm, convw, w_out, tt)
    tt2 = min(2 * tt, x.shape[0])
    act, dact, hid = ffn_gate_up(h2, wg_t, wu_t, tt2)
    dx2, dff, acc3 = ffn_down_loss(hid, x1, tgt, vec3, w_down, tt)
    dgt, dup = ffn_dhid(dff, act, dact, w_down, tt2)
    dx1, acc2 = ffn_dh2(dgt, dup, x1, dx2, vec2, wg_t, wu_t, tt)
    bt = min(2048, x.shape[0])
    gw = {"w_down": tn_matmul(hid, dff, FN, bt), "wg_t": tn_matmul(dgt, h2, FN, bt),
          "wu_t": tn_matmul(dup, h2, FN, bt)}
    vecm = vecm + grads_ready(("wg_t", "wu_t", "w_down"), gw)
    du, dg, do, dmix, accm1, accm2, dconvw = merge_bwd(dx1, mix, u, g, o_f, o_b, yc, vecm, vcm, convw, w_out, tt)
    gw["w_out"] = tn_matmul(cat, dmix, 512, bt)
    dsfin = jnp.zeros((2, HV, DK), F32) + grads_ready(("w_out",), gw)
    dqf, dkf, dvf, dlaf, dqb, dkb, dvb, dlab, ds0 = gla_bwd(q, k, v, la, do, sall_f, sall_b, dsfin, cb)
    gx, dp, acc1, dba, dwa = proj_bwd(du, (dqf, dqb, dkf, dkb, dvf, dvb), dg, dlaf, dlab, la, r, x, dx1, vec1, w_int,
                                      wa, tt)
    tcx = ctx.shape[0]
    zc = lambda w, dt=MXU_DTYPE: jnp.zeros((tcx, w), dt)
    _, dkf, dvf, dlaf, _, dkb, dvb, dlab, _ = gla_bwd(qc0, kc, vc_, lac, zc(DV), sallf_c, sallb_c, ds0, cb_ctx)
    _, dpc, acc1c, dbac, dwac = proj_bwd(zc(1024), (zc(DK), zc(DK), dkf, dkb, dvf, dvb), zc(DV), dlaf, dlab, lac, rc,
                                         ctx, zc(D, F32), vec1c, w_int, wa, tt_ctx)
    dwa_t = dwa + dwac
    dba_t = dba + dbac
    gs = {
        "norm1_g": acc1[2] + acc1c[2], "norm2_g": acc2[2], "final_g": acc3[1], "loss": acc3[2],
        "gla_norm_g": accm2[0], "conv_b": accm2[1], "conv_ln_g": accm2[2], "conv_ln_b": accm2[3],
        "conv_w": dconvw, "b_a": dba_t[0], "w_a2": dwa_t,
    }
    dmod = _vec8([acc1[0], acc1[1], accm1[0], acc2[0], acc2[1], acc3[0]], D)
    dmod_c = _vec8([acc1c[0], acc1c[1]], D)
    dpc = dpc + small_ready(gs, dmod, dmod_c).astype(dpc.dtype)
    btc = min(1024, tcx)
    gw["w_int"] = tn_matmul(dp, h, 896, bt, init=tn_matmul(dpc, hc, 896, btc)[0])
    grads_ready(("w_int",), gw)
    return gx, gw


PACK_ROWS = 96
ROW_N1, ROW_N2, ROW_FG, ROW_LOSS, ROW_GN, ROW_CB, ROW_LG, ROW_LB, ROW_BA = 0, 1, 2, 3, 4, 5, 6, 7, 8
ROW_DMOD, ROW_DMODC, ROW_CW, ROW_WA = 16, 24, 32, 64


def _pack_small(gs, dmod, dmod_c):
    pad = lambda a: jnp.pad(a, ((0, 0), (0, D - a.shape[1])))
    singles = _vec8([gs["norm1_g"], gs["norm2_g"], gs["final_g"], gs["loss"], gs["gla_norm_g"], gs["conv_b"],
                     gs["conv_ln_g"], gs["conv_ln_b"]], D)
    return jnp.concatenate([singles, _vec8([gs["b_a"]], D), dmod, dmod_c, pad(gs["conv_w"]), pad(gs["w_a2"][0:32])],
                           axis=0)


def kernel(x, c, ctx, c_ctx, w_mod, b_mod, norm1_g, norm2_g, w_in, conv_w, conv_b, conv_ln_g, conv_ln_b, w_a2_f, b_a_f, w_a2_b, b_a_b, gla_norm_g, w_out, w_gate, w_up, w_down, final_g, loss_target, m_c_ctx, m_w_mod, m_b_mod, m_norm1_g, m_norm2_g, m_w_in, m_conv_w, m_conv_b, m_conv_ln_g, m_conv_ln_b, m_w_a2_f, m_b_a_f, m_w_a2_b, m_b_a_b, m_gla_norm_g, m_w_out, m_w_gate, m_w_up, m_w_down, m_final_g, v_c_ctx, v_w_mod, v_b_mod, v_norm1_g, v_norm2_g, v_w_in, v_conv_w, v_conv_b, v_conv_ln_g, v_conv_ln_b, v_w_a2_f, v_b_a_f, v_w_a2_b, v_b_a_b, v_gla_norm_g, v_w_out, v_w_gate, v_w_up, v_w_down, v_final_g):
    me = 4 * lax.axis_index("x") + 2 * lax.axis_index("y") + lax.axis_index("c")
    t = x.shape[1]
    tcx = ctx.shape[1]
    r_in, r_out, r_ff = w_in.shape[2], w_out.shape[1], w_gate.shape[2]
    r_in_b = -(-r_in // 16) * 16

    tb = lambda w: w.T.astype(MXU_DTYPE)

    small = dict(norm1_g=norm1_g[0], norm2_g=norm2_g[0], final_g=final_g, gla_norm_g=gla_norm_g[0],
                 conv_b=conv_b[0], conv_ln_g=conv_ln_g[0], conv_ln_b=conv_ln_b[0], b_a_f=b_a_f[0], b_a_b=b_a_b[0])
    sm_pack = jnp.zeros((48, DC), F32)
    sm_pack = lax.dynamic_update_slice(sm_pack, conv_w[0], (0, me * (DC // N_DEV)))
    sm_pack = lax.dynamic_update_slice(sm_pack, w_a2_f[0], (32, me * (DK // N_DEV)))
    sm_pack = lax.dynamic_update_slice(sm_pack, w_a2_b[0], (32, DK + me * (DK // N_DEV)))

    mod_all, s_all, sm_tot, wall = mod_forward(c, c_ctx.reshape(1, D), w_mod[0], b_mod, sm_pack,
                                               jnp.pad(tb(w_in[0]), ((0, r_in_b - r_in), (0, 0))))
    mod = lax.dynamic_slice(mod_all, (me, 0), (1, 6 * D)).reshape(6 * D)
    mod_c = mod_all[8]
    small["conv_w"] = sm_tot[0:CW, :]
    small["w_a2_f"] = sm_tot[32:32 + RANK, 0:DK]
    small["w_a2_b"] = sm_tot[32:32 + RANK, DK:2 * DK]

    w_int =jnp.pad(wall[:, 0:r_in, :].reshape(N_DEV * r_in, D), ((0, DINP - DIN), (0, 0)))
    after_w_in = (wall[0:1, 0:1, 0] * 0).astype(MXU_DTYPE)
    late = [w_out[0].astype(MXU_DTYPE) + after_w_in, tb(w_gate[0]) + after_w_in, tb(w_up[0]) + after_w_in,
            w_down[0].astype(MXU_DTYPE) + after_w_in]
    g_send, g_recv, late_thru, late_lands, g_token = gather_start(late, me, "late")

    def late_weights(after):
        got = gather_wait(g_send, g_recv, late_thru, late_lands, after, "late")
        return tuple(a.reshape(N_DEV * a.shape[1], D) for a in got)

    pad_in = lambda g: jnp.pad(g[0:DIN].reshape(N_DEV, r_in, D), ((0, 0), (0, r_in_b - r_in), (0, 0)))
    blocked = {"w_int": pad_in, "w_out": lambda g: g.reshape(N_DEV, r_out, D)}
    as_blocks = lambda n, g: blocked.get(n, lambda a: a.reshape(N_DEV, r_ff, D))(g)
    pending = []

    def grads_ready(names, gw_now):
        blocks = [as_blocks(n, gw_now[n][1]) for n in names]
        if names[0] == "w_int":
            done = finish_small(gw_now["w_int"][0]).astype(WIRE_DTYPE)
            blocks = [b + done for b in blocks]
        send, recv_s, thru, zones, token = scatter_start(blocks, names[0])
        pending.append((names, send, recv_s, thru, zones))
        return token[0:1, 0:1]

    sm = {}

    def small_ready(gs, dmod, dmod_c):
        sm["copy"] = gather_start([_pack_small(gs, dmod, dmod_c)], me, "small")
        return sm["copy"][4][0:1, 0:1]

    def finish_small(after):
        send, recv_s, thru, zones, _ = sm["copy"]
        gat = gather_wait(send, recv_s, thru, zones, after, "small")[0]
        sm["tot"] = sum_blocks(gat)
        sm["dm"] = jnp.concatenate(
            [gat[:, ROW_DMOD:ROW_DMOD + 6, :].reshape(N_DEV, 6 * D),
             jnp.pad(sm["tot"][ROW_DMODC:ROW_DMODC + 6, :].reshape(1, 6 * D), ((0, 7), (0, 0)))], axis=0)
        ncol = w_mod.shape[2]
        dm_sh = lax.dynamic_slice(sm["dm"], (0, me * ncol), (16, ncol))
        sm["mod"] = mod_backward(s_all, dm_sh, w_mod[0], m_w_mod[0], v_w_mod[0], c_ctx.reshape(1, D),
                                 m_c_ctx.reshape(1, D), v_c_ctx.reshape(1, D))
        return sm["mod"][4][0:1, 0:1] * 0

    gx, gw = local_step(x[0], ctx[0], loss_target[0], mod, mod_c, small, w_int, g_token[0:1, 0:1], late_weights,
                        grads_ready, small_ready, 512, 256, 8, 4)
    tot, dm = sm["tot"], sm["dm"]
    loss = jnp.sum(tot[ROW_LOSS])
    g_wmod, d_wmod, nm_wmod, nv_wmod, g_cc, d_cc, nm_cc, nv_cc = sm["mod"]

    recv = {}

    def wait_for(entry, after):
        names, send, recv_s, thru, zones = entry
        recv.update(dict(zip(names, scatter_wait(send, recv_s, thru, zones, after, names[0]))))

    for entry in pending[:-1]:
        wait_for(entry, tot)
    own = {n: lax.dynamic_index_in_dim(as_blocks(n, gw[n][0]), me, 0, keepdims=False) for n in gw}
    padt = lambda w: jnp.pad(w.T, ((0, r_in_b - r_in), (0, 0)))
    big = {}
    big["w_gate"] = [a.T for a in adamw_sharded(own["wg_t"], recv["wg_t"], w_gate[0].T, m_w_gate[0].T,
                                                 v_w_gate[0].T)]
    big["w_up"] = [a.T for a in adamw_sharded(own["wu_t"], recv["wu_t"], w_up[0].T, m_w_up[0].T, v_w_up[0].T)]
    big["w_down"] = adamw_sharded(own["w_down"], recv["w_down"], w_down[0], m_w_down[0], v_w_down[0])
    big["w_out"] = adamw_sharded(own["w_out"], recv["w_out"], w_out[0], m_w_out[0], v_w_out[0])
    wait_for(pending[-1], big["w_out"][0])
    big["w_in"] = [a[0:r_in].T for a in adamw_sharded(own["w_int"], recv["w_int"], padt(w_in[0]), padt(m_w_in[0]),
                                                       padt(v_w_in[0]))]
    big["w_mod"] = [g_wmod, d_wmod, nm_wmod, nv_wmod]

    row = lambda r, w: tot[r:r + 1, 0:w]
    gn_row = tot[ROW_GN:ROW_GN + 1, 0:DC]
    g_small = {
        "b_mod": jnp.sum(dm, axis=0, keepdims=True),
        "norm1_g": row(ROW_N1, D), "norm2_g": row(ROW_N2, D),
        "conv_w": lax.dynamic_slice(tot, (ROW_CW, me * (DC // N_DEV)), (CW, DC // N_DEV)),
        "conv_b": row(ROW_CB, DC), "conv_ln_g": row(ROW_LG, DC), "conv_ln_b": row(ROW_LB, DC),
        "w_a2_f": lax.dynamic_slice(tot, (ROW_WA, me * (DK // N_DEV)), (RANK, DK // N_DEV)),
        "b_a_f": tot[ROW_BA:ROW_BA + 1, 0:DK],
        "w_a2_b": lax.dynamic_slice(tot, (ROW_WA + RANK, DK + me * (DK // N_DEV)), (RANK, DK // N_DEV)),
        "b_a_b": tot[ROW_BA:ROW_BA + 1, DK:2 * DK],
        "gla_norm_g": gn_row[:, 0:HV] + gn_row[:, HV:2 * HV] + gn_row[:, 2 * HV:3 * HV] + gn_row[:, 3 * HV:4 * HV],
        "final_g": row(ROW_FG, D),
    }
    wmv = {
        "b_mod": (b_mod, m_b_mod, v_b_mod), "norm1_g": (norm1_g, m_norm1_g, v_norm1_g),
        "norm2_g": (norm2_g, m_norm2_g, v_norm2_g), "conv_w": (conv_w[0], m_conv_w[0], v_conv_w[0]),
        "conv_b": (conv_b, m_conv_b, v_conv_b), "conv_ln_g": (conv_ln_g, m_conv_ln_g, v_conv_ln_g),
        "conv_ln_b": (conv_ln_b, m_conv_ln_b, v_conv_ln_b), "w_a2_f": (w_a2_f[0], m_w_a2_f[0], v_w_a2_f[0]),
        "b_a_f": (b_a_f, m_b_a_f, v_b_a_f), "w_a2_b": (w_a2_b[0], m_w_a2_b[0], v_w_a2_b[0]),
        "b_a_b": (b_a_b, m_b_a_b, v_b_a_b), "gla_norm_g": (gla_norm_g, m_gla_norm_g, v_gla_norm_g),
        "final_g": (final_g.reshape(1, D), m_final_g.reshape(1, D), v_final_g.reshape(1, D)),
    }
    names_small = list(g_small)
    upd = adamw_small([(g_small[n],) + wmv[n] for n in names_small])
    res = {n: (g_small[n],) + upd[i] for i, n in enumerate(names_small)}
    res["c_ctx"] = (g_cc, d_cc, nm_cc, nv_cc)
    for n in ("w_mod", "w_in", "w_out", "w_gate", "w_up", "w_down"):
        res[n] = tuple(big[n])

    order = ["c_ctx", "w_mod", "b_mod", "norm1_g", "norm2_g", "w_in", "conv_w", "conv_b", "conv_ln_g", "conv_ln_b",
             "w_a2_f", "b_a_f", "w_a2_b", "b_a_b", "gla_norm_g", "w_out", "w_gate", "w_up", "w_down", "final_g"]
    shapes = {"c_ctx": c_ctx.shape, "w_mod": w_mod.shape, "b_mod": b_mod.shape, "norm1_g": norm1_g.shape,
              "norm2_g": norm2_g.shape, "w_in": w_in.shape, "conv_w": conv_w.shape, "conv_b": conv_b.shape,
              "conv_ln_g": conv_ln_g.shape, "conv_ln_b": conv_ln_b.shape, "w_a2_f": w_a2_f.shape,
              "b_a_f": b_a_f.shape, "w_a2_b": w_a2_b.shape, "b_a_b": b_a_b.shape, "gla_norm_g": gla_norm_g.shape,
              "w_out": w_out.shape, "w_gate": w_gate.shape, "w_up": w_up.shape, "w_down": w_down.shape,
              "final_g": final_g.shape}
    outs = [loss, gx.reshape(x.shape)]
    for i in range(4):
        outs += [res[n][i].reshape(shapes[n]) for n in order]
    return tuple(outs)
```

```python
import functools

import jax
import jax.numpy as jnp
from jax import lax
from jax.experimental import pallas as pl
from jax.experimental.pallas import tpu as pltpu

F32 = jnp.float32
MXU_DTYPE = jnp.bfloat16
WIRE_DTYPE = jnp.bfloat16
HI = lax.Precision.HIGHEST
MESH = pl.DeviceIdType.MESH

N_DEV = 8
D = 1024
DC = 512
NH = 4
HK = 64
HV = 128
DK = NH * HK
DV = NH * HV
RANK = 16
CHUNK = 64
SEG = 64
CW = 31
CPAD = 15
DFF = 2816
DIN = 2592
DINP = 2688
TAU = 16.0
EPS = 1e-6
VMEM_LIMIT = 56 * 1024 * 1024

ADAM_LR = 0.001
ADAM_B1 = 0.9
ADAM_B2 = 0.999
ADAM_EPS = 1e-08
ADAM_WD = 0.01
ADAM_STEP = 10


def _mm(a, b):
    return jnp.dot(a.astype(MXU_DTYPE), b.astype(MXU_DTYPE), preferred_element_type=F32)


def _mm_nt(a, b):
    return lax.dot_general(a.astype(MXU_DTYPE), b.astype(MXU_DTYPE), (((1,), (1,)), ((), ())),
                           preferred_element_type=F32)


def _mm_tn(a, b):
    return lax.dot_general(a.astype(MXU_DTYPE), b.astype(MXU_DTYPE), (((0,), (0,)), ((), ())),
                           preferred_element_type=F32)


def _hi(a, b):
    return jnp.dot(a, b, precision=HI, preferred_element_type=F32)


def _hi_nt(a, b):
    return lax.dot_general(a, b, (((1,), (1,)), ((), ())), precision=HI, preferred_element_type=F32)


def _hi_tn(a, b):
    return lax.dot_general(a, b, (((0,), (0,)), ((), ())), precision=HI, preferred_element_type=F32)


def _sigmoid(x):
    return 1.0 / (1.0 + jnp.exp(-x))


def _cparams(n_axes):
    return pltpu.CompilerParams(dimension_semantics=("arbitrary",) * n_axes, vmem_limit_bytes=VMEM_LIMIT)


def _full(shape):
    n = len(shape)
    return pl.BlockSpec(shape, lambda *_: (0,) * n)


def _rows(tt, width):
    return pl.BlockSpec((tt, width), lambda i: (i, 0))


def _sds(shape, dtype=F32):
    return jax.ShapeDtypeStruct(shape, dtype)


def _norm_mod(x, g, sh, sc):
    r = lax.rsqrt(jnp.mean(x * x, axis=-1, keepdims=True) + EPS)
    xn = x * r
    yy = xn * g
    return r, xn, yy, yy * (1.0 + sc) + sh


def _norm_mod_bwd(dh, r, xn, yy, g, sc):
    dsh = jnp.sum(dh, axis=0, keepdims=True)
    dsc = jnp.sum(dh * yy, axis=0, keepdims=True)
    dy = dh * (1.0 + sc)
    dg = jnp.sum(dy * xn, axis=0, keepdims=True)
    dxn = dy * g
    dx = r * (dxn - xn * jnp.mean(dxn * xn, axis=-1, keepdims=True))
    return dsh, dsc, dg, dx


def _zero_first(*refs):
    @pl.when(pl.program_id(0) == 0)
    def _():
        for r in refs:
            r[...] = jnp.zeros_like(r)


def _acc_rows(ref, rows):
    ref[...] += jnp.concatenate(rows + [jnp.zeros((8 - len(rows), rows[0].shape[1]), F32)], axis=0)


def proj_fwd(x, vec, w_int, wa, ba, tt):
    t = x.shape[0]

    def body(x_ref, vec_ref, w_ref, wa_ref, ba_ref, u_ref, q_ref, k_ref, v_ref, g_ref, r_ref, la_ref, h_ref):
        _, _, _, h = _norm_mod(x_ref[...], vec_ref[0:1, :], vec_ref[1:2, :], vec_ref[2:3, :])
        hb = h.astype(MXU_DTYPE)
        h_ref[...] = hb
        p = _mm_nt(hb, w_ref[...])
        u_ref[...] = p[:, 0:1024]
        q_ref[...] = p[:, 1024:1280]
        k_ref[...] = p[:, 1280:1536]
        v_ref[...] = p[:, 1536:2048]
        g_ref[...] = p[:, 2048:2560]
        rr = p[:, 2560:2688]
        r_ref[...] = rr
        z = _mm(rr, wa_ref[...]) + ba_ref[...]
        la_ref[...] = (jnp.minimum(z, 0.0) - jnp.log(1.0 + jnp.exp(-jnp.abs(z)))) * (1.0 / TAU)

    return pl.pallas_call(
        body, name="proj_fwd", grid=(t // tt,),
        in_specs=[_rows(tt, D), _full((8, D)), _full((DINP, D)), _full((128, 512)), _full((1, 512))],
        out_specs=[_rows(tt, 1024), _rows(tt, DK), _rows(tt, DK), _rows(tt, DV), _rows(tt, DV), _rows(tt, 128),
                   _rows(tt, 512), _rows(tt, D)],
        out_shape=[_sds((t, 1024)), _sds((t, DK)), _sds((t, DK)), _sds((t, DV)), _sds((t, DV)), _sds((t, 128)),
                   _sds((t, 512)), _sds((t, D), MXU_DTYPE)],
        compiler_params=_cparams(1),
    )(x, vec, w_int, wa, ba)


def proj_bwd(du, dqkv, dg, dla_f, dla_b, la, r, x, dx1, vec, w_int, wa, tt):
    t = x.shape[0]

    def body(du_ref, dqf_ref, dqb_ref, dkf_ref, dkb_ref, dvf_ref, dvb_ref, dg_ref, dlaf_ref, dlab_ref, la_ref, r_ref,
             x_ref, dx1_ref, vec_ref, w_ref, wa_ref, gx_ref, dp_ref, acc_ref, dba_ref, dwa_ref):
        _zero_first(acc_ref, dba_ref, dwa_ref)
        dla = jnp.concatenate([dlaf_ref[...], dlab_ref[...]], axis=1)
        dz = dla * (1.0 - jnp.exp(TAU * la_ref[...])) * (1.0 / TAU)
        rr = r_ref[...]
        _acc_rows(dba_ref, [jnp.sum(dz, axis=0, keepdims=True)])
        dwa_ref[...] += _mm_tn(rr, dz)
        dr = _mm_nt(dz, wa_ref[...])
        md = lambda a: a.astype(MXU_DTYPE)
        both = lambda a_ref, b_ref: md(a_ref[...].astype(F32) + b_ref[...].astype(F32))
        dp = jnp.concatenate([du_ref[...], both(dqf_ref, dqb_ref), both(dkf_ref, dkb_ref), both(dvf_ref, dvb_ref),
                              dg_ref[...], md(dr)], axis=1)
        dp_ref[...] = dp
        dh = _mm(dp, w_ref[...])
        g, sc = vec_ref[0:1, :], vec_ref[2:3, :]
        rn, xn, yy, _ = _norm_mod(x_ref[...], g, vec_ref[1:2, :], sc)
        dsh, dsc, dgn, dx = _norm_mod_bwd(dh, rn, xn, yy, g, sc)
        gx_ref[...] = dx1_ref[...] + dx
        _acc_rows(acc_ref, [dsh, dsc, dgn])

    return pl.pallas_call(
        body, name="proj_bwd", grid=(t // tt,),
        in_specs=[_rows(tt, 1024), _rows(tt, DK), _rows(tt, DK), _rows(tt, DK), _rows(tt, DK), _rows(tt, DV),
                  _rows(tt, DV), _rows(tt, DV), _rows(tt, DK), _rows(tt, DK), _rows(tt, 512),
                  _rows(tt, 128), _rows(tt, D), _rows(tt, D), _full((8, D)), _full((DINP, D)), _full((128, 512))],
        out_specs=[_rows(tt, D), _rows(tt, DINP), _full((8, D)), _full((8, 512)), _full((128, 512))],
        out_shape=[_sds((t, D)), _sds((t, DINP), MXU_DTYPE), _sds((8, D)), _sds((8, 512)), _sds((128, 512))],
        compiler_params=_cparams(1),
    )(du, *dqkv, dg, dla_f, dla_b, la, r, x, dx1, vec, w_int, wa)


def _dot_exact01(m01, x):
    bf = jnp.bfloat16
    w = x.shape[1]
    hi = x.astype(bf)
    r1 = x - hi.astype(F32)
    mid = r1.astype(bf)
    lo = (r1 - mid.astype(F32)).astype(bf)
    y = jnp.dot(m01.astype(bf), jnp.concatenate([hi, mid, lo], axis=1), preferred_element_type=F32)
    return y[:, 0:w] + y[:, w:2 * w] + y[:, 2 * w:3 * w]


def _gla_chunk(d, qc, kc, la_c):
    row = lax.broadcasted_iota(jnp.int32, (CHUNK, CHUNK), 0)
    col = lax.broadcasted_iota(jnp.int32, (CHUNK, CHUNK), 1)
    cum = ((col <= row) if d == 0 else (col >= row)).astype(F32)
    cum_t = ((col >= row) if d == 0 else (col <= row)).astype(F32)
    cum4 = jnp.concatenate([cum] * NH, axis=0)
    head_of_lane = lax.broadcasted_iota(jnp.int32, (1, DK), 1) // HK
    b = _dot_exact01(cum, la_c)
    bl = jnp.sum(la_c, axis=0, keepdims=True)
    eb = jnp.exp(b)
    enb = jnp.exp(-b)
    ekd = jnp.exp(bl - b)
    qt = qc * (HK ** -0.5) * eb
    kt = kc * enb
    kd = kc * ekd
    qst = jnp.concatenate([jnp.where(head_of_lane == h, qt, 0.0) for h in range(NH)], axis=0)
    a = _mm_nt(qst, kt) * cum4
    return cum_t, cum4, head_of_lane, eb, enb, ekd, qt, kt, kd, qst, a, jnp.exp(bl)


def gla_fwd(q, k, v, la, s0, cb):
    t = q.shape[0]
    nc = t // CHUNK
    nb = nc // cb

    def body(qf_ref, kf_ref, vf_ref, laf_ref, qb_ref, kb_ref, vb_ref, lab_ref, s0_ref,
             of_ref, ob_ref, sf_ref, sb_ref, sfin_ref, s_scr):
        i = pl.program_id(0)

        @pl.when(i == 0)
        def _():
            s_scr[...] = s0_ref[...]

        def chunk(d, jj, q_ref, k_ref, v_ref, la_ref, o_ref, sall_ref):
            rows = slice(jj * CHUNK, (jj + 1) * CHUNK)
            vc = v_ref[rows, :]
            _, _, head_of_lane, _, _, _, _, _, kd, qst, a, dec = _gla_chunk(
                d, q_ref[rows, :], k_ref[rows, :], la_ref[rows, :])
            s = s_scr[d]
            sall_ref[jj] = s
            inter = _mm_nt(qst, s)
            outs = []
            for h in range(NH):
                hs = slice(h * CHUNK, (h + 1) * CHUNK)
                outs.append(_mm(a[hs], vc[:, h * HV:(h + 1) * HV]) + inter[hs])
            o_ref[rows, :] = jnp.concatenate(outs, axis=1)
            kv = _mm_tn(vc, kd)
            s_new = dec * s
            for h in range(NH):
                s_new = s_new + jnp.where(head_of_lane == h, kv[h * HV:(h + 1) * HV], 0.0)
            s_scr[d] = s_new

        for j in range(cb):
            chunk(0, j, qf_ref, kf_ref, vf_ref, laf_ref, of_ref, sf_ref)
            chunk(1, cb - 1 - j, qb_ref, kb_ref, vb_ref, lab_ref, ob_ref, sb_ref)

        @pl.when(i == nb - 1)
        def _():
            sfin_ref[...] = s_scr[...]

    tb = cb * CHUNK
    fwd = lambda w, c=0: pl.BlockSpec((tb, w), lambda i: (i, c))
    bwd = lambda w, c=0: pl.BlockSpec((tb, w), lambda i: (nb - 1 - i, c))
    return pl.pallas_call(
        body, name="gla_fwd", grid=(nb,),
        in_specs=[fwd(DK), fwd(DK), fwd(DV), fwd(DK, 0), bwd(DK), bwd(DK), bwd(DV), bwd(DK, 1), _full((2, HV, DK))],
        out_specs=[fwd(DV), bwd(DV), pl.BlockSpec((cb, HV, DK), lambda i: (i, 0, 0)),
                   pl.BlockSpec((cb, HV, DK), lambda i: (nb - 1 - i, 0, 0)), _full((2, HV, DK))],
        out_shape=[_sds((t, DV)), _sds((t, DV)), _sds((nc, HV, DK)), _sds((nc, HV, DK)), _sds((2, HV, DK))],
        scratch_shapes=[pltpu.VMEM((2, HV, DK), F32)],
        compiler_params=_cparams(1),
    )(q, k, v, la, q, k, v, la, s0)


def gla_bwd(q, k, v, la, do, sall_f, sall_b, dsfin, cb):
    t = q.shape[0]
    nc = t // CHUNK
    nb = nc // cb

    def body(qf_ref, kf_ref, vf_ref, laf_ref, dof_ref, sf_ref, qb_ref, kb_ref, vb_ref, lab_ref, dob_ref, sb_ref,
             dsfin_ref, dqf_ref, dkf_ref, dvf_ref, dlaf_ref, dqb_ref, dkb_ref, dvb_ref, dlab_ref, ds0_ref, ds_scr):
        i = pl.program_id(0)

        @pl.when(i == 0)
        def _():
            ds_scr[...] = dsfin_ref[...]

        def chunk(d, jj, q_ref, k_ref, v_ref, la_ref, do_ref, sall_ref, dq_ref, dk_ref, dv_ref, dla_ref):
            rows = slice(jj * CHUNK, (jj + 1) * CHUNK)
            vc = v_ref[rows, :]
            doc = do_ref[rows, :]
            cum_t, cum4, head_of_lane, eb, enb, ekd, qt, kt, kd, qst, a, dec = _gla_chunk(
                d, q_ref[rows, :], k_ref[rows, :], la_ref[rows, :])
            s = sall_ref[jj]
            ds = ds_scr[d]
            hv = lambda x, h: x[:, h * HV:(h + 1) * HV]
            hr = lambda x, h: x[h * CHUNK:(h + 1) * CHUNK]
            fold = lambda x: functools.reduce(
                lambda p, c: p + c, [jnp.where(head_of_lane == h, hr(x, h), 0.0) for h in range(NH)])
            dost = jnp.concatenate([hv(doc, h) for h in range(NH)], axis=0)
            vst = jnp.concatenate([hv(vc, h) for h in range(NH)], axis=0)
            da = jnp.concatenate([_mm_nt(hv(doc, h), hv(vc, h)) for h in range(NH)], axis=0) * cum4
            dqt = fold(_mm(da, kt) + _mm(dost, s))
            dkt = _mm_tn(da, qst)
            kdst = jnp.concatenate([jnp.where(head_of_lane == h, kd, 0.0) for h in range(NH)], axis=0)
            dv_inter = _mm_nt(kdst, ds)
            dv_ref[rows, :] = jnp.concatenate(
                [_mm_tn(hr(a, h), hv(doc, h)) + hr(dv_inter, h) for h in range(NH)], axis=1).astype(MXU_DTYPE)
            dkd = fold(_mm(vst, ds))
            ds_scr[d] = dec * ds + _mm_tn(dost, qst)
            tkd = dkd * kd
            db = dqt * qt - dkt * kt - tkd
            dbl = jnp.sum(ds * s, axis=0, keepdims=True) * dec + jnp.sum(tkd, axis=0, keepdims=True)
            dla_ref[rows, :] = _dot_exact01(cum_t, db) + dbl
            dq_ref[rows, :] = (dqt * eb * (HK ** -0.5)).astype(MXU_DTYPE)
            dk_ref[rows, :] = (dkt * enb + dkd * ekd).astype(MXU_DTYPE)

        for j in range(cb):
            chunk(0, cb - 1 - j, qf_ref, kf_ref, vf_ref, laf_ref, dof_ref, sf_ref, dqf_ref, dkf_ref, dvf_ref, dlaf_ref)
            chunk(1, j, qb_ref, kb_ref, vb_ref, lab_ref, dob_ref, sb_ref, dqb_ref, dkb_ref, dvb_ref, dlab_ref)

        @pl.when(i == nb - 1)
        def _():
            ds0_ref[...] = ds_scr[...]

    tb = cb * CHUNK
    rev = lambda w, c=0: pl.BlockSpec((tb, w), lambda i: (nb - 1 - i, c))
    fro = lambda w, c=0: pl.BlockSpec((tb, w), lambda i: (i, c))
    st_rev = pl.BlockSpec((cb, HV, DK), lambda i: (nb - 1 - i, 0, 0))
    st_fro = pl.BlockSpec((cb, HV, DK), lambda i: (i, 0, 0))
    md = MXU_DTYPE
    return pl.pallas_call(
        body, name="gla_bwd", grid=(nb,),
        in_specs=[rev(DK), rev(DK), rev(DV), rev(DK, 0), rev(DV), st_rev,
                  fro(DK), fro(DK), fro(DV), fro(DK, 1), fro(DV), st_fro, _full((2, HV, DK))],
        out_specs=[rev(DK), rev(DK), rev(DV), rev(DK), fro(DK), fro(DK), fro(DV), fro(DK), _full((2, HV, DK))],
        out_shape=[_sds((t, DK), md), _sds((t, DK), md), _sds((t, DV), md), _sds((t, DK)),
                   _sds((t, DK), md), _sds((t, DK), md), _sds((t, DV), md), _sds((t, DK)), _sds((2, HV, DK))],
        scratch_shapes=[pltpu.VMEM((2, HV, DK), F32)],
        compiler_params=_cparams(1),
    )(q, k, v, la, do, sall_f, q, k, v, la, do, sall_b, dsfin)


def _seg_pos(tt):
    return lax.broadcasted_iota(jnp.int32, (tt, 1), 0) % SEG


def _shifted(x, s, pos, tt):
    y = x if s == 0 else pltpu.roll(x, (-s) % tt, 0)
    return jnp.where((pos + s >= 0) & (pos + s < SEG), y, 0.0)


def _head_norm(o, gn):
    rs, xs = [], []
    for h in range(NH):
        oh = o[:, h * HV:(h + 1) * HV]
        r = lax.rsqrt(jnp.mean(oh * oh, axis=-1, keepdims=True) + EPS)
        rs.append(r)
        xs.append(oh * r)
    return rs, xs


def merge_fwd(u, g, o_f, o_b, x, vec, vc, convw, w_out, tt):
    t = x.shape[0]

    def body(u_ref, g_ref, of_ref, ob_ref, x_ref, vec_ref, vc_ref, cw_ref, w_ref, x1_ref, cat_ref, mix_ref, yc_ref,
             h2_ref):
        a = u_ref[:, 0:DC]
        gate = u_ref[:, DC:2 * DC]
        vv = a * _sigmoid(gate)
        pos = _seg_pos(tt)
        cw = cw_ref[...]
        yc = jnp.zeros((tt, DC), F32) + vc_ref[1:2, :]
        for j in range(CW):
            yc = yc + _shifted(vv, j - CPAD, pos, tt) * cw[j:j + 1, :]
        yc_ref[...] = yc
        mu = jnp.mean(yc, axis=-1, keepdims=True)
        yd = yc - mu
        rs = lax.rsqrt(jnp.mean(yd * yd, axis=-1, keepdims=True) + EPS)
        ln = yd * rs * vc_ref[2:3, :] + vc_ref[3:4, :]
        conv_o = ln * _sigmoid(ln)
        o = of_ref[...] + ob_ref[...]
        _, xs = _head_norm(o, None)
        gg = g_ref[...]
        o2g = jnp.concatenate(xs, axis=1) * vc_ref[0:1, :] * (gg * _sigmoid(gg))
        cat = jnp.concatenate([conv_o, o2g], axis=1).astype(MXU_DTYPE)
        cat_ref[...] = cat
        mix = _mm(cat, w_ref[...])
        mix_ref[...] = mix
        x1 = x_ref[...] + vec_ref[0:1, :] * mix
        x1_ref[...] = x1
        _, _, _, h2 = _norm_mod(x1, vec_ref[1:2, :], vec_ref[2:3, :], vec_ref[3:4, :])
        h2_ref[...] = h2.astype(MXU_DTYPE)

    return pl.pallas_call(
        body, name="merge_fwd", grid=(t // tt,),
        in_specs=[_rows(tt, 1024), _rows(tt, DV), _rows(tt, DV), _rows(tt, DV), _rows(tt, D),
                  _full((8, D)), _full((8, DC)), _full((32, DC)), _full((D, D))],
        out_specs=[_rows(tt, D), _rows(tt, D), _rows(tt, D), _rows(tt, DC), _rows(tt, D)],
        out_shape=[_sds((t, D)), _sds((t, D), MXU_DTYPE), _sds((t, D)), _sds((t, DC)), _sds((t, D), MXU_DTYPE)],
        compiler_params=_cparams(1),
    )(u, g, o_f, o_b, x, vec, vc, convw, w_out)


def merge_bwd(dx1, mix, u, g, o_f, o_b, yc, vec, vc, convw, w_out, tt):
    t = dx1.shape[0]

    def body(dx1_ref, mix_ref, u_ref, g_ref, of_ref, ob_ref, yc_ref, vec_ref, vc_ref, cw_ref, w_ref,
             du_ref, dg_ref, do_ref, dmix_ref, acc1_ref, acc2_ref, dcw_ref):
        _zero_first(acc1_ref, acc2_ref, dcw_ref)
        dx1v = dx1_ref[...]
        dg1 = jnp.sum(dx1v * mix_ref[...], axis=0, keepdims=True)
        dmix = (vec_ref[0:1, :] * dx1v).astype(MXU_DTYPE)
        dmix_ref[...] = dmix
        dcat = _mm_nt(dmix, w_ref[...])
        dconv_o = dcat[:, 0:DC]
        do2 = dcat[:, DC:2 * DC]
        gn = vc_ref[0:1, :]
        o = of_ref[...] + ob_ref[...]
        rs, xs = _head_norm(o, None)
        xn = jnp.concatenate(xs, axis=1)
        gg = g_ref[...]
        sg = _sigmoid(gg)
        don = do2 * (gg * sg)
        dg_ref[...] = (do2 * (xn * gn) * (sg * (1.0 + gg * (1.0 - sg)))).astype(MXU_DTYPE)
        dgn = jnp.sum(don * xn, axis=0, keepdims=True)
        dxn = don * gn
        dos = []
        for h in range(NH):
            dh = dxn[:, h * HV:(h + 1) * HV]
            dos.append(rs[h] * (dh - xs[h] * jnp.mean(dh * xs[h], axis=-1, keepdims=True)))
        do_ref[...] = jnp.concatenate(dos, axis=1).astype(MXU_DTYPE)
        yc = yc_ref[...]
        mu = jnp.mean(yc, axis=-1, keepdims=True)
        yd = yc - mu
        rstd = lax.rsqrt(jnp.mean(yd * yd, axis=-1, keepdims=True) + EPS)
        yhat = yd * rstd
        lg = vc_ref[2:3, :]
        ln = yhat * lg + vc_ref[3:4, :]
        sl = _sigmoid(ln)
        dln = dconv_o * (sl * (1.0 + ln * (1.0 - sl)))
        dlb = jnp.sum(dln, axis=0, keepdims=True)
        dlg = jnp.sum(dln * yhat, axis=0, keepdims=True)
        dyh = dln * lg
        dyc = rstd * (dyh - jnp.mean(dyh, axis=-1, keepdims=True)
                      - yhat * jnp.mean(dyh * yhat, axis=-1, keepdims=True))
        dcb = jnp.sum(dyc, axis=0, keepdims=True)
        a = u_ref[:, 0:DC]
        gate = u_ref[:, DC:2 * DC]
        sgt = _sigmoid(gate)
        vv = a * sgt
        pos = _seg_pos(tt)
        cw = cw_ref[...]
        dvv = jnp.zeros((tt, DC), F32)
        dws = []
        for j in range(CW):
            shifted_dyc = _shifted(dyc, CPAD - j, pos, tt)
            dvv = dvv + shifted_dyc * cw[j:j + 1, :]
            dws.append(jnp.sum(shifted_dyc * vv, axis=0, keepdims=True))
        dws.append(jnp.zeros((1, DC), F32))
        du_ref[:, 0:DC] = (dvv * sgt).astype(MXU_DTYPE)
        du_ref[:, DC:2 * DC] = (dvv * a * sgt * (1.0 - sgt)).astype(MXU_DTYPE)
        _acc_rows(acc1_ref, [dg1])
        _acc_rows(acc2_ref, [dgn, dcb, dlg, dlb])
        dcw_ref[...] += jnp.concatenate(dws, axis=0)

    return pl.pallas_call(
        body, name="merge_bwd", grid=(t // tt,),
        in_specs=[_rows(tt, D), _rows(tt, D), _rows(tt, 1024), _rows(tt, DV), _rows(tt, DV), _rows(tt, DV),
                  _rows(tt, DC),
                  _full((8, D)), _full((8, DC)), _full((32, DC)), _full((D, D))],
        out_specs=[_rows(tt, 1024), _rows(tt, DV), _rows(tt, DV), _rows(tt, D), _full((8, D)), _full((8, DC)),
                   _full((32, DC))],
        out_shape=[_sds((t, 1024), MXU_DTYPE), _sds((t, DV), MXU_DTYPE), _sds((t, DV), MXU_DTYPE),
                   _sds((t, D), MXU_DTYPE), _sds((8, D)),
                   _sds((8, DC)), _sds((32, DC))],
        compiler_params=_cparams(1),
    )(dx1, mix, u, g, o_f, o_b, yc, vec, vc, convw, w_out)


FN = DFF // 2


def ffn_gate_up(h2, wg_t, wu_t, tt):
    t = h2.shape[0]

    def body(h2_ref, wg_ref, wu_ref, s_ref, d_ref, hid_ref):
        h2v = h2_ref[...]
        gt = _mm_nt(h2v, wg_ref[...])
        up = _mm_nt(h2v, wu_ref[...])
        sg = _sigmoid(gt)
        act = gt * sg
        s_ref[...] = act.astype(MXU_DTYPE)
        d_ref[...] = (up * (sg * (1.0 + gt * (1.0 - sg)))).astype(MXU_DTYPE)
        hid_ref[...] = (act * up).astype(MXU_DTYPE)

    blk = pl.BlockSpec((tt, FN), lambda j, i: (i, j))
    wblk = pl.BlockSpec((FN, D), lambda j, i: (j, 0))
    return pl.pallas_call(
        body, name="ffn_gate_up", grid=(2, t // tt),
        in_specs=[pl.BlockSpec((tt, D), lambda j, i: (i, 0)), wblk, wblk],
        out_specs=[blk, blk, blk],
        out_shape=[_sds((t, DFF), MXU_DTYPE)] * 3,
        compiler_params=_cparams(2),
    )(h2, wg_t, wu_t)


def ffn_down_loss(hid, x1, tgt, vec, w_down, tt):
    t = x1.shape[0]

    def body(hid_ref, x1_ref, tgt_ref, vec_ref, w_ref, dx2_ref, dff_ref, acc_ref):
        _zero_first(acc_ref)
        g2 = vec_ref[0:1, :]
        fg = vec_ref[1:2, :]
        ff = _mm(hid_ref[...], w_ref[...])
        x2 = x1_ref[...] + g2 * ff
        rf = lax.rsqrt(jnp.mean(x2 * x2, axis=-1, keepdims=True) + EPS)
        xn = x2 * rf
        err = xn * fg - tgt_ref[...]
        dy = err * (1.0 / D)
        dfg = jnp.sum(dy * xn, axis=0, keepdims=True)
        dxn = dy * fg
        dx2 = rf * (dxn - xn * jnp.mean(dxn * xn, axis=-1, keepdims=True))
        dx2_ref[...] = dx2
        dff_ref[...] = (g2 * dx2).astype(MXU_DTYPE)
        dg2 = jnp.sum(dx2 * ff, axis=0, keepdims=True)
        loss = jnp.sum(err * err, axis=0, keepdims=True) * (0.5 / D)
        _acc_rows(acc_ref, [dg2, dfg, loss])

    return pl.pallas_call(
        body, name="ffn_down_loss", grid=(t // tt,),
        in_specs=[_rows(tt, DFF), _rows(tt, D), _rows(tt, D), _full((8, D)), _full((DFF, D))],
        out_specs=[_rows(tt, D), _rows(tt, D), _full((8, D))],
        out_shape=[_sds((t, D)), _sds((t, D), MXU_DTYPE), _sds((8, D))],
        compiler_params=_cparams(1),
    )(hid, x1, tgt, vec, w_down)


def ffn_dhid(dff, s, d, w_down, tt):
    t = dff.shape[0]

    def body(dff_ref, s_ref, d_ref, w_ref, dgt_ref, dup_ref):
        dhid = _mm_nt(dff_ref[...], w_ref[...])
        dgt_ref[...] = (dhid * d_ref[...].astype(F32)).astype(MXU_DTYPE)
        dup_ref[...] = (dhid * s_ref[...].astype(F32)).astype(MXU_DTYPE)

    blk = pl.BlockSpec((tt, FN), lambda j, i: (i, j))
    return pl.pallas_call(
        body, name="ffn_dhid", grid=(2, t // tt),
        in_specs=[pl.BlockSpec((tt, D), lambda j, i: (i, 0)), blk, blk, pl.BlockSpec((FN, D), lambda j, i: (j, 0))],
        out_specs=[blk, blk],
        out_shape=[_sds((t, DFF), MXU_DTYPE), _sds((t, DFF), MXU_DTYPE)],
        compiler_params=_cparams(2),
    )(dff, s, d, w_down)


def ffn_dh2(dgt, dup, x1, dx2, vec, wg_t, wu_t, tt):
    t = x1.shape[0]

    def body(dgt_ref, dup_ref, x1_ref, dx2_ref, vec_ref, wg_ref, wu_ref, dx1_ref, acc_ref):
        _zero_first(acc_ref)
        dh2 = _mm(dgt_ref[...], wg_ref[...]) + _mm(dup_ref[...], wu_ref[...])
        g, sc = vec_ref[0:1, :], vec_ref[2:3, :]
        r, xn, yy, _ = _norm_mod(x1_ref[...], g, vec_ref[1:2, :], sc)
        dsh, dsc, dgn, dx = _norm_mod_bwd(dh2, r, xn, yy, g, sc)
        dx1_ref[...] = dx2_ref[...] + dx
        _acc_rows(acc_ref, [dsh, dsc, dgn])

    return pl.pallas_call(
        body, name="ffn_dh2", grid=(t // tt,),
        in_specs=[_rows(tt, DFF), _rows(tt, DFF), _rows(tt, D), _rows(tt, D), _full((8, D)), _full((DFF, D)),
                  _full((DFF, D))],
        out_specs=[_rows(tt, D), _full((8, D))],
        out_shape=[_sds((t, D)), _sds((8, D))],
        compiler_params=_cparams(1),
    )(dgt, dup, x1, dx2, vec, wg_t, wu_t)


def tn_matmul(a, b, bm, bt, init=None):
    t, m = a.shape
    n = b.shape[1]
    nk = t // bt

    def body(*refs):
        if init is None:
            a_ref, b_ref, o_ref, wire_ref = refs
        else:
            a_ref, b_ref, i_ref, o_ref, wire_ref = refs
        @pl.when(pl.program_id(1) == 0)
        def _():
            o_ref[...] = jnp.zeros_like(o_ref) if init is None else i_ref[...]

        o_ref[...] += _mm_tn(a_ref[...], b_ref[...])

        @pl.when(pl.program_id(1) == nk - 1)
        def _():
            wire_ref[...] = o_ref[...].astype(WIRE_DTYPE)

    in_specs = [pl.BlockSpec((bt, bm), lambda i, k: (k, i)), pl.BlockSpec((bt, n), lambda i, k: (k, 0))]
    args = [a, b]
    if init is not None:
        in_specs.append(pl.BlockSpec((bm, n), lambda i, k: (i, 0)))
        args.append(init)
    oblk = pl.BlockSpec((bm, n), lambda i, k: (i, 0))
    return pl.pallas_call(
        body, name="tn_matmul", grid=(m // bm, nk),
        in_specs=in_specs, out_specs=[oblk, oblk],
        out_shape=[_sds((m, n)), _sds((m, n), WIRE_DTYPE)], compiler_params=_cparams(2),
    )(*args)


def _adamw(w, g, m, v):
    m = ADAM_B1 * m + (1.0 - ADAM_B1) * g
    v = ADAM_B2 * v + (1.0 - ADAM_B2) * (g * g)
    m_hat = m / (1.0 - ADAM_B1 ** ADAM_STEP)
    v_hat = v / (1.0 - ADAM_B2 ** ADAM_STEP)
    delta = -ADAM_LR * (m_hat / (jnp.sqrt(v_hat) + ADAM_EPS) + ADAM_WD * w)
    return delta, m, v


def adamw_sharded(own, recv, w, m, v, transposed=False):
    gshape, shape = own.shape, w.shape

    def body(own_ref, recv_ref, w_ref, m_ref, v_ref, g_ref, d_ref, mo_ref, vo_ref):
        g = own_ref[...]
        for k in range(N_DEV - 1):
            g = g + recv_ref[k].astype(F32)
        if transposed:
            g = g.T[:, 0:shape[1]]
        g_ref[...] = g
        d_ref[...], mo_ref[...], vo_ref[...] = _adamw(w_ref[...], g, m_ref[...], v_ref[...])

    return pl.pallas_call(
        body, name="adamw_sharded",
        in_specs=[_full(gshape), _full((N_DEV - 1,) + gshape), _full(shape), _full(shape), _full(shape)],
        out_specs=[_full(shape)] * 4, out_shape=[_sds(shape)] * 4, grid=(1,),
        compiler_params=_cparams(1),
    )(own, recv, w, m, v)


def adamw_small(items):
    n = len(items)
    flat = [a for it in items for a in it]

    def body(*refs):
        ins, outs = refs[:4 * n], refs[4 * n:]
        for i in range(n):
            g, w, m, v = (r[...] for r in ins[4 * i:4 * i + 4])
            outs[3 * i][...], outs[3 * i + 1][...], outs[3 * i + 2][...] = _adamw(w, g, m, v)

    out = pl.pallas_call(
        body, name="adamw_small", grid=(1,),
        in_specs=[_full(a.shape) for a in flat],
        out_specs=[_full(it[1].shape) for it in items for _ in range(3)],
        out_shape=[_sds(it[1].shape) for it in items for _ in range(3)],
        compiler_params=_cparams(1),
    )(*flat)
    return [tuple(out[3 * i:3 * i + 3]) for i in range(n)]


def _mesh_pos():
    x, y, c = lax.axis_index("x"), lax.axis_index("y"), lax.axis_index("c")
    me = 4 * x + 2 * y + c
    peers = []
    for k in range(1, N_DEV):
        peers.append(((1 - x) if (k >> 2) & 1 else x, (1 - y) if (k >> 1) & 1 else y, (1 - c) if k & 1 else c))
    return me, peers


def _all_gather_issue(buf, send_sems, recv_sems, me, peers):
    sends = []
    for k, peer in enumerate(peers):
        cp = pltpu.make_async_remote_copy(src_ref=buf.at[me], dst_ref=buf.at[me], send_sem=send_sems.at[k],
                                          recv_sem=recv_sems.at[k], device_id=peer, device_id_type=MESH)
        cp.start()
        sends.append(cp)
    return sends


def _all_gather_finish(buf, send_sems, recv_sems, me, peers, sends):
    for k, peer in enumerate(peers):
        src = jnp.bitwise_xor(me, k + 1)
        pltpu.make_async_remote_copy(src_ref=buf.at[src], dst_ref=buf.at[src], send_sem=send_sems.at[k],
                                     recv_sem=recv_sems.at[k], device_id=peer, device_id_type=MESH).wait_recv()
    for cp in sends:
        cp.wait_send()


def _all_gather(buf, send_sems, recv_sems, me, peers):
    _all_gather_finish(buf, send_sems, recv_sems, me, peers, _all_gather_issue(buf, send_sems, recv_sems, me, peers))


_VMEM = pl.BlockSpec(memory_space=pltpu.VMEM)
_ANY = pl.BlockSpec(memory_space=pl.ANY)
_SEMS = pltpu.SemaphoreType.DMA((N_DEV - 1,))


def mod_forward(c, c_ctx, w_mod_sh, b_mod, sm_pack, w_first):
    ncol = w_mod_sh.shape[1]

    def body(c_ref, cc_ref, w_ref, b_ref, sm_ref, wf_ref, mod_ref, s_ref, smt_ref, wall_ref, cbuf, pbuf, smbuf,
             s1, r1, s2, r2, s3, r3, ws, wr, wl):
        me, peers = _mesh_pos()
        x, y, cc = lax.axis_index("x"), lax.axis_index("y"), lax.axis_index("c")
        sibling = (x, y, 1 - cc)
        chips = [(1 - x, y), (x, 1 - y), (1 - x, 1 - y)]
        slot = lambda px, py, pc: wall_ref.at[4 * px + 2 * py + pc]

        def wcopy(k, block, to, src=None):
            return pltpu.make_async_remote_copy(
                src_ref=slot(*block) if src is None else src, dst_ref=slot(*block), send_sem=ws.at[k],
                recv_sem=wr.at[k], device_id=to, device_id_type=MESH)

        mine = pltpu.make_async_copy(wf_ref, slot(x, y, cc), wl)
        mine.start()
        first = [wcopy(0, (x, y, cc), sibling, src=wf_ref)]
        first += [wcopy(1 + j, (x, y, cc), (*chip, cc), src=wf_ref) for j, chip in enumerate(chips)]
        for cp in first:
            cp.start()
        smbuf[me] = sm_ref[...]
        sm_sends = _all_gather_issue(smbuf, s3, r3, me, peers)
        cbuf[me] = jnp.broadcast_to(c_ref[...], (8, D))
        _all_gather(cbuf, s1, r1, me, peers)
        rows = [cbuf[j, 0:1, :] for j in range(N_DEV)] + [cc_ref[...], jnp.zeros((7, D), F32)]
        sx = jnp.concatenate(rows, axis=0)
        s = sx * _sigmoid(sx)
        s_ref[...] = s
        pbuf[me] = _hi(s, w_ref[...])
        _all_gather(pbuf, s2, r2, me, peers)
        for j in range(N_DEV):
            mod_ref[:, j * ncol:(j + 1) * ncol] = pbuf[j] + b_ref[:, j * ncol:(j + 1) * ncol]
        _all_gather_finish(smbuf, s3, r3, me, peers, sm_sends)
        tot = smbuf[0]
        for j in range(1, N_DEV):
            tot = tot + smbuf[j]
        smt_ref[...] = tot
        passed = [wcopy(4 + j, (*chip, cc), sibling) for j, chip in enumerate(chips)]
        for j, chip in enumerate(chips):
            wcopy(1 + j, (*chip, cc), (x, y, cc)).wait_recv()
            passed[j].start()
        wcopy(0, (x, y, 1 - cc), (x, y, cc)).wait_recv()
        for j, chip in enumerate(chips):
            wcopy(4 + j, (*chip, 1 - cc), (x, y, cc)).wait_recv()
        for cp in first + passed:
            cp.wait_send()
        mine.wait()

    return pl.pallas_call(
        body, name="mod_forward",
        in_specs=[_VMEM] * 6, out_specs=[_VMEM] * 4,
        out_shape=[_sds((16, N_DEV * ncol)), _sds((16, D)), _sds(sm_pack.shape),
                   _sds((N_DEV,) + w_first.shape, w_first.dtype)],
        scratch_shapes=[pltpu.VMEM((N_DEV, 8, D), F32), pltpu.VMEM((N_DEV, 16, ncol), F32),
                        pltpu.VMEM((N_DEV,) + sm_pack.shape, F32), _SEMS, _SEMS, _SEMS, _SEMS, _SEMS, _SEMS,
                        _SEMS, _SEMS, pltpu.SemaphoreType.DMA],
        compiler_params=pltpu.CompilerParams(vmem_limit_bytes=VMEM_LIMIT),
    )(c, c_ctx, w_mod_sh, b_mod, sm_pack, w_first)


def sum_blocks(gat):
    def body(g_ref, tot_ref):
        tot = g_ref[0]
        for j in range(1, N_DEV):
            tot = tot + g_ref[j]
        tot_ref[...] = tot

    return pl.pallas_call(
        body, name="sum_blocks", in_specs=[_VMEM], out_specs=_VMEM, out_shape=_sds(gat.shape[1:]),
        compiler_params=pltpu.CompilerParams(vmem_limit_bytes=VMEM_LIMIT),
    )(gat)


def mod_backward(s, dm_sh, w, m, v, cc, m_cc, v_cc):
    shape = w.shape

    def body(s_ref, dm_ref, w_ref, m_ref, v_ref, cc_ref, mcc_ref, vcc_ref,
             gw_ref, dw_ref, mw_ref, vw_ref, gc_ref, dc_ref, mc_ref, vc_ref, pbuf, send_sems, recv_sems):
        me, peers = _mesh_pos()
        wv = w_ref[...]
        pbuf[me] = _hi_nt(dm_ref[8:16, :], wv)
        _all_gather(pbuf, send_sems, recv_sems, me, peers)
        g = _hi_tn(s_ref[...], dm_ref[...])
        gw_ref[...] = g
        dw_ref[...], mw_ref[...], vw_ref[...] = _adamw(wv, g, m_ref[...], v_ref[...])
        tot = pbuf[0]
        for j in range(1, N_DEV):
            tot = tot + pbuf[j]
        ccv = cc_ref[...]
        sg = _sigmoid(ccv)
        gc = tot[0:1, :] * (sg * (1.0 + ccv * (1.0 - sg)))
        gc_ref[...] = gc
        dc_ref[...], mc_ref[...], vc_ref[...] = _adamw(ccv, gc, mcc_ref[...], vcc_ref[...])

    return pl.pallas_call(
        body, name="mod_backward", in_specs=[_VMEM] * 8, out_specs=[_VMEM] * 8,
        out_shape=[_sds(shape)] * 4 + [_sds((1, D))] * 4,
        scratch_shapes=[pltpu.VMEM((N_DEV, 8, D), F32), _SEMS, _SEMS],
        compiler_params=pltpu.CompilerParams(vmem_limit_bytes=VMEM_LIMIT),
    )(s, dm_sh, w, m, v, cc, m_cc, v_cc)


_HBM = pl.BlockSpec(memory_space=pltpu.HBM)
_SEM = pl.BlockSpec(memory_space=pltpu.SEMAPHORE)
_EFFECT = pltpu.SideEffectType.DATAFLOW_SIDE_EFFECTING
_hbm = lambda a: pltpu.with_memory_space_constraint(a, pltpu.HBM)


def gather_start(shards, me, tag):
    n = len(shards)
    sems = pltpu.SemaphoreType.DMA((7 * n,))
    lands = [lax.dynamic_update_slice(lax.empty((N_DEV,) + s.shape, s.dtype), s[None], (me, 0, 0)) for s in shards]

    def body(*refs):
        s_refs, l_refs = refs[:n], refs[n:2 * n]
        send_sems, recv_sems = refs[2 * n], refs[2 * n + 1]
        token = refs[-1]
        my, peers = _mesh_pos()
        for w in range(n):
            for k, peer in enumerate(peers):
                pltpu.make_async_remote_copy(
                    src_ref=s_refs[w], dst_ref=l_refs[w].at[my], send_sem=send_sems.at[w * 7 + k],
                    recv_sem=recv_sems.at[w * 7 + k], device_id=peer, device_id_type=MESH).start()
        token[...] = jnp.zeros_like(token)

    out = pl.pallas_call(
        body, name="gather_start_" + tag,
        out_shape=(sems, sems) + tuple(pltpu.HBM(a.shape, a.dtype) for a in list(shards) + lands) + (_sds((8, 128)),),
        in_specs=(_HBM,) * (2 * n), out_specs=(_SEM, _SEM) + (_HBM,) * (2 * n) + (_VMEM,),
        input_output_aliases={i: i + 2 for i in range(2 * n)},
        compiler_params=pltpu.CompilerParams(has_side_effects=_EFFECT),
    )(*[_hbm(a) for a in list(shards) + lands])
    return out[0], out[1], list(out[2:2 + n]), list(out[2 + n:2 + 2 * n]), out[-1]


def gather_wait(send_sems, recv_sems, shards, lands, after, tag):
    n = len(shards)

    def body(*refs):
        s_refs, l_refs = refs[:n], refs[n:2 * n]
        send_sems, recv_sems = refs[2 * n], refs[2 * n + 1]
        my, peers = _mesh_pos()
        for w in range(n):
            for k, peer in enumerate(peers):
                src = jnp.bitwise_xor(my, k + 1)
                cp = pltpu.make_async_remote_copy(
                    src_ref=s_refs[w], dst_ref=l_refs[w].at[src], send_sem=send_sems.at[w * 7 + k],
                    recv_sem=recv_sems.at[w * 7 + k], device_id=peer, device_id_type=MESH)
                cp.wait_send()
                cp.wait_recv()

    out = pl.pallas_call(
        body, name="gather_wait_" + tag,
        out_shape=tuple(pltpu.HBM(a.shape, a.dtype) for a in list(shards) + list(lands)),
        in_specs=(_HBM,) * (2 * n) + (_SEM, _SEM, _ANY), out_specs=(_HBM,) * (2 * n),
        input_output_aliases={i: i for i in range(2 * n)},
        compiler_params=pltpu.CompilerParams(has_side_effects=_EFFECT),
    )(*shards, *lands, send_sems, recv_sems, after)
    return list(out[n:2 * n])


def scatter_start(grads, tag):
    n = len(grads)
    sems = pltpu.SemaphoreType.DMA((7 * n,))
    lands = [lax.empty((N_DEV - 1,) + g.shape[1:], g.dtype) for g in grads]

    def body(*refs):
        g_refs, l_refs = refs[:n], refs[n:2 * n]
        send_sems, recv_sems = refs[2 * n], refs[2 * n + 1]
        token = refs[-1]
        me, peers = _mesh_pos()
        for w in range(n):
            for k, peer in enumerate(peers):
                dst = jnp.bitwise_xor(me, k + 1)
                pltpu.make_async_remote_copy(
                    src_ref=g_refs[w].at[dst], dst_ref=l_refs[w].at[k], send_sem=send_sems.at[w * 7 + k],
                    recv_sem=recv_sems.at[w * 7 + k], device_id=peer, device_id_type=MESH).start()
        token[...] = jnp.zeros_like(token)

    out = pl.pallas_call(
        body, name="scatter_start_" + tag,
        out_shape=(sems, sems) + tuple(pltpu.HBM(a.shape, a.dtype) for a in list(grads) + lands) + (_sds((8, 128)),),
        in_specs=(_HBM,) * (2 * n), out_specs=(_SEM, _SEM) + (_HBM,) * (2 * n) + (_VMEM,),
        input_output_aliases={i: i + 2 for i in range(2 * n)},
        compiler_params=pltpu.CompilerParams(has_side_effects=_EFFECT),
    )(*[_hbm(a) for a in list(grads) + lands])
    return out[0], out[1], list(out[2:2 + n]), list(out[2 + n:2 + 2 * n]), out[-1]


def scatter_wait(send_sems, recv_sems, grads, lands, after, tag):
    n = len(grads)

    def body(*refs):
        g_refs, l_refs = refs[:n], refs[n:2 * n]
        send_sems, recv_sems = refs[2 * n], refs[2 * n + 1]
        me, peers = _mesh_pos()
        for w in range(n):
            for k, peer in enumerate(peers):
                dst = jnp.bitwise_xor(me, k + 1)
                cp = pltpu.make_async_remote_copy(
                    src_ref=g_refs[w].at[dst], dst_ref=l_refs[w].at[k], send_sem=send_sems.at[w * 7 + k],
                    recv_sem=recv_sems.at[w * 7 + k], device_id=peer, device_id_type=MESH)
                cp.wait_send()
                cp.wait_recv()

    out = pl.pallas_call(
        body, name="scatter_wait_" + tag,
        out_shape=tuple(pltpu.HBM(a.shape, a.dtype) for a in list(grads) + list(lands)),
        in_specs=(_HBM,) * (2 * n) + (_SEM, _SEM, _ANY), out_specs=(_HBM,) * (2 * n),
        input_output_aliases={i: i for i in range(2 * n)},
        compiler_params=pltpu.CompilerParams(has_side_effects=_EFFECT),
    )(*grads, *lands, send_sems, recv_sems, after)
    return list(out[n:2 * n])


def _vec8(rows, width):
    rid = lax.broadcasted_iota(jnp.int32, (8, width), 0)
    out = jnp.zeros((8, width), F32)
    for i, r in enumerate(rows):
        r = r.reshape(-1)
        r = jnp.pad(r, (0, width - r.shape[0]))
        out = jnp.where(rid == i, r[None, :], out)
    return out


def local_step(x, ctx, tgt, mod, mod_c, small, w_int, start, late_weights, grads_ready, small_ready, tt, tt_ctx, cb,
               cb_ctx):
    sh1, sc1, g1, sh2, sc2, g2 = [mod[i * D:(i + 1) * D] for i in range(6)]
    csh1, csc1 = mod_c[0:D], mod_c[D:2 * D]
    vec1 = _vec8([small["norm1_g"], sh1, sc1], D)
    vec1c = _vec8([small["norm1_g"], csh1, csc1], D)
    vec2 = _vec8([small["norm2_g"], sh2, sc2], D)
    vec3 = _vec8([g2, small["final_g"]], D)
    vecm = _vec8([g1, small["norm2_g"], sh2, sc2], D)
    vcm = _vec8([jnp.tile(small["gla_norm_g"].reshape(HV), NH), small["conv_b"], small["conv_ln_g"],
                 small["conv_ln_b"]], DC)
    convw = jnp.pad(small["conv_w"], ((0, 1), (0, 0)))
    wa = jnp.zeros((128, 512), F32)
    wa = wa.at[0:RANK, 0:DK].set(small["w_a2_f"]).at[RANK:2 * RANK, DK:2 * DK].set(small["w_a2_b"])
    ba = jnp.concatenate([small["b_a_f"].reshape(1, DK), small["b_a_b"].reshape(1, DK)], axis=1)

    _, _, kc, vc_, _, rc, lac, hc = proj_fwd(ctx, vec1c + start, w_int, wa, ba, tt_ctx)
    qc0 = jnp.zeros_like(kc)
    _, _, sallf_c, sallb_c, sfin_c = gla_fwd(qc0, kc, vc_, lac, jnp.zeros((2, HV, DK), F32), cb_ctx)
    u, q, k, v, g, r, la, h = proj_fwd(x, vec1, w_int, wa, ba, tt)
    o_f, o_b, sall_f, sall_b, _ = gla_fwd(q, k, v, la, sfin_c, cb)
    w_out, wg_t, wu_t, w_down = late_weights(o_b)
    x1, cat, mix, yc, h2 = merge_fwd(u, g, o_f, o_b, x, vecm, vcm, convw, w_out, tt)
    tt2 = min(2 * tt, x.shape[0])
    act, dact, hid = ffn_gate_up(h2, wg_t, wu_t, tt2)
    dx2, dff, acc3 = ffn_down_loss(hid, x1, tgt, vec3, w_down, tt)
    dgt, dup = ffn_dhid(dff, act, dact, w_down, tt2)
    dx1, acc2 = ffn_dh2(dgt, dup, x1, dx2, vec2, wg_t, wu_t, tt)
    bt = min(2048, x.shape[0])
    gw = {"w_down": tn_matmul(hid, dff, FN, bt), "wg_t": tn_matmul(dgt, h2, FN, bt),
          "wu_t": tn_matmul(dup, h2, FN, bt)}
    vecm = vecm + grads_ready(("wg_t", "wu_t", "w_down"), gw)
    du, dg, do, dmix, accm1, accm2, dconvw = merge_bwd(dx1, mix, u, g, o_f, o_b, yc, vecm, vcm, convw, w_out, tt)
    gw["w_out"] = tn_matmul(cat, dmix, 512, bt)
    dsfin = jnp.zeros((2, HV, DK), F32) + grads_ready(("w_out",), gw)
    dqf, dkf, dvf, dlaf, dqb, dkb, dvb, dlab, ds0 = gla_bwd(q, k, v, la, do, sall_f, sall_b, dsfin, cb)
    gx, dp, acc1, dba, dwa = proj_bwd(du, (dqf, dqb, dkf, dkb, dvf, dvb), dg, dlaf, dlab, la, r, x, dx1, vec1, w_int,
                                      wa, tt)
    tcx = ctx.shape[0]
    zc = lambda w, dt=MXU_DTYPE: jnp.zeros((tcx, w), dt)
    _, dkf, dvf, dlaf, _, dkb, dvb, dlab, _ = gla_bwd(qc0, kc, vc_, lac, zc(DV), sallf_c, sallb_c, ds0, cb_ctx)
    _, dpc, acc1c, dbac, dwac = proj_bwd(zc(1024), (zc(DK), zc(DK), dkf, dkb, dvf, dvb), zc(DV), dlaf, dlab, lac, rc,
                                         ctx, zc(D, F32), vec1c, w_int, wa, tt_ctx)
    dwa_t = dwa + dwac
    dba_t = dba + dbac
    gs = {
        "norm1_g": acc1[2] + acc1c[2], "norm2_g": acc2[2], "final_g": acc3[1], "loss": acc3[2],
        "gla_norm_g": accm2[0], "conv_b": accm2[1], "conv_ln_g": accm2[2], "conv_ln_b": accm2[3],
        "conv_w": dconvw, "b_a": dba_t[0], "w_a2": dwa_t,
    }
    dmod = _vec8([acc1[0], acc1[1], accm1[0], acc2[0], acc2[1], acc3[0]], D)
    dmod_c = _vec8([acc1c[0], acc1c[1]], D)
    dpc = dpc + small_ready(gs, dmod, dmod_c).astype(dpc.dtype)
    btc = min(1024, tcx)
    gw["w_int"] = tn_matmul(dp, h, 896, bt, init=tn_matmul(dpc, hc, 896, btc)[0])
    grads_ready(("w_int",), gw)
    return gx, gw


PACK_ROWS = 96
ROW_N1, ROW_N2, ROW_FG, ROW_LOSS, ROW_GN, ROW_CB, ROW_LG, ROW_LB, ROW_BA = 0, 1, 2, 3, 4, 5, 6, 7, 8
ROW_DMOD, ROW_DMODC, ROW_CW, ROW_WA = 16, 24, 32, 64


def _pack_small(gs, dmod, dmod_c):
    pad = lambda a: jnp.pad(a, ((0, 0), (0, D - a.shape[1])))
    singles = _vec8([gs["norm1_g"], gs["norm2_g"], gs["final_g"], gs["loss"], gs["gla_norm_g"], gs["conv_b"],
                     gs["conv_ln_g"], gs["conv_ln_b"]], D)
    return jnp.concatenate([singles, _vec8([gs["b_a"]], D), dmod, dmod_c, pad(gs["conv_w"]), pad(gs["w_a2"][0:32])],
                           axis=0)


def kernel(x, c, ctx, c_ctx, w_mod, b_mod, norm1_g, norm2_g, w_in, conv_w, conv_b, conv_ln_g, conv_ln_b, w_a2_f, b_a_f, w_a2_b, b_a_b, gla_norm_g, w_out, w_gate, w_up, w_down, final_g, loss_target, m_c_ctx, m_w_mod, m_b_mod, m_norm1_g, m_norm2_g, m_w_in, m_conv_w, m_conv_b, m_conv_ln_g, m_conv_ln_b, m_w_a2_f, m_b_a_f, m_w_a2_b, m_b_a_b, m_gla_norm_g, m_w_out, m_w_gate, m_w_up, m_w_down, m_final_g, v_c_ctx, v_w_mod, v_b_mod, v_norm1_g, v_norm2_g, v_w_in, v_conv_w, v_conv_b, v_conv_ln_g, v_conv_ln_b, v_w_a2_f, v_b_a_f, v_w_a2_b, v_b_a_b, v_gla_norm_g, v_w_out, v_w_gate, v_w_up, v_w_down, v_final_g):
    me = 4 * lax.axis_index("x") + 2 * lax.axis_index("y") + lax.axis_index("c")
    t = x.shape[1]
    tcx = ctx.shape[1]
    r_in, r_out, r_ff = w_in.shape[2], w_out.shape[1], w_gate.shape[2]
    r_in_b = -(-r_in // 16) * 16

    tb = lambda w: w.T.astype(MXU_DTYPE)

    small = dict(norm1_g=norm1_g[0], norm2_g=norm2_g[0], final_g=final_g, gla_norm_g=gla_norm_g[0],
                 conv_b=conv_b[0], conv_ln_g=conv_ln_g[0], conv_ln_b=conv_ln_b[0], b_a_f=b_a_f[0], b_a_b=b_a_b[0])
    sm_pack = jnp.zeros((48, DC), F32)
    sm_pack = lax.dynamic_update_slice(sm_pack, conv_w[0], (0, me * (DC // N_DEV)))
    sm_pack = lax.dynamic_update_slice(sm_pack, w_a2_f[0], (32, me * (DK // N_DEV)))
    sm_pack = lax.dynamic_update_slice(sm_pack, w_a2_b[0], (32, DK + me * (DK // N_DEV)))

    mod_all, s_all, sm_tot, wall = mod_forward(c, c_ctx.reshape(1, D), w_mod[0], b_mod, sm_pack,
                                               jnp.pad(tb(w_in[0]), ((0, r_in_b - r_in), (0, 0))))
    mod = lax.dynamic_slice(mod_all, (me, 0), (1, 6 * D)).reshape(6 * D)
    mod_c = mod_all[8]
    small["conv_w"] = sm_tot[0:CW, :]
    small["w_a2_f"] = sm_tot[32:32 + RANK, 0:DK]
    small["w_a2_b"] = sm_tot[32:32 + RANK, DK:2 * DK]

    w_int =jnp.pad(wall[:, 0:r_in, :].reshape(N_DEV * r_in, D), ((0, DINP - DIN), (0, 0)))
    after_w_in = (wall[0:1, 0:1, 0] * 0).astype(MXU_DTYPE)
    late = [w_out[0].astype(MXU_DTYPE) + after_w_in, tb(w_gate[0]) + after_w_in, tb(w_up[0]) + after_w_in,
            w_down[0].astype(MXU_DTYPE) + after_w_in]
    g_send, g_recv, late_thru, late_lands, g_token = gather_start(late, me, "late")

    def late_weights(after):
        got = gather_wait(g_send, g_recv, late_thru, late_lands, after, "late")
        return tuple(a.reshape(N_DEV * a.shape[1], D) for a in got)

    pad_in = lambda g: jnp.pad(g[0:DIN].reshape(N_DEV, r_in, D), ((0, 0), (0, r_in_b - r_in), (0, 0)))
    blocked = {"w_int": pad_in, "w_out": lambda g: g.reshape(N_DEV, r_out, D)}
    as_blocks = lambda n, g: blocked.get(n, lambda a: a.reshape(N_DEV, r_ff, D))(g)
    pending = []

    def grads_ready(names, gw_now):
        blocks = [as_blocks(n, gw_now[n][1]) for n in names]
        if names[0] == "w_int":
            done = finish_small(gw_now["w_int"][0]).astype(WIRE_DTYPE)
            blocks = [b + done for b in blocks]
        send, recv_s, thru, zones, token = scatter_start(blocks, names[0])
        pending.append((names, send, recv_s, thru, zones))
        return token[0:1, 0:1]

    sm = {}

    def small_ready(gs, dmod, dmod_c):
        sm["copy"] = gather_start([_pack_small(gs, dmod, dmod_c)], me, "small")
        return sm["copy"][4][0:1, 0:1]

    def finish_small(after):
        send, recv_s, thru, zones, _ = sm["copy"]
        gat = gather_wait(send, recv_s, thru, zones, after, "small")[0]
        sm["tot"] = sum_blocks(gat)
        sm["dm"] = jnp.concatenate(
            [gat[:, ROW_DMOD:ROW_DMOD + 6, :].reshape(N_DEV, 6 * D),
             jnp.pad(sm["tot"][ROW_DMODC:ROW_DMODC + 6, :].reshape(1, 6 * D), ((0, 7), (0, 0)))], axis=0)
        ncol = w_mod.shape[2]
        dm_sh = lax.dynamic_slice(sm["dm"], (0, me * ncol), (16, ncol))
        sm["mod"] = mod_backward(s_all, dm_sh, w_mod[0], m_w_mod[0], v_w_mod[0], c_ctx.reshape(1, D),
                                 m_c_ctx.reshape(1, D), v_c_ctx.reshape(1, D))
        return sm["mod"][4][0:1, 0:1] * 0

    gx, gw = local_step(x[0], ctx[0], loss_target[0], mod, mod_c, small, w_int, g_token[0:1, 0:1], late_weights,
                        grads_ready, small_ready, 512, 256, 8, 4)
    tot, dm = sm["tot"], sm["dm"]
    loss = jnp.sum(tot[ROW_LOSS])
    g_wmod, d_wmod, nm_wmod, nv_wmod, g_cc, d_cc, nm_cc, nv_cc = sm["mod"]

    recv = {}

    def wait_for(entry, after):
        names, send, recv_s, thru, zones = entry
        recv.update(dict(zip(names, scatter_wait(send, recv_s, thru, zones, after, names[0]))))

    for entry in pending[:-1]:
        wait_for(entry, tot)
    own = {n: lax.dynamic_index_in_dim(as_blocks(n, gw[n][0]), me, 0, keepdims=False) for n in gw if n != "w_int"}
    own["w_int"] = jnp.pad(lax.dynamic_slice(gw["w_int"][0], (me * r_in, 0), (r_in, D)), ((0, r_in_b - r_in), (0, 0)))
    big = {}
    big["w_gate"] = adamw_sharded(own["wg_t"], recv["wg_t"], w_gate[0], m_w_gate[0], v_w_gate[0], transposed=True)
    big["w_up"] = adamw_sharded(own["wu_t"], recv["wu_t"], w_up[0], m_w_up[0], v_w_up[0], transposed=True)
    big["w_down"] = adamw_sharded(own["w_down"], recv["w_down"], w_down[0], m_w_down[0], v_w_down[0])
    big["w_out"] = adamw_sharded(own["w_out"], recv["w_out"], w_out[0], m_w_out[0], v_w_out[0])
    wait_for(pending[-1], big["w_out"][0])
    big["w_in"] = adamw_sharded(own["w_int"], recv["w_int"], w_in[0], m_w_in[0], v_w_in[0], transposed=True)
    big["w_mod"] = [g_wmod, d_wmod, nm_wmod, nv_wmod]

    row = lambda r, w: tot[r:r + 1, 0:w]
    gn_row = tot[ROW_GN:ROW_GN + 1, 0:DC]
    g_small = {
        "b_mod": jnp.sum(dm, axis=0, keepdims=True),
        "norm1_g": row(ROW_N1, D), "norm2_g": row(ROW_N2, D),
        "conv_w": lax.dynamic_slice(tot, (ROW_CW, me * (DC // N_DEV)), (CW, DC // N_DEV)),
        "conv_b": row(ROW_CB, DC), "conv_ln_g": row(ROW_LG, DC), "conv_ln_b": row(ROW_LB, DC),
        "w_a2_f": lax.dynamic_slice(tot, (ROW_WA, me * (DK // N_DEV)), (RANK, DK // N_DEV)),
        "b_a_f": tot[ROW_BA:ROW_BA + 1, 0:DK],
        "w_a2_b": lax.dynamic_slice(tot, (ROW_WA + RANK, DK + me * (DK // N_DEV)), (RANK, DK // N_DEV)),
        "b_a_b": tot[ROW_BA:ROW_BA + 1, DK:2 * DK],
        "gla_norm_g": gn_row[:, 0:HV] + gn_row[:, HV:2 * HV] + gn_row[:, 2 * HV:3 * HV] + gn_row[:, 3 * HV:4 * HV],
        "final_g": row(ROW_FG, D),
    }
    wmv = {
        "b_mod": (b_mod, m_b_mod, v_b_mod), "norm1_g": (norm1_g, m_norm1_g, v_norm1_g),
        "norm2_g": (norm2_g, m_norm2_g, v_norm2_g), "conv_w": (conv_w[0], m_conv_w[0], v_conv_w[0]),
        "conv_b": (conv_b, m_conv_b, v_conv_b), "conv_ln_g": (conv_ln_g, m_conv_ln_g, v_conv_ln_g),
        "conv_ln_b": (conv_ln_b, m_conv_ln_b, v_conv_ln_b), "w_a2_f": (w_a2_f[0], m_w_a2_f[0], v_w_a2_f[0]),
        "b_a_f": (b_a_f, m_b_a_f, v_b_a_f), "w_a2_b": (w_a2_b[0], m_w_a2_b[0], v_w_a2_b[0]),
        "b_a_b": (b_a_b, m_b_a_b, v_b_a_b), "gla_norm_g": (gla_norm_g, m_gla_norm_g, v_gla_norm_g),
        "final_g": (final_g.reshape(1, D), m_final_g.reshape(1, D), v_final_g.reshape(1, D)),
    }
    names_small = list(g_small)
    upd = adamw_small([(g_small[n],) + wmv[n] for n in names_small])
    res = {n: (g_small[n],) + upd[i] for i, n in enumerate(names_small)}
    res["c_ctx"] = (g_cc, d_cc, nm_cc, nv_cc)
    for n in ("w_mod", "w_in", "w_out", "w_gate", "w_up", "w_down"):
        res[n] = tuple(big[n])

    order = ["c_ctx", "w_mod", "b_mod", "norm1_g", "norm2_g", "w_in", "conv_w", "conv_b", "conv_ln_g", "conv_ln_b",
             "w_a2_f", "b_a_f", "w_a2_b", "b_a_b", "gla_norm_g", "w_out", "w_gate", "w_up", "w_down", "final_g"]
    shapes = {"c_ctx": c_ctx.shape, "w_mod": w_mod.shape, "b_mod": b_mod.shape, "norm1_g": norm1_g.shape,
              "norm2_g": norm2_g.shape, "w_in": w_in.shape, "conv_w": conv_w.shape, "conv_b": conv_b.shape,
              "conv_ln_g": conv_ln_g.shape, "conv_ln_b": conv_ln_b.shape, "w_a2_f": w_a2_f.shape,
              "b_a_f": b_a_f.shape, "w_a2_b": w_a2_b.shape, "b_a_b": b_a_b.shape, "gla_norm_g": gla_norm_g.shape,
              "w_out": w_out.shape, "w_gate": w_gate.shape, "w_up": w_up.shape, "w_down": w_down.shape,
              "final_g": final_g.shape}
    outs = [loss, gx.reshape(x.shape)]
    for i in range(4):
        outs += [res[n][i].reshape(shapes[n]) for n in order]
    return tuple(outs)
```

```python
import functools

import jax
import jax.numpy as jnp
from jax import lax
from jax.experimental import pallas as pl
from jax.experimental.pallas import tpu as pltpu

F32 = jnp.float32
MXU_DTYPE = jnp.bfloat16
WIRE_DTYPE = jnp.bfloat16
HI = lax.Precision.HIGHEST
MESH = pl.DeviceIdType.MESH

N_DEV = 8
D = 1024
DC = 512
NH = 4
HK = 64
HV = 128
DK = NH * HK
DV = NH * HV
RANK = 16
CHUNK = 64
SEG = 64
CW = 31
CPAD = 15
DFF = 2816
DIN = 2592
DINP = 2688
TAU = 16.0
EPS = 1e-6
VMEM_LIMIT = 56 * 1024 * 1024

ADAM_LR = 0.001
ADAM_B1 = 0.9
ADAM_B2 = 0.999
ADAM_EPS = 1e-08
ADAM_WD = 0.01
ADAM_STEP = 10


def _mm(a, b):
    return jnp.dot(a.astype(MXU_DTYPE), b.astype(MXU_DTYPE), preferred_element_type=F32)


def _mm_nt(a, b):
    return lax.dot_general(a.astype(MXU_DTYPE), b.astype(MXU_DTYPE), (((1,), (1,)), ((), ())),
                           preferred_element_type=F32)


def _mm_tn(a, b):
    return lax.dot_general(a.astype(MXU_DTYPE), b.astype(MXU_DTYPE), (((0,), (0,)), ((), ())),
                           preferred_element_type=F32)


def _hi(a, b):
    return jnp.dot(a, b, precision=HI, preferred_element_type=F32)


def _hi_nt(a, b):
    return lax.dot_general(a, b, (((1,), (1,)), ((), ())), precision=HI, preferred_element_type=F32)


def _hi_tn(a, b):
    return lax.dot_general(a, b, (((0,), (0,)), ((), ())), precision=HI, preferred_element_type=F32)


def _sigmoid(x):
    return 1.0 / (1.0 + jnp.exp(-x))


def _cparams(n_axes):
    return pltpu.CompilerParams(dimension_semantics=("arbitrary",) * n_axes, vmem_limit_bytes=VMEM_LIMIT)


def _full(shape):
    n = len(shape)
    return pl.BlockSpec(shape, lambda *_: (0,) * n)


def _rows(tt, width):
    return pl.BlockSpec((tt, width), lambda i: (i, 0))


def _sds(shape, dtype=F32):
    return jax.ShapeDtypeStruct(shape, dtype)


def _norm_mod(x, g, sh, sc):
    r = lax.rsqrt(jnp.mean(x * x, axis=-1, keepdims=True) + EPS)
    xn = x * r
    yy = xn * g
    return r, xn, yy, yy * (1.0 + sc) + sh


def _norm_mod_bwd(dh, r, xn, yy, g, sc):
    dsh = jnp.sum(dh, axis=0, keepdims=True)
    dsc = jnp.sum(dh * yy, axis=0, keepdims=True)
    dy = dh * (1.0 + sc)
    dg = jnp.sum(dy * xn, axis=0, keepdims=True)
    dxn = dy * g
    dx = r * (dxn - xn * jnp.mean(dxn * xn, axis=-1, keepdims=True))
    return dsh, dsc, dg, dx


def _zero_first(*refs):
    @pl.when(pl.program_id(0) == 0)
    def _():
        for r in refs:
            r[...] = jnp.zeros_like(r)


def _acc_rows(ref, rows):
    ref[...] += jnp.concatenate(rows + [jnp.zeros((8 - len(rows), rows[0].shape[1]), F32)], axis=0)


def proj_fwd(x, vec, w_int, wa, ba, tt):
    t = x.shape[0]

    def body(x_ref, vec_ref, w_ref, wa_ref, ba_ref, u_ref, q_ref, k_ref, v_ref, g_ref, r_ref, la_ref, h_ref):
        for rows in (slice(0, tt // 2), slice(tt // 2, tt)):
            _, _, _, h = _norm_mod(x_ref[rows, :], vec_ref[0:1, :], vec_ref[1:2, :], vec_ref[2:3, :])
            hb = h.astype(MXU_DTYPE)
            h_ref[rows, :] = hb
            p = _mm_nt(hb, w_ref[...])
            u_ref[rows, :] = p[:, 0:1024]
            q_ref[rows, :] = p[:, 1024:1280]
            k_ref[rows, :] = p[:, 1280:1536]
            v_ref[rows, :] = p[:, 1536:2048]
            g_ref[rows, :] = p[:, 2048:2560]
            rr = p[:, 2560:2688]
            r_ref[rows, :] = rr
            z = _mm(rr, wa_ref[...]) + ba_ref[...]
            la_ref[rows, :] = (jnp.minimum(z, 0.0) - jnp.log(1.0 + jnp.exp(-jnp.abs(z)))) * (1.0 / TAU)

    return pl.pallas_call(
        body, name="proj_fwd", grid=(t // tt,),
        in_specs=[_rows(tt, D), _full((8, D)), _full((DINP, D)), _full((128, 512)), _full((1, 512))],
        out_specs=[_rows(tt, 1024), _rows(tt, DK), _rows(tt, DK), _rows(tt, DV), _rows(tt, DV), _rows(tt, 128),
                   _rows(tt, 512), _rows(tt, D)],
        out_shape=[_sds((t, 1024)), _sds((t, DK)), _sds((t, DK)), _sds((t, DV)), _sds((t, DV)), _sds((t, 128)),
                   _sds((t, 512)), _sds((t, D), MXU_DTYPE)],
        compiler_params=_cparams(1),
    )(x, vec, w_int, wa, ba)


def proj_bwd(du, dqkv, dg, dla_f, dla_b, la, r, x, dx1, vec, w_int, wa, tt):
    t = x.shape[0]

    def body(du_ref, dqf_ref, dqb_ref, dkf_ref, dkb_ref, dvf_ref, dvb_ref, dg_ref, dlaf_ref, dlab_ref, la_ref, r_ref,
             x_ref, dx1_ref, vec_ref, w_ref, wa_ref, gx_ref, dp_ref, acc_ref, dba_ref, dwa_ref):
        _zero_first(acc_ref, dba_ref, dwa_ref)
        md = lambda a: a.astype(MXU_DTYPE)
        g, sc = vec_ref[0:1, :], vec_ref[2:3, :]
        sums, dba, dwa = None, None, None
        for rows in (slice(0, tt // 2), slice(tt // 2, tt)):
            both = lambda a_ref, b_ref: md(a_ref[rows, :].astype(F32) + b_ref[rows, :].astype(F32))
            dla = jnp.concatenate([dlaf_ref[rows, :], dlab_ref[rows, :]], axis=1)
            dz = dla * (1.0 - jnp.exp(TAU * la_ref[rows, :])) * (1.0 / TAU)
            rr = r_ref[rows, :]
            dba_h = jnp.sum(dz, axis=0, keepdims=True)
            dwa_h = _mm_tn(rr, dz)
            dr = _mm_nt(dz, wa_ref[...])
            dp = jnp.concatenate([du_ref[rows, :], both(dqf_ref, dqb_ref), both(dkf_ref, dkb_ref),
                                  both(dvf_ref, dvb_ref), dg_ref[rows, :], md(dr)], axis=1)
            dp_ref[rows, :] = dp
            dh = _mm(dp, w_ref[...])
            rn, xn, yy, _ = _norm_mod(x_ref[rows, :], g, vec_ref[1:2, :], sc)
            dsh, dsc, dgn, dx = _norm_mod_bwd(dh, rn, xn, yy, g, sc)
            gx_ref[rows, :] = dx1_ref[rows, :] + dx
            part = [dsh, dsc, dgn]
            sums = part if sums is None else [a + b for a, b in zip(sums, part)]
            dba = dba_h if dba is None else dba + dba_h
            dwa = dwa_h if dwa is None else dwa + dwa_h
        _acc_rows(dba_ref, [dba])
        dwa_ref[...] += dwa
        _acc_rows(acc_ref, sums)

    return pl.pallas_call(
        body, name="proj_bwd", grid=(t // tt,),
        in_specs=[_rows(tt, 1024), _rows(tt, DK), _rows(tt, DK), _rows(tt, DK), _rows(tt, DK), _rows(tt, DV),
                  _rows(tt, DV), _rows(tt, DV), _rows(tt, DK), _rows(tt, DK), _rows(tt, 512),
                  _rows(tt, 128), _rows(tt, D), _rows(tt, D), _full((8, D)), _full((DINP, D)), _full((128, 512))],
        out_specs=[_rows(tt, D), _rows(tt, DINP), _full((8, D)), _full((8, 512)), _full((128, 512))],
        out_shape=[_sds((t, D)), _sds((t, DINP), MXU_DTYPE), _sds((8, D)), _sds((8, 512)), _sds((128, 512))],
        compiler_params=_cparams(1),
    )(du, *dqkv, dg, dla_f, dla_b, la, r, x, dx1, vec, w_int, wa)


def _dot_exact01(m01, x):
    bf = jnp.bfloat16
    w = x.shape[1]
    hi = x.astype(bf)
    r1 = x - hi.astype(F32)
    mid = r1.astype(bf)
    lo = (r1 - mid.astype(F32)).astype(bf)
    y = jnp.dot(m01.astype(bf), jnp.concatenate([hi, mid, lo], axis=1), preferred_element_type=F32)
    return y[:, 0:w] + y[:, w:2 * w] + y[:, 2 * w:3 * w]


def _gla_chunk(d, qc, kc, la_c):
    row = lax.broadcasted_iota(jnp.int32, (CHUNK, CHUNK), 0)
    col = lax.broadcasted_iota(jnp.int32, (CHUNK, CHUNK), 1)
    cum = ((col <= row) if d == 0 else (col >= row)).astype(F32)
    cum_t = ((col >= row) if d == 0 else (col <= row)).astype(F32)
    cum4 = jnp.concatenate([cum] * NH, axis=0)
    head_of_lane = lax.broadcasted_iota(jnp.int32, (1, DK), 1) // HK
    b = _dot_exact01(cum, la_c)
    bl = jnp.sum(la_c, axis=0, keepdims=True)
    eb = jnp.exp(b)
    enb = jnp.exp(-b)
    ekd = jnp.exp(bl - b)
    qt = qc * (HK ** -0.5) * eb
    kt = kc * enb
    kd = kc * ekd
    qst = jnp.concatenate([jnp.where(head_of_lane == h, qt, 0.0) for h in range(NH)], axis=0)
    a = _mm_nt(qst, kt) * cum4
    return cum_t, cum4, head_of_lane, eb, enb, ekd, qt, kt, kd, qst, a, jnp.exp(bl)


def gla_fwd(q, k, v, la, s0, cb):
    t = q.shape[0]
    nc = t // CHUNK
    nb = nc // cb

    def body(qf_ref, kf_ref, vf_ref, laf_ref, qb_ref, kb_ref, vb_ref, lab_ref, s0_ref,
             of_ref, ob_ref, sf_ref, sb_ref, sfin_ref, s_scr):
        i = pl.program_id(0)

        @pl.when(i == 0)
        def _():
            s_scr[...] = s0_ref[...]

        def chunk(d, jj, q_ref, k_ref, v_ref, la_ref, o_ref, sall_ref):
            rows = slice(jj * CHUNK, (jj + 1) * CHUNK)
            vc = v_ref[rows, :]
            _, _, head_of_lane, _, _, _, _, _, kd, qst, a, dec = _gla_chunk(
                d, q_ref[rows, :], k_ref[rows, :], la_ref[rows, :])
            s = s_scr[d]
            sall_ref[jj] = s
            inter = _mm_nt(qst, s)
            outs = []
            for h in range(NH):
                hs = slice(h * CHUNK, (h + 1) * CHUNK)
                outs.append(_mm(a[hs], vc[:, h * HV:(h + 1) * HV]) + inter[hs])
            o_ref[rows, :] = jnp.concatenate(outs, axis=1)
            kv = _mm_tn(vc, kd)
            s_new = dec * s
            for h in range(NH):
                s_new = s_new + jnp.where(head_of_lane == h, kv[h * HV:(h + 1) * HV], 0.0)
            s_scr[d] = s_new

        for j in range(cb):
            chunk(0, j, qf_ref, kf_ref, vf_ref, laf_ref, of_ref, sf_ref)
            chunk(1, cb - 1 - j, qb_ref, kb_ref, vb_ref, lab_ref, ob_ref, sb_ref)

        @pl.when(i == nb - 1)
        def _():
            sfin_ref[...] = s_scr[...]

    tb = cb * CHUNK
    fwd = lambda w, c=0: pl.BlockSpec((tb, w), lambda i: (i, c))
    bwd = lambda w, c=0: pl.BlockSpec((tb, w), lambda i: (nb - 1 - i, c))
    return pl.pallas_call(
        body, name="gla_fwd", grid=(nb,),
        in_specs=[fwd(DK), fwd(DK), fwd(DV), fwd(DK, 0), bwd(DK), bwd(DK), bwd(DV), bwd(DK, 1), _full((2, HV, DK))],
        out_specs=[fwd(DV), bwd(DV), pl.BlockSpec((cb, HV, DK), lambda i: (i, 0, 0)),
                   pl.BlockSpec((cb, HV, DK), lambda i: (nb - 1 - i, 0, 0)), _full((2, HV, DK))],
        out_shape=[_sds((t, DV)), _sds((t, DV)), _sds((nc, HV, DK)), _sds((nc, HV, DK)), _sds((2, HV, DK))],
        scratch_shapes=[pltpu.VMEM((2, HV, DK), F32)],
        compiler_params=_cparams(1),
    )(q, k, v, la, q, k, v, la, s0)


def gla_bwd(q, k, v, la, do, sall_f, sall_b, dsfin, cb):
    t = q.shape[0]
    nc = t // CHUNK
    nb = nc // cb

    def body(qf_ref, kf_ref, vf_ref, laf_ref, dof_ref, sf_ref, qb_ref, kb_ref, vb_ref, lab_ref, dob_ref, sb_ref,
             dsfin_ref, dqf_ref, dkf_ref, dvf_ref, dlaf_ref, dqb_ref, dkb_ref, dvb_ref, dlab_ref, ds0_ref, ds_scr):
        i = pl.program_id(0)

        @pl.when(i == 0)
        def _():
            ds_scr[...] = dsfin_ref[...]

        def chunk(d, jj, q_ref, k_ref, v_ref, la_ref, do_ref, sall_ref, dq_ref, dk_ref, dv_ref, dla_ref):
            rows = slice(jj * CHUNK, (jj + 1) * CHUNK)
            vc = v_ref[rows, :]
            doc = do_ref[rows, :]
            cum_t, cum4, head_of_lane, eb, enb, ekd, qt, kt, kd, qst, a, dec = _gla_chunk(
                d, q_ref[rows, :], k_ref[rows, :], la_ref[rows, :])
            s = sall_ref[jj]
            ds = ds_scr[d]
            hv = lambda x, h: x[:, h * HV:(h + 1) * HV]
            hr = lambda x, h: x[h * CHUNK:(h + 1) * CHUNK]
            fold = lambda x: functools.reduce(
                lambda p, c: p + c, [jnp.where(head_of_lane == h, hr(x, h), 0.0) for h in range(NH)])
            dost = jnp.concatenate([hv(doc, h) for h in range(NH)], axis=0)
            vst = jnp.concatenate([hv(vc, h) for h in range(NH)], axis=0)
            da = jnp.concatenate([_mm_nt(hv(doc, h), hv(vc, h)) for h in range(NH)], axis=0) * cum4
            dqt = fold(_mm(da, kt) + _mm(dost, s))
            dkt = _mm_tn(da, qst)
            kdst = jnp.concatenate([jnp.where(head_of_lane == h, kd, 0.0) for h in range(NH)], axis=0)
            dv_inter = _mm_nt(kdst, ds)
            dv_ref[rows, :] = jnp.concatenate(
                [_mm_tn(hr(a, h), hv(doc, h)) + hr(dv_inter, h) for h in range(NH)], axis=1).astype(MXU_DTYPE)
            dkd = fold(_mm(vst, ds))
            ds_scr[d] = dec * ds + _mm_tn(dost, qst)
            tkd = dkd * kd
            db = dqt * qt - dkt * kt - tkd
            dbl = jnp.sum(ds * s, axis=0, keepdims=True) * dec + jnp.sum(tkd, axis=0, keepdims=True)
            dla_ref[rows, :] = _dot_exact01(cum_t, db) + dbl
            dq_ref[rows, :] = (dqt * eb * (HK ** -0.5)).astype(MXU_DTYPE)
            dk_ref[rows, :] = (dkt * enb + dkd * ekd).astype(MXU_DTYPE)

        for j in range(cb):
            chunk(0, cb - 1 - j, qf_ref, kf_ref, vf_ref, laf_ref, dof_ref, sf_ref, dqf_ref, dkf_ref, dvf_ref, dlaf_ref)
            chunk(1, j, qb_ref, kb_ref, vb_ref, lab_ref, dob_ref, sb_ref, dqb_ref, dkb_ref, dvb_ref, dlab_ref)

        @pl.when(i == nb - 1)
        def _():
            ds0_ref[...] = ds_scr[...]

    tb = cb * CHUNK
    rev = lambda w, c=0: pl.BlockSpec((tb, w), lambda i: (nb - 1 - i, c))
    fro = lambda w, c=0: pl.BlockSpec((tb, w), lambda i: (i, c))
    st_rev = pl.BlockSpec((cb, HV, DK), lambda i: (nb - 1 - i, 0, 0))
    st_fro = pl.BlockSpec((cb, HV, DK), lambda i: (i, 0, 0))
    md = MXU_DTYPE
    return pl.pallas_call(
        body, name="gla_bwd", grid=(nb,),
        in_specs=[rev(DK), rev(DK), rev(DV), rev(DK, 0), rev(DV), st_rev,
                  fro(DK), fro(DK), fro(DV), fro(DK, 1), fro(DV), st_fro, _full((2, HV, DK))],
        out_specs=[rev(DK), rev(DK), rev(DV), rev(DK), fro(DK), fro(DK), fro(DV), fro(DK), _full((2, HV, DK))],
        out_shape=[_sds((t, DK), md), _sds((t, DK), md), _sds((t, DV), md), _sds((t, DK)),
                   _sds((t, DK), md), _sds((t, DK), md), _sds((t, DV), md), _sds((t, DK)), _sds((2, HV, DK))],
        scratch_shapes=[pltpu.VMEM((2, HV, DK), F32)],
        compiler_params=_cparams(1),
    )(q, k, v, la, do, sall_f, q, k, v, la, do, sall_b, dsfin)


def _seg_pos(tt):
    return lax.broadcasted_iota(jnp.int32, (tt, 1), 0) % SEG


def _shifted(x, s, pos, tt):
    y = x if s == 0 else pltpu.roll(x, (-s) % tt, 0)
    return jnp.where((pos + s >= 0) & (pos + s < SEG), y, 0.0)


def _head_norm(o, gn):
    rs, xs = [], []
    for h in range(NH):
        oh = o[:, h * HV:(h + 1) * HV]
        r = lax.rsqrt(jnp.mean(oh * oh, axis=-1, keepdims=True) + EPS)
        rs.append(r)
        xs.append(oh * r)
    return rs, xs


def merge_fwd(u, g, o_f, o_b, x, vec, vc, convw, w_out, tt):
    t = x.shape[0]

    def body(u_ref, g_ref, of_ref, ob_ref, x_ref, vec_ref, vc_ref, cw_ref, w_ref, x1_ref, cat_ref, mix_ref, yc_ref,
             h2_ref):
        a = u_ref[:, 0:DC]
        gate = u_ref[:, DC:2 * DC]
        vv = a * _sigmoid(gate)
        pos = _seg_pos(tt)
        cw = cw_ref[...]
        yc = jnp.zeros((tt, DC), F32) + vc_ref[1:2, :]
        for j in range(CW):
            yc = yc + _shifted(vv, j - CPAD, pos, tt) * cw[j:j + 1, :]
        yc_ref[...] = yc
        mu = jnp.mean(yc, axis=-1, keepdims=True)
        yd = yc - mu
        rs = lax.rsqrt(jnp.mean(yd * yd, axis=-1, keepdims=True) + EPS)
        ln = yd * rs * vc_ref[2:3, :] + vc_ref[3:4, :]
        conv_o = ln * _sigmoid(ln)
        o = of_ref[...] + ob_ref[...]
        _, xs = _head_norm(o, None)
        gg = g_ref[...]
        o2g = jnp.concatenate(xs, axis=1) * vc_ref[0:1, :] * (gg * _sigmoid(gg))
        cat = jnp.concatenate([conv_o, o2g], axis=1).astype(MXU_DTYPE)
        cat_ref[...] = cat
        mix = _mm(cat, w_ref[...])
        mix_ref[...] = mix
        x1 = x_ref[...] + vec_ref[0:1, :] * mix
        x1_ref[...] = x1
        _, _, _, h2 = _norm_mod(x1, vec_ref[1:2, :], vec_ref[2:3, :], vec_ref[3:4, :])
        h2_ref[...] = h2.astype(MXU_DTYPE)

    return pl.pallas_call(
        body, name="merge_fwd", grid=(t // tt,),
        in_specs=[_rows(tt, 1024), _rows(tt, DV), _rows(tt, DV), _rows(tt, DV), _rows(tt, D),
                  _full((8, D)), _full((8, DC)), _full((32, DC)), _full((D, D))],
        out_specs=[_rows(tt, D), _rows(tt, D), _rows(tt, D), _rows(tt, DC), _rows(tt, D)],
        out_shape=[_sds((t, D)), _sds((t, D), MXU_DTYPE), _sds((t, D)), _sds((t, DC)), _sds((t, D), MXU_DTYPE)],
        compiler_params=_cparams(1),
    )(u, g, o_f, o_b, x, vec, vc, convw, w_out)


def merge_bwd(dx1, mix, u, g, o_f, o_b, yc, vec, vc, convw, w_out, tt):
    t = dx1.shape[0]

    def body(dx1_ref, mix_ref, u_ref, g_ref, of_ref, ob_ref, yc_ref, vec_ref, vc_ref, cw_ref, w_ref,
             du_ref, dg_ref, do_ref, dmix_ref, acc1_ref, acc2_ref, dcw_ref):
        _zero_first(acc1_ref, acc2_ref, dcw_ref)
        dx1v = dx1_ref[...]
        dg1 = jnp.sum(dx1v * mix_ref[...], axis=0, keepdims=True)
        dmix = (vec_ref[0:1, :] * dx1v).astype(MXU_DTYPE)
        dmix_ref[...] = dmix
        dcat = _mm_nt(dmix, w_ref[...])
        dconv_o = dcat[:, 0:DC]
        do2 = dcat[:, DC:2 * DC]
        gn = vc_ref[0:1, :]
        o = of_ref[...] + ob_ref[...]
        rs, xs = _head_norm(o, None)
        xn = jnp.concatenate(xs, axis=1)
        gg = g_ref[...]
        sg = _sigmoid(gg)
        don = do2 * (gg * sg)
        dg_ref[...] = (do2 * (xn * gn) * (sg * (1.0 + gg * (1.0 - sg)))).astype(MXU_DTYPE)
        dgn = jnp.sum(don * xn, axis=0, keepdims=True)
        dxn = don * gn
        dos = []
        for h in range(NH):
            dh = dxn[:, h * HV:(h + 1) * HV]
            dos.append(rs[h] * (dh - xs[h] * jnp.mean(dh * xs[h], axis=-1, keepdims=True)))
        do_ref[...] = jnp.concatenate(dos, axis=1).astype(MXU_DTYPE)
        yc = yc_ref[...]
        mu = jnp.mean(yc, axis=-1, keepdims=True)
        yd = yc - mu
        rstd = lax.rsqrt(jnp.mean(yd * yd, axis=-1, keepdims=True) + EPS)
        yhat = yd * rstd
        lg = vc_ref[2:3, :]
        ln = yhat * lg + vc_ref[3:4, :]
        sl = _sigmoid(ln)
        dln = dconv_o * (sl * (1.0 + ln * (1.0 - sl)))
        dlb = jnp.sum(dln, axis=0, keepdims=True)
        dlg = jnp.sum(dln * yhat, axis=0, keepdims=True)
        dyh = dln * lg
        dyc = rstd * (dyh - jnp.mean(dyh, axis=-1, keepdims=True)
                      - yhat * jnp.mean(dyh * yhat, axis=-1, keepdims=True))
        dcb = jnp.sum(dyc, axis=0, keepdims=True)
        a = u_ref[:, 0:DC]
        gate = u_ref[:, DC:2 * DC]
        sgt = _sigmoid(gate)
        vv = a * sgt
        pos = _seg_pos(tt)
        cw = cw_ref[...]
        dvv = jnp.zeros((tt, DC), F32)
        dws = []
        for j in range(CW):
            shifted_dyc = _shifted(dyc, CPAD - j, pos, tt)
            dvv = dvv + shifted_dyc * cw[j:j + 1, :]
            dws.append(jnp.sum(shifted_dyc * vv, axis=0, keepdims=True))
        dws.append(jnp.zeros((1, DC), F32))
        du_ref[:, 0:DC] = (dvv * sgt).astype(MXU_DTYPE)
        du_ref[:, DC:2 * DC] = (dvv * a * sgt * (1.0 - sgt)).astype(MXU_DTYPE)
        _acc_rows(acc1_ref, [dg1])
        _acc_rows(acc2_ref, [dgn, dcb, dlg, dlb])
        dcw_ref[...] += jnp.concatenate(dws, axis=0)

    return pl.pallas_call(
        body, name="merge_bwd", grid=(t // tt,),
        in_specs=[_rows(tt, D), _rows(tt, D), _rows(tt, 1024), _rows(tt, DV), _rows(tt, DV), _rows(tt, DV),
                  _rows(tt, DC),
                  _full((8, D)), _full((8, DC)), _full((32, DC)), _full((D, D))],
        out_specs=[_rows(tt, 1024), _rows(tt, DV), _rows(tt, DV), _rows(tt, D), _full((8, D)), _full((8, DC)),
                   _full((32, DC))],
        out_shape=[_sds((t, 1024), MXU_DTYPE), _sds((t, DV), MXU_DTYPE), _sds((t, DV), MXU_DTYPE),
                   _sds((t, D), MXU_DTYPE), _sds((8, D)),
                   _sds((8, DC)), _sds((32, DC))],
        compiler_params=_cparams(1),
    )(dx1, mix, u, g, o_f, o_b, yc, vec, vc, convw, w_out)


FN = DFF // 2


def ffn_gate_up(h2, wg_t, wu_t, tt):
    t = h2.shape[0]

    def body(h2_ref, wg_ref, wu_ref, s_ref, d_ref, hid_ref):
        h2v = h2_ref[...]
        gt = _mm_nt(h2v, wg_ref[...])
        up = _mm_nt(h2v, wu_ref[...])
        sg = _sigmoid(gt)
        act = gt * sg
        s_ref[...] = act.astype(MXU_DTYPE)
        d_ref[...] = (up * (sg * (1.0 + gt * (1.0 - sg)))).astype(MXU_DTYPE)
        hid_ref[...] = (act * up).astype(MXU_DTYPE)

    blk = pl.BlockSpec((tt, FN), lambda j, i: (i, j))
    wblk = pl.BlockSpec((FN, D), lambda j, i: (j, 0))
    return pl.pallas_call(
        body, name="ffn_gate_up", grid=(2, t // tt),
        in_specs=[pl.BlockSpec((tt, D), lambda j, i: (i, 0)), wblk, wblk],
        out_specs=[blk, blk, blk],
        out_shape=[_sds((t, DFF), MXU_DTYPE)] * 3,
        compiler_params=_cparams(2),
    )(h2, wg_t, wu_t)


def ffn_down_loss(hid, x1, tgt, vec, w_down, tt):
    t = x1.shape[0]

    def body(hid_ref, x1_ref, tgt_ref, vec_ref, w_ref, dx2_ref, dff_ref, acc_ref):
        _zero_first(acc_ref)
        g2 = vec_ref[0:1, :]
        fg = vec_ref[1:2, :]
        sums = None
        for rows in (slice(0, tt // 2), slice(tt // 2, tt)):
            ff = _mm(hid_ref[rows, :], w_ref[...])
            x2 = x1_ref[rows, :] + g2 * ff
            rf = lax.rsqrt(jnp.mean(x2 * x2, axis=-1, keepdims=True) + EPS)
            xn = x2 * rf
            err = xn * fg - tgt_ref[rows, :]
            dy = err * (1.0 / D)
            dfg = jnp.sum(dy * xn, axis=0, keepdims=True)
            dxn = dy * fg
            dx2 = rf * (dxn - xn * jnp.mean(dxn * xn, axis=-1, keepdims=True))
            dx2_ref[rows, :] = dx2
            dff_ref[rows, :] = (g2 * dx2).astype(MXU_DTYPE)
            dg2 = jnp.sum(dx2 * ff, axis=0, keepdims=True)
            loss = jnp.sum(err * err, axis=0, keepdims=True) * (0.5 / D)
            part = [dg2, dfg, loss]
            sums = part if sums is None else [a + b for a, b in zip(sums, part)]
        _acc_rows(acc_ref, sums)

    return pl.pallas_call(
        body, name="ffn_down_loss", grid=(t // tt,),
        in_specs=[_rows(tt, DFF), _rows(tt, D), _rows(tt, D), _full((8, D)), _full((DFF, D))],
        out_specs=[_rows(tt, D), _rows(tt, D), _full((8, D))],
        out_shape=[_sds((t, D)), _sds((t, D), MXU_DTYPE), _sds((8, D))],
        compiler_params=_cparams(1),
    )(hid, x1, tgt, vec, w_down)


def ffn_dhid(dff, s, d, w_down, tt):
    t = dff.shape[0]

    def body(dff_ref, s_ref, d_ref, w_ref, dgt_ref, dup_ref):
        dhid = _mm_nt(dff_ref[...], w_ref[...])
        dgt_ref[...] = (dhid * d_ref[...].astype(F32)).astype(MXU_DTYPE)
        dup_ref[...] = (dhid * s_ref[...].astype(F32)).astype(MXU_DTYPE)

    blk = pl.BlockSpec((tt, FN), lambda j, i: (i, j))
    return pl.pallas_call(
        body, name="ffn_dhid", grid=(2, t // tt),
        in_specs=[pl.BlockSpec((tt, D), lambda j, i: (i, 0)), blk, blk, pl.BlockSpec((FN, D), lambda j, i: (j, 0))],
        out_specs=[blk, blk],
        out_shape=[_sds((t, DFF), MXU_DTYPE), _sds((t, DFF), MXU_DTYPE)],
        compiler_params=_cparams(2),
    )(dff, s, d, w_down)


def ffn_dh2(dgt, dup, x1, dx2, vec, wg_t, wu_t, tt):
    t = x1.shape[0]

    def body(dgt_ref, dup_ref, x1_ref, dx2_ref, vec_ref, wg_ref, wu_ref, dx1_ref, acc_ref):
        _zero_first(acc_ref)
        dh2 = _mm(dgt_ref[...], wg_ref[...]) + _mm(dup_ref[...], wu_ref[...])
        g, sc = vec_ref[0:1, :], vec_ref[2:3, :]
        r, xn, yy, _ = _norm_mod(x1_ref[...], g, vec_ref[1:2, :], sc)
        dsh, dsc, dgn, dx = _norm_mod_bwd(dh2, r, xn, yy, g, sc)
        dx1_ref[...] = dx2_ref[...] + dx
        _acc_rows(acc_ref, [dsh, dsc, dgn])

    return pl.pallas_call(
        body, name="ffn_dh2", grid=(t // tt,),
        in_specs=[_rows(tt, DFF), _rows(tt, DFF), _rows(tt, D), _rows(tt, D), _full((8, D)), _full((DFF, D)),
                  _full((DFF, D))],
        out_specs=[_rows(tt, D), _full((8, D))],
        out_shape=[_sds((t, D)), _sds((8, D))],
        compiler_params=_cparams(1),
    )(dgt, dup, x1, dx2, vec, wg_t, wu_t)


def tn_matmul(a, b, bm, bt, init=None):
    t, m = a.shape
    n = b.shape[1]
    nk = t // bt

    def body(*refs):
        if init is None:
            a_ref, b_ref, o_ref, wire_ref = refs
        else:
            a_ref, b_ref, i_ref, o_ref, wire_ref = refs
        @pl.when(pl.program_id(1) == 0)
        def _():
            o_ref[...] = jnp.zeros_like(o_ref) if init is None else i_ref[...]

        o_ref[...] += _mm_tn(a_ref[...], b_ref[...])

        @pl.when(pl.program_id(1) == nk - 1)
        def _():
            wire_ref[...] = o_ref[...].astype(WIRE_DTYPE)

    in_specs = [pl.BlockSpec((bt, bm), lambda i, k: (k, i)), pl.BlockSpec((bt, n), lambda i, k: (k, 0))]
    args = [a, b]
    if init is not None:
        in_specs.append(pl.BlockSpec((bm, n), lambda i, k: (i, 0)))
        args.append(init)
    oblk = pl.BlockSpec((bm, n), lambda i, k: (i, 0))
    return pl.pallas_call(
        body, name="tn_matmul", grid=(m // bm, nk),
        in_specs=in_specs, out_specs=[oblk, oblk],
        out_shape=[_sds((m, n)), _sds((m, n), WIRE_DTYPE)], compiler_params=_cparams(2),
    )(*args)


def _adamw(w, g, m, v):
    m = ADAM_B1 * m + (1.0 - ADAM_B1) * g
    v = ADAM_B2 * v + (1.0 - ADAM_B2) * (g * g)
    m_hat = m / (1.0 - ADAM_B1 ** ADAM_STEP)
    v_hat = v / (1.0 - ADAM_B2 ** ADAM_STEP)
    delta = -ADAM_LR * (m_hat / (jnp.sqrt(v_hat) + ADAM_EPS) + ADAM_WD * w)
    return delta, m, v


def adamw_sharded(own, recv, w, m, v):
    shape = w.shape

    def body(own_ref, recv_ref, w_ref, m_ref, v_ref, g_ref, d_ref, mo_ref, vo_ref):
        g = own_ref[...]
        for k in range(N_DEV - 1):
            g = g + recv_ref[k].astype(F32)
        g_ref[...] = g
        d_ref[...], mo_ref[...], vo_ref[...] = _adamw(w_ref[...], g, m_ref[...], v_ref[...])

    return pl.pallas_call(
        body, name="adamw_sharded",
        in_specs=[_full(shape), _full((N_DEV - 1,) + shape), _full(shape), _full(shape), _full(shape)],
        out_specs=[_full(shape)] * 4, out_shape=[_sds(shape)] * 4, grid=(1,),
        compiler_params=_cparams(1),
    )(own, recv, w, m, v)


def adamw_small(items):
    n = len(items)
    flat = [a for it in items for a in it]

    def body(*refs):
        ins, outs = refs[:4 * n], refs[4 * n:]
        for i in range(n):
            g, w, m, v = (r[...] for r in ins[4 * i:4 * i + 4])
            outs[3 * i][...], outs[3 * i + 1][...], outs[3 * i + 2][...] = _adamw(w, g, m, v)

    out = pl.pallas_call(
        body, name="adamw_small", grid=(1,),
        in_specs=[_full(a.shape) for a in flat],
        out_specs=[_full(it[1].shape) for it in items for _ in range(3)],
        out_shape=[_sds(it[1].shape) for it in items for _ in range(3)],
        compiler_params=_cparams(1),
    )(*flat)
    return [tuple(out[3 * i:3 * i + 3]) for i in range(n)]


def _mesh_pos():
    x, y, c = lax.axis_index("x"), lax.axis_index("y"), lax.axis_index("c")
    me = 4 * x + 2 * y + c
    peers = []
    for k in range(1, N_DEV):
        peers.append(((1 - x) if (k >> 2) & 1 else x, (1 - y) if (k >> 1) & 1 else y, (1 - c) if k & 1 else c))
    return me, peers


def _all_gather_issue(buf, send_sems, recv_sems, me, peers):
    sends = []
    for k, peer in enumerate(peers):
        cp = pltpu.make_async_remote_copy(src_ref=buf.at[me], dst_ref=buf.at[me], send_sem=send_sems.at[k],
                                          recv_sem=recv_sems.at[k], device_id=peer, device_id_type=MESH)
        cp.start()
        sends.append(cp)
    return sends


def _all_gather_finish(buf, send_sems, recv_sems, me, peers, sends):
    for k, peer in enumerate(peers):
        src = jnp.bitwise_xor(me, k + 1)
        pltpu.make_async_remote_copy(src_ref=buf.at[src], dst_ref=buf.at[src], send_sem=send_sems.at[k],
                                     recv_sem=recv_sems.at[k], device_id=peer, device_id_type=MESH).wait_recv()
    for cp in sends:
        cp.wait_send()


def _all_gather(buf, send_sems, recv_sems, me, peers):
    _all_gather_finish(buf, send_sems, recv_sems, me, peers, _all_gather_issue(buf, send_sems, recv_sems, me, peers))


_VMEM = pl.BlockSpec(memory_space=pltpu.VMEM)
_ANY = pl.BlockSpec(memory_space=pl.ANY)
_SEMS = pltpu.SemaphoreType.DMA((N_DEV - 1,))


def mod_forward(c, c_ctx, w_mod_sh, b_mod, sm_pack, w_first):
    ncol = w_mod_sh.shape[1]

    def body(c_ref, cc_ref, w_ref, b_ref, sm_ref, wf_ref, mod_ref, s_ref, smt_ref, wall_ref, cbuf, pbuf, smbuf,
             s1, r1, s2, r2, s3, r3, ws, wr, wl):
        me, peers = _mesh_pos()
        x, y, cc = lax.axis_index("x"), lax.axis_index("y"), lax.axis_index("c")
        sibling = (x, y, 1 - cc)
        chips = [(1 - x, y), (x, 1 - y), (1 - x, 1 - y)]
        slot = lambda px, py, pc: wall_ref.at[4 * px + 2 * py + pc]

        def wcopy(k, block, to, src=None):
            return pltpu.make_async_remote_copy(
                src_ref=slot(*block) if src is None else src, dst_ref=slot(*block), send_sem=ws.at[k],
                recv_sem=wr.at[k], device_id=to, device_id_type=MESH)

        mine = pltpu.make_async_copy(wf_ref, slot(x, y, cc), wl)
        mine.start()
        first = [wcopy(0, (x, y, cc), sibling, src=wf_ref)]
        first += [wcopy(1 + j, (x, y, cc), (*chip, cc), src=wf_ref) for j, chip in enumerate(chips)]
        for cp in first:
            cp.start()
        smbuf[me] = sm_ref[...]
        sm_sends = _all_gather_issue(smbuf, s3, r3, me, peers)
        cbuf[me] = jnp.broadcast_to(c_ref[...], (8, D))
        _all_gather(cbuf, s1, r1, me, peers)
        rows = [cbuf[j, 0:1, :] for j in range(N_DEV)] + [cc_ref[...], jnp.zeros((7, D), F32)]
        sx = jnp.concatenate(rows, axis=0)
        s = sx * _sigmoid(sx)
        s_ref[...] = s
        pbuf[me] = _hi(s, w_ref[...])
        _all_gather(pbuf, s2, r2, me, peers)
        for j in range(N_DEV):
            mod_ref[:, j * ncol:(j + 1) * ncol] = pbuf[j] + b_ref[:, j * ncol:(j + 1) * ncol]
        _all_gather_finish(smbuf, s3, r3, me, peers, sm_sends)
        tot = smbuf[0]
        for j in range(1, N_DEV):
            tot = tot + smbuf[j]
        smt_ref[...] = tot
        passed = [wcopy(4 + j, (*chip, cc), sibling) for j, chip in enumerate(chips)]
        for j, chip in enumerate(chips):
            wcopy(1 + j, (*chip, cc), (x, y, cc)).wait_recv()
            passed[j].start()
        wcopy(0, (x, y, 1 - cc), (x, y, cc)).wait_recv()
        for j, chip in enumerate(chips):
            wcopy(4 + j, (*chip, 1 - cc), (x, y, cc)).wait_recv()
        for cp in first + passed:
            cp.wait_send()
        mine.wait()

    return pl.pallas_call(
        body, name="mod_forward",
        in_specs=[_VMEM] * 6, out_specs=[_VMEM] * 4,
        out_shape=[_sds((16, N_DEV * ncol)), _sds((16, D)), _sds(sm_pack.shape),
                   _sds((N_DEV,) + w_first.shape, w_first.dtype)],
        scratch_shapes=[pltpu.VMEM((N_DEV, 8, D), F32), pltpu.VMEM((N_DEV, 16, ncol), F32),
                        pltpu.VMEM((N_DEV,) + sm_pack.shape, F32), _SEMS, _SEMS, _SEMS, _SEMS, _SEMS, _SEMS,
                        _SEMS, _SEMS, pltpu.SemaphoreType.DMA],
        compiler_params=pltpu.CompilerParams(vmem_limit_bytes=VMEM_LIMIT),
    )(c, c_ctx, w_mod_sh, b_mod, sm_pack, w_first)


def sum_blocks(gat):
    def body(g_ref, tot_ref):
        tot = g_ref[0]
        for j in range(1, N_DEV):
            tot = tot + g_ref[j]
        tot_ref[...] = tot

    return pl.pallas_call(
        body, name="sum_blocks", in_specs=[_VMEM], out_specs=_VMEM, out_shape=_sds(gat.shape[1:]),
        compiler_params=pltpu.CompilerParams(vmem_limit_bytes=VMEM_LIMIT),
    )(gat)


def mod_backward(s, dm_sh, w, m, v, cc, m_cc, v_cc):
    shape = w.shape

    def body(s_ref, dm_ref, w_ref, m_ref, v_ref, cc_ref, mcc_ref, vcc_ref,
             gw_ref, dw_ref, mw_ref, vw_ref, gc_ref, dc_ref, mc_ref, vc_ref, pbuf, send_sems, recv_sems):
        me, peers = _mesh_pos()
        wv = w_ref[...]
        pbuf[me] = _hi_nt(dm_ref[8:16, :], wv)
        _all_gather(pbuf, send_sems, recv_sems, me, peers)
        g = _hi_tn(s_ref[...], dm_ref[...])
        gw_ref[...] = g
        dw_ref[...], mw_ref[...], vw_ref[...] = _adamw(wv, g, m_ref[...], v_ref[...])
        tot = pbuf[0]
        for j in range(1, N_DEV):
            tot = tot + pbuf[j]
        ccv = cc_ref[...]
        sg = _sigmoid(ccv)
        gc = tot[0:1, :] * (sg * (1.0 + ccv * (1.0 - sg)))
        gc_ref[...] = gc
        dc_ref[...], mc_ref[...], vc_ref[...] = _adamw(ccv, gc, mcc_ref[...], vcc_ref[...])

    return pl.pallas_call(
        body, name="mod_backward", in_specs=[_VMEM] * 8, out_specs=[_VMEM] * 8,
        out_shape=[_sds(shape)] * 4 + [_sds((1, D))] * 4,
        scratch_shapes=[pltpu.VMEM((N_DEV, 8, D), F32), _SEMS, _SEMS],
        compiler_params=pltpu.CompilerParams(vmem_limit_bytes=VMEM_LIMIT),
    )(s, dm_sh, w, m, v, cc, m_cc, v_cc)


_HBM = pl.BlockSpec(memory_space=pltpu.HBM)
_SEM = pl.BlockSpec(memory_space=pltpu.SEMAPHORE)
_EFFECT = pltpu.SideEffectType.DATAFLOW_SIDE_EFFECTING
_hbm = lambda a: pltpu.with_memory_space_constraint(a, pltpu.HBM)


def gather_start(shards, me, tag):
    n = len(shards)
    sems = pltpu.SemaphoreType.DMA((7 * n,))
    lands = [lax.dynamic_update_slice(lax.empty((N_DEV,) + s.shape, s.dtype), s[None], (me, 0, 0)) for s in shards]

    def body(*refs):
        s_refs, l_refs = refs[:n], refs[n:2 * n]
        send_sems, recv_sems = refs[2 * n], refs[2 * n + 1]
        token = refs[-1]
        my, peers = _mesh_pos()
        for w in range(n):
            for k, peer in enumerate(peers):
                pltpu.make_async_remote_copy(
                    src_ref=s_refs[w], dst_ref=l_refs[w].at[my], send_sem=send_sems.at[w * 7 + k],
                    recv_sem=recv_sems.at[w * 7 + k], device_id=peer, device_id_type=MESH).start()
        token[...] = jnp.zeros_like(token)

    out = pl.pallas_call(
        body, name="gather_start_" + tag,
        out_shape=(sems, sems) + tuple(pltpu.HBM(a.shape, a.dtype) for a in list(shards) + lands) + (_sds((8, 128)),),
        in_specs=(_HBM,) * (2 * n), out_specs=(_SEM, _SEM) + (_HBM,) * (2 * n) + (_VMEM,),
        input_output_aliases={i: i + 2 for i in range(2 * n)},
        compiler_params=pltpu.CompilerParams(has_side_effects=_EFFECT),
    )(*[_hbm(a) for a in list(shards) + lands])
    return out[0], out[1], list(out[2:2 + n]), list(out[2 + n:2 + 2 * n]), out[-1]


def gather_wait(send_sems, recv_sems, shards, lands, after, tag):
    n = len(shards)

    def body(*refs):
        s_refs, l_refs = refs[:n], refs[n:2 * n]
        send_sems, recv_sems = refs[2 * n], refs[2 * n + 1]
        my, peers = _mesh_pos()
        for w in range(n):
            for k, peer in enumerate(peers):
                src = jnp.bitwise_xor(my, k + 1)
                cp = pltpu.make_async_remote_copy(
                    src_ref=s_refs[w], dst_ref=l_refs[w].at[src], send_sem=send_sems.at[w * 7 + k],
                    recv_sem=recv_sems.at[w * 7 + k], device_id=peer, device_id_type=MESH)
                cp.wait_send()
                cp.wait_recv()

    out = pl.pallas_call(
        body, name="gather_wait_" + tag,
        out_shape=tuple(pltpu.HBM(a.shape, a.dtype) for a in list(shards) + list(lands)),
        in_specs=(_HBM,) * (2 * n) + (_SEM, _SEM, _ANY), out_specs=(_HBM,) * (2 * n),
        input_output_aliases={i: i for i in range(2 * n)},
        compiler_params=pltpu.CompilerParams(has_side_effects=_EFFECT),
    )(*shards, *lands, send_sems, recv_sems, after)
    return list(out[n:2 * n])


def scatter_start(grads, tag):
    n = len(grads)
    sems = pltpu.SemaphoreType.DMA((7 * n,))
    lands = [lax.empty((N_DEV - 1,) + g.shape[1:], g.dtype) for g in grads]

    def body(*refs):
        g_refs, l_refs = refs[:n], refs[n:2 * n]
        send_sems, recv_sems = refs[2 * n], refs[2 * n + 1]
        token = refs[-1]
        me, peers = _mesh_pos()
        for w in range(n):
            for k, peer in enumerate(peers):
                dst = jnp.bitwise_xor(me, k + 1)
                pltpu.make_async_remote_copy(
                    src_ref=g_refs[w].at[dst], dst_ref=l_refs[w].at[k], send_sem=send_sems.at[w * 7 + k],
                    recv_sem=recv_sems.at[w * 7 + k], device_id=peer, device_id_type=MESH).start()
        token[...] = jnp.zeros_like(token)

    out = pl.pallas_call(
        body, name="scatter_start_" + tag,
        out_shape=(sems, sems) + tuple(pltpu.HBM(a.shape, a.dtype) for a in list(grads) + lands) + (_sds((8, 128)),),
        in_specs=(_HBM,) * (2 * n), out_specs=(_SEM, _SEM) + (_HBM,) * (2 * n) + (_VMEM,),
        input_output_aliases={i: i + 2 for i in range(2 * n)},
        compiler_params=pltpu.CompilerParams(has_side_effects=_EFFECT),
    )(*[_hbm(a) for a in list(grads) + lands])
    return out[0], out[1], list(out[2:2 + n]), list(out[2 + n:2 + 2 * n]), out[-1]


def scatter_wait(send_sems, recv_sems, grads, lands, after, tag):
    n = len(grads)

    def body(*refs):
        g_refs, l_refs = refs[:n], refs[n:2 * n]
        send_sems, recv_sems = refs[2 * n], refs[2 * n + 1]
        me, peers = _mesh_pos()
        for w in range(n):
            for k, peer in enumerate(peers):
                dst = jnp.bitwise_xor(me, k + 1)
                cp = pltpu.make_async_remote_copy(
                    src_ref=g_refs[w].at[dst], dst_ref=l_refs[w].at[k], send_sem=send_sems.at[w * 7 + k],
                    recv_sem=recv_sems.at[w * 7 + k], device_id=peer, device_id_type=MESH)
                cp.wait_send()
                cp.wait_recv()

    out = pl.pallas_call(
        body, name="scatter_wait_" + tag,
        out_shape=tuple(pltpu.HBM(a.shape, a.dtype) for a in list(grads) + list(lands)),
        in_specs=(_HBM,) * (2 * n) + (_SEM, _SEM, _ANY), out_specs=(_HBM,) * (2 * n),
        input_output_aliases={i: i for i in range(2 * n)},
        compiler_params=pltpu.CompilerParams(has_side_effects=_EFFECT),
    )(*grads, *lands, send_sems, recv_sems, after)
    return list(out[n:2 * n])


def _vec8(rows, width):
    rid = lax.broadcasted_iota(jnp.int32, (8, width), 0)
    out = jnp.zeros((8, width), F32)
    for i, r in enumerate(rows):
        r = r.reshape(-1)
        r = jnp.pad(r, (0, width - r.shape[0]))
        out = jnp.where(rid == i, r[None, :], out)
    return out


def local_step(x, ctx, tgt, mod, mod_c, small, w_int, start, late_weights, grads_ready, small_ready, tt, tt_ctx, cb,
               cb_ctx):
    sh1, sc1, g1, sh2, sc2, g2 = [mod[i * D:(i + 1) * D] for i in range(6)]
    csh1, csc1 = mod_c[0:D], mod_c[D:2 * D]
    vec1 = _vec8([small["norm1_g"], sh1, sc1], D)
    vec1c = _vec8([small["norm1_g"], csh1, csc1], D)
    vec2 = _vec8([small["norm2_g"], sh2, sc2], D)
    vec3 = _vec8([g2, small["final_g"]], D)
    vecm = _vec8([g1, small["norm2_g"], sh2, sc2], D)
    vcm = _vec8([jnp.tile(small["gla_norm_g"].reshape(HV), NH), small["conv_b"], small["conv_ln_g"],
                 small["conv_ln_b"]], DC)
    convw = jnp.pad(small["conv_w"], ((0, 1), (0, 0)))
    wa = jnp.zeros((128, 512), F32)
    wa = wa.at[0:RANK, 0:DK].set(small["w_a2_f"]).at[RANK:2 * RANK, DK:2 * DK].set(small["w_a2_b"])
    ba = jnp.concatenate([small["b_a_f"].reshape(1, DK), small["b_a_b"].reshape(1, DK)], axis=1)

    _, _, kc, vc_, _, rc, lac, hc = proj_fwd(ctx, vec1c + start, w_int, wa, ba, tt_ctx)
    qc0 = jnp.zeros_like(kc)
    _, _, sallf_c, sallb_c, sfin_c = gla_fwd(qc0, kc, vc_, lac, jnp.zeros((2, HV, DK), F32), cb_ctx)
    u, q, k, v, g, r, la, h = proj_fwd(x, vec1, w_int, wa, ba, tt)
    o_f, o_b, sall_f, sall_b, _ = gla_fwd(q, k, v, la, sfin_c, cb)
    w_out, wg_t, wu_t, w_down = late_weights(o_b)
    x1, cat, mix, yc, h2 = merge_fwd(u, g, o_f, o_b, x, vecm, vcm, convw, w_out, tt)
    tt2 = min(2 * tt, x.shape[0])
    act, dact, hid = ffn_gate_up(h2, wg_t, wu_t, tt2)
    dx2, dff, acc3 = ffn_down_loss(hid, x1, tgt, vec3, w_down, tt)
    dgt, dup = ffn_dhid(dff, act, dact, w_down, tt2)
    dx1, acc2 = ffn_dh2(dgt, dup, x1, dx2, vec2, wg_t, wu_t, tt)
    bt = min(2048, x.shape[0])
    gw = {"w_down": tn_matmul(hid, dff, FN, bt), "wg_t": tn_matmul(dgt, h2, FN, bt),
          "wu_t": tn_matmul(dup, h2, FN, bt)}
    vecm = vecm + grads_ready(("wg_t", "wu_t", "w_down"), gw)
    du, dg, do, dmix, accm1, accm2, dconvw = merge_bwd(dx1, mix, u, g, o_f, o_b, yc, vecm, vcm, convw, w_out, tt)
    gw["w_out"] = tn_matmul(cat, dmix, 512, bt)
    dsfin = jnp.zeros((2, HV, DK), F32) + grads_ready(("w_out",), gw)
    dqf, dkf, dvf, dlaf, dqb, dkb, dvb, dlab, ds0 = gla_bwd(q, k, v, la, do, sall_f, sall_b, dsfin, cb)
    gx, dp, acc1, dba, dwa = proj_bwd(du, (dqf, dqb, dkf, dkb, dvf, dvb), dg, dlaf, dlab, la, r, x, dx1, vec1, w_int,
                                      wa, tt)
    tcx = ctx.shape[0]
    zc = lambda w, dt=MXU_DTYPE: jnp.zeros((tcx, w), dt)
    _, dkf, dvf, dlaf, _, dkb, dvb, dlab, _ = gla_bwd(qc0, kc, vc_, lac, zc(DV), sallf_c, sallb_c, ds0, cb_ctx)
    _, dpc, acc1c, dbac, dwac = proj_bwd(zc(1024), (zc(DK), zc(DK), dkf, dkb, dvf, dvb), zc(DV), dlaf, dlab, lac, rc,
                                         ctx, zc(D, F32), vec1c, w_int, wa, tt_ctx)
    dwa_t = dwa + dwac
    dba_t = dba + dbac
    gs = {
        "norm1_g": acc1[2] + acc1c[2], "norm2_g": acc2[2], "final_g": acc3[1], "loss": acc3[2],
        "gla_norm_g": accm2[0], "conv_b": accm2[1], "conv_ln_g": accm2[2], "conv_ln_b": accm2[3],
        "conv_w": dconvw, "b_a": dba_t[0], "w_a2": dwa_t,
    }
    dmod = _vec8([acc1[0], acc1[1], accm1[0], acc2[0], acc2[1], acc3[0]], D)
    dmod_c = _vec8([acc1c[0], acc1c[1]], D)
    dpc = dpc + small_ready(gs, dmod, dmod_c).astype(dpc.dtype)
    btc = min(1024, tcx)
    gw["w_int"] = tn_matmul(dp, h, 896, bt, init=tn_matmul(dpc, hc, 896, btc)[0])
    grads_ready(("w_int",), gw)
    return gx, gw


PACK_ROWS = 96
ROW_N1, ROW_N2, ROW_FG, ROW_LOSS, ROW_GN, ROW_CB, ROW_LG, ROW_LB, ROW_BA = 0, 1, 2, 3, 4, 5, 6, 7, 8
ROW_DMOD, ROW_DMODC, ROW_CW, ROW_WA = 16, 24, 32, 64


def _pack_small(gs, dmod, dmod_c):
    pad = lambda a: jnp.pad(a, ((0, 0), (0, D - a.shape[1])))
    singles = _vec8([gs["norm1_g"], gs["norm2_g"], gs["final_g"], gs["loss"], gs["gla_norm_g"], gs["conv_b"],
                     gs["conv_ln_g"], gs["conv_ln_b"]], D)
    return jnp.concatenate([singles, _vec8([gs["b_a"]], D), dmod, dmod_c, pad(gs["conv_w"]), pad(gs["w_a2"][0:32])],
                           axis=0)


def kernel(x, c, ctx, c_ctx, w_mod, b_mod, norm1_g, norm2_g, w_in, conv_w, conv_b, conv_ln_g, conv_ln_b, w_a2_f, b_a_f, w_a2_b, b_a_b, gla_norm_g, w_out, w_gate, w_up, w_down, final_g, loss_target, m_c_ctx, m_w_mod, m_b_mod, m_norm1_g, m_norm2_g, m_w_in, m_conv_w, m_conv_b, m_conv_ln_g, m_conv_ln_b, m_w_a2_f, m_b_a_f, m_w_a2_b, m_b_a_b, m_gla_norm_g, m_w_out, m_w_gate, m_w_up, m_w_down, m_final_g, v_c_ctx, v_w_mod, v_b_mod, v_norm1_g, v_norm2_g, v_w_in, v_conv_w, v_conv_b, v_conv_ln_g, v_conv_ln_b, v_w_a2_f, v_b_a_f, v_w_a2_b, v_b_a_b, v_gla_norm_g, v_w_out, v_w_gate, v_w_up, v_w_down, v_final_g):
    me = 4 * lax.axis_index("x") + 2 * lax.axis_index("y") + lax.axis_index("c")
    t = x.shape[1]
    tcx = ctx.shape[1]
    r_in, r_out, r_ff = w_in.shape[2], w_out.shape[1], w_gate.shape[2]
    r_in_b = -(-r_in // 16) * 16

    tb = lambda w: w.T.astype(MXU_DTYPE)

    small = dict(norm1_g=norm1_g[0], norm2_g=norm2_g[0], final_g=final_g, gla_norm_g=gla_norm_g[0],
                 conv_b=conv_b[0], conv_ln_g=conv_ln_g[0], conv_ln_b=conv_ln_b[0], b_a_f=b_a_f[0], b_a_b=b_a_b[0])
    sm_pack = jnp.zeros((48, DC), F32)
    sm_pack = lax.dynamic_update_slice(sm_pack, conv_w[0], (0, me * (DC // N_DEV)))
    sm_pack = lax.dynamic_update_slice(sm_pack, w_a2_f[0], (32, me * (DK // N_DEV)))
    sm_pack = lax.dynamic_update_slice(sm_pack, w_a2_b[0], (32, DK + me * (DK // N_DEV)))

    mod_all, s_all, sm_tot, wall = mod_forward(c, c_ctx.reshape(1, D), w_mod[0], b_mod, sm_pack,
                                               jnp.pad(tb(w_in[0]), ((0, r_in_b - r_in), (0, 0))))
    mod = lax.dynamic_slice(mod_all, (me, 0), (1, 6 * D)).reshape(6 * D)
    mod_c = mod_all[8]
    small["conv_w"] = sm_tot[0:CW, :]
    small["w_a2_f"] = sm_tot[32:32 + RANK, 0:DK]
    small["w_a2_b"] = sm_tot[32:32 + RANK, DK:2 * DK]

    w_int =jnp.pad(wall[:, 0:r_in, :].reshape(N_DEV * r_in, D), ((0, DINP - DIN), (0, 0)))
    after_w_in = (wall[0:1, 0:1, 0] * 0).astype(MXU_DTYPE)
    late = [w_out[0].astype(MXU_DTYPE) + after_w_in, tb(w_gate[0]) + after_w_in, tb(w_up[0]) + after_w_in,
            w_down[0].astype(MXU_DTYPE) + after_w_in]
    g_send, g_recv, late_thru, late_lands, g_token = gather_start(late, me, "late")

    def late_weights(after):
        got = gather_wait(g_send, g_recv, late_thru, late_lands, after, "late")
        return tuple(a.reshape(N_DEV * a.shape[1], D) for a in got)

    pad_in = lambda g: jnp.pad(g[0:DIN].reshape(N_DEV, r_in, D), ((0, 0), (0, r_in_b - r_in), (0, 0)))
    blocked = {"w_int": pad_in, "w_out": lambda g: g.reshape(N_DEV, r_out, D)}
    as_blocks = lambda n, g: blocked.get(n, lambda a: a.reshape(N_DEV, r_ff, D))(g)
    pending = []

    def grads_ready(names, gw_now):
        blocks = [as_blocks(n, gw_now[n][1]) for n in names]
        if names[0] == "w_int":
            done = finish_small(gw_now["w_int"][0]).astype(WIRE_DTYPE)
            blocks = [b + done for b in blocks]
        send, recv_s, thru, zones, token = scatter_start(blocks, names[0])
        pending.append((names, send, recv_s, thru, zones))
        return token[0:1, 0:1]

    sm = {}

    def small_ready(gs, dmod, dmod_c):
        sm["copy"] = gather_start([_pack_small(gs, dmod, dmod_c)], me, "small")
        return sm["copy"][4][0:1, 0:1]

    def finish_small(after):
        send, recv_s, thru, zones, _ = sm["copy"]
        gat = gather_wait(send, recv_s, thru, zones, after, "small")[0]
        sm["tot"] = sum_blocks(gat)
        sm["dm"] = jnp.concatenate(
            [gat[:, ROW_DMOD:ROW_DMOD + 6, :].reshape(N_DEV, 6 * D),
             jnp.pad(sm["tot"][ROW_DMODC:ROW_DMODC + 6, :].reshape(1, 6 * D), ((0, 7), (0, 0)))], axis=0)
        ncol = w_mod.shape[2]
        dm_sh = lax.dynamic_slice(sm["dm"], (0, me * ncol), (16, ncol))
        sm["mod"] = mod_backward(s_all, dm_sh, w_mod[0], m_w_mod[0], v_w_mod[0], c_ctx.reshape(1, D),
                                 m_c_ctx.reshape(1, D), v_c_ctx.reshape(1, D))
        return sm["mod"][4][0:1, 0:1] * 0

    gx, gw = local_step(x[0], ctx[0], loss_target[0], mod, mod_c, small, w_int, g_token[0:1, 0:1], late_weights,
                        grads_ready, small_ready, 512, 256, 8, 4)
    tot, dm = sm["tot"], sm["dm"]
    loss = jnp.sum(tot[ROW_LOSS])
    g_wmod, d_wmod, nm_wmod, nv_wmod, g_cc, d_cc, nm_cc, nv_cc = sm["mod"]

    recv = {}

    def wait_for(entry, after):
        names, send, recv_s, thru, zones = entry
        recv.update(dict(zip(names, scatter_wait(send, recv_s, thru, zones, after, names[0]))))

    for entry in pending[:-1]:
        wait_for(entry, tot)
    own = {n: lax.dynamic_index_in_dim(as_blocks(n, gw[n][0]), me, 0, keepdims=False) for n in gw if n != "w_int"}
    own["w_int"] = jnp.pad(lax.dynamic_slice(gw["w_int"][0], (me * r_in, 0), (r_in, D)), ((0, r_in_b - r_in), (0, 0)))
    padt = lambda w: jnp.pad(w.T, ((0, r_in_b - r_in), (0, 0)))
    big = {}
    big["w_gate"] = [a.T for a in adamw_sharded(own["wg_t"], recv["wg_t"], w_gate[0].T, m_w_gate[0].T,
                                                 v_w_gate[0].T)]
    big["w_up"] = [a.T for a in adamw_sharded(own["wu_t"], recv["wu_t"], w_up[0].T, m_w_up[0].T, v_w_up[0].T)]
    big["w_down"] = adamw_sharded(own["w_down"], recv["w_down"], w_down[0], m_w_down[0], v_w_down[0])
    big["w_out"] = adamw_sharded(own["w_out"], recv["w_out"], w_out[0], m_w_out[0], v_w_out[0])
    wait_for(pending[-1], big["w_out"][0])
    big["w_in"] = [a[0:r_in].T for a in adamw_sharded(own["w_int"], recv["w_int"], padt(w_in[0]), padt(m_w_in[0]),
                                                       padt(v_w_in[0]))]
    big["w_mod"] = [g_wmod, d_wmod, nm_wmod, nv_wmod]

    row = lambda r, w: tot[r:r + 1, 0:w]
    gn_row = tot[ROW_GN:ROW_GN + 1, 0:DC]
    g_small = {
        "b_mod": jnp.sum(dm, axis=0, keepdims=True),
        "norm1_g": row(ROW_N1, D), "norm2_g": row(ROW_N2, D),
        "conv_w": lax.dynamic_slice(tot, (ROW_CW, me * (DC // N_DEV)), (CW, DC // N_DEV)),
        "conv_b": row(ROW_CB, DC), "conv_ln_g": row(ROW_LG, DC), "conv_ln_b": row(ROW_LB, DC),
        "w_a2_f": lax.dynamic_slice(tot, (ROW_WA, me * (DK // N_DEV)), (RANK, DK // N_DEV)),
        "b_a_f": tot[ROW_BA:ROW_BA + 1, 0:DK],
        "w_a2_b": lax.dynamic_slice(tot, (ROW_WA + RANK, DK + me * (DK // N_DEV)), (RANK, DK // N_DEV)),
        "b_a_b": tot[ROW_BA:ROW_BA + 1, DK:2 * DK],
        "gla_norm_g": gn_row[:, 0:HV] + gn_row[:, HV:2 * HV] + gn_row[:, 2 * HV:3 * HV] + gn_row[:, 3 * HV:4 * HV],
        "final_g": row(ROW_FG, D),
    }
    wmv = {
        "b_mod": (b_mod, m_b_mod, v_b_mod), "norm1_g": (norm1_g, m_norm1_g, v_norm1_g),
        "norm2_g": (norm2_g, m_norm2_g, v_norm2_g), "conv_w": (conv_w[0], m_conv_w[0], v_conv_w[0]),
        "conv_b": (conv_b, m_conv_b, v_conv_b), "conv_ln_g": (conv_ln_g, m_conv_ln_g, v_conv_ln_g),
        "conv_ln_b": (conv_ln_b, m_conv_ln_b, v_conv_ln_b), "w_a2_f": (w_a2_f[0], m_w_a2_f[0], v_w_a2_f[0]),
        "b_a_f": (b_a_f, m_b_a_f, v_b_a_f), "w_a2_b": (w_a2_b[0], m_w_a2_b[0], v_w_a2_b[0]),
        "b_a_b": (b_a_b, m_b_a_b, v_b_a_b), "gla_norm_g": (gla_norm_g, m_gla_norm_g, v_gla_norm_g),
        "final_g": (final_g.reshape(1, D), m_final_g.reshape(1, D), v_final_g.reshape(1, D)),
    }
    names_small = list(g_small)
    upd = adamw_small([(g_small[n],) + wmv[n] for n in names_small])
    res = {n: (g_small[n],) + upd[i] for i, n in enumerate(names_small)}
    res["c_ctx"] = (g_cc, d_cc, nm_cc, nv_cc)
    for n in ("w_mod", "w_in", "w_out", "w_gate", "w_up", "w_down"):
        res[n] = tuple(big[n])

    order = ["c_ctx", "w_mod", "b_mod", "norm1_g", "norm2_g", "w_in", "conv_w", "conv_b", "conv_ln_g", "conv_ln_b",
             "w_a2_f", "b_a_f", "w_a2_b", "b_a_b", "gla_norm_g", "w_out", "w_gate", "w_up", "w_down", "final_g"]
    shapes = {"c_ctx": c_ctx.shape, "w_mod": w_mod.shape, "b_mod": b_mod.shape, "norm1_g": norm1_g.shape,
              "norm2_g": norm2_g.shape, "w_in": w_in.shape, "conv_w": conv_w.shape, "conv_b": conv_b.shape,
              "conv_ln_g": conv_ln_g.shape, "conv_ln_b": conv_ln_b.shape, "w_a2_f": w_a2_f.shape,
              "b_a_f": b_a_f.shape, "w_a2_b": w_a2_b.shape, "b_a_b": b_a_b.shape, "gla_norm_g": gla_norm_g.shape,
              "w_out": w_out.shape, "w_gate": w_gate.shape, "w_up": w_up.shape, "w_down": w_down.shape,
              "final_g": final_g.shape}
    outs = [loss, gx.reshape(x.shape)]
    for i in range(4):
        outs += [res[n][i].reshape(shapes[n]) for n in order]
    return tuple(outs)
```

```python
import functools

import jax
import jax.numpy as jnp
from jax import lax
from jax.experimental import pallas as pl
from jax.experimental.pallas import tpu as pltpu

F32 = jnp.float32
MXU_DTYPE = jnp.bfloat16
WIRE_DTYPE = jnp.bfloat16
HI = lax.Precision.HIGHEST
MESH = pl.DeviceIdType.MESH

N_DEV = 8
D = 1024
DC = 512
NH = 4
HK = 64
HV = 128
DK = NH * HK
DV = NH * HV
RANK = 16
CHUNK = 64
SEG = 64
CW = 31
CPAD = 15
DFF = 2816
DIN = 2592
DINP = 2688
TAU = 16.0
EPS = 1e-6
VMEM_LIMIT = 56 * 1024 * 1024

TOKEN_TILE = 512
CTX_TOKEN_TILE = 256
GLA_CHUNKS = 8
CTX_GLA_CHUNKS = 4
GRAD_TOKEN_BLOCK = 2048

ADAM_LR = 0.001
ADAM_B1 = 0.9
ADAM_B2 = 0.999
ADAM_EPS = 1e-08
ADAM_WD = 0.01
ADAM_STEP = 10


def _mm(a, b):
    return jnp.dot(a.astype(MXU_DTYPE), b.astype(MXU_DTYPE), preferred_element_type=F32)


def _mm_nt(a, b):
    return lax.dot_general(a.astype(MXU_DTYPE), b.astype(MXU_DTYPE), (((1,), (1,)), ((), ())),
                           preferred_element_type=F32)


def _mm_tn(a, b):
    return lax.dot_general(a.astype(MXU_DTYPE), b.astype(MXU_DTYPE), (((0,), (0,)), ((), ())),
                           preferred_element_type=F32)


def _hi(a, b):
    return jnp.dot(a, b, precision=HI, preferred_element_type=F32)


def _hi_nt(a, b):
    return lax.dot_general(a, b, (((1,), (1,)), ((), ())), precision=HI, preferred_element_type=F32)


def _hi_tn(a, b):
    return lax.dot_general(a, b, (((0,), (0,)), ((), ())), precision=HI, preferred_element_type=F32)


def _sigmoid(x):
    return 1.0 / (1.0 + jnp.exp(-x))


def _cparams(n_axes):
    return pltpu.CompilerParams(dimension_semantics=("arbitrary",) * n_axes, vmem_limit_bytes=VMEM_LIMIT)


def _full(shape):
    n = len(shape)
    return pl.BlockSpec(shape, lambda *_: (0,) * n)


def _rows(tt, width):
    return pl.BlockSpec((tt, width), lambda i: (i, 0))


def _sds(shape, dtype=F32):
    return jax.ShapeDtypeStruct(shape, dtype)


def _norm_mod(x, g, sh, sc):
    r = lax.rsqrt(jnp.mean(x * x, axis=-1, keepdims=True) + EPS)
    xn = x * r
    yy = xn * g
    return r, xn, yy, yy * (1.0 + sc) + sh


def _norm_mod_bwd(dh, r, xn, yy, g, sc):
    dsh = jnp.sum(dh, axis=0, keepdims=True)
    dsc = jnp.sum(dh * yy, axis=0, keepdims=True)
    dy = dh * (1.0 + sc)
    dg = jnp.sum(dy * xn, axis=0, keepdims=True)
    dxn = dy * g
    dx = r * (dxn - xn * jnp.mean(dxn * xn, axis=-1, keepdims=True))
    return dsh, dsc, dg, dx


def _zero_first(*refs):
    @pl.when(pl.program_id(0) == 0)
    def _():
        for r in refs:
            r[...] = jnp.zeros_like(r)


def _acc_rows(ref, rows):
    ref[...] += jnp.concatenate(rows + [jnp.zeros((8 - len(rows), rows[0].shape[1]), F32)], axis=0)


def proj_fwd(x, vec, w_int, wa, ba, tt):
    t = x.shape[0]

    def body(x_ref, vec_ref, w_ref, wa_ref, ba_ref, u_ref, q_ref, k_ref, v_ref, g_ref, r_ref, la_ref, h_ref):
        _, _, _, h = _norm_mod(x_ref[...], vec_ref[0:1, :], vec_ref[1:2, :], vec_ref[2:3, :])
        hb = h.astype(MXU_DTYPE)
        h_ref[...] = hb
        p = _mm_nt(hb, w_ref[...])
        u_ref[...] = p[:, 0:1024]
        q_ref[...] = p[:, 1024:1280]
        k_ref[...] = p[:, 1280:1536]
        v_ref[...] = p[:, 1536:2048]
        g_ref[...] = p[:, 2048:2560]
        rr = p[:, 2560:2688]
        r_ref[...] = rr
        z = _mm(rr, wa_ref[...]) + ba_ref[...]
        la_ref[...] = (jnp.minimum(z, 0.0) - jnp.log(1.0 + jnp.exp(-jnp.abs(z)))) * (1.0 / TAU)

    return pl.pallas_call(
        body, name="proj_fwd", grid=(t // tt,),
        in_specs=[_rows(tt, D), _full((8, D)), _full((DINP, D)), _full((128, 512)), _full((1, 512))],
        out_specs=[_rows(tt, 1024), _rows(tt, DK), _rows(tt, DK), _rows(tt, DV), _rows(tt, DV), _rows(tt, 128),
                   _rows(tt, 512), _rows(tt, D)],
        out_shape=[_sds((t, 1024)), _sds((t, DK)), _sds((t, DK)), _sds((t, DV)), _sds((t, DV)), _sds((t, 128)),
                   _sds((t, 512)), _sds((t, D), MXU_DTYPE)],
        compiler_params=_cparams(1),
    )(x, vec, w_int, wa, ba)


def proj_bwd(du, dqkv, dg, dla_f, dla_b, la, r, x, dx1, vec, w_int, wa, tt):
    t = x.shape[0]

    def body(du_ref, dqf_ref, dqb_ref, dkf_ref, dkb_ref, dvf_ref, dvb_ref, dg_ref, dlaf_ref, dlab_ref, la_ref, r_ref,
             x_ref, dx1_ref, vec_ref, w_ref, wa_ref, gx_ref, dp_ref, acc_ref, dba_ref, dwa_ref):
        _zero_first(acc_ref, dba_ref, dwa_ref)
        md = lambda a: a.astype(MXU_DTYPE)
        g, sc = vec_ref[0:1, :], vec_ref[2:3, :]
        sums, dba, dwa = None, None, None
        for rows in (slice(0, tt // 2), slice(tt // 2, tt)):
            both = lambda a_ref, b_ref: md(a_ref[rows, :].astype(F32) + b_ref[rows, :].astype(F32))
            dla = jnp.concatenate([dlaf_ref[rows, :], dlab_ref[rows, :]], axis=1)
            dz = dla * (1.0 - jnp.exp(TAU * la_ref[rows, :])) * (1.0 / TAU)
            rr = r_ref[rows, :]
            dba_h = jnp.sum(dz, axis=0, keepdims=True)
            dwa_h = _mm_tn(rr, dz)
            dr = _mm_nt(dz, wa_ref[...])
            dp = jnp.concatenate([du_ref[rows, :], both(dqf_ref, dqb_ref), both(dkf_ref, dkb_ref),
                                  both(dvf_ref, dvb_ref), dg_ref[rows, :], md(dr)], axis=1)
            dp_ref[rows, :] = dp
            dh = _mm(dp, w_ref[...])
            rn, xn, yy, _ = _norm_mod(x_ref[rows, :], g, vec_ref[1:2, :], sc)
            dsh, dsc, dgn, dx = _norm_mod_bwd(dh, rn, xn, yy, g, sc)
            gx_ref[rows, :] = dx1_ref[rows, :] + dx
            part = [dsh, dsc, dgn]
            sums = part if sums is None else [a + b for a, b in zip(sums, part)]
            dba = dba_h if dba is None else dba + dba_h
            dwa = dwa_h if dwa is None else dwa + dwa_h
        _acc_rows(dba_ref, [dba])
        dwa_ref[...] += dwa
        _acc_rows(acc_ref, sums)

    return pl.pallas_call(
        body, name="proj_bwd", grid=(t // tt,),
        in_specs=[_rows(tt, 1024), _rows(tt, DK), _rows(tt, DK), _rows(tt, DK), _rows(tt, DK), _rows(tt, DV),
                  _rows(tt, DV), _rows(tt, DV), _rows(tt, DK), _rows(tt, DK), _rows(tt, 512),
                  _rows(tt, 128), _rows(tt, D), _rows(tt, D), _full((8, D)), _full((DINP, D)), _full((128, 512))],
        out_specs=[_rows(tt, D), _rows(tt, DINP), _full((8, D)), _full((8, 512)), _full((128, 512))],
        out_shape=[_sds((t, D)), _sds((t, DINP), MXU_DTYPE), _sds((8, D)), _sds((8, 512)), _sds((128, 512))],
        compiler_params=_cparams(1),
    )(du, *dqkv, dg, dla_f, dla_b, la, r, x, dx1, vec, w_int, wa)


def _dot_exact01(m01, x):
    bf = jnp.bfloat16
    w = x.shape[1]
    hi = x.astype(bf)
    r1 = x - hi.astype(F32)
    mid = r1.astype(bf)
    lo = (r1 - mid.astype(F32)).astype(bf)
    y = jnp.dot(m01.astype(bf), jnp.concatenate([hi, mid, lo], axis=1), preferred_element_type=F32)
    return y[:, 0:w] + y[:, w:2 * w] + y[:, 2 * w:3 * w]


def _gla_chunk(d, qc, kc, la_c, pair_a=False):
    row = lax.broadcasted_iota(jnp.int32, (CHUNK, CHUNK), 0)
    col = lax.broadcasted_iota(jnp.int32, (CHUNK, CHUNK), 1)
    cum = ((col <= row) if d == 0 else (col >= row)).astype(F32)
    cum_t = ((col >= row) if d == 0 else (col <= row)).astype(F32)
    cum4 = jnp.concatenate([cum] * NH, axis=0)
    head_of_lane = lax.broadcasted_iota(jnp.int32, (1, DK), 1) // HK
    b = _dot_exact01(cum, la_c)
    bl = jnp.sum(la_c, axis=0, keepdims=True)
    eb = jnp.exp(b)
    enb = jnp.exp(-b)
    ekd = jnp.exp(bl - b)
    qt = qc * (HK ** -0.5) * eb
    kt = kc * enb
    kd = kc * ekd
    qst = jnp.concatenate([jnp.where(head_of_lane == h, qt, 0.0) for h in range(NH)], axis=0)
    if pair_a:
        a = jnp.concatenate([_mm_nt(_pair_lanes(_pair_rows(qst, p), p), _pair_lanes(kt, p)) for p in range(NPAIR)],
                            axis=0)
    else:
        a = _mm_nt(qst, kt)
    a = a * cum4
    return cum_t, cum4, head_of_lane, eb, enb, ekd, qt, kt, kd, qst, a, jnp.exp(bl)


NPAIR = NH // 2


def _pair_rows(x, p):
    return x[2 * p * CHUNK:(2 * p + 2) * CHUNK]


def _pair_lanes(x, p):
    return x[:, 2 * p * HK:(2 * p + 2) * HK]


def _pair_fold(r):
    half = lax.broadcasted_iota(jnp.int32, (1, 2 * HK), 1) // HK
    return jnp.where(half == 0, r[0:CHUNK], 0.0) + jnp.where(half == 1, r[CHUNK:2 * CHUNK], 0.0)


def gla_fwd(q, k, v, la, s0, cb):
    t = q.shape[0]
    nc = t // CHUNK
    nb = nc // cb

    def body(qf_ref, kf_ref, vf_ref, laf_ref, qb_ref, kb_ref, vb_ref, lab_ref, s0_ref,
             of_ref, ob_ref, sf_ref, sb_ref, sfin_ref, s_scr):
        i = pl.program_id(0)

        @pl.when(i == 0)
        def _():
            s_scr[...] = s0_ref[...]

        def chunk(d, jj, q_ref, k_ref, v_ref, la_ref, o_ref, sall_ref):
            rows = slice(jj * CHUNK, (jj + 1) * CHUNK)
            vc = v_ref[rows, :]
            _, _, head_of_lane, _, _, _, _, _, kd, qst, a, dec = _gla_chunk(
                d, q_ref[rows, :], k_ref[rows, :], la_ref[rows, :], pair_a=True)
            s = s_scr[d]
            sall_ref[jj] = s
            inter = jnp.concatenate(
                [_mm_nt(_pair_lanes(_pair_rows(qst, p), p), _pair_lanes(s, p)) for p in range(NPAIR)], axis=0)
            outs = []
            for h in range(NH):
                hs = slice(h * CHUNK, (h + 1) * CHUNK)
                outs.append(_mm(a[hs], vc[:, h * HV:(h + 1) * HV]) + inter[hs])
            o_ref[rows, :] = jnp.concatenate(outs, axis=1)
            half = lax.broadcasted_iota(jnp.int32, (1, 2 * HK), 1) // HK
            upd = []
            for p in range(NPAIR):
                kv = _mm_tn(vc[:, 2 * p * HV:(2 * p + 2) * HV], _pair_lanes(kd, p))
                upd.append(jnp.where(half == 0, kv[0:HV], 0.0) + jnp.where(half == 1, kv[HV:2 * HV], 0.0))
            s_scr[d] = dec * s + jnp.concatenate(upd, axis=1)

        for j in range(cb):
            chunk(0, j, qf_ref, kf_ref, vf_ref, laf_ref, of_ref, sf_ref)
            chunk(1, cb - 1 - j, qb_ref, kb_ref, vb_ref, lab_ref, ob_ref, sb_ref)

        @pl.when(i == nb - 1)
        def _():
            sfin_ref[...] = s_scr[...]

    tb = cb * CHUNK
    fwd = lambda w, c=0: pl.BlockSpec((tb, w), lambda i: (i, c))
    bwd = lambda w, c=0: pl.BlockSpec((tb, w), lambda i: (nb - 1 - i, c))
    return pl.pallas_call(
        body, name="gla_fwd", grid=(nb,),
        in_specs=[fwd(DK), fwd(DK), fwd(DV), fwd(DK, 0), bwd(DK), bwd(DK), bwd(DV), bwd(DK, 1), _full((2, HV, DK))],
        out_specs=[fwd(DV), bwd(DV), pl.BlockSpec((cb, HV, DK), lambda i: (i, 0, 0)),
                   pl.BlockSpec((cb, HV, DK), lambda i: (nb - 1 - i, 0, 0)), _full((2, HV, DK))],
        out_shape=[_sds((t, DV)), _sds((t, DV)), _sds((nc, HV, DK)), _sds((nc, HV, DK)), _sds((2, HV, DK))],
        scratch_shapes=[pltpu.VMEM((2, HV, DK), F32)],
        compiler_params=_cparams(1),
    )(q, k, v, la, q, k, v, la, s0)


def gla_bwd(q, k, v, la, do, sall_f, sall_b, dsfin, cb):
    t = q.shape[0]
    nc = t // CHUNK
    nb = nc // cb

    def body(qf_ref, kf_ref, vf_ref, laf_ref, dof_ref, sf_ref, qb_ref, kb_ref, vb_ref, lab_ref, dob_ref, sb_ref,
             dsfin_ref, dqf_ref, dkf_ref, dvf_ref, dlaf_ref, dqb_ref, dkb_ref, dvb_ref, dlab_ref, ds0_ref, ds_scr):
        i = pl.program_id(0)

        @pl.when(i == 0)
        def _():
            ds_scr[...] = dsfin_ref[...]

        def chunk(d, jj, q_ref, k_ref, v_ref, la_ref, do_ref, sall_ref, dq_ref, dk_ref, dv_ref, dla_ref):
            rows = slice(jj * CHUNK, (jj + 1) * CHUNK)
            vc = v_ref[rows, :]
            doc = do_ref[rows, :]
            cum_t, cum4, head_of_lane, eb, enb, ekd, qt, kt, kd, qst, a, dec = _gla_chunk(
                d, q_ref[rows, :], k_ref[rows, :], la_ref[rows, :])
            s = sall_ref[jj]
            ds = ds_scr[d]
            hv = lambda x, h: x[:, h * HV:(h + 1) * HV]
            hr = lambda x, h: x[h * CHUNK:(h + 1) * CHUNK]
            fold = lambda x: functools.reduce(
                lambda p, c: p + c, [jnp.where(head_of_lane == h, hr(x, h), 0.0) for h in range(NH)])
            dost = jnp.concatenate([hv(doc, h) for h in range(NH)], axis=0)
            vst = jnp.concatenate([hv(vc, h) for h in range(NH)], axis=0)
            da = jnp.concatenate([_mm_nt(hv(doc, h), hv(vc, h)) for h in range(NH)], axis=0) * cum4
            pairs = range(NPAIR)
            dqt = fold(_mm(da, kt)) + jnp.concatenate(
                [_pair_fold(_mm(_pair_rows(dost, p), _pair_lanes(s, p))) for p in pairs], axis=1)
            dkt = _mm_tn(da, qst)
            kdst = jnp.concatenate([jnp.where(head_of_lane == h, kd, 0.0) for h in range(NH)], axis=0)
            dv_inter = jnp.concatenate(
                [_mm_nt(_pair_lanes(_pair_rows(kdst, p), p), _pair_lanes(ds, p)) for p in pairs], axis=0)
            dv_ref[rows, :] = jnp.concatenate(
                [_mm_tn(hr(a, h), hv(doc, h)) + hr(dv_inter, h) for h in range(NH)], axis=1).astype(MXU_DTYPE)
            dkd = jnp.concatenate([_pair_fold(_mm(_pair_rows(vst, p), _pair_lanes(ds, p))) for p in pairs], axis=1)
            ds_scr[d] = dec * ds + jnp.concatenate(
                [_mm_tn(_pair_rows(dost, p), _pair_lanes(_pair_rows(qst, p), p)) for p in pairs], axis=1)
            tkd = dkd * kd
            db = dqt * qt - dkt * kt - tkd
            dbl = jnp.sum(ds * s, axis=0, keepdims=True) * dec + jnp.sum(tkd, axis=0, keepdims=True)
            dla_ref[rows, :] = _dot_exact01(cum_t, db) + dbl
            dq_ref[rows, :] = (dqt * eb * (HK ** -0.5)).astype(MXU_DTYPE)
            dk_ref[rows, :] = (dkt * enb + dkd * ekd).astype(MXU_DTYPE)

        for j in range(cb):
            chunk(0, cb - 1 - j, qf_ref, kf_ref, vf_ref, laf_ref, dof_ref, sf_ref, dqf_ref, dkf_ref, dvf_ref, dlaf_ref)
            chunk(1, j, qb_ref, kb_ref, vb_ref, lab_ref, dob_ref, sb_ref, dqb_ref, dkb_ref, dvb_ref, dlab_ref)

        @pl.when(i == nb - 1)
        def _():
            ds0_ref[...] = ds_scr[...]

    tb = cb * CHUNK
    rev = lambda w, c=0: pl.BlockSpec((tb, w), lambda i: (nb - 1 - i, c))
    fro = lambda w, c=0: pl.BlockSpec((tb, w), lambda i: (i, c))
    st_rev = pl.BlockSpec((cb, HV, DK), lambda i: (nb - 1 - i, 0, 0))
    st_fro = pl.BlockSpec((cb, HV, DK), lambda i: (i, 0, 0))
    md = MXU_DTYPE
    return pl.pallas_call(
        body, name="gla_bwd", grid=(nb,),
        in_specs=[rev(DK), rev(DK), rev(DV), rev(DK, 0), rev(DV), st_rev,
                  fro(DK), fro(DK), fro(DV), fro(DK, 1), fro(DV), st_fro, _full((2, HV, DK))],
        out_specs=[rev(DK), rev(DK), rev(DV), rev(DK), fro(DK), fro(DK), fro(DV), fro(DK), _full((2, HV, DK))],
        out_shape=[_sds((t, DK), md), _sds((t, DK), md), _sds((t, DV), md), _sds((t, DK)),
                   _sds((t, DK), md), _sds((t, DK), md), _sds((t, DV), md), _sds((t, DK)), _sds((2, HV, DK))],
        scratch_shapes=[pltpu.VMEM((2, HV, DK), F32)],
        compiler_params=_cparams(1),
    )(q, k, v, la, do, sall_f, q, k, v, la, do, sall_b, dsfin)


def _seg_pos(tt):
    return lax.broadcasted_iota(jnp.int32, (tt, 1), 0) % SEG


def _shifted(x, s, pos, tt):
    y = x if s == 0 else pltpu.roll(x, (-s) % tt, 0)
    return jnp.where((pos + s >= 0) & (pos + s < SEG), y, 0.0)


def _head_norm(o, gn):
    rs, xs = [], []
    for h in range(NH):
        oh = o[:, h * HV:(h + 1) * HV]
        r = lax.rsqrt(jnp.mean(oh * oh, axis=-1, keepdims=True) + EPS)
        rs.append(r)
        xs.append(oh * r)
    return rs, xs


def merge_fwd(u, g, o_f, o_b, x, vec, vc, convw, w_out, tt):
    t = x.shape[0]

    def body(u_ref, g_ref, of_ref, ob_ref, x_ref, vec_ref, vc_ref, cw_ref, w_ref, x1_ref, cat_ref, mix_ref, yc_ref,
             h2_ref):
        a = u_ref[:, 0:DC]
        gate = u_ref[:, DC:2 * DC]
        vv = a * _sigmoid(gate)
        pos = _seg_pos(tt)
        cw = cw_ref[...]
        yc = jnp.zeros((tt, DC), F32) + vc_ref[1:2, :]
        for j in range(CW):
            yc = yc + _shifted(vv, j - CPAD, pos, tt) * cw[j:j + 1, :]
        yc_ref[...] = yc
        mu = jnp.mean(yc, axis=-1, keepdims=True)
        yd = yc - mu
        rs = lax.rsqrt(jnp.mean(yd * yd, axis=-1, keepdims=True) + EPS)
        ln = yd * rs * vc_ref[2:3, :] + vc_ref[3:4, :]
        conv_o = ln * _sigmoid(ln)
        o = of_ref[...] + ob_ref[...]
        _, xs = _head_norm(o, None)
        gg = g_ref[...]
        o2g = jnp.concatenate(xs, axis=1) * vc_ref[0:1, :] * (gg * _sigmoid(gg))
        cat = jnp.concatenate([conv_o, o2g], axis=1).astype(MXU_DTYPE)
        cat_ref[...] = cat
        mix = _mm(cat, w_ref[...])
        mix_ref[...] = mix
        x1 = x_ref[...] + vec_ref[0:1, :] * mix
        x1_ref[...] = x1
        _, _, _, h2 = _norm_mod(x1, vec_ref[1:2, :], vec_ref[2:3, :], vec_ref[3:4, :])
        h2_ref[...] = h2.astype(MXU_DTYPE)

    return pl.pallas_call(
        body, name="merge_fwd", grid=(t // tt,),
        in_specs=[_rows(tt, 1024), _rows(tt, DV), _rows(tt, DV), _rows(tt, DV), _rows(tt, D),
                  _full((8, D)), _full((8, DC)), _full((32, DC)), _full((D, D))],
        out_specs=[_rows(tt, D), _rows(tt, D), _rows(tt, D), _rows(tt, DC), _rows(tt, D)],
        out_shape=[_sds((t, D)), _sds((t, D), MXU_DTYPE), _sds((t, D)), _sds((t, DC)), _sds((t, D), MXU_DTYPE)],
        compiler_params=_cparams(1),
    )(u, g, o_f, o_b, x, vec, vc, convw, w_out)


def merge_bwd(dx1, mix, u, g, o_f, o_b, yc, vec, vc, convw, w_out, tt):
    t = dx1.shape[0]

    def body(dx1_ref, mix_ref, u_ref, g_ref, of_ref, ob_ref, yc_ref, vec_ref, vc_ref, cw_ref, w_ref,
             du_ref, dg_ref, do_ref, dmix_ref, acc1_ref, acc2_ref, dcw_ref):
        _zero_first(acc1_ref, acc2_ref, dcw_ref)
        dx1v = dx1_ref[...]
        dg1 = jnp.sum(dx1v * mix_ref[...], axis=0, keepdims=True)
        dmix = (vec_ref[0:1, :] * dx1v).astype(MXU_DTYPE)
        dmix_ref[...] = dmix
        dcat = _mm_nt(dmix, w_ref[...])
        dconv_o = dcat[:, 0:DC]
        do2 = dcat[:, DC:2 * DC]
        gn = vc_ref[0:1, :]
        o = of_ref[...] + ob_ref[...]
        rs, xs = _head_norm(o, None)
        xn = jnp.concatenate(xs, axis=1)
        gg = g_ref[...]
        sg = _sigmoid(gg)
        don = do2 * (gg * sg)
        dg_ref[...] = (do2 * (xn * gn) * (sg * (1.0 + gg * (1.0 - sg)))).astype(MXU_DTYPE)
        dgn = jnp.sum(don * xn, axis=0, keepdims=True)
        dxn = don * gn
        dos = []
        for h in range(NH):
            dh = dxn[:, h * HV:(h + 1) * HV]
            dos.append(rs[h] * (dh - xs[h] * jnp.mean(dh * xs[h], axis=-1, keepdims=True)))
        do_ref[...] = jnp.concatenate(dos, axis=1).astype(MXU_DTYPE)
        yc = yc_ref[...]
        mu = jnp.mean(yc, axis=-1, keepdims=True)
        yd = yc - mu
        rstd = lax.rsqrt(jnp.mean(yd * yd, axis=-1, keepdims=True) + EPS)
        yhat = yd * rstd
        lg = vc_ref[2:3, :]
        ln = yhat * lg + vc_ref[3:4, :]
        sl = _sigmoid(ln)
        dln = dconv_o * (sl * (1.0 + ln * (1.0 - sl)))
        dlb = jnp.sum(dln, axis=0, keepdims=True)
        dlg = jnp.sum(dln * yhat, axis=0, keepdims=True)
        dyh = dln * lg
        dyc = rstd * (dyh - jnp.mean(dyh, axis=-1, keepdims=True)
                      - yhat * jnp.mean(dyh * yhat, axis=-1, keepdims=True))
        dcb = jnp.sum(dyc, axis=0, keepdims=True)
        a = u_ref[:, 0:DC]
        gate = u_ref[:, DC:2 * DC]
        sgt = _sigmoid(gate)
        vv = a * sgt
        pos = _seg_pos(tt)
        cw = cw_ref[...]
        dvv = jnp.zeros((tt, DC), F32)
        dws = []
        for j in range(CW):
            shifted_dyc = _shifted(dyc, CPAD - j, pos, tt)
            dvv = dvv + shifted_dyc * cw[j:j + 1, :]
            dws.append(jnp.sum(shifted_dyc * vv, axis=0, keepdims=True))
        dws.append(jnp.zeros((1, DC), F32))
        du_ref[:, 0:DC] = (dvv * sgt).astype(MXU_DTYPE)
        du_ref[:, DC:2 * DC] = (dvv * a * sgt * (1.0 - sgt)).astype(MXU_DTYPE)
        _acc_rows(acc1_ref, [dg1])
        _acc_rows(acc2_ref, [dgn, dcb, dlg, dlb])
        dcw_ref[...] += jnp.concatenate(dws, axis=0)

    return pl.pallas_call(
        body, name="merge_bwd", grid=(t // tt,),
        in_specs=[_rows(tt, D), _rows(tt, D), _rows(tt, 1024), _rows(tt, DV), _rows(tt, DV), _rows(tt, DV),
                  _rows(tt, DC),
                  _full((8, D)), _full((8, DC)), _full((32, DC)), _full((D, D))],
        out_specs=[_rows(tt, 1024), _rows(tt, DV), _rows(tt, DV), _rows(tt, D), _full((8, D)), _full((8, DC)),
                   _full((32, DC))],
        out_shape=[_sds((t, 1024), MXU_DTYPE), _sds((t, DV), MXU_DTYPE), _sds((t, DV), MXU_DTYPE),
                   _sds((t, D), MXU_DTYPE), _sds((8, D)),
                   _sds((8, DC)), _sds((32, DC))],
        compiler_params=_cparams(1),
    )(dx1, mix, u, g, o_f, o_b, yc, vec, vc, convw, w_out)


FN = DFF // 2


def ffn_gate_up(h2, wg_t, wu_t, tt):
    t = h2.shape[0]

    def body(h2_ref, wg_ref, wu_ref, s_ref, d_ref, hid_ref):
        h2v = h2_ref[...]
        gt = _mm_nt(h2v, wg_ref[...])
        up = _mm_nt(h2v, wu_ref[...])
        sg = _sigmoid(gt)
        act = gt * sg
        s_ref[...] = act.astype(MXU_DTYPE)
        d_ref[...] = (up * (sg * (1.0 + gt * (1.0 - sg)))).astype(MXU_DTYPE)
        hid_ref[...] = (act * up).astype(MXU_DTYPE)

    blk = pl.BlockSpec((tt, FN), lambda j, i: (i, j))
    wblk = pl.BlockSpec((FN, D), lambda j, i: (j, 0))
    return pl.pallas_call(
        body, name="ffn_gate_up", grid=(2, t // tt),
        in_specs=[pl.BlockSpec((tt, D), lambda j, i: (i, 0)), wblk, wblk],
        out_specs=[blk, blk, blk],
        out_shape=[_sds((t, DFF), MXU_DTYPE)] * 3,
        compiler_params=_cparams(2),
    )(h2, wg_t, wu_t)


def ffn_down_loss(hid, x1, tgt, vec, w_down, tt):
    t = x1.shape[0]

    def body(hid_ref, x1_ref, tgt_ref, vec_ref, w_ref, dx2_ref, dff_ref, acc_ref):
        _zero_first(acc_ref)
        g2 = vec_ref[0:1, :]
        fg = vec_ref[1:2, :]
        ff = _mm(hid_ref[...], w_ref[...])
        x2 = x1_ref[...] + g2 * ff
        rf = lax.rsqrt(jnp.mean(x2 * x2, axis=-1, keepdims=True) + EPS)
        xn = x2 * rf
        err = xn * fg - tgt_ref[...]
        dy = err * (1.0 / D)
        dfg = jnp.sum(dy * xn, axis=0, keepdims=True)
        dxn = dy * fg
        dx2 = rf * (dxn - xn * jnp.mean(dxn * xn, axis=-1, keepdims=True))
        dx2_ref[...] = dx2
        dff_ref[...] = (g2 * dx2).astype(MXU_DTYPE)
        dg2 = jnp.sum(dx2 * ff, axis=0, keepdims=True)
        loss = jnp.sum(err * err, axis=0, keepdims=True) * (0.5 / D)
        _acc_rows(acc_ref, [dg2, dfg, loss])

    return pl.pallas_call(
        body, name="ffn_down_loss", grid=(t // tt,),
        in_specs=[_rows(tt, DFF), _rows(tt, D), _rows(tt, D), _full((8, D)), _full((DFF, D))],
        out_specs=[_rows(tt, D), _rows(tt, D), _full((8, D))],
        out_shape=[_sds((t, D)), _sds((t, D), MXU_DTYPE), _sds((8, D))],
        compiler_params=_cparams(1),
    )(hid, x1, tgt, vec, w_down)


def ffn_dhid(dff, s, d, w_down, tt):
    t = dff.shape[0]

    def body(dff_ref, s_ref, d_ref, w_ref, dgt_ref, dup_ref):
        dhid = _mm_nt(dff_ref[...], w_ref[...])
        dgt_ref[...] = (dhid * d_ref[...].astype(F32)).astype(MXU_DTYPE)
        dup_ref[...] = (dhid * s_ref[...].astype(F32)).astype(MXU_DTYPE)

    blk = pl.BlockSpec((tt, FN), lambda j, i: (i, j))
    return pl.pallas_call(
        body, name="ffn_dhid", grid=(2, t // tt),
        in_specs=[pl.BlockSpec((tt, D), lambda j, i: (i, 0)), blk, blk, pl.BlockSpec((FN, D), lambda j, i: (j, 0))],
        out_specs=[blk, blk],
        out_shape=[_sds((t, DFF), MXU_DTYPE), _sds((t, DFF), MXU_DTYPE)],
        compiler_params=_cparams(2),
    )(dff, s, d, w_down)


def ffn_dh2(dgt, dup, x1, dx2, vec, wg_t, wu_t, tt):
    t = x1.shape[0]

    def body(dgt_ref, dup_ref, x1_ref, dx2_ref, vec_ref, wg_ref, wu_ref, dx1_ref, acc_ref):
        _zero_first(acc_ref)
        dh2 = _mm(dgt_ref[...], wg_ref[...]) + _mm(dup_ref[...], wu_ref[...])
        g, sc = vec_ref[0:1, :], vec_ref[2:3, :]
        r, xn, yy, _ = _norm_mod(x1_ref[...], g, vec_ref[1:2, :], sc)
        dsh, dsc, dgn, dx = _norm_mod_bwd(dh2, r, xn, yy, g, sc)
        dx1_ref[...] = dx2_ref[...] + dx
        _acc_rows(acc_ref, [dsh, dsc, dgn])

    return pl.pallas_call(
        body, name="ffn_dh2", grid=(t // tt,),
        in_specs=[_rows(tt, DFF), _rows(tt, DFF), _rows(tt, D), _rows(tt, D), _full((8, D)), _full((DFF, D)),
                  _full((DFF, D))],
        out_specs=[_rows(tt, D), _full((8, D))],
        out_shape=[_sds((t, D)), _sds((8, D))],
        compiler_params=_cparams(1),
    )(dgt, dup, x1, dx2, vec, wg_t, wu_t)


def tn_matmul(a, b, bm, bt, init=None):
    t, m = a.shape
    n = b.shape[1]
    nk = t // bt

    def body(*refs):
        if init is None:
            a_ref, b_ref, o_ref, wire_ref = refs
        else:
            a_ref, b_ref, i_ref, o_ref, wire_ref = refs
        @pl.when(pl.program_id(1) == 0)
        def _():
            o_ref[...] = jnp.zeros_like(o_ref) if init is None else i_ref[...]

        o_ref[...] += _mm_tn(a_ref[...], b_ref[...])

        @pl.when(pl.program_id(1) == nk - 1)
        def _():
            wire_ref[...] = o_ref[...].astype(WIRE_DTYPE)

    in_specs = [pl.BlockSpec((bt, bm), lambda i, k: (k, i)), pl.BlockSpec((bt, n), lambda i, k: (k, 0))]
    args = [a, b]
    if init is not None:
        in_specs.append(pl.BlockSpec((bm, n), lambda i, k: (i, 0)))
        args.append(init)
    oblk = pl.BlockSpec((bm, n), lambda i, k: (i, 0))
    return pl.pallas_call(
        body, name="tn_matmul", grid=(m // bm, nk),
        in_specs=in_specs, out_specs=[oblk, oblk],
        out_shape=[_sds((m, n)), _sds((m, n), WIRE_DTYPE)], compiler_params=_cparams(2),
    )(*args)


def _adamw(w, g, m, v):
    m = ADAM_B1 * m + (1.0 - ADAM_B1) * g
    v = ADAM_B2 * v + (1.0 - ADAM_B2) * (g * g)
    m_hat = m / (1.0 - ADAM_B1 ** ADAM_STEP)
    v_hat = v / (1.0 - ADAM_B2 ** ADAM_STEP)
    delta = -ADAM_LR * (m_hat / (jnp.sqrt(v_hat) + ADAM_EPS) + ADAM_WD * w)
    return delta, m, v


def adamw_sharded(own, recv, w, m, v):
    shape = w.shape

    def body(own_ref, recv_ref, w_ref, m_ref, v_ref, g_ref, d_ref, mo_ref, vo_ref):
        g = own_ref[...]
        for k in range(N_DEV - 1):
            g = g + recv_ref[k].astype(F32)
        g_ref[...] = g
        d_ref[...], mo_ref[...], vo_ref[...] = _adamw(w_ref[...], g, m_ref[...], v_ref[...])

    return pl.pallas_call(
        body, name="adamw_sharded",
        in_specs=[_full(shape), _full((N_DEV - 1,) + shape), _full(shape), _full(shape), _full(shape)],
        out_specs=[_full(shape)] * 4, out_shape=[_sds(shape)] * 4, grid=(1,),
        compiler_params=_cparams(1),
    )(own, recv, w, m, v)


def adamw_small(items):
    n = len(items)
    flat = [a for it in items for a in it]

    def body(*refs):
        ins, outs = refs[:4 * n], refs[4 * n:]
        for i in range(n):
            g, w, m, v = (r[...] for r in ins[4 * i:4 * i + 4])
            outs[3 * i][...], outs[3 * i + 1][...], outs[3 * i + 2][...] = _adamw(w, g, m, v)

    out = pl.pallas_call(
        body, name="adamw_small", grid=(1,),
        in_specs=[_full(a.shape) for a in flat],
        out_specs=[_full(it[1].shape) for it in items for _ in range(3)],
        out_shape=[_sds(it[1].shape) for it in items for _ in range(3)],
        compiler_params=_cparams(1),
    )(*flat)
    return [tuple(out[3 * i:3 * i + 3]) for i in range(n)]


def _mesh_pos():
    x, y, c = lax.axis_index("x"), lax.axis_index("y"), lax.axis_index("c")
    me = 4 * x + 2 * y + c
    peers = []
    for k in range(1, N_DEV):
        peers.append(((1 - x) if (k >> 2) & 1 else x, (1 - y) if (k >> 1) & 1 else y, (1 - c) if k & 1 else c))
    return me, peers


def _all_gather_issue(buf, send_sems, recv_sems, me, peers):
    sends = []
    for k, peer in enumerate(peers):
        cp = pltpu.make_async_remote_copy(src_ref=buf.at[me], dst_ref=buf.at[me], send_sem=send_sems.at[k],
                                          recv_sem=recv_sems.at[k], device_id=peer, device_id_type=MESH)
        cp.start()
        sends.append(cp)
    return sends


def _all_gather_finish(buf, send_sems, recv_sems, me, peers, sends):
    for k, peer in enumerate(peers):
        src = jnp.bitwise_xor(me, k + 1)
        pltpu.make_async_remote_copy(src_ref=buf.at[src], dst_ref=buf.at[src], send_sem=send_sems.at[k],
                                     recv_sem=recv_sems.at[k], device_id=peer, device_id_type=MESH).wait_recv()
    for cp in sends:
        cp.wait_send()


def _all_gather(buf, send_sems, recv_sems, me, peers):
    _all_gather_finish(buf, send_sems, recv_sems, me, peers, _all_gather_issue(buf, send_sems, recv_sems, me, peers))


_VMEM = pl.BlockSpec(memory_space=pltpu.VMEM)
_ANY = pl.BlockSpec(memory_space=pl.ANY)
_SEMS = pltpu.SemaphoreType.DMA((N_DEV - 1,))


def mod_forward(c, c_ctx, w_mod_sh, b_mod, sm_pack, w_first):
    ncol = w_mod_sh.shape[1]

    def body(c_ref, cc_ref, w_ref, b_ref, sm_ref, wf_ref, mod_ref, s_ref, smt_ref, wall_ref, cbuf, pbuf, smbuf,
             s1, r1, s2, r2, s3, r3, ws, wr, wl):
        me, peers = _mesh_pos()
        x, y, cc = lax.axis_index("x"), lax.axis_index("y"), lax.axis_index("c")
        sibling = (x, y, 1 - cc)
        chips = [(1 - x, y), (x, 1 - y), (1 - x, 1 - y)]
        slot = lambda px, py, pc: wall_ref.at[4 * px + 2 * py + pc]

        def wcopy(k, block, to, src=None):
            return pltpu.make_async_remote_copy(
                src_ref=slot(*block) if src is None else src, dst_ref=slot(*block), send_sem=ws.at[k],
                recv_sem=wr.at[k], device_id=to, device_id_type=MESH)

        mine = pltpu.make_async_copy(wf_ref, slot(x, y, cc), wl)
        mine.start()
        first = [wcopy(0, (x, y, cc), sibling, src=wf_ref)]
        first += [wcopy(1 + j, (x, y, cc), (*chip, cc), src=wf_ref) for j, chip in enumerate(chips)]
        for cp in first:
            cp.start()
        smbuf[me] = sm_ref[...]
        sm_sends = _all_gather_issue(smbuf, s3, r3, me, peers)
        cbuf[me] = jnp.broadcast_to(c_ref[...], (8, D))
        _all_gather(cbuf, s1, r1, me, peers)
        rows = [cbuf[j, 0:1, :] for j in range(N_DEV)] + [cc_ref[...], jnp.zeros((7, D), F32)]
        sx = jnp.concatenate(rows, axis=0)
        s = sx * _sigmoid(sx)
        s_ref[...] = s
        pbuf[me] = _hi(s, w_ref[...])
        _all_gather(pbuf, s2, r2, me, peers)
        for j in range(N_DEV):
            mod_ref[:, j * ncol:(j + 1) * ncol] = pbuf[j] + b_ref[:, j * ncol:(j + 1) * ncol]
        _all_gather_finish(smbuf, s3, r3, me, peers, sm_sends)
        tot = smbuf[0]
        for j in range(1, N_DEV):
            tot = tot + smbuf[j]
        smt_ref[...] = tot
        passed = [wcopy(4 + j, (*chip, cc), sibling) for j, chip in enumerate(chips)]
        for j, chip in enumerate(chips):
            wcopy(1 + j, (*chip, cc), (x, y, cc)).wait_recv()
            passed[j].start()
        wcopy(0, (x, y, 1 - cc), (x, y, cc)).wait_recv()
        for j, chip in enumerate(chips):
            wcopy(4 + j, (*chip, 1 - cc), (x, y, cc)).wait_recv()
        for cp in first + passed:
            cp.wait_send()
        mine.wait()

    return pl.pallas_call(
        body, name="mod_forward",
        in_specs=[_VMEM] * 6, out_specs=[_VMEM] * 4,
        out_shape=[_sds((16, N_DEV * ncol)), _sds((16, D)), _sds(sm_pack.shape),
                   _sds((N_DEV,) + w_first.shape, w_first.dtype)],
        scratch_shapes=[pltpu.VMEM((N_DEV, 8, D), F32), pltpu.VMEM((N_DEV, 16, ncol), F32),
                        pltpu.VMEM((N_DEV,) + sm_pack.shape, F32), _SEMS, _SEMS, _SEMS, _SEMS, _SEMS, _SEMS,
                        _SEMS, _SEMS, pltpu.SemaphoreType.DMA],
        compiler_params=pltpu.CompilerParams(vmem_limit_bytes=VMEM_LIMIT),
    )(c, c_ctx, w_mod_sh, b_mod, sm_pack, w_first)


def sum_blocks(gat, loss_row):
    def body(g_ref, tot_ref, loss_ref):
        tot = g_ref[0]
        for j in range(1, N_DEV):
            tot = tot + g_ref[j]
        tot_ref[...] = tot
        loss_ref[...] = jnp.sum(tot[loss_row:loss_row + 1, :], axis=1, keepdims=True)

    return pl.pallas_call(
        body, name="sum_blocks", in_specs=[_VMEM], out_specs=[_VMEM, _VMEM],
        out_shape=[_sds(gat.shape[1:]), _sds((1, 1))],
        compiler_params=pltpu.CompilerParams(vmem_limit_bytes=VMEM_LIMIT),
    )(gat)


def mod_backward(s, dm_sh, w, m, v, cc, m_cc, v_cc):
    shape = w.shape

    def body(s_ref, dm_ref, w_ref, m_ref, v_ref, cc_ref, mcc_ref, vcc_ref,
             gw_ref, dw_ref, mw_ref, vw_ref, gc_ref, dc_ref, mc_ref, vc_ref, pbuf, send_sems, recv_sems):
        me, peers = _mesh_pos()
        wv = w_ref[...]
        pbuf[me] = _hi_nt(dm_ref[8:16, :], wv)
        _all_gather(pbuf, send_sems, recv_sems, me, peers)
        g = _hi_tn(s_ref[...], dm_ref[...])
        gw_ref[...] = g
        dw_ref[...], mw_ref[...], vw_ref[...] = _adamw(wv, g, m_ref[...], v_ref[...])
        tot = pbuf[0]
        for j in range(1, N_DEV):
            tot = tot + pbuf[j]
        ccv = cc_ref[...]
        sg = _sigmoid(ccv)
        gc = tot[0:1, :] * (sg * (1.0 + ccv * (1.0 - sg)))
        gc_ref[...] = gc
        dc_ref[...], mc_ref[...], vc_ref[...] = _adamw(ccv, gc, mcc_ref[...], vcc_ref[...])

    return pl.pallas_call(
        body, name="mod_backward", in_specs=[_VMEM] * 8, out_specs=[_VMEM] * 8,
        out_shape=[_sds(shape)] * 4 + [_sds((1, D))] * 4,
        scratch_shapes=[pltpu.VMEM((N_DEV, 8, D), F32), _SEMS, _SEMS],
        compiler_params=pltpu.CompilerParams(vmem_limit_bytes=VMEM_LIMIT),
    )(s, dm_sh, w, m, v, cc, m_cc, v_cc)


_HBM = pl.BlockSpec(memory_space=pltpu.HBM)
_SEM = pl.BlockSpec(memory_space=pltpu.SEMAPHORE)
_EFFECT = pltpu.SideEffectType.DATAFLOW_SIDE_EFFECTING
_hbm = lambda a: pltpu.with_memory_space_constraint(a, pltpu.HBM)


def gather_start(shards, me, tag):
    n = len(shards)
    sems = pltpu.SemaphoreType.DMA((7 * n,))
    lands = [lax.dynamic_update_slice(lax.empty((N_DEV,) + s.shape, s.dtype), s[None], (me, 0, 0)) for s in shards]

    def body(*refs):
        s_refs, l_refs = refs[:n], refs[n:2 * n]
        send_sems, recv_sems = refs[2 * n], refs[2 * n + 1]
        token = refs[-1]
        my, peers = _mesh_pos()
        for w in range(n):
            for k, peer in enumerate(peers):
                pltpu.make_async_remote_copy(
                    src_ref=s_refs[w], dst_ref=l_refs[w].at[my], send_sem=send_sems.at[w * 7 + k],
                    recv_sem=recv_sems.at[w * 7 + k], device_id=peer, device_id_type=MESH).start()
        token[...] = jnp.zeros_like(token)

    out = pl.pallas_call(
        body, name="gather_start_" + tag,
        out_shape=(sems, sems) + tuple(pltpu.HBM(a.shape, a.dtype) for a in list(shards) + lands) + (_sds((8, 128)),),
        in_specs=(_HBM,) * (2 * n), out_specs=(_SEM, _SEM) + (_HBM,) * (2 * n) + (_VMEM,),
        input_output_aliases={i: i + 2 for i in range(2 * n)},
        compiler_params=pltpu.CompilerParams(has_side_effects=_EFFECT),
    )(*[_hbm(a) for a in list(shards) + lands])
    return out[0], out[1], list(out[2:2 + n]), list(out[2 + n:2 + 2 * n]), out[-1]


def gather_wait(send_sems, recv_sems, shards, lands, after, tag):
    n = len(shards)

    def body(*refs):
        s_refs, l_refs = refs[:n], refs[n:2 * n]
        send_sems, recv_sems = refs[2 * n], refs[2 * n + 1]
        my, peers = _mesh_pos()
        for w in range(n):
            for k, peer in enumerate(peers):
                src = jnp.bitwise_xor(my, k + 1)
                cp = pltpu.make_async_remote_copy(
                    src_ref=s_refs[w], dst_ref=l_refs[w].at[src], send_sem=send_sems.at[w * 7 + k],
                    recv_sem=recv_sems.at[w * 7 + k], device_id=peer, device_id_type=MESH)
                cp.wait_send()
                cp.wait_recv()

    out = pl.pallas_call(
        body, name="gather_wait_" + tag,
        out_shape=tuple(pltpu.HBM(a.shape, a.dtype) for a in list(shards) + list(lands)),
        in_specs=(_HBM,) * (2 * n) + (_SEM, _SEM, _ANY), out_specs=(_HBM,) * (2 * n),
        input_output_aliases={i: i for i in range(2 * n)},
        compiler_params=pltpu.CompilerParams(has_side_effects=_EFFECT),
    )(*shards, *lands, send_sems, recv_sems, after)
    return list(out[n:2 * n])


def scatter_start(grads, tag):
    n = len(grads)
    sems = pltpu.SemaphoreType.DMA((7 * n,))
    lands = [lax.empty((N_DEV - 1,) + g.shape[1:], g.dtype) for g in grads]

    def body(*refs):
        g_refs, l_refs = refs[:n], refs[n:2 * n]
        send_sems, recv_sems = refs[2 * n], refs[2 * n + 1]
        token = refs[-1]
        me, peers = _mesh_pos()
        for w in range(n):
            for k, peer in enumerate(peers):
                dst = jnp.bitwise_xor(me, k + 1)
                pltpu.make_async_remote_copy(
                    src_ref=g_refs[w].at[dst], dst_ref=l_refs[w].at[k], send_sem=send_sems.at[w * 7 + k],
                    recv_sem=recv_sems.at[w * 7 + k], device_id=peer, device_id_type=MESH).start()
        token[...] = jnp.zeros_like(token)

    out = pl.pallas_call(
        body, name="scatter_start_" + tag,
        out_shape=(sems, sems) + tuple(pltpu.HBM(a.shape, a.dtype) for a in list(grads) + lands) + (_sds((8, 128)),),
        in_specs=(_HBM,) * (2 * n), out_specs=(_SEM, _SEM) + (_HBM,) * (2 * n) + (_VMEM,),
        input_output_aliases={i: i + 2 for i in range(2 * n)},
        compiler_params=pltpu.CompilerParams(has_side_effects=_EFFECT),
    )(*[_hbm(a) for a in list(grads) + lands])
    return out[0], out[1], list(out[2:2 + n]), list(out[2 + n:2 + 2 * n]), out[-1]


def scatter_wait(send_sems, recv_sems, grads, lands, after, tag):
    n = len(grads)

    def body(*refs):
        g_refs, l_refs = refs[:n], refs[n:2 * n]
        send_sems, recv_sems = refs[2 * n], refs[2 * n + 1]
        me, peers = _mesh_pos()
        for w in range(n):
            for k, peer in enumerate(peers):
                dst = jnp.bitwise_xor(me, k + 1)
                cp = pltpu.make_async_remote_copy(
                    src_ref=g_refs[w].at[dst], dst_ref=l_refs[w].at[k], send_sem=send_sems.at[w * 7 + k],
                    recv_sem=recv_sems.at[w * 7 + k], device_id=peer, device_id_type=MESH)
                cp.wait_send()
                cp.wait_recv()

    out = pl.pallas_call(
        body, name="scatter_wait_" + tag,
        out_shape=tuple(pltpu.HBM(a.shape, a.dtype) for a in list(grads) + list(lands)),
        in_specs=(_HBM,) * (2 * n) + (_SEM, _SEM, _ANY), out_specs=(_HBM,) * (2 * n),
        input_output_aliases={i: i for i in range(2 * n)},
        compiler_params=pltpu.CompilerParams(has_side_effects=_EFFECT),
    )(*grads, *lands, send_sems, recv_sems, after)
    return list(out[n:2 * n])


def _vec8(rows, width):
    rid = lax.broadcasted_iota(jnp.int32, (8, width), 0)
    out = jnp.zeros((8, width), F32)
    for i, r in enumerate(rows):
        r = r.reshape(-1)
        r = jnp.pad(r, (0, width - r.shape[0]))
        out = jnp.where(rid == i, r[None, :], out)
    return out


def local_step(x, ctx, tgt, mod, mod_c, small, w_int, start, late_weights, grads_ready, small_ready, tt, tt_ctx, cb,
               cb_ctx):
    sh1, sc1, g1, sh2, sc2, g2 = [mod[i * D:(i + 1) * D] for i in range(6)]
    csh1, csc1 = mod_c[0:D], mod_c[D:2 * D]
    vec1 = _vec8([small["norm1_g"], sh1, sc1], D)
    vec1c = _vec8([small["norm1_g"], csh1, csc1], D)
    vec2 = _vec8([small["norm2_g"], sh2, sc2], D)
    vec3 = _vec8([g2, small["final_g"]], D)
    vecm = _vec8([g1, small["norm2_g"], sh2, sc2], D)
    vcm = _vec8([jnp.tile(small["gla_norm_g"].reshape(HV), NH), small["conv_b"], small["conv_ln_g"],
                 small["conv_ln_b"]], DC)
    convw = jnp.pad(small["conv_w"], ((0, 1), (0, 0)))
    wa = jnp.zeros((128, 512), F32)
    wa = wa.at[0:RANK, 0:DK].set(small["w_a2_f"]).at[RANK:2 * RANK, DK:2 * DK].set(small["w_a2_b"])
    ba = jnp.concatenate([small["b_a_f"].reshape(1, DK), small["b_a_b"].reshape(1, DK)], axis=1)

    _, _, kc, vc_, _, rc, lac, hc = proj_fwd(ctx, vec1c + start, w_int, wa, ba, tt_ctx)
    qc0 = jnp.zeros_like(kc)
    _, _, sallf_c, sallb_c, sfin_c = gla_fwd(qc0, kc, vc_, lac, jnp.zeros((2, HV, DK), F32), cb_ctx)
    u, q, k, v, g, r, la, h = proj_fwd(x, vec1, w_int, wa, ba, tt)
    o_f, o_b, sall_f, sall_b, _ = gla_fwd(q, k, v, la, sfin_c, cb)
    w_out, wg_t, wu_t, w_down = late_weights(o_b)
    x1, cat, mix, yc, h2 = merge_fwd(u, g, o_f, o_b, x, vecm, vcm, convw, w_out, tt)
    tt2 = min(2 * tt, x.shape[0])
    act, dact, hid = ffn_gate_up(h2, wg_t, wu_t, tt2)
    dx2, dff, acc3 = ffn_down_loss(hid, x1, tgt, vec3, w_down, tt)
    dgt, dup = ffn_dhid(dff, act, dact, w_down, tt2)
    dx1, acc2 = ffn_dh2(dgt, dup, x1, dx2, vec2, wg_t, wu_t, tt)
    bt = min(GRAD_TOKEN_BLOCK, x.shape[0])
    gw = {"w_down": tn_matmul(hid, dff, FN, bt), "wg_t": tn_matmul(dgt, h2, FN, bt),
          "wu_t": tn_matmul(dup, h2, FN, bt)}
    vecm = vecm + grads_ready(("wg_t", "wu_t", "w_down"), gw)
    du, dg, do, dmix, accm1, accm2, dconvw = merge_bwd(dx1, mix, u, g, o_f, o_b, yc, vecm, vcm, convw, w_out, tt)
    gw["w_out"] = tn_matmul(cat, dmix, 512, bt)
    dsfin = jnp.zeros((2, HV, DK), F32) + grads_ready(("w_out",), gw)
    dqf, dkf, dvf, dlaf, dqb, dkb, dvb, dlab, ds0 = gla_bwd(q, k, v, la, do, sall_f, sall_b, dsfin, cb)
    gx, dp, acc1, dba, dwa = proj_bwd(du, (dqf, dqb, dkf, dkb, dvf, dvb), dg, dlaf, dlab, la, r, x, dx1, vec1, w_int,
                                      wa, tt)
    tcx = ctx.shape[0]
    zc = lambda w, dt=MXU_DTYPE: jnp.zeros((tcx, w), dt)
    _, dkf, dvf, dlaf, _, dkb, dvb, dlab, _ = gla_bwd(qc0, kc, vc_, lac, zc(DV), sallf_c, sallb_c, ds0, cb_ctx)
    _, dpc, acc1c, dbac, dwac = proj_bwd(zc(1024), (zc(DK), zc(DK), dkf, dkb, dvf, dvb), zc(DV), dlaf, dlab, lac, rc,
                                         ctx, zc(D, F32), vec1c, w_int, wa, tt_ctx)
    dwa_t = dwa + dwac
    dba_t = dba + dbac
    gs = {
        "norm1_g": acc1[2] + acc1c[2], "norm2_g": acc2[2], "final_g": acc3[1], "loss": acc3[2],
        "gla_norm_g": accm2[0], "conv_b": accm2[1], "conv_ln_g": accm2[2], "conv_ln_b": accm2[3],
        "conv_w": dconvw, "b_a": dba_t[0], "w_a2": dwa_t,
    }
    dmod = _vec8([acc1[0], acc1[1], accm1[0], acc2[0], acc2[1], acc3[0]], D)
    dmod_c = _vec8([acc1c[0], acc1c[1]], D)
    dpc = dpc + small_ready(gs, dmod, dmod_c).astype(dpc.dtype)
    btc = min(GRAD_TOKEN_BLOCK, tcx)
    gw["w_int"] = tn_matmul(dp, h, 896, bt, init=tn_matmul(dpc, hc, 896, btc)[0])
    grads_ready(("w_int",), gw)
    return gx, gw


PACK_ROWS = 96
ROW_N1, ROW_N2, ROW_FG, ROW_LOSS, ROW_GN, ROW_CB, ROW_LG, ROW_LB, ROW_BA = 0, 1, 2, 3, 4, 5, 6, 7, 8
ROW_DMOD, ROW_DMODC, ROW_CW, ROW_WA = 16, 24, 32, 64


def _pack_small(gs, dmod, dmod_c):
    pad = lambda a: jnp.pad(a, ((0, 0), (0, D - a.shape[1])))
    singles = _vec8([gs["norm1_g"], gs["norm2_g"], gs["final_g"], gs["loss"], gs["gla_norm_g"], gs["conv_b"],
                     gs["conv_ln_g"], gs["conv_ln_b"]], D)
    return jnp.concatenate([singles, _vec8([gs["b_a"]], D), dmod, dmod_c, pad(gs["conv_w"]), pad(gs["w_a2"][0:32])],
                           axis=0)


def kernel(x, c, ctx, c_ctx, w_mod, b_mod, norm1_g, norm2_g, w_in, conv_w, conv_b, conv_ln_g, conv_ln_b, w_a2_f, b_a_f, w_a2_b, b_a_b, gla_norm_g, w_out, w_gate, w_up, w_down, final_g, loss_target, m_c_ctx, m_w_mod, m_b_mod, m_norm1_g, m_norm2_g, m_w_in, m_conv_w, m_conv_b, m_conv_ln_g, m_conv_ln_b, m_w_a2_f, m_b_a_f, m_w_a2_b, m_b_a_b, m_gla_norm_g, m_w_out, m_w_gate, m_w_up, m_w_down, m_final_g, v_c_ctx, v_w_mod, v_b_mod, v_norm1_g, v_norm2_g, v_w_in, v_conv_w, v_conv_b, v_conv_ln_g, v_conv_ln_b, v_w_a2_f, v_b_a_f, v_w_a2_b, v_b_a_b, v_gla_norm_g, v_w_out, v_w_gate, v_w_up, v_w_down, v_final_g):
    me = 4 * lax.axis_index("x") + 2 * lax.axis_index("y") + lax.axis_index("c")
    t = x.shape[1]
    tcx = ctx.shape[1]
    r_in, r_out, r_ff = w_in.shape[2], w_out.shape[1], w_gate.shape[2]
    r_in_b = -(-r_in // 16) * 16

    tb = lambda w: w.T.astype(MXU_DTYPE)

    small = dict(norm1_g=norm1_g[0], norm2_g=norm2_g[0], final_g=final_g, gla_norm_g=gla_norm_g[0],
                 conv_b=conv_b[0], conv_ln_g=conv_ln_g[0], conv_ln_b=conv_ln_b[0], b_a_f=b_a_f[0], b_a_b=b_a_b[0])
    sm_pack = jnp.zeros((48, DC), F32)
    sm_pack = lax.dynamic_update_slice(sm_pack, conv_w[0], (0, me * (DC // N_DEV)))
    sm_pack = lax.dynamic_update_slice(sm_pack, w_a2_f[0], (32, me * (DK // N_DEV)))
    sm_pack = lax.dynamic_update_slice(sm_pack, w_a2_b[0], (32, DK + me * (DK // N_DEV)))

    mod_all, s_all, sm_tot, wall = mod_forward(c, c_ctx.reshape(1, D), w_mod[0], b_mod, sm_pack,
                                               jnp.pad(tb(w_in[0]), ((0, r_in_b - r_in), (0, 0))))
    mod = lax.dynamic_slice(mod_all, (me, 0), (1, 6 * D)).reshape(6 * D)
    mod_c = mod_all[8]
    small["conv_w"] = sm_tot[0:CW, :]
    small["w_a2_f"] = sm_tot[32:32 + RANK, 0:DK]
    small["w_a2_b"] = sm_tot[32:32 + RANK, DK:2 * DK]

    w_int =jnp.pad(wall[:, 0:r_in, :].reshape(N_DEV * r_in, D), ((0, DINP - DIN), (0, 0)))
    after_w_in = (wall[0:1, 0:1, 0] * 0).astype(MXU_DTYPE)
    late = [w_out[0].astype(MXU_DTYPE) + after_w_in, tb(w_gate[0]) + after_w_in, tb(w_up[0]) + after_w_in,
            w_down[0].astype(MXU_DTYPE) + after_w_in]
    g_send, g_recv, late_thru, late_lands, g_token = gather_start(late, me, "late")

    def late_weights(after):
        got = gather_wait(g_send, g_recv, late_thru, late_lands, after, "late")
        return tuple(a.reshape(N_DEV * a.shape[1], D) for a in got)

    pad_in = lambda g: jnp.pad(g[0:DIN].reshape(N_DEV, r_in, D), ((0, 0), (0, r_in_b - r_in), (0, 0)))
    blocked = {"w_int": pad_in, "w_out": lambda g: g.reshape(N_DEV, r_out, D)}
    as_blocks = lambda n, g: blocked.get(n, lambda a: a.reshape(N_DEV, r_ff, D))(g)
    pending = []

    def grads_ready(names, gw_now):
        blocks = [as_blocks(n, gw_now[n][1]) for n in names]
        if names[0] == "w_int":
            done = finish_small(gw_now["w_int"][0]).astype(WIRE_DTYPE)
            blocks = [b + done for b in blocks]
        send, recv_s, thru, zones, token = scatter_start(blocks, names[0])
        pending.append((names, send, recv_s, thru, zones))
        return token[0:1, 0:1]

    sm = {}

    def small_ready(gs, dmod, dmod_c):
        sm["copy"] = gather_start([_pack_small(gs, dmod, dmod_c)], me, "small")
        return sm["copy"][4][0:1, 0:1]

    def finish_small(after):
        send, recv_s, thru, zones, _ = sm["copy"]
        gat = gather_wait(send, recv_s, thru, zones, after, "small")[0]
        sm["tot"], sm["loss"] = sum_blocks(gat, ROW_LOSS)
        sm["dm"] = jnp.concatenate(
            [gat[:, ROW_DMOD:ROW_DMOD + 6, :].reshape(N_DEV, 6 * D),
             jnp.pad(sm["tot"][ROW_DMODC:ROW_DMODC + 6, :].reshape(1, 6 * D), ((0, 7), (0, 0)))], axis=0)
        ncol = w_mod.shape[2]
        dm_sh = lax.dynamic_slice(sm["dm"], (0, me * ncol), (16, ncol))
        sm["mod"] = mod_backward(s_all, dm_sh, w_mod[0], m_w_mod[0], v_w_mod[0], c_ctx.reshape(1, D),
                                 m_c_ctx.reshape(1, D), v_c_ctx.reshape(1, D))
        return sm["mod"][4][0:1, 0:1] * 0

    gx, gw = local_step(x[0], ctx[0], loss_target[0], mod, mod_c, small, w_int, g_token[0:1, 0:1], late_weights,
                        grads_ready, small_ready, TOKEN_TILE, CTX_TOKEN_TILE, GLA_CHUNKS, CTX_GLA_CHUNKS)
    tot = sm["tot"]
    loss = sm["loss"].reshape(())
    g_wmod, d_wmod, nm_wmod, nv_wmod, g_cc, d_cc, nm_cc, nv_cc = sm["mod"]

    recv = {}

    def wait_for(entry, after):
        names, send, recv_s, thru, zones = entry
        recv.update(dict(zip(names, scatter_wait(send, recv_s, thru, zones, after, names[0]))))

    for entry in pending[:-1]:
        wait_for(entry, tot)
    own = {n: lax.dynamic_index_in_dim(as_blocks(n, gw[n][0]), me, 0, keepdims=False) for n in gw if n != "w_int"}
    own["w_int"] = jnp.pad(lax.dynamic_slice(gw["w_int"][0], (me * r_in, 0), (r_in, D)), ((0, r_in_b - r_in), (0, 0)))
    padt = lambda w: jnp.pad(w.T, ((0, r_in_b - r_in), (0, 0)))
    big = {}
    big["w_gate"] = [a.T for a in adamw_sharded(own["wg_t"], recv["wg_t"], w_gate[0].T, m_w_gate[0].T,
                                                 v_w_gate[0].T)]
    big["w_up"] = [a.T for a in adamw_sharded(own["wu_t"], recv["wu_t"], w_up[0].T, m_w_up[0].T, v_w_up[0].T)]
    big["w_down"] = adamw_sharded(own["w_down"], recv["w_down"], w_down[0], m_w_down[0], v_w_down[0])
    big["w_out"] = adamw_sharded(own["w_out"], recv["w_out"], w_out[0], m_w_out[0], v_w_out[0])
    wait_for(pending[-1], big["w_out"][0])
    big["w_in"] = [a[0:r_in].T for a in adamw_sharded(own["w_int"], recv["w_int"], padt(w_in[0]), padt(m_w_in[0]),
                                                       padt(v_w_in[0]))]
    big["w_mod"] = [g_wmod, d_wmod, nm_wmod, nv_wmod]

    row = lambda r, w: tot[r:r + 1, 0:w]
    gn_row = tot[ROW_GN:ROW_GN + 1, 0:DC]
    g_small = {
        "b_mod": (tot[ROW_DMOD:ROW_DMOD + 6] + tot[ROW_DMODC:ROW_DMODC + 6]).reshape(1, 6 * D),
        "norm1_g": row(ROW_N1, D), "norm2_g": row(ROW_N2, D),
        "conv_w": lax.dynamic_slice(tot, (ROW_CW, me * (DC // N_DEV)), (CW, DC // N_DEV)),
        "conv_b": row(ROW_CB, DC), "conv_ln_g": row(ROW_LG, DC), "conv_ln_b": row(ROW_LB, DC),
        "w_a2_f": lax.dynamic_slice(tot, (ROW_WA, me * (DK // N_DEV)), (RANK, DK // N_DEV)),
        "b_a_f": tot[ROW_BA:ROW_BA + 1, 0:DK],
        "w_a2_b": lax.dynamic_slice(tot, (ROW_WA + RANK, DK + me * (DK // N_DEV)), (RANK, DK // N_DEV)),
        "b_a_b": tot[ROW_BA:ROW_BA + 1, DK:2 * DK],
        "gla_norm_g": gn_row[:, 0:HV] + gn_row[:, HV:2 * HV] + gn_row[:, 2 * HV:3 * HV] + gn_row[:, 3 * HV:4 * HV],
        "final_g": row(ROW_FG, D),
    }
    wmv = {
        "b_mod": (b_mod, m_b_mod, v_b_mod), "norm1_g": (norm1_g, m_norm1_g, v_norm1_g),
        "norm2_g": (norm2_g, m_norm2_g, v_norm2_g), "conv_w": (conv_w[0], m_conv_w[0], v_conv_w[0]),
        "conv_b": (conv_b, m_conv_b, v_conv_b), "conv_ln_g": (conv_ln_g, m_conv_ln_g, v_conv_ln_g),
        "conv_ln_b": (conv_ln_b, m_conv_ln_b, v_conv_ln_b), "w_a2_f": (w_a2_f[0], m_w_a2_f[0], v_w_a2_f[0]),
        "b_a_f": (b_a_f, m_b_a_f, v_b_a_f), "w_a2_b": (w_a2_b[0], m_w_a2_b[0], v_w_a2_b[0]),
        "b_a_b": (b_a_b, m_b_a_b, v_b_a_b), "gla_norm_g": (gla_norm_g, m_gla_norm_g, v_gla_norm_g),
        "final_g": (final_g.reshape(1, D), m_final_g.reshape(1, D), v_final_g.reshape(1, D)),
    }
    names_small = list(g_small)
    upd = adamw_small([(g_small[n],) + wmv[n] for n in names_small])
    res = {n: (g_small[n],) + upd[i] for i, n in enumerate(names_small)}
    res["c_ctx"] = (g_cc, d_cc, nm_cc, nv_cc)
    for n in ("w_mod", "w_in", "w_out", "w_gate", "w_up", "w_down"):
        res[n] = tuple(big[n])

    order = ["c_ctx", "w_mod", "b_mod", "norm1_g", "norm2_g", "w_in", "conv_w", "conv_b", "conv_ln_g", "conv_ln_b",
             "w_a2_f", "b_a_f", "w_a2_b", "b_a_b", "gla_norm_g", "w_out", "w_gate", "w_up", "w_down", "final_g"]
    shapes = {"c_ctx": c_ctx.shape, "w_mod": w_mod.shape, "b_mod": b_mod.shape, "norm1_g": norm1_g.shape,
              "norm2_g": norm2_g.shape, "w_in": w_in.shape, "conv_w": conv_w.shape, "conv_b": conv_b.shape,
              "conv_ln_g": conv_ln_g.shape, "conv_ln_b": conv_ln_b.shape, "w_a2_f": w_a2_f.shape,
              "b_a_f": b_a_f.shape, "w_a2_b": w_a2_b.shape, "b_a_b": b_a_b.shape, "gla_norm_g": gla_norm_g.shape,
              "w_out": w_out.shape, "w_gate": w_gate.shape, "w_up": w_up.shape, "w_down": w_down.shape,
              "final_g": final_g.shape}
    outs = [loss, gx.reshape(x.shape)]
    for i in range(4):
        outs += [res[n][i].reshape(shapes[n]) for n in order]
    return tuple(outs)
```

```python
import functools

import jax
import jax.numpy as jnp
from jax import lax
from jax.experimental import pallas as pl
from jax.experimental.pallas import tpu as pltpu

F32 = jnp.float32
MXU_DTYPE = jnp.bfloat16
WIRE_DTYPE = jnp.bfloat16
HI = lax.Precision.HIGHEST
MESH = pl.DeviceIdType.MESH

N_DEV = 8
D = 1024
DC = 512
NH = 4
HK = 64
HV = 128
DK = NH * HK
DV = NH * HV
RANK = 16
CHUNK = 64
SEG = 64
CW = 31
CPAD = 15
DFF = 2816
DIN = 2592
DINP = 2688
TAU = 16.0
EPS = 1e-6
VMEM_LIMIT = 56 * 1024 * 1024

TOKEN_TILE = 512
CTX_TOKEN_TILE = 256
GLA_CHUNKS = 8
CTX_GLA_CHUNKS = 4
GRAD_TOKEN_BLOCK = 2048

ADAM_LR = 0.001
ADAM_B1 = 0.9
ADAM_B2 = 0.999
ADAM_EPS = 1e-08
ADAM_WD = 0.01
ADAM_STEP = 10


def _mm(a, b):
    return jnp.dot(a.astype(MXU_DTYPE), b.astype(MXU_DTYPE), preferred_element_type=F32)


def _mm_nt(a, b):
    return lax.dot_general(a.astype(MXU_DTYPE), b.astype(MXU_DTYPE), (((1,), (1,)), ((), ())),
                           preferred_element_type=F32)


def _mm_tn(a, b):
    return lax.dot_general(a.astype(MXU_DTYPE), b.astype(MXU_DTYPE), (((0,), (0,)), ((), ())),
                           preferred_element_type=F32)


def _hi(a, b):
    return jnp.dot(a, b, precision=HI, preferred_element_type=F32)


def _hi_nt(a, b):
    return lax.dot_general(a, b, (((1,), (1,)), ((), ())), precision=HI, preferred_element_type=F32)


def _hi_tn(a, b):
    return lax.dot_general(a, b, (((0,), (0,)), ((), ())), precision=HI, preferred_element_type=F32)


def _sigmoid(x):
    return 1.0 / (1.0 + jnp.exp(-x))


def _cparams(n_axes):
    return pltpu.CompilerParams(dimension_semantics=("arbitrary",) * n_axes, vmem_limit_bytes=VMEM_LIMIT)


def _full(shape):
    n = len(shape)
    return pl.BlockSpec(shape, lambda *_: (0,) * n)


def _rows(tt, width):
    return pl.BlockSpec((tt, width), lambda i: (i, 0))


def _sds(shape, dtype=F32):
    return jax.ShapeDtypeStruct(shape, dtype)


def _norm_mod(x, g, sh, sc):
    r = lax.rsqrt(jnp.mean(x * x, axis=-1, keepdims=True) + EPS)
    xn = x * r
    yy = xn * g
    return r, xn, yy, yy * (1.0 + sc) + sh


def _norm_mod_bwd(dh, r, xn, yy, g, sc):
    dsh = jnp.sum(dh, axis=0, keepdims=True)
    dsc = jnp.sum(dh * yy, axis=0, keepdims=True)
    dy = dh * (1.0 + sc)
    dg = jnp.sum(dy * xn, axis=0, keepdims=True)
    dxn = dy * g
    dx = r * (dxn - xn * jnp.mean(dxn * xn, axis=-1, keepdims=True))
    return dsh, dsc, dg, dx


def _zero_first(*refs):
    @pl.when(pl.program_id(0) == 0)
    def _():
        for r in refs:
            r[...] = jnp.zeros_like(r)


def _acc_rows(ref, rows):
    ref[...] += jnp.concatenate(rows + [jnp.zeros((8 - len(rows), rows[0].shape[1]), F32)], axis=0)


def proj_fwd(x, vec, w_int, wa, ba, tt):
    t = x.shape[0]

    def body(x_ref, vec_ref, w_ref, wa_ref, ba_ref, u_ref, q_ref, k_ref, v_ref, g_ref, r_ref, la_ref, h_ref):
        _, _, _, h = _norm_mod(x_ref[...], vec_ref[0:1, :], vec_ref[1:2, :], vec_ref[2:3, :])
        hb = h.astype(MXU_DTYPE)
        h_ref[...] = hb
        p = _mm_nt(hb, w_ref[...])
        u_ref[...] = p[:, 0:1024]
        q_ref[...] = p[:, 1024:1280]
        k_ref[...] = p[:, 1280:1536]
        v_ref[...] = p[:, 1536:2048]
        g_ref[...] = p[:, 2048:2560]
        rr = p[:, 2560:2688]
        r_ref[...] = rr
        z = _mm(rr, wa_ref[...]) + ba_ref[...]
        la_ref[...] = (jnp.minimum(z, 0.0) - jnp.log(1.0 + jnp.exp(-jnp.abs(z)))) * (1.0 / TAU)

    return pl.pallas_call(
        body, name="proj_fwd", grid=(t // tt,),
        in_specs=[_rows(tt, D), _full((8, D)), _full((DINP, D)), _full((128, 512)), _full((1, 512))],
        out_specs=[_rows(tt, 1024), _rows(tt, DK), _rows(tt, DK), _rows(tt, DV), _rows(tt, DV), _rows(tt, 128),
                   _rows(tt, 512), _rows(tt, D)],
        out_shape=[_sds((t, 1024)), _sds((t, DK)), _sds((t, DK)), _sds((t, DV)), _sds((t, DV)), _sds((t, 128)),
                   _sds((t, 512)), _sds((t, D), MXU_DTYPE)],
        compiler_params=_cparams(1),
    )(x, vec, w_int, wa, ba)


def proj_bwd(du, dqkv, dg, dla_f, dla_b, la, r, x, dx1, vec, w_int, wa, tt):
    t = x.shape[0]

    def body(du_ref, dqf_ref, dqb_ref, dkf_ref, dkb_ref, dvf_ref, dvb_ref, dg_ref, dlaf_ref, dlab_ref, la_ref, r_ref,
             x_ref, dx1_ref, vec_ref, w_ref, wa_ref, gx_ref, dp_ref, acc_ref, dba_ref, dwa_ref):
        _zero_first(acc_ref, dba_ref, dwa_ref)
        md = lambda a: a.astype(MXU_DTYPE)
        g, sc = vec_ref[0:1, :], vec_ref[2:3, :]
        sums, dba, dwa = None, None, None
        for rows in (slice(0, tt // 2), slice(tt // 2, tt)):
            both = lambda a_ref, b_ref: md(a_ref[rows, :].astype(F32) + b_ref[rows, :].astype(F32))
            dla = jnp.concatenate([dlaf_ref[rows, :], dlab_ref[rows, :]], axis=1)
            dz = dla * (1.0 - jnp.exp(TAU * la_ref[rows, :])) * (1.0 / TAU)
            rr = r_ref[rows, :]
            dba_h = jnp.sum(dz, axis=0, keepdims=True)
            dwa_h = _mm_tn(rr, dz)
            dr = _mm_nt(dz, wa_ref[...])
            dp = jnp.concatenate([du_ref[rows, :], both(dqf_ref, dqb_ref), both(dkf_ref, dkb_ref),
                                  both(dvf_ref, dvb_ref), dg_ref[rows, :], md(dr)], axis=1)
            dp_ref[rows, :] = dp
            dh = _mm(dp, w_ref[...])
            rn, xn, yy, _ = _norm_mod(x_ref[rows, :], g, vec_ref[1:2, :], sc)
            dsh, dsc, dgn, dx = _norm_mod_bwd(dh, rn, xn, yy, g, sc)
            gx_ref[rows, :] = dx1_ref[rows, :] + dx
            part = [dsh, dsc, dgn]
            sums = part if sums is None else [a + b for a, b in zip(sums, part)]
            dba = dba_h if dba is None else dba + dba_h
            dwa = dwa_h if dwa is None else dwa + dwa_h
        _acc_rows(dba_ref, [dba])
        dwa_ref[...] += dwa
        _acc_rows(acc_ref, sums)

    return pl.pallas_call(
        body, name="proj_bwd", grid=(t // tt,),
        in_specs=[_rows(tt, 1024), _rows(tt, DK), _rows(tt, DK), _rows(tt, DK), _rows(tt, DK), _rows(tt, DV),
                  _rows(tt, DV), _rows(tt, DV), _rows(tt, DK), _rows(tt, DK), _rows(tt, 512),
                  _rows(tt, 128), _rows(tt, D), _rows(tt, D), _full((8, D)), _full((DINP, D)), _full((128, 512))],
        out_specs=[_rows(tt, D), _rows(tt, DINP), _full((8, D)), _full((8, 512)), _full((128, 512))],
        out_shape=[_sds((t, D)), _sds((t, DINP), MXU_DTYPE), _sds((8, D)), _sds((8, 512)), _sds((128, 512))],
        compiler_params=_cparams(1),
    )(du, *dqkv, dg, dla_f, dla_b, la, r, x, dx1, vec, w_int, wa)


def _dot_exact01(m01, x):
    bf = jnp.bfloat16
    w = x.shape[1]
    hi = x.astype(bf)
    r1 = x - hi.astype(F32)
    mid = r1.astype(bf)
    lo = (r1 - mid.astype(F32)).astype(bf)
    y = jnp.dot(m01.astype(bf), jnp.concatenate([hi, mid, lo], axis=1), preferred_element_type=F32)
    return y[:, 0:w] + y[:, w:2 * w] + y[:, 2 * w:3 * w]


def _gla_chunk(d, qc, kc, la_c):
    row = lax.broadcasted_iota(jnp.int32, (CHUNK, CHUNK), 0)
    col = lax.broadcasted_iota(jnp.int32, (CHUNK, CHUNK), 1)
    cum = ((col <= row) if d == 0 else (col >= row)).astype(F32)
    cum_t = ((col >= row) if d == 0 else (col <= row)).astype(F32)
    cum4 = jnp.concatenate([cum] * NH, axis=0)
    head_of_lane = lax.broadcasted_iota(jnp.int32, (1, DK), 1) // HK
    b = _dot_exact01(cum, la_c)
    bl = jnp.sum(la_c, axis=0, keepdims=True)
    eb = jnp.exp(b)
    enb = jnp.exp(-b)
    ekd = jnp.exp(bl - b)
    qt = qc * (HK ** -0.5) * eb
    kt = kc * enb
    kd = kc * ekd
    qst = jnp.concatenate([jnp.where(head_of_lane == h, qt, 0.0) for h in range(NH)], axis=0)
    a = _mm_nt(qst, kt) * cum4
    return cum_t, cum4, head_of_lane, eb, enb, ekd, qt, kt, kd, qst, a, jnp.exp(bl)


NPAIR = NH // 2


def _pair_rows(x, p):
    return x[2 * p * CHUNK:(2 * p + 2) * CHUNK]


def _pair_lanes(x, p):
    return x[:, 2 * p * HK:(2 * p + 2) * HK]


def _pair_fold(r):
    half = lax.broadcasted_iota(jnp.int32, (1, 2 * HK), 1) // HK
    return jnp.where(half == 0, r[0:CHUNK], 0.0) + jnp.where(half == 1, r[CHUNK:2 * CHUNK], 0.0)


def gla_fwd(q, k, v, la, s0, cb):
    t = q.shape[0]
    nc = t // CHUNK
    nb = nc // cb

    def body(qf_ref, kf_ref, vf_ref, laf_ref, qb_ref, kb_ref, vb_ref, lab_ref, s0_ref,
             of_ref, ob_ref, sf_ref, sb_ref, sfin_ref, s_scr):
        i = pl.program_id(0)

        @pl.when(i == 0)
        def _():
            s_scr[...] = s0_ref[...]

        def chunk(d, jj, q_ref, k_ref, v_ref, la_ref, o_ref, sall_ref):
            rows = slice(jj * CHUNK, (jj + 1) * CHUNK)
            vc = v_ref[rows, :]
            _, _, head_of_lane, _, _, _, _, _, kd, qst, a, dec = _gla_chunk(
                d, q_ref[rows, :], k_ref[rows, :], la_ref[rows, :])
            s = s_scr[d]
            sall_ref[jj] = s
            inter = _mm_nt(qst, s)
            outs = []
            for h in range(NH):
                hs = slice(h * CHUNK, (h + 1) * CHUNK)
                outs.append(_mm(a[hs], vc[:, h * HV:(h + 1) * HV]) + inter[hs])
            o_ref[rows, :] = jnp.concatenate(outs, axis=1)
            kv = _mm_tn(vc, kd)
            s_new = dec * s
            for h in range(NH):
                s_new = s_new + jnp.where(head_of_lane == h, kv[h * HV:(h + 1) * HV], 0.0)
            s_scr[d] = s_new

        for j in range(cb):
            chunk(0, j, qf_ref, kf_ref, vf_ref, laf_ref, of_ref, sf_ref)
            chunk(1, cb - 1 - j, qb_ref, kb_ref, vb_ref, lab_ref, ob_ref, sb_ref)

        @pl.when(i == nb - 1)
        def _():
            sfin_ref[...] = s_scr[...]

    tb = cb * CHUNK
    fwd = lambda w, c=0: pl.BlockSpec((tb, w), lambda i: (i, c))
    bwd = lambda w, c=0: pl.BlockSpec((tb, w), lambda i: (nb - 1 - i, c))
    return pl.pallas_call(
        body, name="gla_fwd", grid=(nb,),
        in_specs=[fwd(DK), fwd(DK), fwd(DV), fwd(DK, 0), bwd(DK), bwd(DK), bwd(DV), bwd(DK, 1), _full((2, HV, DK))],
        out_specs=[fwd(DV), bwd(DV), pl.BlockSpec((cb, HV, DK), lambda i: (i, 0, 0)),
                   pl.BlockSpec((cb, HV, DK), lambda i: (nb - 1 - i, 0, 0)), _full((2, HV, DK))],
        out_shape=[_sds((t, DV)), _sds((t, DV)), _sds((nc, HV, DK)), _sds((nc, HV, DK)), _sds((2, HV, DK))],
        scratch_shapes=[pltpu.VMEM((2, HV, DK), F32)],
        compiler_params=_cparams(1),
    )(q, k, v, la, q, k, v, la, s0)


def gla_bwd(q, k, v, la, do, sall_f, sall_b, dsfin, cb):
    t = q.shape[0]
    nc = t // CHUNK
    nb = nc // cb

    def body(qf_ref, kf_ref, vf_ref, laf_ref, dof_ref, sf_ref, qb_ref, kb_ref, vb_ref, lab_ref, dob_ref, sb_ref,
             dsfin_ref, dqf_ref, dkf_ref, dvf_ref, dlaf_ref, dqb_ref, dkb_ref, dvb_ref, dlab_ref, ds0_ref, ds_scr):
        i = pl.program_id(0)

        @pl.when(i == 0)
        def _():
            ds_scr[...] = dsfin_ref[...]

        def chunk(d, jj, q_ref, k_ref, v_ref, la_ref, do_ref, sall_ref, dq_ref, dk_ref, dv_ref, dla_ref):
            rows = slice(jj * CHUNK, (jj + 1) * CHUNK)
            vc = v_ref[rows, :]
            doc = do_ref[rows, :]
            cum_t, cum4, head_of_lane, eb, enb, ekd, qt, kt, kd, qst, a, dec = _gla_chunk(
                d, q_ref[rows, :], k_ref[rows, :], la_ref[rows, :])
            s = sall_ref[jj]
            ds = ds_scr[d]
            hv = lambda x, h: x[:, h * HV:(h + 1) * HV]
            hr = lambda x, h: x[h * CHUNK:(h + 1) * CHUNK]
            fold = lambda x: functools.reduce(
                lambda p, c: p + c, [jnp.where(head_of_lane == h, hr(x, h), 0.0) for h in range(NH)])
            dost = jnp.concatenate([hv(doc, h) for h in range(NH)], axis=0)
            vst = jnp.concatenate([hv(vc, h) for h in range(NH)], axis=0)
            da = jnp.concatenate([_mm_nt(hv(doc, h), hv(vc, h)) for h in range(NH)], axis=0) * cum4
            pairs = range(NPAIR)
            dqt = fold(_mm(da, kt)) + jnp.concatenate(
                [_pair_fold(_mm(_pair_rows(dost, p), _pair_lanes(s, p))) for p in pairs], axis=1)
            dkt = _mm_tn(da, qst)
            kdst = jnp.concatenate([jnp.where(head_of_lane == h, kd, 0.0) for h in range(NH)], axis=0)
            dv_inter = jnp.concatenate(
                [_mm_nt(_pair_lanes(_pair_rows(kdst, p), p), _pair_lanes(ds, p)) for p in pairs], axis=0)
            dv_ref[rows, :] = jnp.concatenate(
                [_mm_tn(hr(a, h), hv(doc, h)) + hr(dv_inter, h) for h in range(NH)], axis=1).astype(MXU_DTYPE)
            dkd = jnp.concatenate([_pair_fold(_mm(_pair_rows(vst, p), _pair_lanes(ds, p))) for p in pairs], axis=1)
            ds_scr[d] = dec * ds + jnp.concatenate(
                [_mm_tn(_pair_rows(dost, p), _pair_lanes(_pair_rows(qst, p), p)) for p in pairs], axis=1)
            tkd = dkd * kd
            db = dqt * qt - dkt * kt - tkd
            dbl = jnp.sum(ds * s, axis=0, keepdims=True) * dec + jnp.sum(tkd, axis=0, keepdims=True)
            dla_ref[rows, :] = _dot_exact01(cum_t, db) + dbl
            dq_ref[rows, :] = (dqt * eb * (HK ** -0.5)).astype(MXU_DTYPE)
            dk_ref[rows, :] = (dkt * enb + dkd * ekd).astype(MXU_DTYPE)

        for j in range(cb):
            chunk(0, cb - 1 - j, qf_ref, kf_ref, vf_ref, laf_ref, dof_ref, sf_ref, dqf_ref, dkf_ref, dvf_ref, dlaf_ref)
            chunk(1, j, qb_ref, kb_ref, vb_ref, lab_ref, dob_ref, sb_ref, dqb_ref, dkb_ref, dvb_ref, dlab_ref)

        @pl.when(i == nb - 1)
        def _():
            ds0_ref[...] = ds_scr[...]

    tb = cb * CHUNK
    rev = lambda w, c=0: pl.BlockSpec((tb, w), lambda i: (nb - 1 - i, c))
    fro = lambda w, c=0: pl.BlockSpec((tb, w), lambda i: (i, c))
    st_rev = pl.BlockSpec((cb, HV, DK), lambda i: (nb - 1 - i, 0, 0))
    st_fro = pl.BlockSpec((cb, HV, DK), lambda i: (i, 0, 0))
    md = MXU_DTYPE
    return pl.pallas_call(
        body, name="gla_bwd", grid=(nb,),
        in_specs=[rev(DK), rev(DK), rev(DV), rev(DK, 0), rev(DV), st_rev,
                  fro(DK), fro(DK), fro(DV), fro(DK, 1), fro(DV), st_fro, _full((2, HV, DK))],
        out_specs=[rev(DK), rev(DK), rev(DV), rev(DK), fro(DK), fro(DK), fro(DV), fro(DK), _full((2, HV, DK))],
        out_shape=[_sds((t, DK), md), _sds((t, DK), md), _sds((t, DV), md), _sds((t, DK)),
                   _sds((t, DK), md), _sds((t, DK), md), _sds((t, DV), md), _sds((t, DK)), _sds((2, HV, DK))],
        scratch_shapes=[pltpu.VMEM((2, HV, DK), F32)],
        compiler_params=_cparams(1),
    )(q, k, v, la, do, sall_f, q, k, v, la, do, sall_b, dsfin)


def _seg_pos(tt):
    return lax.broadcasted_iota(jnp.int32, (tt, 1), 0) % SEG


def _shifted(x, s, pos, tt):
    y = x if s == 0 else pltpu.roll(x, (-s) % tt, 0)
    return jnp.where((pos + s >= 0) & (pos + s < SEG), y, 0.0)


def _head_norm(o, gn):
    rs, xs = [], []
    for h in range(NH):
        oh = o[:, h * HV:(h + 1) * HV]
        r = lax.rsqrt(jnp.mean(oh * oh, axis=-1, keepdims=True) + EPS)
        rs.append(r)
        xs.append(oh * r)
    return rs, xs


def merge_fwd(u, g, o_f, o_b, x, vec, vc, convw, w_out, tt):
    t = x.shape[0]

    def body(u_ref, g_ref, of_ref, ob_ref, x_ref, vec_ref, vc_ref, cw_ref, w_ref, x1_ref, cat_ref, mix_ref, yc_ref,
             h2_ref):
        a = u_ref[:, 0:DC]
        gate = u_ref[:, DC:2 * DC]
        vv = a * _sigmoid(gate)
        pos = _seg_pos(tt)
        cw = cw_ref[...]
        yc = jnp.zeros((tt, DC), F32) + vc_ref[1:2, :]
        for j in range(CW):
            yc = yc + _shifted(vv, j - CPAD, pos, tt) * cw[j:j + 1, :]
        yc_ref[...] = yc
        mu = jnp.mean(yc, axis=-1, keepdims=True)
        yd = yc - mu
        rs = lax.rsqrt(jnp.mean(yd * yd, axis=-1, keepdims=True) + EPS)
        ln = yd * rs * vc_ref[2:3, :] + vc_ref[3:4, :]
        conv_o = ln * _sigmoid(ln)
        o = of_ref[...] + ob_ref[...]
        _, xs = _head_norm(o, None)
        gg = g_ref[...]
        o2g = jnp.concatenate(xs, axis=1) * vc_ref[0:1, :] * (gg * _sigmoid(gg))
        cat = jnp.concatenate([conv_o, o2g], axis=1).astype(MXU_DTYPE)
        cat_ref[...] = cat
        mix = _mm(cat, w_ref[...])
        mix_ref[...] = mix
        x1 = x_ref[...] + vec_ref[0:1, :] * mix
        x1_ref[...] = x1
        _, _, _, h2 = _norm_mod(x1, vec_ref[1:2, :], vec_ref[2:3, :], vec_ref[3:4, :])
        h2_ref[...] = h2.astype(MXU_DTYPE)

    return pl.pallas_call(
        body, name="merge_fwd", grid=(t // tt,),
        in_specs=[_rows(tt, 1024), _rows(tt, DV), _rows(tt, DV), _rows(tt, DV), _rows(tt, D),
                  _full((8, D)), _full((8, DC)), _full((32, DC)), _full((D, D))],
        out_specs=[_rows(tt, D), _rows(tt, D), _rows(tt, D), _rows(tt, DC), _rows(tt, D)],
        out_shape=[_sds((t, D)), _sds((t, D), MXU_DTYPE), _sds((t, D)), _sds((t, DC)), _sds((t, D), MXU_DTYPE)],
        compiler_params=_cparams(1),
    )(u, g, o_f, o_b, x, vec, vc, convw, w_out)


def merge_bwd(dx1, mix, u, g, o_f, o_b, yc, vec, vc, convw, w_out, tt):
    t = dx1.shape[0]

    def body(dx1_ref, mix_ref, u_ref, g_ref, of_ref, ob_ref, yc_ref, vec_ref, vc_ref, cw_ref, w_ref,
             du_ref, dg_ref, do_ref, dmix_ref, acc1_ref, acc2_ref, dcw_ref):
        _zero_first(acc1_ref, acc2_ref, dcw_ref)
        dx1v = dx1_ref[...]
        dg1 = jnp.sum(dx1v * mix_ref[...], axis=0, keepdims=True)
        dmix = (vec_ref[0:1, :] * dx1v).astype(MXU_DTYPE)
        dmix_ref[...] = dmix
        dcat = _mm_nt(dmix, w_ref[...])
        dconv_o = dcat[:, 0:DC]
        do2 = dcat[:, DC:2 * DC]
        gn = vc_ref[0:1, :]
        o = of_ref[...] + ob_ref[...]
        rs, xs = _head_norm(o, None)
        xn = jnp.concatenate(xs, axis=1)
        gg = g_ref[...]
        sg = _sigmoid(gg)
        don = do2 * (gg * sg)
        dg_ref[...] = (do2 * (xn * gn) * (sg * (1.0 + gg * (1.0 - sg)))).astype(MXU_DTYPE)
        dgn = jnp.sum(don * xn, axis=0, keepdims=True)
        dxn = don * gn
        dos = []
        for h in range(NH):
            dh = dxn[:, h * HV:(h + 1) * HV]
            dos.append(rs[h] * (dh - xs[h] * jnp.mean(dh * xs[h], axis=-1, keepdims=True)))
        do_ref[...] = jnp.concatenate(dos, axis=1).astype(MXU_DTYPE)
        yc = yc_ref[...]
        mu = jnp.mean(yc, axis=-1, keepdims=True)
        yd = yc - mu
        rstd = lax.rsqrt(jnp.mean(yd * yd, axis=-1, keepdims=True) + EPS)
        yhat = yd * rstd
        lg = vc_ref[2:3, :]
        ln = yhat * lg + vc_ref[3:4, :]
        sl = _sigmoid(ln)
        dln = dconv_o * (sl * (1.0 + ln * (1.0 - sl)))
        dlb = jnp.sum(dln, axis=0, keepdims=True)
        dlg = jnp.sum(dln * yhat, axis=0, keepdims=True)
        dyh = dln * lg
        dyc = rstd * (dyh - jnp.mean(dyh, axis=-1, keepdims=True)
                      - yhat * jnp.mean(dyh * yhat, axis=-1, keepdims=True))
        dcb = jnp.sum(dyc, axis=0, keepdims=True)
        a = u_ref[:, 0:DC]
        gate = u_ref[:, DC:2 * DC]
        sgt = _sigmoid(gate)
        vv = a * sgt
        pos = _seg_pos(tt)
        cw = cw_ref[...]
        dvv = jnp.zeros((tt, DC), F32)
        dws = []
        for j in range(CW):
            shifted_dyc = _shifted(dyc, CPAD - j, pos, tt)
            dvv = dvv + shifted_dyc * cw[j:j + 1, :]
            dws.append(jnp.sum(shifted_dyc * vv, axis=0, keepdims=True))
        dws.append(jnp.zeros((1, DC), F32))
        du_ref[:, 0:DC] = (dvv * sgt).astype(MXU_DTYPE)
        du_ref[:, DC:2 * DC] = (dvv * a * sgt * (1.0 - sgt)).astype(MXU_DTYPE)
        _acc_rows(acc1_ref, [dg1])
        _acc_rows(acc2_ref, [dgn, dcb, dlg, dlb])
        dcw_ref[...] += jnp.concatenate(dws, axis=0)

    return pl.pallas_call(
        body, name="merge_bwd", grid=(t // tt,),
        in_specs=[_rows(tt, D), _rows(tt, D), _rows(tt, 1024), _rows(tt, DV), _rows(tt, DV), _rows(tt, DV),
                  _rows(tt, DC),
                  _full((8, D)), _full((8, DC)), _full((32, DC)), _full((D, D))],
        out_specs=[_rows(tt, 1024), _rows(tt, DV), _rows(tt, DV), _rows(tt, D), _full((8, D)), _full((8, DC)),
                   _full((32, DC))],
        out_shape=[_sds((t, 1024), MXU_DTYPE), _sds((t, DV), MXU_DTYPE), _sds((t, DV), MXU_DTYPE),
                   _sds((t, D), MXU_DTYPE), _sds((8, D)),
                   _sds((8, DC)), _sds((32, DC))],
        compiler_params=_cparams(1),
    )(dx1, mix, u, g, o_f, o_b, yc, vec, vc, convw, w_out)


FN = DFF // 2


def ffn_gate_up(h2, wg_t, wu_t, tt):
    t = h2.shape[0]

    def body(h2_ref, wg_ref, wu_ref, s_ref, d_ref, hid_ref):
        h2v = h2_ref[...]
        gt = _mm_nt(h2v, wg_ref[...])
        up = _mm_nt(h2v, wu_ref[...])
        sg = _sigmoid(gt)
        act = gt * sg
        s_ref[...] = act.astype(MXU_DTYPE)
        d_ref[...] = (up * (sg * (1.0 + gt * (1.0 - sg)))).astype(MXU_DTYPE)
        hid_ref[...] = (act * up).astype(MXU_DTYPE)

    blk = pl.BlockSpec((tt, FN), lambda j, i: (i, j))
    wblk = pl.BlockSpec((FN, D), lambda j, i: (j, 0))
    return pl.pallas_call(
        body, name="ffn_gate_up", grid=(2, t // tt),
        in_specs=[pl.BlockSpec((tt, D), lambda j, i: (i, 0)), wblk, wblk],
        out_specs=[blk, blk, blk],
        out_shape=[_sds((t, DFF), MXU_DTYPE)] * 3,
        compiler_params=_cparams(2),
    )(h2, wg_t, wu_t)


def ffn_down_loss(hid, x1, tgt, vec, w_down, tt):
    t = x1.shape[0]

    def body(hid_ref, x1_ref, tgt_ref, vec_ref, w_ref, dx2_ref, dff_ref, acc_ref):
        _zero_first(acc_ref)
        g2 = vec_ref[0:1, :]
        fg = vec_ref[1:2, :]
        ff = _mm(hid_ref[...], w_ref[...])
        x2 = x1_ref[...] + g2 * ff
        rf = lax.rsqrt(jnp.mean(x2 * x2, axis=-1, keepdims=True) + EPS)
        xn = x2 * rf
        err = xn * fg - tgt_ref[...]
        dy = err * (1.0 / D)
        dfg = jnp.sum(dy * xn, axis=0, keepdims=True)
        dxn = dy * fg
        dx2 = rf * (dxn - xn * jnp.mean(dxn * xn, axis=-1, keepdims=True))
        dx2_ref[...] = dx2
        dff_ref[...] = (g2 * dx2).astype(MXU_DTYPE)
        dg2 = jnp.sum(dx2 * ff, axis=0, keepdims=True)
        loss = jnp.sum(err * err, axis=0, keepdims=True) * (0.5 / D)
        _acc_rows(acc_ref, [dg2, dfg, loss])

    return pl.pallas_call(
        body, name="ffn_down_loss", grid=(t // tt,),
        in_specs=[_rows(tt, DFF), _rows(tt, D), _rows(tt, D), _full((8, D)), _full((DFF, D))],
        out_specs=[_rows(tt, D), _rows(tt, D), _full((8, D))],
        out_shape=[_sds((t, D)), _sds((t, D), MXU_DTYPE), _sds((8, D))],
        compiler_params=_cparams(1),
    )(hid, x1, tgt, vec, w_down)


def ffn_dhid(dff, s, d, w_down, tt):
    t = dff.shape[0]

    def body(dff_ref, s_ref, d_ref, w_ref, dgt_ref, dup_ref):
        dhid = _mm_nt(dff_ref[...], w_ref[...])
        dgt_ref[...] = (dhid * d_ref[...].astype(F32)).astype(MXU_DTYPE)
        dup_ref[...] = (dhid * s_ref[...].astype(F32)).astype(MXU_DTYPE)

    blk = pl.BlockSpec((tt, FN), lambda j, i: (i, j))
    return pl.pallas_call(
        body, name="ffn_dhid", grid=(2, t // tt),
        in_specs=[pl.BlockSpec((tt, D), lambda j, i: (i, 0)), blk, blk, pl.BlockSpec((FN, D), lambda j, i: (j, 0))],
        out_specs=[blk, blk],
        out_shape=[_sds((t, DFF), MXU_DTYPE), _sds((t, DFF), MXU_DTYPE)],
        compiler_params=_cparams(2),
    )(dff, s, d, w_down)


def ffn_dh2(dgt, dup, x1, dx2, vec, wg_t, wu_t, tt):
    t = x1.shape[0]

    def body(dgt_ref, dup_ref, x1_ref, dx2_ref, vec_ref, wg_ref, wu_ref, dx1_ref, acc_ref):
        _zero_first(acc_ref)
        dh2 = _mm(dgt_ref[...], wg_ref[...]) + _mm(dup_ref[...], wu_ref[...])
        g, sc = vec_ref[0:1, :], vec_ref[2:3, :]
        r, xn, yy, _ = _norm_mod(x1_ref[...], g, vec_ref[1:2, :], sc)
        dsh, dsc, dgn, dx = _norm_mod_bwd(dh2, r, xn, yy, g, sc)
        dx1_ref[...] = dx2_ref[...] + dx
        _acc_rows(acc_ref, [dsh, dsc, dgn])

    return pl.pallas_call(
        body, name="ffn_dh2", grid=(t // tt,),
        in_specs=[_rows(tt, DFF), _rows(tt, DFF), _rows(tt, D), _rows(tt, D), _full((8, D)), _full((DFF, D)),
                  _full((DFF, D))],
        out_specs=[_rows(tt, D), _full((8, D))],
        out_shape=[_sds((t, D)), _sds((8, D))],
        compiler_params=_cparams(1),
    )(dgt, dup, x1, dx2, vec, wg_t, wu_t)


def tn_matmul(a, b, bm, bt, init=None):
    t, m = a.shape
    n = b.shape[1]
    nk = t // bt

    def body(*refs):
        if init is None:
            a_ref, b_ref, o_ref, wire_ref = refs
        else:
            a_ref, b_ref, i_ref, o_ref, wire_ref = refs
        @pl.when(pl.program_id(1) == 0)
        def _():
            o_ref[...] = jnp.zeros_like(o_ref) if init is None else i_ref[...]

        o_ref[...] += _mm_tn(a_ref[...], b_ref[...])

        @pl.when(pl.program_id(1) == nk - 1)
        def _():
            wire_ref[...] = o_ref[...].astype(WIRE_DTYPE)

    in_specs = [pl.BlockSpec((bt, bm), lambda i, k: (k, i)), pl.BlockSpec((bt, n), lambda i, k: (k, 0))]
    args = [a, b]
    if init is not None:
        in_specs.append(pl.BlockSpec((bm, n), lambda i, k: (i, 0)))
        args.append(init)
    oblk = pl.BlockSpec((bm, n), lambda i, k: (i, 0))
    return pl.pallas_call(
        body, name="tn_matmul", grid=(m // bm, nk),
        in_specs=in_specs, out_specs=[oblk, oblk],
        out_shape=[_sds((m, n)), _sds((m, n), WIRE_DTYPE)], compiler_params=_cparams(2),
    )(*args)


def _adamw(w, g, m, v):
    m = ADAM_B1 * m + (1.0 - ADAM_B1) * g
    v = ADAM_B2 * v + (1.0 - ADAM_B2) * (g * g)
    m_hat = m / (1.0 - ADAM_B1 ** ADAM_STEP)
    v_hat = v / (1.0 - ADAM_B2 ** ADAM_STEP)
    delta = -ADAM_LR * (m_hat / (jnp.sqrt(v_hat) + ADAM_EPS) + ADAM_WD * w)
    return delta, m, v


def adamw_sharded(own, recv, w, m, v):
    shape = w.shape

    def body(own_ref, recv_ref, w_ref, m_ref, v_ref, g_ref, d_ref, mo_ref, vo_ref):
        g = own_ref[...]
        for k in range(N_DEV - 1):
            g = g + recv_ref[k].astype(F32)
        g_ref[...] = g
        d_ref[...], mo_ref[...], vo_ref[...] = _adamw(w_ref[...], g, m_ref[...], v_ref[...])

    return pl.pallas_call(
        body, name="adamw_sharded",
        in_specs=[_full(shape), _full((N_DEV - 1,) + shape), _full(shape), _full(shape), _full(shape)],
        out_specs=[_full(shape)] * 4, out_shape=[_sds(shape)] * 4, grid=(1,),
        compiler_params=_cparams(1),
    )(own, recv, w, m, v)


def adamw_small(items):
    n = len(items)
    flat = [a for it in items for a in it]

    def body(*refs):
        ins, outs = refs[:4 * n], refs[4 * n:]
        for i in range(n):
            g, w, m, v = (r[...] for r in ins[4 * i:4 * i + 4])
            outs[3 * i][...], outs[3 * i + 1][...], outs[3 * i + 2][...] = _adamw(w, g, m, v)

    out = pl.pallas_call(
        body, name="adamw_small", grid=(1,),
        in_specs=[_full(a.shape) for a in flat],
        out_specs=[_full(it[1].shape) for it in items for _ in range(3)],
        out_shape=[_sds(it[1].shape) for it in items for _ in range(3)],
        compiler_params=_cparams(1),
    )(*flat)
    return [tuple(out[3 * i:3 * i + 3]) for i in range(n)]


def _mesh_pos():
    x, y, c = lax.axis_index("x"), lax.axis_index("y"), lax.axis_index("c")
    me = 4 * x + 2 * y + c
    peers = []
    for k in range(1, N_DEV):
        peers.append(((1 - x) if (k >> 2) & 1 else x, (1 - y) if (k >> 1) & 1 else y, (1 - c) if k & 1 else c))
    return me, peers


def _all_gather_issue(buf, send_sems, recv_sems, me, peers):
    sends = []
    for k, peer in enumerate(peers):
        cp = pltpu.make_async_remote_copy(src_ref=buf.at[me], dst_ref=buf.at[me], send_sem=send_sems.at[k],
                                          recv_sem=recv_sems.at[k], device_id=peer, device_id_type=MESH)
        cp.start()
        sends.append(cp)
    return sends


def _all_gather_finish(buf, send_sems, recv_sems, me, peers, sends):
    for k, peer in enumerate(peers):
        src = jnp.bitwise_xor(me, k + 1)
        pltpu.make_async_remote_copy(src_ref=buf.at[src], dst_ref=buf.at[src], send_sem=send_sems.at[k],
                                     recv_sem=recv_sems.at[k], device_id=peer, device_id_type=MESH).wait_recv()
    for cp in sends:
        cp.wait_send()


def _all_gather(buf, send_sems, recv_sems, me, peers):
    _all_gather_finish(buf, send_sems, recv_sems, me, peers, _all_gather_issue(buf, send_sems, recv_sems, me, peers))


_VMEM = pl.BlockSpec(memory_space=pltpu.VMEM)
_ANY = pl.BlockSpec(memory_space=pl.ANY)
_SEMS = pltpu.SemaphoreType.DMA((N_DEV - 1,))


def mod_forward(c, c_ctx, w_mod_sh, b_mod, sm_pack, w_first):
    ncol = w_mod_sh.shape[1]

    def body(c_ref, cc_ref, w_ref, b_ref, sm_ref, wf_ref, mod_ref, s_ref, smt_ref, wall_ref, cbuf, pbuf, smbuf,
             s1, r1, s2, r2, s3, r3, ws, wr, wl):
        me, peers = _mesh_pos()
        x, y, cc = lax.axis_index("x"), lax.axis_index("y"), lax.axis_index("c")
        sibling = (x, y, 1 - cc)
        chips = [(1 - x, y), (x, 1 - y), (1 - x, 1 - y)]
        slot = lambda px, py, pc: wall_ref.at[4 * px + 2 * py + pc]

        def wcopy(k, block, to, src=None):
            return pltpu.make_async_remote_copy(
                src_ref=slot(*block) if src is None else src, dst_ref=slot(*block), send_sem=ws.at[k],
                recv_sem=wr.at[k], device_id=to, device_id_type=MESH)

        mine = pltpu.make_async_copy(wf_ref, slot(x, y, cc), wl)
        mine.start()
        first = [wcopy(0, (x, y, cc), sibling, src=wf_ref)]
        first += [wcopy(1 + j, (x, y, cc), (*chip, cc), src=wf_ref) for j, chip in enumerate(chips)]
        for cp in first:
            cp.start()
        smbuf[me] = sm_ref[...]
        sm_sends = _all_gather_issue(smbuf, s3, r3, me, peers)
        cbuf[me] = jnp.broadcast_to(c_ref[...], (8, D))
        _all_gather(cbuf, s1, r1, me, peers)
        rows = [cbuf[j, 0:1, :] for j in range(N_DEV)] + [cc_ref[...], jnp.zeros((7, D), F32)]
        sx = jnp.concatenate(rows, axis=0)
        s = sx * _sigmoid(sx)
        s_ref[...] = s
        pbuf[me] = _hi(s, w_ref[...])
        _all_gather(pbuf, s2, r2, me, peers)
        for j in range(N_DEV):
            mod_ref[:, j * ncol:(j + 1) * ncol] = pbuf[j] + b_ref[:, j * ncol:(j + 1) * ncol]
        _all_gather_finish(smbuf, s3, r3, me, peers, sm_sends)
        tot = smbuf[0]
        for j in range(1, N_DEV):
            tot = tot + smbuf[j]
        smt_ref[...] = tot
        passed = [wcopy(4 + j, (*chip, cc), sibling) for j, chip in enumerate(chips)]
        for j, chip in enumerate(chips):
            wcopy(1 + j, (*chip, cc), (x, y, cc)).wait_recv()
            passed[j].start()
        wcopy(0, (x, y, 1 - cc), (x, y, cc)).wait_recv()
        for j, chip in enumerate(chips):
            wcopy(4 + j, (*chip, 1 - cc), (x, y, cc)).wait_recv()
        for cp in first + passed:
            cp.wait_send()
        mine.wait()

    return pl.pallas_call(
        body, name="mod_forward",
        in_specs=[_VMEM] * 6, out_specs=[_VMEM] * 4,
        out_shape=[_sds((16, N_DEV * ncol)), _sds((16, D)), _sds(sm_pack.shape),
                   _sds((N_DEV,) + w_first.shape, w_first.dtype)],
        scratch_shapes=[pltpu.VMEM((N_DEV, 8, D), F32), pltpu.VMEM((N_DEV, 16, ncol), F32),
                        pltpu.VMEM((N_DEV,) + sm_pack.shape, F32), _SEMS, _SEMS, _SEMS, _SEMS, _SEMS, _SEMS,
                        _SEMS, _SEMS, pltpu.SemaphoreType.DMA],
        compiler_params=pltpu.CompilerParams(vmem_limit_bytes=VMEM_LIMIT),
    )(c, c_ctx, w_mod_sh, b_mod, sm_pack, w_first)


def sum_blocks(gat, loss_row):
    def body(g_ref, tot_ref, loss_ref):
        tot = g_ref[0]
        for j in range(1, N_DEV):
            tot = tot + g_ref[j]
        tot_ref[...] = tot
        loss_ref[...] = jnp.sum(tot[loss_row:loss_row + 1, :], axis=1, keepdims=True)

    return pl.pallas_call(
        body, name="sum_blocks", in_specs=[_VMEM], out_specs=[_VMEM, _VMEM],
        out_shape=[_sds(gat.shape[1:]), _sds((1, 1))],
        compiler_params=pltpu.CompilerParams(vmem_limit_bytes=VMEM_LIMIT),
    )(gat)


def mod_backward(s, dm_sh, w, m, v, cc, m_cc, v_cc):
    shape = w.shape

    def body(s_ref, dm_ref, w_ref, m_ref, v_ref, cc_ref, mcc_ref, vcc_ref,
             gw_ref, dw_ref, mw_ref, vw_ref, gc_ref, dc_ref, mc_ref, vc_ref, pbuf, send_sems, recv_sems):
        me, peers = _mesh_pos()
        wv = w_ref[...]
        pbuf[me] = _hi_nt(dm_ref[8:16, :], wv)
        _all_gather(pbuf, send_sems, recv_sems, me, peers)
        g = _hi_tn(s_ref[...], dm_ref[...])
        gw_ref[...] = g
        dw_ref[...], mw_ref[...], vw_ref[...] = _adamw(wv, g, m_ref[...], v_ref[...])
        tot = pbuf[0]
        for j in range(1, N_DEV):
            tot = tot + pbuf[j]
        ccv = cc_ref[...]
        sg = _sigmoid(ccv)
        gc = tot[0:1, :] * (sg * (1.0 + ccv * (1.0 - sg)))
        gc_ref[...] = gc
        dc_ref[...], mc_ref[...], vc_ref[...] = _adamw(ccv, gc, mcc_ref[...], vcc_ref[...])

    return pl.pallas_call(
        body, name="mod_backward", in_specs=[_VMEM] * 8, out_specs=[_VMEM] * 8,
        out_shape=[_sds(shape)] * 4 + [_sds((1, D))] * 4,
        scratch_shapes=[pltpu.VMEM((N_DEV, 8, D), F32), _SEMS, _SEMS],
        compiler_params=pltpu.CompilerParams(vmem_limit_bytes=VMEM_LIMIT),
    )(s, dm_sh, w, m, v, cc, m_cc, v_cc)


_HBM = pl.BlockSpec(memory_space=pltpu.HBM)
_SEM = pl.BlockSpec(memory_space=pltpu.SEMAPHORE)
_EFFECT = pltpu.SideEffectType.DATAFLOW_SIDE_EFFECTING
_hbm = lambda a: pltpu.with_memory_space_constraint(a, pltpu.HBM)


def gather_start(shards, me, tag):
    n = len(shards)
    sems = pltpu.SemaphoreType.DMA((7 * n,))
    lands = [lax.dynamic_update_slice(lax.empty((N_DEV,) + s.shape, s.dtype), s[None], (me, 0, 0)) for s in shards]

    def body(*refs):
        s_refs, l_refs = refs[:n], refs[n:2 * n]
        send_sems, recv_sems = refs[2 * n], refs[2 * n + 1]
        token = refs[-1]
        my, peers = _mesh_pos()
        for w in range(n):
            for k, peer in enumerate(peers):
                pltpu.make_async_remote_copy(
                    src_ref=s_refs[w], dst_ref=l_refs[w].at[my], send_sem=send_sems.at[w * 7 + k],
                    recv_sem=recv_sems.at[w * 7 + k], device_id=peer, device_id_type=MESH).start()
        token[...] = jnp.zeros_like(token)

    out = pl.pallas_call(
        body, name="gather_start_" + tag,
        out_shape=(sems, sems) + tuple(pltpu.HBM(a.shape, a.dtype) for a in list(shards) + lands) + (_sds((8, 128)),),
        in_specs=(_HBM,) * (2 * n), out_specs=(_SEM, _SEM) + (_HBM,) * (2 * n) + (_VMEM,),
        input_output_aliases={i: i + 2 for i in range(2 * n)},
        compiler_params=pltpu.CompilerParams(has_side_effects=_EFFECT),
    )(*[_hbm(a) for a in list(shards) + lands])
    return out[0], out[1], list(out[2:2 + n]), list(out[2 + n:2 + 2 * n]), out[-1]


def gather_wait(send_sems, recv_sems, shards, lands, after, tag):
    n = len(shards)

    def body(*refs):
        s_refs, l_refs = refs[:n], refs[n:2 * n]
        send_sems, recv_sems = refs[2 * n], refs[2 * n + 1]
        my, peers = _mesh_pos()
        for w in range(n):
            for k, peer in enumerate(peers):
                src = jnp.bitwise_xor(my, k + 1)
                cp = pltpu.make_async_remote_copy(
                    src_ref=s_refs[w], dst_ref=l_refs[w].at[src], send_sem=send_sems.at[w * 7 + k],
                    recv_sem=recv_sems.at[w * 7 + k], device_id=peer, device_id_type=MESH)
                cp.wait_send()
                cp.wait_recv()

    out = pl.pallas_call(
        body, name="gather_wait_" + tag,
        out_shape=tuple(pltpu.HBM(a.shape, a.dtype) for a in list(shards) + list(lands)),
        in_specs=(_HBM,) * (2 * n) + (_SEM, _SEM, _ANY), out_specs=(_HBM,) * (2 * n),
        input_output_aliases={i: i for i in range(2 * n)},
        compiler_params=pltpu.CompilerParams(has_side_effects=_EFFECT),
    )(*shards, *lands, send_sems, recv_sems, after)
    return list(out[n:2 * n])


def scatter_start(grads, tag):
    n = len(grads)
    sems = pltpu.SemaphoreType.DMA((7 * n,))
    lands = [lax.empty((N_DEV - 1,) + g.shape[1:], g.dtype) for g in grads]

    def body(*refs):
        g_refs, l_refs = refs[:n], refs[n:2 * n]
        send_sems, recv_sems = refs[2 * n], refs[2 * n + 1]
        token = refs[-1]
        me, peers = _mesh_pos()
        for w in range(n):
            for k, peer in enumerate(peers):
                dst = jnp.bitwise_xor(me, k + 1)
                pltpu.make_async_remote_copy(
                    src_ref=g_refs[w].at[dst], dst_ref=l_refs[w].at[k], send_sem=send_sems.at[w * 7 + k],
                    recv_sem=recv_sems.at[w * 7 + k], device_id=peer, device_id_type=MESH).start()
        token[...] = jnp.zeros_like(token)

    out = pl.pallas_call(
        body, name="scatter_start_" + tag,
        out_shape=(sems, sems) + tuple(pltpu.HBM(a.shape, a.dtype) for a in list(grads) + lands) + (_sds((8, 128)),),
        in_specs=(_HBM,) * (2 * n), out_specs=(_SEM, _SEM) + (_HBM,) * (2 * n) + (_VMEM,),
        input_output_aliases={i: i + 2 for i in range(2 * n)},
        compiler_params=pltpu.CompilerParams(has_side_effects=_EFFECT),
    )(*[_hbm(a) for a in list(grads) + lands])
    return out[0], out[1], list(out[2:2 + n]), list(out[2 + n:2 + 2 * n]), out[-1]


def scatter_wait(send_sems, recv_sems, grads, lands, after, tag):
    n = len(grads)

    def body(*refs):
        g_refs, l_refs = refs[:n], refs[n:2 * n]
        send_sems, recv_sems = refs[2 * n], refs[2 * n + 1]
        me, peers = _mesh_pos()
        for w in range(n):
            for k, peer in enumerate(peers):
                dst = jnp.bitwise_xor(me, k + 1)
                cp = pltpu.make_async_remote_copy(
                    src_ref=g_refs[w].at[dst], dst_ref=l_refs[w].at[k], send_sem=send_sems.at[w * 7 + k],
                    recv_sem=recv_sems.at[w * 7 + k], device_id=peer, device_id_type=MESH)
                cp.wait_send()
                cp.wait_recv()

    out = pl.pallas_call(
        body, name="scatter_wait_" + tag,
        out_shape=tuple(pltpu.HBM(a.shape, a.dtype) for a in list(grads) + list(lands)),
        in_specs=(_HBM,) * (2 * n) + (_SEM, _SEM, _ANY), out_specs=(_HBM,) * (2 * n),
        input_output_aliases={i: i for i in range(2 * n)},
        compiler_params=pltpu.CompilerParams(has_side_effects=_EFFECT),
    )(*grads, *lands, send_sems, recv_sems, after)
    return list(out[n:2 * n])


def _vec8(rows, width):
    rid = lax.broadcasted_iota(jnp.int32, (8, width), 0)
    out = jnp.zeros((8, width), F32)
    for i, r in enumerate(rows):
        r = r.reshape(-1)
        r = jnp.pad(r, (0, width - r.shape[0]))
        out = jnp.where(rid == i, r[None, :], out)
    return out


def local_step(x, ctx, tgt, mod, mod_c, small, w_int, start, late_weights, grads_ready, small_ready, tt, tt_ctx, cb,
               cb_ctx):
    sh1, sc1, g1, sh2, sc2, g2 = [mod[i * D:(i + 1) * D] for i in range(6)]
    csh1, csc1 = mod_c[0:D], mod_c[D:2 * D]
    vec1 = _vec8([small["norm1_g"], sh1, sc1], D)
    vec1c = _vec8([small["norm1_g"], csh1, csc1], D)
    vec2 = _vec8([small["norm2_g"], sh2, sc2], D)
    vec3 = _vec8([g2, small["final_g"]], D)
    vecm = _vec8([g1, small["norm2_g"], sh2, sc2], D)
    vcm = _vec8([jnp.tile(small["gla_norm_g"].reshape(HV), NH), small["conv_b"], small["conv_ln_g"],
                 small["conv_ln_b"]], DC)
    convw = jnp.pad(small["conv_w"], ((0, 1), (0, 0)))
    wa = jnp.zeros((128, 512), F32)
    wa = wa.at[0:RANK, 0:DK].set(small["w_a2_f"]).at[RANK:2 * RANK, DK:2 * DK].set(small["w_a2_b"])
    ba = jnp.concatenate([small["b_a_f"].reshape(1, DK), small["b_a_b"].reshape(1, DK)], axis=1)

    _, _, kc, vc_, _, rc, lac, hc = proj_fwd(ctx, vec1c + start, w_int, wa, ba, tt_ctx)
    qc0 = jnp.zeros_like(kc)
    _, _, sallf_c, sallb_c, sfin_c = gla_fwd(qc0, kc, vc_, lac, jnp.zeros((2, HV, DK), F32), cb_ctx)
    u, q, k, v, g, r, la, h = proj_fwd(x, vec1, w_int, wa, ba, tt)
    o_f, o_b, sall_f, sall_b, _ = gla_fwd(q, k, v, la, sfin_c, cb)
    w_out, wg_t, wu_t, w_down = late_weights(o_b)
    x1, cat, mix, yc, h2 = merge_fwd(u, g, o_f, o_b, x, vecm, vcm, convw, w_out, tt)
    tt2 = min(2 * tt, x.shape[0])
    act, dact, hid = ffn_gate_up(h2, wg_t, wu_t, tt2)
    dx2, dff, acc3 = ffn_down_loss(hid, x1, tgt, vec3, w_down, tt)
    dgt, dup = ffn_dhid(dff, act, dact, w_down, tt2)
    dx1, acc2 = ffn_dh2(dgt, dup, x1, dx2, vec2, wg_t, wu_t, tt)
    bt = min(GRAD_TOKEN_BLOCK, x.shape[0])
    gw = {"w_down": tn_matmul(hid, dff, FN, bt), "wg_t": tn_matmul(dgt, h2, FN, bt),
          "wu_t": tn_matmul(dup, h2, FN, bt)}
    vecm = vecm + grads_ready(("wg_t", "wu_t", "w_down"), gw)
    du, dg, do, dmix, accm1, accm2, dconvw = merge_bwd(dx1, mix, u, g, o_f, o_b, yc, vecm, vcm, convw, w_out, tt)
    gw["w_out"] = tn_matmul(cat, dmix, D, bt)
    dsfin = jnp.zeros((2, HV, DK), F32) + grads_ready(("w_out",), gw)
    dqf, dkf, dvf, dlaf, dqb, dkb, dvb, dlab, ds0 = gla_bwd(q, k, v, la, do, sall_f, sall_b, dsfin, cb)
    gx, dp, acc1, dba, dwa = proj_bwd(du, (dqf, dqb, dkf, dkb, dvf, dvb), dg, dlaf, dlab, la, r, x, dx1, vec1, w_int,
                                      wa, tt)
    tcx = ctx.shape[0]
    zc = lambda w, dt=MXU_DTYPE: jnp.zeros((tcx, w), dt)
    _, dkf, dvf, dlaf, _, dkb, dvb, dlab, _ = gla_bwd(qc0, kc, vc_, lac, zc(DV), sallf_c, sallb_c, ds0, cb_ctx)
    _, dpc, acc1c, dbac, dwac = proj_bwd(zc(1024), (zc(DK), zc(DK), dkf, dkb, dvf, dvb), zc(DV), dlaf, dlab, lac, rc,
                                         ctx, zc(D, F32), vec1c, w_int, wa, tt_ctx)
    dwa_t = dwa + dwac
    dba_t = dba + dbac
    gs = {
        "norm1_g": acc1[2] + acc1c[2], "norm2_g": acc2[2], "final_g": acc3[1], "loss": acc3[2],
        "gla_norm_g": accm2[0], "conv_b": accm2[1], "conv_ln_g": accm2[2], "conv_ln_b": accm2[3],
        "conv_w": dconvw, "b_a": dba_t[0], "w_a2": dwa_t,
    }
    dmod = _vec8([acc1[0], acc1[1], accm1[0], acc2[0], acc2[1], acc3[0]], D)
    dmod_c = _vec8([acc1c[0], acc1c[1]], D)
    dpc = dpc + small_ready(gs, dmod, dmod_c).astype(dpc.dtype)
    btc = min(GRAD_TOKEN_BLOCK, tcx)
    gw["w_int"] = tn_matmul(dp, h, 896, bt, init=tn_matmul(dpc, hc, 896, btc)[0])
    grads_ready(("w_int",), gw)
    return gx, gw


PACK_ROWS = 96
ROW_N1, ROW_N2, ROW_FG, ROW_LOSS, ROW_GN, ROW_CB, ROW_LG, ROW_LB, ROW_BA = 0, 1, 2, 3, 4, 5, 6, 7, 8
ROW_DMOD, ROW_DMODC, ROW_CW, ROW_WA = 16, 24, 32, 64


def _pack_small(gs, dmod, dmod_c):
    pad = lambda a: jnp.pad(a, ((0, 0), (0, D - a.shape[1])))
    singles = _vec8([gs["norm1_g"], gs["norm2_g"], gs["final_g"], gs["loss"], gs["gla_norm_g"], gs["conv_b"],
                     gs["conv_ln_g"], gs["conv_ln_b"]], D)
    return jnp.concatenate([singles, _vec8([gs["b_a"]], D), dmod, dmod_c, pad(gs["conv_w"]), pad(gs["w_a2"][0:32])],
                           axis=0)


def kernel(x, c, ctx, c_ctx, w_mod, b_mod, norm1_g, norm2_g, w_in, conv_w, conv_b, conv_ln_g, conv_ln_b, w_a2_f, b_a_f, w_a2_b, b_a_b, gla_norm_g, w_out, w_gate, w_up, w_down, final_g, loss_target, m_c_ctx, m_w_mod, m_b_mod, m_norm1_g, m_norm2_g, m_w_in, m_conv_w, m_conv_b, m_conv_ln_g, m_conv_ln_b, m_w_a2_f, m_b_a_f, m_w_a2_b, m_b_a_b, m_gla_norm_g, m_w_out, m_w_gate, m_w_up, m_w_down, m_final_g, v_c_ctx, v_w_mod, v_b_mod, v_norm1_g, v_norm2_g, v_w_in, v_conv_w, v_conv_b, v_conv_ln_g, v_conv_ln_b, v_w_a2_f, v_b_a_f, v_w_a2_b, v_b_a_b, v_gla_norm_g, v_w_out, v_w_gate, v_w_up, v_w_down, v_final_g):
    me = 4 * lax.axis_index("x") + 2 * lax.axis_index("y") + lax.axis_index("c")
    t = x.shape[1]
    tcx = ctx.shape[1]
    r_in, r_out, r_ff = w_in.shape[2], w_out.shape[1], w_gate.shape[2]
    r_in_b = -(-r_in // 16) * 16

    tb = lambda w: w.T.astype(MXU_DTYPE)

    small = dict(norm1_g=norm1_g[0], norm2_g=norm2_g[0], final_g=final_g, gla_norm_g=gla_norm_g[0],
                 conv_b=conv_b[0], conv_ln_g=conv_ln_g[0], conv_ln_b=conv_ln_b[0], b_a_f=b_a_f[0], b_a_b=b_a_b[0])
    sm_pack = jnp.zeros((48, DC), F32)
    sm_pack = lax.dynamic_update_slice(sm_pack, conv_w[0], (0, me * (DC // N_DEV)))
    sm_pack = lax.dynamic_update_slice(sm_pack, w_a2_f[0], (32, me * (DK // N_DEV)))
    sm_pack = lax.dynamic_update_slice(sm_pack, w_a2_b[0], (32, DK + me * (DK // N_DEV)))

    mod_all, s_all, sm_tot, wall = mod_forward(c, c_ctx.reshape(1, D), w_mod[0], b_mod, sm_pack,
                                               jnp.pad(tb(w_in[0]), ((0, r_in_b - r_in), (0, 0))))
    mod = lax.dynamic_slice(mod_all, (me, 0), (1, 6 * D)).reshape(6 * D)
    mod_c = mod_all[8]
    small["conv_w"] = sm_tot[0:CW, :]
    small["w_a2_f"] = sm_tot[32:32 + RANK, 0:DK]
    small["w_a2_b"] = sm_tot[32:32 + RANK, DK:2 * DK]

    w_int =jnp.pad(wall[:, 0:r_in, :].reshape(N_DEV * r_in, D), ((0, DINP - DIN), (0, 0)))
    after_w_in = (wall[0:1, 0:1, 0] * 0).astype(MXU_DTYPE)
    late = [w_out[0].astype(MXU_DTYPE) + after_w_in, tb(w_gate[0]) + after_w_in, tb(w_up[0]) + after_w_in,
            w_down[0].astype(MXU_DTYPE) + after_w_in]
    g_send, g_recv, late_thru, late_lands, g_token = gather_start(late, me, "late")

    def late_weights(after):
        got = gather_wait(g_send, g_recv, late_thru, late_lands, after, "late")
        return tuple(a.reshape(N_DEV * a.shape[1], D) for a in got)

    pad_in = lambda g: jnp.pad(g[0:DIN].reshape(N_DEV, r_in, D), ((0, 0), (0, r_in_b - r_in), (0, 0)))
    blocked = {"w_int": pad_in, "w_out": lambda g: g.reshape(N_DEV, r_out, D)}
    as_blocks = lambda n, g: blocked.get(n, lambda a: a.reshape(N_DEV, r_ff, D))(g)
    pending = []

    def grads_ready(names, gw_now):
        blocks = [as_blocks(n, gw_now[n][1]) for n in names]
        if names[0] == "w_int":
            done = finish_small(gw_now["w_int"][0]).astype(WIRE_DTYPE)
            blocks = [b + done for b in blocks]
        send, recv_s, thru, zones, token = scatter_start(blocks, names[0])
        pending.append((names, send, recv_s, thru, zones))
        return token[0:1, 0:1]

    sm = {}

    def small_ready(gs, dmod, dmod_c):
        sm["copy"] = gather_start([_pack_small(gs, dmod, dmod_c)], me, "small")
        return sm["copy"][4][0:1, 0:1]

    def finish_small(after):
        send, recv_s, thru, zones, _ = sm["copy"]
        gat = gather_wait(send, recv_s, thru, zones, after, "small")[0]
        sm["tot"], sm["loss"] = sum_blocks(gat, ROW_LOSS)
        sm["dm"] = jnp.concatenate(
            [gat[:, ROW_DMOD:ROW_DMOD + 6, :].reshape(N_DEV, 6 * D),
             jnp.pad(sm["tot"][ROW_DMODC:ROW_DMODC + 6, :].reshape(1, 6 * D), ((0, 7), (0, 0)))], axis=0)
        ncol = w_mod.shape[2]
        dm_sh = lax.dynamic_slice(sm["dm"], (0, me * ncol), (16, ncol))
        sm["mod"] = mod_backward(s_all, dm_sh, w_mod[0], m_w_mod[0], v_w_mod[0], c_ctx.reshape(1, D),
                                 m_c_ctx.reshape(1, D), v_c_ctx.reshape(1, D))
        return sm["mod"][4][0:1, 0:1] * 0

    gx, gw = local_step(x[0], ctx[0], loss_target[0], mod, mod_c, small, w_int, g_token[0:1, 0:1], late_weights,
                        grads_ready, small_ready, TOKEN_TILE, CTX_TOKEN_TILE, GLA_CHUNKS, CTX_GLA_CHUNKS)
    tot = sm["tot"]
    loss = sm["loss"].reshape(())
    g_wmod, d_wmod, nm_wmod, nv_wmod, g_cc, d_cc, nm_cc, nv_cc = sm["mod"]

    recv = {}

    def wait_for(entry, after):
        names, send, recv_s, thru, zones = entry
        recv.update(dict(zip(names, scatter_wait(send, recv_s, thru, zones, after, names[0]))))

    for entry in pending[:-1]:
        wait_for(entry, tot)
    own = {n: lax.dynamic_index_in_dim(as_blocks(n, gw[n][0]), me, 0, keepdims=False) for n in gw if n != "w_int"}
    own["w_int"] = jnp.pad(lax.dynamic_slice(gw["w_int"][0], (me * r_in, 0), (r_in, D)), ((0, r_in_b - r_in), (0, 0)))
    padt = lambda w: jnp.pad(w.T, ((0, r_in_b - r_in), (0, 0)))
    big = {}
    big["w_gate"] = [a.T for a in adamw_sharded(own["wg_t"], recv["wg_t"], w_gate[0].T, m_w_gate[0].T,
                                                 v_w_gate[0].T)]
    big["w_up"] = [a.T for a in adamw_sharded(own["wu_t"], recv["wu_t"], w_up[0].T, m_w_up[0].T, v_w_up[0].T)]
    big["w_down"] = adamw_sharded(own["w_down"], recv["w_down"], w_down[0], m_w_down[0], v_w_down[0])
    big["w_out"] = adamw_sharded(own["w_out"], recv["w_out"], w_out[0], m_w_out[0], v_w_out[0])
    wait_for(pending[-1], big["w_out"][0])
    big["w_in"] = [a[0:r_in].T for a in adamw_sharded(own["w_int"], recv["w_int"], padt(w_in[0]), padt(m_w_in[0]),
                                                       padt(v_w_in[0]))]
    big["w_mod"] = [g_wmod, d_wmod, nm_wmod, nv_wmod]

    row = lambda r, w: tot[r:r + 1, 0:w]
    gn_row = tot[ROW_GN:ROW_GN + 1, 0:DC]
    g_small = {
        "b_mod": (tot[ROW_DMOD:ROW_DMOD + 6] + tot[ROW_DMODC:ROW_DMODC + 6]).reshape(1, 6 * D),
        "norm1_g": row(ROW_N1, D), "norm2_g": row(ROW_N2, D),
        "conv_w": lax.dynamic_slice(tot, (ROW_CW, me * (DC // N_DEV)), (CW, DC // N_DEV)),
        "conv_b": row(ROW_CB, DC), "conv_ln_g": row(ROW_LG, DC), "conv_ln_b": row(ROW_LB, DC),
        "w_a2_f": lax.dynamic_slice(tot, (ROW_WA, me * (DK // N_DEV)), (RANK, DK // N_DEV)),
        "b_a_f": tot[ROW_BA:ROW_BA + 1, 0:DK],
        "w_a2_b": lax.dynamic_slice(tot, (ROW_WA + RANK, DK + me * (DK // N_DEV)), (RANK, DK // N_DEV)),
        "b_a_b": tot[ROW_BA:ROW_BA + 1, DK:2 * DK],
        "gla_norm_g": gn_row[:, 0:HV] + gn_row[:, HV:2 * HV] + gn_row[:, 2 * HV:3 * HV] + gn_row[:, 3 * HV:4 * HV],
        "final_g": row(ROW_FG, D),
    }
    wmv = {
        "b_mod": (b_mod, m_b_mod, v_b_mod), "norm1_g": (norm1_g, m_norm1_g, v_norm1_g),
        "norm2_g": (norm2_g, m_norm2_g, v_norm2_g), "conv_w": (conv_w[0], m_conv_w[0], v_conv_w[0]),
        "conv_b": (conv_b, m_conv_b, v_conv_b), "conv_ln_g": (conv_ln_g, m_conv_ln_g, v_conv_ln_g),
        "conv_ln_b": (conv_ln_b, m_conv_ln_b, v_conv_ln_b), "w_a2_f": (w_a2_f[0], m_w_a2_f[0], v_w_a2_f[0]),
        "b_a_f": (b_a_f, m_b_a_f, v_b_a_f), "w_a2_b": (w_a2_b[0], m_w_a2_b[0], v_w_a2_b[0]),
        "b_a_b": (b_a_b, m_b_a_b, v_b_a_b), "gla_norm_g": (gla_norm_g, m_gla_norm_g, v_gla_norm_g),
        "final_g": (final_g.reshape(1, D), m_final_g.reshape(1, D), v_final_g.reshape(1, D)),
    }
    names_small = list(g_small)
    upd = adamw_small([(g_small[n],) + wmv[n] for n in names_small])
    res = {n: (g_small[n],) + upd[i] for i, n in enumerate(names_small)}
    res["c_ctx"] = (g_cc, d_cc, nm_cc, nv_cc)
    for n in ("w_mod", "w_in", "w_out", "w_gate", "w_up", "w_down"):
        res[n] = tuple(big[n])

    order = ["c_ctx", "w_mod", "b_mod", "norm1_g", "norm2_g", "w_in", "conv_w", "conv_b", "conv_ln_g", "conv_ln_b",
             "w_a2_f", "b_a_f", "w_a2_b", "b_a_b", "gla_norm_g", "w_out", "w_gate", "w_up", "w_down", "final_g"]
    shapes = {"c_ctx": c_ctx.shape, "w_mod": w_mod.shape, "b_mod": b_mod.shape, "norm1_g": norm1_g.shape,
              "norm2_g": norm2_g.shape, "w_in": w_in.shape, "conv_w": conv_w.shape, "conv_b": conv_b.shape,
              "conv_ln_g": conv_ln_g.shape, "conv_ln_b": conv_ln_b.shape, "w_a2_f": w_a2_f.shape,
              "b_a_f": b_a_f.shape, "w_a2_b": w_a2_b.shape, "b_a_b": b_a_b.shape, "gla_norm_g": gla_norm_g.shape,
              "w_out": w_out.shape, "w_gate": w_gate.shape, "w_up": w_up.shape, "w_down": w_down.shape,
              "final_g": final_g.shape}
    outs = [loss, gx.reshape(x.shape)]
    for i in range(4):
        outs += [res[n][i].reshape(shapes[n]) for n in order]
    return tuple(outs)
```

```python
import functools

import jax
import jax.numpy as jnp
from jax import lax
from jax.experimental import pallas as pl
from jax.experimental.pallas import tpu as pltpu

F32 = jnp.float32
MXU_DTYPE = jnp.bfloat16
WIRE_DTYPE = jnp.bfloat16
HI = lax.Precision.HIGHEST
MESH = pl.DeviceIdType.MESH

N_DEV = 8
D = 1024
DC = 512
NH = 4
HK = 64
HV = 128
DK = NH * HK
DV = NH * HV
RANK = 16
CHUNK = 64
SEG = 64
CW = 31
CPAD = 15
DFF = 2816
DIN = 2592
DINP = 2688
TAU = 16.0
EPS = 1e-6
VMEM_LIMIT = 56 * 1024 * 1024

TOKEN_TILE = 512
CTX_TOKEN_TILE = 256
GLA_CHUNKS = 8
CTX_GLA_CHUNKS = 4
GRAD_TOKEN_BLOCK = 2048

ADAM_LR = 0.001
ADAM_B1 = 0.9
ADAM_B2 = 0.999
ADAM_EPS = 1e-08
ADAM_WD = 0.01
ADAM_STEP = 10


def _mm(a, b):
    return jnp.dot(a.astype(MXU_DTYPE), b.astype(MXU_DTYPE), preferred_element_type=F32)


def _mm_nt(a, b):
    return lax.dot_general(a.astype(MXU_DTYPE), b.astype(MXU_DTYPE), (((1,), (1,)), ((), ())),
                           preferred_element_type=F32)


def _mm_tn(a, b):
    return lax.dot_general(a.astype(MXU_DTYPE), b.astype(MXU_DTYPE), (((0,), (0,)), ((), ())),
                           preferred_element_type=F32)


def _hi(a, b):
    return jnp.dot(a, b, precision=HI, preferred_element_type=F32)


def _hi_nt(a, b):
    return lax.dot_general(a, b, (((1,), (1,)), ((), ())), precision=HI, preferred_element_type=F32)


def _hi_tn(a, b):
    return lax.dot_general(a, b, (((0,), (0,)), ((), ())), precision=HI, preferred_element_type=F32)


def _sigmoid(x):
    return 1.0 / (1.0 + jnp.exp(-x))


def _cparams(n_axes):
    return pltpu.CompilerParams(dimension_semantics=("arbitrary",) * n_axes, vmem_limit_bytes=VMEM_LIMIT)


def _full(shape):
    n = len(shape)
    return pl.BlockSpec(shape, lambda *_: (0,) * n)


def _rows(tt, width):
    return pl.BlockSpec((tt, width), lambda i: (i, 0))


def _sds(shape, dtype=F32):
    return jax.ShapeDtypeStruct(shape, dtype)


def _norm_mod(x, g, sh, sc):
    r = lax.rsqrt(jnp.mean(x * x, axis=-1, keepdims=True) + EPS)
    xn = x * r
    yy = xn * g
    return r, xn, yy, yy * (1.0 + sc) + sh


def _norm_mod_bwd(dh, r, xn, yy, g, sc):
    dsh = jnp.sum(dh, axis=0, keepdims=True)
    dsc = jnp.sum(dh * yy, axis=0, keepdims=True)
    dy = dh * (1.0 + sc)
    dg = jnp.sum(dy * xn, axis=0, keepdims=True)
    dxn = dy * g
    dx = r * (dxn - xn * jnp.mean(dxn * xn, axis=-1, keepdims=True))
    return dsh, dsc, dg, dx


def _zero_first(*refs):
    @pl.when(pl.program_id(0) == 0)
    def _():
        for r in refs:
            r[...] = jnp.zeros_like(r)


def _acc_rows(ref, rows):
    ref[...] += jnp.concatenate(rows + [jnp.zeros((8 - len(rows), rows[0].shape[1]), F32)], axis=0)


def proj_fwd(x, vec, w_int, wa, ba, tt):
    t = x.shape[0]

    def body(x_ref, vec_ref, w_ref, wa_ref, ba_ref, u_ref, q_ref, k_ref, v_ref, g_ref, r_ref, la_ref, h_ref):
        _, _, _, h = _norm_mod(x_ref[...], vec_ref[0:1, :], vec_ref[1:2, :], vec_ref[2:3, :])
        hb = h.astype(MXU_DTYPE)
        h_ref[...] = hb
        p = _mm_nt(hb, w_ref[...])
        u_ref[...] = p[:, 0:1024]
        q_ref[...] = p[:, 1024:1280]
        k_ref[...] = p[:, 1280:1536]
        v_ref[...] = p[:, 1536:2048]
        g_ref[...] = p[:, 2048:2560]
        rr = p[:, 2560:2688]
        r_ref[...] = rr
        z = _mm(rr, wa_ref[...]) + ba_ref[...]
        la_ref[...] = (jnp.minimum(z, 0.0) - jnp.log(1.0 + jnp.exp(-jnp.abs(z)))) * (1.0 / TAU)

    return pl.pallas_call(
        body, name="proj_fwd", grid=(t // tt,),
        in_specs=[_rows(tt, D), _full((8, D)), _full((DINP, D)), _full((128, 512)), _full((1, 512))],
        out_specs=[_rows(tt, 1024), _rows(tt, DK), _rows(tt, DK), _rows(tt, DV), _rows(tt, DV), _rows(tt, 128),
                   _rows(tt, 512), _rows(tt, D)],
        out_shape=[_sds((t, 1024)), _sds((t, DK)), _sds((t, DK)), _sds((t, DV)), _sds((t, DV)), _sds((t, 128)),
                   _sds((t, 512)), _sds((t, D), MXU_DTYPE)],
        compiler_params=_cparams(1),
    )(x, vec, w_int, wa, ba)


def proj_bwd(du, dqkv, dg, dla_f, dla_b, la, r, x, dx1, vec, w_int, wa, tt):
    t = x.shape[0]

    def body(du_ref, dqf_ref, dqb_ref, dkf_ref, dkb_ref, dvf_ref, dvb_ref, dg_ref, dlaf_ref, dlab_ref, la_ref, r_ref,
             x_ref, dx1_ref, vec_ref, w_ref, wa_ref, gx_ref, dp_ref, acc_ref, dba_ref, dwa_ref):
        _zero_first(acc_ref, dba_ref, dwa_ref)
        md = lambda a: a.astype(MXU_DTYPE)
        g, sc = vec_ref[0:1, :], vec_ref[2:3, :]
        sums, dba, dwa = None, None, None
        for rows in (slice(0, tt // 2), slice(tt // 2, tt)):
            both = lambda a_ref, b_ref: md(a_ref[rows, :].astype(F32) + b_ref[rows, :].astype(F32))
            dla = jnp.concatenate([dlaf_ref[rows, :], dlab_ref[rows, :]], axis=1)
            dz = dla * (1.0 - jnp.exp(TAU * la_ref[rows, :])) * (1.0 / TAU)
            rr = r_ref[rows, :]
            dba_h = jnp.sum(dz, axis=0, keepdims=True)
            dwa_h = _mm_tn(rr, dz)
            dr = _mm_nt(dz, wa_ref[...])
            dp = jnp.concatenate([du_ref[rows, :], both(dqf_ref, dqb_ref), both(dkf_ref, dkb_ref),
                                  both(dvf_ref, dvb_ref), dg_ref[rows, :], md(dr)], axis=1)
            dp_ref[rows, :] = dp
            dh = _mm(dp, w_ref[...])
            rn, xn, yy, _ = _norm_mod(x_ref[rows, :], g, vec_ref[1:2, :], sc)
            dsh, dsc, dgn, dx = _norm_mod_bwd(dh, rn, xn, yy, g, sc)
            gx_ref[rows, :] = dx1_ref[rows, :] + dx
            part = [dsh, dsc, dgn]
            sums = part if sums is None else [a + b for a, b in zip(sums, part)]
            dba = dba_h if dba is None else dba + dba_h
            dwa = dwa_h if dwa is None else dwa + dwa_h
        _acc_rows(dba_ref, [dba])
        dwa_ref[...] += dwa
        _acc_rows(acc_ref, sums)

    return pl.pallas_call(
        body, name="proj_bwd", grid=(t // tt,),
        in_specs=[_rows(tt, 1024), _rows(tt, DK), _rows(tt, DK), _rows(tt, DK), _rows(tt, DK), _rows(tt, DV),
                  _rows(tt, DV), _rows(tt, DV), _rows(tt, DK), _rows(tt, DK), _rows(tt, 512),
                  _rows(tt, 128), _rows(tt, D), _rows(tt, D), _full((8, D)), _full((DINP, D)), _full((128, 512))],
        out_specs=[_rows(tt, D), _rows(tt, DINP), _full((8, D)), _full((8, 512)), _full((128, 512))],
        out_shape=[_sds((t, D)), _sds((t, DINP), MXU_DTYPE), _sds((8, D)), _sds((8, 512)), _sds((128, 512))],
        compiler_params=_cparams(1),
    )(du, *dqkv, dg, dla_f, dla_b, la, r, x, dx1, vec, w_int, wa)


def _dot_exact01(m01, x):
    bf = jnp.bfloat16
    w = x.shape[1]
    hi = x.astype(bf)
    r1 = x - hi.astype(F32)
    mid = r1.astype(bf)
    lo = (r1 - mid.astype(F32)).astype(bf)
    y = jnp.dot(m01.astype(bf), jnp.concatenate([hi, mid, lo], axis=1), preferred_element_type=F32)
    return y[:, 0:w] + y[:, w:2 * w] + y[:, 2 * w:3 * w]


def _gla_chunk(d, qc, kc, la_c):
    row = lax.broadcasted_iota(jnp.int32, (CHUNK, CHUNK), 0)
    col = lax.broadcasted_iota(jnp.int32, (CHUNK, CHUNK), 1)
    cum = ((col <= row) if d == 0 else (col >= row)).astype(F32)
    cum_t = ((col >= row) if d == 0 else (col <= row)).astype(F32)
    cum4 = jnp.concatenate([cum] * NH, axis=0)
    head_of_lane = lax.broadcasted_iota(jnp.int32, (1, DK), 1) // HK
    b = _dot_exact01(cum, la_c)
    bl = jnp.sum(la_c, axis=0, keepdims=True)
    eb = jnp.exp(b)
    enb = jnp.exp(-b)
    ekd = jnp.exp(bl - b)
    qt = qc * (HK ** -0.5) * eb
    kt = kc * enb
    kd = kc * ekd
    qst = jnp.concatenate([jnp.where(head_of_lane == h, qt, 0.0) for h in range(NH)], axis=0)
    a = _mm_nt(qst, kt) * cum4
    return cum_t, cum4, head_of_lane, eb, enb, ekd, qt, kt, kd, qst, a, jnp.exp(bl)


NPAIR = NH // 2


def _pair_rows(x, p):
    return x[2 * p * CHUNK:(2 * p + 2) * CHUNK]


def _pair_lanes(x, p):
    return x[:, 2 * p * HK:(2 * p + 2) * HK]


def _pair_fold(r):
    half = lax.broadcasted_iota(jnp.int32, (1, 2 * HK), 1) // HK
    return jnp.where(half == 0, r[0:CHUNK], 0.0) + jnp.where(half == 1, r[CHUNK:2 * CHUNK], 0.0)


def gla_fwd(q, k, v, la, s0, cb):
    t = q.shape[0]
    nc = t // CHUNK
    nb = nc // cb

    def body(qf_ref, kf_ref, vf_ref, laf_ref, qb_ref, kb_ref, vb_ref, lab_ref, s0_ref,
             of_ref, ob_ref, sf_ref, sb_ref, sfin_ref, s_scr):
        i = pl.program_id(0)

        @pl.when(i == 0)
        def _():
            s_scr[...] = s0_ref[...]

        def chunk(d, jj, q_ref, k_ref, v_ref, la_ref, o_ref, sall_ref):
            rows = slice(jj * CHUNK, (jj + 1) * CHUNK)
            vc = v_ref[rows, :]
            _, _, head_of_lane, _, _, _, _, _, kd, qst, a, dec = _gla_chunk(
                d, q_ref[rows, :], k_ref[rows, :], la_ref[rows, :])
            s = s_scr[d]
            sall_ref[jj] = s
            inter = _mm_nt(qst, s)
            outs = []
            for h in range(NH):
                hs = slice(h * CHUNK, (h + 1) * CHUNK)
                outs.append(_mm(a[hs], vc[:, h * HV:(h + 1) * HV]) + inter[hs])
            o_ref[rows, :] = jnp.concatenate(outs, axis=1)
            kv = _mm_tn(vc, kd)
            s_new = dec * s
            for h in range(NH):
                s_new = s_new + jnp.where(head_of_lane == h, kv[h * HV:(h + 1) * HV], 0.0)
            s_scr[d] = s_new

        for j in range(cb):
            chunk(0, j, qf_ref, kf_ref, vf_ref, laf_ref, of_ref, sf_ref)
            chunk(1, cb - 1 - j, qb_ref, kb_ref, vb_ref, lab_ref, ob_ref, sb_ref)

        @pl.when(i == nb - 1)
        def _():
            sfin_ref[...] = s_scr[...]

    tb = cb * CHUNK
    fwd = lambda w, c=0: pl.BlockSpec((tb, w), lambda i: (i, c))
    bwd = lambda w, c=0: pl.BlockSpec((tb, w), lambda i: (nb - 1 - i, c))
    return pl.pallas_call(
        body, name="gla_fwd", grid=(nb,),
        in_specs=[fwd(DK), fwd(DK), fwd(DV), fwd(DK, 0), bwd(DK), bwd(DK), bwd(DV), bwd(DK, 1), _full((2, HV, DK))],
        out_specs=[fwd(DV), bwd(DV), pl.BlockSpec((cb, HV, DK), lambda i: (i, 0, 0)),
                   pl.BlockSpec((cb, HV, DK), lambda i: (nb - 1 - i, 0, 0)), _full((2, HV, DK))],
        out_shape=[_sds((t, DV)), _sds((t, DV)), _sds((nc, HV, DK)), _sds((nc, HV, DK)), _sds((2, HV, DK))],
        scratch_shapes=[pltpu.VMEM((2, HV, DK), F32)],
        compiler_params=_cparams(1),
    )(q, k, v, la, q, k, v, la, s0)


def gla_bwd(q, k, v, la, do, sall_f, sall_b, dsfin, cb):
    t = q.shape[0]
    nc = t // CHUNK
    nb = nc // cb

    def body(qf_ref, kf_ref, vf_ref, laf_ref, dof_ref, sf_ref, qb_ref, kb_ref, vb_ref, lab_ref, dob_ref, sb_ref,
             dsfin_ref, dqf_ref, dkf_ref, dvf_ref, dlaf_ref, dqb_ref, dkb_ref, dvb_ref, dlab_ref, ds0_ref, ds_scr):
        i = pl.program_id(0)

        @pl.when(i == 0)
        def _():
            ds_scr[...] = dsfin_ref[...]

        def chunk(d, jj, q_ref, k_ref, v_ref, la_ref, do_ref, sall_ref, dq_ref, dk_ref, dv_ref, dla_ref):
            rows = slice(jj * CHUNK, (jj + 1) * CHUNK)
            vc = v_ref[rows, :]
            doc = do_ref[rows, :]
            cum_t, cum4, head_of_lane, eb, enb, ekd, qt, kt, kd, qst, a, dec = _gla_chunk(
                d, q_ref[rows, :], k_ref[rows, :], la_ref[rows, :])
            s = sall_ref[jj]
            ds = ds_scr[d]
            hv = lambda x, h: x[:, h * HV:(h + 1) * HV]
            hr = lambda x, h: x[h * CHUNK:(h + 1) * CHUNK]
            fold = lambda x: functools.reduce(
                lambda p, c: p + c, [jnp.where(head_of_lane == h, hr(x, h), 0.0) for h in range(NH)])
            dost = jnp.concatenate([hv(doc, h) for h in range(NH)], axis=0)
            vst = jnp.concatenate([hv(vc, h) for h in range(NH)], axis=0)
            da = jnp.concatenate([_mm_nt(hv(doc, h), hv(vc, h)) for h in range(NH)], axis=0) * cum4
            pairs = range(NPAIR)
            dqt = fold(_mm(da, kt)) + jnp.concatenate(
                [_pair_fold(_mm(_pair_rows(dost, p), _pair_lanes(s, p))) for p in pairs], axis=1)
            dkt = _mm_tn(da, qst)
            kdst = jnp.concatenate([jnp.where(head_of_lane == h, kd, 0.0) for h in range(NH)], axis=0)
            dv_inter = jnp.concatenate(
                [_mm_nt(_pair_lanes(_pair_rows(kdst, p), p), _pair_lanes(ds, p)) for p in pairs], axis=0)
            dv_ref[rows, :] = jnp.concatenate(
                [_mm_tn(hr(a, h), hv(doc, h)) + hr(dv_inter, h) for h in range(NH)], axis=1).astype(MXU_DTYPE)
            dkd = jnp.concatenate([_pair_fold(_mm(_pair_rows(vst, p), _pair_lanes(ds, p))) for p in pairs], axis=1)
            ds_scr[d] = dec * ds + jnp.concatenate(
                [_mm_tn(_pair_rows(dost, p), _pair_lanes(_pair_rows(qst, p), p)) for p in pairs], axis=1)
            tkd = dkd * kd
            db = dqt * qt - dkt * kt - tkd
            dbl = jnp.sum(ds * s, axis=0, keepdims=True) * dec + jnp.sum(tkd, axis=0, keepdims=True)
            dla_ref[rows, :] = _dot_exact01(cum_t, db) + dbl
            dq_ref[rows, :] = (dqt * eb * (HK ** -0.5)).astype(MXU_DTYPE)
            dk_ref[rows, :] = (dkt * enb + dkd * ekd).astype(MXU_DTYPE)

        for j in range(cb):
            chunk(0, cb - 1 - j, qf_ref, kf_ref, vf_ref, laf_ref, dof_ref, sf_ref, dqf_ref, dkf_ref, dvf_ref, dlaf_ref)
            chunk(1, j, qb_ref, kb_ref, vb_ref, lab_ref, dob_ref, sb_ref, dqb_ref, dkb_ref, dvb_ref, dlab_ref)

        @pl.when(i == nb - 1)
        def _():
            ds0_ref[...] = ds_scr[...]

    tb = cb * CHUNK
    rev = lambda w, c=0: pl.BlockSpec((tb, w), lambda i: (nb - 1 - i, c))
    fro = lambda w, c=0: pl.BlockSpec((tb, w), lambda i: (i, c))
    st_rev = pl.BlockSpec((cb, HV, DK), lambda i: (nb - 1 - i, 0, 0))
    st_fro = pl.BlockSpec((cb, HV, DK), lambda i: (i, 0, 0))
    md = MXU_DTYPE
    return pl.pallas_call(
        body, name="gla_bwd", grid=(nb,),
        in_specs=[rev(DK), rev(DK), rev(DV), rev(DK, 0), rev(DV), st_rev,
                  fro(DK), fro(DK), fro(DV), fro(DK, 1), fro(DV), st_fro, _full((2, HV, DK))],
        out_specs=[rev(DK), rev(DK), rev(DV), rev(DK), fro(DK), fro(DK), fro(DV), fro(DK), _full((2, HV, DK))],
        out_shape=[_sds((t, DK), md), _sds((t, DK), md), _sds((t, DV), md), _sds((t, DK)),
                   _sds((t, DK), md), _sds((t, DK), md), _sds((t, DV), md), _sds((t, DK)), _sds((2, HV, DK))],
        scratch_shapes=[pltpu.VMEM((2, HV, DK), F32)],
        compiler_params=_cparams(1),
    )(q, k, v, la, do, sall_f, q, k, v, la, do, sall_b, dsfin)


def _seg_pos(tt):
    return lax.broadcasted_iota(jnp.int32, (tt, 1), 0) % SEG


def _shifted(x, s, pos, tt):
    y = x if s == 0 else pltpu.roll(x, (-s) % tt, 0)
    return jnp.where((pos + s >= 0) & (pos + s < SEG), y, 0.0)


def _head_norm(o, gn):
    rs, xs = [], []
    for h in range(NH):
        oh = o[:, h * HV:(h + 1) * HV]
        r = lax.rsqrt(jnp.mean(oh * oh, axis=-1, keepdims=True) + EPS)
        rs.append(r)
        xs.append(oh * r)
    return rs, xs


def merge_fwd(u, g, o_f, o_b, x, vec, vc, convw, w_out, tt):
    t = x.shape[0]

    def body(u_ref, g_ref, of_ref, ob_ref, x_ref, vec_ref, vc_ref, cw_ref, w_ref, x1_ref, cat_ref, mix_ref, yc_ref,
             h2_ref):
        a = u_ref[:, 0:DC]
        gate = u_ref[:, DC:2 * DC]
        vv = a * _sigmoid(gate)
        pos = _seg_pos(tt)
        cw = cw_ref[...]
        yc = jnp.zeros((tt, DC), F32) + vc_ref[1:2, :]
        for j in range(CW):
            yc = yc + _shifted(vv, j - CPAD, pos, tt) * cw[j:j + 1, :]
        yc_ref[...] = yc
        mu = jnp.mean(yc, axis=-1, keepdims=True)
        yd = yc - mu
        rs = lax.rsqrt(jnp.mean(yd * yd, axis=-1, keepdims=True) + EPS)
        ln = yd * rs * vc_ref[2:3, :] + vc_ref[3:4, :]
        conv_o = ln * _sigmoid(ln)
        o = of_ref[...] + ob_ref[...]
        _, xs = _head_norm(o, None)
        gg = g_ref[...]
        o2g = jnp.concatenate(xs, axis=1) * vc_ref[0:1, :] * (gg * _sigmoid(gg))
        cat = jnp.concatenate([conv_o, o2g], axis=1).astype(MXU_DTYPE)
        cat_ref[...] = cat
        mix = _mm(cat, w_ref[...])
        mix_ref[...] = mix
        x1 = x_ref[...] + vec_ref[0:1, :] * mix
        x1_ref[...] = x1
        _, _, _, h2 = _norm_mod(x1, vec_ref[1:2, :], vec_ref[2:3, :], vec_ref[3:4, :])
        h2_ref[...] = h2.astype(MXU_DTYPE)

    return pl.pallas_call(
        body, name="merge_fwd", grid=(t // tt,),
        in_specs=[_rows(tt, 1024), _rows(tt, DV), _rows(tt, DV), _rows(tt, DV), _rows(tt, D),
                  _full((8, D)), _full((8, DC)), _full((32, DC)), _full((D, D))],
        out_specs=[_rows(tt, D), _rows(tt, D), _rows(tt, D), _rows(tt, DC), _rows(tt, D)],
        out_shape=[_sds((t, D)), _sds((t, D), MXU_DTYPE), _sds((t, D)), _sds((t, DC)), _sds((t, D), MXU_DTYPE)],
        compiler_params=_cparams(1),
    )(u, g, o_f, o_b, x, vec, vc, convw, w_out)


def merge_bwd(dx1, mix, u, g, o_f, o_b, yc, vec, vc, convw, w_out, tt):
    t = dx1.shape[0]

    def body(dx1_ref, mix_ref, u_ref, g_ref, of_ref, ob_ref, yc_ref, vec_ref, vc_ref, cw_ref, w_ref,
             du_ref, dg_ref, do_ref, dmix_ref, acc1_ref, acc2_ref, dcw_ref):
        _zero_first(acc1_ref, acc2_ref, dcw_ref)
        dx1v = dx1_ref[...]
        dg1 = jnp.sum(dx1v * mix_ref[...], axis=0, keepdims=True)
        dmix = (vec_ref[0:1, :] * dx1v).astype(MXU_DTYPE)
        dmix_ref[...] = dmix
        dcat = _mm_nt(dmix, w_ref[...])
        dconv_o = dcat[:, 0:DC]
        do2 = dcat[:, DC:2 * DC]
        gn = vc_ref[0:1, :]
        o = of_ref[...] + ob_ref[...]
        rs, xs = _head_norm(o, None)
        xn = jnp.concatenate(xs, axis=1)
        gg = g_ref[...]
        sg = _sigmoid(gg)
        don = do2 * (gg * sg)
        dg_ref[...] = (do2 * (xn * gn) * (sg * (1.0 + gg * (1.0 - sg)))).astype(MXU_DTYPE)
        dgn = jnp.sum(don * xn, axis=0, keepdims=True)
        dxn = don * gn
        dos = []
        for h in range(NH):
            dh = dxn[:, h * HV:(h + 1) * HV]
            dos.append(rs[h] * (dh - xs[h] * jnp.mean(dh * xs[h], axis=-1, keepdims=True)))
        do_ref[...] = jnp.concatenate(dos, axis=1).astype(MXU_DTYPE)
        yc = yc_ref[...]
        mu = jnp.mean(yc, axis=-1, keepdims=True)
        yd = yc - mu
        rstd = lax.rsqrt(jnp.mean(yd * yd, axis=-1, keepdims=True) + EPS)
        yhat = yd * rstd
        lg = vc_ref[2:3, :]
        ln = yhat * lg + vc_ref[3:4, :]
        sl = _sigmoid(ln)
        dln = dconv_o * (sl * (1.0 + ln * (1.0 - sl)))
        dlb = jnp.sum(dln, axis=0, keepdims=True)
        dlg = jnp.sum(dln * yhat, axis=0, keepdims=True)
        dyh = dln * lg
        dyc = rstd * (dyh - jnp.mean(dyh, axis=-1, keepdims=True)
                      - yhat * jnp.mean(dyh * yhat, axis=-1, keepdims=True))
        dcb = jnp.sum(dyc, axis=0, keepdims=True)
        a = u_ref[:, 0:DC]
        gate = u_ref[:, DC:2 * DC]
        sgt = _sigmoid(gate)
        vv = a * sgt
        pos = _seg_pos(tt)
        cw = cw_ref[...]
        dvv = jnp.zeros((tt, DC), F32)
        dws = []
        for j in range(CW):
            shifted_dyc = _shifted(dyc, CPAD - j, pos, tt)
            dvv = dvv + shifted_dyc * cw[j:j + 1, :]
            dws.append(jnp.sum(shifted_dyc * vv, axis=0, keepdims=True))
        dws.append(jnp.zeros((1, DC), F32))
        du_ref[:, 0:DC] = (dvv * sgt).astype(MXU_DTYPE)
        du_ref[:, DC:2 * DC] = (dvv * a * sgt * (1.0 - sgt)).astype(MXU_DTYPE)
        _acc_rows(acc1_ref, [dg1])
        _acc_rows(acc2_ref, [dgn, dcb, dlg, dlb])
        dcw_ref[...] += jnp.concatenate(dws, axis=0)

    return pl.pallas_call(
        body, name="merge_bwd", grid=(t // tt,),
        in_specs=[_rows(tt, D), _rows(tt, D), _rows(tt, 1024), _rows(tt, DV), _rows(tt, DV), _rows(tt, DV),
                  _rows(tt, DC),
                  _full((8, D)), _full((8, DC)), _full((32, DC)), _full((D, D))],
        out_specs=[_rows(tt, 1024), _rows(tt, DV), _rows(tt, DV), _rows(tt, D), _full((8, D)), _full((8, DC)),
                   _full((32, DC))],
        out_shape=[_sds((t, 1024), MXU_DTYPE), _sds((t, DV), MXU_DTYPE), _sds((t, DV), MXU_DTYPE),
                   _sds((t, D), MXU_DTYPE), _sds((8, D)),
                   _sds((8, DC)), _sds((32, DC))],
        compiler_params=_cparams(1),
    )(dx1, mix, u, g, o_f, o_b, yc, vec, vc, convw, w_out)


FN = DFF // 2


def ffn_gate_up(h2, wg_t, wu_t, tt):
    t = h2.shape[0]

    def body(h2_ref, wg_ref, wu_ref, s_ref, d_ref, hid_ref):
        h2v = h2_ref[...]
        gt = _mm_nt(h2v, wg_ref[...])
        up = _mm_nt(h2v, wu_ref[...])
        sg = _sigmoid(gt)
        act = gt * sg
        s_ref[...] = act.astype(MXU_DTYPE)
        d_ref[...] = (up * (sg * (1.0 + gt * (1.0 - sg)))).astype(MXU_DTYPE)
        hid_ref[...] = (act * up).astype(MXU_DTYPE)

    blk = pl.BlockSpec((tt, FN), lambda j, i: (i, j))
    wblk = pl.BlockSpec((FN, D), lambda j, i: (j, 0))
    return pl.pallas_call(
        body, name="ffn_gate_up", grid=(2, t // tt),
        in_specs=[pl.BlockSpec((tt, D), lambda j, i: (i, 0)), wblk, wblk],
        out_specs=[blk, blk, blk],
        out_shape=[_sds((t, DFF), MXU_DTYPE)] * 3,
        compiler_params=_cparams(2),
    )(h2, wg_t, wu_t)


def ffn_down_loss(hid, x1, tgt, vec, w_down, tt):
    t = x1.shape[0]

    def body(hid_ref, x1_ref, tgt_ref, vec_ref, w_ref, dx2_ref, dff_ref, acc_ref):
        _zero_first(acc_ref)
        g2 = vec_ref[0:1, :]
        fg = vec_ref[1:2, :]
        ff = _mm(hid_ref[...], w_ref[...])
        x2 = x1_ref[...] + g2 * ff
        rf = lax.rsqrt(jnp.mean(x2 * x2, axis=-1, keepdims=True) + EPS)
        xn = x2 * rf
        err = xn * fg - tgt_ref[...]
        dy = err * (1.0 / D)
        dfg = jnp.sum(dy * xn, axis=0, keepdims=True)
        dxn = dy * fg
        dx2 = rf * (dxn - xn * jnp.mean(dxn * xn, axis=-1, keepdims=True))
        dx2_ref[...] = dx2
        dff_ref[...] = (g2 * dx2).astype(MXU_DTYPE)
        dg2 = jnp.sum(dx2 * ff, axis=0, keepdims=True)
        loss = jnp.sum(err * err, axis=0, keepdims=True) * (0.5 / D)
        _acc_rows(acc_ref, [dg2, dfg, loss])

    return pl.pallas_call(
        body, name="ffn_down_loss", grid=(t // tt,),
        in_specs=[_rows(tt, DFF), _rows(tt, D), _rows(tt, D), _full((8, D)), _full((DFF, D))],
        out_specs=[_rows(tt, D), _rows(tt, D), _full((8, D))],
        out_shape=[_sds((t, D)), _sds((t, D), MXU_DTYPE), _sds((8, D))],
        compiler_params=_cparams(1),
    )(hid, x1, tgt, vec, w_down)


def ffn_dhid(dff, s, d, w_down, tt):
    t = dff.shape[0]

    def body(dff_ref, s_ref, d_ref, w_ref, dgt_ref, dup_ref):
        dhid = _mm_nt(dff_ref[...], w_ref[...])
        dgt_ref[...] = (dhid * d_ref[...].astype(F32)).astype(MXU_DTYPE)
        dup_ref[...] = (dhid * s_ref[...].astype(F32)).astype(MXU_DTYPE)

    blk = pl.BlockSpec((tt, FN), lambda j, i: (i, j))
    return pl.pallas_call(
        body, name="ffn_dhid", grid=(2, t // tt),
        in_specs=[pl.BlockSpec((tt, D), lambda j, i: (i, 0)), blk, blk, pl.BlockSpec((FN, D), lambda j, i: (j, 0))],
        out_specs=[blk, blk],
        out_shape=[_sds((t, DFF), MXU_DTYPE), _sds((t, DFF), MXU_DTYPE)],
        compiler_params=_cparams(2),
    )(dff, s, d, w_down)


def ffn_dh2(dgt, dup, x1, dx2, vec, wg_t, wu_t, tt):
    t = x1.shape[0]

    def body(dgt_ref, dup_ref, x1_ref, dx2_ref, vec_ref, wg_ref, wu_ref, dx1_ref, acc_ref):
        _zero_first(acc_ref)
        dh2 = _mm(dgt_ref[...], wg_ref[...]) + _mm(dup_ref[...], wu_ref[...])
        g, sc = vec_ref[0:1, :], vec_ref[2:3, :]
        r, xn, yy, _ = _norm_mod(x1_ref[...], g, vec_ref[1:2, :], sc)
        dsh, dsc, dgn, dx = _norm_mod_bwd(dh2, r, xn, yy, g, sc)
        dx1_ref[...] = dx2_ref[...] + dx
        _acc_rows(acc_ref, [dsh, dsc, dgn])

    return pl.pallas_call(
        body, name="ffn_dh2", grid=(t // tt,),
        in_specs=[_rows(tt, DFF), _rows(tt, DFF), _rows(tt, D), _rows(tt, D), _full((8, D)), _full((DFF, D)),
                  _full((DFF, D))],
        out_specs=[_rows(tt, D), _full((8, D))],
        out_shape=[_sds((t, D)), _sds((8, D))],
        compiler_params=_cparams(1),
    )(dgt, dup, x1, dx2, vec, wg_t, wu_t)


def tn_matmul(a, b, bm, bt, init=None):
    t, m = a.shape
    n = b.shape[1]
    nk = t // bt

    def body(*refs):
        if init is None:
            a_ref, b_ref, o_ref, wire_ref = refs
        else:
            a_ref, b_ref, i_ref, o_ref, wire_ref = refs
        @pl.when(pl.program_id(1) == 0)
        def _():
            o_ref[...] = jnp.zeros_like(o_ref) if init is None else i_ref[...]

        o_ref[...] += _mm_tn(a_ref[...], b_ref[...])

        @pl.when(pl.program_id(1) == nk - 1)
        def _():
            wire_ref[...] = o_ref[...].astype(WIRE_DTYPE)

    in_specs = [pl.BlockSpec((bt, bm), lambda i, k: (k, i)), pl.BlockSpec((bt, n), lambda i, k: (k, 0))]
    args = [a, b]
    if init is not None:
        in_specs.append(pl.BlockSpec((bm, n), lambda i, k: (i, 0)))
        args.append(init)
    oblk = pl.BlockSpec((bm, n), lambda i, k: (i, 0))
    return pl.pallas_call(
        body, name="tn_matmul", grid=(m // bm, nk),
        in_specs=in_specs, out_specs=[oblk, oblk],
        out_shape=[_sds((m, n)), _sds((m, n), WIRE_DTYPE)], compiler_params=_cparams(2),
    )(*args)


def _adamw(w, g, m, v):
    m = ADAM_B1 * m + (1.0 - ADAM_B1) * g
    v = ADAM_B2 * v + (1.0 - ADAM_B2) * (g * g)
    m_hat = m / (1.0 - ADAM_B1 ** ADAM_STEP)
    v_hat = v / (1.0 - ADAM_B2 ** ADAM_STEP)
    delta = -ADAM_LR * (m_hat / (jnp.sqrt(v_hat) + ADAM_EPS) + ADAM_WD * w)
    return delta, m, v


def adamw_sharded(own, recv, w, m, v):
    shape = w.shape

    def body(own_ref, recv_ref, w_ref, m_ref, v_ref, g_ref, d_ref, mo_ref, vo_ref):
        g = own_ref[...]
        for k in range(N_DEV - 1):
            g = g + recv_ref[k].astype(F32)
        g_ref[...] = g
        d_ref[...], mo_ref[...], vo_ref[...] = _adamw(w_ref[...], g, m_ref[...], v_ref[...])

    return pl.pallas_call(
        body, name="adamw_sharded",
        in_specs=[_full(shape), _full((N_DEV - 1,) + shape), _full(shape), _full(shape), _full(shape)],
        out_specs=[_full(shape)] * 4, out_shape=[_sds(shape)] * 4, grid=(1,),
        compiler_params=_cparams(1),
    )(own, recv, w, m, v)


def adamw_small(items):
    n = len(items)
    flat = [a for it in items for a in it]

    def body(*refs):
        ins, outs = refs[:4 * n], refs[4 * n:]
        for i in range(n):
            g, w, m, v = (r[...] for r in ins[4 * i:4 * i + 4])
            outs[3 * i][...], outs[3 * i + 1][...], outs[3 * i + 2][...] = _adamw(w, g, m, v)

    out = pl.pallas_call(
        body, name="adamw_small", grid=(1,),
        in_specs=[_full(a.shape) for a in flat],
        out_specs=[_full(it[1].shape) for it in items for _ in range(3)],
        out_shape=[_sds(it[1].shape) for it in items for _ in range(3)],
        compiler_params=_cparams(1),
    )(*flat)
    return [tuple(out[3 * i:3 * i + 3]) for i in range(n)]


def _mesh_pos():
    x, y, c = lax.axis_index("x"), lax.axis_index("y"), lax.axis_index("c")
    me = 4 * x + 2 * y + c
    peers = []
    for k in range(1, N_DEV):
        peers.append(((1 - x) if (k >> 2) & 1 else x, (1 - y) if (k >> 1) & 1 else y, (1 - c) if k & 1 else c))
    return me, peers


def _all_gather_issue(buf, send_sems, recv_sems, me, peers):
    sends = []
    for k, peer in enumerate(peers):
        cp = pltpu.make_async_remote_copy(src_ref=buf.at[me], dst_ref=buf.at[me], send_sem=send_sems.at[k],
                                          recv_sem=recv_sems.at[k], device_id=peer, device_id_type=MESH)
        cp.start()
        sends.append(cp)
    return sends


def _all_gather_finish(buf, send_sems, recv_sems, me, peers, sends):
    for k, peer in enumerate(peers):
        src = jnp.bitwise_xor(me, k + 1)
        pltpu.make_async_remote_copy(src_ref=buf.at[src], dst_ref=buf.at[src], send_sem=send_sems.at[k],
                                     recv_sem=recv_sems.at[k], device_id=peer, device_id_type=MESH).wait_recv()
    for cp in sends:
        cp.wait_send()


def _all_gather(buf, send_sems, recv_sems, me, peers):
    _all_gather_finish(buf, send_sems, recv_sems, me, peers, _all_gather_issue(buf, send_sems, recv_sems, me, peers))


_VMEM = pl.BlockSpec(memory_space=pltpu.VMEM)
_ANY = pl.BlockSpec(memory_space=pl.ANY)
_SEMS = pltpu.SemaphoreType.DMA((N_DEV - 1,))


def mod_forward(c, c_ctx, w_mod_sh, b_mod, sm_pack, w_first):
    ncol = w_mod_sh.shape[1]

    def body(c_ref, cc_ref, w_ref, b_ref, sm_ref, wf_ref, mod_ref, s_ref, smt_ref, wall_ref, cbuf, pbuf, smbuf,
             s1, r1, s2, r2, s3, r3, ws, wr, wl):
        me, peers = _mesh_pos()
        x, y, cc = lax.axis_index("x"), lax.axis_index("y"), lax.axis_index("c")
        sibling = (x, y, 1 - cc)
        chips = [(1 - x, y), (x, 1 - y), (1 - x, 1 - y)]
        slot = lambda px, py, pc: wall_ref.at[4 * px + 2 * py + pc]

        def wcopy(k, block, to, src=None):
            return pltpu.make_async_remote_copy(
                src_ref=slot(*block) if src is None else src, dst_ref=slot(*block), send_sem=ws.at[k],
                recv_sem=wr.at[k], device_id=to, device_id_type=MESH)

        cbuf[me] = jnp.broadcast_to(c_ref[...], (8, D))
        c_sends = _all_gather_issue(cbuf, s1, r1, me, peers)
        mine = pltpu.make_async_copy(wf_ref, slot(x, y, cc), wl)
        mine.start()
        first = [wcopy(0, (x, y, cc), sibling, src=wf_ref)]
        first += [wcopy(1 + j, (x, y, cc), (*chip, cc), src=wf_ref) for j, chip in enumerate(chips)]
        for cp in first:
            cp.start()
        smbuf[me] = sm_ref[...]
        sm_sends = _all_gather_issue(smbuf, s3, r3, me, peers)
        _all_gather_finish(cbuf, s1, r1, me, peers, c_sends)
        rows = [cbuf[j, 0:1, :] for j in range(N_DEV)] + [cc_ref[...], jnp.zeros((7, D), F32)]
        sx = jnp.concatenate(rows, axis=0)
        s = sx * _sigmoid(sx)
        s_ref[...] = s
        pbuf[me] = _hi(s, w_ref[...])
        _all_gather(pbuf, s2, r2, me, peers)
        for j in range(N_DEV):
            mod_ref[:, j * ncol:(j + 1) * ncol] = pbuf[j] + b_ref[:, j * ncol:(j + 1) * ncol]
        _all_gather_finish(smbuf, s3, r3, me, peers, sm_sends)
        tot = smbuf[0]
        for j in range(1, N_DEV):
            tot = tot + smbuf[j]
        smt_ref[...] = tot
        passed = [wcopy(4 + j, (*chip, cc), sibling) for j, chip in enumerate(chips)]
        for j, chip in enumerate(chips):
            wcopy(1 + j, (*chip, cc), (x, y, cc)).wait_recv()
            passed[j].start()
        wcopy(0, (x, y, 1 - cc), (x, y, cc)).wait_recv()
        for j, chip in enumerate(chips):
            wcopy(4 + j, (*chip, 1 - cc), (x, y, cc)).wait_recv()
        for cp in first + passed:
            cp.wait_send()
        mine.wait()

    return pl.pallas_call(
        body, name="mod_forward",
        in_specs=[_VMEM] * 6, out_specs=[_VMEM] * 4,
        out_shape=[_sds((16, N_DEV * ncol)), _sds((16, D)), _sds(sm_pack.shape),
                   _sds((N_DEV,) + w_first.shape, w_first.dtype)],
        scratch_shapes=[pltpu.VMEM((N_DEV, 8, D), F32), pltpu.VMEM((N_DEV, 16, ncol), F32),
                        pltpu.VMEM((N_DEV,) + sm_pack.shape, F32), _SEMS, _SEMS, _SEMS, _SEMS, _SEMS, _SEMS,
                        _SEMS, _SEMS, pltpu.SemaphoreType.DMA],
        compiler_params=pltpu.CompilerParams(vmem_limit_bytes=VMEM_LIMIT),
    )(c, c_ctx, w_mod_sh, b_mod, sm_pack, w_first)


def sum_blocks(gat, loss_row):
    def body(g_ref, tot_ref, loss_ref):
        tot = g_ref[0]
        for j in range(1, N_DEV):
            tot = tot + g_ref[j]
        tot_ref[...] = tot
        loss_ref[...] = jnp.sum(tot[loss_row:loss_row + 1, :], axis=1, keepdims=True)

    return pl.pallas_call(
        body, name="sum_blocks", in_specs=[_VMEM], out_specs=[_VMEM, _VMEM],
        out_shape=[_sds(gat.shape[1:]), _sds((1, 1))],
        compiler_params=pltpu.CompilerParams(vmem_limit_bytes=VMEM_LIMIT),
    )(gat)


def mod_backward(s, dm_sh, w, m, v, cc, m_cc, v_cc):
    shape = w.shape

    def body(s_ref, dm_ref, w_ref, m_ref, v_ref, cc_ref, mcc_ref, vcc_ref,
             gw_ref, dw_ref, mw_ref, vw_ref, gc_ref, dc_ref, mc_ref, vc_ref, pbuf, send_sems, recv_sems):
        me, peers = _mesh_pos()
        wv = w_ref[...]
        pbuf[me] = _hi_nt(dm_ref[8:16, :], wv)
        _all_gather(pbuf, send_sems, recv_sems, me, peers)
        g = _hi_tn(s_ref[...], dm_ref[...])
        gw_ref[...] = g
        dw_ref[...], mw_ref[...], vw_ref[...] = _adamw(wv, g, m_ref[...], v_ref[...])
        tot = pbuf[0]
        for j in range(1, N_DEV):
            tot = tot + pbuf[j]
        ccv = cc_ref[...]
        sg = _sigmoid(ccv)
        gc = tot[0:1, :] * (sg * (1.0 + ccv * (1.0 - sg)))
        gc_ref[...] = gc
        dc_ref[...], mc_ref[...], vc_ref[...] = _adamw(ccv, gc, mcc_ref[...], vcc_ref[...])

    return pl.pallas_call(
        body, name="mod_backward", in_specs=[_VMEM] * 8, out_specs=[_VMEM] * 8,
        out_shape=[_sds(shape)] * 4 + [_sds((1, D))] * 4,
        scratch_shapes=[pltpu.VMEM((N_DEV, 8, D), F32), _SEMS, _SEMS],
        compiler_params=pltpu.CompilerParams(vmem_limit_bytes=VMEM_LIMIT),
    )(s, dm_sh, w, m, v, cc, m_cc, v_cc)


_HBM = pl.BlockSpec(memory_space=pltpu.HBM)
_SEM = pl.BlockSpec(memory_space=pltpu.SEMAPHORE)
_EFFECT = pltpu.SideEffectType.DATAFLOW_SIDE_EFFECTING
_hbm = lambda a: pltpu.with_memory_space_constraint(a, pltpu.HBM)


def gather_start(shards, me, tag):
    n = len(shards)
    sems = pltpu.SemaphoreType.DMA((7 * n,))
    lands = [lax.dynamic_update_slice(lax.empty((N_DEV,) + s.shape, s.dtype), s[None], (me, 0, 0)) for s in shards]

    def body(*refs):
        s_refs, l_refs = refs[:n], refs[n:2 * n]
        send_sems, recv_sems = refs[2 * n], refs[2 * n + 1]
        token = refs[-1]
        my, peers = _mesh_pos()
        for w in range(n):
            for k, peer in enumerate(peers):
                pltpu.make_async_remote_copy(
                    src_ref=s_refs[w], dst_ref=l_refs[w].at[my], send_sem=send_sems.at[w * 7 + k],
                    recv_sem=recv_sems.at[w * 7 + k], device_id=peer, device_id_type=MESH).start()
        token[...] = jnp.zeros_like(token)

    out = pl.pallas_call(
        body, name="gather_start_" + tag,
        out_shape=(sems, sems) + tuple(pltpu.HBM(a.shape, a.dtype) for a in list(shards) + lands) + (_sds((8, 128)),),
        in_specs=(_HBM,) * (2 * n), out_specs=(_SEM, _SEM) + (_HBM,) * (2 * n) + (_VMEM,),
        input_output_aliases={i: i + 2 for i in range(2 * n)},
        compiler_params=pltpu.CompilerParams(has_side_effects=_EFFECT),
    )(*[_hbm(a) for a in list(shards) + lands])
    return out[0], out[1], list(out[2:2 + n]), list(out[2 + n:2 + 2 * n]), out[-1]


def gather_wait(send_sems, recv_sems, shards, lands, after, tag):
    n = len(shards)

    def body(*refs):
        s_refs, l_refs = refs[:n], refs[n:2 * n]
        send_sems, recv_sems = refs[2 * n], refs[2 * n + 1]
        my, peers = _mesh_pos()
        for w in range(n):
            for k, peer in enumerate(peers):
                src = jnp.bitwise_xor(my, k + 1)
                cp = pltpu.make_async_remote_copy(
                    src_ref=s_refs[w], dst_ref=l_refs[w].at[src], send_sem=send_sems.at[w * 7 + k],
                    recv_sem=recv_sems.at[w * 7 + k], device_id=peer, device_id_type=MESH)
                cp.wait_send()
                cp.wait_recv()

    out = pl.pallas_call(
        body, name="gather_wait_" + tag,
        out_shape=tuple(pltpu.HBM(a.shape, a.dtype) for a in list(shards) + list(lands)),
        in_specs=(_HBM,) * (2 * n) + (_SEM, _SEM, _ANY), out_specs=(_HBM,) * (2 * n),
        input_output_aliases={i: i for i in range(2 * n)},
        compiler_params=pltpu.CompilerParams(has_side_effects=_EFFECT),
    )(*shards, *lands, send_sems, recv_sems, after)
    return list(out[n:2 * n])


def scatter_start(grads, tag):
    n = len(grads)
    sems = pltpu.SemaphoreType.DMA((7 * n,))
    lands = [lax.empty((N_DEV - 1,) + g.shape[1:], g.dtype) for g in grads]

    def body(*refs):
        g_refs, l_refs = refs[:n], refs[n:2 * n]
        send_sems, recv_sems = refs[2 * n], refs[2 * n + 1]
        token = refs[-1]
        me, peers = _mesh_pos()
        for w in range(n):
            for k, peer in enumerate(peers):
                dst = jnp.bitwise_xor(me, k + 1)
                pltpu.make_async_remote_copy(
                    src_ref=g_refs[w].at[dst], dst_ref=l_refs[w].at[k], send_sem=send_sems.at[w * 7 + k],
                    recv_sem=recv_sems.at[w * 7 + k], device_id=peer, device_id_type=MESH).start()
        token[...] = jnp.zeros_like(token)

    out = pl.pallas_call(
        body, name="scatter_start_" + tag,
        out_shape=(sems, sems) + tuple(pltpu.HBM(a.shape, a.dtype) for a in list(grads) + lands) + (_sds((8, 128)),),
        in_specs=(_HBM,) * (2 * n), out_specs=(_SEM, _SEM) + (_HBM,) * (2 * n) + (_VMEM,),
        input_output_aliases={i: i + 2 for i in range(2 * n)},
        compiler_params=pltpu.CompilerParams(has_side_effects=_EFFECT),
    )(*[_hbm(a) for a in list(grads) + lands])
    return out[0], out[1], list(out[2:2 + n]), list(out[2 + n:2 + 2 * n]), out[-1]


def scatter_wait(send_sems, recv_sems, grads, lands, after, tag):
    n = len(grads)

    def body(*refs):
        g_refs, l_refs = refs[:n], refs[n:2 * n]
        send_sems, recv_sems = refs[2 * n], refs[2 * n + 1]
        me, peers = _mesh_pos()
        for w in range(n):
            for k, peer in enumerate(peers):
                dst = jnp.bitwise_xor(me, k + 1)
                cp = pltpu.make_async_remote_copy(
                    src_ref=g_refs[w].at[dst], dst_ref=l_refs[w].at[k], send_sem=send_sems.at[w * 7 + k],
                    recv_sem=recv_sems.at[w * 7 + k], device_id=peer, device_id_type=MESH)
                cp.wait_send()
                cp.wait_recv()

    out = pl.pallas_call(
        body, name="scatter_wait_" + tag,
        out_shape=tuple(pltpu.HBM(a.shape, a.dtype) for a in list(grads) + list(lands)),
        in_specs=(_HBM,) * (2 * n) + (_SEM, _SEM, _ANY), out_specs=(_HBM,) * (2 * n),
        input_output_aliases={i: i for i in range(2 * n)},
        compiler_params=pltpu.CompilerParams(has_side_effects=_EFFECT),
    )(*grads, *lands, send_sems, recv_sems, after)
    return list(out[n:2 * n])


def _vec8(rows, width):
    rid = lax.broadcasted_iota(jnp.int32, (8, width), 0)
    out = jnp.zeros((8, width), F32)
    for i, r in enumerate(rows):
        r = r.reshape(-1)
        r = jnp.pad(r, (0, width - r.shape[0]))
        out = jnp.where(rid == i, r[None, :], out)
    return out


def local_step(x, ctx, tgt, mod, mod_c, small, w_int, start, late_weights, grads_ready, small_ready, tt, tt_ctx, cb,
               cb_ctx):
    sh1, sc1, g1, sh2, sc2, g2 = [mod[i * D:(i + 1) * D] for i in range(6)]
    csh1, csc1 = mod_c[0:D], mod_c[D:2 * D]
    vec1 = _vec8([small["norm1_g"], sh1, sc1], D)
    vec1c = _vec8([small["norm1_g"], csh1, csc1], D)
    vec2 = _vec8([small["norm2_g"], sh2, sc2], D)
    vec3 = _vec8([g2, small["final_g"]], D)
    vecm = _vec8([g1, small["norm2_g"], sh2, sc2], D)
    vcm = _vec8([jnp.tile(small["gla_norm_g"].reshape(HV), NH), small["conv_b"], small["conv_ln_g"],
                 small["conv_ln_b"]], DC)
    convw = jnp.pad(small["conv_w"], ((0, 1), (0, 0)))
    wa = jnp.zeros((128, 512), F32)
    wa = wa.at[0:RANK, 0:DK].set(small["w_a2_f"]).at[RANK:2 * RANK, DK:2 * DK].set(small["w_a2_b"])
    ba = jnp.concatenate([small["b_a_f"].reshape(1, DK), small["b_a_b"].reshape(1, DK)], axis=1)

    _, _, kc, vc_, _, rc, lac, hc = proj_fwd(ctx, vec1c + start, w_int, wa, ba, tt_ctx)
    qc0 = jnp.zeros_like(kc)
    _, _, sallf_c, sallb_c, sfin_c = gla_fwd(qc0, kc, vc_, lac, jnp.zeros((2, HV, DK), F32), cb_ctx)
    u, q, k, v, g, r, la, h = proj_fwd(x, vec1, w_int, wa, ba, tt)
    o_f, o_b, sall_f, sall_b, _ = gla_fwd(q, k, v, la, sfin_c, cb)
    w_out, wg_t, wu_t, w_down = late_weights(o_b)
    x1, cat, mix, yc, h2 = merge_fwd(u, g, o_f, o_b, x, vecm, vcm, convw, w_out, tt)
    tt2 = min(2 * tt, x.shape[0])
    act, dact, hid = ffn_gate_up(h2, wg_t, wu_t, tt2)
    dx2, dff, acc3 = ffn_down_loss(hid, x1, tgt, vec3, w_down, tt)
    dgt, dup = ffn_dhid(dff, act, dact, w_down, tt2)
    dx1, acc2 = ffn_dh2(dgt, dup, x1, dx2, vec2, wg_t, wu_t, tt)
    bt = min(GRAD_TOKEN_BLOCK, x.shape[0])
    gw = {"w_down": tn_matmul(hid, dff, FN, bt), "wg_t": tn_matmul(dgt, h2, FN, bt),
          "wu_t": tn_matmul(dup, h2, FN, bt)}
    vecm = vecm + grads_ready(("wg_t", "wu_t", "w_down"), gw)
    du, dg, do, dmix, accm1, accm2, dconvw = merge_bwd(dx1, mix, u, g, o_f, o_b, yc, vecm, vcm, convw, w_out, tt)
    gw["w_out"] = tn_matmul(cat, dmix, D, bt)
    dsfin = jnp.zeros((2, HV, DK), F32) + grads_ready(("w_out",), gw)
    dqf, dkf, dvf, dlaf, dqb, dkb, dvb, dlab, ds0 = gla_bwd(q, k, v, la, do, sall_f, sall_b, dsfin, cb)
    gx, dp, acc1, dba, dwa = proj_bwd(du, (dqf, dqb, dkf, dkb, dvf, dvb), dg, dlaf, dlab, la, r, x, dx1, vec1, w_int,
                                      wa, tt)
    tcx = ctx.shape[0]
    zc = lambda w, dt=MXU_DTYPE: jnp.zeros((tcx, w), dt)
    _, dkf, dvf, dlaf, _, dkb, dvb, dlab, _ = gla_bwd(qc0, kc, vc_, lac, zc(DV), sallf_c, sallb_c, ds0, cb_ctx)
    _, dpc, acc1c, dbac, dwac = proj_bwd(zc(1024), (zc(DK), zc(DK), dkf, dkb, dvf, dvb), zc(DV), dlaf, dlab, lac, rc,
                                         ctx, zc(D, F32), vec1c, w_int, wa, tt_ctx)
    dwa_t = dwa + dwac
    dba_t = dba + dbac
    gs = {
        "norm1_g": acc1[2] + acc1c[2], "norm2_g": acc2[2], "final_g": acc3[1], "loss": acc3[2],
        "gla_norm_g": accm2[0], "conv_b": accm2[1], "conv_ln_g": accm2[2], "conv_ln_b": accm2[3],
        "conv_w": dconvw, "b_a": dba_t[0], "w_a2": dwa_t,
    }
    dmod = _vec8([acc1[0], acc1[1], accm1[0], acc2[0], acc2[1], acc3[0]], D)
    dmod_c = _vec8([acc1c[0], acc1c[1]], D)
    dpc = dpc + small_ready(gs, dmod, dmod_c).astype(dpc.dtype)
    btc = min(GRAD_TOKEN_BLOCK, tcx)
    gw["w_int"] = tn_matmul(dp, h, 896, bt, init=tn_matmul(dpc, hc, 896, btc)[0])
    grads_ready(("w_int",), gw)
    return gx, gw


PACK_ROWS = 96
ROW_N1, ROW_N2, ROW_FG, ROW_LOSS, ROW_GN, ROW_CB, ROW_LG, ROW_LB, ROW_BA = 0, 1, 2, 3, 4, 5, 6, 7, 8
ROW_DMOD, ROW_DMODC, ROW_CW, ROW_WA = 16, 24, 32, 64


def _pack_small(gs, dmod, dmod_c):
    pad = lambda a: jnp.pad(a, ((0, 0), (0, D - a.shape[1])))
    singles = _vec8([gs["norm1_g"], gs["norm2_g"], gs["final_g"], gs["loss"], gs["gla_norm_g"], gs["conv_b"],
                     gs["conv_ln_g"], gs["conv_ln_b"]], D)
    return jnp.concatenate([singles, _vec8([gs["b_a"]], D), dmod, dmod_c, pad(gs["conv_w"]), pad(gs["w_a2"][0:32])],
                           axis=0)


def kernel(x, c, ctx, c_ctx, w_mod, b_mod, norm1_g, norm2_g, w_in, conv_w, conv_b, conv_ln_g, conv_ln_b, w_a2_f, b_a_f, w_a2_b, b_a_b, gla_norm_g, w_out, w_gate, w_up, w_down, final_g, loss_target, m_c_ctx, m_w_mod, m_b_mod, m_norm1_g, m_norm2_g, m_w_in, m_conv_w, m_conv_b, m_conv_ln_g, m_conv_ln_b, m_w_a2_f, m_b_a_f, m_w_a2_b, m_b_a_b, m_gla_norm_g, m_w_out, m_w_gate, m_w_up, m_w_down, m_final_g, v_c_ctx, v_w_mod, v_b_mod, v_norm1_g, v_norm2_g, v_w_in, v_conv_w, v_conv_b, v_conv_ln_g, v_conv_ln_b, v_w_a2_f, v_b_a_f, v_w_a2_b, v_b_a_b, v_gla_norm_g, v_w_out, v_w_gate, v_w_up, v_w_down, v_final_g):
    me = 4 * lax.axis_index("x") + 2 * lax.axis_index("y") + lax.axis_index("c")
    t = x.shape[1]
    tcx = ctx.shape[1]
    r_in, r_out, r_ff = w_in.shape[2], w_out.shape[1], w_gate.shape[2]
    r_in_b = -(-r_in // 16) * 16

    tb = lambda w: w.T.astype(MXU_DTYPE)

    small = dict(norm1_g=norm1_g[0], norm2_g=norm2_g[0], final_g=final_g, gla_norm_g=gla_norm_g[0],
                 conv_b=conv_b[0], conv_ln_g=conv_ln_g[0], conv_ln_b=conv_ln_b[0], b_a_f=b_a_f[0], b_a_b=b_a_b[0])
    sm_pack = jnp.zeros((48, DC), F32)
    sm_pack = lax.dynamic_update_slice(sm_pack, conv_w[0], (0, me * (DC // N_DEV)))
    sm_pack = lax.dynamic_update_slice(sm_pack, w_a2_f[0], (32, me * (DK // N_DEV)))
    sm_pack = lax.dynamic_update_slice(sm_pack, w_a2_b[0], (32, DK + me * (DK // N_DEV)))

    mod_all, s_all, sm_tot, wall = mod_forward(c, c_ctx.reshape(1, D), w_mod[0], b_mod, sm_pack,
                                               jnp.pad(tb(w_in[0]), ((0, r_in_b - r_in), (0, 0))))
    mod = lax.dynamic_slice(mod_all, (me, 0), (1, 6 * D)).reshape(6 * D)
    mod_c = mod_all[8]
    small["conv_w"] = sm_tot[0:CW, :]
    small["w_a2_f"] = sm_tot[32:32 + RANK, 0:DK]
    small["w_a2_b"] = sm_tot[32:32 + RANK, DK:2 * DK]

    w_int =jnp.pad(wall[:, 0:r_in, :].reshape(N_DEV * r_in, D), ((0, DINP - DIN), (0, 0)))
    after_w_in = (wall[0:1, 0:1, 0] * 0).astype(MXU_DTYPE)
    late = [w_out[0].astype(MXU_DTYPE) + after_w_in, tb(w_gate[0]) + after_w_in, tb(w_up[0]) + after_w_in,
            w_down[0].astype(MXU_DTYPE) + after_w_in]
    g_send, g_recv, late_thru, late_lands, g_token = gather_start(late, me, "late")

    def late_weights(after):
        got = gather_wait(g_send, g_recv, late_thru, late_lands, after, "late")
        return tuple(a.reshape(N_DEV * a.shape[1], D) for a in got)

    pad_in = lambda g: jnp.pad(g[0:DIN].reshape(N_DEV, r_in, D), ((0, 0), (0, r_in_b - r_in), (0, 0)))
    blocked = {"w_int": pad_in, "w_out": lambda g: g.reshape(N_DEV, r_out, D)}
    as_blocks = lambda n, g: blocked.get(n, lambda a: a.reshape(N_DEV, r_ff, D))(g)
    pending = []

    def grads_ready(names, gw_now):
        blocks = [as_blocks(n, gw_now[n][1]) for n in names]
        send, recv_s, thru, zones, token = scatter_start(blocks, names[0])
        pending.append((names, send, recv_s, thru, zones))
        if names[0] == "w_int":
            finish_small(token)
        return token[0:1, 0:1]

    sm = {}

    def small_ready(gs, dmod, dmod_c):
        sm["copy"] = gather_start([_pack_small(gs, dmod, dmod_c)], me, "small")
        return sm["copy"][4][0:1, 0:1]

    def finish_small(after):
        send, recv_s, thru, zones, _ = sm["copy"]
        gat = gather_wait(send, recv_s, thru, zones, after, "small")[0]
        sm["tot"], sm["loss"] = sum_blocks(gat, ROW_LOSS)
        sm["dm"] = jnp.concatenate(
            [gat[:, ROW_DMOD:ROW_DMOD + 6, :].reshape(N_DEV, 6 * D),
             jnp.pad(sm["tot"][ROW_DMODC:ROW_DMODC + 6, :].reshape(1, 6 * D), ((0, 7), (0, 0)))], axis=0)
        ncol = w_mod.shape[2]
        dm_sh = lax.dynamic_slice(sm["dm"], (0, me * ncol), (16, ncol))
        sm["mod"] = mod_backward(s_all, dm_sh, w_mod[0], m_w_mod[0], v_w_mod[0], c_ctx.reshape(1, D),
                                 m_c_ctx.reshape(1, D), v_c_ctx.reshape(1, D))
        return sm["mod"][4][0:1, 0:1] * 0

    gx, gw = local_step(x[0], ctx[0], loss_target[0], mod, mod_c, small, w_int, g_token[0:1, 0:1], late_weights,
                        grads_ready, small_ready, TOKEN_TILE, CTX_TOKEN_TILE, GLA_CHUNKS, CTX_GLA_CHUNKS)
    tot = sm["tot"]
    loss = sm["loss"].reshape(())
    g_wmod, d_wmod, nm_wmod, nv_wmod, g_cc, d_cc, nm_cc, nv_cc = sm["mod"]

    recv = {}

    def wait_for(entry, after):
        names, send, recv_s, thru, zones = entry
        recv.update(dict(zip(names, scatter_wait(send, recv_s, thru, zones, after, names[0]))))

    for entry in pending[:-1]:
        wait_for(entry, tot)
    own = {n: lax.dynamic_index_in_dim(as_blocks(n, gw[n][0]), me, 0, keepdims=False) for n in gw if n != "w_int"}
    own["w_int"] = jnp.pad(lax.dynamic_slice(gw["w_int"][0], (me * r_in, 0), (r_in, D)), ((0, r_in_b - r_in), (0, 0)))
    padt = lambda w: jnp.pad(w.T, ((0, r_in_b - r_in), (0, 0)))
    big = {}
    big["w_gate"] = [a.T for a in adamw_sharded(own["wg_t"], recv["wg_t"], w_gate[0].T, m_w_gate[0].T,
                                                 v_w_gate[0].T)]
    big["w_up"] = [a.T for a in adamw_sharded(own["wu_t"], recv["wu_t"], w_up[0].T, m_w_up[0].T, v_w_up[0].T)]
    big["w_down"] = adamw_sharded(own["w_down"], recv["w_down"], w_down[0], m_w_down[0], v_w_down[0])
    big["w_out"] = adamw_sharded(own["w_out"], recv["w_out"], w_out[0], m_w_out[0], v_w_out[0])
    wait_for(pending[-1], big["w_out"][0])
    big["w_in"] = [a[0:r_in].T for a in adamw_sharded(own["w_int"], recv["w_int"], padt(w_in[0]), padt(m_w_in[0]),
                                                       padt(v_w_in[0]))]
    big["w_mod"] = [g_wmod, d_wmod, nm_wmod, nv_wmod]

    row = lambda r, w: tot[r:r + 1, 0:w]
    gn_row = tot[ROW_GN:ROW_GN + 1, 0:DC]
    g_small = {
        "b_mod": (tot[ROW_DMOD:ROW_DMOD + 6] + tot[ROW_DMODC:ROW_DMODC + 6]).reshape(1, 6 * D),
        "norm1_g": row(ROW_N1, D), "norm2_g": row(ROW_N2, D),
        "conv_w": lax.dynamic_slice(tot, (ROW_CW, me * (DC // N_DEV)), (CW, DC // N_DEV)),
        "conv_b": row(ROW_CB, DC), "conv_ln_g": row(ROW_LG, DC), "conv_ln_b": row(ROW_LB, DC),
        "w_a2_f": lax.dynamic_slice(tot, (ROW_WA, me * (DK // N_DEV)), (RANK, DK // N_DEV)),
        "b_a_f": tot[ROW_BA:ROW_BA + 1, 0:DK],
        "w_a2_b": lax.dynamic_slice(tot, (ROW_WA + RANK, DK + me * (DK // N_DEV)), (RANK, DK // N_DEV)),
        "b_a_b": tot[ROW_BA:ROW_BA + 1, DK:2 * DK],
        "gla_norm_g": gn_row[:, 0:HV] + gn_row[:, HV:2 * HV] + gn_row[:, 2 * HV:3 * HV] + gn_row[:, 3 * HV:4 * HV],
        "final_g": row(ROW_FG, D),
    }
    wmv = {
        "b_mod": (b_mod, m_b_mod, v_b_mod), "norm1_g": (norm1_g, m_norm1_g, v_norm1_g),
        "norm2_g": (norm2_g, m_norm2_g, v_norm2_g), "conv_w": (conv_w[0], m_conv_w[0], v_conv_w[0]),
        "conv_b": (conv_b, m_conv_b, v_conv_b), "conv_ln_g": (conv_ln_g, m_conv_ln_g, v_conv_ln_g),
        "conv_ln_b": (conv_ln_b, m_conv_ln_b, v_conv_ln_b), "w_a2_f": (w_a2_f[0], m_w_a2_f[0], v_w_a2_f[0]),
        "b_a_f": (b_a_f, m_b_a_f, v_b_a_f), "w_a2_b": (w_a2_b[0], m_w_a2_b[0], v_w_a2_b[0]),
        "b_a_b": (b_a_b, m_b_a_b, v_b_a_b), "gla_norm_g": (gla_norm_g, m_gla_norm_g, v_gla_norm_g),
        "final_g": (final_g.reshape(1, D), m_final_g.reshape(1, D), v_final_g.reshape(1, D)),
    }
    names_small = list(g_small)
    upd = adamw_small([(g_small[n],) + wmv[n] for n in names_small])
    res = {n: (g_small[n],) + upd[i] for i, n in enumerate(names_small)}
    res["c_ctx"] = (g_cc, d_cc, nm_cc, nv_cc)
    for n in ("w_mod", "w_in", "w_out", "w_gate", "w_up", "w_down"):
        res[n] = tuple(big[n])

    order = ["c_ctx", "w_mod", "b_mod", "norm1_g", "norm2_g", "w_in", "conv_w", "conv_b", "conv_ln_g", "conv_ln_b",
             "w_a2_f", "b_a_f", "w_a2_b", "b_a_b", "gla_norm_g", "w_out", "w_gate", "w_up", "w_down", "final_g"]
    shapes = {"c_ctx": c_ctx.shape, "w_mod": w_mod.shape, "b_mod": b_mod.shape, "norm1_g": norm1_g.shape,
              "norm2_g": norm2_g.shape, "w_in": w_in.shape, "conv_w": conv_w.shape, "conv_b": conv_b.shape,
              "conv_ln_g": conv_ln_g.shape, "conv_ln_b": conv_ln_b.shape, "w_a2_f": w_a2_f.shape,
              "b_a_f": b_a_f.shape, "w_a2_b": w_a2_b.shape, "b_a_b": b_a_b.shape, "gla_norm_g": gla_norm_g.shape,
              "w_out": w_out.shape, "w_gate": w_gate.shape, "w_up": w_up.shape, "w_down": w_down.shape,
              "final_g": final_g.shape}
    outs = [loss, gx.reshape(x.shape)]
    for i in range(4):
        outs += [res[n][i].reshape(shapes[n]) for n in order]
    return tuple(outs)
```

```python
import functools

import jax
import jax.numpy as jnp
from jax import lax
from jax.experimental import pallas as pl
from jax.experimental.pallas import tpu as pltpu

F32 = jnp.float32
MXU_DTYPE = jnp.bfloat16
WIRE_DTYPE = jnp.bfloat16
HI = lax.Precision.HIGHEST
MESH = pl.DeviceIdType.MESH

N_DEV = 8
D = 1024
DC = 512
NH = 4
HK = 64
HV = 128
DK = NH * HK
DV = NH * HV
RANK = 16
CHUNK = 64
SEG = 64
CW = 31
CPAD = 15
DFF = 2816
DIN = 2592
DINP = 2688
TAU = 16.0
EPS = 1e-6
VMEM_LIMIT = 56 * 1024 * 1024

TOKEN_TILE = 512
CTX_TOKEN_TILE = 256
GLA_CHUNKS = 16
CTX_GLA_CHUNKS = 4
GRAD_TOKEN_BLOCK = 2048

ADAM_LR = 0.001
ADAM_B1 = 0.9
ADAM_B2 = 0.999
ADAM_EPS = 1e-08
ADAM_WD = 0.01
ADAM_STEP = 10


def _mm(a, b):
    return jnp.dot(a.astype(MXU_DTYPE), b.astype(MXU_DTYPE), preferred_element_type=F32)


def _mm_nt(a, b):
    return lax.dot_general(a.astype(MXU_DTYPE), b.astype(MXU_DTYPE), (((1,), (1,)), ((), ())),
                           preferred_element_type=F32)


def _mm_tn(a, b):
    return lax.dot_general(a.astype(MXU_DTYPE), b.astype(MXU_DTYPE), (((0,), (0,)), ((), ())),
                           preferred_element_type=F32)


def _hi(a, b):
    return jnp.dot(a, b, precision=HI, preferred_element_type=F32)


def _hi_nt(a, b):
    return lax.dot_general(a, b, (((1,), (1,)), ((), ())), precision=HI, preferred_element_type=F32)


def _hi_tn(a, b):
    return lax.dot_general(a, b, (((0,), (0,)), ((), ())), precision=HI, preferred_element_type=F32)


def _sigmoid(x):
    return 1.0 / (1.0 + jnp.exp(-x))


def _cparams(n_axes):
    return pltpu.CompilerParams(dimension_semantics=("arbitrary",) * n_axes, vmem_limit_bytes=VMEM_LIMIT)


def _full(shape):
    n = len(shape)
    return pl.BlockSpec(shape, lambda *_: (0,) * n)


def _rows(tt, width):
    return pl.BlockSpec((tt, width), lambda i: (i, 0))


def _sds(shape, dtype=F32):
    return jax.ShapeDtypeStruct(shape, dtype)


def _norm_mod(x, g, sh, sc):
    r = lax.rsqrt(jnp.mean(x * x, axis=-1, keepdims=True) + EPS)
    xn = x * r
    yy = xn * g
    return r, xn, yy, yy * (1.0 + sc) + sh


def _norm_mod_bwd(dh, r, xn, yy, g, sc):
    dsh = jnp.sum(dh, axis=0, keepdims=True)
    dsc = jnp.sum(dh * yy, axis=0, keepdims=True)
    dy = dh * (1.0 + sc)
    dg = jnp.sum(dy * xn, axis=0, keepdims=True)
    dxn = dy * g
    dx = r * (dxn - xn * jnp.mean(dxn * xn, axis=-1, keepdims=True))
    return dsh, dsc, dg, dx


def _zero_first(*refs):
    @pl.when(pl.program_id(0) == 0)
    def _():
        for r in refs:
            r[...] = jnp.zeros_like(r)


def _acc_rows(ref, rows):
    ref[...] += jnp.concatenate(rows + [jnp.zeros((8 - len(rows), rows[0].shape[1]), F32)], axis=0)


def proj_fwd(x, vec, w_int, wa, ba, tt):
    t = x.shape[0]

    def body(x_ref, vec_ref, w_ref, wa_ref, ba_ref, u_ref, q_ref, k_ref, v_ref, g_ref, r_ref, la_ref, h_ref):
        _, _, _, h = _norm_mod(x_ref[...], vec_ref[0:1, :], vec_ref[1:2, :], vec_ref[2:3, :])
        hb = h.astype(MXU_DTYPE)
        h_ref[...] = hb
        p = _mm_nt(hb, w_ref[...])
        u_ref[...] = p[:, 0:1024]
        q_ref[...] = p[:, 1024:1280]
        k_ref[...] = p[:, 1280:1536]
        v_ref[...] = p[:, 1536:2048]
        g_ref[...] = p[:, 2048:2560]
        rr = p[:, 2560:2688]
        r_ref[...] = rr
        z = _mm(rr, wa_ref[...]) + ba_ref[...]
        la_ref[...] = (jnp.minimum(z, 0.0) - jnp.log(1.0 + jnp.exp(-jnp.abs(z)))) * (1.0 / TAU)

    return pl.pallas_call(
        body, name="proj_fwd", grid=(t // tt,),
        in_specs=[_rows(tt, D), _full((8, D)), _full((DINP, D)), _full((128, 512)), _full((1, 512))],
        out_specs=[_rows(tt, 1024), _rows(tt, DK), _rows(tt, DK), _rows(tt, DV), _rows(tt, DV), _rows(tt, 128),
                   _rows(tt, 512), _rows(tt, D)],
        out_shape=[_sds((t, 1024)), _sds((t, DK)), _sds((t, DK)), _sds((t, DV)), _sds((t, DV)), _sds((t, 128)),
                   _sds((t, 512)), _sds((t, D), MXU_DTYPE)],
        compiler_params=_cparams(1),
    )(x, vec, w_int, wa, ba)


def proj_bwd(du, dqkv, dg, dla_f, dla_b, la, r, x, dx1, vec, w_int, wa, tt):
    t = x.shape[0]

    def body(du_ref, dqf_ref, dqb_ref, dkf_ref, dkb_ref, dvf_ref, dvb_ref, dg_ref, dlaf_ref, dlab_ref, la_ref, r_ref,
             x_ref, dx1_ref, vec_ref, w_ref, wa_ref, gx_ref, dp_ref, acc_ref, dba_ref, dwa_ref):
        _zero_first(acc_ref, dba_ref, dwa_ref)
        md = lambda a: a.astype(MXU_DTYPE)
        g, sc = vec_ref[0:1, :], vec_ref[2:3, :]
        sums, dba, dwa = None, None, None
        for rows in (slice(0, tt // 2), slice(tt // 2, tt)):
            both = lambda a_ref, b_ref: md(a_ref[rows, :].astype(F32) + b_ref[rows, :].astype(F32))
            dla = jnp.concatenate([dlaf_ref[rows, :], dlab_ref[rows, :]], axis=1)
            dz = dla * (1.0 - jnp.exp(TAU * la_ref[rows, :])) * (1.0 / TAU)
            rr = r_ref[rows, :]
            dba_h = jnp.sum(dz, axis=0, keepdims=True)
            dwa_h = _mm_tn(rr, dz)
            dr = _mm_nt(dz, wa_ref[...])
            dp = jnp.concatenate([du_ref[rows, :], both(dqf_ref, dqb_ref), both(dkf_ref, dkb_ref),
                                  both(dvf_ref, dvb_ref), dg_ref[rows, :], md(dr)], axis=1)
            dp_ref[rows, :] = dp
            dh = _mm(dp, w_ref[...])
            rn, xn, yy, _ = _norm_mod(x_ref[rows, :], g, vec_ref[1:2, :], sc)
            dsh, dsc, dgn, dx = _norm_mod_bwd(dh, rn, xn, yy, g, sc)
            gx_ref[rows, :] = dx1_ref[rows, :] + dx
            part = [dsh, dsc, dgn]
            sums = part if sums is None else [a + b for a, b in zip(sums, part)]
            dba = dba_h if dba is None else dba + dba_h
            dwa = dwa_h if dwa is None else dwa + dwa_h
        _acc_rows(dba_ref, [dba])
        dwa_ref[...] += dwa
        _acc_rows(acc_ref, sums)

    return pl.pallas_call(
        body, name="proj_bwd", grid=(t // tt,),
        in_specs=[_rows(tt, 1024), _rows(tt, DK), _rows(tt, DK), _rows(tt, DK), _rows(tt, DK), _rows(tt, DV),
                  _rows(tt, DV), _rows(tt, DV), _rows(tt, DK), _rows(tt, DK), _rows(tt, 512),
                  _rows(tt, 128), _rows(tt, D), _rows(tt, D), _full((8, D)), _full((DINP, D)), _full((128, 512))],
        out_specs=[_rows(tt, D), _rows(tt, DINP), _full((8, D)), _full((8, 512)), _full((128, 512))],
        out_shape=[_sds((t, D)), _sds((t, DINP), MXU_DTYPE), _sds((8, D)), _sds((8, 512)), _sds((128, 512))],
        compiler_params=_cparams(1),
    )(du, *dqkv, dg, dla_f, dla_b, la, r, x, dx1, vec, w_int, wa)


def _dot_exact01(m01, x):
    bf = jnp.bfloat16
    w = x.shape[1]
    hi = x.astype(bf)
    r1 = x - hi.astype(F32)
    mid = r1.astype(bf)
    lo = (r1 - mid.astype(F32)).astype(bf)
    y = jnp.dot(m01.astype(bf), jnp.concatenate([hi, mid, lo], axis=1), preferred_element_type=F32)
    return y[:, 0:w] + y[:, w:2 * w] + y[:, 2 * w:3 * w]


def _gla_chunk(d, qc, kc, la_c):
    row = lax.broadcasted_iota(jnp.int32, (CHUNK, CHUNK), 0)
    col = lax.broadcasted_iota(jnp.int32, (CHUNK, CHUNK), 1)
    cum = ((col <= row) if d == 0 else (col >= row)).astype(F32)
    cum_t = ((col >= row) if d == 0 else (col <= row)).astype(F32)
    cum4 = jnp.concatenate([cum] * NH, axis=0)
    head_of_lane = lax.broadcasted_iota(jnp.int32, (1, DK), 1) // HK
    b = _dot_exact01(cum, la_c)
    bl = jnp.sum(la_c, axis=0, keepdims=True)
    eb = jnp.exp(b)
    enb = jnp.exp(-b)
    ekd = jnp.exp(bl - b)
    qt = qc * (HK ** -0.5) * eb
    kt = kc * enb
    kd = kc * ekd
    qst = jnp.concatenate([jnp.where(head_of_lane == h, qt, 0.0) for h in range(NH)], axis=0)
    a = _mm_nt(qst, kt) * cum4
    return cum_t, cum4, head_of_lane, eb, enb, ekd, qt, kt, kd, qst, a, jnp.exp(bl)


NPAIR = NH // 2


def _pair_rows(x, p):
    return x[2 * p * CHUNK:(2 * p + 2) * CHUNK]


def _pair_lanes(x, p):
    return x[:, 2 * p * HK:(2 * p + 2) * HK]


def _pair_fold(r):
    half = lax.broadcasted_iota(jnp.int32, (1, 2 * HK), 1) // HK
    return jnp.where(half == 0, r[0:CHUNK], 0.0) + jnp.where(half == 1, r[CHUNK:2 * CHUNK], 0.0)


def gla_fwd(q, k, v, la, s0, cb):
    t = q.shape[0]
    nc = t // CHUNK
    nb = nc // cb

    def body(qf_ref, kf_ref, vf_ref, laf_ref, qb_ref, kb_ref, vb_ref, lab_ref, s0_ref,
             of_ref, ob_ref, sf_ref, sb_ref, sfin_ref, s_scr):
        i = pl.program_id(0)

        @pl.when(i == 0)
        def _():
            s_scr[...] = s0_ref[...]

        def chunk(d, jj, q_ref, k_ref, v_ref, la_ref, o_ref, sall_ref):
            rows = slice(jj * CHUNK, (jj + 1) * CHUNK)
            vc = v_ref[rows, :]
            _, _, head_of_lane, _, _, _, _, _, kd, qst, a, dec = _gla_chunk(
                d, q_ref[rows, :], k_ref[rows, :], la_ref[rows, :])
            s = s_scr[d]
            sall_ref[jj] = s
            inter = _mm_nt(qst, s)
            outs = []
            for h in range(NH):
                hs = slice(h * CHUNK, (h + 1) * CHUNK)
                outs.append(_mm(a[hs], vc[:, h * HV:(h + 1) * HV]) + inter[hs])
            o_ref[rows, :] = jnp.concatenate(outs, axis=1)
            kv = _mm_tn(vc, kd)
            s_new = dec * s
            for h in range(NH):
                s_new = s_new + jnp.where(head_of_lane == h, kv[h * HV:(h + 1) * HV], 0.0)
            s_scr[d] = s_new

        for j in range(cb):
            chunk(0, j, qf_ref, kf_ref, vf_ref, laf_ref, of_ref, sf_ref)
            chunk(1, cb - 1 - j, qb_ref, kb_ref, vb_ref, lab_ref, ob_ref, sb_ref)

        @pl.when(i == nb - 1)
        def _():
            sfin_ref[...] = s_scr[...]

    tb = cb * CHUNK
    fwd = lambda w, c=0: pl.BlockSpec((tb, w), lambda i: (i, c))
    bwd = lambda w, c=0: pl.BlockSpec((tb, w), lambda i: (nb - 1 - i, c))
    return pl.pallas_call(
        body, name="gla_fwd", grid=(nb,),
        in_specs=[fwd(DK), fwd(DK), fwd(DV), fwd(DK, 0), bwd(DK), bwd(DK), bwd(DV), bwd(DK, 1), _full((2, HV, DK))],
        out_specs=[fwd(DV), bwd(DV), pl.BlockSpec((cb, HV, DK), lambda i: (i, 0, 0)),
                   pl.BlockSpec((cb, HV, DK), lambda i: (nb - 1 - i, 0, 0)), _full((2, HV, DK))],
        out_shape=[_sds((t, DV)), _sds((t, DV)), _sds((nc, HV, DK)), _sds((nc, HV, DK)), _sds((2, HV, DK))],
        scratch_shapes=[pltpu.VMEM((2, HV, DK), F32)],
        compiler_params=_cparams(1),
    )(q, k, v, la, q, k, v, la, s0)


def gla_bwd(q, k, v, la, do, sall_f, sall_b, dsfin, cb):
    t = q.shape[0]
    nc = t // CHUNK
    nb = nc // cb

    def body(qf_ref, kf_ref, vf_ref, laf_ref, dof_ref, sf_ref, qb_ref, kb_ref, vb_ref, lab_ref, dob_ref, sb_ref,
             dsfin_ref, dqf_ref, dkf_ref, dvf_ref, dlaf_ref, dqb_ref, dkb_ref, dvb_ref, dlab_ref, ds0_ref, ds_scr):
        i = pl.program_id(0)

        @pl.when(i == 0)
        def _():
            ds_scr[...] = dsfin_ref[...]

        def chunk(d, jj, q_ref, k_ref, v_ref, la_ref, do_ref, sall_ref, dq_ref, dk_ref, dv_ref, dla_ref):
            rows = slice(jj * CHUNK, (jj + 1) * CHUNK)
            vc = v_ref[rows, :]
            doc = do_ref[rows, :]
            cum_t, cum4, head_of_lane, eb, enb, ekd, qt, kt, kd, qst, a, dec = _gla_chunk(
                d, q_ref[rows, :], k_ref[rows, :], la_ref[rows, :])
            s = sall_ref[jj]
            ds = ds_scr[d]
            hv = lambda x, h: x[:, h * HV:(h + 1) * HV]
            hr = lambda x, h: x[h * CHUNK:(h + 1) * CHUNK]
            fold = lambda x: functools.reduce(
                lambda p, c: p + c, [jnp.where(head_of_lane == h, hr(x, h), 0.0) for h in range(NH)])
            dost = jnp.concatenate([hv(doc, h) for h in range(NH)], axis=0)
            vst = jnp.concatenate([hv(vc, h) for h in range(NH)], axis=0)
            da = jnp.concatenate([_mm_nt(hv(doc, h), hv(vc, h)) for h in range(NH)], axis=0) * cum4
            pairs = range(NPAIR)
            dqt = fold(_mm(da, kt)) + jnp.concatenate(
                [_pair_fold(_mm(_pair_rows(dost, p), _pair_lanes(s, p))) for p in pairs], axis=1)
            dkt = _mm_tn(da, qst)
            kdst = jnp.concatenate([jnp.where(head_of_lane == h, kd, 0.0) for h in range(NH)], axis=0)
            dv_inter = jnp.concatenate(
                [_mm_nt(_pair_lanes(_pair_rows(kdst, p), p), _pair_lanes(ds, p)) for p in pairs], axis=0)
            dv_ref[rows, :] = jnp.concatenate(
                [_mm_tn(hr(a, h), hv(doc, h)) + hr(dv_inter, h) for h in range(NH)], axis=1).astype(MXU_DTYPE)
            dkd = jnp.concatenate([_pair_fold(_mm(_pair_rows(vst, p), _pair_lanes(ds, p))) for p in pairs], axis=1)
            ds_scr[d] = dec * ds + jnp.concatenate(
                [_mm_tn(_pair_rows(dost, p), _pair_lanes(_pair_rows(qst, p), p)) for p in pairs], axis=1)
            tkd = dkd * kd
            db = dqt * qt - dkt * kt - tkd
            dbl = jnp.sum(ds * s, axis=0, keepdims=True) * dec + jnp.sum(tkd, axis=0, keepdims=True)
            dla_ref[rows, :] = _dot_exact01(cum_t, db) + dbl
            dq_ref[rows, :] = (dqt * eb * (HK ** -0.5)).astype(MXU_DTYPE)
            dk_ref[rows, :] = (dkt * enb + dkd * ekd).astype(MXU_DTYPE)

        for j in range(cb):
            chunk(0, cb - 1 - j, qf_ref, kf_ref, vf_ref, laf_ref, dof_ref, sf_ref, dqf_ref, dkf_ref, dvf_ref, dlaf_ref)
            chunk(1, j, qb_ref, kb_ref, vb_ref, lab_ref, dob_ref, sb_ref, dqb_ref, dkb_ref, dvb_ref, dlab_ref)

        @pl.when(i == nb - 1)
        def _():
            ds0_ref[...] = ds_scr[...]

    tb = cb * CHUNK
    rev = lambda w, c=0: pl.BlockSpec((tb, w), lambda i: (nb - 1 - i, c))
    fro = lambda w, c=0: pl.BlockSpec((tb, w), lambda i: (i, c))
    st_rev = pl.BlockSpec((cb, HV, DK), lambda i: (nb - 1 - i, 0, 0))
    st_fro = pl.BlockSpec((cb, HV, DK), lambda i: (i, 0, 0))
    md = MXU_DTYPE
    return pl.pallas_call(
        body, name="gla_bwd", grid=(nb,),
        in_specs=[rev(DK), rev(DK), rev(DV), rev(DK, 0), rev(DV), st_rev,
                  fro(DK), fro(DK), fro(DV), fro(DK, 1), fro(DV), st_fro, _full((2, HV, DK))],
        out_specs=[rev(DK), rev(DK), rev(DV), rev(DK), fro(DK), fro(DK), fro(DV), fro(DK), _full((2, HV, DK))],
        out_shape=[_sds((t, DK), md), _sds((t, DK), md), _sds((t, DV), md), _sds((t, DK)),
                   _sds((t, DK), md), _sds((t, DK), md), _sds((t, DV), md), _sds((t, DK)), _sds((2, HV, DK))],
        scratch_shapes=[pltpu.VMEM((2, HV, DK), F32)],
        compiler_params=_cparams(1),
    )(q, k, v, la, do, sall_f, q, k, v, la, do, sall_b, dsfin)


def _seg_pos(tt):
    return lax.broadcasted_iota(jnp.int32, (tt, 1), 0) % SEG


def _shifted(x, s, pos, tt):
    y = x if s == 0 else pltpu.roll(x, (-s) % tt, 0)
    return jnp.where((pos + s >= 0) & (pos + s < SEG), y, 0.0)


def _head_norm(o, gn):
    rs, xs = [], []
    for h in range(NH):
        oh = o[:, h * HV:(h + 1) * HV]
        r = lax.rsqrt(jnp.mean(oh * oh, axis=-1, keepdims=True) + EPS)
        rs.append(r)
        xs.append(oh * r)
    return rs, xs


def merge_fwd(u, g, o_f, o_b, x, vec, vc, convw, w_out, tt):
    t = x.shape[0]

    def body(u_ref, g_ref, of_ref, ob_ref, x_ref, vec_ref, vc_ref, cw_ref, w_ref, x1_ref, cat_ref, mix_ref, yc_ref,
             h2_ref):
        a = u_ref[:, 0:DC]
        gate = u_ref[:, DC:2 * DC]
        vv = a * _sigmoid(gate)
        pos = _seg_pos(tt)
        cw = cw_ref[...]
        yc = jnp.zeros((tt, DC), F32) + vc_ref[1:2, :]
        for j in range(CW):
            yc = yc + _shifted(vv, j - CPAD, pos, tt) * cw[j:j + 1, :]
        yc_ref[...] = yc
        mu = jnp.mean(yc, axis=-1, keepdims=True)
        yd = yc - mu
        rs = lax.rsqrt(jnp.mean(yd * yd, axis=-1, keepdims=True) + EPS)
        ln = yd * rs * vc_ref[2:3, :] + vc_ref[3:4, :]
        conv_o = ln * _sigmoid(ln)
        o = of_ref[...] + ob_ref[...]
        _, xs = _head_norm(o, None)
        gg = g_ref[...]
        o2g = jnp.concatenate(xs, axis=1) * vc_ref[0:1, :] * (gg * _sigmoid(gg))
        cat = jnp.concatenate([conv_o, o2g], axis=1).astype(MXU_DTYPE)
        cat_ref[...] = cat
        mix = _mm(cat, w_ref[...])
        mix_ref[...] = mix
        x1 = x_ref[...] + vec_ref[0:1, :] * mix
        x1_ref[...] = x1
        _, _, _, h2 = _norm_mod(x1, vec_ref[1:2, :], vec_ref[2:3, :], vec_ref[3:4, :])
        h2_ref[...] = h2.astype(MXU_DTYPE)

    return pl.pallas_call(
        body, name="merge_fwd", grid=(t // tt,),
        in_specs=[_rows(tt, 1024), _rows(tt, DV), _rows(tt, DV), _rows(tt, DV), _rows(tt, D),
                  _full((8, D)), _full((8, DC)), _full((32, DC)), _full((D, D))],
        out_specs=[_rows(tt, D), _rows(tt, D), _rows(tt, D), _rows(tt, DC), _rows(tt, D)],
        out_shape=[_sds((t, D)), _sds((t, D), MXU_DTYPE), _sds((t, D)), _sds((t, DC)), _sds((t, D), MXU_DTYPE)],
        compiler_params=_cparams(1),
    )(u, g, o_f, o_b, x, vec, vc, convw, w_out)


def merge_bwd(dx1, mix, u, g, o_f, o_b, yc, vec, vc, convw, w_out, tt):
    t = dx1.shape[0]

    def body(dx1_ref, mix_ref, u_ref, g_ref, of_ref, ob_ref, yc_ref, vec_ref, vc_ref, cw_ref, w_ref,
             du_ref, dg_ref, do_ref, dmix_ref, acc1_ref, acc2_ref, dcw_ref):
        _zero_first(acc1_ref, acc2_ref, dcw_ref)
        dx1v = dx1_ref[...]
        dg1 = jnp.sum(dx1v * mix_ref[...], axis=0, keepdims=True)
        dmix = (vec_ref[0:1, :] * dx1v).astype(MXU_DTYPE)
        dmix_ref[...] = dmix
        dcat = _mm_nt(dmix, w_ref[...])
        dconv_o = dcat[:, 0:DC]
        do2 = dcat[:, DC:2 * DC]
        gn = vc_ref[0:1, :]
        o = of_ref[...] + ob_ref[...]
        rs, xs = _head_norm(o, None)
        xn = jnp.concatenate(xs, axis=1)
        gg = g_ref[...]
        sg = _sigmoid(gg)
        don = do2 * (gg * sg)
        dg_ref[...] = (do2 * (xn * gn) * (sg * (1.0 + gg * (1.0 - sg)))).astype(MXU_DTYPE)
        dgn = jnp.sum(don * xn, axis=0, keepdims=True)
        dxn = don * gn
        dos = []
        for h in range(NH):
            dh = dxn[:, h * HV:(h + 1) * HV]
            dos.append(rs[h] * (dh - xs[h] * jnp.mean(dh * xs[h], axis=-1, keepdims=True)))
        do_ref[...] = jnp.concatenate(dos, axis=1).astype(MXU_DTYPE)
        yc = yc_ref[...]
        mu = jnp.mean(yc, axis=-1, keepdims=True)
        yd = yc - mu
        rstd = lax.rsqrt(jnp.mean(yd * yd, axis=-1, keepdims=True) + EPS)
        yhat = yd * rstd
        lg = vc_ref[2:3, :]
        ln = yhat * lg + vc_ref[3:4, :]
        sl = _sigmoid(ln)
        dln = dconv_o * (sl * (1.0 + ln * (1.0 - sl)))
        dlb = jnp.sum(dln, axis=0, keepdims=True)
        dlg = jnp.sum(dln * yhat, axis=0, keepdims=True)
        dyh = dln * lg
        dyc = rstd * (dyh - jnp.mean(dyh, axis=-1, keepdims=True)
                      - yhat * jnp.mean(dyh * yhat, axis=-1, keepdims=True))
        dcb = jnp.sum(dyc, axis=0, keepdims=True)
        a = u_ref[:, 0:DC]
        gate = u_ref[:, DC:2 * DC]
        sgt = _sigmoid(gate)
        vv = a * sgt
        pos = _seg_pos(tt)
        cw = cw_ref[...]
        dvv = jnp.zeros((tt, DC), F32)
        dws = []
        for j in range(CW):
            shifted_dyc = _shifted(dyc, CPAD - j, pos, tt)
            dvv = dvv + shifted_dyc * cw[j:j + 1, :]
            dws.append(jnp.sum(shifted_dyc * vv, axis=0, keepdims=True))
        dws.append(jnp.zeros((1, DC), F32))
        du_ref[:, 0:DC] = (dvv * sgt).astype(MXU_DTYPE)
        du_ref[:, DC:2 * DC] = (dvv * a * sgt * (1.0 - sgt)).astype(MXU_DTYPE)
        _acc_rows(acc1_ref, [dg1])
        _acc_rows(acc2_ref, [dgn, dcb, dlg, dlb])
        dcw_ref[...] += jnp.concatenate(dws, axis=0)

    return pl.pallas_call(
        body, name="merge_bwd", grid=(t // tt,),
        in_specs=[_rows(tt, D), _rows(tt, D), _rows(tt, 1024), _rows(tt, DV), _rows(tt, DV), _rows(tt, DV),
                  _rows(tt, DC),
                  _full((8, D)), _full((8, DC)), _full((32, DC)), _full((D, D))],
        out_specs=[_rows(tt, 1024), _rows(tt, DV), _rows(tt, DV), _rows(tt, D), _full((8, D)), _full((8, DC)),
                   _full((32, DC))],
        out_shape=[_sds((t, 1024), MXU_DTYPE), _sds((t, DV), MXU_DTYPE), _sds((t, DV), MXU_DTYPE),
                   _sds((t, D), MXU_DTYPE), _sds((8, D)),
                   _sds((8, DC)), _sds((32, DC))],
        compiler_params=_cparams(1),
    )(dx1, mix, u, g, o_f, o_b, yc, vec, vc, convw, w_out)


FN = DFF // 2


def ffn_gate_up(h2, wg_t, wu_t, tt):
    t = h2.shape[0]

    def body(h2_ref, wg_ref, wu_ref, s_ref, d_ref, hid_ref):
        h2v = h2_ref[...]
        gt = _mm_nt(h2v, wg_ref[...])
        up = _mm_nt(h2v, wu_ref[...])
        sg = _sigmoid(gt)
        act = gt * sg
        s_ref[...] = act.astype(MXU_DTYPE)
        d_ref[...] = (up * (sg * (1.0 + gt * (1.0 - sg)))).astype(MXU_DTYPE)
        hid_ref[...] = (act * up).astype(MXU_DTYPE)

    blk = pl.BlockSpec((tt, FN), lambda j, i: (i, j))
    wblk = pl.BlockSpec((FN, D), lambda j, i: (j, 0))
    return pl.pallas_call(
        body, name="ffn_gate_up", grid=(2, t // tt),
        in_specs=[pl.BlockSpec((tt, D), lambda j, i: (i, 0)), wblk, wblk],
        out_specs=[blk, blk, blk],
        out_shape=[_sds((t, DFF), MXU_DTYPE)] * 3,
        compiler_params=_cparams(2),
    )(h2, wg_t, wu_t)


def ffn_down_loss(hid, x1, tgt, vec, w_down, tt):
    t = x1.shape[0]

    def body(hid_ref, x1_ref, tgt_ref, vec_ref, w_ref, dx2_ref, dff_ref, acc_ref):
        _zero_first(acc_ref)
        g2 = vec_ref[0:1, :]
        fg = vec_ref[1:2, :]
        ff = _mm(hid_ref[...], w_ref[...])
        x2 = x1_ref[...] + g2 * ff
        rf = lax.rsqrt(jnp.mean(x2 * x2, axis=-1, keepdims=True) + EPS)
        xn = x2 * rf
        err = xn * fg - tgt_ref[...]
        dy = err * (1.0 / D)
        dfg = jnp.sum(dy * xn, axis=0, keepdims=True)
        dxn = dy * fg
        dx2 = rf * (dxn - xn * jnp.mean(dxn * xn, axis=-1, keepdims=True))
        dx2_ref[...] = dx2
        dff_ref[...] = (g2 * dx2).astype(MXU_DTYPE)
        dg2 = jnp.sum(dx2 * ff, axis=0, keepdims=True)
        loss = jnp.sum(err * err, axis=0, keepdims=True) * (0.5 / D)
        _acc_rows(acc_ref, [dg2, dfg, loss])

    return pl.pallas_call(
        body, name="ffn_down_loss", grid=(t // tt,),
        in_specs=[_rows(tt, DFF), _rows(tt, D), _rows(tt, D), _full((8, D)), _full((DFF, D))],
        out_specs=[_rows(tt, D), _rows(tt, D), _full((8, D))],
        out_shape=[_sds((t, D)), _sds((t, D), MXU_DTYPE), _sds((8, D))],
        compiler_params=_cparams(1),
    )(hid, x1, tgt, vec, w_down)


def ffn_dhid(dff, s, d, w_down, tt):
    t = dff.shape[0]

    def body(dff_ref, s_ref, d_ref, w_ref, dgt_ref, dup_ref):
        dhid = _mm_nt(dff_ref[...], w_ref[...])
        dgt_ref[...] = (dhid * d_ref[...].astype(F32)).astype(MXU_DTYPE)
        dup_ref[...] = (dhid * s_ref[...].astype(F32)).astype(MXU_DTYPE)

    blk = pl.BlockSpec((tt, FN), lambda j, i: (i, j))
    return pl.pallas_call(
        body, name="ffn_dhid", grid=(2, t // tt),
        in_specs=[pl.BlockSpec((tt, D), lambda j, i: (i, 0)), blk, blk, pl.BlockSpec((FN, D), lambda j, i: (j, 0))],
        out_specs=[blk, blk],
        out_shape=[_sds((t, DFF), MXU_DTYPE), _sds((t, DFF), MXU_DTYPE)],
        compiler_params=_cparams(2),
    )(dff, s, d, w_down)


def ffn_dh2(dgt, dup, x1, dx2, vec, wg_t, wu_t, tt):
    t = x1.shape[0]

    def body(dgt_ref, dup_ref, x1_ref, dx2_ref, vec_ref, wg_ref, wu_ref, dx1_ref, acc_ref):
        _zero_first(acc_ref)
        dh2 = _mm(dgt_ref[...], wg_ref[...]) + _mm(dup_ref[...], wu_ref[...])
        g, sc = vec_ref[0:1, :], vec_ref[2:3, :]
        r, xn, yy, _ = _norm_mod(x1_ref[...], g, vec_ref[1:2, :], sc)
        dsh, dsc, dgn, dx = _norm_mod_bwd(dh2, r, xn, yy, g, sc)
        dx1_ref[...] = dx2_ref[...] + dx
        _acc_rows(acc_ref, [dsh, dsc, dgn])

    return pl.pallas_call(
        body, name="ffn_dh2", grid=(t // tt,),
        in_specs=[_rows(tt, DFF), _rows(tt, DFF), _rows(tt, D), _rows(tt, D), _full((8, D)), _full((DFF, D)),
                  _full((DFF, D))],
        out_specs=[_rows(tt, D), _full((8, D))],
        out_shape=[_sds((t, D)), _sds((8, D))],
        compiler_params=_cparams(1),
    )(dgt, dup, x1, dx2, vec, wg_t, wu_t)


def tn_matmul(a, b, bm, bt, init=None):
    t, m = a.shape
    n = b.shape[1]
    nk = t // bt

    def body(*refs):
        if init is None:
            a_ref, b_ref, o_ref, wire_ref = refs
        else:
            a_ref, b_ref, i_ref, o_ref, wire_ref = refs
        @pl.when(pl.program_id(1) == 0)
        def _():
            o_ref[...] = jnp.zeros_like(o_ref) if init is None else i_ref[...]

        o_ref[...] += _mm_tn(a_ref[...], b_ref[...])

        @pl.when(pl.program_id(1) == nk - 1)
        def _():
            wire_ref[...] = o_ref[...].astype(WIRE_DTYPE)

    in_specs = [pl.BlockSpec((bt, bm), lambda i, k: (k, i)), pl.BlockSpec((bt, n), lambda i, k: (k, 0))]
    args = [a, b]
    if init is not None:
        in_specs.append(pl.BlockSpec((bm, n), lambda i, k: (i, 0)))
        args.append(init)
    oblk = pl.BlockSpec((bm, n), lambda i, k: (i, 0))
    return pl.pallas_call(
        body, name="tn_matmul", grid=(m // bm, nk),
        in_specs=in_specs, out_specs=[oblk, oblk],
        out_shape=[_sds((m, n)), _sds((m, n), WIRE_DTYPE)], compiler_params=_cparams(2),
    )(*args)


def _adamw(w, g, m, v):
    m = ADAM_B1 * m + (1.0 - ADAM_B1) * g
    v = ADAM_B2 * v + (1.0 - ADAM_B2) * (g * g)
    m_hat = m / (1.0 - ADAM_B1 ** ADAM_STEP)
    v_hat = v / (1.0 - ADAM_B2 ** ADAM_STEP)
    delta = -ADAM_LR * (m_hat / (jnp.sqrt(v_hat) + ADAM_EPS) + ADAM_WD * w)
    return delta, m, v


def adamw_sharded(own, recv, w, m, v):
    shape = w.shape

    def body(own_ref, recv_ref, w_ref, m_ref, v_ref, g_ref, d_ref, mo_ref, vo_ref):
        g = own_ref[...]
        for k in range(N_DEV - 1):
            g = g + recv_ref[k].astype(F32)
        g_ref[...] = g
        d_ref[...], mo_ref[...], vo_ref[...] = _adamw(w_ref[...], g, m_ref[...], v_ref[...])

    return pl.pallas_call(
        body, name="adamw_sharded",
        in_specs=[_full(shape), _full((N_DEV - 1,) + shape), _full(shape), _full(shape), _full(shape)],
        out_specs=[_full(shape)] * 4, out_shape=[_sds(shape)] * 4, grid=(1,),
        compiler_params=_cparams(1),
    )(own, recv, w, m, v)


def adamw_small(items):
    n = len(items)
    flat = [a for it in items for a in it]

    def body(*refs):
        ins, outs = refs[:4 * n], refs[4 * n:]
        for i in range(n):
            g, w, m, v = (r[...] for r in ins[4 * i:4 * i + 4])
            outs[3 * i][...], outs[3 * i + 1][...], outs[3 * i + 2][...] = _adamw(w, g, m, v)

    out = pl.pallas_call(
        body, name="adamw_small", grid=(1,),
        in_specs=[_full(a.shape) for a in flat],
        out_specs=[_full(it[1].shape) for it in items for _ in range(3)],
        out_shape=[_sds(it[1].shape) for it in items for _ in range(3)],
        compiler_params=_cparams(1),
    )(*flat)
    return [tuple(out[3 * i:3 * i + 3]) for i in range(n)]


def _mesh_pos():
    x, y, c = lax.axis_index("x"), lax.axis_index("y"), lax.axis_index("c")
    me = 4 * x + 2 * y + c
    peers = []
    for k in range(1, N_DEV):
        peers.append(((1 - x) if (k >> 2) & 1 else x, (1 - y) if (k >> 1) & 1 else y, (1 - c) if k & 1 else c))
    return me, peers


def _all_gather_issue(buf, send_sems, recv_sems, me, peers):
    sends = []
    for k, peer in enumerate(peers):
        cp = pltpu.make_async_remote_copy(src_ref=buf.at[me], dst_ref=buf.at[me], send_sem=send_sems.at[k],
                                          recv_sem=recv_sems.at[k], device_id=peer, device_id_type=MESH)
        cp.start()
        sends.append(cp)
    return sends


def _all_gather_finish(buf, send_sems, recv_sems, me, peers, sends):
    for k, peer in enumerate(peers):
        src = jnp.bitwise_xor(me, k + 1)
        pltpu.make_async_remote_copy(src_ref=buf.at[src], dst_ref=buf.at[src], send_sem=send_sems.at[k],
                                     recv_sem=recv_sems.at[k], device_id=peer, device_id_type=MESH).wait_recv()
    for cp in sends:
        cp.wait_send()


def _all_gather(buf, send_sems, recv_sems, me, peers):
    _all_gather_finish(buf, send_sems, recv_sems, me, peers, _all_gather_issue(buf, send_sems, recv_sems, me, peers))


_VMEM = pl.BlockSpec(memory_space=pltpu.VMEM)
_ANY = pl.BlockSpec(memory_space=pl.ANY)
_SEMS = pltpu.SemaphoreType.DMA((N_DEV - 1,))


def mod_forward(c, c_ctx, w_mod_sh, b_mod, sm_pack, w_first):
    ncol = w_mod_sh.shape[1]

    def body(c_ref, cc_ref, w_ref, b_ref, sm_ref, wf_ref, mod_ref, s_ref, smt_ref, wall_ref, cbuf, pbuf, smbuf,
             s1, r1, s2, r2, s3, r3, ws, wr, wl):
        me, peers = _mesh_pos()
        x, y, cc = lax.axis_index("x"), lax.axis_index("y"), lax.axis_index("c")
        sibling = (x, y, 1 - cc)
        chips = [(1 - x, y), (x, 1 - y), (1 - x, 1 - y)]
        slot = lambda px, py, pc: wall_ref.at[4 * px + 2 * py + pc]

        def wcopy(k, block, to, src=None):
            return pltpu.make_async_remote_copy(
                src_ref=slot(*block) if src is None else src, dst_ref=slot(*block), send_sem=ws.at[k],
                recv_sem=wr.at[k], device_id=to, device_id_type=MESH)

        cbuf[me] = jnp.broadcast_to(c_ref[...], (8, D))
        c_sends = _all_gather_issue(cbuf, s1, r1, me, peers)
        mine = pltpu.make_async_copy(wf_ref, slot(x, y, cc), wl)
        mine.start()
        first = [wcopy(0, (x, y, cc), sibling, src=wf_ref)]
        first += [wcopy(1 + j, (x, y, cc), (*chip, cc), src=wf_ref) for j, chip in enumerate(chips)]
        for cp in first:
            cp.start()
        smbuf[me] = sm_ref[...]
        sm_sends = _all_gather_issue(smbuf, s3, r3, me, peers)
        _all_gather_finish(cbuf, s1, r1, me, peers, c_sends)
        rows = [cbuf[j, 0:1, :] for j in range(N_DEV)] + [cc_ref[...], jnp.zeros((7, D), F32)]
        sx = jnp.concatenate(rows, axis=0)
        s = sx * _sigmoid(sx)
        s_ref[...] = s
        pbuf[me] = _hi(s, w_ref[...])
        _all_gather(pbuf, s2, r2, me, peers)
        for j in range(N_DEV):
            mod_ref[:, j * ncol:(j + 1) * ncol] = pbuf[j] + b_ref[:, j * ncol:(j + 1) * ncol]
        _all_gather_finish(smbuf, s3, r3, me, peers, sm_sends)
        tot = smbuf[0]
        for j in range(1, N_DEV):
            tot = tot + smbuf[j]
        smt_ref[...] = tot
        passed = [wcopy(4 + j, (*chip, cc), sibling) for j, chip in enumerate(chips)]
        for j, chip in enumerate(chips):
            wcopy(1 + j, (*chip, cc), (x, y, cc)).wait_recv()
            passed[j].start()
        wcopy(0, (x, y, 1 - cc), (x, y, cc)).wait_recv()
        for j, chip in enumerate(chips):
            wcopy(4 + j, (*chip, 1 - cc), (x, y, cc)).wait_recv()
        for cp in first + passed:
            cp.wait_send()
        mine.wait()

    return pl.pallas_call(
        body, name="mod_forward",
        in_specs=[_VMEM] * 6, out_specs=[_VMEM] * 4,
        out_shape=[_sds((16, N_DEV * ncol)), _sds((16, D)), _sds(sm_pack.shape),
                   _sds((N_DEV,) + w_first.shape, w_first.dtype)],
        scratch_shapes=[pltpu.VMEM((N_DEV, 8, D), F32), pltpu.VMEM((N_DEV, 16, ncol), F32),
                        pltpu.VMEM((N_DEV,) + sm_pack.shape, F32), _SEMS, _SEMS, _SEMS, _SEMS, _SEMS, _SEMS,
                        _SEMS, _SEMS, pltpu.SemaphoreType.DMA],
        compiler_params=pltpu.CompilerParams(vmem_limit_bytes=VMEM_LIMIT),
    )(c, c_ctx, w_mod_sh, b_mod, sm_pack, w_first)


def sum_blocks(gat, loss_row):
    def body(g_ref, tot_ref, loss_ref):
        tot = g_ref[0]
        for j in range(1, N_DEV):
            tot = tot + g_ref[j]
        tot_ref[...] = tot
        loss_ref[...] = jnp.sum(tot[loss_row:loss_row + 1, :], axis=1, keepdims=True)

    return pl.pallas_call(
        body, name="sum_blocks", in_specs=[_VMEM], out_specs=[_VMEM, _VMEM],
        out_shape=[_sds(gat.shape[1:]), _sds((1, 1))],
        compiler_params=pltpu.CompilerParams(vmem_limit_bytes=VMEM_LIMIT),
    )(gat)


def mod_backward(s, dm_sh, w, m, v, cc, m_cc, v_cc):
    shape = w.shape

    def body(s_ref, dm_ref, w_ref, m_ref, v_ref, cc_ref, mcc_ref, vcc_ref,
             gw_ref, dw_ref, mw_ref, vw_ref, gc_ref, dc_ref, mc_ref, vc_ref, pbuf, send_sems, recv_sems):
        me, peers = _mesh_pos()
        wv = w_ref[...]
        pbuf[me] = _hi_nt(dm_ref[8:16, :], wv)
        _all_gather(pbuf, send_sems, recv_sems, me, peers)
        g = _hi_tn(s_ref[...], dm_ref[...])
        gw_ref[...] = g
        dw_ref[...], mw_ref[...], vw_ref[...] = _adamw(wv, g, m_ref[...], v_ref[...])
        tot = pbuf[0]
        for j in range(1, N_DEV):
            tot = tot + pbuf[j]
        ccv = cc_ref[...]
        sg = _sigmoid(ccv)
        gc = tot[0:1, :] * (sg * (1.0 + ccv * (1.0 - sg)))
        gc_ref[...] = gc
        dc_ref[...], mc_ref[...], vc_ref[...] = _adamw(ccv, gc, mcc_ref[...], vcc_ref[...])

    return pl.pallas_call(
        body, name="mod_backward", in_specs=[_VMEM] * 8, out_specs=[_VMEM] * 8,
        out_shape=[_sds(shape)] * 4 + [_sds((1, D))] * 4,
        scratch_shapes=[pltpu.VMEM((N_DEV, 8, D), F32), _SEMS, _SEMS],
        compiler_params=pltpu.CompilerParams(vmem_limit_bytes=VMEM_LIMIT),
    )(s, dm_sh, w, m, v, cc, m_cc, v_cc)


_HBM = pl.BlockSpec(memory_space=pltpu.HBM)
_SEM = pl.BlockSpec(memory_space=pltpu.SEMAPHORE)
_EFFECT = pltpu.SideEffectType.DATAFLOW_SIDE_EFFECTING
_hbm = lambda a: pltpu.with_memory_space_constraint(a, pltpu.HBM)


def gather_start(shards, me, tag):
    n = len(shards)
    sems = pltpu.SemaphoreType.DMA((7 * n,))
    lands = [lax.dynamic_update_slice(lax.empty((N_DEV,) + s.shape, s.dtype), s[None], (me, 0, 0)) for s in shards]

    def body(*refs):
        s_refs, l_refs = refs[:n], refs[n:2 * n]
        send_sems, recv_sems = refs[2 * n], refs[2 * n + 1]
        token = refs[-1]
        my, peers = _mesh_pos()
        for w in range(n):
            for k, peer in enumerate(peers):
                pltpu.make_async_remote_copy(
                    src_ref=s_refs[w], dst_ref=l_refs[w].at[my], send_sem=send_sems.at[w * 7 + k],
                    recv_sem=recv_sems.at[w * 7 + k], device_id=peer, device_id_type=MESH).start()
        token[...] = jnp.zeros_like(token)

    out = pl.pallas_call(
        body, name="gather_start_" + tag,
        out_shape=(sems, sems) + tuple(pltpu.HBM(a.shape, a.dtype) for a in list(shards) + lands) + (_sds((8, 128)),),
        in_specs=(_HBM,) * (2 * n), out_specs=(_SEM, _SEM) + (_HBM,) * (2 * n) + (_VMEM,),
        input_output_aliases={i: i + 2 for i in range(2 * n)},
        compiler_params=pltpu.CompilerParams(has_side_effects=_EFFECT),
    )(*[_hbm(a) for a in list(shards) + lands])
    return out[0], out[1], list(out[2:2 + n]), list(out[2 + n:2 + 2 * n]), out[-1]


def gather_wait(send_sems, recv_sems, shards, lands, after, tag):
    n = len(shards)

    def body(*refs):
        s_refs, l_refs = refs[:n], refs[n:2 * n]
        send_sems, recv_sems = refs[2 * n], refs[2 * n + 1]
        my, peers = _mesh_pos()
        for w in range(n):
            for k, peer in enumerate(peers):
                src = jnp.bitwise_xor(my, k + 1)
                cp = pltpu.make_async_remote_copy(
                    src_ref=s_refs[w], dst_ref=l_refs[w].at[src], send_sem=send_sems.at[w * 7 + k],
                    recv_sem=recv_sems.at[w * 7 + k], device_id=peer, device_id_type=MESH)
                cp.wait_send()
                cp.wait_recv()

    out = pl.pallas_call(
        body, name="gather_wait_" + tag,
        out_shape=tuple(pltpu.HBM(a.shape, a.dtype) for a in list(shards) + list(lands)),
        in_specs=(_HBM,) * (2 * n) + (_SEM, _SEM, _ANY), out_specs=(_HBM,) * (2 * n),
        input_output_aliases={i: i for i in range(2 * n)},
        compiler_params=pltpu.CompilerParams(has_side_effects=_EFFECT),
    )(*shards, *lands, send_sems, recv_sems, after)
    return list(out[n:2 * n])


def scatter_start(grads, tag):
    n = len(grads)
    sems = pltpu.SemaphoreType.DMA((7 * n,))
    lands = [lax.empty((N_DEV - 1,) + g.shape[1:], g.dtype) for g in grads]

    def body(*refs):
        g_refs, l_refs = refs[:n], refs[n:2 * n]
        send_sems, recv_sems = refs[2 * n], refs[2 * n + 1]
        token = refs[-1]
        me, peers = _mesh_pos()
        for w in range(n):
            for k, peer in enumerate(peers):
                dst = jnp.bitwise_xor(me, k + 1)
                pltpu.make_async_remote_copy(
                    src_ref=g_refs[w].at[dst], dst_ref=l_refs[w].at[k], send_sem=send_sems.at[w * 7 + k],
                    recv_sem=recv_sems.at[w * 7 + k], device_id=peer, device_id_type=MESH).start()
        token[...] = jnp.zeros_like(token)

    out = pl.pallas_call(
        body, name="scatter_start_" + tag,
        out_shape=(sems, sems) + tuple(pltpu.HBM(a.shape, a.dtype) for a in list(grads) + lands) + (_sds((8, 128)),),
        in_specs=(_HBM,) * (2 * n), out_specs=(_SEM, _SEM) + (_HBM,) * (2 * n) + (_VMEM,),
        input_output_aliases={i: i + 2 for i in range(2 * n)},
        compiler_params=pltpu.CompilerParams(has_side_effects=_EFFECT),
    )(*[_hbm(a) for a in list(grads) + lands])
    return out[0], out[1], list(out[2:2 + n]), list(out[2 + n:2 + 2 * n]), out[-1]


def scatter_wait(send_sems, recv_sems, grads, lands, after, tag):
    n = len(grads)

    def body(*refs):
        g_refs, l_refs = refs[:n], refs[n:2 * n]
        send_sems, recv_sems = refs[2 * n], refs[2 * n + 1]
        me, peers = _mesh_pos()
        for w in range(n):
            for k, peer in enumerate(peers):
                dst = jnp.bitwise_xor(me, k + 1)
                cp = pltpu.make_async_remote_copy(
                    src_ref=g_refs[w].at[dst], dst_ref=l_refs[w].at[k], send_sem=send_sems.at[w * 7 + k],
                    recv_sem=recv_sems.at[w * 7 + k], device_id=peer, device_id_type=MESH)
                cp.wait_send()
                cp.wait_recv()

    out = pl.pallas_call(
        body, name="scatter_wait_" + tag,
        out_shape=tuple(pltpu.HBM(a.shape, a.dtype) for a in list(grads) + list(lands)),
        in_specs=(_HBM,) * (2 * n) + (_SEM, _SEM, _ANY), out_specs=(_HBM,) * (2 * n),
        input_output_aliases={i: i for i in range(2 * n)},
        compiler_params=pltpu.CompilerParams(has_side_effects=_EFFECT),
    )(*grads, *lands, send_sems, recv_sems, after)
    return list(out[n:2 * n])


def _vec8(rows, width):
    rid = lax.broadcasted_iota(jnp.int32, (8, width), 0)
    out = jnp.zeros((8, width), F32)
    for i, r in enumerate(rows):
        r = r.reshape(-1)
        r = jnp.pad(r, (0, width - r.shape[0]))
        out = jnp.where(rid == i, r[None, :], out)
    return out


def local_step(x, ctx, tgt, mod, mod_c, small, w_int, start, late_weights, grads_ready, small_ready, tt, tt_ctx, cb,
               cb_ctx):
    sh1, sc1, g1, sh2, sc2, g2 = [mod[i * D:(i + 1) * D] for i in range(6)]
    csh1, csc1 = mod_c[0:D], mod_c[D:2 * D]
    vec1 = _vec8([small["norm1_g"], sh1, sc1], D)
    vec1c = _vec8([small["norm1_g"], csh1, csc1], D)
    vec2 = _vec8([small["norm2_g"], sh2, sc2], D)
    vec3 = _vec8([g2, small["final_g"]], D)
    vecm = _vec8([g1, small["norm2_g"], sh2, sc2], D)
    vcm = _vec8([jnp.tile(small["gla_norm_g"].reshape(HV), NH), small["conv_b"], small["conv_ln_g"],
                 small["conv_ln_b"]], DC)
    convw = jnp.pad(small["conv_w"], ((0, 1), (0, 0)))
    wa = jnp.zeros((128, 512), F32)
    wa = wa.at[0:RANK, 0:DK].set(small["w_a2_f"]).at[RANK:2 * RANK, DK:2 * DK].set(small["w_a2_b"])
    ba = jnp.concatenate([small["b_a_f"].reshape(1, DK), small["b_a_b"].reshape(1, DK)], axis=1)

    _, _, kc, vc_, _, rc, lac, hc = proj_fwd(ctx, vec1c + start, w_int, wa, ba, tt_ctx)
    qc0 = jnp.zeros_like(kc)
    _, _, sallf_c, sallb_c, sfin_c = gla_fwd(qc0, kc, vc_, lac, jnp.zeros((2, HV, DK), F32), cb_ctx)
    u, q, k, v, g, r, la, h = proj_fwd(x, vec1, w_int, wa, ba, tt)
    o_f, o_b, sall_f, sall_b, _ = gla_fwd(q, k, v, la, sfin_c, cb)
    w_out, wg_t, wu_t, w_down = late_weights(o_b)
    x1, cat, mix, yc, h2 = merge_fwd(u, g, o_f, o_b, x, vecm, vcm, convw, w_out, tt)
    tt2 = min(2 * tt, x.shape[0])
    act, dact, hid = ffn_gate_up(h2, wg_t, wu_t, tt2)
    dx2, dff, acc3 = ffn_down_loss(hid, x1, tgt, vec3, w_down, tt)
    dgt, dup = ffn_dhid(dff, act, dact, w_down, tt2)
    dx1, acc2 = ffn_dh2(dgt, dup, x1, dx2, vec2, wg_t, wu_t, tt)
    bt = min(GRAD_TOKEN_BLOCK, x.shape[0])
    gw = {"w_down": tn_matmul(hid, dff, FN, bt), "wg_t": tn_matmul(dgt, h2, FN, bt),
          "wu_t": tn_matmul(dup, h2, FN, bt)}
    vecm = vecm + grads_ready(("wg_t", "wu_t", "w_down"), gw)
    du, dg, do, dmix, accm1, accm2, dconvw = merge_bwd(dx1, mix, u, g, o_f, o_b, yc, vecm, vcm, convw, w_out, tt)
    gw["w_out"] = tn_matmul(cat, dmix, D, bt)
    dsfin = jnp.zeros((2, HV, DK), F32) + grads_ready(("w_out",), gw)
    dqf, dkf, dvf, dlaf, dqb, dkb, dvb, dlab, ds0 = gla_bwd(q, k, v, la, do, sall_f, sall_b, dsfin, cb)
    gx, dp, acc1, dba, dwa = proj_bwd(du, (dqf, dqb, dkf, dkb, dvf, dvb), dg, dlaf, dlab, la, r, x, dx1, vec1, w_int,
                                      wa, tt)
    tcx = ctx.shape[0]
    zc = lambda w, dt=MXU_DTYPE: jnp.zeros((tcx, w), dt)
    _, dkf, dvf, dlaf, _, dkb, dvb, dlab, _ = gla_bwd(qc0, kc, vc_, lac, zc(DV), sallf_c, sallb_c, ds0, cb_ctx)
    _, dpc, acc1c, dbac, dwac = proj_bwd(zc(1024), (zc(DK), zc(DK), dkf, dkb, dvf, dvb), zc(DV), dlaf, dlab, lac, rc,
                                         ctx, zc(D, F32), vec1c, w_int, wa, tt_ctx)
    dwa_t = dwa + dwac
    dba_t = dba + dbac
    gs = {
        "norm1_g": acc1[2] + acc1c[2], "norm2_g": acc2[2], "final_g": acc3[1], "loss": acc3[2],
        "gla_norm_g": accm2[0], "conv_b": accm2[1], "conv_ln_g": accm2[2], "conv_ln_b": accm2[3],
        "conv_w": dconvw, "b_a": dba_t[0], "w_a2": dwa_t,
    }
    dmod = _vec8([acc1[0], acc1[1], accm1[0], acc2[0], acc2[1], acc3[0]], D)
    dmod_c = _vec8([acc1c[0], acc1c[1]], D)
    dpc = dpc + small_ready(gs, dmod, dmod_c).astype(dpc.dtype)
    btc = min(GRAD_TOKEN_BLOCK, tcx)
    gw["w_int"] = tn_matmul(dp, h, 896, bt, init=tn_matmul(dpc, hc, 896, btc)[0])
    grads_ready(("w_int",), gw)
    return gx, gw


PACK_ROWS = 96
ROW_N1, ROW_N2, ROW_FG, ROW_LOSS, ROW_GN, ROW_CB, ROW_LG, ROW_LB, ROW_BA = 0, 1, 2, 3, 4, 5, 6, 7, 8
ROW_DMOD, ROW_DMODC, ROW_CW, ROW_WA = 16, 24, 32, 64


def _pack_small(gs, dmod, dmod_c):
    pad = lambda a: jnp.pad(a, ((0, 0), (0, D - a.shape[1])))
    singles = _vec8([gs["norm1_g"], gs["norm2_g"], gs["final_g"], gs["loss"], gs["gla_norm_g"], gs["conv_b"],
                     gs["conv_ln_g"], gs["conv_ln_b"]], D)
    return jnp.concatenate([singles, _vec8([gs["b_a"]], D), dmod, dmod_c, pad(gs["conv_w"]), pad(gs["w_a2"][0:32])],
                           axis=0)


def kernel(x, c, ctx, c_ctx, w_mod, b_mod, norm1_g, norm2_g, w_in, conv_w, conv_b, conv_ln_g, conv_ln_b, w_a2_f, b_a_f, w_a2_b, b_a_b, gla_norm_g, w_out, w_gate, w_up, w_down, final_g, loss_target, m_c_ctx, m_w_mod, m_b_mod, m_norm1_g, m_norm2_g, m_w_in, m_conv_w, m_conv_b, m_conv_ln_g, m_conv_ln_b, m_w_a2_f, m_b_a_f, m_w_a2_b, m_b_a_b, m_gla_norm_g, m_w_out, m_w_gate, m_w_up, m_w_down, m_final_g, v_c_ctx, v_w_mod, v_b_mod, v_norm1_g, v_norm2_g, v_w_in, v_conv_w, v_conv_b, v_conv_ln_g, v_conv_ln_b, v_w_a2_f, v_b_a_f, v_w_a2_b, v_b_a_b, v_gla_norm_g, v_w_out, v_w_gate, v_w_up, v_w_down, v_final_g):
    me = 4 * lax.axis_index("x") + 2 * lax.axis_index("y") + lax.axis_index("c")
    t = x.shape[1]
    tcx = ctx.shape[1]
    r_in, r_out, r_ff = w_in.shape[2], w_out.shape[1], w_gate.shape[2]
    r_in_b = -(-r_in // 16) * 16

    tb = lambda w: w.T.astype(MXU_DTYPE)

    small = dict(norm1_g=norm1_g[0], norm2_g=norm2_g[0], final_g=final_g, gla_norm_g=gla_norm_g[0],
                 conv_b=conv_b[0], conv_ln_g=conv_ln_g[0], conv_ln_b=conv_ln_b[0], b_a_f=b_a_f[0], b_a_b=b_a_b[0])
    sm_pack = jnp.zeros((48, DC), F32)
    sm_pack = lax.dynamic_update_slice(sm_pack, conv_w[0], (0, me * (DC // N_DEV)))
    sm_pack = lax.dynamic_update_slice(sm_pack, w_a2_f[0], (32, me * (DK // N_DEV)))
    sm_pack = lax.dynamic_update_slice(sm_pack, w_a2_b[0], (32, DK + me * (DK // N_DEV)))

    mod_all, s_all, sm_tot, wall = mod_forward(c, c_ctx.reshape(1, D), w_mod[0], b_mod, sm_pack,
                                               jnp.pad(tb(w_in[0]), ((0, r_in_b - r_in), (0, 0))))
    mod = lax.dynamic_slice(mod_all, (me, 0), (1, 6 * D)).reshape(6 * D)
    mod_c = mod_all[8]
    small["conv_w"] = sm_tot[0:CW, :]
    small["w_a2_f"] = sm_tot[32:32 + RANK, 0:DK]
    small["w_a2_b"] = sm_tot[32:32 + RANK, DK:2 * DK]

    w_int =jnp.pad(wall[:, 0:r_in, :].reshape(N_DEV * r_in, D), ((0, DINP - DIN), (0, 0)))
    after_w_in = (wall[0:1, 0:1, 0] * 0).astype(MXU_DTYPE)
    late = [w_out[0].astype(MXU_DTYPE) + after_w_in, tb(w_gate[0]) + after_w_in, tb(w_up[0]) + after_w_in,
            w_down[0].astype(MXU_DTYPE) + after_w_in]
    g_send, g_recv, late_thru, late_lands, g_token = gather_start(late, me, "late")

    def late_weights(after):
        got = gather_wait(g_send, g_recv, late_thru, late_lands, after, "late")
        return tuple(a.reshape(N_DEV * a.shape[1], D) for a in got)

    pad_in = lambda g: jnp.pad(g[0:DIN].reshape(N_DEV, r_in, D), ((0, 0), (0, r_in_b - r_in), (0, 0)))
    blocked = {"w_int": pad_in, "w_out": lambda g: g.reshape(N_DEV, r_out, D)}
    as_blocks = lambda n, g: blocked.get(n, lambda a: a.reshape(N_DEV, r_ff, D))(g)
    pending = []

    def grads_ready(names, gw_now):
        blocks = [as_blocks(n, gw_now[n][1]) for n in names]
        send, recv_s, thru, zones, token = scatter_start(blocks, names[0])
        pending.append((names, send, recv_s, thru, zones))
        if names[0] == "w_int":
            finish_small(token)
        return token[0:1, 0:1]

    sm = {}

    def small_ready(gs, dmod, dmod_c):
        sm["copy"] = gather_start([_pack_small(gs, dmod, dmod_c)], me, "small")
        return sm["copy"][4][0:1, 0:1]

    def finish_small(after):
        send, recv_s, thru, zones, _ = sm["copy"]
        gat = gather_wait(send, recv_s, thru, zones, after, "small")[0]
        sm["tot"], sm["loss"] = sum_blocks(gat, ROW_LOSS)
        sm["dm"] = jnp.concatenate(
            [gat[:, ROW_DMOD:ROW_DMOD + 6, :].reshape(N_DEV, 6 * D),
             jnp.pad(sm["tot"][ROW_DMODC:ROW_DMODC + 6, :].reshape(1, 6 * D), ((0, 7), (0, 0)))], axis=0)
        ncol = w_mod.shape[2]
        dm_sh = lax.dynamic_slice(sm["dm"], (0, me * ncol), (16, ncol))
        sm["mod"] = mod_backward(s_all, dm_sh, w_mod[0], m_w_mod[0], v_w_mod[0], c_ctx.reshape(1, D),
                                 m_c_ctx.reshape(1, D), v_c_ctx.reshape(1, D))
        return sm["mod"][4][0:1, 0:1] * 0

    gx, gw = local_step(x[0], ctx[0], loss_target[0], mod, mod_c, small, w_int, g_token[0:1, 0:1], late_weights,
                        grads_ready, small_ready, TOKEN_TILE, CTX_TOKEN_TILE, GLA_CHUNKS, CTX_GLA_CHUNKS)
    tot = sm["tot"]
    loss = sm["loss"].reshape(())
    g_wmod, d_wmod, nm_wmod, nv_wmod, g_cc, d_cc, nm_cc, nv_cc = sm["mod"]

    recv = {}

    def wait_for(entry, after):
        names, send, recv_s, thru, zones = entry
        recv.update(dict(zip(names, scatter_wait(send, recv_s, thru, zones, after, names[0]))))

    for entry in pending[:-1]:
        wait_for(entry, tot)
    own = {n: lax.dynamic_index_in_dim(as_blocks(n, gw[n][0]), me, 0, keepdims=False) for n in gw if n != "w_int"}
    own["w_int"] = jnp.pad(lax.dynamic_slice(gw["w_int"][0], (me * r_in, 0), (r_in, D)), ((0, r_in_b - r_in), (0, 0)))
    padt = lambda w: jnp.pad(w.T, ((0, r_in_b - r_in), (0, 0)))
    big = {}
    big["w_gate"] = [a.T for a in adamw_sharded(own["wg_t"], recv["wg_t"], w_gate[0].T, m_w_gate[0].T,
                                                 v_w_gate[0].T)]
    big["w_up"] = [a.T for a in adamw_sharded(own["wu_t"], recv["wu_t"], w_up[0].T, m_w_up[0].T, v_w_up[0].T)]
    big["w_down"] = adamw_sharded(own["w_down"], recv["w_down"], w_down[0], m_w_down[0], v_w_down[0])
    big["w_out"] = adamw_sharded(own["w_out"], recv["w_out"], w_out[0], m_w_out[0], v_w_out[0])
    wait_for(pending[-1], big["w_out"][0])
    big["w_in"] = [a[0:r_in].T for a in adamw_sharded(own["w_int"], recv["w_int"], padt(w_in[0]), padt(m_w_in[0]),
                                                       padt(v_w_in[0]))]
    big["w_mod"] = [g_wmod, d_wmod, nm_wmod, nv_wmod]

    row = lambda r, w: tot[r:r + 1, 0:w]
    gn_row = tot[ROW_GN:ROW_GN + 1, 0:DC]
    g_small = {
        "b_mod": (tot[ROW_DMOD:ROW_DMOD + 6] + tot[ROW_DMODC:ROW_DMODC + 6]).reshape(1, 6 * D),
        "norm1_g": row(ROW_N1, D), "norm2_g": row(ROW_N2, D),
        "conv_w": lax.dynamic_slice(tot, (ROW_CW, me * (DC // N_DEV)), (CW, DC // N_DEV)),
        "conv_b": row(ROW_CB, DC), "conv_ln_g": row(ROW_LG, DC), "conv_ln_b": row(ROW_LB, DC),
        "w_a2_f": lax.dynamic_slice(tot, (ROW_WA, me * (DK // N_DEV)), (RANK, DK // N_DEV)),
        "b_a_f": tot[ROW_BA:ROW_BA + 1, 0:DK],
        "w_a2_b": lax.dynamic_slice(tot, (ROW_WA + RANK, DK + me * (DK // N_DEV)), (RANK, DK // N_DEV)),
        "b_a_b": tot[ROW_BA:ROW_BA + 1, DK:2 * DK],
        "gla_norm_g": gn_row[:, 0:HV] + gn_row[:, HV:2 * HV] + gn_row[:, 2 * HV:3 * HV] + gn_row[:, 3 * HV:4 * HV],
        "final_g": row(ROW_FG, D),
    }
    wmv = {
        "b_mod": (b_mod, m_b_mod, v_b_mod), "norm1_g": (norm1_g, m_norm1_g, v_norm1_g),
        "norm2_g": (norm2_g, m_norm2_g, v_norm2_g), "conv_w": (conv_w[0], m_conv_w[0], v_conv_w[0]),
        "conv_b": (conv_b, m_conv_b, v_conv_b), "conv_ln_g": (conv_ln_g, m_conv_ln_g, v_conv_ln_g),
        "conv_ln_b": (conv_ln_b, m_conv_ln_b, v_conv_ln_b), "w_a2_f": (w_a2_f[0], m_w_a2_f[0], v_w_a2_f[0]),
        "b_a_f": (b_a_f, m_b_a_f, v_b_a_f), "w_a2_b": (w_a2_b[0], m_w_a2_b[0], v_w_a2_b[0]),
        "b_a_b": (b_a_b, m_b_a_b, v_b_a_b), "gla_norm_g": (gla_norm_g, m_gla_norm_g, v_gla_norm_g),
        "final_g": (final_g.reshape(1, D), m_final_g.reshape(1, D), v_final_g.reshape(1, D)),
    }
    names_small = list(g_small)
    upd = adamw_small([(g_small[n],) + wmv[n] for n in names_small])
    res = {n: (g_small[n],) + upd[i] for i, n in enumerate(names_small)}
    res["c_ctx"] = (g_cc, d_cc, nm_cc, nv_cc)
    for n in ("w_mod", "w_in", "w_out", "w_gate", "w_up", "w_down"):
        res[n] = tuple(big[n])

    order = ["c_ctx", "w_mod", "b_mod", "norm1_g", "norm2_g", "w_in", "conv_w", "conv_b", "conv_ln_g", "conv_ln_b",
             "w_a2_f", "b_a_f", "w_a2_b", "b_a_b", "gla_norm_g", "w_out", "w_gate", "w_up", "w_down", "final_g"]
    shapes = {"c_ctx": c_ctx.shape, "w_mod": w_mod.shape, "b_mod": b_mod.shape, "norm1_g": norm1_g.shape,
              "norm2_g": norm2_g.shape, "w_in": w_in.shape, "conv_w": conv_w.shape, "conv_b": conv_b.shape,
              "conv_ln_g": conv_ln_g.shape, "conv_ln_b": conv_ln_b.shape, "w_a2_f": w_a2_f.shape,
              "b_a_f": b_a_f.shape, "w_a2_b": w_a2_b.shape, "b_a_b": b_a_b.shape, "gla_norm_g": gla_norm_g.shape,
              "w_out": w_out.shape, "w_gate": w_gate.shape, "w_up": w_up.shape, "w_down": w_down.shape,
              "final_g": final_g.shape}
    outs = [loss, gx.reshape(x.shape)]
    for i in range(4):
        outs += [res[n][i].reshape(shapes[n]) for n in order]
    return tuple(outs)
```

```python
import functools

import jax
import jax.numpy as jnp
from jax import lax
from jax.experimental import pallas as pl
from jax.experimental.pallas import tpu as pltpu

F32 = jnp.float32
MXU_DTYPE = jnp.bfloat16
WIRE_DTYPE = jnp.bfloat16
HI = lax.Precision.HIGHEST
MESH = pl.DeviceIdType.MESH

N_DEV = 8
D = 1024
DC = 512
NH = 4
HK = 64
HV = 128
DK = NH * HK
DV = NH * HV
RANK = 16
CHUNK = 64
SEG = 64
CW = 31
CPAD = 15
DFF = 2816
DIN = 2592
DINP = 2688
TAU = 16.0
EPS = 1e-6
VMEM_LIMIT = 56 * 1024 * 1024

TOKEN_TILE = 512
CTX_TOKEN_TILE = 256
GLA_CHUNKS = 16
CTX_GLA_CHUNKS = 4
GRAD_TOKEN_BLOCK = 2048

ADAM_LR = 0.001
ADAM_B1 = 0.9
ADAM_B2 = 0.999
ADAM_EPS = 1e-08
ADAM_WD = 0.01
ADAM_STEP = 10


def _mm(a, b):
    return jnp.dot(a.astype(MXU_DTYPE), b.astype(MXU_DTYPE), preferred_element_type=F32)


def _mm_nt(a, b):
    return lax.dot_general(a.astype(MXU_DTYPE), b.astype(MXU_DTYPE), (((1,), (1,)), ((), ())),
                           preferred_element_type=F32)


def _mm_tn(a, b):
    return lax.dot_general(a.astype(MXU_DTYPE), b.astype(MXU_DTYPE), (((0,), (0,)), ((), ())),
                           preferred_element_type=F32)


def _hi(a, b):
    return jnp.dot(a, b, precision=HI, preferred_element_type=F32)


def _hi_nt(a, b):
    return lax.dot_general(a, b, (((1,), (1,)), ((), ())), precision=HI, preferred_element_type=F32)


def _hi_tn(a, b):
    return lax.dot_general(a, b, (((0,), (0,)), ((), ())), precision=HI, preferred_element_type=F32)


def _sigmoid(x):
    return 1.0 / (1.0 + jnp.exp(-x))


def _cparams(n_axes):
    return pltpu.CompilerParams(dimension_semantics=("arbitrary",) * n_axes, vmem_limit_bytes=VMEM_LIMIT)


def _full(shape):
    n = len(shape)
    return pl.BlockSpec(shape, lambda *_: (0,) * n)


def _rows(tt, width):
    return pl.BlockSpec((tt, width), lambda i: (i, 0))


def _sds(shape, dtype=F32):
    return jax.ShapeDtypeStruct(shape, dtype)


def _norm_mod(x, g, sh, sc):
    r = lax.rsqrt(jnp.mean(x * x, axis=-1, keepdims=True) + EPS)
    xn = x * r
    yy = xn * g
    return r, xn, yy, yy * (1.0 + sc) + sh


def _norm_mod_bwd(dh, r, xn, yy, g, sc):
    dsh = jnp.sum(dh, axis=0, keepdims=True)
    dsc = jnp.sum(dh * yy, axis=0, keepdims=True)
    dy = dh * (1.0 + sc)
    dg = jnp.sum(dy * xn, axis=0, keepdims=True)
    dxn = dy * g
    dx = r * (dxn - xn * jnp.mean(dxn * xn, axis=-1, keepdims=True))
    return dsh, dsc, dg, dx


def _zero_first(*refs):
    @pl.when(pl.program_id(0) == 0)
    def _():
        for r in refs:
            r[...] = jnp.zeros_like(r)


def _acc_rows(ref, rows):
    ref[...] += jnp.concatenate(rows + [jnp.zeros((8 - len(rows), rows[0].shape[1]), F32)], axis=0)


def proj_fwd(x, vec, w_int, wa, ba, tt):
    t = x.shape[0]

    def body(x_ref, vec_ref, w_ref, wa_ref, ba_ref, u_ref, q_ref, k_ref, v_ref, g_ref, r_ref, la_ref, h_ref):
        _, _, _, h = _norm_mod(x_ref[...], vec_ref[0:1, :], vec_ref[1:2, :], vec_ref[2:3, :])
        hb = h.astype(MXU_DTYPE)
        h_ref[...] = hb
        p = _mm_nt(hb, w_ref[...])
        u_ref[...] = p[:, 0:1024]
        q_ref[...] = p[:, 1024:1280]
        k_ref[...] = p[:, 1280:1536]
        v_ref[...] = p[:, 1536:2048].astype(MXU_DTYPE)
        g_ref[...] = p[:, 2048:2560]
        rr = p[:, 2560:2688]
        r_ref[...] = rr
        z = _mm(rr, wa_ref[...]) + ba_ref[...]
        la_ref[...] = (jnp.minimum(z, 0.0) - jnp.log(1.0 + jnp.exp(-jnp.abs(z)))) * (1.0 / TAU)

    return pl.pallas_call(
        body, name="proj_fwd", grid=(t // tt,),
        in_specs=[_rows(tt, D), _full((8, D)), _full((DINP, D)), _full((128, 512)), _full((1, 512))],
        out_specs=[_rows(tt, 1024), _rows(tt, DK), _rows(tt, DK), _rows(tt, DV), _rows(tt, DV), _rows(tt, 128),
                   _rows(tt, 512), _rows(tt, D)],
        out_shape=[_sds((t, 1024)), _sds((t, DK)), _sds((t, DK)), _sds((t, DV), MXU_DTYPE), _sds((t, DV)),
                   _sds((t, 128)),
                   _sds((t, 512)), _sds((t, D), MXU_DTYPE)],
        compiler_params=_cparams(1),
    )(x, vec, w_int, wa, ba)


def proj_bwd(du, dqkv, dg, dla_f, dla_b, la, r, x, dx1, vec, w_int, wa, tt):
    t = x.shape[0]

    def body(du_ref, dqf_ref, dqb_ref, dkf_ref, dkb_ref, dvf_ref, dvb_ref, dg_ref, dlaf_ref, dlab_ref, la_ref, r_ref,
             x_ref, dx1_ref, vec_ref, w_ref, wa_ref, gx_ref, dp_ref, acc_ref, dba_ref, dwa_ref):
        _zero_first(acc_ref, dba_ref, dwa_ref)
        md = lambda a: a.astype(MXU_DTYPE)
        g, sc = vec_ref[0:1, :], vec_ref[2:3, :]
        sums, dba, dwa = None, None, None
        for rows in (slice(0, tt // 2), slice(tt // 2, tt)):
            both = lambda a_ref, b_ref: md(a_ref[rows, :].astype(F32) + b_ref[rows, :].astype(F32))
            dla = jnp.concatenate([dlaf_ref[rows, :], dlab_ref[rows, :]], axis=1)
            dz = dla * (1.0 - jnp.exp(TAU * la_ref[rows, :])) * (1.0 / TAU)
            rr = r_ref[rows, :]
            dba_h = jnp.sum(dz, axis=0, keepdims=True)
            dwa_h = _mm_tn(rr, dz)
            dr = _mm_nt(dz, wa_ref[...])
            dp = jnp.concatenate([du_ref[rows, :], both(dqf_ref, dqb_ref), both(dkf_ref, dkb_ref),
                                  both(dvf_ref, dvb_ref), dg_ref[rows, :], md(dr)], axis=1)
            dp_ref[rows, :] = dp
            dh = _mm(dp, w_ref[...])
            rn, xn, yy, _ = _norm_mod(x_ref[rows, :], g, vec_ref[1:2, :], sc)
            dsh, dsc, dgn, dx = _norm_mod_bwd(dh, rn, xn, yy, g, sc)
            gx_ref[rows, :] = dx1_ref[rows, :] + dx
            part = [dsh, dsc, dgn]
            sums = part if sums is None else [a + b for a, b in zip(sums, part)]
            dba = dba_h if dba is None else dba + dba_h
            dwa = dwa_h if dwa is None else dwa + dwa_h
        _acc_rows(dba_ref, [dba])
        dwa_ref[...] += dwa
        _acc_rows(acc_ref, sums)

    return pl.pallas_call(
        body, name="proj_bwd", grid=(t // tt,),
        in_specs=[_rows(tt, 1024), _rows(tt, DK), _rows(tt, DK), _rows(tt, DK), _rows(tt, DK), _rows(tt, DV),
                  _rows(tt, DV), _rows(tt, DV), _rows(tt, DK), _rows(tt, DK), _rows(tt, 512),
                  _rows(tt, 128), _rows(tt, D), _rows(tt, D), _full((8, D)), _full((DINP, D)), _full((128, 512))],
        out_specs=[_rows(tt, D), _rows(tt, DINP), _full((8, D)), _full((8, 512)), _full((128, 512))],
        out_shape=[_sds((t, D)), _sds((t, DINP), MXU_DTYPE), _sds((8, D)), _sds((8, 512)), _sds((128, 512))],
        compiler_params=_cparams(1),
    )(du, *dqkv, dg, dla_f, dla_b, la, r, x, dx1, vec, w_int, wa)


def _dot_exact01(m01, x):
    bf = jnp.bfloat16
    w = x.shape[1]
    hi = x.astype(bf)
    r1 = x - hi.astype(F32)
    mid = r1.astype(bf)
    lo = (r1 - mid.astype(F32)).astype(bf)
    y = jnp.dot(m01.astype(bf), jnp.concatenate([hi, mid, lo], axis=1), preferred_element_type=F32)
    return y[:, 0:w] + y[:, w:2 * w] + y[:, 2 * w:3 * w]


def _gla_chunk(d, qc, kc, la_c):
    row = lax.broadcasted_iota(jnp.int32, (CHUNK, CHUNK), 0)
    col = lax.broadcasted_iota(jnp.int32, (CHUNK, CHUNK), 1)
    cum = ((col <= row) if d == 0 else (col >= row)).astype(F32)
    cum_t = ((col >= row) if d == 0 else (col <= row)).astype(F32)
    cum4 = jnp.concatenate([cum] * NH, axis=0)
    head_of_lane = lax.broadcasted_iota(jnp.int32, (1, DK), 1) // HK
    b = _dot_exact01(cum, la_c)
    bl = jnp.sum(la_c, axis=0, keepdims=True)
    eb = jnp.exp(b)
    enb = jnp.exp(-b)
    ekd = jnp.exp(bl - b)
    qt = qc * (HK ** -0.5) * eb
    kt = kc * enb
    kd = kc * ekd
    qst = jnp.concatenate([jnp.where(head_of_lane == h, qt, 0.0) for h in range(NH)], axis=0)
    a = _mm_nt(qst, kt) * cum4
    return cum_t, cum4, head_of_lane, eb, enb, ekd, qt, kt, kd, qst, a, jnp.exp(bl)


NPAIR = NH // 2


def _pair_rows(x, p):
    return x[2 * p * CHUNK:(2 * p + 2) * CHUNK]


def _pair_lanes(x, p):
    return x[:, 2 * p * HK:(2 * p + 2) * HK]


def _pair_fold(r):
    half = lax.broadcasted_iota(jnp.int32, (1, 2 * HK), 1) // HK
    return jnp.where(half == 0, r[0:CHUNK], 0.0) + jnp.where(half == 1, r[CHUNK:2 * CHUNK], 0.0)


def gla_fwd(q, k, v, la, s0, cb):
    t = q.shape[0]
    nc = t // CHUNK
    nb = nc // cb

    def body(qf_ref, kf_ref, vf_ref, laf_ref, qb_ref, kb_ref, vb_ref, lab_ref, s0_ref,
             of_ref, ob_ref, sf_ref, sb_ref, sfin_ref, s_scr):
        i = pl.program_id(0)

        @pl.when(i == 0)
        def _():
            s_scr[...] = s0_ref[...]

        def chunk(d, jj, q_ref, k_ref, v_ref, la_ref, o_ref, sall_ref):
            rows = slice(jj * CHUNK, (jj + 1) * CHUNK)
            vc = v_ref[rows, :]
            _, _, head_of_lane, _, _, _, _, _, kd, qst, a, dec = _gla_chunk(
                d, q_ref[rows, :], k_ref[rows, :], la_ref[rows, :])
            s = s_scr[d]
            sall_ref[jj] = s
            inter = _mm_nt(qst, s)
            outs = []
            for h in range(NH):
                hs = slice(h * CHUNK, (h + 1) * CHUNK)
                outs.append(_mm(a[hs], vc[:, h * HV:(h + 1) * HV]) + inter[hs])
            o_ref[rows, :] = jnp.concatenate(outs, axis=1)
            kv = _mm_tn(vc, kd)
            s_new = dec * s
            for h in range(NH):
                s_new = s_new + jnp.where(head_of_lane == h, kv[h * HV:(h + 1) * HV], 0.0)
            s_scr[d] = s_new

        for j in range(cb):
            chunk(0, j, qf_ref, kf_ref, vf_ref, laf_ref, of_ref, sf_ref)
            chunk(1, cb - 1 - j, qb_ref, kb_ref, vb_ref, lab_ref, ob_ref, sb_ref)

        @pl.when(i == nb - 1)
        def _():
            sfin_ref[...] = s_scr[...]

    tb = cb * CHUNK
    fwd = lambda w, c=0: pl.BlockSpec((tb, w), lambda i: (i, c))
    bwd = lambda w, c=0: pl.BlockSpec((tb, w), lambda i: (nb - 1 - i, c))
    return pl.pallas_call(
        body, name="gla_fwd", grid=(nb,),
        in_specs=[fwd(DK), fwd(DK), fwd(DV), fwd(DK, 0), bwd(DK), bwd(DK), bwd(DV), bwd(DK, 1), _full((2, HV, DK))],
        out_specs=[fwd(DV), bwd(DV), pl.BlockSpec((cb, HV, DK), lambda i: (i, 0, 0)),
                   pl.BlockSpec((cb, HV, DK), lambda i: (nb - 1 - i, 0, 0)), _full((2, HV, DK))],
        out_shape=[_sds((t, DV)), _sds((t, DV)), _sds((nc, HV, DK)), _sds((nc, HV, DK)), _sds((2, HV, DK))],
        scratch_shapes=[pltpu.VMEM((2, HV, DK), F32)],
        compiler_params=_cparams(1),
    )(q, k, v, la, q, k, v, la, s0)


def gla_bwd(q, k, v, la, do, sall_f, sall_b, dsfin, cb):
    t = q.shape[0]
    nc = t // CHUNK
    nb = nc // cb

    def body(qf_ref, kf_ref, vf_ref, laf_ref, dof_ref, sf_ref, qb_ref, kb_ref, vb_ref, lab_ref, dob_ref, sb_ref,
             dsfin_ref, dqf_ref, dkf_ref, dvf_ref, dlaf_ref, dqb_ref, dkb_ref, dvb_ref, dlab_ref, ds0_ref, ds_scr):
        i = pl.program_id(0)

        @pl.when(i == 0)
        def _():
            ds_scr[...] = dsfin_ref[...]

        def chunk(d, jj, q_ref, k_ref, v_ref, la_ref, do_ref, sall_ref, dq_ref, dk_ref, dv_ref, dla_ref):
            rows = slice(jj * CHUNK, (jj + 1) * CHUNK)
            vc = v_ref[rows, :]
            doc = do_ref[rows, :]
            cum_t, cum4, head_of_lane, eb, enb, ekd, qt, kt, kd, qst, a, dec = _gla_chunk(
                d, q_ref[rows, :], k_ref[rows, :], la_ref[rows, :])
            s = sall_ref[jj]
            ds = ds_scr[d]
            hv = lambda x, h: x[:, h * HV:(h + 1) * HV]
            hr = lambda x, h: x[h * CHUNK:(h + 1) * CHUNK]
            fold = lambda x: functools.reduce(
                lambda p, c: p + c, [jnp.where(head_of_lane == h, hr(x, h), 0.0) for h in range(NH)])
            dost = jnp.concatenate([hv(doc, h) for h in range(NH)], axis=0)
            vst = jnp.concatenate([hv(vc, h) for h in range(NH)], axis=0)
            da = jnp.concatenate([_mm_nt(hv(doc, h), hv(vc, h)) for h in range(NH)], axis=0) * cum4
            pairs = range(NPAIR)
            dqt = fold(_mm(da, kt)) + jnp.concatenate(
                [_pair_fold(_mm(_pair_rows(dost, p), _pair_lanes(s, p))) for p in pairs], axis=1)
            dkt = _mm_tn(da, qst)
            kdst = jnp.concatenate([jnp.where(head_of_lane == h, kd, 0.0) for h in range(NH)], axis=0)
            dv_inter = jnp.concatenate(
                [_mm_nt(_pair_lanes(_pair_rows(kdst, p), p), _pair_lanes(ds, p)) for p in pairs], axis=0)
            dv_ref[rows, :] = jnp.concatenate(
                [_mm_tn(hr(a, h), hv(doc, h)) + hr(dv_inter, h) for h in range(NH)], axis=1).astype(MXU_DTYPE)
            dkd = jnp.concatenate([_pair_fold(_mm(_pair_rows(vst, p), _pair_lanes(ds, p))) for p in pairs], axis=1)
            ds_scr[d] = dec * ds + jnp.concatenate(
                [_mm_tn(_pair_rows(dost, p), _pair_lanes(_pair_rows(qst, p), p)) for p in pairs], axis=1)
            tkd = dkd * kd
            db = dqt * qt - dkt * kt - tkd
            dbl = jnp.sum(ds * s, axis=0, keepdims=True) * dec + jnp.sum(tkd, axis=0, keepdims=True)
            dla_ref[rows, :] = _dot_exact01(cum_t, db) + dbl
            dq_ref[rows, :] = (dqt * eb * (HK ** -0.5)).astype(MXU_DTYPE)
            dk_ref[rows, :] = (dkt * enb + dkd * ekd).astype(MXU_DTYPE)

        for j in range(cb):
            chunk(0, cb - 1 - j, qf_ref, kf_ref, vf_ref, laf_ref, dof_ref, sf_ref, dqf_ref, dkf_ref, dvf_ref, dlaf_ref)
            chunk(1, j, qb_ref, kb_ref, vb_ref, lab_ref, dob_ref, sb_ref, dqb_ref, dkb_ref, dvb_ref, dlab_ref)

        @pl.when(i == nb - 1)
        def _():
            ds0_ref[...] = ds_scr[...]

    tb = cb * CHUNK
    rev = lambda w, c=0: pl.BlockSpec((tb, w), lambda i: (nb - 1 - i, c))
    fro = lambda w, c=0: pl.BlockSpec((tb, w), lambda i: (i, c))
    st_rev = pl.BlockSpec((cb, HV, DK), lambda i: (nb - 1 - i, 0, 0))
    st_fro = pl.BlockSpec((cb, HV, DK), lambda i: (i, 0, 0))
    md = MXU_DTYPE
    return pl.pallas_call(
        body, name="gla_bwd", grid=(nb,),
        in_specs=[rev(DK), rev(DK), rev(DV), rev(DK, 0), rev(DV), st_rev,
                  fro(DK), fro(DK), fro(DV), fro(DK, 1), fro(DV), st_fro, _full((2, HV, DK))],
        out_specs=[rev(DK), rev(DK), rev(DV), rev(DK), fro(DK), fro(DK), fro(DV), fro(DK), _full((2, HV, DK))],
        out_shape=[_sds((t, DK), md), _sds((t, DK), md), _sds((t, DV), md), _sds((t, DK)),
                   _sds((t, DK), md), _sds((t, DK), md), _sds((t, DV), md), _sds((t, DK)), _sds((2, HV, DK))],
        scratch_shapes=[pltpu.VMEM((2, HV, DK), F32)],
        compiler_params=_cparams(1),
    )(q, k, v, la, do, sall_f, q, k, v, la, do, sall_b, dsfin)


def _seg_pos(tt):
    return lax.broadcasted_iota(jnp.int32, (tt, 1), 0) % SEG


def _shifted(x, s, pos, tt):
    if s == 0:
        return x
    return jnp.where((pos + s >= 0) & (pos + s < SEG), pltpu.roll(x, (-s) % tt, 0), 0.0)


def _head_norm(o, gn):
    rs, xs = [], []
    for h in range(NH):
        oh = o[:, h * HV:(h + 1) * HV]
        r = lax.rsqrt(jnp.mean(oh * oh, axis=-1, keepdims=True) + EPS)
        rs.append(r)
        xs.append(oh * r)
    return rs, xs


def merge_fwd(u, g, o_f, o_b, x, vec, vc, convw, w_out, tt):
    t = x.shape[0]

    def body(u_ref, g_ref, of_ref, ob_ref, x_ref, vec_ref, vc_ref, cw_ref, w_ref, x1_ref, cat_ref, mix_ref, yc_ref,
             h2_ref):
        a = u_ref[:, 0:DC]
        gate = u_ref[:, DC:2 * DC]
        vv = a * _sigmoid(gate)
        pos = _seg_pos(tt)
        cw = cw_ref[...]
        yc = jnp.zeros((tt, DC), F32) + vc_ref[1:2, :]
        for j in range(CW):
            yc = yc + _shifted(vv, j - CPAD, pos, tt) * cw[j:j + 1, :]
        yc_ref[...] = yc
        mu = jnp.mean(yc, axis=-1, keepdims=True)
        yd = yc - mu
        rs = lax.rsqrt(jnp.mean(yd * yd, axis=-1, keepdims=True) + EPS)
        ln = yd * rs * vc_ref[2:3, :] + vc_ref[3:4, :]
        conv_o = ln * _sigmoid(ln)
        o = of_ref[...] + ob_ref[...]
        _, xs = _head_norm(o, None)
        gg = g_ref[...]
        o2g = jnp.concatenate(xs, axis=1) * vc_ref[0:1, :] * (gg * _sigmoid(gg))
        cat = jnp.concatenate([conv_o, o2g], axis=1).astype(MXU_DTYPE)
        cat_ref[...] = cat
        mix = _mm(cat, w_ref[...])
        mix_ref[...] = mix
        x1 = x_ref[...] + vec_ref[0:1, :] * mix
        x1_ref[...] = x1
        _, _, _, h2 = _norm_mod(x1, vec_ref[1:2, :], vec_ref[2:3, :], vec_ref[3:4, :])
        h2_ref[...] = h2.astype(MXU_DTYPE)

    return pl.pallas_call(
        body, name="merge_fwd", grid=(t // tt,),
        in_specs=[_rows(tt, 1024), _rows(tt, DV), _rows(tt, DV), _rows(tt, DV), _rows(tt, D),
                  _full((8, D)), _full((8, DC)), _full((32, DC)), _full((D, D))],
        out_specs=[_rows(tt, D), _rows(tt, D), _rows(tt, D), _rows(tt, DC), _rows(tt, D)],
        out_shape=[_sds((t, D)), _sds((t, D), MXU_DTYPE), _sds((t, D)), _sds((t, DC)), _sds((t, D), MXU_DTYPE)],
        compiler_params=_cparams(1),
    )(u, g, o_f, o_b, x, vec, vc, convw, w_out)


def merge_bwd(dx1, mix, u, g, o_f, o_b, yc, vec, vc, convw, w_out, tt):
    t = dx1.shape[0]

    def body(dx1_ref, mix_ref, u_ref, g_ref, of_ref, ob_ref, yc_ref, vec_ref, vc_ref, cw_ref, w_ref,
             du_ref, dg_ref, do_ref, dmix_ref, acc1_ref, acc2_ref, dcw_ref):
        _zero_first(acc1_ref, acc2_ref, dcw_ref)
        dx1v = dx1_ref[...]
        dg1 = jnp.sum(dx1v * mix_ref[...], axis=0, keepdims=True)
        dmix = (vec_ref[0:1, :] * dx1v).astype(MXU_DTYPE)
        dmix_ref[...] = dmix
        dcat = _mm_nt(dmix, w_ref[...])
        dconv_o = dcat[:, 0:DC]
        do2 = dcat[:, DC:2 * DC]
        gn = vc_ref[0:1, :]
        o = of_ref[...] + ob_ref[...]
        rs, xs = _head_norm(o, None)
        xn = jnp.concatenate(xs, axis=1)
        gg = g_ref[...]
        sg = _sigmoid(gg)
        don = do2 * (gg * sg)
        dg_ref[...] = (do2 * (xn * gn) * (sg * (1.0 + gg * (1.0 - sg)))).astype(MXU_DTYPE)
        dgn = jnp.sum(don * xn, axis=0, keepdims=True)
        dxn = don * gn
        dos = []
        for h in range(NH):
            dh = dxn[:, h * HV:(h + 1) * HV]
            dos.append(rs[h] * (dh - xs[h] * jnp.mean(dh * xs[h], axis=-1, keepdims=True)))
        do_ref[...] = jnp.concatenate(dos, axis=1).astype(MXU_DTYPE)
        yc = yc_ref[...]
        mu = jnp.mean(yc, axis=-1, keepdims=True)
        yd = yc - mu
        rstd = lax.rsqrt(jnp.mean(yd * yd, axis=-1, keepdims=True) + EPS)
        yhat = yd * rstd
        lg = vc_ref[2:3, :]
        ln = yhat * lg + vc_ref[3:4, :]
        sl = _sigmoid(ln)
        dln = dconv_o * (sl * (1.0 + ln * (1.0 - sl)))
        dlb = jnp.sum(dln, axis=0, keepdims=True)
        dlg = jnp.sum(dln * yhat, axis=0, keepdims=True)
        dyh = dln * lg
        dyc = rstd * (dyh - jnp.mean(dyh, axis=-1, keepdims=True)
                      - yhat * jnp.mean(dyh * yhat, axis=-1, keepdims=True))
        dcb = jnp.sum(dyc, axis=0, keepdims=True)
        a = u_ref[:, 0:DC]
        gate = u_ref[:, DC:2 * DC]
        sgt = _sigmoid(gate)
        vv = a * sgt
        pos = _seg_pos(tt)
        cw = cw_ref[...]
        dvv = jnp.zeros((tt, DC), F32)
        dws = []
        for j in range(CW):
            shifted_dyc = _shifted(dyc, CPAD - j, pos, tt)
            dvv = dvv + shifted_dyc * cw[j:j + 1, :]
            dws.append(jnp.sum(shifted_dyc * vv, axis=0, keepdims=True))
        dws.append(jnp.zeros((1, DC), F32))
        du_ref[:, 0:DC] = (dvv * sgt).astype(MXU_DTYPE)
        du_ref[:, DC:2 * DC] = (dvv * a * sgt * (1.0 - sgt)).astype(MXU_DTYPE)
        _acc_rows(acc1_ref, [dg1])
        _acc_rows(acc2_ref, [dgn, dcb, dlg, dlb])
        dcw_ref[...] += jnp.concatenate(dws, axis=0)

    return pl.pallas_call(
        body, name="merge_bwd", grid=(t // tt,),
        in_specs=[_rows(tt, D), _rows(tt, D), _rows(tt, 1024), _rows(tt, DV), _rows(tt, DV), _rows(tt, DV),
                  _rows(tt, DC),
                  _full((8, D)), _full((8, DC)), _full((32, DC)), _full((D, D))],
        out_specs=[_rows(tt, 1024), _rows(tt, DV), _rows(tt, DV), _rows(tt, D), _full((8, D)), _full((8, DC)),
                   _full((32, DC))],
        out_shape=[_sds((t, 1024), MXU_DTYPE), _sds((t, DV), MXU_DTYPE), _sds((t, DV), MXU_DTYPE),
                   _sds((t, D), MXU_DTYPE), _sds((8, D)),
                   _sds((8, DC)), _sds((32, DC))],
        compiler_params=_cparams(1),
    )(dx1, mix, u, g, o_f, o_b, yc, vec, vc, convw, w_out)


FN = DFF // 2


def ffn_gate_up(h2, wg_t, wu_t, tt):
    t = h2.shape[0]

    def body(h2_ref, wg_ref, wu_ref, s_ref, d_ref, hid_ref):
        h2v = h2_ref[...]
        gt = _mm_nt(h2v, wg_ref[...])
        up = _mm_nt(h2v, wu_ref[...])
        sg = _sigmoid(gt)
        act = gt * sg
        s_ref[...] = act.astype(MXU_DTYPE)
        d_ref[...] = (up * (sg * (1.0 + gt * (1.0 - sg)))).astype(MXU_DTYPE)
        hid_ref[...] = (act * up).astype(MXU_DTYPE)

    blk = pl.BlockSpec((tt, FN), lambda j, i: (i, j))
    wblk = pl.BlockSpec((FN, D), lambda j, i: (j, 0))
    return pl.pallas_call(
        body, name="ffn_gate_up", grid=(2, t // tt),
        in_specs=[pl.BlockSpec((tt, D), lambda j, i: (i, 0)), wblk, wblk],
        out_specs=[blk, blk, blk],
        out_shape=[_sds((t, DFF), MXU_DTYPE)] * 3,
        compiler_params=_cparams(2),
    )(h2, wg_t, wu_t)


def ffn_down_loss(hid, x1, tgt, vec, w_down, tt):
    t = x1.shape[0]

    def body(hid_ref, x1_ref, tgt_ref, vec_ref, w_ref, dx2_ref, dff_ref, acc_ref):
        _zero_first(acc_ref)
        g2 = vec_ref[0:1, :]
        fg = vec_ref[1:2, :]
        ff = _mm(hid_ref[...], w_ref[...])
        x2 = x1_ref[...] + g2 * ff
        rf = lax.rsqrt(jnp.mean(x2 * x2, axis=-1, keepdims=True) + EPS)
        xn = x2 * rf
        err = xn * fg - tgt_ref[...]
        dy = err * (1.0 / D)
        dfg = jnp.sum(dy * xn, axis=0, keepdims=True)
        dxn = dy * fg
        dx2 = rf * (dxn - xn * jnp.mean(dxn * xn, axis=-1, keepdims=True))
        dx2_ref[...] = dx2
        dff_ref[...] = (g2 * dx2).astype(MXU_DTYPE)
        dg2 = jnp.sum(dx2 * ff, axis=0, keepdims=True)
        loss = jnp.sum(err * err, axis=0, keepdims=True) * (0.5 / D)
        _acc_rows(acc_ref, [dg2, dfg, loss])

    return pl.pallas_call(
        body, name="ffn_down_loss", grid=(t // tt,),
        in_specs=[_rows(tt, DFF), _rows(tt, D), _rows(tt, D), _full((8, D)), _full((DFF, D))],
        out_specs=[_rows(tt, D), _rows(tt, D), _full((8, D))],
        out_shape=[_sds((t, D)), _sds((t, D), MXU_DTYPE), _sds((8, D))],
        compiler_params=_cparams(1),
    )(hid, x1, tgt, vec, w_down)


def ffn_dhid(dff, s, d, w_down, tt):
    t = dff.shape[0]

    def body(dff_ref, s_ref, d_ref, w_ref, dgt_ref, dup_ref):
        dhid = _mm_nt(dff_ref[...], w_ref[...])
        dgt_ref[...] = (dhid * d_ref[...].astype(F32)).astype(MXU_DTYPE)
        dup_ref[...] = (dhid * s_ref[...].astype(F32)).astype(MXU_DTYPE)

    blk = pl.BlockSpec((tt, FN), lambda j, i: (i, j))
    return pl.pallas_call(
        body, name="ffn_dhid", grid=(2, t // tt),
        in_specs=[pl.BlockSpec((tt, D), lambda j, i: (i, 0)), blk, blk, pl.BlockSpec((FN, D), lambda j, i: (j, 0))],
        out_specs=[blk, blk],
        out_shape=[_sds((t, DFF), MXU_DTYPE), _sds((t, DFF), MXU_DTYPE)],
        compiler_params=_cparams(2),
    )(dff, s, d, w_down)


def ffn_dh2(dgt, dup, x1, dx2, vec, wg_t, wu_t, tt):
    t = x1.shape[0]

    def body(dgt_ref, dup_ref, x1_ref, dx2_ref, vec_ref, wg_ref, wu_ref, dx1_ref, acc_ref):
        _zero_first(acc_ref)
        dh2 = _mm(dgt_ref[...], wg_ref[...]) + _mm(dup_ref[...], wu_ref[...])
        g, sc = vec_ref[0:1, :], vec_ref[2:3, :]
        r, xn, yy, _ = _norm_mod(x1_ref[...], g, vec_ref[1:2, :], sc)
        dsh, dsc, dgn, dx = _norm_mod_bwd(dh2, r, xn, yy, g, sc)
        dx1_ref[...] = dx2_ref[...] + dx
        _acc_rows(acc_ref, [dsh, dsc, dgn])

    return pl.pallas_call(
        body, name="ffn_dh2", grid=(t // tt,),
        in_specs=[_rows(tt, DFF), _rows(tt, DFF), _rows(tt, D), _rows(tt, D), _full((8, D)), _full((DFF, D)),
                  _full((DFF, D))],
        out_specs=[_rows(tt, D), _full((8, D))],
        out_shape=[_sds((t, D)), _sds((8, D))],
        compiler_params=_cparams(1),
    )(dgt, dup, x1, dx2, vec, wg_t, wu_t)


def tn_matmul(a, b, bm, bt, init=None):
    t, m = a.shape
    n = b.shape[1]
    nk = t // bt

    def body(*refs):
        if init is None:
            a_ref, b_ref, o_ref, wire_ref = refs
        else:
            a_ref, b_ref, i_ref, o_ref, wire_ref = refs
        @pl.when(pl.program_id(1) == 0)
        def _():
            o_ref[...] = jnp.zeros_like(o_ref) if init is None else i_ref[...]

        o_ref[...] += _mm_tn(a_ref[...], b_ref[...])

        @pl.when(pl.program_id(1) == nk - 1)
        def _():
            wire_ref[...] = o_ref[...].astype(WIRE_DTYPE)

    in_specs = [pl.BlockSpec((bt, bm), lambda i, k: (k, i)), pl.BlockSpec((bt, n), lambda i, k: (k, 0))]
    args = [a, b]
    if init is not None:
        in_specs.append(pl.BlockSpec((bm, n), lambda i, k: (i, 0)))
        args.append(init)
    oblk = pl.BlockSpec((bm, n), lambda i, k: (i, 0))
    return pl.pallas_call(
        body, name="tn_matmul", grid=(m // bm, nk),
        in_specs=in_specs, out_specs=[oblk, oblk],
        out_shape=[_sds((m, n)), _sds((m, n), WIRE_DTYPE)], compiler_params=_cparams(2),
    )(*args)


def _adamw(w, g, m, v):
    m = ADAM_B1 * m + (1.0 - ADAM_B1) * g
    v = ADAM_B2 * v + (1.0 - ADAM_B2) * (g * g)
    m_hat = m / (1.0 - ADAM_B1 ** ADAM_STEP)
    v_hat = v / (1.0 - ADAM_B2 ** ADAM_STEP)
    delta = -ADAM_LR * (m_hat / (jnp.sqrt(v_hat) + ADAM_EPS) + ADAM_WD * w)
    return delta, m, v


def adamw_sharded(own, recv, w, m, v):
    shape = w.shape

    def body(own_ref, recv_ref, w_ref, m_ref, v_ref, g_ref, d_ref, mo_ref, vo_ref):
        g = own_ref[...]
        for k in range(N_DEV - 1):
            g = g + recv_ref[k].astype(F32)
        g_ref[...] = g
        d_ref[...], mo_ref[...], vo_ref[...] = _adamw(w_ref[...], g, m_ref[...], v_ref[...])

    return pl.pallas_call(
        body, name="adamw_sharded",
        in_specs=[_full(shape), _full((N_DEV - 1,) + shape), _full(shape), _full(shape), _full(shape)],
        out_specs=[_full(shape)] * 4, out_shape=[_sds(shape)] * 4, grid=(1,),
        compiler_params=_cparams(1),
    )(own, recv, w, m, v)


def adamw_small(items):
    n = len(items)
    flat = [a for it in items for a in it]

    def body(*refs):
        ins, outs = refs[:4 * n], refs[4 * n:]
        for i in range(n):
            g, w, m, v = (r[...] for r in ins[4 * i:4 * i + 4])
            outs[3 * i][...], outs[3 * i + 1][...], outs[3 * i + 2][...] = _adamw(w, g, m, v)

    out = pl.pallas_call(
        body, name="adamw_small", grid=(1,),
        in_specs=[_full(a.shape) for a in flat],
        out_specs=[_full(it[1].shape) for it in items for _ in range(3)],
        out_shape=[_sds(it[1].shape) for it in items for _ in range(3)],
        compiler_params=_cparams(1),
    )(*flat)
    return [tuple(out[3 * i:3 * i + 3]) for i in range(n)]


def _mesh_pos():
    x, y, c = lax.axis_index("x"), lax.axis_index("y"), lax.axis_index("c")
    me = 4 * x + 2 * y + c
    peers = []
    for k in range(1, N_DEV):
        peers.append(((1 - x) if (k >> 2) & 1 else x, (1 - y) if (k >> 1) & 1 else y, (1 - c) if k & 1 else c))
    return me, peers


def _all_gather_issue(buf, send_sems, recv_sems, me, peers):
    sends = []
    for k, peer in enumerate(peers):
        cp = pltpu.make_async_remote_copy(src_ref=buf.at[me], dst_ref=buf.at[me], send_sem=send_sems.at[k],
                                          recv_sem=recv_sems.at[k], device_id=peer, device_id_type=MESH)
        cp.start()
        sends.append(cp)
    return sends


def _all_gather_finish(buf, send_sems, recv_sems, me, peers, sends):
    for k, peer in enumerate(peers):
        src = jnp.bitwise_xor(me, k + 1)
        pltpu.make_async_remote_copy(src_ref=buf.at[src], dst_ref=buf.at[src], send_sem=send_sems.at[k],
                                     recv_sem=recv_sems.at[k], device_id=peer, device_id_type=MESH).wait_recv()
    for cp in sends:
        cp.wait_send()


def _all_gather(buf, send_sems, recv_sems, me, peers):
    _all_gather_finish(buf, send_sems, recv_sems, me, peers, _all_gather_issue(buf, send_sems, recv_sems, me, peers))


_VMEM = pl.BlockSpec(memory_space=pltpu.VMEM)
_ANY = pl.BlockSpec(memory_space=pl.ANY)
_SEMS = pltpu.SemaphoreType.DMA((N_DEV - 1,))


def mod_forward(c, c_ctx, w_mod_sh, b_mod, sm_pack, w_first):
    ncol = w_mod_sh.shape[1]

    def body(c_ref, cc_ref, w_ref, b_ref, sm_ref, wf_ref, mod_ref, s_ref, smt_ref, wall_ref, cbuf, pbuf, smbuf,
             s1, r1, s2, r2, s3, r3, ws, wr, wl):
        me, peers = _mesh_pos()
        x, y, cc = lax.axis_index("x"), lax.axis_index("y"), lax.axis_index("c")
        sibling = (x, y, 1 - cc)
        chips = [(1 - x, y), (x, 1 - y), (1 - x, 1 - y)]
        slot = lambda px, py, pc: wall_ref.at[4 * px + 2 * py + pc]

        def wcopy(k, block, to, src=None):
            return pltpu.make_async_remote_copy(
                src_ref=slot(*block) if src is None else src, dst_ref=slot(*block), send_sem=ws.at[k],
                recv_sem=wr.at[k], device_id=to, device_id_type=MESH)

        cbuf[me] = jnp.broadcast_to(c_ref[...], (8, D))
        c_sends = _all_gather_issue(cbuf, s1, r1, me, peers)
        mine = pltpu.make_async_copy(wf_ref, slot(x, y, cc), wl)
        mine.start()
        first = [wcopy(0, (x, y, cc), sibling, src=wf_ref)]
        first += [wcopy(1 + j, (x, y, cc), (*chip, cc), src=wf_ref) for j, chip in enumerate(chips)]
        for cp in first:
            cp.start()
        smbuf[me] = sm_ref[...]
        sm_sends = _all_gather_issue(smbuf, s3, r3, me, peers)
        _all_gather_finish(cbuf, s1, r1, me, peers, c_sends)
        rows = [cbuf[j, 0:1, :] for j in range(N_DEV)] + [cc_ref[...], jnp.zeros((7, D), F32)]
        sx = jnp.concatenate(rows, axis=0)
        s = sx * _sigmoid(sx)
        s_ref[...] = s
        pbuf[me] = _hi(s, w_ref[...])
        _all_gather(pbuf, s2, r2, me, peers)
        for j in range(N_DEV):
            mod_ref[:, j * ncol:(j + 1) * ncol] = pbuf[j] + b_ref[:, j * ncol:(j + 1) * ncol]
        _all_gather_finish(smbuf, s3, r3, me, peers, sm_sends)
        tot = smbuf[0]
        for j in range(1, N_DEV):
            tot = tot + smbuf[j]
        smt_ref[...] = tot
        passed = [wcopy(4 + j, (*chip, cc), sibling) for j, chip in enumerate(chips)]
        for j, chip in enumerate(chips):
            wcopy(1 + j, (*chip, cc), (x, y, cc)).wait_recv()
            passed[j].start()
        wcopy(0, (x, y, 1 - cc), (x, y, cc)).wait_recv()
        for j, chip in enumerate(chips):
            wcopy(4 + j, (*chip, 1 - cc), (x, y, cc)).wait_recv()
        for cp in first + passed:
            cp.wait_send()
        mine.wait()

    return pl.pallas_call(
        body, name="mod_forward",
        in_specs=[_VMEM] * 6, out_specs=[_VMEM] * 4,
        out_shape=[_sds((16, N_DEV * ncol)), _sds((16, D)), _sds(sm_pack.shape),
                   _sds((N_DEV,) + w_first.shape, w_first.dtype)],
        scratch_shapes=[pltpu.VMEM((N_DEV, 8, D), F32), pltpu.VMEM((N_DEV, 16, ncol), F32),
                        pltpu.VMEM((N_DEV,) + sm_pack.shape, F32), _SEMS, _SEMS, _SEMS, _SEMS, _SEMS, _SEMS,
                        _SEMS, _SEMS, pltpu.SemaphoreType.DMA],
        compiler_params=pltpu.CompilerParams(vmem_limit_bytes=VMEM_LIMIT),
    )(c, c_ctx, w_mod_sh, b_mod, sm_pack, w_first)


def sum_blocks(gat, loss_row):
    def body(g_ref, tot_ref, loss_ref):
        tot = g_ref[0]
        for j in range(1, N_DEV):
            tot = tot + g_ref[j]
        tot_ref[...] = tot
        loss_ref[...] = jnp.sum(tot[loss_row:loss_row + 1, :], axis=1, keepdims=True)

    return pl.pallas_call(
        body, name="sum_blocks", in_specs=[_VMEM], out_specs=[_VMEM, _VMEM],
        out_shape=[_sds(gat.shape[1:]), _sds((1, 1))],
        compiler_params=pltpu.CompilerParams(vmem_limit_bytes=VMEM_LIMIT),
    )(gat)


def mod_backward(s, dm_sh, w, m, v, cc, m_cc, v_cc):
    shape = w.shape

    def body(s_ref, dm_ref, w_ref, m_ref, v_ref, cc_ref, mcc_ref, vcc_ref,
             gw_ref, dw_ref, mw_ref, vw_ref, gc_ref, dc_ref, mc_ref, vc_ref, pbuf, send_sems, recv_sems):
        me, peers = _mesh_pos()
        wv = w_ref[...]
        pbuf[me] = _hi_nt(dm_ref[8:16, :], wv)
        _all_gather(pbuf, send_sems, recv_sems, me, peers)
        g = _hi_tn(s_ref[...], dm_ref[...])
        gw_ref[...] = g
        dw_ref[...], mw_ref[...], vw_ref[...] = _adamw(wv, g, m_ref[...], v_ref[...])
        tot = pbuf[0]
        for j in range(1, N_DEV):
            tot = tot + pbuf[j]
        ccv = cc_ref[...]
        sg = _sigmoid(ccv)
        gc = tot[0:1, :] * (sg * (1.0 + ccv * (1.0 - sg)))
        gc_ref[...] = gc
        dc_ref[...], mc_ref[...], vc_ref[...] = _adamw(ccv, gc, mcc_ref[...], vcc_ref[...])

    return pl.pallas_call(
        body, name="mod_backward", in_specs=[_VMEM] * 8, out_specs=[_VMEM] * 8,
        out_shape=[_sds(shape)] * 4 + [_sds((1, D))] * 4,
        scratch_shapes=[pltpu.VMEM((N_DEV, 8, D), F32), _SEMS, _SEMS],
        compiler_params=pltpu.CompilerParams(vmem_limit_bytes=VMEM_LIMIT),
    )(s, dm_sh, w, m, v, cc, m_cc, v_cc)


_HBM = pl.BlockSpec(memory_space=pltpu.HBM)
_SEM = pl.BlockSpec(memory_space=pltpu.SEMAPHORE)
_EFFECT = pltpu.SideEffectType.DATAFLOW_SIDE_EFFECTING
_hbm = lambda a: pltpu.with_memory_space_constraint(a, pltpu.HBM)


def gather_start(shards, me, tag):
    n = len(shards)
    sems = pltpu.SemaphoreType.DMA((7 * n,))
    lands = [lax.dynamic_update_slice(lax.empty((N_DEV,) + s.shape, s.dtype), s[None], (me, 0, 0)) for s in shards]

    def body(*refs):
        s_refs, l_refs = refs[:n], refs[n:2 * n]
        send_sems, recv_sems = refs[2 * n], refs[2 * n + 1]
        token = refs[-1]
        my, peers = _mesh_pos()
        for w in range(n):
            for k, peer in enumerate(peers):
                pltpu.make_async_remote_copy(
                    src_ref=s_refs[w], dst_ref=l_refs[w].at[my], send_sem=send_sems.at[w * 7 + k],
                    recv_sem=recv_sems.at[w * 7 + k], device_id=peer, device_id_type=MESH).start()
        token[...] = jnp.zeros_like(token)

    out = pl.pallas_call(
        body, name="gather_start_" + tag,
        out_shape=(sems, sems) + tuple(pltpu.HBM(a.shape, a.dtype) for a in list(shards) + lands) + (_sds((8, 128)),),
        in_specs=(_HBM,) * (2 * n), out_specs=(_SEM, _SEM) + (_HBM,) * (2 * n) + (_VMEM,),
        input_output_aliases={i: i + 2 for i in range(2 * n)},
        compiler_params=pltpu.CompilerParams(has_side_effects=_EFFECT),
    )(*[_hbm(a) for a in list(shards) + lands])
    return out[0], out[1], list(out[2:2 + n]), list(out[2 + n:2 + 2 * n]), out[-1]


def gather_wait(send_sems, recv_sems, shards, lands, after, tag):
    n = len(shards)

    def body(*refs):
        s_refs, l_refs = refs[:n], refs[n:2 * n]
        send_sems, recv_sems = refs[2 * n], refs[2 * n + 1]
        my, peers = _mesh_pos()
        for w in range(n):
            for k, peer in enumerate(peers):
                src = jnp.bitwise_xor(my, k + 1)
                cp = pltpu.make_async_remote_copy(
                    src_ref=s_refs[w], dst_ref=l_refs[w].at[src], send_sem=send_sems.at[w * 7 + k],
                    recv_sem=recv_sems.at[w * 7 + k], device_id=peer, device_id_type=MESH)
                cp.wait_send()
                cp.wait_recv()

    out = pl.pallas_call(
        body, name="gather_wait_" + tag,
        out_shape=tuple(pltpu.HBM(a.shape, a.dtype) for a in list(shards) + list(lands)),
        in_specs=(_HBM,) * (2 * n) + (_SEM, _SEM, _ANY), out_specs=(_HBM,) * (2 * n),
        input_output_aliases={i: i for i in range(2 * n)},
        compiler_params=pltpu.CompilerParams(has_side_effects=_EFFECT),
    )(*shards, *lands, send_sems, recv_sems, after)
    return list(out[n:2 * n])


def scatter_start(grads, tag):
    n = len(grads)
    sems = pltpu.SemaphoreType.DMA((7 * n,))
    lands = [lax.empty((N_DEV - 1,) + g.shape[1:], g.dtype) for g in grads]

    def body(*refs):
        g_refs, l_refs = refs[:n], refs[n:2 * n]
        send_sems, recv_sems = refs[2 * n], refs[2 * n + 1]
        token = refs[-1]
        me, peers = _mesh_pos()
        for w in range(n):
            for k, peer in enumerate(peers):
                dst = jnp.bitwise_xor(me, k + 1)
                pltpu.make_async_remote_copy(
                    src_ref=g_refs[w].at[dst], dst_ref=l_refs[w].at[k], send_sem=send_sems.at[w * 7 + k],
                    recv_sem=recv_sems.at[w * 7 + k], device_id=peer, device_id_type=MESH).start()
        token[...] = jnp.zeros_like(token)

    out = pl.pallas_call(
        body, name="scatter_start_" + tag,
        out_shape=(sems, sems) + tuple(pltpu.HBM(a.shape, a.dtype) for a in list(grads) + lands) + (_sds((8, 128)),),
        in_specs=(_HBM,) * (2 * n), out_specs=(_SEM, _SEM) + (_HBM,) * (2 * n) + (_VMEM,),
        input_output_aliases={i: i + 2 for i in range(2 * n)},
        compiler_params=pltpu.CompilerParams(has_side_effects=_EFFECT),
    )(*[_hbm(a) for a in list(grads) + lands])
    return out[0], out[1], list(out[2:2 + n]), list(out[2 + n:2 + 2 * n]), out[-1]


def scatter_wait(send_sems, recv_sems, grads, lands, after, tag):
    n = len(grads)

    def body(*refs):
        g_refs, l_refs = refs[:n], refs[n:2 * n]
        send_sems, recv_sems = refs[2 * n], refs[2 * n + 1]
        me, peers = _mesh_pos()
        for w in range(n):
            for k, peer in enumerate(peers):
                dst = jnp.bitwise_xor(me, k + 1)
                cp = pltpu.make_async_remote_copy(
                    src_ref=g_refs[w].at[dst], dst_ref=l_refs[w].at[k], send_sem=send_sems.at[w * 7 + k],
                    recv_sem=recv_sems.at[w * 7 + k], device_id=peer, device_id_type=MESH)
                cp.wait_send()
                cp.wait_recv()

    out = pl.pallas_call(
        body, name="scatter_wait_" + tag,
        out_shape=tuple(pltpu.HBM(a.shape, a.dtype) for a in list(grads) + list(lands)),
        in_specs=(_HBM,) * (2 * n) + (_SEM, _SEM, _ANY), out_specs=(_HBM,) * (2 * n),
        input_output_aliases={i: i for i in range(2 * n)},
        compiler_params=pltpu.CompilerParams(has_side_effects=_EFFECT),
    )(*grads, *lands, send_sems, recv_sems, after)
    return list(out[n:2 * n])


def _vec8(rows, width):
    rid = lax.broadcasted_iota(jnp.int32, (8, width), 0)
    out = jnp.zeros((8, width), F32)
    for i, r in enumerate(rows):
        r = r.reshape(-1)
        r = jnp.pad(r, (0, width - r.shape[0]))
        out = jnp.where(rid == i, r[None, :], out)
    return out


def local_step(x, ctx, tgt, mod, mod_c, small, w_int, start, late_weights, grads_ready, small_ready, tt, tt_ctx, cb,
               cb_ctx):
    sh1, sc1, g1, sh2, sc2, g2 = [mod[i * D:(i + 1) * D] for i in range(6)]
    csh1, csc1 = mod_c[0:D], mod_c[D:2 * D]
    vec1 = _vec8([small["norm1_g"], sh1, sc1], D)
    vec1c = _vec8([small["norm1_g"], csh1, csc1], D)
    vec2 = _vec8([small["norm2_g"], sh2, sc2], D)
    vec3 = _vec8([g2, small["final_g"]], D)
    vecm = _vec8([g1, small["norm2_g"], sh2, sc2], D)
    vcm = _vec8([jnp.tile(small["gla_norm_g"].reshape(HV), NH), small["conv_b"], small["conv_ln_g"],
                 small["conv_ln_b"]], DC)
    convw = jnp.pad(small["conv_w"], ((0, 1), (0, 0)))
    wa = jnp.zeros((128, 512), F32)
    wa = wa.at[0:RANK, 0:DK].set(small["w_a2_f"]).at[RANK:2 * RANK, DK:2 * DK].set(small["w_a2_b"])
    ba = jnp.concatenate([small["b_a_f"].reshape(1, DK), small["b_a_b"].reshape(1, DK)], axis=1)

    _, _, kc, vc_, _, rc, lac, hc = proj_fwd(ctx, vec1c + start, w_int, wa, ba, tt_ctx)
    qc0 = jnp.zeros_like(kc)
    _, _, sallf_c, sallb_c, sfin_c = gla_fwd(qc0, kc, vc_, lac, jnp.zeros((2, HV, DK), F32), cb_ctx)
    u, q, k, v, g, r, la, h = proj_fwd(x, vec1, w_int, wa, ba, tt)
    o_f, o_b, sall_f, sall_b, _ = gla_fwd(q, k, v, la, sfin_c, cb)
    w_out, wg_t, wu_t, w_down = late_weights(o_b)
    x1, cat, mix, yc, h2 = merge_fwd(u, g, o_f, o_b, x, vecm, vcm, convw, w_out, tt)
    tt2 = min(2 * tt, x.shape[0])
    act, dact, hid = ffn_gate_up(h2, wg_t, wu_t, tt2)
    dx2, dff, acc3 = ffn_down_loss(hid, x1, tgt, vec3, w_down, tt)
    dgt, dup = ffn_dhid(dff, act, dact, w_down, tt2)
    dx1, acc2 = ffn_dh2(dgt, dup, x1, dx2, vec2, wg_t, wu_t, tt)
    bt = min(GRAD_TOKEN_BLOCK, x.shape[0])
    gw = {"w_down": tn_matmul(hid, dff, FN, bt), "wg_t": tn_matmul(dgt, h2, FN, bt),
          "wu_t": tn_matmul(dup, h2, FN, bt)}
    vecm = vecm + grads_ready(("wg_t", "wu_t", "w_down"), gw)
    du, dg, do, dmix, accm1, accm2, dconvw = merge_bwd(dx1, mix, u, g, o_f, o_b, yc, vecm, vcm, convw, w_out, tt)
    gw["w_out"] = tn_matmul(cat, dmix, D, bt)
    dsfin = jnp.zeros((2, HV, DK), F32) + grads_ready(("w_out",), gw)
    dqf, dkf, dvf, dlaf, dqb, dkb, dvb, dlab, ds0 = gla_bwd(q, k, v, la, do, sall_f, sall_b, dsfin, cb)
    gx, dp, acc1, dba, dwa = proj_bwd(du, (dqf, dqb, dkf, dkb, dvf, dvb), dg, dlaf, dlab, la, r, x, dx1, vec1, w_int,
                                      wa, tt)
    tcx = ctx.shape[0]
    zc = lambda w, dt=MXU_DTYPE: jnp.zeros((tcx, w), dt)
    _, dkf, dvf, dlaf, _, dkb, dvb, dlab, _ = gla_bwd(qc0, kc, vc_, lac, zc(DV), sallf_c, sallb_c, ds0, cb_ctx)
    _, dpc, acc1c, dbac, dwac = proj_bwd(zc(1024), (zc(DK), zc(DK), dkf, dkb, dvf, dvb), zc(DV), dlaf, dlab, lac, rc,
                                         ctx, zc(D, F32), vec1c, w_int, wa, tt_ctx)
    dwa_t = dwa + dwac
    dba_t = dba + dbac
    gs = {
        "norm1_g": acc1[2] + acc1c[2], "norm2_g": acc2[2], "final_g": acc3[1], "loss": acc3[2],
        "gla_norm_g": accm2[0], "conv_b": accm2[1], "conv_ln_g": accm2[2], "conv_ln_b": accm2[3],
        "conv_w": dconvw, "b_a": dba_t[0], "w_a2": dwa_t,
    }
    dmod = _vec8([acc1[0], acc1[1], accm1[0], acc2[0], acc2[1], acc3[0]], D)
    dmod_c = _vec8([acc1c[0], acc1c[1]], D)
    dpc = dpc + small_ready(gs, dmod, dmod_c).astype(dpc.dtype)
    btc = min(GRAD_TOKEN_BLOCK, tcx)
    gw["w_int"] = tn_matmul(dp, h, 896, bt, init=tn_matmul(dpc, hc, 896, btc)[0])
    grads_ready(("w_int",), gw)
    return gx, gw


PACK_ROWS = 96
ROW_N1, ROW_N2, ROW_FG, ROW_LOSS, ROW_GN, ROW_CB, ROW_LG, ROW_LB, ROW_BA = 0, 1, 2, 3, 4, 5, 6, 7, 8
ROW_DMOD, ROW_DMODC, ROW_CW, ROW_WA = 16, 24, 32, 64


def _pack_small(gs, dmod, dmod_c):
    pad = lambda a: jnp.pad(a, ((0, 0), (0, D - a.shape[1])))
    singles = _vec8([gs["norm1_g"], gs["norm2_g"], gs["final_g"], gs["loss"], gs["gla_norm_g"], gs["conv_b"],
                     gs["conv_ln_g"], gs["conv_ln_b"]], D)
    return jnp.concatenate([singles, _vec8([gs["b_a"]], D), dmod, dmod_c, pad(gs["conv_w"]), pad(gs["w_a2"][0:32])],
                           axis=0)


def kernel(x, c, ctx, c_ctx, w_mod, b_mod, norm1_g, norm2_g, w_in, conv_w, conv_b, conv_ln_g, conv_ln_b, w_a2_f, b_a_f, w_a2_b, b_a_b, gla_norm_g, w_out, w_gate, w_up, w_down, final_g, loss_target, m_c_ctx, m_w_mod, m_b_mod, m_norm1_g, m_norm2_g, m_w_in, m_conv_w, m_conv_b, m_conv_ln_g, m_conv_ln_b, m_w_a2_f, m_b_a_f, m_w_a2_b, m_b_a_b, m_gla_norm_g, m_w_out, m_w_gate, m_w_up, m_w_down, m_final_g, v_c_ctx, v_w_mod, v_b_mod, v_norm1_g, v_norm2_g, v_w_in, v_conv_w, v_conv_b, v_conv_ln_g, v_conv_ln_b, v_w_a2_f, v_b_a_f, v_w_a2_b, v_b_a_b, v_gla_norm_g, v_w_out, v_w_gate, v_w_up, v_w_down, v_final_g):
    me = 4 * lax.axis_index("x") + 2 * lax.axis_index("y") + lax.axis_index("c")
    t = x.shape[1]
    tcx = ctx.shape[1]
    r_in, r_out, r_ff = w_in.shape[2], w_out.shape[1], w_gate.shape[2]
    r_in_b = -(-r_in // 16) * 16

    tb = lambda w: w.T.astype(MXU_DTYPE)

    small = dict(norm1_g=norm1_g[0], norm2_g=norm2_g[0], final_g=final_g, gla_norm_g=gla_norm_g[0],
                 conv_b=conv_b[0], conv_ln_g=conv_ln_g[0], conv_ln_b=conv_ln_b[0], b_a_f=b_a_f[0], b_a_b=b_a_b[0])
    sm_pack = jnp.zeros((48, DC), F32)
    sm_pack = lax.dynamic_update_slice(sm_pack, conv_w[0], (0, me * (DC // N_DEV)))
    sm_pack = lax.dynamic_update_slice(sm_pack, w_a2_f[0], (32, me * (DK // N_DEV)))
    sm_pack = lax.dynamic_update_slice(sm_pack, w_a2_b[0], (32, DK + me * (DK // N_DEV)))

    mod_all, s_all, sm_tot, wall = mod_forward(c, c_ctx.reshape(1, D), w_mod[0], b_mod, sm_pack,
                                               jnp.pad(tb(w_in[0]), ((0, r_in_b - r_in), (0, 0))))
    mod = lax.dynamic_slice(mod_all, (me, 0), (1, 6 * D)).reshape(6 * D)
    mod_c = mod_all[8]
    small["conv_w"] = sm_tot[0:CW, :]
    small["w_a2_f"] = sm_tot[32:32 + RANK, 0:DK]
    small["w_a2_b"] = sm_tot[32:32 + RANK, DK:2 * DK]

    w_int =jnp.pad(wall[:, 0:r_in, :].reshape(N_DEV * r_in, D), ((0, DINP - DIN), (0, 0)))
    after_w_in = (wall[0:1, 0:1, 0] * 0).astype(MXU_DTYPE)
    late = [w_out[0].astype(MXU_DTYPE) + after_w_in, tb(w_gate[0]) + after_w_in, tb(w_up[0]) + after_w_in,
            w_down[0].astype(MXU_DTYPE) + after_w_in]
    g_send, g_recv, late_thru, late_lands, g_token = gather_start(late, me, "late")

    def late_weights(after):
        got = gather_wait(g_send, g_recv, late_thru, late_lands, after, "late")
        return tuple(a.reshape(N_DEV * a.shape[1], D) for a in got)

    pad_in = lambda g: jnp.pad(g[0:DIN].reshape(N_DEV, r_in, D), ((0, 0), (0, r_in_b - r_in), (0, 0)))
    blocked = {"w_int": pad_in, "w_out": lambda g: g.reshape(N_DEV, r_out, D)}
    as_blocks = lambda n, g: blocked.get(n, lambda a: a.reshape(N_DEV, r_ff, D))(g)
    pending = []

    def grads_ready(names, gw_now):
        blocks = [as_blocks(n, gw_now[n][1]) for n in names]
        send, recv_s, thru, zones, token = scatter_start(blocks, names[0])
        pending.append((names, send, recv_s, thru, zones))
        if names[0] == "w_int":
            finish_small(token)
        return token[0:1, 0:1]

    sm = {}

    def small_ready(gs, dmod, dmod_c):
        sm["copy"] = gather_start([_pack_small(gs, dmod, dmod_c)], me, "small")
        return sm["copy"][4][0:1, 0:1]

    def finish_small(after):
        send, recv_s, thru, zones, _ = sm["copy"]
        gat = gather_wait(send, recv_s, thru, zones, after, "small")[0]
        sm["tot"], sm["loss"] = sum_blocks(gat, ROW_LOSS)
        sm["dm"] = jnp.concatenate(
            [gat[:, ROW_DMOD:ROW_DMOD + 6, :].reshape(N_DEV, 6 * D),
             jnp.pad(sm["tot"][ROW_DMODC:ROW_DMODC + 6, :].reshape(1, 6 * D), ((0, 7), (0, 0)))], axis=0)
        ncol = w_mod.shape[2]
        dm_sh = lax.dynamic_slice(sm["dm"], (0, me * ncol), (16, ncol))
        sm["mod"] = mod_backward(s_all, dm_sh, w_mod[0], m_w_mod[0], v_w_mod[0], c_ctx.reshape(1, D),
                                 m_c_ctx.reshape(1, D), v_c_ctx.reshape(1, D))
        return sm["mod"][4][0:1, 0:1] * 0

    gx, gw = local_step(x[0], ctx[0], loss_target[0], mod, mod_c, small, w_int, g_token[0:1, 0:1], late_weights,
                        grads_ready, small_ready, TOKEN_TILE, CTX_TOKEN_TILE, GLA_CHUNKS, CTX_GLA_CHUNKS)
    tot = sm["tot"]
    loss = sm["loss"].reshape(())
    g_wmod, d_wmod, nm_wmod, nv_wmod, g_cc, d_cc, nm_cc, nv_cc = sm["mod"]

    recv = {}

    def wait_for(entry, after):
        names, send, recv_s, thru, zones = entry
        recv.update(dict(zip(names, scatter_wait(send, recv_s, thru, zones, after, names[0]))))

    for entry in pending[:-1]:
        wait_for(entry, tot)
    own = {n: lax.dynamic_index_in_dim(as_blocks(n, gw[n][0]), me, 0, keepdims=False) for n in gw if n != "w_int"}
    own["w_int"] = jnp.pad(lax.dynamic_slice(gw["w_int"][0], (me * r_in, 0), (r_in, D)), ((0, r_in_b - r_in), (0, 0)))
    padt = lambda w: jnp.pad(w.T, ((0, r_in_b - r_in), (0, 0)))
    big = {}
    big["w_gate"] = [a.T for a in adamw_sharded(own["wg_t"], recv["wg_t"], w_gate[0].T, m_w_gate[0].T,
                                                 v_w_gate[0].T)]
    big["w_up"] = [a.T for a in adamw_sharded(own["wu_t"], recv["wu_t"], w_up[0].T, m_w_up[0].T, v_w_up[0].T)]
    big["w_down"] = adamw_sharded(own["w_down"], recv["w_down"], w_down[0], m_w_down[0], v_w_down[0])
    big["w_out"] = adamw_sharded(own["w_out"], recv["w_out"], w_out[0], m_w_out[0], v_w_out[0])
    wait_for(pending[-1], big["w_out"][0])
    big["w_in"] = [a[0:r_in].T for a in adamw_sharded(own["w_int"], recv["w_int"], padt(w_in[0]), padt(m_w_in[0]),
                                                       padt(v_w_in[0]))]
    big["w_mod"] = [g_wmod, d_wmod, nm_wmod, nv_wmod]

    row = lambda r, w: tot[r:r + 1, 0:w]
    gn_row = tot[ROW_GN:ROW_GN + 1, 0:DC]
    g_small = {
        "b_mod": (tot[ROW_DMOD:ROW_DMOD + 6] + tot[ROW_DMODC:ROW_DMODC + 6]).reshape(1, 6 * D),
        "norm1_g": row(ROW_N1, D), "norm2_g": row(ROW_N2, D),
        "conv_w": lax.dynamic_slice(tot, (ROW_CW, me * (DC // N_DEV)), (CW, DC // N_DEV)),
        "conv_b": row(ROW_CB, DC), "conv_ln_g": row(ROW_LG, DC), "conv_ln_b": row(ROW_LB, DC),
        "w_a2_f": lax.dynamic_slice(tot, (ROW_WA, me * (DK // N_DEV)), (RANK, DK // N_DEV)),
        "b_a_f": tot[ROW_BA:ROW_BA + 1, 0:DK],
        "w_a2_b": lax.dynamic_slice(tot, (ROW_WA + RANK, DK + me * (DK // N_DEV)), (RANK, DK // N_DEV)),
        "b_a_b": tot[ROW_BA:ROW_BA + 1, DK:2 * DK],
        "gla_norm_g": gn_row[:, 0:HV] + gn_row[:, HV:2 * HV] + gn_row[:, 2 * HV:3 * HV] + gn_row[:, 3 * HV:4 * HV],
        "final_g": row(ROW_FG, D),
    }
    wmv = {
        "b_mod": (b_mod, m_b_mod, v_b_mod), "norm1_g": (norm1_g, m_norm1_g, v_norm1_g),
        "norm2_g": (norm2_g, m_norm2_g, v_norm2_g), "conv_w": (conv_w[0], m_conv_w[0], v_conv_w[0]),
        "conv_b": (conv_b, m_conv_b, v_conv_b), "conv_ln_g": (conv_ln_g, m_conv_ln_g, v_conv_ln_g),
        "conv_ln_b": (conv_ln_b, m_conv_ln_b, v_conv_ln_b), "w_a2_f": (w_a2_f[0], m_w_a2_f[0], v_w_a2_f[0]),
        "b_a_f": (b_a_f, m_b_a_f, v_b_a_f), "w_a2_b": (w_a2_b[0], m_w_a2_b[0], v_w_a2_b[0]),
        "b_a_b": (b_a_b, m_b_a_b, v_b_a_b), "gla_norm_g": (gla_norm_g, m_gla_norm_g, v_gla_norm_g),
        "final_g": (final_g.reshape(1, D), m_final_g.reshape(1, D), v_final_g.reshape(1, D)),
    }
    names_small = list(g_small)
    upd = adamw_small([(g_small[n],) + wmv[n] for n in names_small])
    res = {n: (g_small[n],) + upd[i] for i, n in enumerate(names_small)}
    res["c_ctx"] = (g_cc, d_cc, nm_cc, nv_cc)
    for n in ("w_mod", "w_in", "w_out", "w_gate", "w_up", "w_down"):
        res[n] = tuple(big[n])

    order = ["c_ctx", "w_mod", "b_mod", "norm1_g", "norm2_g", "w_in", "conv_w", "conv_b", "conv_ln_g", "conv_ln_b",
             "w_a2_f", "b_a_f", "w_a2_b", "b_a_b", "gla_norm_g", "w_out", "w_gate", "w_up", "w_down", "final_g"]
    shapes = {"c_ctx": c_ctx.shape, "w_mod": w_mod.shape, "b_mod": b_mod.shape, "norm1_g": norm1_g.shape,
              "norm2_g": norm2_g.shape, "w_in": w_in.shape, "conv_w": conv_w.shape, "conv_b": conv_b.shape,
              "conv_ln_g": conv_ln_g.shape, "conv_ln_b": conv_ln_b.shape, "w_a2_f": w_a2_f.shape,
              "b_a_f": b_a_f.shape, "w_a2_b": w_a2_b.shape, "b_a_b": b_a_b.shape, "gla_norm_g": gla_norm_g.shape,
              "w_out": w_out.shape, "w_gate": w_gate.shape, "w_up": w_up.shape, "w_down": w_down.shape,
              "final_g": final_g.shape}
    outs = [loss, gx.reshape(x.shape)]
    for i in range(4):
        outs += [res[n][i].reshape(shapes[n]) for n in order]
    return tuple(outs)
```
